```python
import jax, jax.numpy as jnp
from jax import lax
import numpy as np

D_MODEL = 2048
BATCH = 8
SEQ = 2048
DEPTH = 2

CHUNK = 64
D_RWKV = D_MODEL // 2
D_HGRN = D_MODEL - D_RWKV
RWKV_HEAD = 64
RWKV_HEADS = D_RWKV // RWKV_HEAD
DECAY_RANK = max(32, int(round(1.8 * D_RWKV ** 0.5 / 32)) * 32)
A_RANK = max(32, int(round(1.8 * D_RWKV ** 0.5 / 32)) * 32)
VRES_RANK = max(32, int(round(1.3 * D_RWKV ** 0.5 / 32)) * 32)
HGRN_EXPAND = 128
HGRN_HEADS = D_HGRN // HGRN_EXPAND
HGRN_HEAD_V = D_HGRN // HGRN_HEADS
RWKV_COLS = 4 * D_RWKV + DECAY_RANK + A_RANK
HGRN_COLS = 4 * D_HGRN
IN_COLS = RWKV_COLS + HGRN_COLS
ALPHA = (2 * DEPTH) ** 0.25
BETA = (8 * DEPTH) ** -0.25
LN_EPS = 1e-5
GN_EPS = 64e-5
RMS_EPS = 1e-5
LB_FLOOR = 1e-30

kernel_name = "rwkv7_hgrn2_parallel_deepnorm"


def _heads(t, n):
    return t.reshape(t.shape[:-1] + (-1, n))


def _layer_norm(x, w, b):
    x = x.astype(jnp.float32)
    mu = jnp.mean(x, -1, keepdims=True)
    var = jnp.mean(jnp.square(x - mu), -1, keepdims=True)
    return (x - mu) * lax.rsqrt(var + LN_EPS) * w + b


def _token_shift(y, mu):
    y_prev = jnp.pad(y, ((0, 0), (1, 0), (0, 0)))[:, :-1]
    return y + mu * (y_prev - y)


def _rwkv7_scan(r, w, k, v, a_vec, b_vec):
    Bsz, T, H, N = r.shape

    def step(S, inp):
        r_t, w_t, k_t, v_t, a_t, b_t = inp
        sa = jnp.einsum('bhij,bhj->bhi', S, a_t)
        S = S * w_t[:, :, None, :] + sa[..., None] * b_t[:, :, None, :] + v_t[..., None] * k_t[:, :, None, :]
        return S, jnp.einsum('bhij,bhj->bhi', S, r_t)

    S0 = jnp.zeros((Bsz, H, N, N), jnp.float32)
    xs = tuple(jnp.moveaxis(t, 1, 0) for t in (r, w, k, v, a_vec, b_vec))
    _, o = lax.scan(step, S0, xs)
    return jnp.moveaxis(o, 0, 1)


def _rwkv7_branch(rw, v_first, w0, w_up, a0, a_up, k_k, k_a, r_k, gn_w, gn_b, v_mix):
    r, k, v, z, wd, ad = jnp.split(
        rw, [D_RWKV, 2 * D_RWKV, 3 * D_RWKV, 4 * D_RWKV, 4 * D_RWKV + DECAY_RANK], axis=-1)
    w_raw = w0 + jnp.tanh(wd) @ w_up
    decay = jnp.exp(-jnp.exp(-jax.nn.softplus(-w_raw) - 0.5))
    a = jax.nn.sigmoid(a0 + ad @ a_up)
    if v_mix is None:
        v_first = v
    else:
        v0, v_down, v_up = v_mix
        v = v + (v_first - v) * jax.nn.sigmoid(v0 + (v @ v_down) @ v_up)
    kk = _heads(k * k_k, RWKV_HEAD)
    kk = kk / jnp.maximum(jnp.sqrt(jnp.sum(kk * kk, -1, keepdims=True)), 1e-12)
    k = k * (1.0 + (a - 1.0) * k_a)
    rh, kh, vh = _heads(r, RWKV_HEAD), _heads(k, RWKV_HEAD), _heads(v, RWKV_HEAD)
    ah = _heads(a, RWKV_HEAD)
    o = _rwkv7_scan(rh, _heads(decay, RWKV_HEAD), kh, vh, -kk, kk * ah)
    mu = jnp.mean(o, -1, keepdims=True)
    var = jnp.mean(jnp.square(o - mu), -1, keepdims=True)
    o = (o - mu) * lax.rsqrt(var + GN_EPS) * _heads(gn_w, RWKV_HEAD) + _heads(gn_b, RWKV_HEAD)
    o = o + jnp.sum(rh * kh * _heads(r_k, RWKV_HEAD), -1, keepdims=True) * vh
    return o.reshape(rw.shape[:-1] + (D_RWKV,)) * jax.nn.silu(z), v_first


def _hgrn2_chunkwise(q, log_f, k, i):
    Bsz, T, H, DK = q.shape
    DV = i.shape[-1]
    NC = T // CHUNK

    def to_chunks(t):
        return jnp.moveaxis(t.reshape(Bsz, NC, CHUNK, H, t.shape[-1]), 1, 0)

    causal = jnp.tril(jnp.ones((CHUNK, CHUNK), bool))[None, :, :, None, None]

    def step(S, inp):
        q_c, lf_c, k_c, i_c = inp
        b = jnp.cumsum(lf_c, axis=1)
        diff = b[:, :, None] - b[:, None, :]
        decay = jnp.where(causal, jnp.exp(jnp.where(causal, diff, 0.0)), 0.0)
        att = jnp.einsum('btshd,bshd->btsh', q_c[:, :, None] * decay, k_c)
        o_intra = jnp.einsum('btsh,bshv->bthv', att, i_c)
        o_inter = jnp.einsum('bthd,bhdv->bthv', q_c * jnp.exp(b), S)
        b_last = b[:, -1]
        k_dec = k_c * jnp.exp(b_last[:, None] - b)
        S = S * jnp.exp(b_last)[..., None] + jnp.einsum('bshd,bshv->bhdv', k_dec, i_c)
        return S, o_intra + o_inter

    S0 = jnp.zeros((Bsz, H, DK, DV), jnp.float32)
    _, o = lax.scan(step, S0, (to_chunks(q), to_chunks(log_f), to_chunks(k), to_chunks(i)))
    return jnp.moveaxis(o, 0, 1).reshape(Bsz, T, H, DV)


def _hgrn2_branch(hg, lb, g_norm_w):
    q, f_raw, i_in, z = jnp.split(hg, 4, axis=-1)
    q = jax.nn.silu(q)
    log_lb = jnp.log(jnp.maximum(lb, LB_FLOOR))
    log_f = jnp.logaddexp(log_lb, jnp.log1p(-lb) + jax.nn.log_sigmoid(f_raw))
    k = (1.0 - lb) * jax.nn.sigmoid(-f_raw)
    o = _hgrn2_chunkwise(_heads(q, HGRN_EXPAND), _heads(log_f, HGRN_EXPAND),
                         _heads(k, HGRN_EXPAND), _heads(i_in, HGRN_HEAD_V))
    o = o * lax.rsqrt(jnp.mean(o * o, -1, keepdims=True) + RMS_EPS)
    return o.reshape(hg.shape[:-1] + (D_HGRN,)) * g_norm_w * jax.nn.silu(z)


def _fwd_setup_inputs(seed: int = 0) -> dict:
    key = jax.random.key(seed)
    ks = jax.random.split(key, 24)
    L, L1 = DEPTH, DEPTH - 1

    def nrm(k, shape, s):
        return s * jax.random.normal(k, shape, jnp.float32)

    x = nrm(ks[0], (BATCH, SEQ, D_MODEL), 1.0)
    col_scale = jnp.concatenate([
        jnp.ones((2 * D_RWKV,), jnp.float32), jnp.full((D_RWKV,), BETA, jnp.float32),
        jnp.ones((D_RWKV + DECAY_RANK + A_RANK,), jnp.float32),
        jnp.ones((2 * D_HGRN,), jnp.float32), jnp.full((D_HGRN,), BETA, jnp.float32),
        jnp.ones((D_HGRN,), jnp.float32)])
    w_in = nrm(ks[1], (L, D_MODEL, IN_COLS), D_MODEL ** -0.5) * col_scale
    shift_mu = jax.random.uniform(ks[2], (L, RWKV_COLS), jnp.float32)
    ramp = (jnp.arange(D_RWKV, dtype=jnp.float32) / (D_RWKV - 1)) ** 0.85
    w_decay0 = -6.0 + 5.0 * ramp + nrm(ks[3], (L, D_RWKV), 0.1)
    w_decay_up = nrm(ks[4], (L, DECAY_RANK, D_RWKV), 0.3 * DECAY_RANK ** -0.5)
    a0 = nrm(ks[5], (L, D_RWKV), 0.1)
    a_up = nrm(ks[6], (L, A_RANK, D_RWKV), 0.3 * A_RANK ** -0.5)
    k_k = 0.85 + nrm(ks[7], (L, D_RWKV), 0.05)
    k_a = 1.0 + nrm(ks[8], (L, D_RWKV), 0.05)
    r_k = nrm(ks[9], (L, D_RWKV), 0.1)
    ln_x_w = 1.0 + nrm(ks[10], (L, D_RWKV), 0.05)
    ln_x_b = nrm(ks[11], (L, D_RWKV), 0.02)
    v_mix0 = 1.0 + nrm(ks[12], (L1, D_RWKV), 0.1)
    v_mix_down = nrm(ks[13], (L1, D_RWKV, VRES_RANK), D_RWKV ** -0.5)
    v_mix_up = nrm(ks[14], (L1, VRES_RANK, D_RWKV), 0.3 * VRES_RANK ** -0.5)
    lb_logits = nrm(ks[15], (L, D_HGRN), 0.5)
    g_norm_w = 1.0 + nrm(ks[16], (L, D_HGRN), 0.05)
    w_out = nrm(ks[17], (L, D_MODEL, D_MODEL), BETA * D_MODEL ** -0.5)
    ln_w = 1.0 + nrm(ks[18], (L, D_MODEL), 0.05)
    ln_b = nrm(ks[19], (L, D_MODEL), 0.02)
    return {"x": x, "w_in": w_in, "shift_mu": shift_mu, "w_decay0": w_decay0,
            "w_decay_up": w_decay_up, "a0": a0, "a_up": a_up, "k_k": k_k, "k_a": k_a,
            "r_k": r_k, "ln_x_w": ln_x_w, "ln_x_b": ln_x_b, "v_mix0": v_mix0,
            "v_mix_down": v_mix_down, "v_mix_up": v_mix_up, "lb_logits": lb_logits,
            "g_norm_w": g_norm_w, "w_out": w_out, "ln_w": ln_w, "ln_b": ln_b}


def _fwd_reference(x, w_in, shift_mu, w_decay0, w_decay_up, a0, a_up, k_k, k_a, r_k, ln_x_w, ln_x_b,
              v_mix0, v_mix_down, v_mix_up, lb_logits, g_norm_w, w_out, ln_w, ln_b):
    out_dtype = x.dtype
    lb_sm = jax.nn.softmax(lb_logits.astype(jnp.float32), axis=0)
    lower_bounds = jnp.cumsum(lb_sm, axis=0) - lb_sm[0]
    h = x.astype(jnp.float32)
    v_first = None
    for l in range(DEPTH):
        proj = jnp.einsum('btd,dc->btc', h, w_in[l].astype(jnp.float32))
        rw = _token_shift(proj[..., :RWKV_COLS], shift_mu[l])
        hg = proj[..., RWKV_COLS:]
        v_mix = None if l == 0 else (v_mix0[l - 1], v_mix_down[l - 1], v_mix_up[l - 1])
        o_rwkv, v_first = _rwkv7_branch(rw, v_first, w_decay0[l], w_decay_up[l], a0[l], a_up[l],
                                        k_k[l], k_a[l], r_k[l], ln_x_w[l], ln_x_b[l], v_mix)
        o_hgrn = _hgrn2_branch(hg, lower_bounds[l], g_norm_w[l])
        y = jnp.einsum('btc,cd->btd', jnp.concatenate([o_rwkv, o_hgrn], axis=-1), w_out[l])
        h = _layer_norm(ALPHA * h + y, ln_w[l], ln_b[l])
    return h.astype(out_dtype)


import jax as _jax
import jax.numpy as _jnp

TWIN_FORMAT = 'train_step'
FWD_PARAMS = ['x', 'w_in', 'shift_mu', 'w_decay0', 'w_decay_up', 'a0', 'a_up', 'k_k', 'k_a', 'r_k', 'ln_x_w', 'ln_x_b', 'v_mix0', 'v_mix_down', 'v_mix_up', 'lb_logits', 'g_norm_w', 'w_out', 'ln_w', 'ln_b']
TWIN_WEIGHTS = ['w_in', 'shift_mu', 'w_decay0', 'w_decay_up', 'a0', 'a_up', 'k_k', 'k_a', 'r_k', 'ln_x_w', 'ln_x_b', 'v_mix0', 'v_mix_down', 'v_mix_up', 'lb_logits', 'g_norm_w', 'w_out', 'ln_w', 'ln_b']
TWIN_DIFF_INPUT = 'x'
TWIN_INPUTS = ['x', 'w_in', 'shift_mu', 'w_decay0', 'w_decay_up', 'a0', 'a_up', 'k_k', 'k_a', 'r_k', 'ln_x_w', 'ln_x_b', 'v_mix0', 'v_mix_down', 'v_mix_up', 'lb_logits', 'g_norm_w', 'w_out', 'ln_w', 'ln_b', 'loss_target', 'm_w_in', 'm_shift_mu', 'm_w_decay0', 'm_w_decay_up', 'm_a0', 'm_a_up', 'm_k_k', 'm_k_a', 'm_r_k', 'm_ln_x_w', 'm_ln_x_b', 'm_v_mix0', 'm_v_mix_down', 'm_v_mix_up', 'm_lb_logits', 'm_g_norm_w', 'm_w_out', 'm_ln_w', 'm_ln_b', 'v_w_in', 'v_shift_mu', 'v_w_decay0', 'v_w_decay_up', 'v_a0', 'v_a_up', 'v_k_k', 'v_k_a', 'v_r_k', 'v_ln_x_w', 'v_ln_x_b', 'v_v_mix0', 'v_v_mix_down', 'v_v_mix_up', 'v_lb_logits', 'v_g_norm_w', 'v_w_out', 'v_ln_w', 'v_ln_b']
TWIN_OUTPUTS = ['loss', 'grad_x', 'grad_w_in', 'grad_shift_mu', 'grad_w_decay0', 'grad_w_decay_up', 'grad_a0', 'grad_a_up', 'grad_k_k', 'grad_k_a', 'grad_r_k', 'grad_ln_x_w', 'grad_ln_x_b', 'grad_v_mix0', 'grad_v_mix_down', 'grad_v_mix_up', 'grad_lb_logits', 'grad_g_norm_w', 'grad_w_out', 'grad_ln_w', 'grad_ln_b', 'delta_w_in', 'delta_shift_mu', 'delta_w_decay0', 'delta_w_decay_up', 'delta_a0', 'delta_a_up', 'delta_k_k', 'delta_k_a', 'delta_r_k', 'delta_ln_x_w', 'delta_ln_x_b', 'delta_v_mix0', 'delta_v_mix_down', 'delta_v_mix_up', 'delta_lb_logits', 'delta_g_norm_w', 'delta_w_out', 'delta_ln_w', 'delta_ln_b', 'new_m_w_in', 'new_m_shift_mu', 'new_m_w_decay0', 'new_m_w_decay_up', 'new_m_a0', 'new_m_a_up', 'new_m_k_k', 'new_m_k_a', 'new_m_r_k', 'new_m_ln_x_w', 'new_m_ln_x_b', 'new_m_v_mix0', 'new_m_v_mix_down', 'new_m_v_mix_up', 'new_m_lb_logits', 'new_m_g_norm_w', 'new_m_w_out', 'new_m_ln_w', 'new_m_ln_b', 'new_v_w_in', 'new_v_shift_mu', 'new_v_w_decay0', 'new_v_w_decay_up', 'new_v_a0', 'new_v_a_up', 'new_v_k_k', 'new_v_k_a', 'new_v_r_k', 'new_v_ln_x_w', 'new_v_ln_x_b', 'new_v_v_mix0', 'new_v_v_mix_down', 'new_v_v_mix_up', 'new_v_lb_logits', 'new_v_g_norm_w', 'new_v_w_out', 'new_v_ln_w', 'new_v_ln_b']
TWIN_LEAF_KINDS = {'loss': 'loss', 'grad_x': 'grad_x', 'grad_w_in': 'grad_w', 'grad_shift_mu': 'grad_w', 'grad_w_decay0': 'grad_w', 'grad_w_decay_up': 'grad_w', 'grad_a0': 'grad_w', 'grad_a_up': 'grad_w', 'grad_k_k': 'grad_w', 'grad_k_a': 'grad_w', 'grad_r_k': 'grad_w', 'grad_ln_x_w': 'grad_w', 'grad_ln_x_b': 'grad_w', 'grad_v_mix0': 'grad_w', 'grad_v_mix_down': 'grad_w', 'grad_v_mix_up': 'grad_w', 'grad_lb_logits': 'grad_w', 'grad_g_norm_w': 'grad_w', 'grad_w_out': 'grad_w', 'grad_ln_w': 'grad_w', 'grad_ln_b': 'grad_w', 'delta_w_in': 'delta_w', 'delta_shift_mu': 'delta_w', 'delta_w_decay0': 'delta_w', 'delta_w_decay_up': 'delta_w', 'delta_a0': 'delta_w', 'delta_a_up': 'delta_w', 'delta_k_k': 'delta_w', 'delta_k_a': 'delta_w', 'delta_r_k': 'delta_w', 'delta_ln_x_w': 'delta_w', 'delta_ln_x_b': 'delta_w', 'delta_v_mix0': 'delta_w', 'delta_v_mix_down': 'delta_w', 'delta_v_mix_up': 'delta_w', 'delta_lb_logits': 'delta_w', 'delta_g_norm_w': 'delta_w', 'delta_w_out': 'delta_w', 'delta_ln_w': 'delta_w', 'delta_ln_b': 'delta_w', 'new_m_w_in': 'new_m', 'new_m_shift_mu': 'new_m', 'new_m_w_decay0': 'new_m', 'new_m_w_decay_up': 'new_m', 'new_m_a0': 'new_m', 'new_m_a_up': 'new_m', 'new_m_k_k': 'new_m', 'new_m_k_a': 'new_m', 'new_m_r_k': 'new_m', 'new_m_ln_x_w': 'new_m', 'new_m_ln_x_b': 'new_m', 'new_m_v_mix0': 'new_m', 'new_m_v_mix_down': 'new_m', 'new_m_v_mix_up': 'new_m', 'new_m_lb_logits': 'new_m', 'new_m_g_norm_w': 'new_m', 'new_m_w_out': 'new_m', 'new_m_ln_w': 'new_m', 'new_m_ln_b': 'new_m', 'new_v_w_in': 'new_v', 'new_v_shift_mu': 'new_v', 'new_v_w_decay0': 'new_v', 'new_v_w_decay_up': 'new_v', 'new_v_a0': 'new_v', 'new_v_a_up': 'new_v', 'new_v_k_k': 'new_v', 'new_v_k_a': 'new_v', 'new_v_r_k': 'new_v', 'new_v_ln_x_w': 'new_v', 'new_v_ln_x_b': 'new_v', 'new_v_v_mix0': 'new_v', 'new_v_v_mix_down': 'new_v', 'new_v_v_mix_up': 'new_v', 'new_v_lb_logits': 'new_v', 'new_v_g_norm_w': 'new_v', 'new_v_w_out': 'new_v', 'new_v_ln_w': 'new_v', 'new_v_ln_b': 'new_v'}


def _forward(args):
    return _fwd_reference(*[args[k] for k in FWD_PARAMS])


def _output_shape():
    out = _jax.eval_shape(lambda: _forward(_fwd_setup_inputs(0)))
    return out.shape, out.dtype

N_MICROBATCH = 1
ADAM_LR = 0.001
ADAM_B1 = 0.9
ADAM_B2 = 0.999
ADAM_EPS = 1e-08
ADAM_WD = 0.01
ADAM_STEP = 10
PER_EXAMPLE_BATCH_AXIS = {'x': 0, 'loss_target': 0}
SHARED_INPUTS = []
_WEIGHT_DTYPES = {'w_in': _jnp.float32, 'shift_mu': _jnp.float32, 'w_decay0': _jnp.float32, 'w_decay_up': _jnp.float32, 'a0': _jnp.float32, 'a_up': _jnp.float32, 'k_k': _jnp.float32, 'k_a': _jnp.float32, 'r_k': _jnp.float32, 'ln_x_w': _jnp.float32, 'ln_x_b': _jnp.float32, 'v_mix0': _jnp.float32, 'v_mix_down': _jnp.float32, 'v_mix_up': _jnp.float32, 'lb_logits': _jnp.float32, 'g_norm_w': _jnp.float32, 'w_out': _jnp.float32, 'ln_w': _jnp.float32, 'ln_b': _jnp.float32}
MOMENT_SCALE = {'w_in': 1.475312e-02, 'shift_mu': 1.838474e-02, 'w_decay0': 5.736936e-03, 'w_decay_up': 8.765231e-04, 'a0': 5.021950e-03, 'a_up': 4.485893e-03, 'k_k': 9.149597e-03, 'k_a': 1.165315e-02, 'r_k': 1.072218e-02, 'ln_x_w': 1.063867e-02, 'ln_x_b': 1.162474e-02, 'v_mix0': 3.707998e-03, 'v_mix_down': 2.516969e-03, 'v_mix_up': 1.490168e-03, 'lb_logits': 1.216361e-03, 'g_norm_w': 1.354107e-02, 'w_out': 2.398358e-02, 'ln_w': 5.716732e+00, 'ln_b': 1.716845e-01}


def _to_microbatches(a, axis):
    t = _jnp.moveaxis(a, axis, 0)
    t = t.reshape((N_MICROBATCH, t.shape[0] // N_MICROBATCH) + t.shape[1:])
    return _jnp.moveaxis(t, 1, axis + 1)


def setup_inputs(seed: int = 0) -> dict:
    inp = _fwd_setup_inputs(seed)
    key = _jax.random.fold_in(_jax.random.key(seed), 7919)
    shape, _ = _output_shape()
    out = dict(inp)
    out["loss_target"] = _jax.random.normal(_jax.random.fold_in(key, 0), shape, _jnp.float32)
    for i, name in enumerate(TWIN_WEIGHTS):
        w = inp[name].astype(_jnp.float32)
        if MOMENT_SCALE is None:
            s = _jnp.sqrt(_jnp.mean(_jnp.square(w)) + 1e-30)
        else:
            s = MOMENT_SCALE[name]
        km, kv = _jax.random.split(_jax.random.fold_in(key, i + 1))
        out[name] = w
        out["m_" + name] = s * _jax.random.normal(km, w.shape, _jnp.float32)
        out["v_" + name] = (s * s) * _jax.random.uniform(kv, w.shape, _jnp.float32, 0.5, 1.5)
    if N_MICROBATCH > 1:
        for name, axis in PER_EXAMPLE_BATCH_AXIS.items():
            out[name] = _to_microbatches(out[name], axis)
    return {'x': out['x'], 'w_in': out['w_in'], 'shift_mu': out['shift_mu'], 'w_decay0': out['w_decay0'], 'w_decay_up': out['w_decay_up'], 'a0': out['a0'], 'a_up': out['a_up'], 'k_k': out['k_k'], 'k_a': out['k_a'], 'r_k': out['r_k'], 'ln_x_w': out['ln_x_w'], 'ln_x_b': out['ln_x_b'], 'v_mix0': out['v_mix0'], 'v_mix_down': out['v_mix_down'], 'v_mix_up': out['v_mix_up'], 'lb_logits': out['lb_logits'], 'g_norm_w': out['g_norm_w'], 'w_out': out['w_out'], 'ln_w': out['ln_w'], 'ln_b': out['ln_b'], 'loss_target': out['loss_target'], 'm_w_in': out['m_w_in'], 'm_shift_mu': out['m_shift_mu'], 'm_w_decay0': out['m_w_decay0'], 'm_w_decay_up': out['m_w_decay_up'], 'm_a0': out['m_a0'], 'm_a_up': out['m_a_up'], 'm_k_k': out['m_k_k'], 'm_k_a': out['m_k_a'], 'm_r_k': out['m_r_k'], 'm_ln_x_w': out['m_ln_x_w'], 'm_ln_x_b': out['m_ln_x_b'], 'm_v_mix0': out['m_v_mix0'], 'm_v_mix_down': out['m_v_mix_down'], 'm_v_mix_up': out['m_v_mix_up'], 'm_lb_logits': out['m_lb_logits'], 'm_g_norm_w': out['m_g_norm_w'], 'm_w_out': out['m_w_out'], 'm_ln_w': out['m_ln_w'], 'm_ln_b': out['m_ln_b'], 'v_w_in': out['v_w_in'], 'v_shift_mu': out['v_shift_mu'], 'v_w_decay0': out['v_w_decay0'], 'v_w_decay_up': out['v_w_decay_up'], 'v_a0': out['v_a0'], 'v_a_up': out['v_a_up'], 'v_k_k': out['v_k_k'], 'v_k_a': out['v_k_a'], 'v_r_k': out['v_r_k'], 'v_ln_x_w': out['v_ln_x_w'], 'v_ln_x_b': out['v_ln_x_b'], 'v_v_mix0': out['v_v_mix0'], 'v_v_mix_down': out['v_v_mix_down'], 'v_v_mix_up': out['v_v_mix_up'], 'v_lb_logits': out['v_lb_logits'], 'v_g_norm_w': out['v_g_norm_w'], 'v_w_out': out['v_w_out'], 'v_ln_w': out['v_ln_w'], 'v_ln_b': out['v_ln_b']}


def _loss(weights, diff, rest, loss_target):
    with _jax.named_scope("forward"):
        args = {**rest, TWIN_DIFF_INPUT: diff, **{k: w.astype(_WEIGHT_DTYPES[k]) for k, w in weights.items()}}
        y = _forward(args)
    with _jax.named_scope("loss_head"):
        err = _jnp.square(y.astype(_jnp.float32) - loss_target)
        return 0.5 * _jnp.sum(_jnp.mean(err, axis=-1)) if err.ndim else 0.5 * err


def _adamw(w, g, m, v):
    m = ADAM_B1 * m + (1.0 - ADAM_B1) * g
    v = ADAM_B2 * v + (1.0 - ADAM_B2) * _jnp.square(g)
    m_hat = m / (1.0 - ADAM_B1 ** ADAM_STEP)
    v_hat = v / (1.0 - ADAM_B2 ** ADAM_STEP)
    delta = -ADAM_LR * (m_hat / (_jnp.sqrt(v_hat) + ADAM_EPS) + ADAM_WD * w)
    return delta, m, v


def reference(x, w_in, shift_mu, w_decay0, w_decay_up, a0, a_up, k_k, k_a, r_k, ln_x_w, ln_x_b, v_mix0, v_mix_down, v_mix_up, lb_logits, g_norm_w, w_out, ln_w, ln_b, loss_target, m_w_in, m_shift_mu, m_w_decay0, m_w_decay_up, m_a0, m_a_up, m_k_k, m_k_a, m_r_k, m_ln_x_w, m_ln_x_b, m_v_mix0, m_v_mix_down, m_v_mix_up, m_lb_logits, m_g_norm_w, m_w_out, m_ln_w, m_ln_b, v_w_in, v_shift_mu, v_w_decay0, v_w_decay_up, v_a0, v_a_up, v_k_k, v_k_a, v_r_k, v_ln_x_w, v_ln_x_b, v_v_mix0, v_v_mix_down, v_v_mix_up, v_lb_logits, v_g_norm_w, v_w_out, v_ln_w, v_ln_b):
    given = dict(x=x, w_in=w_in, shift_mu=shift_mu, w_decay0=w_decay0, w_decay_up=w_decay_up, a0=a0, a_up=a_up, k_k=k_k, k_a=k_a, r_k=r_k, ln_x_w=ln_x_w, ln_x_b=ln_x_b, v_mix0=v_mix0, v_mix_down=v_mix_down, v_mix_up=v_mix_up, lb_logits=lb_logits, g_norm_w=g_norm_w, w_out=w_out, ln_w=ln_w, ln_b=ln_b, loss_target=loss_target, m_w_in=m_w_in, m_shift_mu=m_shift_mu, m_w_decay0=m_w_decay0, m_w_decay_up=m_w_decay_up, m_a0=m_a0, m_a_up=m_a_up, m_k_k=m_k_k, m_k_a=m_k_a, m_r_k=m_r_k, m_ln_x_w=m_ln_x_w, m_ln_x_b=m_ln_x_b, m_v_mix0=m_v_mix0, m_v_mix_down=m_v_mix_down, m_v_mix_up=m_v_mix_up, m_lb_logits=m_lb_logits, m_g_norm_w=m_g_norm_w, m_w_out=m_w_out, m_ln_w=m_ln_w, m_ln_b=m_ln_b, v_w_in=v_w_in, v_shift_mu=v_shift_mu, v_w_decay0=v_w_decay0, v_w_decay_up=v_w_decay_up, v_a0=v_a0, v_a_up=v_a_up, v_k_k=v_k_k, v_k_a=v_k_a, v_r_k=v_r_k, v_ln_x_w=v_ln_x_w, v_ln_x_b=v_ln_x_b, v_v_mix0=v_v_mix0, v_v_mix_down=v_v_mix_down, v_v_mix_up=v_v_mix_up, v_lb_logits=v_lb_logits, v_g_norm_w=v_g_norm_w, v_w_out=v_w_out, v_ln_w=v_ln_w, v_ln_b=v_ln_b)
    weights = {n: given[n] for n in TWIN_WEIGHTS}
    shared = {n: given[n] for n in SHARED_INPUTS}
    per_example = {n: given[n] for n in ['x']}
    grad_fn = _jax.value_and_grad(_loss, argnums=(0, 1))

    def one_microbatch(ex, loss_target):
        ex = dict(ex)
        diff = ex.pop(TWIN_DIFF_INPUT)
        return grad_fn(weights, diff, {**shared, **ex}, loss_target)

    if N_MICROBATCH == 1:
        loss, (grad_w, grad_x) = one_microbatch(per_example, given["loss_target"])
    else:
        def body(carry, xs):
            loss_sum, grad_sum = carry
            l_k, (gw_k, gx_k) = one_microbatch(xs[0], xs[1])
            with _jax.named_scope("update"):
                return (loss_sum + l_k, _jax.tree.map(_jnp.add, grad_sum, gw_k)), gx_k

        init = (_jnp.zeros((), _jnp.float32), _jax.tree.map(_jnp.zeros_like, weights))
        (loss, grad_w), grad_x = _jax.lax.scan(body, init, (per_example, given["loss_target"]))
    with _jax.named_scope("update"):
        delta_w, new_m, new_v = {}, {}, {}
        for n in TWIN_WEIGHTS:
            delta_w[n], new_m[n], new_v[n] = _adamw(weights[n], grad_w[n], given["m_" + n], given["v_" + n])
    return (loss, grad_x, *[grad_w[n] for n in TWIN_WEIGHTS], *[delta_w[n] for n in TWIN_WEIGHTS],
            *[new_m[n] for n in TWIN_WEIGHTS], *[new_v[n] for n in TWIN_WEIGHTS])
```

```python
import functools
import math

import jax
import jax.numpy as jnp
from jax import lax
from jax.experimental import pallas as pl
from jax.experimental.pallas import tpu as pltpu

f32 = jnp.float32
bf16 = jnp.bfloat16
SDS = jax.ShapeDtypeStruct
MESH = pl.DeviceIdType.MESH

D_MODEL = 2048
DEPTH = 2
D_RWKV = 1024
D_HGRN = 1024
RWKV_HEAD = 64
RWKV_HEADS = 16
LORA = 64
VRES = 32
HG_HEAD = 128
HG_HEADS = 8
CHUNK = 64
RWKV_COLS = 4 * D_RWKV + 2 * LORA
IN_COLS = RWKV_COLS + 4 * D_HGRN
SHARD_COLS = IN_COLS // 4
LANES = 128
RW_BLKS = RWKV_COLS // LANES
ALPHA = (2 * DEPTH) ** 0.25
LN_EPS = 1e-5
GN_EPS = 64e-5
RMS_EPS = 1e-5
LB_FLOOR = 1e-30
EXP_M05 = math.exp(-0.5)
ADAM_LR, ADAM_B1, ADAM_B2, ADAM_EPS, ADAM_WD, ADAM_STEP = 0.001, 0.9, 0.999, 1e-08, 0.01, 10

TB = 128
SCAN_TB = 8
HSUB = 32
HIGHEST = lax.Precision.HIGHEST


def _params(sem=None, vmem_mb=None):
    kw = {}
    if sem is not None:
        kw["dimension_semantics"] = sem
    if vmem_mb is not None:
        kw["vmem_limit_bytes"] = vmem_mb << 20
    return pltpu.CompilerParams(**kw)


def _split3(x):
    hi = x.astype(bf16)
    r1 = x - hi.astype(f32)
    mid = r1.astype(bf16)
    lo = (r1 - mid.astype(f32)).astype(bf16)
    return hi, mid, lo


def _dot3(x, m):
    return sum(jnp.dot(p, m, preferred_element_type=f32) for p in _split3(x))


def _seg_raw(x, ee):
    parts = _split3(x)
    outs = []
    for c in range(4):
        sl = slice(256 * c, 256 * c + 256)
        outs.append(sum(jnp.dot(p[:, sl], ee, preferred_element_type=f32) for p in parts))
    return jnp.concatenate(outs, axis=1)


@jax.custom_vjp
def _seg(x, ee):
    return _seg_raw(x, ee)


def _seg_fwd(x, ee):
    return _seg_raw(x, ee), ee


def _seg_bwd(ee, g):
    return _seg_raw(g, ee), jnp.zeros_like(ee)


_seg.defvjp(_seg_fwd, _seg_bwd)


def _const_mats():
    i256 = jnp.arange(256) // RWKV_HEAD
    ee = (i256[:, None] == i256[None, :]).astype(bf16)
    head = jnp.arange(D_RWKV) // RWKV_HEAD
    e16 = (head[:, None] == jnp.arange(LANES)[None, :]).astype(bf16)
    return ee, e16, e16.T


def _matmul(a, b, *, mode, tm, tn, tk, name, add=None, b_layer=None):
    bs = b.shape if b_layer is None else b.shape[1:]
    lead = () if b_layer is None else (None,)
    bidx = (lambda *t: t) if b_layer is None else (lambda *t: (b_layer,) + t)
    if mode == "nn":
        (M, K), N = a.shape, bs[1]
        tm, tn, tk = min(tm, M), min(tn, N), min(tk, K)
        a_spec = pl.BlockSpec((tm, tk), lambda i, j, k: (i, k))
        b_spec = pl.BlockSpec(lead + (tk, tn), lambda i, j, k: bidx(k, j))
        dims = (((1,), (0,)), ((), ()))
    elif mode == "nt":
        (M, K), N = a.shape, bs[0]
        tm, tn, tk = min(tm, M), min(tn, N), min(tk, K)
        a_spec = pl.BlockSpec((tm, tk), lambda i, j, k: (i, k))
        b_spec = pl.BlockSpec(lead + (tn, tk), lambda i, j, k: bidx(j, k))
        dims = (((1,), (1,)), ((), ()))
    else:
        (K, M), N = a.shape, bs[1]
        tm, tn, tk = min(tm, M), min(tn, N), min(tk, K)
        a_spec = pl.BlockSpec((tk, tm), lambda i, j, k: (k, i))
        b_spec = pl.BlockSpec(lead + (tk, tn), lambda i, j, k: bidx(k, j))
        dims = (((0,), (0,)), ((), ()))
    assert M % tm == 0 and N % tn == 0 and K % tk == 0, (M, N, K, tm, tn, tk)
    has_add = add is not None

    def body(*refs):
        if has_add:
            a_ref, b_ref, c_ref, o_ref = refs
        else:
            a_ref, b_ref, o_ref = refs
        k = pl.program_id(2)
        p = lax.dot_general(a_ref[...].astype(bf16), b_ref[...].astype(bf16), dims, preferred_element_type=f32)

        @pl.when(k == 0)
        def _():
            o_ref[...] = p + c_ref[...] if has_add else p

        @pl.when(k > 0)
        def _():
            o_ref[...] += p

    o_spec = pl.BlockSpec((tm, tn), lambda i, j, k: (i, j))
    in_specs = [a_spec, b_spec] + ([o_spec] if has_add else [])
    args = (a, b) + ((add,) if has_add else ())
    return pl.pallas_call(
        body, name=name, grid=(M // tm, N // tn, K // tk), in_specs=in_specs, out_specs=o_spec,
        out_shape=SDS((M, N), f32),
        compiler_params=_params(("parallel", "parallel", "arbitrary"), 56),
    )(*args)


def _rwkv_core(k_in, v_in, wdad, w0, wup, a0, aup, kkw, kaw, ee, vmix):
    w_raw = w0 + jnp.dot(jnp.tanh(wdad), wup, preferred_element_type=f32)
    decay = jnp.exp(-EXP_M05 * jax.nn.sigmoid(w_raw))
    a = jax.nn.sigmoid(a0 + jnp.dot(wdad, aup, preferred_element_type=f32))
    if vmix is None:
        v = v_in
    else:
        v0, vdn, vup, vfirst = vmix
        gate = jax.nn.sigmoid(v0 + jnp.dot(jnp.dot(v_in, vdn, preferred_element_type=f32), vup,
                                           preferred_element_type=f32))
        v = v_in + (vfirst - v_in) * gate
    kk = k_in * kkw
    n2 = _seg(kk * kk, ee)
    kk = kk / jnp.maximum(jnp.sqrt(n2), 1e-12)
    k2 = k_in * (1.0 + (a - 1.0) * kaw)
    return decay, k2, v, -kk, kk * a


def _shifted(y, prev_row):
    yp = pltpu.roll(y, 1, 0)
    row = lax.broadcasted_iota(jnp.int32, y.shape, 0)
    return jnp.where(row == 0, prev_row, yp)


def _rwkv_pre_fwd(proj, mu, w0, wup, a0, aup, kkw, kaw, ee, vmix, layer):
    T = proj.shape[0]
    nb = T // TB
    has_mix = vmix is not None

    def body(*refs):
        y_ref, p8_ref, mu_ref, w0_ref, wup_ref, a0_ref, aup_ref, kkw_ref, kaw_ref, ee_ref = refs[:10]
        rest = refs[10:]
        if has_mix:
            v0_ref, vdn_ref, vup_ref, vf_ref = rest[:4]
            rest = rest[4:]
        r_o, w_o, k_o, v_o, a_o, b_o, z_o = rest
        i = pl.program_id(0)
        y = y_ref[...]
        prev = jnp.where(i == 0, 0.0, p8_ref[7:8, :])
        rw = y + mu_ref[...] * (_shifted(y, prev) - y)
        mix = (v0_ref[...], vdn_ref[...], vup_ref[...], vf_ref[...]) if has_mix else None
        dec, k2, v, av, bv = _rwkv_core(rw[:, 1024:2048], rw[:, 2048:3072], rw[:, 4096:4224], w0_ref[...],
                                        wup_ref[...], a0_ref[...], aup_ref[...], kkw_ref[...], kaw_ref[...],
                                        ee_ref[...], mix)
        r_o[...] = rw[:, 0:1024]
        w_o[...] = dec
        k_o[...] = k2
        v_o[...] = v
        a_o[...] = av
        b_o[...] = bv
        z_o[...] = rw[:, 3072:4096]

    row = lambda n: pl.BlockSpec((1, n), lambda i: (0, 0))
    full = lambda s: pl.BlockSpec(s, lambda i: (0,) * len(s))
    blk = pl.BlockSpec((TB, D_RWKV), lambda i: (i, 0))
    in_specs = [pl.BlockSpec((TB, RWKV_COLS), lambda i: (i, 0)),
                pl.BlockSpec((8, RWKV_COLS), lambda i: (jnp.maximum(i * (TB // 8) - 1, 0), 0)),
                row(RWKV_COLS), row(D_RWKV), full((LANES, D_RWKV)), row(D_RWKV), full((LANES, D_RWKV)),
                row(D_RWKV), row(D_RWKV), full((256, 256))]
    args = [proj, proj, mu, w0, wup, a0, aup, kkw, kaw, ee]
    if has_mix:
        v0, vdn, vup, vfirst = vmix
        in_specs += [row(D_RWKV), full((D_RWKV, LANES)), full((LANES, D_RWKV)), blk]
        args += [v0, vdn, vup, vfirst]
    return pl.pallas_call(
        body, name=f"rwkv_pre_fwd_l{layer}", grid=(nb,), in_specs=in_specs, out_specs=[blk] * 7,
        out_shape=[SDS((T, D_RWKV), f32)] * 7, compiler_params=_params(("parallel",), 48),
    )(*args)


def _rwkv_pre_bwd(proj, mu, w0, wup, a0, aup, kkw, kaw, ee, vmix, cots, layer):
    T = proj.shape[0]
    nb = T // TB
    has_mix = vmix is not None
    cot_list = cots["r"] + [cots["w"]] + cots["k"] + cots["v"] + [cots["a"], cots["b"], cots["z"]]
    n_r, n_k, n_v = len(cots["r"]), len(cots["k"]), len(cots["v"])
    n_cot = len(cot_list)

    def body(*refs):
        y_ref, p8_ref, mu_ref, w0_ref, wup_ref, a0_ref, aup_ref, kkw_ref, kaw_ref, ee_ref = refs[:10]
        rest = refs[10:]
        if has_mix:
            v0_ref, vdn_ref, vup_ref, vf_ref = rest[:4]
            rest = rest[4:]
        cot_refs, rest = rest[:n_cot], rest[n_cot:]
        if has_mix:
            (dproj_o, dmu_o, dw0_o, dwup_o, da0_o, daup_o, dkkw_o, dkaw_o,
             dv0_o, dvdn_o, dvup_o, dvf_o, carry) = rest
        else:
            dproj_o, dmu_o, dw0_o, dwup_o, da0_o, daup_o, dkkw_o, dkaw_o, carry = rest
        i = pl.program_id(0)
        blk_i = nb - 1 - i
        y = y_ref[...]
        prev = jnp.where(blk_i == 0, 0.0, p8_ref[7:8, :])
        yp = _shifted(y, prev)
        mu_v = mu_ref[...]
        rw = y + mu_v * (yp - y)

        def sum_refs(rs):
            acc = rs[0][...]
            for r_ in rs[1:]:
                acc = acc + r_[...]
            return acc

        c = list(cot_refs)
        g_r = sum_refs(c[:n_r]); c = c[n_r:]
        g_w = c[0][...]; c = c[1:]
        g_k = sum_refs(c[:n_k]); c = c[n_k:]
        g_v = sum_refs(c[:n_v]); c = c[n_v:]
        g_a, g_b, g_z = c[0][...], c[1][...], c[2][...]

        ee_v = ee_ref[...]
        if has_mix:
            def fn(k_in, v_in, wdad, w0_, wup_, a0_, aup_, kkw_, kaw_, v0_, vdn_, vup_, vf_):
                return _rwkv_core(k_in, v_in, wdad, w0_, wup_, a0_, aup_, kkw_, kaw_, ee_v, (v0_, vdn_, vup_, vf_))
            prim = (rw[:, 1024:2048], rw[:, 2048:3072], rw[:, 4096:4224], w0_ref[...], wup_ref[...], a0_ref[...],
                    aup_ref[...], kkw_ref[...], kaw_ref[...], v0_ref[...], vdn_ref[...], vup_ref[...], vf_ref[...])
        else:
            def fn(k_in, v_in, wdad, w0_, wup_, a0_, aup_, kkw_, kaw_):
                return _rwkv_core(k_in, v_in, wdad, w0_, wup_, a0_, aup_, kkw_, kaw_, ee_v, None)
            prim = (rw[:, 1024:2048], rw[:, 2048:3072], rw[:, 4096:4224], w0_ref[...], wup_ref[...], a0_ref[...],
                    aup_ref[...], kkw_ref[...], kaw_ref[...])
        _, vjp = jax.vjp(fn, *prim)
        gs = vjp((g_w, g_k, g_v, g_a, g_b))
        d_k, d_v, d_wdad = gs[0], gs[1], gs[2]
        g_rw = jnp.concatenate([g_r, d_k, d_v, g_z, d_wdad], axis=1)

        @pl.when(i == 0)
        def _():
            carry[...] = jnp.zeros_like(carry)
            dmu_o[...] = jnp.zeros_like(dmu_o)
            dw0_o[...] = jnp.zeros_like(dw0_o)
            dwup_o[...] = jnp.zeros_like(dwup_o)
            da0_o[...] = jnp.zeros_like(da0_o)
            daup_o[...] = jnp.zeros_like(daup_o)
            dkkw_o[...] = jnp.zeros_like(dkkw_o)
            dkaw_o[...] = jnp.zeros_like(dkaw_o)
            if has_mix:
                dv0_o[...] = jnp.zeros_like(dv0_o)
                dvdn_o[...] = jnp.zeros_like(dvdn_o)
                dvup_o[...] = jnp.zeros_like(dvup_o)

        dmu_o[...] += jnp.sum(g_rw * (yp - y), axis=0, keepdims=True)
        dw0_o[...] += gs[3]
        dwup_o[...] += gs[4]
        da0_o[...] += gs[5]
        daup_o[...] += gs[6]
        dkkw_o[...] += gs[7]
        dkaw_o[...] += gs[8]
        if has_mix:
            dv0_o[...] += gs[9]
            dvdn_o[...] += gs[10]
            dvup_o[...] += gs[11]
            dvf_o[...] = gs[12]
        gm = g_rw * mu_v
        nxt = pltpu.roll(gm, TB - 1, 0)
        rowi = lax.broadcasted_iota(jnp.int32, gm.shape, 0)
        nxt = jnp.where(rowi == TB - 1, carry[...], nxt)
        dproj_o[...] = g_rw - gm + nxt
        carry[...] = gm[0:1, :]

    rev = lambda i: (nb - 1 - i, 0)
    row = lambda n: pl.BlockSpec((1, n), lambda i: (0, 0))
    full = lambda s: pl.BlockSpec(s, lambda i: (0,) * len(s))
    blk = pl.BlockSpec((TB, D_RWKV), rev)
    in_specs = [pl.BlockSpec((TB, RWKV_COLS), rev),
                pl.BlockSpec((8, RWKV_COLS), lambda i: (jnp.maximum((nb - 1 - i) * (TB // 8) - 1, 0), 0)),
                row(RWKV_COLS), row(D_RWKV), full((LANES, D_RWKV)), row(D_RWKV), full((LANES, D_RWKV)),
                row(D_RWKV), row(D_RWKV), full((256, 256))]
    args = [proj, proj, mu, w0, wup, a0, aup, kkw, kaw, ee]
    out_specs = [pl.BlockSpec((TB, RWKV_COLS), rev), row(RWKV_COLS), row(D_RWKV), full((LANES, D_RWKV)),
                 row(D_RWKV), full((LANES, D_RWKV)), row(D_RWKV), row(D_RWKV)]
    out_shape = [SDS((T, RWKV_COLS), f32), SDS((1, RWKV_COLS), f32), SDS((1, D_RWKV), f32), SDS((LANES, D_RWKV), f32),
                 SDS((1, D_RWKV), f32), SDS((LANES, D_RWKV), f32), SDS((1, D_RWKV), f32), SDS((1, D_RWKV), f32)]
    if has_mix:
        v0, vdn, vup, vfirst = vmix
        in_specs += [row(D_RWKV), full((D_RWKV, LANES)), full((LANES, D_RWKV)), blk]
        args += [v0, vdn, vup, vfirst]
        out_specs += [row(D_RWKV), full((D_RWKV, LANES)), full((LANES, D_RWKV)), blk]
        out_shape += [SDS((1, D_RWKV), f32), SDS((D_RWKV, LANES), f32), SDS((LANES, D_RWKV), f32),
                      SDS((T, D_RWKV), f32)]
    in_specs += [blk] * n_cot
    args += cot_list
    return pl.pallas_call(
        body, name=f"rwkv_pre_bwd_l{layer}", grid=(nb,), in_specs=in_specs, out_specs=out_specs, out_shape=out_shape,
        scratch_shapes=[pltpu.VMEM((1, RWKV_COLS), f32)],
        compiler_params=_params(("arbitrary",), 60),
    )(*args)


def _to_col(v):
    T = v.shape[0]
    vc = jnp.swapaxes(v.reshape(T, RWKV_HEADS, RWKV_HEAD), 1, 2)
    return jnp.pad(vc, ((0, 0), (0, 0), (0, LANES - RWKV_HEADS)))


def _from_col(vc):
    T = vc.shape[0]
    return jnp.swapaxes(vc[:, :, :RWKV_HEADS], 1, 2).reshape(T, D_RWKV)


def _rwkv_scan_fwd(r, w, k, vcol, a, b, ee, e16, e16t, layer):
    T = r.shape[0]
    nb = T // SCAN_TB

    def body(r_ref, w_ref, k_ref, v_ref, a_ref, b_ref, ee_ref, e16_ref, e16t_ref, o_ref, sp_ref, s_ref):
        @pl.when(pl.program_id(0) == 0)
        def _():
            s_ref[...] = jnp.zeros_like(s_ref)

        ee_v, e16_v, e16t_v = ee_ref[...], e16_ref[...], e16t_ref[...]

        def step(t, carry):
            sp = s_ref[...]
            sp_ref[t] = sp
            row = lambda ref: ref[pl.ds(t, 1), :]
            sa = _seg_raw(sp * row(a_ref), ee_v)
            vb = _dot3(v_ref[t], e16t_v)
            sn = sp * row(w_ref) + sa * row(b_ref) + vb * row(k_ref)
            s_ref[...] = sn
            o_ref[t] = _dot3(sn * row(r_ref), e16_v)
            return carry

        lax.fori_loop(0, SCAN_TB, step, 0)

    rows = pl.BlockSpec((SCAN_TB, D_RWKV), lambda i: (i, 0))
    col = pl.BlockSpec((SCAN_TB, RWKV_HEAD, LANES), lambda i: (i, 0, 0))
    full = lambda s: pl.BlockSpec(s, lambda i: (0,) * len(s))
    return pl.pallas_call(
        body, name=f"rwkv_scan_fwd_l{layer}", grid=(nb,),
        in_specs=[rows, rows, rows, col, rows, rows, full((256, 256)), full((D_RWKV, LANES)), full((LANES, D_RWKV))],
        out_specs=[col, pl.BlockSpec((SCAN_TB, RWKV_HEAD, D_RWKV), lambda i: (i, 0, 0))],
        out_shape=[SDS((T, RWKV_HEAD, LANES), f32), SDS((T, RWKV_HEAD, D_RWKV), f32)],
        scratch_shapes=[pltpu.VMEM((RWKV_HEAD, D_RWKV), f32)],
        compiler_params=_params(("arbitrary",), 40),
    )(r, w, k, vcol, a, b, ee, e16, e16t)


def _rwkv_scan_bwd(r, w, k, vcol, a, b, sprev, docol, ee, e16, e16t, layer):
    T = r.shape[0]
    nb = T // SCAN_TB

    def body(r_ref, w_ref, k_ref, v_ref, a_ref, b_ref, sp_ref, do_ref, ee_ref, e16_ref, e16t_ref,
             dr_o, dw_o, dk_o, da_o, db_o, dv_o, ds_ref):
        @pl.when(pl.program_id(0) == 0)
        def _():
            ds_ref[...] = jnp.zeros_like(ds_ref)

        ee_v, e16_v, e16t_v = ee_ref[...], e16_ref[...], e16t_ref[...]
        csum = lambda x: jnp.sum(x, axis=0, keepdims=True)

        def step(s, carry):
            t = SCAN_TB - 1 - s
            row = lambda ref: ref[pl.ds(t, 1), :]
            sp = sp_ref[t]
            a_r, b_r, w_r, k_r, r_r = row(a_ref), row(b_ref), row(w_ref), row(k_ref), row(r_ref)
            sa = _seg_raw(sp * a_r, ee_v)
            vb = _dot3(v_ref[t], e16t_v)
            dob = _dot3(do_ref[t], e16t_v)
            sn = sp * w_r + sa * b_r + vb * k_r
            ds = ds_ref[...] + dob * r_r
            dr_o[pl.ds(t, 1), :] = csum(sn * dob)
            dv_o[t] = _dot3(ds * k_r, e16_v)
            dk_o[pl.ds(t, 1), :] = csum(ds * vb)
            db_o[pl.ds(t, 1), :] = csum(ds * sa)
            dsa = _seg_raw(ds * b_r, ee_v)
            dw_o[pl.ds(t, 1), :] = csum(ds * sp)
            da_o[pl.ds(t, 1), :] = csum(sp * dsa)
            ds_ref[...] = ds * w_r + dsa * a_r
            return carry

        lax.fori_loop(0, SCAN_TB, step, 0)

    rows = pl.BlockSpec((SCAN_TB, D_RWKV), lambda i: (nb - 1 - i, 0))
    col = pl.BlockSpec((SCAN_TB, RWKV_HEAD, LANES), lambda i: (nb - 1 - i, 0, 0))
    st = pl.BlockSpec((SCAN_TB, RWKV_HEAD, D_RWKV), lambda i: (nb - 1 - i, 0, 0))
    full = lambda s: pl.BlockSpec(s, lambda i: (0,) * len(s))
    return pl.pallas_call(
        body, name=f"rwkv_scan_bwd_l{layer}", grid=(nb,),
        in_specs=[rows, rows, rows, col, rows, rows, st, col, full((256, 256)), full((D_RWKV, LANES)),
                  full((LANES, D_RWKV))],
        out_specs=[rows] * 5 + [col],
        out_shape=[SDS((T, D_RWKV), f32)] * 5 + [SDS((T, RWKV_HEAD, LANES), f32)],
        scratch_shapes=[pltpu.VMEM((RWKV_HEAD, D_RWKV), f32)],
        compiler_params=_params(("arbitrary",), 40),
    )(r, w, k, vcol, a, b, sprev, docol, ee, e16, e16t)


def _post_core(o, r, k2, v, z, gw, gb, rk, ee):
    inv = 1.0 / RWKV_HEAD
    mu = _seg(o, ee) * inv
    d = o - mu
    var = _seg(d * d, ee) * inv
    on = d * lax.rsqrt(var + GN_EPS) * gw + gb
    bonus = _seg(r * k2 * rk, ee) * v
    return (on + bonus) * jax.nn.silu(z)


def _rwkv_post_fwd(o, r, k2, v, z, gw, gb, rk, ee, layer):
    T = o.shape[0]

    def body(o_ref, r_ref, k_ref, v_ref, z_ref, gw_ref, gb_ref, rk_ref, ee_ref, out_ref):
        out_ref[...] = _post_core(o_ref[...], r_ref[...], k_ref[...], v_ref[...], z_ref[...], gw_ref[...],
                                  gb_ref[...], rk_ref[...], ee_ref[...])

    blk = pl.BlockSpec((TB, D_RWKV), lambda i: (i, 0))
    row = pl.BlockSpec((1, D_RWKV), lambda i: (0, 0))
    return pl.pallas_call(
        body, name=f"rwkv_post_fwd_l{layer}", grid=(T // TB,),
        in_specs=[blk] * 5 + [row] * 3 + [pl.BlockSpec((256, 256), lambda i: (0, 0))],
        out_specs=blk, out_shape=SDS((T, D_MODEL), f32), compiler_params=_params(("parallel",), 40),
    )(o, r, k2, v, z, gw, gb, rk, ee)


def _rwkv_post_bwd(o, r, k2, v, z, gw, gb, rk, ee, dcat, layer):
    T = o.shape[0]

    def body(o_ref, r_ref, k_ref, v_ref, z_ref, gw_ref, gb_ref, rk_ref, ee_ref, g_ref,
             do_o, dr_o, dk_o, dv_o, dz_o, dgw_o, dgb_o, drk_o):
        ee_v = ee_ref[...]
        fn = lambda o_, r_, k_, v_, z_, gw_, gb_, rk_: _post_core(o_, r_, k_, v_, z_, gw_, gb_, rk_, ee_v)
        _, vjp = jax.vjp(fn, o_ref[...], r_ref[...], k_ref[...], v_ref[...], z_ref[...], gw_ref[...], gb_ref[...],
                         rk_ref[...])
        gs = vjp(g_ref[...])
        do_o[...], dr_o[...], dk_o[...], dv_o[...], dz_o[...] = gs[:5]

        @pl.when(pl.program_id(0) == 0)
        def _():
            dgw_o[...] = jnp.zeros_like(dgw_o)
            dgb_o[...] = jnp.zeros_like(dgb_o)
            drk_o[...] = jnp.zeros_like(drk_o)

        dgw_o[...] += gs[5]
        dgb_o[...] += gs[6]
        drk_o[...] += gs[7]

    blk = pl.BlockSpec((TB, D_RWKV), lambda i: (i, 0))
    row = pl.BlockSpec((1, D_RWKV), lambda i: (0, 0))
    return pl.pallas_call(
        body, name=f"rwkv_post_bwd_l{layer}", grid=(T // TB,),
        in_specs=[blk] * 5 + [row] * 3 + [pl.BlockSpec((256, 256), lambda i: (0, 0)), blk],
        out_specs=[blk] * 5 + [row] * 3,
        out_shape=[SDS((T, D_RWKV), f32)] * 5 + [SDS((1, D_RWKV), f32)] * 3,
        compiler_params=_params(("arbitrary",), 48),
    )(o, r, k2, v, z, gw, gb, rk, ee, dcat)


def _hgrn_chunk(qr, fr, ii, z, S, lbl, gw, layer):
    L = CHUNK
    m = jnp.max(lbl, axis=0, keepdims=True)
    e = jnp.exp(lbl - m)
    sm = e / jnp.sum(e, axis=0, keepdims=True)
    if layer == 0:
        lb = sm[0:1] - sm[0:1]
    else:
        lb = (sm[0:1] + sm[1:2]) - sm[0:1]
    q = jax.nn.silu(qr)
    log_lb = jnp.log(jnp.maximum(lb, LB_FLOOR))
    lf = jnp.logaddexp(log_lb, jnp.log1p(-lb) + jax.nn.log_sigmoid(fr))
    k = (1.0 - lb) * jax.nn.sigmoid(-fr)
    ti = lax.broadcasted_iota(jnp.int32, (L, L), 0)
    si = lax.broadcasted_iota(jnp.int32, (L, L), 1)
    ltri = (si <= ti).astype(f32)
    b = jnp.dot(ltri, lf, precision=HIGHEST, preferred_element_type=f32)
    btot = jnp.dot(jnp.ones((L, L), f32), lf, precision=HIGHEST, preferred_element_type=f32)
    o = jnp.dot(q * jnp.exp(b), S, preferred_element_type=f32)
    nsub = L // HSUB
    t3 = lax.broadcasted_iota(jnp.int32, (HSUB, HSUB, HG_HEAD), 0)
    s3 = lax.broadcasted_iota(jnp.int32, (HSUB, HSUB, HG_HEAD), 1)
    causal = s3 <= t3
    rows = []
    for I in range(nsub):
        sl = slice(I * HSUB, (I + 1) * HSUB)
        bI, qI, kI, iI = b[sl], q[sl], k[sl], ii[sl]
        d3 = bI[:, None, :] - bI[None, :, :]
        dec = jnp.where(causal, jnp.exp(jnp.where(causal, d3, 0.0)), 0.0)
        att = jnp.sum(qI[:, None, :] * dec * kI[None, :, :], axis=-1)
        oI = jnp.dot(att, iI, preferred_element_type=f32)
        for J in range(I):
            sj = slice(J * HSUB, (J + 1) * HSUB)
            bm = b[(J + 1) * HSUB - 1:(J + 1) * HSUB]
            qs = qI * jnp.exp(bI - bm)
            ks = k[sj] * jnp.exp(bm - b[sj])
            att_ij = lax.dot_general(qs, ks, (((1,), (1,)), ((), ())), preferred_element_type=f32)
            oI = oI + jnp.dot(att_ij, ii[sj], preferred_element_type=f32)
        rows.append(oI)
    o = o + jnp.concatenate(rows, axis=0)
    k_dec = k * jnp.exp(btot - b)
    bcol = lax.dot_general(lf, jnp.ones((L, HG_HEAD), f32), (((0,), (0,)), ((), ())), precision=HIGHEST,
                           preferred_element_type=f32)
    s_new = S * jnp.exp(bcol) + lax.dot_general(k_dec, ii, (((0,), (0,)), ((), ())), preferred_element_type=f32)
    ms = jnp.mean(o * o, axis=-1, keepdims=True)
    out = o * lax.rsqrt(ms + RMS_EPS) * gw * jax.nn.silu(z)
    return out, s_new


def _hg_specs(nc, rev):
    cidx = (lambda c: nc - 1 - c) if rev else (lambda c: c)
    pj = lambda off: pl.BlockSpec((CHUNK, HG_HEAD), lambda h, c: (cidx(c), RW_BLKS + off + h))
    return cidx, [pj(0), pj(HG_HEADS), pj(2 * HG_HEADS), pj(3 * HG_HEADS)]


def _hgrn_fwd(proj, cat, lbl, gw, layer):
    T = proj.shape[0]
    nc = T // CHUNK
    cidx, pspecs = _hg_specs(nc, False)

    def body(q_ref, f_ref, i_ref, z_ref, lbl_ref, gw_ref, cat_in, out_ref, sh_ref, s_ref):
        del cat_in

        @pl.when(pl.program_id(1) == 0)
        def _():
            s_ref[...] = jnp.zeros_like(s_ref)

        s0 = s_ref[...]
        sh_ref[0, 0] = s0
        out, s_new = _hgrn_chunk(q_ref[...], f_ref[...], i_ref[...], z_ref[...], s0, lbl_ref[...], gw_ref[...], layer)
        out_ref[...] = out
        s_ref[...] = s_new

    return pl.pallas_call(
        body, name=f"hgrn_fwd_l{layer}", grid=(HG_HEADS, nc),
        in_specs=pspecs + [pl.BlockSpec((2, HG_HEAD), lambda h, c: (0, h)),
                           pl.BlockSpec((1, HG_HEAD), lambda h, c: (0, h)),
                           pl.BlockSpec(memory_space=pl.ANY)],
        out_specs=[pl.BlockSpec((CHUNK, HG_HEAD), lambda h, c: (c, HG_HEADS + h)),
                   pl.BlockSpec((1, 1, HG_HEAD, HG_HEAD), lambda h, c: (h, c, 0, 0))],
        out_shape=[SDS((T, D_MODEL), f32), SDS((HG_HEADS, nc, HG_HEAD, HG_HEAD), f32)],
        scratch_shapes=[pltpu.VMEM((HG_HEAD, HG_HEAD), f32)],
        input_output_aliases={6: 0},
        compiler_params=_params(("parallel", "arbitrary"), 40),
    )(proj, proj, proj, proj, lbl, gw, cat)


def _hgrn_bwd(proj, shist, dcat, lbl, gw, layer):
    T = proj.shape[0]
    nc = T // CHUNK
    cidx, pspecs = _hg_specs(nc, True)

    def body(q_ref, f_ref, i_ref, z_ref, lbl_ref, gw_ref, sh_ref, g_ref,
             dq_o, df_o, di_o, dz_o, dlbl_o, dgw_o, ds_ref):
        @pl.when(pl.program_id(1) == 0)
        def _():
            ds_ref[...] = jnp.zeros_like(ds_ref)
            dlbl_o[...] = jnp.zeros_like(dlbl_o)
            dgw_o[...] = jnp.zeros_like(dgw_o)

        fn = functools.partial(_hgrn_chunk, layer=layer)
        _, vjp = jax.vjp(fn, q_ref[...], f_ref[...], i_ref[...], z_ref[...], sh_ref[0, 0], lbl_ref[...], gw_ref[...])
        dq, df, di, dz, ds, dl, dg = vjp((g_ref[...], ds_ref[...]))
        dq_o[...], df_o[...], di_o[...], dz_o[...] = dq, df, di, dz
        ds_ref[...] = ds
        dlbl_o[...] += dl
        dgw_o[...] += dg

    ospec = pl.BlockSpec((CHUNK, HG_HEAD), lambda h, c: (cidx(c), h))
    return pl.pallas_call(
        body, name=f"hgrn_bwd_l{layer}", grid=(HG_HEADS, nc),
        in_specs=pspecs + [pl.BlockSpec((2, HG_HEAD), lambda h, c: (0, h)),
                           pl.BlockSpec((1, HG_HEAD), lambda h, c: (0, h)),
                           pl.BlockSpec((1, 1, HG_HEAD, HG_HEAD), lambda h, c: (h, cidx(c), 0, 0)),
                           pl.BlockSpec((CHUNK, HG_HEAD), lambda h, c: (cidx(c), HG_HEADS + h))],
        out_specs=[ospec] * 4 + [pl.BlockSpec((2, HG_HEAD), lambda h, c: (0, h)),
                                 pl.BlockSpec((1, HG_HEAD), lambda h, c: (0, h))],
        out_shape=[SDS((T, D_HGRN), f32)] * 4 + [SDS((2, D_HGRN), f32), SDS((1, D_HGRN), f32)],
        scratch_shapes=[pltpu.VMEM((HG_HEAD, HG_HEAD), f32)],
        compiler_params=_params(("parallel", "arbitrary"), 48),
    )(proj, proj, proj, proj, lbl, gw, shist, dcat)


def _ln_core(h, y, w, b):
    u = ALPHA * h + y
    mu = jnp.mean(u, axis=-1, keepdims=True)
    d = u - mu
    var = jnp.mean(d * d, axis=-1, keepdims=True)
    return d * lax.rsqrt(var + LN_EPS) * w + b


def _ln_fwd(h, y, w, b, layer):
    T = h.shape[0]

    def body(h_ref, y_ref, w_ref, b_ref, o_ref):
        o_ref[...] = _ln_core(h_ref[...], y_ref[...], w_ref[...], b_ref[...])

    blk = pl.BlockSpec((TB, D_MODEL), lambda i: (i, 0))
    row = pl.BlockSpec((1, D_MODEL), lambda i: (0, 0))
    return pl.pallas_call(body, name=f"ln_fwd_l{layer}", grid=(T // TB,), in_specs=[blk, blk, row, row],
                          out_specs=blk, out_shape=SDS((T, D_MODEL), f32),
                          compiler_params=_params(("parallel",), 40))(h, y, w, b)


def _ln_bwd(h, y, w, b, g, layer):
    T = h.shape[0]

    def body(h_ref, y_ref, w_ref, b_ref, g_ref, dh_o, dy_o, dw_o, db_o):
        _, vjp = jax.vjp(_ln_core, h_ref[...], y_ref[...], w_ref[...], b_ref[...])
        dh, dy, dw, db = vjp(g_ref[...])
        dh_o[...] = dh
        dy_o[...] = dy

        @pl.when(pl.program_id(0) == 0)
        def _():
            dw_o[...] = jnp.zeros_like(dw_o)
            db_o[...] = jnp.zeros_like(db_o)

        dw_o[...] += dw
        db_o[...] += db

    blk = pl.BlockSpec((TB, D_MODEL), lambda i: (i, 0))
    row = pl.BlockSpec((1, D_MODEL), lambda i: (0, 0))
    return pl.pallas_call(body, name=f"ln_bwd_l{layer}", grid=(T // TB,), in_specs=[blk, blk, row, row, blk],
                          out_specs=[blk, blk, row, row],
                          out_shape=[SDS((T, D_MODEL), f32)] * 2 + [SDS((1, D_MODEL), f32)] * 2,
                          compiler_params=_params(("arbitrary",), 48))(h, y, w, b, g)


def _loss_head(hout, target):
    T = hout.shape[0]

    def body(h_ref, t_ref, loss_o, g_o):
        err = h_ref[...] - t_ref[...]
        g_o[...] = err * (1.0 / D_MODEL)

        @pl.when(pl.program_id(0) == 0)
        def _():
            loss_o[...] = jnp.zeros_like(loss_o)

        part = jnp.sum(jnp.sum(err * err, axis=-1, keepdims=True) * (1.0 / D_MODEL), axis=0, keepdims=True)
        loss_o[...] += 0.5 * part

    blk = pl.BlockSpec((TB, D_MODEL), lambda i: (i, 0))
    return pl.pallas_call(body, name="loss_head", grid=(T // TB,), in_specs=[blk, blk],
                          out_specs=[pl.BlockSpec((1, 1), lambda i: (0, 0)), blk],
                          out_shape=[SDS((1, 1), f32), SDS((T, D_MODEL), f32)],
                          compiler_params=_params(("arbitrary",), 40))(hout, target)


def _pad_rows(m, lo, total):
    return jnp.pad(m, ((0, 0), (lo, total - lo - m.shape[1]), (0, 0)))


def _device_step(x, target, w_in, w_out, shift_mu, w_decay0, w_decay_up, a0, a_up, k_k, k_a, r_k, ln_x_w, ln_x_b,
                 v_mix0, v_mix_down, v_mix_up, lb_logits, g_norm_w, ln_w, ln_b):
    T = x.shape[0]
    ee, e16, e16t = _const_mats()
    wup_pad = _pad_rows(w_decay_up, 0, LANES)
    aup_pad = _pad_rows(a_up, LORA, LANES)
    vdn_pad = jnp.pad(v_mix_down, ((0, 0), (0, 0), (0, LANES - VRES)))
    vup_pad = _pad_rows(v_mix_up, 0, LANES)
    row = lambda p, l: p[l][None, :]

    def vmix_of(l, vfirst):
        if l == 0:
            return None
        return (row(v_mix0, l - 1), vdn_pad[l - 1], vup_pad[l - 1], vfirst)

    h = x
    saved = []
    vfirst = None
    for l in range(DEPTH):
        proj = _matmul(h, w_in, b_layer=l, mode="nn", tm=512, tn=1664, tk=D_MODEL, name=f"proj_fwd_l{l}")
        pre_args = (proj, row(shift_mu, l), row(w_decay0, l), wup_pad[l], row(a0, l), aup_pad[l], row(k_k, l),
                    row(k_a, l), ee, vmix_of(l, vfirst))
        r, w, k2, v, av, bv, z = _rwkv_pre_fwd(*pre_args, layer=l)
        if l == 0:
            vfirst = v
        vcol = _to_col(v)
        ocol, sprev = _rwkv_scan_fwd(r, w, k2, vcol, av, bv, ee, e16, e16t, l)
        o = _from_col(ocol)
        post_args = (o, r, k2, v, z, row(ln_x_w, l), row(ln_x_b, l), row(r_k, l), ee)
        cat = _rwkv_post_fwd(*post_args, layer=l)
        cat, shist = _hgrn_fwd(proj, cat, lb_logits, row(g_norm_w, l), l)
        y = _matmul(cat, w_out, b_layer=l, mode="nn", tm=512, tn=1024, tk=D_MODEL, name=f"out_fwd_l{l}")
        h_new = _ln_fwd(h, y, row(ln_w, l), row(ln_b, l), l)
        saved.append(dict(h=h, proj=proj, pre_args=pre_args, post_args=post_args, scan=(r, w, k2, vcol, av, bv, sprev),
                          cat=cat, shist=shist, y=y))
        h = h_new

    loss_sum, g = _loss_head(h, target)

    grads = {n: [None] * DEPTH for n in ("w_in", "w_out", "shift_mu", "w_decay0", "w_decay_up", "a0", "a_up", "k_k",
                                         "k_a", "r_k", "ln_x_w", "ln_x_b", "g_norm_w", "ln_w", "ln_b")}
    dlbl = []
    dvfirst = None
    for l in reversed(range(DEPTH)):
        s = saved[l]
        dh_res, dy, dlnw, dlnb = _ln_bwd(s["h"], s["y"], row(ln_w, l), row(ln_b, l), g, l)
        grads["ln_w"][l], grads["ln_b"][l] = dlnw[0], dlnb[0]
        dcat = _matmul(dy, w_out, b_layer=l, mode="nt", tm=512, tn=1024, tk=D_MODEL, name=f"out_bwd_dx_l{l}")
        grads["w_out"][l] = _matmul(s["cat"], dy, mode="tn", tm=1024, tn=1024, tk=512, name=f"out_bwd_dw_l{l}")
        do, dr_p, dk_p, dv_p, dz, dgw, dgb, drk = _rwkv_post_bwd(*s["post_args"], dcat, layer=l)
        grads["ln_x_w"][l], grads["ln_x_b"][l], grads["r_k"][l] = dgw[0], dgb[0], drk[0]
        r, w, k2, vcol, av, bv, sprev = s["scan"]
        dr_s, dw_s, dk_s, da_s, db_s, dvcol = _rwkv_scan_bwd(r, w, k2, vcol, av, bv, sprev, _to_col(do), ee, e16,
                                                            e16t, l)
        v_cots = [dv_p, _from_col(dvcol)] + ([dvfirst] if (l == 0 and dvfirst is not None) else [])
        cots = dict(r=[dr_p, dr_s], w=dw_s, k=[dk_p, dk_s], v=v_cots, a=da_s, b=db_s, z=dz)
        outs = _rwkv_pre_bwd(*s["pre_args"], cots, layer=l)
        dproj, dmu, dw0, dwup, da0, daup, dkkw, dkaw = outs[:8]
        grads["shift_mu"][l], grads["w_decay0"][l], grads["a0"][l] = dmu[0], dw0[0], da0[0]
        grads["k_k"][l], grads["k_a"][l] = dkkw[0], dkaw[0]
        grads["w_decay_up"][l], grads["a_up"][l] = dwup[:LORA], daup[LORA:]
        if l > 0:
            dv0, dvdn, dvup, dvfirst = outs[8:]
            g_vmix = (dv0, dvdn[:, :VRES][None], dvup[:VRES][None])
        dq, df, di, dzh, dl, dgn = _hgrn_bwd(s["proj"], s["shist"], dcat, lb_logits, row(g_norm_w, l), l)
        dlbl.append(dl)
        grads["g_norm_w"][l] = dgn[0]
        dproj = jnp.concatenate([dproj, dq, df, di, dzh], axis=1)
        grads["w_in"][l] = _matmul(s["h"], dproj, mode="tn", tm=1024, tn=1664, tk=512, name=f"proj_bwd_dw_l{l}")
        g = _matmul(dproj, w_in, b_layer=l, mode="nt", tm=512, tn=1024, tk=1664, name=f"proj_bwd_dx_l{l}",
                    add=dh_res)
    out = {n: jnp.stack(v) for n, v in grads.items()}
    out["lb_logits"] = dlbl[0] + dlbl[1]
    out["v_mix0"], out["v_mix_down"], out["v_mix_up"] = g_vmix
    return loss_sum, g, out


CHIP_COMBOS = ((1, 0), (0, 1), (1, 1))
HBM_SPEC = pl.BlockSpec(memory_space=pl.ANY)


def _mesh_pos():
    return lax.axis_index("x"), lax.axis_index("y"), lax.axis_index("c")


def _flip(v, d):
    return 1 - v if d else v


def _half(ref, dim, cc, hs):
    idx = [slice(None)] * len(ref.shape)
    idx[dim] = pl.ds(cc * hs, hs)
    return ref.at[tuple(idx)]


def _gather_shards(arrs, half_dims):
    n = len(arrs)
    hs = [a.shape[d] // 2 for a, d in zip(arrs, half_dims)]

    def body(*refs):
        ins, outs = refs[:n], refs[n:2 * n]
        send_sems, recv_sems, local_sems = refs[2 * n:]
        x, y, c = _mesh_pos()
        kc = 2 * x + y
        local, sends = [], []
        for a in range(n):
            hd, h = half_dims[a], hs[a]
            loc = pltpu.make_async_copy(ins[a], outs[a].at[kc], local_sems.at[a])
            loc.start()
            local.append(loc)
            for j, (dx, dy) in enumerate(CHIP_COMBOS):
                cp = pltpu.make_async_remote_copy(
                    src_ref=_half(ins[a], hd, c, h), dst_ref=_half(outs[a].at[kc], hd, c, h),
                    send_sem=send_sems.at[a, j], recv_sem=recv_sems.at[a, j],
                    device_id=(_flip(x, dx), _flip(y, dy), c), device_id_type=MESH)
                cp.start()
                sends.append(cp)
        for a in range(n):
            hd, h = half_dims[a], hs[a]
            for j, (dx, dy) in enumerate(CHIP_COMBOS):
                kj = 2 * _flip(x, dx) + _flip(y, dy)
                landed = _half(outs[a].at[kj], hd, c, h)
                pltpu.make_async_remote_copy(
                    src_ref=landed, dst_ref=landed, send_sem=send_sems.at[a, j], recv_sem=recv_sems.at[a, j],
                    device_id=(x, y, c), device_id_type=MESH).wait_recv()
                fw = pltpu.make_async_remote_copy(
                    src_ref=landed, dst_ref=landed, send_sem=send_sems.at[a, 3 + j], recv_sem=recv_sems.at[a, 3 + j],
                    device_id=(x, y, 1 - c), device_id_type=MESH)
                fw.start()
                sends.append(fw)
        for a in range(n):
            hd, h = half_dims[a], hs[a]
            for j, (dx, dy) in enumerate(CHIP_COMBOS):
                kj = 2 * _flip(x, dx) + _flip(y, dy)
                other = _half(outs[a].at[kj], hd, 1 - c, h)
                pltpu.make_async_remote_copy(
                    src_ref=other, dst_ref=other, send_sem=send_sems.at[a, 3 + j], recv_sem=recv_sems.at[a, 3 + j],
                    device_id=(x, y, 1 - c), device_id_type=MESH).wait_recv()
        for cp in sends:
            cp.wait_send()
        for loc in local:
            loc.wait()

    return pl.pallas_call(
        body, name="gather_shards", in_specs=[HBM_SPEC] * n, out_specs=[HBM_SPEC] * n,
        out_shape=[SDS((4,) + a.shape, a.dtype) for a in arrs],
        scratch_shapes=[pltpu.SemaphoreType.DMA((n, 6)), pltpu.SemaphoreType.DMA((n, 6)),
                        pltpu.SemaphoreType.DMA((n,))],
    )(*arrs)


DEV_COMBOS = tuple((dx, dy, dc) for dx in (0, 1) for dy in (0, 1) for dc in (0, 1))[1:]


def _allreduce_small(buf):
    R = buf.shape[0]

    def body(x_ref, o_ref, slots, send_sems, recv_sems):
        x, y, c = _mesh_pos()
        me = 4 * x + 2 * y + c
        cps = []
        for k, (dx, dy, dc) in enumerate(DEV_COMBOS):
            cp = pltpu.make_async_remote_copy(
                src_ref=x_ref, dst_ref=slots.at[k + 1], send_sem=send_sems.at[k], recv_sem=recv_sems.at[k],
                device_id=(_flip(x, dx), _flip(y, dy), _flip(c, dc)), device_id_type=MESH)
            cp.start()
            cps.append(cp)
        slots[0] = x_ref[...]
        for cp in cps:
            cp.wait_recv()
        acc = slots[jnp.bitwise_xor(me, 0)]
        for d in range(1, 8):
            acc = acc + slots[jnp.bitwise_xor(me, d)]
        o_ref[...] = acc
        for cp in cps:
            cp.wait_send()

    vm = pl.BlockSpec(memory_space=pltpu.VMEM)
    return pl.pallas_call(
        body, name="allreduce_small", in_specs=[vm], out_specs=vm, out_shape=SDS((R, LANES), f32),
        scratch_shapes=[pltpu.VMEM((8, R, LANES), f32), pltpu.SemaphoreType.DMA((7,)), pltpu.SemaphoreType.DMA((7,))],
        compiler_params=_params(None, 48),
    )(buf)


def _swap_halves(gs):
    n = len(gs)

    def body(*refs):
        ins, outs = refs[:n], refs[n:2 * n]
        send_sems, recv_sems = refs[2 * n:]
        x, y, c = _mesh_pos()
        cps = []
        for a in range(n):
            h = ins[a].shape[1] // 2
            cp = pltpu.make_async_remote_copy(
                src_ref=_half(ins[a], 1, 1 - c, h), dst_ref=outs[a], send_sem=send_sems.at[a],
                recv_sem=recv_sems.at[a], device_id=(x, y, 1 - c), device_id_type=MESH)
            cp.start()
            cps.append(cp)
        for cp in cps:
            cp.wait()

    return pl.pallas_call(
        body, name="rs_swap_halves", in_specs=[HBM_SPEC] * n, out_specs=[HBM_SPEC] * n,
        out_shape=[SDS((g.shape[0], g.shape[1] // 2, g.shape[2]), g.dtype) for g in gs],
        scratch_shapes=[pltpu.SemaphoreType.DMA((n,)), pltpu.SemaphoreType.DMA((n,))],
    )(*gs)


def _send_to_chips(ps):
    n = len(ps)

    def body(*refs):
        ins, outs = refs[:n], refs[n:2 * n]
        send_sems, recv_sems = refs[2 * n:]
        x, y, c = _mesh_pos()
        cps = []
        for a in range(n):
            for j, (dx, dy) in enumerate(CHIP_COMBOS):
                px, py = _flip(x, dx), _flip(y, dy)
                cp = pltpu.make_async_remote_copy(
                    src_ref=ins[a].at[2 * px + py], dst_ref=outs[a].at[j], send_sem=send_sems.at[a, j],
                    recv_sem=recv_sems.at[a, j], device_id=(px, py, c), device_id_type=MESH)
                cp.start()
                cps.append(cp)
        for cp in cps:
            cp.wait()

    return pl.pallas_call(
        body, name="rs_send_to_chips", in_specs=[HBM_SPEC] * n, out_specs=[HBM_SPEC] * n,
        out_shape=[SDS((3,) + p.shape[1:], p.dtype) for p in ps],
        scratch_shapes=[pltpu.SemaphoreType.DMA((n, 3)), pltpu.SemaphoreType.DMA((n, 3))],
    )(*ps)


def _share_halves(qs):
    n = len(qs)

    def body(*refs):
        ins, outs = refs[:n], refs[n:2 * n]
        send_sems, recv_sems, local_sems = refs[2 * n:]
        x, y, c = _mesh_pos()
        cps, locs = [], []
        for a in range(n):
            h = ins[a].shape[1]
            mine = _half(outs[a], 1, c, h)
            loc = pltpu.make_async_copy(ins[a], mine, local_sems.at[a])
            loc.start()
            locs.append(loc)
            cp = pltpu.make_async_remote_copy(
                src_ref=ins[a], dst_ref=mine, send_sem=send_sems.at[a], recv_sem=recv_sems.at[a],
                device_id=(x, y, 1 - c), device_id_type=MESH)
            cp.start()
            cps.append(cp)
        for a in range(n):
            cps[a].wait_send()
            other = _half(outs[a], 1, 1 - c, ins[a].shape[1])
            pltpu.make_async_remote_copy(
                src_ref=other, dst_ref=other, send_sem=send_sems.at[a], recv_sem=recv_sems.at[a],
                device_id=(x, y, 1 - c), device_id_type=MESH).wait_recv()
            locs[a].wait()

    return pl.pallas_call(
        body, name="rs_share_halves", in_specs=[HBM_SPEC] * n, out_specs=[HBM_SPEC] * n,
        out_shape=[SDS((q.shape[0], 2 * q.shape[1], q.shape[2]), q.dtype) for q in qs],
        scratch_shapes=[pltpu.SemaphoreType.DMA((n,)), pltpu.SemaphoreType.DMA((n,)), pltpu.SemaphoreType.DMA((n,))],
    )(*qs)


def _add_own_half(g, recv, c, name):
    S8, R, C = g.shape
    h = R // 2
    tr = min(256, h)
    nt = h // tr

    def body(c_ref, g_ref, r_ref, o_ref):
        del c_ref
        o_ref[...] = g_ref[...] + r_ref[...]

    return pl.pallas_call(
        body, name=name,
        grid_spec=pltpu.PrefetchScalarGridSpec(
            num_scalar_prefetch=1, grid=(S8, nt),
            in_specs=[pl.BlockSpec((1, tr, C), lambda s, i, cr: (s, cr[0] * nt + i, 0)),
                      pl.BlockSpec((1, tr, C), lambda s, i, cr: (s, i, 0))],
            out_specs=pl.BlockSpec((1, tr, C), lambda s, i, cr: (s, i, 0))),
        out_shape=SDS((S8, h, C), f32), compiler_params=_params(("parallel", "parallel"), 40),
    )(c, g, recv)


def _add_chip_parts(p, recv, kc, name):
    _, L, R, C = p.shape
    tr = min(256, R)

    def body(k_ref, p_ref, r0, r1, r2, o_ref):
        del k_ref
        o_ref[...] = ((p_ref[0] + r0[0]) + r1[0]) + r2[0]

    rspec = lambda j: pl.BlockSpec((1, 1, tr, C), lambda l, i, kr: (j, l, i, 0))
    return pl.pallas_call(
        body, name=name,
        grid_spec=pltpu.PrefetchScalarGridSpec(
            num_scalar_prefetch=1, grid=(L, R // tr),
            in_specs=[pl.BlockSpec((1, 1, tr, C), lambda l, i, kr: (kr[0], l, i, 0)), rspec(0), rspec(1), rspec(2)],
            out_specs=pl.BlockSpec((1, tr, C), lambda l, i, kr: (l, i, 0))),
        out_shape=SDS((L, R, C), f32), compiler_params=_params(("parallel", "parallel"), 40),
    )(kc, p, recv, recv, recv)


def _reduce_scatter(gs, c, kc):
    flat = [g.reshape((8,) + g.shape[2:]) for g in gs]
    got = _swap_halves(flat)
    ps = [_add_own_half(f, r, c, f"rs_add_pair_{a}").reshape((4, 2) + r.shape[1:])
          for a, (f, r) in enumerate(zip(flat, got))]
    got2 = _send_to_chips(ps)
    qs = [_add_chip_parts(p, r, kc, f"rs_add_chips_{a}") for a, (p, r) in enumerate(zip(ps, got2))]
    return _share_halves(qs)


def _adamw(w, g, m, v, name):
    R, C = w.shape
    tr = 256 if R % 256 == 0 else R

    def body(w_ref, g_ref, m_ref, v_ref, d_o, m_o, v_o):
        g_ = g_ref[...]
        m2 = ADAM_B1 * m_ref[...] + (1.0 - ADAM_B1) * g_
        v2 = ADAM_B2 * v_ref[...] + (1.0 - ADAM_B2) * (g_ * g_)
        m_hat = m2 / (1.0 - ADAM_B1 ** ADAM_STEP)
        v_hat = v2 / (1.0 - ADAM_B2 ** ADAM_STEP)
        d_o[...] = -ADAM_LR * (m_hat / (jnp.sqrt(v_hat) + ADAM_EPS) + ADAM_WD * w_ref[...])
        m_o[...] = m2
        v_o[...] = v2

    blk = pl.BlockSpec((tr, C), lambda i: (i, 0))
    return pl.pallas_call(body, name=name, grid=(R // tr,), in_specs=[blk] * 4, out_specs=[blk] * 3,
                          out_shape=[SDS((R, C), f32)] * 3, compiler_params=_params(("parallel",), 40))(w, g, m, v)


def _pack(arrs):
    flat = [a.reshape(-1, LANES) for a in arrs]
    rows = sum(f.shape[0] for f in flat)
    pad = (-rows) % 16
    if pad:
        flat.append(jnp.zeros((pad, LANES), f32))
    return jnp.concatenate(flat, axis=0)


def _unpack(buf, shapes):
    out, r = [], 0
    for s in shapes:
        n = math.prod(s) // LANES
        out.append(buf[r:r + n].reshape(s))
        r += n
    return out


REPLICATED = ("shift_mu", "w_decay0", "a0", "k_k", "k_a", "r_k", "ln_x_w", "ln_x_b", "v_mix0", "lb_logits", "g_norm_w",
              "ln_w", "ln_b")
CHIP_SMALL = (("w_decay_up", 2), ("a_up", 2), ("v_mix_down", 1), ("v_mix_up", 2))
WEIGHTS = ("w_in", "shift_mu", "w_decay0", "w_decay_up", "a0", "a_up", "k_k", "k_a", "r_k", "ln_x_w", "ln_x_b",
           "v_mix0", "v_mix_down", "v_mix_up", "lb_logits", "g_norm_w", "w_out", "ln_w", "ln_b")


def kernel(x, w_in, shift_mu, w_decay0, w_decay_up, a0, a_up, k_k, k_a, r_k, ln_x_w, ln_x_b, v_mix0, v_mix_down, v_mix_up, lb_logits, g_norm_w, w_out, ln_w, ln_b, loss_target, m_w_in, m_shift_mu, m_w_decay0, m_w_decay_up, m_a0, m_a_up, m_k_k, m_k_a, m_r_k, m_ln_x_w, m_ln_x_b, m_v_mix0, m_v_mix_down, m_v_mix_up, m_lb_logits, m_g_norm_w, m_w_out, m_ln_w, m_ln_b, v_w_in, v_shift_mu, v_w_decay0, v_w_decay_up, v_a0, v_a_up, v_k_k, v_k_a, v_r_k, v_ln_x_w, v_ln_x_b, v_v_mix0, v_v_mix_down, v_v_mix_up, v_lb_logits, v_g_norm_w, v_w_out, v_ln_w, v_ln_b):
    W = dict(w_in=w_in, shift_mu=shift_mu, w_decay0=w_decay0, w_decay_up=w_decay_up, a0=a0, a_up=a_up, k_k=k_k, k_a=k_a,
             r_k=r_k, ln_x_w=ln_x_w, ln_x_b=ln_x_b, v_mix0=v_mix0, v_mix_down=v_mix_down, v_mix_up=v_mix_up,
             lb_logits=lb_logits, g_norm_w=g_norm_w, w_out=w_out, ln_w=ln_w, ln_b=ln_b)
    M = dict(w_in=m_w_in, shift_mu=m_shift_mu, w_decay0=m_w_decay0, w_decay_up=m_w_decay_up, a0=m_a0, a_up=m_a_up,
             k_k=m_k_k, k_a=m_k_a, r_k=m_r_k, ln_x_w=m_ln_x_w, ln_x_b=m_ln_x_b, v_mix0=m_v_mix0,
             v_mix_down=m_v_mix_down, v_mix_up=m_v_mix_up, lb_logits=m_lb_logits, g_norm_w=m_g_norm_w, w_out=m_w_out,
             ln_w=m_ln_w, ln_b=m_ln_b)
    V = dict(w_in=v_w_in, shift_mu=v_shift_mu, w_decay0=v_w_decay0, w_decay_up=v_w_decay_up, a0=v_a0, a_up=v_a_up,
             k_k=v_k_k, k_a=v_k_a, r_k=v_r_k, ln_x_w=v_ln_x_w, ln_x_b=v_ln_x_b, v_mix0=v_v_mix0,
             v_mix_down=v_v_mix_down, v_mix_up=v_v_mix_up, lb_logits=v_lb_logits, g_norm_w=v_g_norm_w, w_out=v_w_out,
             ln_w=v_ln_w, ln_b=v_ln_b)
    mx, my, mc = _mesh_pos()
    kc = (2 * mx + my).astype(jnp.int32)
    c_arr = jnp.reshape(mc.astype(jnp.int32), (1,))
    kc_arr = jnp.reshape(kc, (1,))

    small_shard = _pack([W[n] for n, _ in CHIP_SMALL])
    g_in, g_out, g_small = _gather_shards([w_in.astype(bf16), w_out.astype(bf16), small_shard], [1, 1, 0])
    w_in_full = jnp.transpose(g_in, (1, 2, 0, 3)).reshape(DEPTH, D_MODEL, IN_COLS)
    w_out_full = jnp.transpose(g_out, (1, 0, 2, 3)).reshape(DEPTH, D_MODEL, D_MODEL)
    shard_shapes = [W[n].shape for n, _ in CHIP_SMALL]
    per_chip = [_unpack(g_small[k], shard_shapes) for k in range(4)]
    full_small = {n: jnp.concatenate([per_chip[k][i] for k in range(4)], axis=ax)
                  for i, (n, ax) in enumerate(CHIP_SMALL)}

    loss_sum, gx, G = _device_step(
        x[0], loss_target[0], w_in_full, w_out_full, shift_mu, w_decay0, full_small["w_decay_up"], a0,
        full_small["a_up"], k_k, k_a, r_k, ln_x_w, ln_x_b, v_mix0, full_small["v_mix_down"], full_small["v_mix_up"],
        lb_logits, g_norm_w, ln_w, ln_b)
    loss = lax.psum(loss_sum[0, 0], ("x", "y", "c"))

    gin_slabs = jnp.transpose(G["w_in"].reshape(DEPTH, D_MODEL, 4, SHARD_COLS), (2, 0, 1, 3))
    gout_slabs = jnp.transpose(G["w_out"].reshape(DEPTH, 4, D_MODEL // 4, D_MODEL), (1, 0, 2, 3))
    grad_w_in, grad_w_out = _reduce_scatter([gin_slabs, gout_slabs], c_arr, kc_arr)

    small_names = list(REPLICATED) + [n for n, _ in CHIP_SMALL]
    small_sum = _allreduce_small(_pack([G[n] for n in small_names]))
    small_full = dict(zip(small_names, _unpack(small_sum, [G[n].shape for n in small_names])))
    grads = {n: small_full[n] for n in REPLICATED}
    for n, ax in CHIP_SMALL:
        width = W[n].shape[ax]
        grads[n] = lax.dynamic_slice_in_dim(small_full[n], kc * width, width, axis=ax)
    grads["w_in"], grads["w_out"] = grad_w_in, grad_w_out

    delta, new_m, new_v = {}, {}, {}
    for n, cols in (("w_in", SHARD_COLS), ("w_out", D_MODEL)):
        two = lambda a, cols=cols: a.reshape(-1, cols)
        d_, m_, v_ = _adamw(two(W[n]), two(grads[n]), two(M[n]), two(V[n]), f"adamw_{n}")
        delta[n], new_m[n], new_v[n] = d_.reshape(W[n].shape), m_.reshape(W[n].shape), v_.reshape(W[n].shape)
    shapes = [W[n].shape for n in small_names]
    d_, m_, v_ = _adamw(_pack([W[n] for n in small_names]), _pack([grads[n] for n in small_names]),
                        _pack([M[n] for n in small_names]), _pack([V[n] for n in small_names]), "adamw_small")
    for n, d1, m1, v1 in zip(small_names, _unpack(d_, shapes), _unpack(m_, shapes), _unpack(v_, shapes)):
        delta[n], new_m[n], new_v[n] = d1, m1, v1

    return (loss, gx[None], *[grads[n] for n in WEIGHTS], *[delta[n] for n in WEIGHTS],
            *[new_m[n] for n in WEIGHTS], *[new_v[n] for n in WEIGHTS])
```

```python
import functools
import math

import jax
import jax.numpy as jnp
from jax import lax
from jax.experimental import pallas as pl
from jax.experimental.pallas import tpu as pltpu

f32 = jnp.float32
bf16 = jnp.bfloat16
SDS = jax.ShapeDtypeStruct
MESH = pl.DeviceIdType.MESH

D_MODEL = 2048
DEPTH = 2
D_RWKV = 1024
D_HGRN = 1024
RWKV_HEAD = 64
RWKV_HEADS = 16
LORA = 64
VRES = 32
HG_HEAD = 128
HG_HEADS = 8
CHUNK = 64
RWKV_COLS = 4 * D_RWKV + 2 * LORA
IN_COLS = RWKV_COLS + 4 * D_HGRN
SHARD_COLS = IN_COLS // 4
LANES = 128
RW_BLKS = RWKV_COLS // LANES
ALPHA = (2 * DEPTH) ** 0.25
LN_EPS = 1e-5
GN_EPS = 64e-5
RMS_EPS = 1e-5
LB_FLOOR = 1e-30
EXP_M05 = math.exp(-0.5)
ADAM_LR, ADAM_B1, ADAM_B2, ADAM_EPS, ADAM_WD, ADAM_STEP = 0.001, 0.9, 0.999, 1e-08, 0.01, 10

TB = 128
SCAN_TB = 8
SCAN_PARTS = 2
HSUB = 32
HIGHEST = lax.Precision.HIGHEST


def _params(sem=None, vmem_mb=None):
    kw = {}
    if sem is not None:
        kw["dimension_semantics"] = sem
    if vmem_mb is not None:
        kw["vmem_limit_bytes"] = vmem_mb << 20
    return pltpu.CompilerParams(**kw)


def _split(x, n):
    parts, r = [], x
    for i in range(n):
        p = r.astype(bf16)
        parts.append(p)
        if i + 1 < n:
            r = r - p.astype(f32)
    return parts


def _dot_parts(x, m, n=3):
    R = x.shape[0]
    res = jnp.dot(jnp.concatenate(_split(x, n), axis=0), m, preferred_element_type=f32)
    acc = res[0:R]
    for i in range(1, n):
        acc = acc + res[i * R:(i + 1) * R]
    return acc


def _seg_raw(x, ee, n=3, per_tile=False):
    R = x.shape[0]
    if per_tile:
        outs = []
        for c in range(4):
            tile_parts = _split(x[:, 256 * c:256 * c + 256], n)
            res = jnp.dot(jnp.concatenate(tile_parts, axis=0), ee, preferred_element_type=f32)
            acc = res[0:R]
            for i in range(1, n):
                acc = acc + res[i * R:(i + 1) * R]
            outs.append(acc)
        return jnp.concatenate(outs, axis=1)
    parts = _split(x, n)
    lhs = jnp.concatenate([p[:, 256 * c:256 * c + 256] for p in parts for c in range(4)], axis=0)
    res = jnp.dot(lhs, ee, preferred_element_type=f32)

    def tile(c):
        acc = res[c * R:(c + 1) * R]
        for i in range(1, n):
            acc = acc + res[(4 * i + c) * R:(4 * i + c + 1) * R]
        return acc

    return jnp.concatenate([tile(c) for c in range(4)], axis=1)


@jax.custom_vjp
def _seg(x, ee):
    return _seg_raw(x, ee)


def _seg_fwd(x, ee):
    return _seg_raw(x, ee), ee


def _seg_bwd(ee, g):
    return _seg_raw(g, ee), jnp.zeros_like(ee)


_seg.defvjp(_seg_fwd, _seg_bwd)


def _const_mats():
    i256 = jnp.arange(256) // RWKV_HEAD
    ee = (i256[:, None] == i256[None, :]).astype(bf16)
    head = jnp.arange(D_RWKV) // RWKV_HEAD
    lane = jnp.arange(LANES)
    e16 = (head[:, None] == lane[None, :]).astype(bf16)
    e16t3 = ((lane[:, None] % RWKV_HEADS == head[None, :]) & (lane[:, None] < 3 * RWKV_HEADS)).astype(bf16)
    return ee, e16, e16t3


def _matmul(a, b, *, mode, tm, tn, tk, name, add=None, b_layer=None):
    bs = b.shape if b_layer is None else b.shape[1:]
    lead = () if b_layer is None else (None,)
    bidx = (lambda *t: t) if b_layer is None else (lambda *t: (b_layer,) + t)
    if mode == "nn":
        (M, K), N = a.shape, bs[1]
        tm, tn, tk = min(tm, M), min(tn, N), min(tk, K)
        a_spec = pl.BlockSpec((tm, tk), lambda i, j, k: (i, k))
        b_spec = pl.BlockSpec(lead + (tk, tn), lambda i, j, k: bidx(k, j))
        dims = (((1,), (0,)), ((), ()))
    elif mode == "nt":
        (M, K), N = a.shape, bs[0]
        tm, tn, tk = min(tm, M), min(tn, N), min(tk, K)
        a_spec = pl.BlockSpec((tm, tk), lambda i, j, k: (i, k))
        b_spec = pl.BlockSpec(lead + (tn, tk), lambda i, j, k: bidx(j, k))
        dims = (((1,), (1,)), ((), ()))
    else:
        (K, M), N = a.shape, bs[1]
        tm, tn, tk = min(tm, M), min(tn, N), min(tk, K)
        a_spec = pl.BlockSpec((tk, tm), lambda i, j, k: (k, i))
        b_spec = pl.BlockSpec(lead + (tk, tn), lambda i, j, k: bidx(k, j))
        dims = (((0,), (0,)), ((), ()))
    assert M % tm == 0 and N % tn == 0 and K % tk == 0, (M, N, K, tm, tn, tk)
    has_add = add is not None

    def body(*refs):
        if has_add:
            a_ref, b_ref, c_ref, o_ref = refs
        else:
            a_ref, b_ref, o_ref = refs
        k = pl.program_id(2)
        p = lax.dot_general(a_ref[...].astype(bf16), b_ref[...].astype(bf16), dims, preferred_element_type=f32)

        @pl.when(k == 0)
        def _():
            o_ref[...] = p + c_ref[...] if has_add else p

        @pl.when(k > 0)
        def _():
            o_ref[...] += p

    o_spec = pl.BlockSpec((tm, tn), lambda i, j, k: (i, j))
    in_specs = [a_spec, b_spec] + ([o_spec] if has_add else [])
    args = (a, b) + ((add,) if has_add else ())
    return pl.pallas_call(
        body, name=name, grid=(M // tm, N // tn, K // tk), in_specs=in_specs, out_specs=o_spec,
        out_shape=SDS((M, N), f32),
        compiler_params=_params(("parallel", "parallel", "arbitrary"), 56),
    )(*args)


def _rwkv_core(k_in, v_in, wdad, w0, wup, a0, aup, kkw, kaw, ee, vmix):
    w_raw = w0 + jnp.dot(jnp.tanh(wdad), wup, preferred_element_type=f32)
    decay = jnp.exp(-EXP_M05 * jax.nn.sigmoid(w_raw))
    a = jax.nn.sigmoid(a0 + jnp.dot(wdad, aup, preferred_element_type=f32))
    if vmix is None:
        v = v_in
    else:
        v0, vdn, vup, vfirst = vmix
        gate = jax.nn.sigmoid(v0 + jnp.dot(jnp.dot(v_in, vdn, preferred_element_type=f32), vup,
                                           preferred_element_type=f32))
        v = v_in + (vfirst - v_in) * gate
    kk = k_in * kkw
    n2 = _seg(kk * kk, ee)
    kk = kk / jnp.maximum(jnp.sqrt(n2), 1e-12)
    k2 = k_in * (1.0 + (a - 1.0) * kaw)
    return decay, k2, v, -kk, kk * a


def _shifted(y, prev_row):
    yp = pltpu.roll(y, 1, 0)
    row = lax.broadcasted_iota(jnp.int32, y.shape, 0)
    return jnp.where(row == 0, prev_row, yp)


def _rwkv_pre_fwd(proj, mu, w0, wup, a0, aup, kkw, kaw, ee, vmix, layer):
    T = proj.shape[0]
    nb = T // TB
    has_mix = vmix is not None

    def body(*refs):
        y_ref, p8_ref, mu_ref, w0_ref, wup_ref, a0_ref, aup_ref, kkw_ref, kaw_ref, ee_ref = refs[:10]
        rest = refs[10:]
        if has_mix:
            v0_ref, vdn_ref, vup_ref, vf_ref = rest[:4]
            rest = rest[4:]
        r_o, w_o, k_o, v_o, a_o, b_o, z_o = rest
        i = pl.program_id(0)
        y = y_ref[...]
        prev = jnp.where(i == 0, 0.0, p8_ref[7:8, :])
        rw = y + mu_ref[...] * (_shifted(y, prev) - y)
        mix = (v0_ref[...], vdn_ref[...], vup_ref[...], vf_ref[...]) if has_mix else None
        dec, k2, v, av, bv = _rwkv_core(rw[:, 1024:2048], rw[:, 2048:3072], rw[:, 4096:4224], w0_ref[...],
                                        wup_ref[...], a0_ref[...], aup_ref[...], kkw_ref[...], kaw_ref[...],
                                        ee_ref[...], mix)
        r_o[...] = rw[:, 0:1024]
        w_o[...] = dec
        k_o[...] = k2
        v_o[...] = v
        a_o[...] = av
        b_o[...] = bv
        z_o[...] = rw[:, 3072:4096]

    row = lambda n: pl.BlockSpec((1, n), lambda i: (0, 0))
    full = lambda s: pl.BlockSpec(s, lambda i: (0,) * len(s))
    blk = pl.BlockSpec((TB, D_RWKV), lambda i: (i, 0))
    in_specs = [pl.BlockSpec((TB, RWKV_COLS), lambda i: (i, 0)),
                pl.BlockSpec((8, RWKV_COLS), lambda i: (jnp.maximum(i * (TB // 8) - 1, 0), 0)),
                row(RWKV_COLS), row(D_RWKV), full((LANES, D_RWKV)), row(D_RWKV), full((LANES, D_RWKV)),
                row(D_RWKV), row(D_RWKV), full((256, 256))]
    args = [proj, proj, mu, w0, wup, a0, aup, kkw, kaw, ee]
    if has_mix:
        v0, vdn, vup, vfirst = vmix
        in_specs += [row(D_RWKV), full((D_RWKV, LANES)), full((LANES, D_RWKV)), blk]
        args += [v0, vdn, vup, vfirst]
    return pl.pallas_call(
        body, name=f"rwkv_pre_fwd_l{layer}", grid=(nb,), in_specs=in_specs, out_specs=[blk] * 7,
        out_shape=[SDS((T, D_RWKV), f32)] * 7, compiler_params=_params(("parallel",), 48),
    )(*args)


def _rwkv_pre_bwd(proj, mu, w0, wup, a0, aup, kkw, kaw, ee, vmix, cots, layer):
    T = proj.shape[0]
    nb = T // TB
    has_mix = vmix is not None
    cot_list = cots["r"] + [cots["w"]] + cots["k"] + cots["v"] + [cots["a"], cots["b"], cots["z"]]
    n_r, n_k, n_v = len(cots["r"]), len(cots["k"]), len(cots["v"])
    n_cot = len(cot_list)

    def body(*refs):
        y_ref, p8_ref, mu_ref, w0_ref, wup_ref, a0_ref, aup_ref, kkw_ref, kaw_ref, ee_ref = refs[:10]
        rest = refs[10:]
        if has_mix:
            v0_ref, vdn_ref, vup_ref, vf_ref = rest[:4]
            rest = rest[4:]
        cot_refs, rest = rest[:n_cot], rest[n_cot:]
        if has_mix:
            (dproj_o, dmu_o, dw0_o, dwup_o, da0_o, daup_o, dkkw_o, dkaw_o,
             dv0_o, dvdn_o, dvup_o, dvf_o, carry) = rest
        else:
            dproj_o, dmu_o, dw0_o, dwup_o, da0_o, daup_o, dkkw_o, dkaw_o, carry = rest
        i = pl.program_id(0)
        blk_i = nb - 1 - i
        y = y_ref[...]
        prev = jnp.where(blk_i == 0, 0.0, p8_ref[7:8, :])
        yp = _shifted(y, prev)
        mu_v = mu_ref[...]
        rw = y + mu_v * (yp - y)

        def sum_refs(rs):
            acc = rs[0][...]
            for r_ in rs[1:]:
                acc = acc + r_[...]
            return acc

        c = list(cot_refs)
        g_r = sum_refs(c[:n_r]); c = c[n_r:]
        g_w = c[0][...]; c = c[1:]
        g_k = sum_refs(c[:n_k]); c = c[n_k:]
        g_v = sum_refs(c[:n_v]); c = c[n_v:]
        g_a, g_b, g_z = c[0][...], c[1][...], c[2][...]

        ee_v = ee_ref[...]
        if has_mix:
            def fn(k_in, v_in, wdad, w0_, wup_, a0_, aup_, kkw_, kaw_, v0_, vdn_, vup_, vf_):
                return _rwkv_core(k_in, v_in, wdad, w0_, wup_, a0_, aup_, kkw_, kaw_, ee_v, (v0_, vdn_, vup_, vf_))
            prim = (rw[:, 1024:2048], rw[:, 2048:3072], rw[:, 4096:4224], w0_ref[...], wup_ref[...], a0_ref[...],
                    aup_ref[...], kkw_ref[...], kaw_ref[...], v0_ref[...], vdn_ref[...], vup_ref[...], vf_ref[...])
        else:
            def fn(k_in, v_in, wdad, w0_, wup_, a0_, aup_, kkw_, kaw_):
                return _rwkv_core(k_in, v_in, wdad, w0_, wup_, a0_, aup_, kkw_, kaw_, ee_v, None)
            prim = (rw[:, 1024:2048], rw[:, 2048:3072], rw[:, 4096:4224], w0_ref[...], wup_ref[...], a0_ref[...],
                    aup_ref[...], kkw_ref[...], kaw_ref[...])
        _, vjp = jax.vjp(fn, *prim)
        gs = vjp((g_w, g_k, g_v, g_a, g_b))
        d_k, d_v, d_wdad = gs[0], gs[1], gs[2]
        g_rw = jnp.concatenate([g_r, d_k, d_v, g_z, d_wdad], axis=1)

        @pl.when(i == 0)
        def _():
            carry[...] = jnp.zeros_like(carry)
            dmu_o[...] = jnp.zeros_like(dmu_o)
            dw0_o[...] = jnp.zeros_like(dw0_o)
            dwup_o[...] = jnp.zeros_like(dwup_o)
            da0_o[...] = jnp.zeros_like(da0_o)
            daup_o[...] = jnp.zeros_like(daup_o)
            dkkw_o[...] = jnp.zeros_like(dkkw_o)
            dkaw_o[...] = jnp.zeros_like(dkaw_o)
            if has_mix:
                dv0_o[...] = jnp.zeros_like(dv0_o)
                dvdn_o[...] = jnp.zeros_like(dvdn_o)
                dvup_o[...] = jnp.zeros_like(dvup_o)

        dmu_o[...] += jnp.sum(g_rw * (yp - y), axis=0, keepdims=True)
        dw0_o[...] += gs[3]
        dwup_o[...] += gs[4]
        da0_o[...] += gs[5]
        daup_o[...] += gs[6]
        dkkw_o[...] += gs[7]
        dkaw_o[...] += gs[8]
        if has_mix:
            dv0_o[...] += gs[9]
            dvdn_o[...] += gs[10]
            dvup_o[...] += gs[11]
            dvf_o[...] = gs[12]
        gm = g_rw * mu_v
        nxt = pltpu.roll(gm, TB - 1, 0)
        rowi = lax.broadcasted_iota(jnp.int32, gm.shape, 0)
        nxt = jnp.where(rowi == TB - 1, carry[...], nxt)
        dproj_o[...] = g_rw - gm + nxt
        carry[...] = gm[0:1, :]

    rev = lambda i: (nb - 1 - i, 0)
    row = lambda n: pl.BlockSpec((1, n), lambda i: (0, 0))
    full = lambda s: pl.BlockSpec(s, lambda i: (0,) * len(s))
    blk = pl.BlockSpec((TB, D_RWKV), rev)
    in_specs = [pl.BlockSpec((TB, RWKV_COLS), rev),
                pl.BlockSpec((8, RWKV_COLS), lambda i: (jnp.maximum((nb - 1 - i) * (TB // 8) - 1, 0), 0)),
                row(RWKV_COLS), row(D_RWKV), full((LANES, D_RWKV)), row(D_RWKV), full((LANES, D_RWKV)),
                row(D_RWKV), row(D_RWKV), full((256, 256))]
    args = [proj, proj, mu, w0, wup, a0, aup, kkw, kaw, ee]
    out_specs = [pl.BlockSpec((TB, RWKV_COLS), rev), row(RWKV_COLS), row(D_RWKV), full((LANES, D_RWKV)),
                 row(D_RWKV), full((LANES, D_RWKV)), row(D_RWKV), row(D_RWKV)]
    out_shape = [SDS((T, RWKV_COLS), f32), SDS((1, RWKV_COLS), f32), SDS((1, D_RWKV), f32), SDS((LANES, D_RWKV), f32),
                 SDS((1, D_RWKV), f32), SDS((LANES, D_RWKV), f32), SDS((1, D_RWKV), f32), SDS((1, D_RWKV), f32)]
    if has_mix:
        v0, vdn, vup, vfirst = vmix
        in_specs += [row(D_RWKV), full((D_RWKV, LANES)), full((LANES, D_RWKV)), blk]
        args += [v0, vdn, vup, vfirst]
        out_specs += [row(D_RWKV), full((D_RWKV, LANES)), full((LANES, D_RWKV)), blk]
        out_shape += [SDS((1, D_RWKV), f32), SDS((D_RWKV, LANES), f32), SDS((LANES, D_RWKV), f32),
                      SDS((T, D_RWKV), f32)]
    in_specs += [blk] * n_cot
    args += cot_list
    return pl.pallas_call(
        body, name=f"rwkv_pre_bwd_l{layer}", grid=(nb,), in_specs=in_specs, out_specs=out_specs, out_shape=out_shape,
        scratch_shapes=[pltpu.VMEM((1, RWKV_COLS), f32)],
        compiler_params=_params(("arbitrary",), 60),
    )(*args)


def _to_col(v):
    T = v.shape[0]
    vc = jnp.swapaxes(v.reshape(T, RWKV_HEADS, RWKV_HEAD), 1, 2)
    hi = lax.reduce_precision(vc, 8, 7)
    r1 = vc - hi
    mid = lax.reduce_precision(r1, 8, 7)
    parts = jnp.concatenate([hi, mid, r1 - mid], axis=2).astype(bf16)
    return jnp.pad(parts, ((0, 0), (0, 0), (0, LANES - 3 * RWKV_HEADS)))


def _from_col(vc):
    T = vc.shape[0]
    return jnp.swapaxes(vc[:, :, :RWKV_HEADS], 1, 2).reshape(T, D_RWKV)


def _rwkv_scan_fwd(r, w, k, vcol, a, b, ee, e16, e16t, layer):
    T = r.shape[0]
    nb = T // SCAN_TB

    def body(r_ref, w_ref, k_ref, v_ref, a_ref, an_ref, b_ref, ee_ref, e16_ref, e16t_ref, o_ref, sp_ref, s_ref, sa_ref,
             vb_ref, po_ref):
        @pl.when(pl.program_id(0) == 0)
        def _():
            s_ref[...] = jnp.zeros_like(s_ref)
            sa_ref[...] = jnp.zeros_like(sa_ref)

        ee_v = ee_ref[...]
        r_b, w_b, k_b, a_b, b_b = r_ref[...], w_ref[...], k_ref[...], a_ref[...], b_ref[...]
        rowi = lax.broadcasted_iota(jnp.int32, a_b.shape, 0)
        a_nx = jnp.where(rowi == SCAN_TB - 1, an_ref[0:1, :], pltpu.roll(a_b, SCAN_TB - 1, 0))
        wa = w_b * a_nx
        dots = _seg_raw(jnp.concatenate([b_b * a_nx, k_b * a_nx], axis=0), ee_v, 3)
        beta, kappa = dots[0:SCAN_TB], dots[SCAN_TB:2 * SCAN_TB]
        vb_ref[...] = jnp.dot(v_ref[...].reshape(SCAN_TB * RWKV_HEAD, LANES), e16t_ref[...],
                              preferred_element_type=f32)

        for t in range(SCAN_TB):
            row = lambda x: x[t:t + 1, :]
            blk = pl.ds(t * RWKV_HEAD, RWKV_HEAD)
            sp = s_ref[...]
            sp_ref[t] = sp
            sa = sa_ref[...]
            vb = vb_ref[blk, :]
            ahead = _seg_raw(sp * row(wa), ee_v, SCAN_PARTS, per_tile=True)
            sn = sp * row(w_b) + sa * row(b_b) + vb * row(k_b)
            s_ref[...] = sn
            sa_ref[...] = ahead + sa * row(beta) + vb * row(kappa)
            po_ref[blk, :] = sn * row(r_b)
        o_ref[...] = _dot_parts(po_ref[...], e16_ref[...], SCAN_PARTS).reshape(SCAN_TB, RWKV_HEAD, LANES)

    rows = pl.BlockSpec((SCAN_TB, D_RWKV), lambda i: (i, 0))
    rows_next = pl.BlockSpec((SCAN_TB, D_RWKV), lambda i: (jnp.minimum(i + 1, nb - 1), 0))
    col = pl.BlockSpec((SCAN_TB, RWKV_HEAD, LANES), lambda i: (i, 0, 0))
    full = lambda s: pl.BlockSpec(s, lambda i: (0,) * len(s))
    return pl.pallas_call(
        body, name=f"rwkv_scan_fwd_l{layer}", grid=(nb,),
        in_specs=[rows, rows, rows, col, rows, rows_next, rows, full((256, 256)), full((D_RWKV, LANES)),
                  full((LANES, D_RWKV))],
        out_specs=[col, pl.BlockSpec((SCAN_TB, RWKV_HEAD, D_RWKV), lambda i: (i, 0, 0))],
        out_shape=[SDS((T, RWKV_HEAD, LANES), f32), SDS((T, RWKV_HEAD, D_RWKV), f32)],
        scratch_shapes=[pltpu.VMEM((RWKV_HEAD, D_RWKV), f32), pltpu.VMEM((RWKV_HEAD, D_RWKV), f32),
                        pltpu.VMEM((SCAN_TB * RWKV_HEAD, D_RWKV), f32),
                        pltpu.VMEM((SCAN_TB * RWKV_HEAD, D_RWKV), f32)],
        compiler_params=_params(("arbitrary",), 48),
    )(r, w, k, vcol, a, a, b, ee, e16, e16t)


def _rwkv_scan_bwd(r, w, k, vcol, a, b, sprev, docol, ee, e16, e16t, layer):
    T = r.shape[0]
    nb = T // SCAN_TB

    def body(r_ref, w_ref, k_ref, v_ref, a_ref, b_ref, sp_ref, do_ref, ee_ref, e16_ref, e16t_ref,
             dr_o, dw_o, dk_o, da_o, db_o, dv_o, ds_ref, snext_ref, sa_ref, vb_ref, dob_ref, pdv_ref):
        ee_v, e16t_v = ee_ref[...], e16t_ref[...]
        csum = lambda x: jnp.sum(x, axis=0, keepdims=True)
        nrow = SCAN_TB * RWKV_HEAD
        last = pl.ds(nrow - RWKV_HEAD, RWKV_HEAD)
        r_b, w_b, k_b, a_b, b_b = r_ref[...], w_ref[...], k_ref[...], a_ref[...], b_ref[...]
        sa_ref[...] = _seg_raw((sp_ref[...] * a_b[:, None, :]).reshape(nrow, D_RWKV), ee_v, SCAN_PARTS)
        vb_ref[...] = jnp.dot(v_ref[...].reshape(nrow, LANES), e16t_v, preferred_element_type=f32)
        dob_ref[...] = jnp.dot(do_ref[...].reshape(nrow, LANES), e16t_v, preferred_element_type=f32)

        @pl.when(pl.program_id(0) == 0)
        def _():
            ds_ref[...] = jnp.zeros_like(ds_ref)
            snext_ref[...] = (sp_ref[SCAN_TB - 1] * w_b[SCAN_TB - 1:SCAN_TB, :]
                              + sa_ref[last, :] * b_b[SCAN_TB - 1:SCAN_TB, :]
                              + vb_ref[last, :] * k_b[SCAN_TB - 1:SCAN_TB, :])

        for s in range(SCAN_TB):
            t = SCAN_TB - 1 - s
            row = lambda x: x[t:t + 1, :]
            blk = pl.ds(t * RWKV_HEAD, RWKV_HEAD)
            sp = sp_ref[t]
            sn = snext_ref[...] if s == 0 else sp_ref[t + 1]
            sa, vb, dob = sa_ref[blk, :], vb_ref[blk, :], dob_ref[blk, :]
            ds = ds_ref[...] + dob * row(r_b)
            dsa = _seg_raw(ds * row(b_b), ee_v, SCAN_PARTS, per_tile=True)
            ds_ref[...] = ds * row(w_b) + dsa * row(a_b)
            dr_o[t:t + 1, :] = csum(sn * dob)
            pdv_ref[blk, :] = ds * row(k_b)
            dk_o[t:t + 1, :] = csum(ds * vb)
            db_o[t:t + 1, :] = csum(ds * sa)
            dw_o[t:t + 1, :] = csum(ds * sp)
            da_o[t:t + 1, :] = csum(sp * dsa)
        snext_ref[...] = sp_ref[0]
        dv_o[...] = _dot_parts(pdv_ref[...], e16_ref[...], SCAN_PARTS).reshape(SCAN_TB, RWKV_HEAD, LANES)

    rows = pl.BlockSpec((SCAN_TB, D_RWKV), lambda i: (nb - 1 - i, 0))
    col = pl.BlockSpec((SCAN_TB, RWKV_HEAD, LANES), lambda i: (nb - 1 - i, 0, 0))
    st = pl.BlockSpec((SCAN_TB, RWKV_HEAD, D_RWKV), lambda i: (nb - 1 - i, 0, 0))
    full = lambda s: pl.BlockSpec(s, lambda i: (0,) * len(s))
    big = pltpu.VMEM((SCAN_TB * RWKV_HEAD, D_RWKV), f32)
    return pl.pallas_call(
        body, name=f"rwkv_scan_bwd_l{layer}", grid=(nb,),
        in_specs=[rows, rows, rows, col, rows, rows, st, col, full((256, 256)), full((D_RWKV, LANES)),
                  full((LANES, D_RWKV))],
        out_specs=[rows] * 5 + [col],
        out_shape=[SDS((T, D_RWKV), f32)] * 5 + [SDS((T, RWKV_HEAD, LANES), f32)],
        scratch_shapes=[pltpu.VMEM((RWKV_HEAD, D_RWKV), f32)] * 2 + [big] * 4,
        compiler_params=_params(("arbitrary",), 56),
    )(r, w, k, vcol, a, b, sprev, docol, ee, e16, e16t)


def _post_core(o, r, k2, v, z, gw, gb, rk, ee):
    inv = 1.0 / RWKV_HEAD
    mu = _seg(o, ee) * inv
    d = o - mu
    var = _seg(d * d, ee) * inv
    on = d * lax.rsqrt(var + GN_EPS) * gw + gb
    bonus = _seg(r * k2 * rk, ee) * v
    return (on + bonus) * jax.nn.silu(z)


def _rwkv_post_fwd(o, r, k2, v, z, gw, gb, rk, ee, layer):
    T = o.shape[0]

    def body(o_ref, r_ref, k_ref, v_ref, z_ref, gw_ref, gb_ref, rk_ref, ee_ref, out_ref):
        out_ref[...] = _post_core(o_ref[...], r_ref[...], k_ref[...], v_ref[...], z_ref[...], gw_ref[...],
                                  gb_ref[...], rk_ref[...], ee_ref[...])

    blk = pl.BlockSpec((TB, D_RWKV), lambda i: (i, 0))
    row = pl.BlockSpec((1, D_RWKV), lambda i: (0, 0))
    return pl.pallas_call(
        body, name=f"rwkv_post_fwd_l{layer}", grid=(T // TB,),
        in_specs=[blk] * 5 + [row] * 3 + [pl.BlockSpec((256, 256), lambda i: (0, 0))],
        out_specs=blk, out_shape=SDS((T, D_MODEL), f32), compiler_params=_params(("parallel",), 40),
    )(o, r, k2, v, z, gw, gb, rk, ee)


def _rwkv_post_bwd(o, r, k2, v, z, gw, gb, rk, ee, dcat, layer):
    T = o.shape[0]

    def body(o_ref, r_ref, k_ref, v_ref, z_ref, gw_ref, gb_ref, rk_ref, ee_ref, g_ref,
             do_o, dr_o, dk_o, dv_o, dz_o, dgw_o, dgb_o, drk_o):
        ee_v = ee_ref[...]
        fn = lambda o_, r_, k_, v_, z_, gw_, gb_, rk_: _post_core(o_, r_, k_, v_, z_, gw_, gb_, rk_, ee_v)
        _, vjp = jax.vjp(fn, o_ref[...], r_ref[...], k_ref[...], v_ref[...], z_ref[...], gw_ref[...], gb_ref[...],
                         rk_ref[...])
        gs = vjp(g_ref[...])
        do_o[...], dr_o[...], dk_o[...], dv_o[...], dz_o[...] = gs[:5]

        @pl.when(pl.program_id(0) == 0)
        def _():
            dgw_o[...] = jnp.zeros_like(dgw_o)
            dgb_o[...] = jnp.zeros_like(dgb_o)
            drk_o[...] = jnp.zeros_like(drk_o)

        dgw_o[...] += gs[5]
        dgb_o[...] += gs[6]
        drk_o[...] += gs[7]

    blk = pl.BlockSpec((TB, D_RWKV), lambda i: (i, 0))
    row = pl.BlockSpec((1, D_RWKV), lambda i: (0, 0))
    return pl.pallas_call(
        body, name=f"rwkv_post_bwd_l{layer}", grid=(T // TB,),
        in_specs=[blk] * 5 + [row] * 3 + [pl.BlockSpec((256, 256), lambda i: (0, 0)), blk],
        out_specs=[blk] * 5 + [row] * 3,
        out_shape=[SDS((T, D_RWKV), f32)] * 5 + [SDS((1, D_RWKV), f32)] * 3,
        compiler_params=_params(("arbitrary",), 48),
    )(o, r, k2, v, z, gw, gb, rk, ee, dcat)


def _hgrn_chunk(qr, fr, ii, z, S, lbl, gw, layer):
    L = CHUNK
    m = jnp.max(lbl, axis=0, keepdims=True)
    e = jnp.exp(lbl - m)
    sm = e / jnp.sum(e, axis=0, keepdims=True)
    if layer == 0:
        lb = sm[0:1] - sm[0:1]
    else:
        lb = (sm[0:1] + sm[1:2]) - sm[0:1]
    q = jax.nn.silu(qr)
    log_lb = jnp.log(jnp.maximum(lb, LB_FLOOR))
    lf = jnp.logaddexp(log_lb, jnp.log1p(-lb) + jax.nn.log_sigmoid(fr))
    k = (1.0 - lb) * jax.nn.sigmoid(-fr)
    ti = lax.broadcasted_iota(jnp.int32, (L, L), 0)
    si = lax.broadcasted_iota(jnp.int32, (L, L), 1)
    ltri = (si <= ti).astype(f32)
    b = jnp.dot(ltri, lf, precision=HIGHEST, preferred_element_type=f32)
    btot = jnp.dot(jnp.ones((L, L), f32), lf, precision=HIGHEST, preferred_element_type=f32)
    o = jnp.dot(q * jnp.exp(b), S, preferred_element_type=f32)
    nsub = L // HSUB
    t3 = lax.broadcasted_iota(jnp.int32, (HSUB, HSUB, HG_HEAD), 0)
    s3 = lax.broadcasted_iota(jnp.int32, (HSUB, HSUB, HG_HEAD), 1)
    causal = s3 <= t3
    rows = []
    for I in range(nsub):
        sl = slice(I * HSUB, (I + 1) * HSUB)
        bI, qI, kI, iI = b[sl], q[sl], k[sl], ii[sl]
        d3 = bI[:, None, :] - bI[None, :, :]
        dec = jnp.where(causal, jnp.exp(jnp.where(causal, d3, 0.0)), 0.0)
        att = jnp.sum(qI[:, None, :] * dec * kI[None, :, :], axis=-1)
        oI = jnp.dot(att, iI, preferred_element_type=f32)
        for J in range(I):
            sj = slice(J * HSUB, (J + 1) * HSUB)
            bm = b[(J + 1) * HSUB - 1:(J + 1) * HSUB]
            qs = qI * jnp.exp(bI - bm)
            ks = k[sj] * jnp.exp(bm - b[sj])
            att_ij = lax.dot_general(qs, ks, (((1,), (1,)), ((), ())), preferred_element_type=f32)
            oI = oI + jnp.dot(att_ij, ii[sj], preferred_element_type=f32)
        rows.append(oI)
    o = o + jnp.concatenate(rows, axis=0)
    k_dec = k * jnp.exp(btot - b)
    bcol = lax.dot_general(lf, jnp.ones((L, HG_HEAD), f32), (((0,), (0,)), ((), ())), precision=HIGHEST,
                           preferred_element_type=f32)
    s_new = S * jnp.exp(bcol) + lax.dot_general(k_dec, ii, (((0,), (0,)), ((), ())), preferred_element_type=f32)
    ms = jnp.mean(o * o, axis=-1, keepdims=True)
    out = o * lax.rsqrt(ms + RMS_EPS) * gw * jax.nn.silu(z)
    return out, s_new


def _hg_specs(nc, rev):
    cidx = (lambda c: nc - 1 - c) if rev else (lambda c: c)
    pj = lambda off: pl.BlockSpec((CHUNK, HG_HEAD), lambda h, c: (cidx(c), RW_BLKS + off + h))
    return cidx, [pj(0), pj(HG_HEADS), pj(2 * HG_HEADS), pj(3 * HG_HEADS)]


def _hgrn_fwd(proj, cat, lbl, gw, layer):
    T = proj.shape[0]
    nc = T // CHUNK
    cidx, pspecs = _hg_specs(nc, False)

    def body(q_ref, f_ref, i_ref, z_ref, lbl_ref, gw_ref, cat_in, out_ref, sh_ref, s_ref):
        del cat_in

        @pl.when(pl.program_id(1) == 0)
        def _():
            s_ref[...] = jnp.zeros_like(s_ref)

        s0 = s_ref[...]
        sh_ref[0, 0] = s0
        out, s_new = _hgrn_chunk(q_ref[...], f_ref[...], i_ref[...], z_ref[...], s0, lbl_ref[...], gw_ref[...], layer)
        out_ref[...] = out
        s_ref[...] = s_new

    return pl.pallas_call(
        body, name=f"hgrn_fwd_l{layer}", grid=(HG_HEADS, nc),
        in_specs=pspecs + [pl.BlockSpec((2, HG_HEAD), lambda h, c: (0, h)),
                           pl.BlockSpec((1, HG_HEAD), lambda h, c: (0, h)),
                           pl.BlockSpec(memory_space=pl.ANY)],
        out_specs=[pl.BlockSpec((CHUNK, HG_HEAD), lambda h, c: (c, HG_HEADS + h)),
                   pl.BlockSpec((1, 1, HG_HEAD, HG_HEAD), lambda h, c: (h, c, 0, 0))],
        out_shape=[SDS((T, D_MODEL), f32), SDS((HG_HEADS, nc, HG_HEAD, HG_HEAD), f32)],
        scratch_shapes=[pltpu.VMEM((HG_HEAD, HG_HEAD), f32)],
        input_output_aliases={6: 0},
        compiler_params=_params(("parallel", "arbitrary"), 40),
    )(proj, proj, proj, proj, lbl, gw, cat)


def _hgrn_bwd(proj, shist, dcat, lbl, gw, layer):
    T = proj.shape[0]
    nc = T // CHUNK
    cidx, pspecs = _hg_specs(nc, True)

    def body(q_ref, f_ref, i_ref, z_ref, lbl_ref, gw_ref, sh_ref, g_ref,
             dq_o, df_o, di_o, dz_o, dlbl_o, dgw_o, ds_ref):
        @pl.when(pl.program_id(1) == 0)
        def _():
            ds_ref[...] = jnp.zeros_like(ds_ref)
            dlbl_o[...] = jnp.zeros_like(dlbl_o)
            dgw_o[...] = jnp.zeros_like(dgw_o)

        fn = functools.partial(_hgrn_chunk, layer=layer)
        _, vjp = jax.vjp(fn, q_ref[...], f_ref[...], i_ref[...], z_ref[...], sh_ref[0, 0], lbl_ref[...], gw_ref[...])
        dq, df, di, dz, ds, dl, dg = vjp((g_ref[...], ds_ref[...]))
        dq_o[...], df_o[...], di_o[...], dz_o[...] = dq, df, di, dz
        ds_ref[...] = ds
        dlbl_o[...] += dl
        dgw_o[...] += dg

    ospec = pl.BlockSpec((CHUNK, HG_HEAD), lambda h, c: (cidx(c), h))
    return pl.pallas_call(
        body, name=f"hgrn_bwd_l{layer}", grid=(HG_HEADS, nc),
        in_specs=pspecs + [pl.BlockSpec((2, HG_HEAD), lambda h, c: (0, h)),
                           pl.BlockSpec((1, HG_HEAD), lambda h, c: (0, h)),
                           pl.BlockSpec((1, 1, HG_HEAD, HG_HEAD), lambda h, c: (h, cidx(c), 0, 0)),
                           pl.BlockSpec((CHUNK, HG_HEAD), lambda h, c: (cidx(c), HG_HEADS + h))],
        out_specs=[ospec] * 4 + [pl.BlockSpec((2, HG_HEAD), lambda h, c: (0, h)),
                                 pl.BlockSpec((1, HG_HEAD), lambda h, c: (0, h))],
        out_shape=[SDS((T, D_HGRN), f32)] * 4 + [SDS((2, D_HGRN), f32), SDS((1, D_HGRN), f32)],
        scratch_shapes=[pltpu.VMEM((HG_HEAD, HG_HEAD), f32)],
        compiler_params=_params(("parallel", "arbitrary"), 48),
    )(proj, proj, proj, proj, lbl, gw, shist, dcat)


def _ln_core(h, y, w, b):
    u = ALPHA * h + y
    mu = jnp.mean(u, axis=-1, keepdims=True)
    d = u - mu
    var = jnp.mean(d * d, axis=-1, keepdims=True)
    return d * lax.rsqrt(var + LN_EPS) * w + b


def _ln_fwd(h, y, w, b, layer):
    T = h.shape[0]

    def body(h_ref, y_ref, w_ref, b_ref, o_ref):
        o_ref[...] = _ln_core(h_ref[...], y_ref[...], w_ref[...], b_ref[...])

    blk = pl.BlockSpec((TB, D_MODEL), lambda i: (i, 0))
    row = pl.BlockSpec((1, D_MODEL), lambda i: (0, 0))
    return pl.pallas_call(body, name=f"ln_fwd_l{layer}", grid=(T // TB,), in_specs=[blk, blk, row, row],
                          out_specs=blk, out_shape=SDS((T, D_MODEL), f32),
                          compiler_params=_params(("parallel",), 40))(h, y, w, b)


def _ln_bwd(h, y, w, b, g, layer):
    T = h.shape[0]

    def body(h_ref, y_ref, w_ref, b_ref, g_ref, dh_o, dy_o, dw_o, db_o):
        _, vjp = jax.vjp(_ln_core, h_ref[...], y_ref[...], w_ref[...], b_ref[...])
        dh, dy, dw, db = vjp(g_ref[...])
        dh_o[...] = dh
        dy_o[...] = dy

        @pl.when(pl.program_id(0) == 0)
        def _():
            dw_o[...] = jnp.zeros_like(dw_o)
            db_o[...] = jnp.zeros_like(db_o)

        dw_o[...] += dw
        db_o[...] += db

    blk = pl.BlockSpec((TB, D_MODEL), lambda i: (i, 0))
    row = pl.BlockSpec((1, D_MODEL), lambda i: (0, 0))
    return pl.pallas_call(body, name=f"ln_bwd_l{layer}", grid=(T // TB,), in_specs=[blk, blk, row, row, blk],
                          out_specs=[blk, blk, row, row],
                          out_shape=[SDS((T, D_MODEL), f32)] * 2 + [SDS((1, D_MODEL), f32)] * 2,
                          compiler_params=_params(("arbitrary",), 48))(h, y, w, b, g)


def _loss_head(hout, target):
    T = hout.shape[0]

    def body(h_ref, t_ref, loss_o, g_o):
        err = h_ref[...] - t_ref[...]
        g_o[...] = err * (1.0 / D_MODEL)

        @pl.when(pl.program_id(0) == 0)
        def _():
            loss_o[...] = jnp.zeros_like(loss_o)

        part = jnp.sum(jnp.sum(err * err, axis=-1, keepdims=True) * (1.0 / D_MODEL), axis=0, keepdims=True)
        loss_o[...] += 0.5 * part

    blk = pl.BlockSpec((TB, D_MODEL), lambda i: (i, 0))
    return pl.pallas_call(body, name="loss_head", grid=(T // TB,), in_specs=[blk, blk],
                          out_specs=[pl.BlockSpec((1, 1), lambda i: (0, 0)), blk],
                          out_shape=[SDS((1, 1), f32), SDS((T, D_MODEL), f32)],
                          compiler_params=_params(("arbitrary",), 40))(hout, target)


def _pad_rows(m, lo, total):
    return jnp.pad(m, ((0, 0), (lo, total - lo - m.shape[1]), (0, 0)))


def _device_step(x, target, w_in, w_out, shift_mu, w_decay0, w_decay_up, a0, a_up, k_k, k_a, r_k, ln_x_w, ln_x_b,
                 v_mix0, v_mix_down, v_mix_up, lb_logits, g_norm_w, ln_w, ln_b):
    T = x.shape[0]
    ee, e16, e16t = _const_mats()
    wup_pad = _pad_rows(w_decay_up, 0, LANES)
    aup_pad = _pad_rows(a_up, LORA, LANES)
    vdn_pad = jnp.pad(v_mix_down, ((0, 0), (0, 0), (0, LANES - VRES)))
    vup_pad = _pad_rows(v_mix_up, 0, LANES)
    row = lambda p, l: p[l][None, :]

    def vmix_of(l, vfirst):
        if l == 0:
            return None
        return (row(v_mix0, l - 1), vdn_pad[l - 1], vup_pad[l - 1], vfirst)

    h = x
    saved = []
    vfirst = None
    for l in range(DEPTH):
        proj = _matmul(h, w_in, b_layer=l, mode="nn", tm=512, tn=1664, tk=D_MODEL, name=f"proj_fwd_l{l}")
        pre_args = (proj, row(shift_mu, l), row(w_decay0, l), wup_pad[l], row(a0, l), aup_pad[l], row(k_k, l),
                    row(k_a, l), ee, vmix_of(l, vfirst))
        r, w, k2, v, av, bv, z = _rwkv_pre_fwd(*pre_args, layer=l)
        if l == 0:
            vfirst = v
        vcol = _to_col(v)
        ocol, sprev = _rwkv_scan_fwd(r, w, k2, vcol, av, bv, ee, e16, e16t, l)
        o = _from_col(ocol)
        post_args = (o, r, k2, v, z, row(ln_x_w, l), row(ln_x_b, l), row(r_k, l), ee)
        cat = _rwkv_post_fwd(*post_args, layer=l)
        cat, shist = _hgrn_fwd(proj, cat, lb_logits, row(g_norm_w, l), l)
        y = _matmul(cat, w_out, b_layer=l, mode="nn", tm=512, tn=1024, tk=D_MODEL, name=f"out_fwd_l{l}")
        h_new = _ln_fwd(h, y, row(ln_w, l), row(ln_b, l), l)
        saved.append(dict(h=h, proj=proj, pre_args=pre_args, post_args=post_args, scan=(r, w, k2, vcol, av, bv, sprev),
                          cat=cat, shist=shist, y=y))
        h = h_new

    loss_sum, g = _loss_head(h, target)

    grads = {n: [None] * DEPTH for n in ("w_in", "w_out", "shift_mu", "w_decay0", "w_decay_up", "a0", "a_up", "k_k",
                                         "k_a", "r_k", "ln_x_w", "ln_x_b", "g_norm_w", "ln_w", "ln_b")}
    dlbl = []
    dvfirst = None
    for l in reversed(range(DEPTH)):
        s = saved[l]
        dh_res, dy, dlnw, dlnb = _ln_bwd(s["h"], s["y"], row(ln_w, l), row(ln_b, l), g, l)
        grads["ln_w"][l], grads["ln_b"][l] = dlnw[0], dlnb[0]
        dcat = _matmul(dy, w_out, b_layer=l, mode="nt", tm=512, tn=1024, tk=D_MODEL, name=f"out_bwd_dx_l{l}")
        grads["w_out"][l] = _matmul(s["cat"], dy, mode="tn", tm=1024, tn=1024, tk=512, name=f"out_bwd_dw_l{l}")
        do, dr_p, dk_p, dv_p, dz, dgw, dgb, drk = _rwkv_post_bwd(*s["post_args"], dcat, layer=l)
        grads["ln_x_w"][l], grads["ln_x_b"][l], grads["r_k"][l] = dgw[0], dgb[0], drk[0]
        r, w, k2, vcol, av, bv, sprev = s["scan"]
        dr_s, dw_s, dk_s, da_s, db_s, dvcol = _rwkv_scan_bwd(r, w, k2, vcol, av, bv, sprev, _to_col(do), ee, e16,
                                                            e16t, l)
        v_cots = [dv_p, _from_col(dvcol)] + ([dvfirst] if (l == 0 and dvfirst is not None) else [])
        cots = dict(r=[dr_p, dr_s], w=dw_s, k=[dk_p, dk_s], v=v_cots, a=da_s, b=db_s, z=dz)
        outs = _rwkv_pre_bwd(*s["pre_args"], cots, layer=l)
        dproj, dmu, dw0, dwup, da0, daup, dkkw, dkaw = outs[:8]
        grads["shift_mu"][l], grads["w_decay0"][l], grads["a0"][l] = dmu[0], dw0[0], da0[0]
        grads["k_k"][l], grads["k_a"][l] = dkkw[0], dkaw[0]
        grads["w_decay_up"][l], grads["a_up"][l] = dwup[:LORA], daup[LORA:]
        if l > 0:
            dv0, dvdn, dvup, dvfirst = outs[8:]
            g_vmix = (dv0, dvdn[:, :VRES][None], dvup[:VRES][None])
        dq, df, di, dzh, dl, dgn = _hgrn_bwd(s["proj"], s["shist"], dcat, lb_logits, row(g_norm_w, l), l)
        dlbl.append(dl)
        grads["g_norm_w"][l] = dgn[0]
        dproj = jnp.concatenate([dproj, dq, df, di, dzh], axis=1)
        grads["w_in"][l] = _matmul(s["h"], dproj, mode="tn", tm=1024, tn=1664, tk=512, name=f"proj_bwd_dw_l{l}")
        g = _matmul(dproj, w_in, b_layer=l, mode="nt", tm=512, tn=1024, tk=1664, name=f"proj_bwd_dx_l{l}",
                    add=dh_res)
    out = {n: jnp.stack(v) for n, v in grads.items()}
    out["lb_logits"] = dlbl[0] + dlbl[1]
    out["v_mix0"], out["v_mix_down"], out["v_mix_up"] = g_vmix
    return loss_sum, g, out


CHIP_COMBOS = ((1, 0), (0, 1), (1, 1))
HBM_SPEC = pl.BlockSpec(memory_space=pl.ANY)


def _mesh_pos():
    return lax.axis_index("x"), lax.axis_index("y"), lax.axis_index("c")


def _flip(v, d):
    return 1 - v if d else v


def _half(ref, dim, cc, hs):
    idx = [slice(None)] * len(ref.shape)
    idx[dim] = pl.ds(cc * hs, hs)
    return ref.at[tuple(idx)]


def _gather_shards(arrs):
    n = len(arrs)

    def body(*refs):
        ins, outs = refs[:n], refs[n:2 * n]
        send_sems, recv_sems, local_sems = refs[2 * n:]
        x, y, c = _mesh_pos()
        kc = 2 * x + y
        local, sends = [], []
        for a in range(n):
            loc = pltpu.make_async_copy(ins[a], outs[a].at[kc], local_sems.at[a])
            loc.start()
            local.append(loc)
            for j, (dx, dy) in enumerate(CHIP_COMBOS):
                cp = pltpu.make_async_remote_copy(
                    src_ref=ins[a].at[c], dst_ref=outs[a].at[kc, c],
                    send_sem=send_sems.at[a, j], recv_sem=recv_sems.at[a, j],
                    device_id=(_flip(x, dx), _flip(y, dy), c), device_id_type=MESH)
                cp.start()
                sends.append(cp)
        for a in range(n):
            for j, (dx, dy) in enumerate(CHIP_COMBOS):
                kj = 2 * _flip(x, dx) + _flip(y, dy)
                landed = outs[a].at[kj, c]
                pltpu.make_async_remote_copy(
                    src_ref=landed, dst_ref=landed, send_sem=send_sems.at[a, j], recv_sem=recv_sems.at[a, j],
                    device_id=(x, y, c), device_id_type=MESH).wait_recv()
                fw = pltpu.make_async_remote_copy(
                    src_ref=landed, dst_ref=landed, send_sem=send_sems.at[a, 3 + j], recv_sem=recv_sems.at[a, 3 + j],
                    device_id=(x, y, 1 - c), device_id_type=MESH)
                fw.start()
                sends.append(fw)
        for a in range(n):
            for j, (dx, dy) in enumerate(CHIP_COMBOS):
                kj = 2 * _flip(x, dx) + _flip(y, dy)
                other = outs[a].at[kj, 1 - c]
                pltpu.make_async_remote_copy(
                    src_ref=other, dst_ref=other, send_sem=send_sems.at[a, 3 + j], recv_sem=recv_sems.at[a, 3 + j],
                    device_id=(x, y, 1 - c), device_id_type=MESH).wait_recv()
        for cp in sends:
            cp.wait_send()
        for loc in local:
            loc.wait()

    return pl.pallas_call(
        body, name="gather_shards", in_specs=[HBM_SPEC] * n, out_specs=[HBM_SPEC] * n,
        out_shape=[SDS((4,) + a.shape, a.dtype) for a in arrs],
        scratch_shapes=[pltpu.SemaphoreType.DMA((n, 6)), pltpu.SemaphoreType.DMA((n, 6)),
                        pltpu.SemaphoreType.DMA((n,))],
    )(*arrs)


DEV_COMBOS = tuple((dx, dy, dc) for dx in (0, 1) for dy in (0, 1) for dc in (0, 1))[1:]


def _allreduce_small(buf):
    R = buf.shape[0]

    def body(x_ref, o_ref, slots, send_sems, recv_sems):
        x, y, c = _mesh_pos()
        me = 4 * x + 2 * y + c
        cps = []
        for k, (dx, dy, dc) in enumerate(DEV_COMBOS):
            cp = pltpu.make_async_remote_copy(
                src_ref=x_ref, dst_ref=slots.at[k + 1], send_sem=send_sems.at[k], recv_sem=recv_sems.at[k],
                device_id=(_flip(x, dx), _flip(y, dy), _flip(c, dc)), device_id_type=MESH)
            cp.start()
            cps.append(cp)
        slots[0] = x_ref[...]
        for cp in cps:
            cp.wait_recv()
        acc = slots[jnp.bitwise_xor(me, 0)]
        for d in range(1, 8):
            acc = acc + slots[jnp.bitwise_xor(me, d)]
        o_ref[...] = acc
        for cp in cps:
            cp.wait_send()

    vm = pl.BlockSpec(memory_space=pltpu.VMEM)
    return pl.pallas_call(
        body, name="allreduce_small", in_specs=[vm], out_specs=vm, out_shape=SDS((R, LANES), f32),
        scratch_shapes=[pltpu.VMEM((8, R, LANES), f32), pltpu.SemaphoreType.DMA((7,)), pltpu.SemaphoreType.DMA((7,))],
        compiler_params=_params(None, 48),
    )(buf)


def _swap_halves(gs):
    n = len(gs)

    def body(*refs):
        ins, outs = refs[:n], refs[n:2 * n]
        send_sems, recv_sems = refs[2 * n:]
        x, y, c = _mesh_pos()
        cps = []
        for a in range(n):
            h = ins[a].shape[1] // 2
            cp = pltpu.make_async_remote_copy(
                src_ref=_half(ins[a], 1, 1 - c, h), dst_ref=outs[a], send_sem=send_sems.at[a],
                recv_sem=recv_sems.at[a], device_id=(x, y, 1 - c), device_id_type=MESH)
            cp.start()
            cps.append(cp)
        for cp in cps:
            cp.wait()

    return pl.pallas_call(
        body, name="rs_swap_halves", in_specs=[HBM_SPEC] * n, out_specs=[HBM_SPEC] * n,
        out_shape=[SDS((g.shape[0], g.shape[1] // 2, g.shape[2]), g.dtype) for g in gs],
        scratch_shapes=[pltpu.SemaphoreType.DMA((n,)), pltpu.SemaphoreType.DMA((n,))],
    )(*gs)


def _send_to_chips(ps):
    n = len(ps)

    def body(*refs):
        ins, outs = refs[:n], refs[n:2 * n]
        send_sems, recv_sems = refs[2 * n:]
        x, y, c = _mesh_pos()
        cps = []
        for a in range(n):
            for j, (dx, dy) in enumerate(CHIP_COMBOS):
                px, py = _flip(x, dx), _flip(y, dy)
                cp = pltpu.make_async_remote_copy(
                    src_ref=ins[a].at[2 * px + py], dst_ref=outs[a].at[j], send_sem=send_sems.at[a, j],
                    recv_sem=recv_sems.at[a, j], device_id=(px, py, c), device_id_type=MESH)
                cp.start()
                cps.append(cp)
        for cp in cps:
            cp.wait()

    return pl.pallas_call(
        body, name="rs_send_to_chips", in_specs=[HBM_SPEC] * n, out_specs=[HBM_SPEC] * n,
        out_shape=[SDS((3,) + p.shape[1:], p.dtype) for p in ps],
        scratch_shapes=[pltpu.SemaphoreType.DMA((n, 3)), pltpu.SemaphoreType.DMA((n, 3))],
    )(*ps)


def _share_halves(qs):
    n = len(qs)

    def body(*refs):
        ins, outs = refs[:n], refs[n:2 * n]
        send_sems, recv_sems = refs[2 * n:]
        x, y, c = _mesh_pos()
        cps = []
        for a in range(n):
            cp = pltpu.make_async_remote_copy(
                src_ref=ins[a], dst_ref=outs[a], send_sem=send_sems.at[a], recv_sem=recv_sems.at[a],
                device_id=(x, y, 1 - c), device_id_type=MESH)
            cp.start()
            cps.append(cp)
        for cp in cps:
            cp.wait()

    return pl.pallas_call(
        body, name="rs_share_halves", in_specs=[HBM_SPEC] * n, out_specs=[HBM_SPEC] * n,
        out_shape=[SDS(q.shape, q.dtype) for q in qs],
        scratch_shapes=[pltpu.SemaphoreType.DMA((n,)), pltpu.SemaphoreType.DMA((n,))],
    )(*qs)


def _add_own_half(g, recv, c, name):
    S8, R, C = g.shape
    h = R // 2
    tr = min(256, h)
    nt = h // tr

    def body(c_ref, g_ref, r_ref, o_ref):
        del c_ref
        o_ref[...] = g_ref[...] + r_ref[...]

    return pl.pallas_call(
        body, name=name,
        grid_spec=pltpu.PrefetchScalarGridSpec(
            num_scalar_prefetch=1, grid=(S8, nt),
            in_specs=[pl.BlockSpec((1, tr, C), lambda s, i, cr: (s, cr[0] * nt + i, 0)),
                      pl.BlockSpec((1, tr, C), lambda s, i, cr: (s, i, 0))],
            out_specs=pl.BlockSpec((1, tr, C), lambda s, i, cr: (s, i, 0))),
        out_shape=SDS((S8, h, C), f32), compiler_params=_params(("parallel", "parallel"), 40),
    )(c, g, recv)


def _add_chip_parts(p, recv, kc, name):
    _, L, R, C = p.shape
    tr = min(256, R)

    def body(k_ref, p_ref, r0, r1, r2, o_ref):
        del k_ref
        o_ref[...] = ((p_ref[0] + r0[0]) + r1[0]) + r2[0]

    rspec = lambda j: pl.BlockSpec((1, 1, tr, C), lambda l, i, kr: (j, l, i, 0))
    return pl.pallas_call(
        body, name=name,
        grid_spec=pltpu.PrefetchScalarGridSpec(
            num_scalar_prefetch=1, grid=(L, R // tr),
            in_specs=[pl.BlockSpec((1, 1, tr, C), lambda l, i, kr: (kr[0], l, i, 0)), rspec(0), rspec(1), rspec(2)],
            out_specs=pl.BlockSpec((1, tr, C), lambda l, i, kr: (l, i, 0))),
        out_shape=SDS((L, R, C), f32), compiler_params=_params(("parallel", "parallel"), 40),
    )(kc, p, recv, recv, recv)


def _reduce_scatter(gs, c, kc):
    flat = [g.reshape((8,) + g.shape[2:]) for g in gs]
    got = _swap_halves(flat)
    ps = [_add_own_half(f, r, c, f"rs_add_pair_{a}").reshape((4, 2) + r.shape[1:])
          for a, (f, r) in enumerate(zip(flat, got))]
    got2 = _send_to_chips(ps)
    qs = [_add_chip_parts(p, r, kc, f"rs_add_chips_{a}") for a, (p, r) in enumerate(zip(ps, got2))]
    return qs, _share_halves(qs)


def _adamw_math(w, g, m, v):
    m2 = ADAM_B1 * m + (1.0 - ADAM_B1) * g
    v2 = ADAM_B2 * v + (1.0 - ADAM_B2) * (g * g)
    m_hat = m2 / (1.0 - ADAM_B1 ** ADAM_STEP)
    v_hat = v2 / (1.0 - ADAM_B2 ** ADAM_STEP)
    return -ADAM_LR * (m_hat / (jnp.sqrt(v_hat) + ADAM_EPS) + ADAM_WD * w), m2, v2


def _adamw_shard(w, g_own, g_other, m, v, c, name):
    L, R, C = w.shape
    h = R // 2
    tr = min(256, h)
    nh = h // tr

    def body(c_ref, w_ref, go_ref, gx_ref, m_ref, v_ref, g_o, d_o, m_o, v_o):
        mine = (pl.program_id(1) // nh) == c_ref[0]
        g = jnp.where(mine, go_ref[...], gx_ref[...])
        d, m2, v2 = _adamw_math(w_ref[...], g, m_ref[...], v_ref[...])
        g_o[...], d_o[...], m_o[...], v_o[...] = g, d, m2, v2

    full = pl.BlockSpec((1, tr, C), lambda l, i, cr: (l, i, 0))
    half = pl.BlockSpec((1, tr, C), lambda l, i, cr: (l, i % nh, 0))
    return pl.pallas_call(
        body, name=name,
        grid_spec=pltpu.PrefetchScalarGridSpec(num_scalar_prefetch=1, grid=(L, R // tr),
                                               in_specs=[full, half, half, full, full], out_specs=[full] * 4),
        out_shape=[SDS((L, R, C), f32)] * 4, compiler_params=_params(("parallel", "parallel"), 48),
    )(c, w, g_own, g_other, m, v)


def _adamw(w, g, m, v, name):
    R, C = w.shape
    tr = 256 if R % 256 == 0 else R

    def body(w_ref, g_ref, m_ref, v_ref, d_o, m_o, v_o):
        d_o[...], m_o[...], v_o[...] = _adamw_math(w_ref[...], g_ref[...], m_ref[...], v_ref[...])

    blk = pl.BlockSpec((tr, C), lambda i: (i, 0))
    return pl.pallas_call(body, name=name, grid=(R // tr,), in_specs=[blk] * 4, out_specs=[blk] * 3,
                          out_shape=[SDS((R, C), f32)] * 3, compiler_params=_params(("parallel",), 40))(w, g, m, v)


def _pack(arrs):
    flat = [a.reshape(-1, LANES) for a in arrs]
    rows = sum(f.shape[0] for f in flat)
    pad = (-rows) % 16
    if pad:
        flat.append(jnp.zeros((pad, LANES), f32))
    return jnp.concatenate(flat, axis=0)


def _unpack(buf, shapes):
    out, r = [], 0
    for s in shapes:
        n = math.prod(s) // LANES
        out.append(buf[r:r + n].reshape(s))
        r += n
    return out


REPLICATED = ("shift_mu", "w_decay0", "a0", "k_k", "k_a", "r_k", "ln_x_w", "ln_x_b", "v_mix0", "lb_logits", "g_norm_w",
              "ln_w", "ln_b")
CHIP_SMALL = (("w_decay_up", 2), ("a_up", 2), ("v_mix_down", 1), ("v_mix_up", 2))
WEIGHTS = ("w_in", "shift_mu", "w_decay0", "w_decay_up", "a0", "a_up", "k_k", "k_a", "r_k", "ln_x_w", "ln_x_b",
           "v_mix0", "v_mix_down", "v_mix_up", "lb_logits", "g_norm_w", "w_out", "ln_w", "ln_b")


def kernel(x, w_in, shift_mu, w_decay0, w_decay_up, a0, a_up, k_k, k_a, r_k, ln_x_w, ln_x_b, v_mix0, v_mix_down, v_mix_up, lb_logits, g_norm_w, w_out, ln_w, ln_b, loss_target, m_w_in, m_shift_mu, m_w_decay0, m_w_decay_up, m_a0, m_a_up, m_k_k, m_k_a, m_r_k, m_ln_x_w, m_ln_x_b, m_v_mix0, m_v_mix_down, m_v_mix_up, m_lb_logits, m_g_norm_w, m_w_out, m_ln_w, m_ln_b, v_w_in, v_shift_mu, v_w_decay0, v_w_decay_up, v_a0, v_a_up, v_k_k, v_k_a, v_r_k, v_ln_x_w, v_ln_x_b, v_v_mix0, v_v_mix_down, v_v_mix_up, v_lb_logits, v_g_norm_w, v_w_out, v_ln_w, v_ln_b):
    W = dict(w_in=w_in, shift_mu=shift_mu, w_decay0=w_decay0, w_decay_up=w_decay_up, a0=a0, a_up=a_up, k_k=k_k, k_a=k_a,
             r_k=r_k, ln_x_w=ln_x_w, ln_x_b=ln_x_b, v_mix0=v_mix0, v_mix_down=v_mix_down, v_mix_up=v_mix_up,
             lb_logits=lb_logits, g_norm_w=g_norm_w, w_out=w_out, ln_w=ln_w, ln_b=ln_b)
    M = dict(w_in=m_w_in, shift_mu=m_shift_mu, w_decay0=m_w_decay0, w_decay_up=m_w_decay_up, a0=m_a0, a_up=m_a_up,
             k_k=m_k_k, k_a=m_k_a, r_k=m_r_k, ln_x_w=m_ln_x_w, ln_x_b=m_ln_x_b, v_mix0=m_v_mix0,
             v_mix_down=m_v_mix_down, v_mix_up=m_v_mix_up, lb_logits=m_lb_logits, g_norm_w=m_g_norm_w, w_out=m_w_out,
             ln_w=m_ln_w, ln_b=m_ln_b)
    V = dict(w_in=v_w_in, shift_mu=v_shift_mu, w_decay0=v_w_decay0, w_decay_up=v_w_decay_up, a0=v_a0, a_up=v_a_up,
             k_k=v_k_k, k_a=v_k_a, r_k=v_r_k, ln_x_w=v_ln_x_w, ln_x_b=v_ln_x_b, v_mix0=v_v_mix0,
             v_mix_down=v_v_mix_down, v_mix_up=v_v_mix_up, lb_logits=v_lb_logits, g_norm_w=v_g_norm_w, w_out=v_w_out,
             ln_w=v_ln_w, ln_b=v_ln_b)
    mx, my, mc = _mesh_pos()
    kc = (2 * mx + my).astype(jnp.int32)
    c_arr = jnp.reshape(mc.astype(jnp.int32), (1,))
    kc_arr = jnp.reshape(kc, (1,))

    small_shard = _pack([W[n] for n, _ in CHIP_SMALL])
    in_halves = jnp.swapaxes(w_in.astype(bf16).reshape(DEPTH, 2, D_MODEL // 2, SHARD_COLS), 0, 1)
    out_halves = jnp.swapaxes(w_out.astype(bf16).reshape(DEPTH, 2, D_MODEL // 8, D_MODEL), 0, 1)
    g_in, g_out, g_small = _gather_shards([in_halves, out_halves, small_shard.reshape(2, -1, LANES)])
    w_in_full = jnp.transpose(g_in, (2, 1, 3, 0, 4)).reshape(DEPTH, D_MODEL, IN_COLS)
    w_out_full = jnp.transpose(g_out, (2, 0, 1, 3, 4)).reshape(DEPTH, D_MODEL, D_MODEL)
    shard_shapes = [W[n].shape for n, _ in CHIP_SMALL]
    g_small = g_small.reshape(4, -1, LANES)
    per_chip = [_unpack(g_small[k], shard_shapes) for k in range(4)]
    full_small = {n: jnp.concatenate([per_chip[k][i] for k in range(4)], axis=ax)
                  for i, (n, ax) in enumerate(CHIP_SMALL)}

    loss_sum, gx, G = _device_step(
        x[0], loss_target[0], w_in_full, w_out_full, shift_mu, w_decay0, full_small["w_decay_up"], a0,
        full_small["a_up"], k_k, k_a, r_k, ln_x_w, ln_x_b, v_mix0, full_small["v_mix_down"], full_small["v_mix_up"],
        lb_logits, g_norm_w, ln_w, ln_b)
    loss = lax.psum(loss_sum[0, 0], ("x", "y", "c"))

    gin_slabs = jnp.transpose(G["w_in"].reshape(DEPTH, D_MODEL, 4, SHARD_COLS), (2, 0, 1, 3))
    gout_slabs = jnp.transpose(G["w_out"].reshape(DEPTH, 4, D_MODEL // 4, D_MODEL), (1, 0, 2, 3))
    (gin_own, gout_own), (gin_other, gout_other) = _reduce_scatter([gin_slabs, gout_slabs], c_arr, kc_arr)

    small_names = list(REPLICATED) + [n for n, _ in CHIP_SMALL]
    small_sum = _allreduce_small(_pack([G[n] for n in small_names]))
    small_full = dict(zip(small_names, _unpack(small_sum, [G[n].shape for n in small_names])))
    grads = {n: small_full[n] for n in REPLICATED}
    for n, ax in CHIP_SMALL:
        width = W[n].shape[ax]
        grads[n] = lax.dynamic_slice_in_dim(small_full[n], kc * width, width, axis=ax)

    delta, new_m, new_v = {}, {}, {}
    for n, own, other in (("w_in", gin_own, gin_other), ("w_out", gout_own, gout_other)):
        grads[n], delta[n], new_m[n], new_v[n] = _adamw_shard(W[n], own, other, M[n], V[n], c_arr, f"adamw_{n}")
    shapes = [W[n].shape for n in small_names]
    d_, m_, v_ = _adamw(_pack([W[n] for n in small_names]), _pack([grads[n] for n in small_names]),
                        _pack([M[n] for n in small_names]), _pack([V[n] for n in small_names]), "adamw_small")
    for n, d1, m1, v1 in zip(small_names, _unpack(d_, shapes), _unpack(m_, shapes), _unpack(v_, shapes)):
        delta[n], new_m[n], new_v[n] = d1, m1, v1

    return (loss, gx[None], *[grads[n] for n in WEIGHTS], *[delta[n] for n in WEIGHTS],
            *[new_m[n] for n in WEIGHTS], *[new_v[n] for n in WEIGHTS])
```

```python
import functools
import math

import jax
import jax.numpy as jnp
from jax import lax
from jax.experimental import pallas as pl
from jax.experimental.pallas import tpu as pltpu

f32 = jnp.float32
bf16 = jnp.bfloat16
SDS = jax.ShapeDtypeStruct
MESH = pl.DeviceIdType.MESH

D_MODEL = 2048
DEPTH = 2
D_RWKV = 1024
D_HGRN = 1024
RWKV_HEAD = 64
RWKV_HEADS = 16
LORA = 64
VRES = 32
HG_HEAD = 128
HG_HEADS = 8
CHUNK = 64
RWKV_COLS = 4 * D_RWKV + 2 * LORA
IN_COLS = RWKV_COLS + 4 * D_HGRN
SHARD_COLS = IN_COLS // 4
LANES = 128
RW_BLKS = RWKV_COLS // LANES
ALPHA = (2 * DEPTH) ** 0.25
LN_EPS = 1e-5
GN_EPS = 64e-5
RMS_EPS = 1e-5
LB_FLOOR = 1e-30
EXP_M05 = math.exp(-0.5)
ADAM_LR, ADAM_B1, ADAM_B2, ADAM_EPS, ADAM_WD, ADAM_STEP = 0.001, 0.9, 0.999, 1e-08, 0.01, 10

TB = 128
SCAN_TB = 8
SCAN_PARTS = 2
HSUB = 16
HIGHEST = lax.Precision.HIGHEST


def _params(sem=None, vmem_mb=None):
    kw = {}
    if sem is not None:
        kw["dimension_semantics"] = sem
    if vmem_mb is not None:
        kw["vmem_limit_bytes"] = vmem_mb << 20
    return pltpu.CompilerParams(**kw)


def _split(x, n):
    parts, r = [], x
    for i in range(n):
        p = r.astype(bf16)
        parts.append(p)
        if i + 1 < n:
            r = r - p.astype(f32)
    return parts


def _dot_parts(x, m, n=3):
    R = x.shape[0]
    res = jnp.dot(jnp.concatenate(_split(x, n), axis=0), m, preferred_element_type=f32)
    acc = res[0:R]
    for i in range(1, n):
        acc = acc + res[i * R:(i + 1) * R]
    return acc


def _seg_raw(x, ee, n=3, per_tile=False):
    R = x.shape[0]
    if per_tile:
        outs = []
        for c in range(4):
            tile_parts = _split(x[:, 256 * c:256 * c + 256], n)
            res = jnp.dot(jnp.concatenate(tile_parts, axis=0), ee, preferred_element_type=f32)
            acc = res[0:R]
            for i in range(1, n):
                acc = acc + res[i * R:(i + 1) * R]
            outs.append(acc)
        return jnp.concatenate(outs, axis=1)
    parts = _split(x, n)
    lhs = jnp.concatenate([p[:, 256 * c:256 * c + 256] for p in parts for c in range(4)], axis=0)
    res = jnp.dot(lhs, ee, preferred_element_type=f32)

    def tile(c):
        acc = res[c * R:(c + 1) * R]
        for i in range(1, n):
            acc = acc + res[(4 * i + c) * R:(4 * i + c + 1) * R]
        return acc

    return jnp.concatenate([tile(c) for c in range(4)], axis=1)


@jax.custom_vjp
def _seg(x, ee):
    return _seg_raw(x, ee)


def _seg_fwd(x, ee):
    return _seg_raw(x, ee), ee


def _seg_bwd(ee, g):
    return _seg_raw(g, ee), jnp.zeros_like(ee)


_seg.defvjp(_seg_fwd, _seg_bwd)


def _const_mats():
    i256 = jnp.arange(256) // RWKV_HEAD
    ee = (i256[:, None] == i256[None, :]).astype(bf16)
    head = jnp.arange(D_RWKV) // RWKV_HEAD
    lane = jnp.arange(LANES)
    e16 = (head[:, None] == lane[None, :]).astype(bf16)
    e16t3 = ((lane[:, None] % RWKV_HEADS == head[None, :]) & (lane[:, None] < 3 * RWKV_HEADS)).astype(bf16)
    return ee, e16, e16t3


def _matmul(a, b, *, mode, tm, tn, tk, name, add=None, b_layer=None):
    bs = b.shape if b_layer is None else b.shape[1:]
    lead = () if b_layer is None else (None,)
    bidx = (lambda *t: t) if b_layer is None else (lambda *t: (b_layer,) + t)
    if mode == "nn":
        (M, K), N = a.shape, bs[1]
        tm, tn, tk = min(tm, M), min(tn, N), min(tk, K)
        a_spec = pl.BlockSpec((tm, tk), lambda i, j, k: (i, k))
        b_spec = pl.BlockSpec(lead + (tk, tn), lambda i, j, k: bidx(k, j))
        dims = (((1,), (0,)), ((), ()))
    elif mode == "nt":
        (M, K), N = a.shape, bs[0]
        tm, tn, tk = min(tm, M), min(tn, N), min(tk, K)
        a_spec = pl.BlockSpec((tm, tk), lambda i, j, k: (i, k))
        b_spec = pl.BlockSpec(lead + (tn, tk), lambda i, j, k: bidx(j, k))
        dims = (((1,), (1,)), ((), ()))
    else:
        (K, M), N = a.shape, bs[1]
        tm, tn, tk = min(tm, M), min(tn, N), min(tk, K)
        a_spec = pl.BlockSpec((tk, tm), lambda i, j, k: (k, i))
        b_spec = pl.BlockSpec(lead + (tk, tn), lambda i, j, k: bidx(k, j))
        dims = (((0,), (0,)), ((), ()))
    assert M % tm == 0 and N % tn == 0 and K % tk == 0, (M, N, K, tm, tn, tk)
    has_add = add is not None

    def body(*refs):
        if has_add:
            a_ref, b_ref, c_ref, o_ref = refs
        else:
            a_ref, b_ref, o_ref = refs
        k = pl.program_id(2)
        p = lax.dot_general(a_ref[...].astype(bf16), b_ref[...].astype(bf16), dims, preferred_element_type=f32)

        @pl.when(k == 0)
        def _():
            o_ref[...] = p + c_ref[...] if has_add else p

        @pl.when(k > 0)
        def _():
            o_ref[...] += p

    o_spec = pl.BlockSpec((tm, tn), lambda i, j, k: (i, j))
    in_specs = [a_spec, b_spec] + ([o_spec] if has_add else [])
    args = (a, b) + ((add,) if has_add else ())
    return pl.pallas_call(
        body, name=name, grid=(M // tm, N // tn, K // tk), in_specs=in_specs, out_specs=o_spec,
        out_shape=SDS((M, N), f32),
        compiler_params=_params(("parallel", "parallel", "arbitrary"), 56),
    )(*args)


def _rwkv_core(k_in, v_in, wdad, w0, wup, a0, aup, kkw, kaw, ee, vmix):
    w_raw = w0 + jnp.dot(jnp.tanh(wdad), wup, preferred_element_type=f32)
    decay = jnp.exp(-EXP_M05 * jax.nn.sigmoid(w_raw))
    a = jax.nn.sigmoid(a0 + jnp.dot(wdad, aup, preferred_element_type=f32))
    if vmix is None:
        v = v_in
    else:
        v0, vdn, vup, vfirst = vmix
        gate = jax.nn.sigmoid(v0 + jnp.dot(jnp.dot(v_in, vdn, preferred_element_type=f32), vup,
                                           preferred_element_type=f32))
        v = v_in + (vfirst - v_in) * gate
    kk = k_in * kkw
    n2 = _seg(kk * kk, ee)
    kk = kk / jnp.maximum(jnp.sqrt(n2), 1e-12)
    k2 = k_in * (1.0 + (a - 1.0) * kaw)
    return decay, k2, v, -kk, kk * a


def _shifted(y, prev_row):
    yp = pltpu.roll(y, 1, 0)
    row = lax.broadcasted_iota(jnp.int32, y.shape, 0)
    return jnp.where(row == 0, prev_row, yp)


def _rwkv_pre_fwd(proj, mu, w0, wup, a0, aup, kkw, kaw, ee, vmix, layer):
    T = proj.shape[0]
    nb = T // TB
    has_mix = vmix is not None

    def body(*refs):
        y_ref, p8_ref, mu_ref, w0_ref, wup_ref, a0_ref, aup_ref, kkw_ref, kaw_ref, ee_ref = refs[:10]
        rest = refs[10:]
        if has_mix:
            v0_ref, vdn_ref, vup_ref, vf_ref = rest[:4]
            rest = rest[4:]
        r_o, w_o, k_o, v_o, a_o, b_o, z_o = rest
        i = pl.program_id(0)
        y = y_ref[...]
        prev = jnp.where(i == 0, 0.0, p8_ref[7:8, :])
        rw = y + mu_ref[...] * (_shifted(y, prev) - y)
        mix = (v0_ref[...], vdn_ref[...], vup_ref[...], vf_ref[...]) if has_mix else None
        dec, k2, v, av, bv = _rwkv_core(rw[:, 1024:2048], rw[:, 2048:3072], rw[:, 4096:4224], w0_ref[...],
                                        wup_ref[...], a0_ref[...], aup_ref[...], kkw_ref[...], kaw_ref[...],
                                        ee_ref[...], mix)
        r_o[...] = rw[:, 0:1024]
        w_o[...] = dec
        k_o[...] = k2
        v_o[...] = v
        a_o[...] = av
        b_o[...] = bv
        z_o[...] = rw[:, 3072:4096]

    row = lambda n: pl.BlockSpec((1, n), lambda i: (0, 0))
    full = lambda s: pl.BlockSpec(s, lambda i: (0,) * len(s))
    blk = pl.BlockSpec((TB, D_RWKV), lambda i: (i, 0))
    in_specs = [pl.BlockSpec((TB, RWKV_COLS), lambda i: (i, 0)),
                pl.BlockSpec((8, RWKV_COLS), lambda i: (jnp.maximum(i * (TB // 8) - 1, 0), 0)),
                row(RWKV_COLS), row(D_RWKV), full((LANES, D_RWKV)), row(D_RWKV), full((LANES, D_RWKV)),
                row(D_RWKV), row(D_RWKV), full((256, 256))]
    args = [proj, proj, mu, w0, wup, a0, aup, kkw, kaw, ee]
    if has_mix:
        v0, vdn, vup, vfirst = vmix
        in_specs += [row(D_RWKV), full((D_RWKV, LANES)), full((LANES, D_RWKV)), blk]
        args += [v0, vdn, vup, vfirst]
    return pl.pallas_call(
        body, name=f"rwkv_pre_fwd_l{layer}", grid=(nb,), in_specs=in_specs, out_specs=[blk] * 7,
        out_shape=[SDS((T, D_RWKV), f32)] * 7, compiler_params=_params(("parallel",), 48),
    )(*args)


def _rwkv_pre_bwd(proj, mu, w0, wup, a0, aup, kkw, kaw, ee, vmix, cots, layer):
    T = proj.shape[0]
    nb = T // TB
    has_mix = vmix is not None
    cot_list = cots["r"] + [cots["w"]] + cots["k"] + cots["v"] + [cots["a"], cots["b"], cots["z"]]
    n_r, n_k, n_v = len(cots["r"]), len(cots["k"]), len(cots["v"])
    n_cot = len(cot_list)

    def body(*refs):
        y_ref, p8_ref, mu_ref, w0_ref, wup_ref, a0_ref, aup_ref, kkw_ref, kaw_ref, ee_ref = refs[:10]
        rest = refs[10:]
        if has_mix:
            v0_ref, vdn_ref, vup_ref, vf_ref = rest[:4]
            rest = rest[4:]
        cot_refs, rest = rest[:n_cot], rest[n_cot:]
        if has_mix:
            (dproj_o, dmu_o, dw0_o, dwup_o, da0_o, daup_o, dkkw_o, dkaw_o,
             dv0_o, dvdn_o, dvup_o, dvf_o, carry) = rest
        else:
            dproj_o, dmu_o, dw0_o, dwup_o, da0_o, daup_o, dkkw_o, dkaw_o, carry = rest
        i = pl.program_id(0)
        blk_i = nb - 1 - i
        y = y_ref[...]
        prev = jnp.where(blk_i == 0, 0.0, p8_ref[7:8, :])
        yp = _shifted(y, prev)
        mu_v = mu_ref[...]
        rw = y + mu_v * (yp - y)

        def sum_refs(rs):
            acc = rs[0][...]
            for r_ in rs[1:]:
                acc = acc + r_[...]
            return acc

        c = list(cot_refs)
        g_r = sum_refs(c[:n_r]); c = c[n_r:]
        g_w = c[0][...]; c = c[1:]
        g_k = sum_refs(c[:n_k]); c = c[n_k:]
        g_v = sum_refs(c[:n_v]); c = c[n_v:]
        g_a, g_b, g_z = c[0][...], c[1][...], c[2][...]

        ee_v = ee_ref[...]
        if has_mix:
            def fn(k_in, v_in, wdad, w0_, wup_, a0_, aup_, kkw_, kaw_, v0_, vdn_, vup_, vf_):
                return _rwkv_core(k_in, v_in, wdad, w0_, wup_, a0_, aup_, kkw_, kaw_, ee_v, (v0_, vdn_, vup_, vf_))
            prim = (rw[:, 1024:2048], rw[:, 2048:3072], rw[:, 4096:4224], w0_ref[...], wup_ref[...], a0_ref[...],
                    aup_ref[...], kkw_ref[...], kaw_ref[...], v0_ref[...], vdn_ref[...], vup_ref[...], vf_ref[...])
        else:
            def fn(k_in, v_in, wdad, w0_, wup_, a0_, aup_, kkw_, kaw_):
                return _rwkv_core(k_in, v_in, wdad, w0_, wup_, a0_, aup_, kkw_, kaw_, ee_v, None)
            prim = (rw[:, 1024:2048], rw[:, 2048:3072], rw[:, 4096:4224], w0_ref[...], wup_ref[...], a0_ref[...],
                    aup_ref[...], kkw_ref[...], kaw_ref[...])
        _, vjp = jax.vjp(fn, *prim)
        gs = vjp((g_w, g_k, g_v, g_a, g_b))
        d_k, d_v, d_wdad = gs[0], gs[1], gs[2]
        g_rw = jnp.concatenate([g_r, d_k, d_v, g_z, d_wdad], axis=1)

        @pl.when(i == 0)
        def _():
            carry[...] = jnp.zeros_like(carry)
            dmu_o[...] = jnp.zeros_like(dmu_o)
            dw0_o[...] = jnp.zeros_like(dw0_o)
            dwup_o[...] = jnp.zeros_like(dwup_o)
            da0_o[...] = jnp.zeros_like(da0_o)
            daup_o[...] = jnp.zeros_like(daup_o)
            dkkw_o[...] = jnp.zeros_like(dkkw_o)
            dkaw_o[...] = jnp.zeros_like(dkaw_o)
            if has_mix:
                dv0_o[...] = jnp.zeros_like(dv0_o)
                dvdn_o[...] = jnp.zeros_like(dvdn_o)
                dvup_o[...] = jnp.zeros_like(dvup_o)

        dmu_o[...] += jnp.sum(g_rw * (yp - y), axis=0, keepdims=True)
        dw0_o[...] += gs[3]
        dwup_o[...] += gs[4]
        da0_o[...] += gs[5]
        daup_o[...] += gs[6]
        dkkw_o[...] += gs[7]
        dkaw_o[...] += gs[8]
        if has_mix:
            dv0_o[...] += gs[9]
            dvdn_o[...] += gs[10]
            dvup_o[...] += gs[11]
            dvf_o[...] = gs[12]
        gm = g_rw * mu_v
        nxt = pltpu.roll(gm, TB - 1, 0)
        rowi = lax.broadcasted_iota(jnp.int32, gm.shape, 0)
        nxt = jnp.where(rowi == TB - 1, carry[...], nxt)
        dproj_o[...] = g_rw - gm + nxt
        carry[...] = gm[0:1, :]

    rev = lambda i: (nb - 1 - i, 0)
    row = lambda n: pl.BlockSpec((1, n), lambda i: (0, 0))
    full = lambda s: pl.BlockSpec(s, lambda i: (0,) * len(s))
    blk = pl.BlockSpec((TB, D_RWKV), rev)
    in_specs = [pl.BlockSpec((TB, RWKV_COLS), rev),
                pl.BlockSpec((8, RWKV_COLS), lambda i: (jnp.maximum((nb - 1 - i) * (TB // 8) - 1, 0), 0)),
                row(RWKV_COLS), row(D_RWKV), full((LANES, D_RWKV)), row(D_RWKV), full((LANES, D_RWKV)),
                row(D_RWKV), row(D_RWKV), full((256, 256))]
    args = [proj, proj, mu, w0, wup, a0, aup, kkw, kaw, ee]
    out_specs = [pl.BlockSpec((TB, RWKV_COLS), rev), row(RWKV_COLS), row(D_RWKV), full((LANES, D_RWKV)),
                 row(D_RWKV), full((LANES, D_RWKV)), row(D_RWKV), row(D_RWKV)]
    out_shape = [SDS((T, RWKV_COLS), f32), SDS((1, RWKV_COLS), f32), SDS((1, D_RWKV), f32), SDS((LANES, D_RWKV), f32),
                 SDS((1, D_RWKV), f32), SDS((LANES, D_RWKV), f32), SDS((1, D_RWKV), f32), SDS((1, D_RWKV), f32)]
    if has_mix:
        v0, vdn, vup, vfirst = vmix
        in_specs += [row(D_RWKV), full((D_RWKV, LANES)), full((LANES, D_RWKV)), blk]
        args += [v0, vdn, vup, vfirst]
        out_specs += [row(D_RWKV), full((D_RWKV, LANES)), full((LANES, D_RWKV)), blk]
        out_shape += [SDS((1, D_RWKV), f32), SDS((D_RWKV, LANES), f32), SDS((LANES, D_RWKV), f32),
                      SDS((T, D_RWKV), f32)]
    in_specs += [blk] * n_cot
    args += cot_list
    return pl.pallas_call(
        body, name=f"rwkv_pre_bwd_l{layer}", grid=(nb,), in_specs=in_specs, out_specs=out_specs, out_shape=out_shape,
        scratch_shapes=[pltpu.VMEM((1, RWKV_COLS), f32)],
        compiler_params=_params(("arbitrary",), 60),
    )(*args)


def _to_col(v):
    T = v.shape[0]
    vc = jnp.swapaxes(v.reshape(T, RWKV_HEADS, RWKV_HEAD), 1, 2)
    hi = lax.reduce_precision(vc, 8, 7)
    r1 = vc - hi
    mid = lax.reduce_precision(r1, 8, 7)
    parts = jnp.concatenate([hi, mid, r1 - mid], axis=2).astype(bf16)
    return jnp.pad(parts, ((0, 0), (0, 0), (0, LANES - 3 * RWKV_HEADS)))


def _from_col(vc):
    T = vc.shape[0]
    return jnp.swapaxes(vc[:, :, :RWKV_HEADS], 1, 2).reshape(T, D_RWKV)


def _rwkv_scan_fwd(r, w, k, vcol, a, b, ee, e16, e16t, layer, gather=()):
    T = r.shape[0]
    nb = T // SCAN_TB
    ng = len(gather)

    def body(*refs):
        r_ref, w_ref, k_ref, v_ref, a_ref, an_ref, b_ref, ee_ref, e16_ref, e16t_ref = refs[:10]
        g_ins, refs = refs[10:10 + ng], refs[10 + ng:]
        o_ref, sp_ref = refs[:2]
        g_outs, refs = refs[2:2 + ng], refs[2 + ng:]
        s_ref, sa_ref, vb_ref, po_ref = refs[:4]
        g_sems = refs[4:]
        step = pl.program_id(0)

        @pl.when(step == 0)
        def _():
            s_ref[...] = jnp.zeros_like(s_ref)
            sa_ref[...] = jnp.zeros_like(sa_ref)
            if ng:
                _gather_start(g_ins, g_outs, *g_sems)

        if ng:
            @pl.when(step == (3 * nb) // 4)
            def _():
                _gather_forward(g_ins, g_outs, *g_sems)

        ee_v = ee_ref[...]
        r_b, w_b, k_b, a_b, b_b = r_ref[...], w_ref[...], k_ref[...], a_ref[...], b_ref[...]
        rowi = lax.broadcasted_iota(jnp.int32, a_b.shape, 0)
        a_nx = jnp.where(rowi == SCAN_TB - 1, an_ref[0:1, :], pltpu.roll(a_b, SCAN_TB - 1, 0))
        wa = w_b * a_nx
        dots = _seg_raw(jnp.concatenate([b_b * a_nx, k_b * a_nx], axis=0), ee_v, 3)
        beta, kappa = dots[0:SCAN_TB], dots[SCAN_TB:2 * SCAN_TB]
        vb_ref[...] = jnp.dot(v_ref[...].reshape(SCAN_TB * RWKV_HEAD, LANES), e16t_ref[...],
                              preferred_element_type=f32)

        for t in range(SCAN_TB):
            row = lambda x: x[t:t + 1, :]
            blk = pl.ds(t * RWKV_HEAD, RWKV_HEAD)
            sp = s_ref[...]
            sp_ref[t] = sp
            sa = sa_ref[...]
            vb = vb_ref[blk, :]
            ahead = _seg_raw(sp * row(wa), ee_v, SCAN_PARTS, per_tile=True)
            sn = sp * row(w_b) + sa * row(b_b) + vb * row(k_b)
            s_ref[...] = sn
            sa_ref[...] = ahead + sa * row(beta) + vb * row(kappa)
            po_ref[blk, :] = sn * row(r_b)
        o_ref[...] = _dot_parts(po_ref[...], e16_ref[...], 1).reshape(SCAN_TB, RWKV_HEAD, LANES)

        if ng:
            @pl.when(step == nb - 1)
            def _():
                _gather_finish(g_ins, g_outs, *g_sems)

    rows = pl.BlockSpec((SCAN_TB, D_RWKV), lambda i: (i, 0))
    rows_next = pl.BlockSpec((SCAN_TB, D_RWKV), lambda i: (jnp.minimum(i + 1, nb - 1), 0))
    col = pl.BlockSpec((SCAN_TB, RWKV_HEAD, LANES), lambda i: (i, 0, 0))
    full = lambda s: pl.BlockSpec(s, lambda i: (0,) * len(s))
    outs = pl.pallas_call(
        body, name=f"rwkv_scan_fwd_l{layer}", grid=(nb,),
        in_specs=[rows, rows, rows, col, rows, rows_next, rows, full((256, 256)), full((D_RWKV, LANES)),
                  full((LANES, D_RWKV))] + [HBM_SPEC] * ng,
        out_specs=[col, pl.BlockSpec((SCAN_TB, RWKV_HEAD, D_RWKV), lambda i: (i, 0, 0))] + [HBM_SPEC] * ng,
        out_shape=[SDS((T, RWKV_HEAD, LANES), f32), SDS((T, RWKV_HEAD, D_RWKV), f32)] + _gather_out_shapes(gather),
        scratch_shapes=[pltpu.VMEM((RWKV_HEAD, D_RWKV), f32), pltpu.VMEM((RWKV_HEAD, D_RWKV), f32),
                        pltpu.VMEM((SCAN_TB * RWKV_HEAD, D_RWKV), f32),
                        pltpu.VMEM((SCAN_TB * RWKV_HEAD, D_RWKV), f32)] + (_gather_sems(ng) if ng else []),
        compiler_params=_params(("arbitrary",), 48),
    )(r, w, k, vcol, a, a, b, ee, e16, e16t, *gather)
    return outs[0], outs[1], list(outs[2:])


def _rwkv_scan_bwd(r, w, k, vcol, a, b, sprev, docol, ee, e16, e16t, layer, send=()):
    T = r.shape[0]
    nb = T // SCAN_TB
    ns = len(send)

    def body(*refs):
        r_ref, w_ref, k_ref, v_ref, a_ref, b_ref, sp_ref, do_ref, ee_ref, e16_ref, e16t_ref = refs[:11]
        c_ins, refs = refs[11:11 + ns], refs[11 + ns:]
        dr_o, dw_o, dk_o, da_o, db_o, dv_o = refs[:6]
        c_outs, refs = refs[6:6 + ns], refs[6 + ns:]
        ds_ref, snext_ref, sa_ref, vb_ref, dob_ref, pdv_ref = refs[:6]
        c_sems = refs[6:]
        if ns:
            @pl.when(pl.program_id(0) == 0)
            def _():
                for cp in _chips_copies(c_ins, c_outs, *c_sems):
                    cp.start()

        ee_v, e16t_v = ee_ref[...], e16t_ref[...]
        csum = lambda x: jnp.sum(x, axis=0, keepdims=True)
        nrow = SCAN_TB * RWKV_HEAD
        last = pl.ds(nrow - RWKV_HEAD, RWKV_HEAD)
        r_b, w_b, k_b, a_b, b_b = r_ref[...], w_ref[...], k_ref[...], a_ref[...], b_ref[...]
        sa_ref[...] = _seg_raw((sp_ref[...] * a_b[:, None, :]).reshape(nrow, D_RWKV), ee_v, SCAN_PARTS)
        vb_ref[...] = jnp.dot(v_ref[...].reshape(nrow, LANES), e16t_v, preferred_element_type=f32)
        dob_ref[...] = jnp.dot(do_ref[...].reshape(nrow, LANES), e16t_v, preferred_element_type=f32)

        @pl.when(pl.program_id(0) == 0)
        def _():
            ds_ref[...] = jnp.zeros_like(ds_ref)
            snext_ref[...] = (sp_ref[SCAN_TB - 1] * w_b[SCAN_TB - 1:SCAN_TB, :]
                              + sa_ref[last, :] * b_b[SCAN_TB - 1:SCAN_TB, :]
                              + vb_ref[last, :] * k_b[SCAN_TB - 1:SCAN_TB, :])

        for s in range(SCAN_TB):
            t = SCAN_TB - 1 - s
            row = lambda x: x[t:t + 1, :]
            blk = pl.ds(t * RWKV_HEAD, RWKV_HEAD)
            sp = sp_ref[t]
            sn = snext_ref[...] if s == 0 else sp_ref[t + 1]
            sa, vb, dob = sa_ref[blk, :], vb_ref[blk, :], dob_ref[blk, :]
            ds = ds_ref[...] + dob * row(r_b)
            dsa = _seg_raw(ds * row(b_b), ee_v, SCAN_PARTS, per_tile=True)
            ds_ref[...] = ds * row(w_b) + dsa * row(a_b)
            dr_o[t:t + 1, :] = csum(sn * dob)
            pdv_ref[blk, :] = ds * row(k_b)
            dk_o[t:t + 1, :] = csum(ds * vb)
            db_o[t:t + 1, :] = csum(ds * sa)
            dw_o[t:t + 1, :] = csum(ds * sp)
            da_o[t:t + 1, :] = csum(sp * dsa)
        snext_ref[...] = sp_ref[0]
        dv_o[...] = _dot_parts(pdv_ref[...], e16_ref[...], 1).reshape(SCAN_TB, RWKV_HEAD, LANES)

        if ns:
            @pl.when(pl.program_id(0) == nb - 1)
            def _():
                for cp in _chips_copies(c_ins, c_outs, *c_sems):
                    cp.wait()

    rows = pl.BlockSpec((SCAN_TB, D_RWKV), lambda i: (nb - 1 - i, 0))
    col = pl.BlockSpec((SCAN_TB, RWKV_HEAD, LANES), lambda i: (nb - 1 - i, 0, 0))
    st = pl.BlockSpec((SCAN_TB, RWKV_HEAD, D_RWKV), lambda i: (nb - 1 - i, 0, 0))
    full = lambda s: pl.BlockSpec(s, lambda i: (0,) * len(s))
    big = pltpu.VMEM((SCAN_TB * RWKV_HEAD, D_RWKV), f32)
    outs = pl.pallas_call(
        body, name=f"rwkv_scan_bwd_l{layer}", grid=(nb,),
        in_specs=[rows, rows, rows, col, rows, rows, st, col, full((256, 256)), full((D_RWKV, LANES)),
                  full((LANES, D_RWKV))] + [HBM_SPEC] * ns,
        out_specs=[rows] * 5 + [col] + [HBM_SPEC] * ns,
        out_shape=[SDS((T, D_RWKV), f32)] * 5 + [SDS((T, RWKV_HEAD, LANES), f32)] + _chips_out_shapes(send),
        scratch_shapes=[pltpu.VMEM((RWKV_HEAD, D_RWKV), f32)] * 2 + [big] * 4 + (_chips_sems(ns) if ns else []),
        compiler_params=_params(("arbitrary",), 56),
    )(r, w, k, vcol, a, b, sprev, docol, ee, e16, e16t, *send)
    return (*outs[:6], list(outs[6:]))


def _post_core(o, r, k2, v, z, gw, gb, rk, ee):
    inv = 1.0 / RWKV_HEAD
    mu = _seg(o, ee) * inv
    d = o - mu
    var = _seg(d * d, ee) * inv
    on = d * lax.rsqrt(var + GN_EPS) * gw + gb
    bonus = _seg(r * k2 * rk, ee) * v
    return (on + bonus) * jax.nn.silu(z)


def _rwkv_post_fwd(o, r, k2, v, z, gw, gb, rk, ee, layer):
    T = o.shape[0]

    def body(o_ref, r_ref, k_ref, v_ref, z_ref, gw_ref, gb_ref, rk_ref, ee_ref, out_ref):
        out_ref[...] = _post_core(o_ref[...], r_ref[...], k_ref[...], v_ref[...], z_ref[...], gw_ref[...],
                                  gb_ref[...], rk_ref[...], ee_ref[...])

    blk = pl.BlockSpec((TB, D_RWKV), lambda i: (i, 0))
    row = pl.BlockSpec((1, D_RWKV), lambda i: (0, 0))
    return pl.pallas_call(
        body, name=f"rwkv_post_fwd_l{layer}", grid=(T // TB,),
        in_specs=[blk] * 5 + [row] * 3 + [pl.BlockSpec((256, 256), lambda i: (0, 0))],
        out_specs=blk, out_shape=SDS((T, D_MODEL), f32), compiler_params=_params(("parallel",), 40),
    )(o, r, k2, v, z, gw, gb, rk, ee)


def _rwkv_post_bwd(o, r, k2, v, z, gw, gb, rk, ee, dcat, layer):
    T = o.shape[0]

    def body(o_ref, r_ref, k_ref, v_ref, z_ref, gw_ref, gb_ref, rk_ref, ee_ref, g_ref,
             do_o, dr_o, dk_o, dv_o, dz_o, dgw_o, dgb_o, drk_o):
        ee_v = ee_ref[...]
        fn = lambda o_, r_, k_, v_, z_, gw_, gb_, rk_: _post_core(o_, r_, k_, v_, z_, gw_, gb_, rk_, ee_v)
        _, vjp = jax.vjp(fn, o_ref[...], r_ref[...], k_ref[...], v_ref[...], z_ref[...], gw_ref[...], gb_ref[...],
                         rk_ref[...])
        gs = vjp(g_ref[...])
        do_o[...], dr_o[...], dk_o[...], dv_o[...], dz_o[...] = gs[:5]

        @pl.when(pl.program_id(0) == 0)
        def _():
            dgw_o[...] = jnp.zeros_like(dgw_o)
            dgb_o[...] = jnp.zeros_like(dgb_o)
            drk_o[...] = jnp.zeros_like(drk_o)

        dgw_o[...] += gs[5]
        dgb_o[...] += gs[6]
        drk_o[...] += gs[7]

    blk = pl.BlockSpec((TB, D_RWKV), lambda i: (i, 0))
    row = pl.BlockSpec((1, D_RWKV), lambda i: (0, 0))
    return pl.pallas_call(
        body, name=f"rwkv_post_bwd_l{layer}", grid=(T // TB,),
        in_specs=[blk] * 5 + [row] * 3 + [pl.BlockSpec((256, 256), lambda i: (0, 0)), blk],
        out_specs=[blk] * 5 + [row] * 3,
        out_shape=[SDS((T, D_RWKV), f32)] * 5 + [SDS((1, D_RWKV), f32)] * 3,
        compiler_params=_params(("arbitrary",), 48),
    )(o, r, k2, v, z, gw, gb, rk, ee, dcat)


def _hgrn_chunk(qr, fr, ii, z, S, lbl, gw, layer):
    L = CHUNK
    m = jnp.max(lbl, axis=0, keepdims=True)
    e = jnp.exp(lbl - m)
    sm = e / jnp.sum(e, axis=0, keepdims=True)
    if layer == 0:
        lb = sm[0:1] - sm[0:1]
    else:
        lb = (sm[0:1] + sm[1:2]) - sm[0:1]
    q = jax.nn.silu(qr)
    log_lb = jnp.log(jnp.maximum(lb, LB_FLOOR))
    lf = jnp.logaddexp(log_lb, jnp.log1p(-lb) + jax.nn.log_sigmoid(fr))
    k = (1.0 - lb) * jax.nn.sigmoid(-fr)
    ti = lax.broadcasted_iota(jnp.int32, (L, L), 0)
    si = lax.broadcasted_iota(jnp.int32, (L, L), 1)
    ltri = (si <= ti).astype(f32)
    b = jnp.dot(ltri, lf, precision=HIGHEST, preferred_element_type=f32)
    btot = jnp.dot(jnp.ones((L, L), f32), lf, precision=HIGHEST, preferred_element_type=f32)
    o = jnp.dot(q * jnp.exp(b), S, preferred_element_type=f32)
    nsub = L // HSUB
    t3 = lax.broadcasted_iota(jnp.int32, (HSUB, HSUB, HG_HEAD), 0)
    s3 = lax.broadcasted_iota(jnp.int32, (HSUB, HSUB, HG_HEAD), 1)
    causal = s3 <= t3
    rows = []
    for I in range(nsub):
        sl = slice(I * HSUB, (I + 1) * HSUB)
        bI, qI, kI, iI = b[sl], q[sl], k[sl], ii[sl]
        d3 = bI[:, None, :] - bI[None, :, :]
        dec = jnp.where(causal, jnp.exp(jnp.where(causal, d3, 0.0)), 0.0)
        att = jnp.sum(qI[:, None, :] * dec * kI[None, :, :], axis=-1)
        oI = jnp.dot(att, iI, preferred_element_type=f32)
        for J in range(I):
            sj = slice(J * HSUB, (J + 1) * HSUB)
            bm = b[(J + 1) * HSUB - 1:(J + 1) * HSUB]
            qs = qI * jnp.exp(bI - bm)
            ks = k[sj] * jnp.exp(bm - b[sj])
            att_ij = lax.dot_general(qs, ks, (((1,), (1,)), ((), ())), preferred_element_type=f32)
            oI = oI + jnp.dot(att_ij, ii[sj], preferred_element_type=f32)
        rows.append(oI)
    o = o + jnp.concatenate(rows, axis=0)
    k_dec = k * jnp.exp(btot - b)
    bcol = lax.dot_general(lf, jnp.ones((L, HG_HEAD), f32), (((0,), (0,)), ((), ())), precision=HIGHEST,
                           preferred_element_type=f32)
    s_new = S * jnp.exp(bcol) + lax.dot_general(k_dec, ii, (((0,), (0,)), ((), ())), preferred_element_type=f32)
    ms = jnp.mean(o * o, axis=-1, keepdims=True)
    out = o * lax.rsqrt(ms + RMS_EPS) * gw * jax.nn.silu(z)
    return out, s_new


def _hg_specs(nc, rev):
    cidx = (lambda c: nc - 1 - c) if rev else (lambda c: c)
    pj = lambda off: pl.BlockSpec((CHUNK, HG_HEAD), lambda h, c: (cidx(c), RW_BLKS + off + h))
    return cidx, [pj(0), pj(HG_HEADS), pj(2 * HG_HEADS), pj(3 * HG_HEADS)]


def _hgrn_fwd(proj, cat, lbl, gw, layer):
    T = proj.shape[0]
    nc = T // CHUNK
    cidx, pspecs = _hg_specs(nc, False)

    def body(q_ref, f_ref, i_ref, z_ref, lbl_ref, gw_ref, cat_in, out_ref, sh_ref, s_ref):
        del cat_in

        @pl.when(pl.program_id(1) == 0)
        def _():
            s_ref[...] = jnp.zeros_like(s_ref)

        s0 = s_ref[...]
        sh_ref[0, 0] = s0
        out, s_new = _hgrn_chunk(q_ref[...], f_ref[...], i_ref[...], z_ref[...], s0, lbl_ref[...], gw_ref[...], layer)
        out_ref[...] = out
        s_ref[...] = s_new

    return pl.pallas_call(
        body, name=f"hgrn_fwd_l{layer}", grid=(HG_HEADS, nc),
        in_specs=pspecs + [pl.BlockSpec((2, HG_HEAD), lambda h, c: (0, h)),
                           pl.BlockSpec((1, HG_HEAD), lambda h, c: (0, h)),
                           pl.BlockSpec(memory_space=pl.ANY)],
        out_specs=[pl.BlockSpec((CHUNK, HG_HEAD), lambda h, c: (c, HG_HEADS + h)),
                   pl.BlockSpec((1, 1, HG_HEAD, HG_HEAD), lambda h, c: (h, c, 0, 0))],
        out_shape=[SDS((T, D_MODEL), f32), SDS((HG_HEADS, nc, HG_HEAD, HG_HEAD), f32)],
        scratch_shapes=[pltpu.VMEM((HG_HEAD, HG_HEAD), f32)],
        input_output_aliases={6: 0},
        compiler_params=_params(("parallel", "arbitrary"), 40),
    )(proj, proj, proj, proj, lbl, gw, cat)


def _hgrn_bwd(proj, shist, dcat, lbl, gw, layer):
    T = proj.shape[0]
    nc = T // CHUNK
    cidx, pspecs = _hg_specs(nc, True)

    def body(q_ref, f_ref, i_ref, z_ref, lbl_ref, gw_ref, sh_ref, g_ref,
             dq_o, df_o, di_o, dz_o, dlbl_o, dgw_o, ds_ref):
        @pl.when(pl.program_id(1) == 0)
        def _():
            ds_ref[...] = jnp.zeros_like(ds_ref)
            dlbl_o[...] = jnp.zeros_like(dlbl_o)
            dgw_o[...] = jnp.zeros_like(dgw_o)

        fn = functools.partial(_hgrn_chunk, layer=layer)
        _, vjp = jax.vjp(fn, q_ref[...], f_ref[...], i_ref[...], z_ref[...], sh_ref[0, 0], lbl_ref[...], gw_ref[...])
        dq, df, di, dz, ds, dl, dg = vjp((g_ref[...], ds_ref[...]))
        dq_o[...], df_o[...], di_o[...], dz_o[...] = dq, df, di, dz
        ds_ref[...] = ds
        dlbl_o[...] += dl
        dgw_o[...] += dg

    ospec = pl.BlockSpec((CHUNK, HG_HEAD), lambda h, c: (cidx(c), h))
    return pl.pallas_call(
        body, name=f"hgrn_bwd_l{layer}", grid=(HG_HEADS, nc),
        in_specs=pspecs + [pl.BlockSpec((2, HG_HEAD), lambda h, c: (0, h)),
                           pl.BlockSpec((1, HG_HEAD), lambda h, c: (0, h)),
                           pl.BlockSpec((1, 1, HG_HEAD, HG_HEAD), lambda h, c: (h, cidx(c), 0, 0)),
                           pl.BlockSpec((CHUNK, HG_HEAD), lambda h, c: (cidx(c), HG_HEADS + h))],
        out_specs=[ospec] * 4 + [pl.BlockSpec((2, HG_HEAD), lambda h, c: (0, h)),
                                 pl.BlockSpec((1, HG_HEAD), lambda h, c: (0, h))],
        out_shape=[SDS((T, D_HGRN), f32)] * 4 + [SDS((2, D_HGRN), f32), SDS((1, D_HGRN), f32)],
        scratch_shapes=[pltpu.VMEM((HG_HEAD, HG_HEAD), f32)],
        compiler_params=_params(("parallel", "arbitrary"), 48),
    )(proj, proj, proj, proj, lbl, gw, shist, dcat)


def _ln_core(h, y, w, b):
    u = ALPHA * h + y
    mu = jnp.mean(u, axis=-1, keepdims=True)
    d = u - mu
    var = jnp.mean(d * d, axis=-1, keepdims=True)
    return d * lax.rsqrt(var + LN_EPS) * w + b


def _ln_fwd(h, y, w, b, layer):
    T = h.shape[0]

    def body(h_ref, y_ref, w_ref, b_ref, o_ref):
        o_ref[...] = _ln_core(h_ref[...], y_ref[...], w_ref[...], b_ref[...])

    blk = pl.BlockSpec((TB, D_MODEL), lambda i: (i, 0))
    row = pl.BlockSpec((1, D_MODEL), lambda i: (0, 0))
    return pl.pallas_call(body, name=f"ln_fwd_l{layer}", grid=(T // TB,), in_specs=[blk, blk, row, row],
                          out_specs=blk, out_shape=SDS((T, D_MODEL), f32),
                          compiler_params=_params(("parallel",), 40))(h, y, w, b)


def _ln_bwd(h, y, w, b, g, layer):
    T = h.shape[0]

    def body(h_ref, y_ref, w_ref, b_ref, g_ref, dh_o, dy_o, dw_o, db_o):
        _, vjp = jax.vjp(_ln_core, h_ref[...], y_ref[...], w_ref[...], b_ref[...])
        dh, dy, dw, db = vjp(g_ref[...])
        dh_o[...] = dh
        dy_o[...] = dy

        @pl.when(pl.program_id(0) == 0)
        def _():
            dw_o[...] = jnp.zeros_like(dw_o)
            db_o[...] = jnp.zeros_like(db_o)

        dw_o[...] += dw
        db_o[...] += db

    blk = pl.BlockSpec((TB, D_MODEL), lambda i: (i, 0))
    row = pl.BlockSpec((1, D_MODEL), lambda i: (0, 0))
    return pl.pallas_call(body, name=f"ln_bwd_l{layer}", grid=(T // TB,), in_specs=[blk, blk, row, row, blk],
                          out_specs=[blk, blk, row, row],
                          out_shape=[SDS((T, D_MODEL), f32)] * 2 + [SDS((1, D_MODEL), f32)] * 2,
                          compiler_params=_params(("arbitrary",), 48))(h, y, w, b, g)


def _loss_head(hout, target):
    T = hout.shape[0]

    def body(h_ref, t_ref, loss_o, g_o):
        err = h_ref[...] - t_ref[...]
        g_o[...] = err * (1.0 / D_MODEL)

        @pl.when(pl.program_id(0) == 0)
        def _():
            loss_o[...] = jnp.zeros_like(loss_o)

        part = jnp.sum(jnp.sum(err * err, axis=-1, keepdims=True) * (1.0 / D_MODEL), axis=0, keepdims=True)
        loss_o[...] += 0.5 * part

    blk = pl.BlockSpec((TB, D_MODEL), lambda i: (i, 0))
    return pl.pallas_call(body, name="loss_head", grid=(T // TB,), in_specs=[blk, blk],
                          out_specs=[pl.BlockSpec((1, 1), lambda i: (0, 0)), blk],
                          out_shape=[SDS((1, 1), f32), SDS((T, D_MODEL), f32)],
                          compiler_params=_params(("arbitrary",), 40))(hout, target)


def _pad_rows(m, lo, total):
    return jnp.pad(m, ((0, 0), (lo, total - lo - m.shape[1]), (0, 0)))


def _device_step(x, target, w_in0, later_shards, assemble_later, on_layer1_grads, on_layer1_recv,
                 shift_mu, w_decay0, w_decay_up, a0, a_up, k_k, k_a, r_k, ln_x_w, ln_x_b,
                 v_mix0, v_mix_down, v_mix_up, lb_logits, g_norm_w, ln_w, ln_b):
    T = x.shape[0]
    ee, e16, e16t = _const_mats()
    wup_pad = _pad_rows(w_decay_up, 0, LANES)
    aup_pad = _pad_rows(a_up, LORA, LANES)
    vdn_pad = jnp.pad(v_mix_down, ((0, 0), (0, 0), (0, LANES - VRES)))
    vup_pad = _pad_rows(v_mix_up, 0, LANES)
    row = lambda p, l: p[l][None, :]

    def vmix_of(l, vfirst):
        if l == 0:
            return None
        return (row(v_mix0, l - 1), vdn_pad[l - 1], vup_pad[l - 1], vfirst)

    h = x
    saved = []
    vfirst = None
    w_in = [w_in0, None]
    w_out = None
    for l in range(DEPTH):
        proj = _matmul(h, w_in[l], mode="nn", tm=512, tn=1664, tk=D_MODEL, name=f"proj_fwd_l{l}")
        pre_args = (proj, row(shift_mu, l), row(w_decay0, l), wup_pad[l], row(a0, l), aup_pad[l], row(k_k, l),
                    row(k_a, l), ee, vmix_of(l, vfirst))
        r, w, k2, v, av, bv, z = _rwkv_pre_fwd(*pre_args, layer=l)
        if l == 0:
            vfirst = v
        vcol = _to_col(v)
        ocol, sprev, gathered = _rwkv_scan_fwd(r, w, k2, vcol, av, bv, ee, e16, e16t, l,
                                               gather=later_shards if l == 0 else [])
        if l == 0:
            w_in[1], w_out = assemble_later(gathered)
        o = _from_col(ocol)
        post_args = (o, r, k2, v, z, row(ln_x_w, l), row(ln_x_b, l), row(r_k, l), ee)
        cat = _rwkv_post_fwd(*post_args, layer=l)
        cat, shist = _hgrn_fwd(proj, cat, lb_logits, row(g_norm_w, l), l)
        y = _matmul(cat, w_out, b_layer=l, mode="nn", tm=512, tn=1024, tk=D_MODEL, name=f"out_fwd_l{l}")
        h_new = _ln_fwd(h, y, row(ln_w, l), row(ln_b, l), l)
        saved.append(dict(h=h, proj=proj, pre_args=pre_args, post_args=post_args, scan=(r, w, k2, vcol, av, bv, sprev),
                          cat=cat, shist=shist, y=y))
        h = h_new

    loss_sum, g = _loss_head(h, target)

    grads = {n: [None] * DEPTH for n in ("w_in", "w_out", "shift_mu", "w_decay0", "w_decay_up", "a0", "a_up", "k_k",
                                         "k_a", "r_k", "ln_x_w", "ln_x_b", "g_norm_w", "ln_w", "ln_b")}
    dlbl = []
    dvfirst = None
    to_send = []
    for l in reversed(range(DEPTH)):
        s = saved[l]
        dh_res, dy, dlnw, dlnb = _ln_bwd(s["h"], s["y"], row(ln_w, l), row(ln_b, l), g, l)
        grads["ln_w"][l], grads["ln_b"][l] = dlnw[0], dlnb[0]
        dcat = _matmul(dy, w_out, b_layer=l, mode="nt", tm=512, tn=1024, tk=D_MODEL, name=f"out_bwd_dx_l{l}")
        grads["w_out"][l] = _matmul(s["cat"], dy, mode="tn", tm=1024, tn=1024, tk=512, name=f"out_bwd_dw_l{l}")
        do, dr_p, dk_p, dv_p, dz, dgw, dgb, drk = _rwkv_post_bwd(*s["post_args"], dcat, layer=l)
        grads["ln_x_w"][l], grads["ln_x_b"][l], grads["r_k"][l] = dgw[0], dgb[0], drk[0]
        r, w, k2, vcol, av, bv, sprev = s["scan"]
        dr_s, dw_s, dk_s, da_s, db_s, dvcol, arrived = _rwkv_scan_bwd(r, w, k2, vcol, av, bv, sprev, _to_col(do), ee,
                                                                     e16, e16t, l, send=to_send if l == 0 else [])
        if l == 0:
            on_layer1_recv(arrived)
        v_cots = [dv_p, _from_col(dvcol)] + ([dvfirst] if (l == 0 and dvfirst is not None) else [])
        cots = dict(r=[dr_p, dr_s], w=dw_s, k=[dk_p, dk_s], v=v_cots, a=da_s, b=db_s, z=dz)
        outs = _rwkv_pre_bwd(*s["pre_args"], cots, layer=l)
        dproj, dmu, dw0, dwup, da0, daup, dkkw, dkaw = outs[:8]
        grads["shift_mu"][l], grads["w_decay0"][l], grads["a0"][l] = dmu[0], dw0[0], da0[0]
        grads["k_k"][l], grads["k_a"][l] = dkkw[0], dkaw[0]
        grads["w_decay_up"][l], grads["a_up"][l] = dwup[:LORA], daup[LORA:]
        if l > 0:
            dv0, dvdn, dvup, dvfirst = outs[8:]
            g_vmix = (dv0, dvdn[:, :VRES][None], dvup[:VRES][None])
        dq, df, di, dzh, dl, dgn = _hgrn_bwd(s["proj"], s["shist"], dcat, lb_logits, row(g_norm_w, l), l)
        dlbl.append(dl)
        grads["g_norm_w"][l] = dgn[0]
        dproj = jnp.concatenate([dproj, dq, df, di, dzh], axis=1)
        grads["w_in"][l] = _matmul(s["h"], dproj, mode="tn", tm=1024, tn=1664, tk=512, name=f"proj_bwd_dw_l{l}")
        g = _matmul(dproj, w_in[l], mode="nt", tm=512, tn=1024, tk=1664, name=f"proj_bwd_dx_l{l}", add=dh_res)
        if l == 1:
            to_send = on_layer1_grads(grads["w_in"][1], grads["w_out"][1])
    w_in_grads, w_out_grads = grads.pop("w_in"), grads.pop("w_out")
    out = {n: jnp.stack(v) for n, v in grads.items()}
    out["w_in"], out["w_out"] = w_in_grads, w_out_grads
    out["lb_logits"] = dlbl[0] + dlbl[1]
    out["v_mix0"], out["v_mix_down"], out["v_mix_up"] = g_vmix
    return loss_sum, g, out


CHIP_COMBOS = ((1, 0), (0, 1), (1, 1))
HBM_SPEC = pl.BlockSpec(memory_space=pl.ANY)


def _mesh_pos():
    return lax.axis_index("x"), lax.axis_index("y"), lax.axis_index("c")


def _flip(v, d):
    return 1 - v if d else v


def _half(ref, dim, cc, hs):
    idx = [slice(None)] * len(ref.shape)
    idx[dim] = pl.ds(cc * hs, hs)
    return ref.at[tuple(idx)]


def _gather_shards(arrs):
    n = len(arrs)

    def body(*refs):
        ins, outs = refs[:n], refs[n:2 * n]
        sems = refs[2 * n:]
        _gather_start(ins, outs, *sems)
        _gather_forward(ins, outs, *sems)
        _gather_finish(ins, outs, *sems)

    return pl.pallas_call(
        body, name="gather_shards", in_specs=[HBM_SPEC] * n, out_specs=[HBM_SPEC] * n,
        out_shape=_gather_out_shapes(arrs), scratch_shapes=_gather_sems(n),
    )(*arrs)


def _gather_out_shapes(arrs):
    return [SDS((4,) + a.shape, a.dtype) for a in arrs]


def _gather_sems(n):
    return [pltpu.SemaphoreType.DMA((n, 6)), pltpu.SemaphoreType.DMA((n, 6)), pltpu.SemaphoreType.DMA((n,))]


def _gather_copies(kind, ins, outs, send_sems, recv_sems, local_sems):
    x, y, c = _mesh_pos()
    kc = 2 * x + y
    cps = []
    for a in range(len(ins)):
        if kind == "local":
            cps.append(pltpu.make_async_copy(ins[a], outs[a].at[kc], local_sems.at[a]))
            continue
        for j, (dx, dy) in enumerate(CHIP_COMBOS):
            px, py = _flip(x, dx), _flip(y, dy)
            kj = 2 * px + py
            src, dst, s, to = {
                "sends": (ins[a].at[c], outs[a].at[kc, c], j, (px, py, c)),
                "landed": (outs[a].at[kj, c], outs[a].at[kj, c], j, (px, py, c)),
                "forwards": (outs[a].at[kj, c], outs[a].at[kj, c], 3 + j, (x, y, 1 - c)),
                "passed": (outs[a].at[kj, 1 - c], outs[a].at[kj, 1 - c], 3 + j, (x, y, 1 - c)),
            }[kind]
            cps.append(pltpu.make_async_remote_copy(
                src_ref=src, dst_ref=dst, send_sem=send_sems.at[a, s], recv_sem=recv_sems.at[a, s], device_id=to,
                device_id_type=MESH))
    return cps


def _gather_start(*refs):
    for cp in _gather_copies("local", *refs) + _gather_copies("sends", *refs):
        cp.start()


def _gather_forward(*refs):
    for arrival, fw in zip(_gather_copies("landed", *refs), _gather_copies("forwards", *refs)):
        arrival.wait_recv()
        fw.start()


def _gather_finish(*refs):
    for cp in _gather_copies("passed", *refs):
        cp.wait_recv()
    for cp in _gather_copies("sends", *refs) + _gather_copies("forwards", *refs):
        cp.wait_send()
    for cp in _gather_copies("local", *refs):
        cp.wait()


DEV_COMBOS = tuple((dx, dy, dc) for dx in (0, 1) for dy in (0, 1) for dc in (0, 1))[1:]


def _allreduce_small(buf):
    R = buf.shape[0]

    def body(x_ref, o_ref, slots, send_sems, recv_sems):
        x, y, c = _mesh_pos()
        me = 4 * x + 2 * y + c
        cps = []
        for k, (dx, dy, dc) in enumerate(DEV_COMBOS):
            cp = pltpu.make_async_remote_copy(
                src_ref=x_ref, dst_ref=slots.at[k + 1], send_sem=send_sems.at[k], recv_sem=recv_sems.at[k],
                device_id=(_flip(x, dx), _flip(y, dy), _flip(c, dc)), device_id_type=MESH)
            cp.start()
            cps.append(cp)
        slots[0] = x_ref[...]
        for cp in cps:
            cp.wait_recv()
        acc = slots[jnp.bitwise_xor(me, 0)]
        for d in range(1, 8):
            acc = acc + slots[jnp.bitwise_xor(me, d)]
        o_ref[...] = acc
        for cp in cps:
            cp.wait_send()

    vm = pl.BlockSpec(memory_space=pltpu.VMEM)
    return pl.pallas_call(
        body, name="allreduce_small", in_specs=[vm], out_specs=vm, out_shape=SDS((R, LANES), f32),
        scratch_shapes=[pltpu.VMEM((8, R, LANES), f32), pltpu.SemaphoreType.DMA((7,)), pltpu.SemaphoreType.DMA((7,))],
        compiler_params=_params(None, 48),
    )(buf)


def _swap_halves(gs, tag):
    n = len(gs)

    def body(*refs):
        ins, outs = refs[:n], refs[n:2 * n]
        send_sems, recv_sems = refs[2 * n:]
        x, y, c = _mesh_pos()
        cps = []
        for a in range(n):
            h = ins[a].shape[1] // 2
            cp = pltpu.make_async_remote_copy(
                src_ref=_half(ins[a], 1, 1 - c, h), dst_ref=outs[a], send_sem=send_sems.at[a],
                recv_sem=recv_sems.at[a], device_id=(x, y, 1 - c), device_id_type=MESH)
            cp.start()
            cps.append(cp)
        for cp in cps:
            cp.wait()

    return pl.pallas_call(
        body, name=f"rs_swap_halves_{tag}", in_specs=[HBM_SPEC] * n, out_specs=[HBM_SPEC] * n,
        out_shape=[SDS((g.shape[0], g.shape[1] // 2, g.shape[2]), g.dtype) for g in gs],
        scratch_shapes=[pltpu.SemaphoreType.DMA((n,)), pltpu.SemaphoreType.DMA((n,))],
    )(*gs)


def _chips_copies(ins, outs, send_sems, recv_sems):
    x, y, c = _mesh_pos()
    cps = []
    for a in range(len(ins)):
        for j, (dx, dy) in enumerate(CHIP_COMBOS):
            px, py = _flip(x, dx), _flip(y, dy)
            cps.append(pltpu.make_async_remote_copy(
                src_ref=ins[a].at[2 * px + py], dst_ref=outs[a].at[j], send_sem=send_sems.at[a, j],
                recv_sem=recv_sems.at[a, j], device_id=(px, py, c), device_id_type=MESH))
    return cps


def _chips_out_shapes(ps):
    return [SDS((3,) + p.shape[1:], p.dtype) for p in ps]


def _chips_sems(n):
    return [pltpu.SemaphoreType.DMA((n, 3)), pltpu.SemaphoreType.DMA((n, 3))]


def _send_to_chips(ps, tag):
    n = len(ps)

    def body(*refs):
        cps = _chips_copies(refs[:n], refs[n:2 * n], *refs[2 * n:])
        for cp in cps:
            cp.start()
        for cp in cps:
            cp.wait()

    return pl.pallas_call(
        body, name=f"rs_send_to_chips_{tag}", in_specs=[HBM_SPEC] * n, out_specs=[HBM_SPEC] * n,
        out_shape=_chips_out_shapes(ps), scratch_shapes=_chips_sems(n),
    )(*ps)


def _share_halves(qs, tag):
    n = len(qs)

    def body(*refs):
        ins, outs = refs[:n], refs[n:2 * n]
        send_sems, recv_sems = refs[2 * n:]
        x, y, c = _mesh_pos()
        cps = []
        for a in range(n):
            cp = pltpu.make_async_remote_copy(
                src_ref=ins[a], dst_ref=outs[a], send_sem=send_sems.at[a], recv_sem=recv_sems.at[a],
                device_id=(x, y, 1 - c), device_id_type=MESH)
            cp.start()
            cps.append(cp)
        for cp in cps:
            cp.wait()

    return pl.pallas_call(
        body, name=f"rs_share_halves_{tag}", in_specs=[HBM_SPEC] * n, out_specs=[HBM_SPEC] * n,
        out_shape=[SDS(q.shape, q.dtype) for q in qs],
        scratch_shapes=[pltpu.SemaphoreType.DMA((n,)), pltpu.SemaphoreType.DMA((n,))],
    )(*qs)


def _add_own_half(g, recv, c, name):
    S, R, C = g.shape
    h = R // 2
    tr = min(256, h)
    nt = h // tr

    def body(c_ref, g_ref, r_ref, o_ref, o16_ref):
        del c_ref
        s = g_ref[...] + r_ref[...]
        o_ref[...] = s
        o16_ref[...] = s.astype(bf16)

    ospec = pl.BlockSpec((1, tr, C), lambda s, i, cr: (s, i, 0))
    return pl.pallas_call(
        body, name=name,
        grid_spec=pltpu.PrefetchScalarGridSpec(
            num_scalar_prefetch=1, grid=(S, nt),
            in_specs=[pl.BlockSpec((1, tr, C), lambda s, i, cr: (s, cr[0] * nt + i, 0)), ospec],
            out_specs=[ospec, ospec]),
        out_shape=[SDS((S, h, C), f32), SDS((S, h, C), bf16)],
        compiler_params=_params(("parallel", "parallel"), 40),
    )(c, g, recv)


def _add_chip_parts(p, recv, kc, name):
    _, R, C = p.shape
    tr = min(256, R)

    def body(k_ref, p_ref, r0, r1, r2, o_ref):
        del k_ref
        o_ref[...] = ((p_ref[0] + r0[0].astype(f32)) + r1[0].astype(f32)) + r2[0].astype(f32)

    rspec = lambda j: pl.BlockSpec((1, tr, C), lambda i, kr: (j, i, 0))
    return pl.pallas_call(
        body, name=name,
        grid_spec=pltpu.PrefetchScalarGridSpec(
            num_scalar_prefetch=1, grid=(R // tr,),
            in_specs=[pl.BlockSpec((1, tr, C), lambda i, kr: (kr[0], i, 0)), rspec(0), rspec(1), rspec(2)],
            out_specs=pl.BlockSpec((tr, C), lambda i, kr: (i, 0))),
        out_shape=SDS((R, C), f32), compiler_params=_params(("parallel",), 40),
    )(kc, p, recv, recv, recv)


def _rs_pair(gs, c, tag):
    got = _swap_halves(gs, tag)
    pairs = [_add_own_half(g, r, c, f"rs_add_pair_{tag}_{a}") for a, (g, r) in enumerate(zip(gs, got))]
    return [p[0] for p in pairs], [p[1] for p in pairs]


def _rs_finish(ps, arrived, c, kc, tag):
    qs = [_add_chip_parts(p, r, kc, f"rs_add_chips_{tag}_{a}") for a, (p, r) in enumerate(zip(ps, arrived))]
    return qs, _share_halves(qs, tag)


def _adamw_math(w, g, m, v):
    m2 = ADAM_B1 * m + (1.0 - ADAM_B1) * g
    v2 = ADAM_B2 * v + (1.0 - ADAM_B2) * (g * g)
    m_hat = m2 / (1.0 - ADAM_B1 ** ADAM_STEP)
    v_hat = v2 / (1.0 - ADAM_B2 ** ADAM_STEP)
    return -ADAM_LR * (m_hat / (jnp.sqrt(v_hat) + ADAM_EPS) + ADAM_WD * w), m2, v2


def _adamw_shard(w, g_own, g_other, m, v, c, name):
    L, R, C = w.shape
    h = R // 2
    tr = min(128, h)
    nh = h // tr

    def body(c_ref, w_ref, go0, gx0, go1, gx1, m_ref, v_ref, g_o, d_o, m_o, v_o):
        mine = (pl.program_id(1) // nh) == c_ref[0]
        first = pl.program_id(0) == 0
        g = jnp.where(mine, jnp.where(first, go0[...], go1[...]), jnp.where(first, gx0[...], gx1[...]))[None]
        d, m2, v2 = _adamw_math(w_ref[...], g, m_ref[...], v_ref[...])
        g_o[...], d_o[...], m_o[...], v_o[...] = g, d, m2, v2

    full = pl.BlockSpec((1, tr, C), lambda l, i, cr: (l, i, 0))
    half = pl.BlockSpec((tr, C), lambda l, i, cr: (i % nh, 0))
    return pl.pallas_call(
        body, name=name,
        grid_spec=pltpu.PrefetchScalarGridSpec(num_scalar_prefetch=1, grid=(L, R // tr),
                                               in_specs=[full, half, half, half, half, full, full],
                                               out_specs=[full] * 4),
        out_shape=[SDS((L, R, C), f32)] * 4, compiler_params=_params(("parallel", "parallel"), 48),
    )(c, w, g_own[0], g_other[0], g_own[1], g_other[1], m, v)


def _adamw(w, g, m, v, name):
    R, C = w.shape
    tr = 256 if R % 256 == 0 else R

    def body(w_ref, g_ref, m_ref, v_ref, d_o, m_o, v_o):
        d_o[...], m_o[...], v_o[...] = _adamw_math(w_ref[...], g_ref[...], m_ref[...], v_ref[...])

    blk = pl.BlockSpec((tr, C), lambda i: (i, 0))
    return pl.pallas_call(body, name=name, grid=(R // tr,), in_specs=[blk] * 4, out_specs=[blk] * 3,
                          out_shape=[SDS((R, C), f32)] * 3, compiler_params=_params(("parallel",), 40))(w, g, m, v)


def _pack(arrs):
    flat = [a.reshape(-1, LANES) for a in arrs]
    rows = sum(f.shape[0] for f in flat)
    pad = (-rows) % 16
    if pad:
        flat.append(jnp.zeros((pad, LANES), f32))
    return jnp.concatenate(flat, axis=0)


def _unpack(buf, shapes):
    out, r = [], 0
    for s in shapes:
        n = math.prod(s) // LANES
        out.append(buf[r:r + n].reshape(s))
        r += n
    return out


REPLICATED = ("shift_mu", "w_decay0", "a0", "k_k", "k_a", "r_k", "ln_x_w", "ln_x_b", "v_mix0", "lb_logits", "g_norm_w",
              "ln_w", "ln_b")
CHIP_SMALL = (("w_decay_up", 2), ("a_up", 2), ("v_mix_down", 1), ("v_mix_up", 2))
WEIGHTS = ("w_in", "shift_mu", "w_decay0", "w_decay_up", "a0", "a_up", "k_k", "k_a", "r_k", "ln_x_w", "ln_x_b",
           "v_mix0", "v_mix_down", "v_mix_up", "lb_logits", "g_norm_w", "w_out", "ln_w", "ln_b")


def kernel(x, w_in, shift_mu, w_decay0, w_decay_up, a0, a_up, k_k, k_a, r_k, ln_x_w, ln_x_b, v_mix0, v_mix_down, v_mix_up, lb_logits, g_norm_w, w_out, ln_w, ln_b, loss_target, m_w_in, m_shift_mu, m_w_decay0, m_w_decay_up, m_a0, m_a_up, m_k_k, m_k_a, m_r_k, m_ln_x_w, m_ln_x_b, m_v_mix0, m_v_mix_down, m_v_mix_up, m_lb_logits, m_g_norm_w, m_w_out, m_ln_w, m_ln_b, v_w_in, v_shift_mu, v_w_decay0, v_w_decay_up, v_a0, v_a_up, v_k_k, v_k_a, v_r_k, v_ln_x_w, v_ln_x_b, v_v_mix0, v_v_mix_down, v_v_mix_up, v_lb_logits, v_g_norm_w, v_w_out, v_ln_w, v_ln_b):
    W = dict(w_in=w_in, shift_mu=shift_mu, w_decay0=w_decay0, w_decay_up=w_decay_up, a0=a0, a_up=a_up, k_k=k_k, k_a=k_a,
             r_k=r_k, ln_x_w=ln_x_w, ln_x_b=ln_x_b, v_mix0=v_mix0, v_mix_down=v_mix_down, v_mix_up=v_mix_up,
             lb_logits=lb_logits, g_norm_w=g_norm_w, w_out=w_out, ln_w=ln_w, ln_b=ln_b)
    M = dict(w_in=m_w_in, shift_mu=m_shift_mu, w_decay0=m_w_decay0, w_decay_up=m_w_decay_up, a0=m_a0, a_up=m_a_up,
             k_k=m_k_k, k_a=m_k_a, r_k=m_r_k, ln_x_w=m_ln_x_w, ln_x_b=m_ln_x_b, v_mix0=m_v_mix0,
             v_mix_down=m_v_mix_down, v_mix_up=m_v_mix_up, lb_logits=m_lb_logits, g_norm_w=m_g_norm_w, w_out=m_w_out,
             ln_w=m_ln_w, ln_b=m_ln_b)
    V = dict(w_in=v_w_in, shift_mu=v_shift_mu, w_decay0=v_w_decay0, w_decay_up=v_w_decay_up, a0=v_a0, a_up=v_a_up,
             k_k=v_k_k, k_a=v_k_a, r_k=v_r_k, ln_x_w=v_ln_x_w, ln_x_b=v_ln_x_b, v_mix0=v_v_mix0,
             v_mix_down=v_v_mix_down, v_mix_up=v_v_mix_up, lb_logits=v_lb_logits, g_norm_w=v_g_norm_w, w_out=v_w_out,
             ln_w=v_ln_w, ln_b=v_ln_b)
    mx, my, mc = _mesh_pos()
    kc = (2 * mx + my).astype(jnp.int32)
    c_arr = jnp.reshape(mc.astype(jnp.int32), (1,))
    kc_arr = jnp.reshape(kc, (1,))

    small_shard = _pack([W[n] for n, _ in CHIP_SMALL])
    in_halves = w_in.astype(bf16).reshape(DEPTH, 2, D_MODEL // 2, SHARD_COLS)
    out_halves = jnp.swapaxes(w_out.astype(bf16).reshape(DEPTH, 2, D_MODEL // 8, D_MODEL), 0, 1)
    g_in0, g_small = _gather_shards([in_halves[0], small_shard.reshape(2, -1, LANES)])
    in_full = lambda g: jnp.transpose(g, (1, 2, 0, 3)).reshape(D_MODEL, IN_COLS)
    shard_shapes = [W[n].shape for n, _ in CHIP_SMALL]
    g_small = g_small.reshape(4, -1, LANES)
    per_chip = [_unpack(g_small[k], shard_shapes) for k in range(4)]
    full_small = {n: jnp.concatenate([per_chip[k][i] for k in range(4)], axis=ax)
                  for i, (n, ax) in enumerate(CHIP_SMALL)}

    def assemble_later(gathered):
        g_in1, g_out = gathered
        return in_full(g_in1), jnp.transpose(g_out, (2, 0, 1, 3, 4)).reshape(DEPTH, D_MODEL, D_MODEL)

    in_slabs = lambda g: jnp.transpose(g.reshape(D_MODEL, 4, SHARD_COLS), (1, 0, 2))
    out_slabs = lambda g: g.reshape(4, D_MODEL // 4, D_MODEL)
    layer1 = {}

    def on_layer1_grads(dw_in, dw_out):
        layer1["pairs"], to_send = _rs_pair([in_slabs(dw_in), out_slabs(dw_out)], c_arr, "l1")
        return to_send

    def on_layer1_recv(arrived):
        layer1["arrived"] = arrived

    loss_sum, gx, G = _device_step(
        x[0], loss_target[0], in_full(g_in0), [in_halves[1], out_halves], assemble_later, on_layer1_grads,
        on_layer1_recv, shift_mu, w_decay0, full_small["w_decay_up"], a0, full_small["a_up"], k_k, k_a, r_k, ln_x_w,
        ln_x_b, v_mix0, full_small["v_mix_down"], full_small["v_mix_up"], lb_logits, g_norm_w, ln_w, ln_b)
    loss = lax.psum(loss_sum[0, 0], ("x", "y", "c"))

    own1, other1 = _rs_finish(layer1["pairs"], layer1["arrived"], c_arr, kc_arr, "l1")
    pairs0, to_send0 = _rs_pair([in_slabs(G["w_in"][0]), out_slabs(G["w_out"][0])], c_arr, "l0")
    own0, other0 = _rs_finish(pairs0, _send_to_chips(to_send0, "l0"), c_arr, kc_arr, "l0")
    gin_own, gout_own = [own0[0], own1[0]], [own0[1], own1[1]]
    gin_other, gout_other = [other0[0], other1[0]], [other0[1], other1[1]]

    small_names = list(REPLICATED) + [n for n, _ in CHIP_SMALL]
    small_sum = _allreduce_small(_pack([G[n] for n in small_names]))
    small_full = dict(zip(small_names, _unpack(small_sum, [G[n].shape for n in small_names])))
    grads = {n: small_full[n] for n in REPLICATED}
    for n, ax in CHIP_SMALL:
        width = W[n].shape[ax]
        grads[n] = lax.dynamic_slice_in_dim(small_full[n], kc * width, width, axis=ax)

    delta, new_m, new_v = {}, {}, {}
    for n, own, other in (("w_in", gin_own, gin_other), ("w_out", gout_own, gout_other)):
        grads[n], delta[n], new_m[n], new_v[n] = _adamw_shard(W[n], own, other, M[n], V[n], c_arr, f"adamw_{n}")
    shapes = [W[n].shape for n in small_names]
    d_, m_, v_ = _adamw(_pack([W[n] for n in small_names]), _pack([grads[n] for n in small_names]),
                        _pack([M[n] for n in small_names]), _pack([V[n] for n in small_names]), "adamw_small")
    for n, d1, m1, v1 in zip(small_names, _unpack(d_, shapes), _unpack(m_, shapes), _unpack(v_, shapes)):
        delta[n], new_m[n], new_v[n] = d1, m1, v1

    return (loss, gx[None], *[grads[n] for n in WEIGHTS], *[delta[n] for n in WEIGHTS],
            *[new_m[n] for n in WEIGHTS], *[new_v[n] for n in WEIGHTS])
```

```python
import functools
import math

import jax
import jax.numpy as jnp
from jax import lax
from jax.experimental import pallas as pl
from jax.experimental.pallas import tpu as pltpu

f32 = jnp.float32
bf16 = jnp.bfloat16
SDS = jax.ShapeDtypeStruct
MESH = pl.DeviceIdType.MESH

D_MODEL = 2048
DEPTH = 2
D_RWKV = 1024
D_HGRN = 1024
RWKV_HEAD = 64
RWKV_HEADS = 16
LORA = 64
VRES = 32
HG_HEAD = 128
HG_HEADS = 8
CHUNK = 64
RWKV_COLS = 4 * D_RWKV + 2 * LORA
IN_COLS = RWKV_COLS + 4 * D_HGRN
SHARD_COLS = IN_COLS // 4
LANES = 128
RW_BLKS = RWKV_COLS // LANES
ALPHA = (2 * DEPTH) ** 0.25
LN_EPS = 1e-5
GN_EPS = 64e-5
RMS_EPS = 1e-5
LB_FLOOR = 1e-30
EXP_M05 = math.exp(-0.5)
ADAM_LR, ADAM_B1, ADAM_B2, ADAM_EPS, ADAM_WD, ADAM_STEP = 0.001, 0.9, 0.999, 1e-08, 0.01, 10

TB = 128
SCAN_TB = 8
SCAN_PARTS = 2
HSUB = 32
HIGHEST = lax.Precision.HIGHEST


def _params(sem=None, vmem_mb=None):
    kw = {}
    if sem is not None:
        kw["dimension_semantics"] = sem
    if vmem_mb is not None:
        kw["vmem_limit_bytes"] = vmem_mb << 20
    return pltpu.CompilerParams(**kw)


def _split(x, n):
    parts, r = [], x
    for i in range(n):
        p = r.astype(bf16)
        parts.append(p)
        if i + 1 < n:
            r = r - p.astype(f32)
    return parts


def _dot_parts(x, m, n=3):
    R = x.shape[0]
    res = jnp.dot(jnp.concatenate(_split(x, n), axis=0), m, preferred_element_type=f32)
    acc = res[0:R]
    for i in range(1, n):
        acc = acc + res[i * R:(i + 1) * R]
    return acc


def _seg_raw(x, ee, n=3, per_tile=False):
    R = x.shape[0]
    if per_tile:
        outs = []
        for c in range(4):
            tile_parts = _split(x[:, 256 * c:256 * c + 256], n)
            res = jnp.dot(jnp.concatenate(tile_parts, axis=0), ee, preferred_element_type=f32)
            acc = res[0:R]
            for i in range(1, n):
                acc = acc + res[i * R:(i + 1) * R]
            outs.append(acc)
        return jnp.concatenate(outs, axis=1)
    parts = _split(x, n)
    lhs = jnp.concatenate([p[:, 256 * c:256 * c + 256] for p in parts for c in range(4)], axis=0)
    res = jnp.dot(lhs, ee, preferred_element_type=f32)

    def tile(c):
        acc = res[c * R:(c + 1) * R]
        for i in range(1, n):
            acc = acc + res[(4 * i + c) * R:(4 * i + c + 1) * R]
        return acc

    return jnp.concatenate([tile(c) for c in range(4)], axis=1)


@jax.custom_vjp
def _seg(x, ee):
    return _seg_raw(x, ee)


def _seg_fwd(x, ee):
    return _seg_raw(x, ee), ee


def _seg_bwd(ee, g):
    return _seg_raw(g, ee), jnp.zeros_like(ee)


_seg.defvjp(_seg_fwd, _seg_bwd)


def _const_mats():
    i256 = jnp.arange(256) // RWKV_HEAD
    ee = (i256[:, None] == i256[None, :]).astype(bf16)
    head = jnp.arange(D_RWKV) // RWKV_HEAD
    lane = jnp.arange(LANES)
    e16 = (head[:, None] == lane[None, :]).astype(bf16)
    e16t3 = ((lane[:, None] % RWKV_HEADS == head[None, :]) & (lane[:, None] < 3 * RWKV_HEADS)).astype(bf16)
    return ee, e16, e16t3


def _matmul(a, b, *, mode, tm, tn, tk, name, add=None, b_layer=None):
    bs = b.shape if b_layer is None else b.shape[1:]
    lead = () if b_layer is None else (None,)
    bidx = (lambda *t: t) if b_layer is None else (lambda *t: (b_layer,) + t)
    if mode == "nn":
        (M, K), N = a.shape, bs[1]
        tm, tn, tk = min(tm, M), min(tn, N), min(tk, K)
        a_spec = pl.BlockSpec((tm, tk), lambda i, j, k: (i, k))
        b_spec = pl.BlockSpec(lead + (tk, tn), lambda i, j, k: bidx(k, j))
        dims = (((1,), (0,)), ((), ()))
    elif mode == "nt":
        (M, K), N = a.shape, bs[0]
        tm, tn, tk = min(tm, M), min(tn, N), min(tk, K)
        a_spec = pl.BlockSpec((tm, tk), lambda i, j, k: (i, k))
        b_spec = pl.BlockSpec(lead + (tn, tk), lambda i, j, k: bidx(j, k))
        dims = (((1,), (1,)), ((), ()))
    else:
        (K, M), N = a.shape, bs[1]
        tm, tn, tk = min(tm, M), min(tn, N), min(tk, K)
        a_spec = pl.BlockSpec((tk, tm), lambda i, j, k: (k, i))
        b_spec = pl.BlockSpec(lead + (tk, tn), lambda i, j, k: bidx(k, j))
        dims = (((0,), (0,)), ((), ()))
    assert M % tm == 0 and N % tn == 0 and K % tk == 0, (M, N, K, tm, tn, tk)
    has_add = add is not None

    def body(*refs):
        if has_add:
            a_ref, b_ref, c_ref, o_ref = refs
        else:
            a_ref, b_ref, o_ref = refs
        k = pl.program_id(2)
        p = lax.dot_general(a_ref[...].astype(bf16), b_ref[...].astype(bf16), dims, preferred_element_type=f32)

        @pl.when(k == 0)
        def _():
            o_ref[...] = p + c_ref[...] if has_add else p

        @pl.when(k > 0)
        def _():
            o_ref[...] += p

    o_spec = pl.BlockSpec((tm, tn), lambda i, j, k: (i, j))
    in_specs = [a_spec, b_spec] + ([o_spec] if has_add else [])
    args = (a, b) + ((add,) if has_add else ())
    return pl.pallas_call(
        body, name=name, grid=(M // tm, N // tn, K // tk), in_specs=in_specs, out_specs=o_spec,
        out_shape=SDS((M, N), f32),
        compiler_params=_params(("parallel", "parallel", "arbitrary"), 56),
    )(*args)


def _rwkv_core(k_in, v_in, wdad, w0, wup, a0, aup, kkw, kaw, ee, vmix):
    w_raw = w0 + jnp.dot(jnp.tanh(wdad), wup, preferred_element_type=f32)
    decay = jnp.exp(-EXP_M05 * jax.nn.sigmoid(w_raw))
    a = jax.nn.sigmoid(a0 + jnp.dot(wdad, aup, preferred_element_type=f32))
    if vmix is None:
        v = v_in
    else:
        v0, vdn, vup, vfirst = vmix
        gate = jax.nn.sigmoid(v0 + jnp.dot(jnp.dot(v_in, vdn, preferred_element_type=f32), vup,
                                           preferred_element_type=f32))
        v = v_in + (vfirst - v_in) * gate
    kk = k_in * kkw
    n2 = _seg(kk * kk, ee)
    kk = kk / jnp.maximum(jnp.sqrt(n2), 1e-12)
    k2 = k_in * (1.0 + (a - 1.0) * kaw)
    return decay, k2, v, -kk, kk * a


def _shifted(y, prev_row):
    yp = pltpu.roll(y, 1, 0)
    row = lax.broadcasted_iota(jnp.int32, y.shape, 0)
    return jnp.where(row == 0, prev_row, yp)


def _rwkv_pre_fwd(proj, mu, w0, wup, a0, aup, kkw, kaw, ee, vmix, layer):
    T = proj.shape[0]
    nb = T // TB
    has_mix = vmix is not None

    def body(*refs):
        y_ref, p8_ref, mu_ref, w0_ref, wup_ref, a0_ref, aup_ref, kkw_ref, kaw_ref, ee_ref = refs[:10]
        rest = refs[10:]
        if has_mix:
            v0_ref, vdn_ref, vup_ref, vf_ref = rest[:4]
            rest = rest[4:]
        r_o, w_o, k_o, v_o, a_o, b_o, z_o = rest
        i = pl.program_id(0)
        y = y_ref[...]
        prev = jnp.where(i == 0, 0.0, p8_ref[7:8, :])
        rw = y + mu_ref[...] * (_shifted(y, prev) - y)
        mix = (v0_ref[...], vdn_ref[...], vup_ref[...], vf_ref[...]) if has_mix else None
        dec, k2, v, av, bv = _rwkv_core(rw[:, 1024:2048], rw[:, 2048:3072], rw[:, 4096:4224], w0_ref[...],
                                        wup_ref[...], a0_ref[...], aup_ref[...], kkw_ref[...], kaw_ref[...],
                                        ee_ref[...], mix)
        r_o[...] = rw[:, 0:1024]
        w_o[...] = dec
        k_o[...] = k2
        v_o[...] = v
        a_o[...] = av
        b_o[...] = bv
        z_o[...] = rw[:, 3072:4096]

    row = lambda n: pl.BlockSpec((1, n), lambda i: (0, 0))
    full = lambda s: pl.BlockSpec(s, lambda i: (0,) * len(s))
    blk = pl.BlockSpec((TB, D_RWKV), lambda i: (i, 0))
    in_specs = [pl.BlockSpec((TB, RWKV_COLS), lambda i: (i, 0)),
                pl.BlockSpec((8, RWKV_COLS), lambda i: (jnp.maximum(i * (TB // 8) - 1, 0), 0)),
                row(RWKV_COLS), row(D_RWKV), full((LANES, D_RWKV)), row(D_RWKV), full((LANES, D_RWKV)),
                row(D_RWKV), row(D_RWKV), full((256, 256))]
    args = [proj, proj, mu, w0, wup, a0, aup, kkw, kaw, ee]
    if has_mix:
        v0, vdn, vup, vfirst = vmix
        in_specs += [row(D_RWKV), full((D_RWKV, LANES)), full((LANES, D_RWKV)), blk]
        args += [v0, vdn, vup, vfirst]
    return pl.pallas_call(
        body, name=f"rwkv_pre_fwd_l{layer}", grid=(nb,), in_specs=in_specs, out_specs=[blk] * 7,
        out_shape=[SDS((T, D_RWKV), f32)] * 7, compiler_params=_params(("parallel",), 48),
    )(*args)


def _rwkv_pre_bwd(proj, mu, w0, wup, a0, aup, kkw, kaw, ee, vmix, cots, layer):
    T = proj.shape[0]
    nb = T // TB
    has_mix = vmix is not None
    cot_list = cots["r"] + [cots["w"]] + cots["k"] + cots["v"] + [cots["a"], cots["b"], cots["z"]]
    n_r, n_k, n_v = len(cots["r"]), len(cots["k"]), len(cots["v"])
    n_cot = len(cot_list)

    def body(*refs):
        y_ref, p8_ref, mu_ref, w0_ref, wup_ref, a0_ref, aup_ref, kkw_ref, kaw_ref, ee_ref = refs[:10]
        rest = refs[10:]
        if has_mix:
            v0_ref, vdn_ref, vup_ref, vf_ref = rest[:4]
            rest = rest[4:]
        cot_refs, rest = rest[:n_cot], rest[n_cot:]
        if has_mix:
            (dproj_o, dmu_o, dw0_o, dwup_o, da0_o, daup_o, dkkw_o, dkaw_o,
             dv0_o, dvdn_o, dvup_o, dvf_o, carry) = rest
        else:
            dproj_o, dmu_o, dw0_o, dwup_o, da0_o, daup_o, dkkw_o, dkaw_o, carry = rest
        i = pl.program_id(0)
        blk_i = nb - 1 - i
        y = y_ref[...]
        prev = jnp.where(blk_i == 0, 0.0, p8_ref[7:8, :])
        yp = _shifted(y, prev)
        mu_v = mu_ref[...]
        rw = y + mu_v * (yp - y)

        def sum_refs(rs):
            acc = rs[0][...]
            for r_ in rs[1:]:
                acc = acc + r_[...]
            return acc

        c = list(cot_refs)
        g_r = sum_refs(c[:n_r]); c = c[n_r:]
        g_w = c[0][...]; c = c[1:]
        g_k = sum_refs(c[:n_k]); c = c[n_k:]
        g_v = sum_refs(c[:n_v]); c = c[n_v:]
        g_a, g_b, g_z = c[0][...], c[1][...], c[2][...]

        ee_v = ee_ref[...]
        if has_mix:
            def fn(k_in, v_in, wdad, w0_, wup_, a0_, aup_, kkw_, kaw_, v0_, vdn_, vup_, vf_):
                return _rwkv_core(k_in, v_in, wdad, w0_, wup_, a0_, aup_, kkw_, kaw_, ee_v, (v0_, vdn_, vup_, vf_))
            prim = (rw[:, 1024:2048], rw[:, 2048:3072], rw[:, 4096:4224], w0_ref[...], wup_ref[...], a0_ref[...],
                    aup_ref[...], kkw_ref[...], kaw_ref[...], v0_ref[...], vdn_ref[...], vup_ref[...], vf_ref[...])
        else:
            def fn(k_in, v_in, wdad, w0_, wup_, a0_, aup_, kkw_, kaw_):
                return _rwkv_core(k_in, v_in, wdad, w0_, wup_, a0_, aup_, kkw_, kaw_, ee_v, None)
            prim = (rw[:, 1024:2048], rw[:, 2048:3072], rw[:, 4096:4224], w0_ref[...], wup_ref[...], a0_ref[...],
                    aup_ref[...], kkw_ref[...], kaw_ref[...])
        _, vjp = jax.vjp(fn, *prim)
        gs = vjp((g_w, g_k, g_v, g_a, g_b))
        d_k, d_v, d_wdad = gs[0], gs[1], gs[2]
        g_rw = jnp.concatenate([g_r, d_k, d_v, g_z, d_wdad], axis=1)

        @pl.when(i == 0)
        def _():
            carry[...] = jnp.zeros_like(carry)
            dmu_o[...] = jnp.zeros_like(dmu_o)
            dw0_o[...] = jnp.zeros_like(dw0_o)
            dwup_o[...] = jnp.zeros_like(dwup_o)
            da0_o[...] = jnp.zeros_like(da0_o)
            daup_o[...] = jnp.zeros_like(daup_o)
            dkkw_o[...] = jnp.zeros_like(dkkw_o)
            dkaw_o[...] = jnp.zeros_like(dkaw_o)
            if has_mix:
                dv0_o[...] = jnp.zeros_like(dv0_o)
                dvdn_o[...] = jnp.zeros_like(dvdn_o)
                dvup_o[...] = jnp.zeros_like(dvup_o)

        dmu_o[...] += jnp.sum(g_rw * (yp - y), axis=0, keepdims=True)
        dw0_o[...] += gs[3]
        dwup_o[...] += gs[4]
        da0_o[...] += gs[5]
        daup_o[...] += gs[6]
        dkkw_o[...] += gs[7]
        dkaw_o[...] += gs[8]
        if has_mix:
            dv0_o[...] += gs[9]
            dvdn_o[...] += gs[10]
            dvup_o[...] += gs[11]
            dvf_o[...] = gs[12]
        gm = g_rw * mu_v
        nxt = pltpu.roll(gm, TB - 1, 0)
        rowi = lax.broadcasted_iota(jnp.int32, gm.shape, 0)
        nxt = jnp.where(rowi == TB - 1, carry[...], nxt)
        dproj_o[...] = g_rw - gm + nxt
        carry[...] = gm[0:1, :]

    rev = lambda i: (nb - 1 - i, 0)
    row = lambda n: pl.BlockSpec((1, n), lambda i: (0, 0))
    full = lambda s: pl.BlockSpec(s, lambda i: (0,) * len(s))
    blk = pl.BlockSpec((TB, D_RWKV), rev)
    in_specs = [pl.BlockSpec((TB, RWKV_COLS), rev),
                pl.BlockSpec((8, RWKV_COLS), lambda i: (jnp.maximum((nb - 1 - i) * (TB // 8) - 1, 0), 0)),
                row(RWKV_COLS), row(D_RWKV), full((LANES, D_RWKV)), row(D_RWKV), full((LANES, D_RWKV)),
                row(D_RWKV), row(D_RWKV), full((256, 256))]
    args = [proj, proj, mu, w0, wup, a0, aup, kkw, kaw, ee]
    out_specs = [pl.BlockSpec((TB, RWKV_COLS), rev), row(RWKV_COLS), row(D_RWKV), full((LANES, D_RWKV)),
                 row(D_RWKV), full((LANES, D_RWKV)), row(D_RWKV), row(D_RWKV)]
    out_shape = [SDS((T, RWKV_COLS), f32), SDS((1, RWKV_COLS), f32), SDS((1, D_RWKV), f32), SDS((LANES, D_RWKV), f32),
                 SDS((1, D_RWKV), f32), SDS((LANES, D_RWKV), f32), SDS((1, D_RWKV), f32), SDS((1, D_RWKV), f32)]
    if has_mix:
        v0, vdn, vup, vfirst = vmix
        in_specs += [row(D_RWKV), full((D_RWKV, LANES)), full((LANES, D_RWKV)), blk]
        args += [v0, vdn, vup, vfirst]
        out_specs += [row(D_RWKV), full((D_RWKV, LANES)), full((LANES, D_RWKV)), blk]
        out_shape += [SDS((1, D_RWKV), f32), SDS((D_RWKV, LANES), f32), SDS((LANES, D_RWKV), f32),
                      SDS((T, D_RWKV), f32)]
    in_specs += [blk] * n_cot
    args += cot_list
    return pl.pallas_call(
        body, name=f"rwkv_pre_bwd_l{layer}", grid=(nb,), in_specs=in_specs, out_specs=out_specs, out_shape=out_shape,
        scratch_shapes=[pltpu.VMEM((1, RWKV_COLS), f32)],
        compiler_params=_params(("arbitrary",), 60),
    )(*args)


def _to_col(v):
    T = v.shape[0]
    vc = jnp.swapaxes(v.reshape(T, RWKV_HEADS, RWKV_HEAD), 1, 2)
    hi = lax.reduce_precision(vc, 8, 7)
    r1 = vc - hi
    mid = lax.reduce_precision(r1, 8, 7)
    parts = jnp.concatenate([hi, mid, r1 - mid], axis=2).astype(bf16)
    return jnp.pad(parts, ((0, 0), (0, 0), (0, LANES - 3 * RWKV_HEADS)))


def _from_col(vc):
    T = vc.shape[0]
    return jnp.swapaxes(vc[:, :, :RWKV_HEADS], 1, 2).reshape(T, D_RWKV)


def _rwkv_scan_fwd(r, w, k, vcol, a, b, ee, e16, e16t, layer, gather=()):
    T = r.shape[0]
    nb = T // SCAN_TB
    ng = len(gather)

    def body(*refs):
        r_ref, w_ref, k_ref, v_ref, a_ref, an_ref, b_ref, ee_ref, e16_ref, e16t_ref = refs[:10]
        g_ins, refs = refs[10:10 + ng], refs[10 + ng:]
        o_ref, sp_ref = refs[:2]
        g_outs, refs = refs[2:2 + ng], refs[2 + ng:]
        s_ref, sa_ref, vb_ref, po_ref = refs[:4]
        g_sems = refs[4:]
        step = pl.program_id(0)

        @pl.when(step == 0)
        def _():
            s_ref[...] = jnp.zeros_like(s_ref)
            sa_ref[...] = jnp.zeros_like(sa_ref)
            if ng:
                _gather_start(g_ins, g_outs, *g_sems)

        if ng:
            @pl.when(step == (3 * nb) // 4)
            def _():
                _gather_forward(g_ins, g_outs, *g_sems)

        ee_v = ee_ref[...]
        r_b, w_b, k_b, a_b, b_b = r_ref[...], w_ref[...], k_ref[...], a_ref[...], b_ref[...]
        rowi = lax.broadcasted_iota(jnp.int32, a_b.shape, 0)
        a_nx = jnp.where(rowi == SCAN_TB - 1, an_ref[0:1, :], pltpu.roll(a_b, SCAN_TB - 1, 0))
        wa = w_b * a_nx
        dots = _seg_raw(jnp.concatenate([b_b * a_nx, k_b * a_nx], axis=0), ee_v, 3)
        beta, kappa = dots[0:SCAN_TB], dots[SCAN_TB:2 * SCAN_TB]
        vb_ref[...] = jnp.dot(v_ref[...].reshape(SCAN_TB * RWKV_HEAD, LANES), e16t_ref[...],
                              preferred_element_type=f32)

        for t in range(SCAN_TB):
            row = lambda x: x[t:t + 1, :]
            blk = pl.ds(t * RWKV_HEAD, RWKV_HEAD)
            sp = s_ref[...]
            sp_ref[t] = sp
            sa = sa_ref[...]
            vb = vb_ref[blk, :]
            ahead = _seg_raw(sp * row(wa), ee_v, SCAN_PARTS, per_tile=True)
            sn = sp * row(w_b) + sa * row(b_b) + vb * row(k_b)
            s_ref[...] = sn
            sa_ref[...] = ahead + sa * row(beta) + vb * row(kappa)
            po_ref[blk, :] = sn * row(r_b)
        o_ref[...] = _dot_parts(po_ref[...], e16_ref[...], 1).reshape(SCAN_TB, RWKV_HEAD, LANES)

        if ng:
            @pl.when(step == nb - 1)
            def _():
                _gather_finish(g_ins, g_outs, *g_sems)

    rows = pl.BlockSpec((SCAN_TB, D_RWKV), lambda i: (i, 0))
    rows_next = pl.BlockSpec((SCAN_TB, D_RWKV), lambda i: (jnp.minimum(i + 1, nb - 1), 0))
    col = pl.BlockSpec((SCAN_TB, RWKV_HEAD, LANES), lambda i: (i, 0, 0))
    full = lambda s: pl.BlockSpec(s, lambda i: (0,) * len(s))
    outs = pl.pallas_call(
        body, name=f"rwkv_scan_fwd_l{layer}", grid=(nb,),
        in_specs=[rows, rows, rows, col, rows, rows_next, rows, full((256, 256)), full((D_RWKV, LANES)),
                  full((LANES, D_RWKV))] + [HBM_SPEC] * ng,
        out_specs=[col, pl.BlockSpec((SCAN_TB, RWKV_HEAD, D_RWKV), lambda i: (i, 0, 0))] + [HBM_SPEC] * ng,
        out_shape=[SDS((T, RWKV_HEAD, LANES), f32), SDS((T, RWKV_HEAD, D_RWKV), f32)] + _gather_out_shapes(gather),
        scratch_shapes=[pltpu.VMEM((RWKV_HEAD, D_RWKV), f32), pltpu.VMEM((RWKV_HEAD, D_RWKV), f32),
                        pltpu.VMEM((SCAN_TB * RWKV_HEAD, D_RWKV), f32),
                        pltpu.VMEM((SCAN_TB * RWKV_HEAD, D_RWKV), f32)] + (_gather_sems(ng) if ng else []),
        compiler_params=_params(("arbitrary",), 48),
    )(r, w, k, vcol, a, a, b, ee, e16, e16t, *gather)
    return outs[0], outs[1], list(outs[2:])


def _rwkv_scan_bwd(r, w, k, vcol, a, b, sprev, docol, ee, e16, e16t, layer, send=()):
    T = r.shape[0]
    nb = T // SCAN_TB
    ns = len(send)

    def body(*refs):
        r_ref, w_ref, k_ref, v_ref, a_ref, b_ref, sp_ref, do_ref, ee_ref, e16_ref, e16t_ref = refs[:11]
        c_ins, refs = refs[11:11 + ns], refs[11 + ns:]
        dr_o, dw_o, dk_o, da_o, db_o, dv_o = refs[:6]
        c_outs, refs = refs[6:6 + ns], refs[6 + ns:]
        ds_ref, snext_ref, sa_ref, vb_ref, dob_ref, pdv_ref = refs[:6]
        c_sems = refs[6:]
        if ns:
            @pl.when(pl.program_id(0) == 0)
            def _():
                for cp in _chips_copies(c_ins, c_outs, *c_sems):
                    cp.start()

        ee_v, e16t_v = ee_ref[...], e16t_ref[...]
        csum = lambda x: jnp.sum(x, axis=0, keepdims=True)
        nrow = SCAN_TB * RWKV_HEAD
        last = pl.ds(nrow - RWKV_HEAD, RWKV_HEAD)
        r_b, w_b, k_b, a_b, b_b = r_ref[...], w_ref[...], k_ref[...], a_ref[...], b_ref[...]
        sa_ref[...] = _seg_raw((sp_ref[...] * a_b[:, None, :]).reshape(nrow, D_RWKV), ee_v, SCAN_PARTS)
        vb_ref[...] = jnp.dot(v_ref[...].reshape(nrow, LANES), e16t_v, preferred_element_type=f32)
        dob_ref[...] = jnp.dot(do_ref[...].reshape(nrow, LANES), e16t_v, preferred_element_type=f32)

        @pl.when(pl.program_id(0) == 0)
        def _():
            ds_ref[...] = jnp.zeros_like(ds_ref)
            snext_ref[...] = (sp_ref[SCAN_TB - 1] * w_b[SCAN_TB - 1:SCAN_TB, :]
                              + sa_ref[last, :] * b_b[SCAN_TB - 1:SCAN_TB, :]
                              + vb_ref[last, :] * k_b[SCAN_TB - 1:SCAN_TB, :])

        for s in range(SCAN_TB):
            t = SCAN_TB - 1 - s
            row = lambda x: x[t:t + 1, :]
            blk = pl.ds(t * RWKV_HEAD, RWKV_HEAD)
            sp = sp_ref[t]
            sn = snext_ref[...] if s == 0 else sp_ref[t + 1]
            sa, vb, dob = sa_ref[blk, :], vb_ref[blk, :], dob_ref[blk, :]
            ds = ds_ref[...] + dob * row(r_b)
            dsa = _seg_raw(ds * row(b_b), ee_v, SCAN_PARTS, per_tile=True)
            ds_ref[...] = ds * row(w_b) + dsa * row(a_b)
            dr_o[t:t + 1, :] = csum(sn * dob)
            pdv_ref[blk, :] = ds * row(k_b)
            dk_o[t:t + 1, :] = csum(ds * vb)
            db_o[t:t + 1, :] = csum(ds * sa)
            dw_o[t:t + 1, :] = csum(ds * sp)
            da_o[t:t + 1, :] = csum(sp * dsa)
        snext_ref[...] = sp_ref[0]
        dv_o[...] = _dot_parts(pdv_ref[...], e16_ref[...], 1).reshape(SCAN_TB, RWKV_HEAD, LANES)

        if ns:
            @pl.when(pl.program_id(0) == nb - 1)
            def _():
                for cp in _chips_copies(c_ins, c_outs, *c_sems):
                    cp.wait()

    rows = pl.BlockSpec((SCAN_TB, D_RWKV), lambda i: (nb - 1 - i, 0))
    col = pl.BlockSpec((SCAN_TB, RWKV_HEAD, LANES), lambda i: (nb - 1 - i, 0, 0))
    st = pl.BlockSpec((SCAN_TB, RWKV_HEAD, D_RWKV), lambda i: (nb - 1 - i, 0, 0))
    full = lambda s: pl.BlockSpec(s, lambda i: (0,) * len(s))
    big = pltpu.VMEM((SCAN_TB * RWKV_HEAD, D_RWKV), f32)
    outs = pl.pallas_call(
        body, name=f"rwkv_scan_bwd_l{layer}", grid=(nb,),
        in_specs=[rows, rows, rows, col, rows, rows, st, col, full((256, 256)), full((D_RWKV, LANES)),
                  full((LANES, D_RWKV))] + [HBM_SPEC] * ns,
        out_specs=[rows] * 5 + [col] + [HBM_SPEC] * ns,
        out_shape=[SDS((T, D_RWKV), f32)] * 5 + [SDS((T, RWKV_HEAD, LANES), f32)] + _chips_out_shapes(send),
        scratch_shapes=[pltpu.VMEM((RWKV_HEAD, D_RWKV), f32)] * 2 + [big] * 4 + (_chips_sems(ns) if ns else []),
        compiler_params=_params(("arbitrary",), 56),
    )(r, w, k, vcol, a, b, sprev, docol, ee, e16, e16t, *send)
    return (*outs[:6], list(outs[6:]))


def _post_core(o, r, k2, v, z, gw, gb, rk, ee):
    inv = 1.0 / RWKV_HEAD
    mu = _seg(o, ee) * inv
    d = o - mu
    var = _seg(d * d, ee) * inv
    on = d * lax.rsqrt(var + GN_EPS) * gw + gb
    bonus = _seg(r * k2 * rk, ee) * v
    return (on + bonus) * jax.nn.silu(z)


def _rwkv_post_fwd(o, r, k2, v, z, gw, gb, rk, ee, layer):
    T = o.shape[0]

    def body(o_ref, r_ref, k_ref, v_ref, z_ref, gw_ref, gb_ref, rk_ref, ee_ref, out_ref):
        out_ref[...] = _post_core(o_ref[...], r_ref[...], k_ref[...], v_ref[...], z_ref[...], gw_ref[...],
                                  gb_ref[...], rk_ref[...], ee_ref[...])

    blk = pl.BlockSpec((TB, D_RWKV), lambda i: (i, 0))
    row = pl.BlockSpec((1, D_RWKV), lambda i: (0, 0))
    return pl.pallas_call(
        body, name=f"rwkv_post_fwd_l{layer}", grid=(T // TB,),
        in_specs=[blk] * 5 + [row] * 3 + [pl.BlockSpec((256, 256), lambda i: (0, 0))],
        out_specs=blk, out_shape=SDS((T, D_MODEL), f32), compiler_params=_params(("parallel",), 40),
    )(o, r, k2, v, z, gw, gb, rk, ee)


def _rwkv_post_bwd(o, r, k2, v, z, gw, gb, rk, ee, dcat, layer):
    T = o.shape[0]

    def body(o_ref, r_ref, k_ref, v_ref, z_ref, gw_ref, gb_ref, rk_ref, ee_ref, g_ref,
             do_o, dr_o, dk_o, dv_o, dz_o, dgw_o, dgb_o, drk_o):
        ee_v = ee_ref[...]
        fn = lambda o_, r_, k_, v_, z_, gw_, gb_, rk_: _post_core(o_, r_, k_, v_, z_, gw_, gb_, rk_, ee_v)
        _, vjp = jax.vjp(fn, o_ref[...], r_ref[...], k_ref[...], v_ref[...], z_ref[...], gw_ref[...], gb_ref[...],
                         rk_ref[...])
        gs = vjp(g_ref[...])
        do_o[...], dr_o[...], dk_o[...], dv_o[...], dz_o[...] = gs[:5]

        @pl.when(pl.program_id(0) == 0)
        def _():
            dgw_o[...] = jnp.zeros_like(dgw_o)
            dgb_o[...] = jnp.zeros_like(dgb_o)
            drk_o[...] = jnp.zeros_like(drk_o)

        dgw_o[...] += gs[5]
        dgb_o[...] += gs[6]
        drk_o[...] += gs[7]

    blk = pl.BlockSpec((TB, D_RWKV), lambda i: (i, 0))
    row = pl.BlockSpec((1, D_RWKV), lambda i: (0, 0))
    return pl.pallas_call(
        body, name=f"rwkv_post_bwd_l{layer}", grid=(T // TB,),
        in_specs=[blk] * 5 + [row] * 3 + [pl.BlockSpec((256, 256), lambda i: (0, 0)), blk],
        out_specs=[blk] * 5 + [row] * 3,
        out_shape=[SDS((T, D_RWKV), f32)] * 5 + [SDS((1, D_RWKV), f32)] * 3,
        compiler_params=_params(("arbitrary",), 48),
    )(o, r, k2, v, z, gw, gb, rk, ee, dcat)


def _hgrn_chunk(qr, fr, ii, z, S, lbl, gw, layer):
    L = CHUNK
    m = jnp.max(lbl, axis=0, keepdims=True)
    e = jnp.exp(lbl - m)
    sm = e / jnp.sum(e, axis=0, keepdims=True)
    if layer == 0:
        lb = sm[0:1] - sm[0:1]
    else:
        lb = (sm[0:1] + sm[1:2]) - sm[0:1]
    q = jax.nn.silu(qr)
    log_lb = jnp.log(jnp.maximum(lb, LB_FLOOR))
    lf = jnp.logaddexp(log_lb, jnp.log1p(-lb) + jax.nn.log_sigmoid(fr))
    k = (1.0 - lb) * jax.nn.sigmoid(-fr)
    ti = lax.broadcasted_iota(jnp.int32, (L, L), 0)
    si = lax.broadcasted_iota(jnp.int32, (L, L), 1)
    ltri = (si <= ti).astype(f32)
    b = jnp.dot(ltri, lf, precision=HIGHEST, preferred_element_type=f32)
    btot = jnp.dot(jnp.ones((L, L), f32), lf, precision=HIGHEST, preferred_element_type=f32)
    o = jnp.dot(q * jnp.exp(b), S, preferred_element_type=f32)
    nsub = L // HSUB
    t3 = lax.broadcasted_iota(jnp.int32, (HSUB, HSUB, HG_HEAD), 0)
    s3 = lax.broadcasted_iota(jnp.int32, (HSUB, HSUB, HG_HEAD), 1)
    causal = s3 <= t3
    rows = []
    for I in range(nsub):
        sl = slice(I * HSUB, (I + 1) * HSUB)
        bI, qI, kI, iI = b[sl], q[sl], k[sl], ii[sl]
        d3 = bI[:, None, :] - bI[None, :, :]
        dec = jnp.where(causal, jnp.exp(jnp.where(causal, d3, 0.0)), 0.0)
        att = jnp.sum(qI[:, None, :] * dec * kI[None, :, :], axis=-1)
        oI = jnp.dot(att, iI, preferred_element_type=f32)
        for J in range(I):
            sj = slice(J * HSUB, (J + 1) * HSUB)
            bm = b[(J + 1) * HSUB - 1:(J + 1) * HSUB]
            qs = qI * jnp.exp(bI - bm)
            ks = k[sj] * jnp.exp(bm - b[sj])
            att_ij = lax.dot_general(qs, ks, (((1,), (1,)), ((), ())), preferred_element_type=f32)
            oI = oI + jnp.dot(att_ij, ii[sj], preferred_element_type=f32)
        rows.append(oI)
    o = o + jnp.concatenate(rows, axis=0)
    k_dec = k * jnp.exp(btot - b)
    bcol = lax.dot_general(lf, jnp.ones((L, HG_HEAD), f32), (((0,), (0,)), ((), ())), precision=HIGHEST,
                           preferred_element_type=f32)
    s_new = S * jnp.exp(bcol) + lax.dot_general(k_dec, ii, (((0,), (0,)), ((), ())), preferred_element_type=f32)
    ms = jnp.mean(o * o, axis=-1, keepdims=True)
    out = o * lax.rsqrt(ms + RMS_EPS) * gw * jax.nn.silu(z)
    return out, s_new


def _hg_specs(nc, rev):
    cidx = (lambda c: nc - 1 - c) if rev else (lambda c: c)
    pj = lambda off: pl.BlockSpec((CHUNK, HG_HEAD), lambda h, c: (cidx(c), RW_BLKS + off + h))
    return cidx, [pj(0), pj(HG_HEADS), pj(2 * HG_HEADS), pj(3 * HG_HEADS)]


def _hgrn_fwd(proj, cat, lbl, gw, layer):
    T = proj.shape[0]
    nc = T // CHUNK
    cidx, pspecs = _hg_specs(nc, False)

    def body(q_ref, f_ref, i_ref, z_ref, lbl_ref, gw_ref, cat_in, out_ref, sh_ref, s_ref):
        del cat_in

        @pl.when(pl.program_id(1) == 0)
        def _():
            s_ref[...] = jnp.zeros_like(s_ref)

        s0 = s_ref[...]
        sh_ref[0, 0] = s0
        out, s_new = _hgrn_chunk(q_ref[...], f_ref[...], i_ref[...], z_ref[...], s0, lbl_ref[...], gw_ref[...], layer)
        out_ref[...] = out
        s_ref[...] = s_new

    return pl.pallas_call(
        body, name=f"hgrn_fwd_l{layer}", grid=(HG_HEADS, nc),
        in_specs=pspecs + [pl.BlockSpec((2, HG_HEAD), lambda h, c: (0, h)),
                           pl.BlockSpec((1, HG_HEAD), lambda h, c: (0, h)),
                           pl.BlockSpec(memory_space=pl.ANY)],
        out_specs=[pl.BlockSpec((CHUNK, HG_HEAD), lambda h, c: (c, HG_HEADS + h)),
                   pl.BlockSpec((1, 1, HG_HEAD, HG_HEAD), lambda h, c: (h, c, 0, 0))],
        out_shape=[SDS((T, D_MODEL), f32), SDS((HG_HEADS, nc, HG_HEAD, HG_HEAD), f32)],
        scratch_shapes=[pltpu.VMEM((HG_HEAD, HG_HEAD), f32)],
        input_output_aliases={6: 0},
        compiler_params=_params(("parallel", "arbitrary"), 40),
    )(proj, proj, proj, proj, lbl, gw, cat)


def _hgrn_bwd(proj, shist, dcat, lbl, gw, layer):
    T = proj.shape[0]
    nc = T // CHUNK
    cidx, pspecs = _hg_specs(nc, True)

    def body(q_ref, f_ref, i_ref, z_ref, lbl_ref, gw_ref, sh_ref, g_ref,
             dq_o, df_o, di_o, dz_o, dlbl_o, dgw_o, ds_ref):
        @pl.when(pl.program_id(1) == 0)
        def _():
            ds_ref[...] = jnp.zeros_like(ds_ref)
            dlbl_o[...] = jnp.zeros_like(dlbl_o)
            dgw_o[...] = jnp.zeros_like(dgw_o)

        fn = functools.partial(_hgrn_chunk, layer=layer)
        _, vjp = jax.vjp(fn, q_ref[...], f_ref[...], i_ref[...], z_ref[...], sh_ref[0, 0], lbl_ref[...], gw_ref[...])
        dq, df, di, dz, ds, dl, dg = vjp((g_ref[...], ds_ref[...]))
        dq_o[...], df_o[...], di_o[...], dz_o[...] = dq, df, di, dz
        ds_ref[...] = ds
        dlbl_o[...] += dl
        dgw_o[...] += dg

    ospec = pl.BlockSpec((CHUNK, HG_HEAD), lambda h, c: (cidx(c), h))
    return pl.pallas_call(
        body, name=f"hgrn_bwd_l{layer}", grid=(HG_HEADS, nc),
        in_specs=pspecs + [pl.BlockSpec((2, HG_HEAD), lambda h, c: (0, h)),
                           pl.BlockSpec((1, HG_HEAD), lambda h, c: (0, h)),
                           pl.BlockSpec((1, 1, HG_HEAD, HG_HEAD), lambda h, c: (h, cidx(c), 0, 0)),
                           pl.BlockSpec((CHUNK, HG_HEAD), lambda h, c: (cidx(c), HG_HEADS + h))],
        out_specs=[ospec] * 4 + [pl.BlockSpec((2, HG_HEAD), lambda h, c: (0, h)),
                                 pl.BlockSpec((1, HG_HEAD), lambda h, c: (0, h))],
        out_shape=[SDS((T, D_HGRN), f32)] * 4 + [SDS((2, D_HGRN), f32), SDS((1, D_HGRN), f32)],
        scratch_shapes=[pltpu.VMEM((HG_HEAD, HG_HEAD), f32)],
        compiler_params=_params(("parallel", "arbitrary"), 48),
    )(proj, proj, proj, proj, lbl, gw, shist, dcat)


def _ln_core(h, y, w, b):
    u = ALPHA * h + y
    mu = jnp.mean(u, axis=-1, keepdims=True)
    d = u - mu
    var = jnp.mean(d * d, axis=-1, keepdims=True)
    return d * lax.rsqrt(var + LN_EPS) * w + b


def _ln_fwd(h, y, w, b, layer):
    T = h.shape[0]

    def body(h_ref, y_ref, w_ref, b_ref, o_ref):
        o_ref[...] = _ln_core(h_ref[...], y_ref[...], w_ref[...], b_ref[...])

    blk = pl.BlockSpec((TB, D_MODEL), lambda i: (i, 0))
    row = pl.BlockSpec((1, D_MODEL), lambda i: (0, 0))
    return pl.pallas_call(body, name=f"ln_fwd_l{layer}", grid=(T // TB,), in_specs=[blk, blk, row, row],
                          out_specs=blk, out_shape=SDS((T, D_MODEL), f32),
                          compiler_params=_params(("parallel",), 40))(h, y, w, b)


def _ln_bwd(h, y, w, b, g, layer):
    T = h.shape[0]

    def body(h_ref, y_ref, w_ref, b_ref, g_ref, dh_o, dy_o, dw_o, db_o):
        _, vjp = jax.vjp(_ln_core, h_ref[...], y_ref[...], w_ref[...], b_ref[...])
        dh, dy, dw, db = vjp(g_ref[...])
        dh_o[...] = dh
        dy_o[...] = dy

        @pl.when(pl.program_id(0) == 0)
        def _():
            dw_o[...] = jnp.zeros_like(dw_o)
            db_o[...] = jnp.zeros_like(db_o)

        dw_o[...] += dw
        db_o[...] += db

    blk = pl.BlockSpec((TB, D_MODEL), lambda i: (i, 0))
    row = pl.BlockSpec((1, D_MODEL), lambda i: (0, 0))
    return pl.pallas_call(body, name=f"ln_bwd_l{layer}", grid=(T // TB,), in_specs=[blk, blk, row, row, blk],
                          out_specs=[blk, blk, row, row],
                          out_shape=[SDS((T, D_MODEL), f32)] * 2 + [SDS((1, D_MODEL), f32)] * 2,
                          compiler_params=_params(("arbitrary",), 48))(h, y, w, b, g)


def _loss_head(hout, target):
    T = hout.shape[0]

    def body(h_ref, t_ref, loss_o, g_o):
        err = h_ref[...] - t_ref[...]
        g_o[...] = err * (1.0 / D_MODEL)

        @pl.when(pl.program_id(0) == 0)
        def _():
            loss_o[...] = jnp.zeros_like(loss_o)

        part = jnp.sum(jnp.sum(err * err, axis=-1, keepdims=True) * (1.0 / D_MODEL), axis=0, keepdims=True)
        loss_o[...] += 0.5 * part

    blk = pl.BlockSpec((TB, D_MODEL), lambda i: (i, 0))
    return pl.pallas_call(body, name="loss_head", grid=(T // TB,), in_specs=[blk, blk],
                          out_specs=[pl.BlockSpec((1, 1), lambda i: (0, 0)), blk],
                          out_shape=[SDS((1, 1), f32), SDS((T, D_MODEL), f32)],
                          compiler_params=_params(("arbitrary",), 40))(hout, target)


def _pad_rows(m, lo, total):
    return jnp.pad(m, ((0, 0), (lo, total - lo - m.shape[1]), (0, 0)))


def _device_step(x, target, w_in0, later_shards, assemble_later, on_layer1_grads, on_layer1_recv,
                 shift_mu, w_decay0, w_decay_up, a0, a_up, k_k, k_a, r_k, ln_x_w, ln_x_b,
                 v_mix0, v_mix_down, v_mix_up, lb_logits, g_norm_w, ln_w, ln_b):
    T = x.shape[0]
    ee, e16, e16t = _const_mats()
    wup_pad = _pad_rows(w_decay_up, 0, LANES)
    aup_pad = _pad_rows(a_up, LORA, LANES)
    vdn_pad = jnp.pad(v_mix_down, ((0, 0), (0, 0), (0, LANES - VRES)))
    vup_pad = _pad_rows(v_mix_up, 0, LANES)
    row = lambda p, l: p[l][None, :]

    def vmix_of(l, vfirst):
        if l == 0:
            return None
        return (row(v_mix0, l - 1), vdn_pad[l - 1], vup_pad[l - 1], vfirst)

    h = x
    saved = []
    vfirst = None
    w_in = [w_in0, None]
    w_out = None
    for l in range(DEPTH):
        proj = _matmul(h, w_in[l], mode="nt", tm=512, tn=1664, tk=D_MODEL, name=f"proj_fwd_l{l}")
        pre_args = (proj, row(shift_mu, l), row(w_decay0, l), wup_pad[l], row(a0, l), aup_pad[l], row(k_k, l),
                    row(k_a, l), ee, vmix_of(l, vfirst))
        r, w, k2, v, av, bv, z = _rwkv_pre_fwd(*pre_args, layer=l)
        if l == 0:
            vfirst = v
        vcol = _to_col(v)
        ocol, sprev, gathered = _rwkv_scan_fwd(r, w, k2, vcol, av, bv, ee, e16, e16t, l,
                                               gather=later_shards if l == 0 else [])
        if l == 0:
            w_in[1], w_out = assemble_later(gathered)
        o = _from_col(ocol)
        post_args = (o, r, k2, v, z, row(ln_x_w, l), row(ln_x_b, l), row(r_k, l), ee)
        cat = _rwkv_post_fwd(*post_args, layer=l)
        cat, shist = _hgrn_fwd(proj, cat, lb_logits, row(g_norm_w, l), l)
        y = _matmul(cat, w_out, b_layer=l, mode="nn", tm=512, tn=1024, tk=D_MODEL, name=f"out_fwd_l{l}")
        h_new = _ln_fwd(h, y, row(ln_w, l), row(ln_b, l), l)
        saved.append(dict(h=h, proj=proj, pre_args=pre_args, post_args=post_args, scan=(r, w, k2, vcol, av, bv, sprev),
                          cat=cat, shist=shist, y=y))
        h = h_new

    loss_sum, g = _loss_head(h, target)

    grads = {n: [None] * DEPTH for n in ("w_in", "w_out", "shift_mu", "w_decay0", "w_decay_up", "a0", "a_up", "k_k",
                                         "k_a", "r_k", "ln_x_w", "ln_x_b", "g_norm_w", "ln_w", "ln_b")}
    dlbl = []
    dvfirst = None
    to_send = []
    for l in reversed(range(DEPTH)):
        s = saved[l]
        dh_res, dy, dlnw, dlnb = _ln_bwd(s["h"], s["y"], row(ln_w, l), row(ln_b, l), g, l)
        grads["ln_w"][l], grads["ln_b"][l] = dlnw[0], dlnb[0]
        dcat = _matmul(dy, w_out, b_layer=l, mode="nt", tm=512, tn=1024, tk=D_MODEL, name=f"out_bwd_dx_l{l}")
        grads["w_out"][l] = _matmul(s["cat"], dy, mode="tn", tm=1024, tn=1024, tk=512, name=f"out_bwd_dw_l{l}")
        do, dr_p, dk_p, dv_p, dz, dgw, dgb, drk = _rwkv_post_bwd(*s["post_args"], dcat, layer=l)
        grads["ln_x_w"][l], grads["ln_x_b"][l], grads["r_k"][l] = dgw[0], dgb[0], drk[0]
        r, w, k2, vcol, av, bv, sprev = s["scan"]
        dr_s, dw_s, dk_s, da_s, db_s, dvcol, arrived = _rwkv_scan_bwd(r, w, k2, vcol, av, bv, sprev, _to_col(do), ee,
                                                                     e16, e16t, l, send=to_send if l == 0 else [])
        if l == 0:
            on_layer1_recv(arrived)
        v_cots = [dv_p, _from_col(dvcol)] + ([dvfirst] if (l == 0 and dvfirst is not None) else [])
        cots = dict(r=[dr_p, dr_s], w=dw_s, k=[dk_p, dk_s], v=v_cots, a=da_s, b=db_s, z=dz)
        outs = _rwkv_pre_bwd(*s["pre_args"], cots, layer=l)
        dproj, dmu, dw0, dwup, da0, daup, dkkw, dkaw = outs[:8]
        grads["shift_mu"][l], grads["w_decay0"][l], grads["a0"][l] = dmu[0], dw0[0], da0[0]
        grads["k_k"][l], grads["k_a"][l] = dkkw[0], dkaw[0]
        grads["w_decay_up"][l], grads["a_up"][l] = dwup[:LORA], daup[LORA:]
        if l > 0:
            dv0, dvdn, dvup, dvfirst = outs[8:]
            g_vmix = (dv0, dvdn[:, :VRES][None], dvup[:VRES][None])
        dq, df, di, dzh, dl, dgn = _hgrn_bwd(s["proj"], s["shist"], dcat, lb_logits, row(g_norm_w, l), l)
        dlbl.append(dl)
        grads["g_norm_w"][l] = dgn[0]
        dproj = jnp.concatenate([dproj, dq, df, di, dzh], axis=1)
        grads["w_in"][l] = _matmul(dproj, s["h"], mode="tn", tm=1664, tn=1024, tk=512, name=f"proj_bwd_dw_l{l}")
        g = _matmul(dproj, w_in[l], mode="nn", tm=512, tn=1024, tk=1664, name=f"proj_bwd_dx_l{l}", add=dh_res)
        if l == 1:
            to_send = on_layer1_grads(grads["w_in"][1], grads["w_out"][1])
    w_in_grads, w_out_grads = grads.pop("w_in"), grads.pop("w_out")
    out = {n: jnp.stack(v) for n, v in grads.items()}
    out["w_in"], out["w_out"] = w_in_grads, w_out_grads
    out["lb_logits"] = dlbl[0] + dlbl[1]
    out["v_mix0"], out["v_mix_down"], out["v_mix_up"] = g_vmix
    return loss_sum, g, out


CHIP_COMBOS = ((1, 0), (0, 1), (1, 1))
HBM_SPEC = pl.BlockSpec(memory_space=pl.ANY)


def _mesh_pos():
    return lax.axis_index("x"), lax.axis_index("y"), lax.axis_index("c")


def _flip(v, d):
    return 1 - v if d else v


def _half(ref, dim, cc, hs):
    idx = [slice(None)] * len(ref.shape)
    idx[dim] = pl.ds(cc * hs, hs)
    return ref.at[tuple(idx)]


def _gather_shards(arrs):
    n = len(arrs)

    def body(*refs):
        ins, outs = refs[:n], refs[n:2 * n]
        sems = refs[2 * n:]
        _gather_start(ins, outs, *sems)
        _gather_forward(ins, outs, *sems)
        _gather_finish(ins, outs, *sems)

    return pl.pallas_call(
        body, name="gather_shards", in_specs=[HBM_SPEC] * n, out_specs=[HBM_SPEC] * n,
        out_shape=_gather_out_shapes(arrs), scratch_shapes=_gather_sems(n),
    )(*arrs)


def _gather_out_shapes(arrs):
    return [SDS((4,) + a.shape, a.dtype) for a in arrs]


def _gather_sems(n):
    return [pltpu.SemaphoreType.DMA((n, 6)), pltpu.SemaphoreType.DMA((n, 6)), pltpu.SemaphoreType.DMA((n,))]


def _gather_copies(kind, ins, outs, send_sems, recv_sems, local_sems):
    x, y, c = _mesh_pos()
    kc = 2 * x + y
    cps = []
    for a in range(len(ins)):
        if kind == "local":
            cps.append(pltpu.make_async_copy(ins[a], outs[a].at[kc], local_sems.at[a]))
            continue
        for j, (dx, dy) in enumerate(CHIP_COMBOS):
            px, py = _flip(x, dx), _flip(y, dy)
            kj = 2 * px + py
            src, dst, s, to = {
                "sends": (ins[a].at[c], outs[a].at[kc, c], j, (px, py, c)),
                "landed": (outs[a].at[kj, c], outs[a].at[kj, c], j, (px, py, c)),
                "forwards": (outs[a].at[kj, c], outs[a].at[kj, c], 3 + j, (x, y, 1 - c)),
                "passed": (outs[a].at[kj, 1 - c], outs[a].at[kj, 1 - c], 3 + j, (x, y, 1 - c)),
            }[kind]
            cps.append(pltpu.make_async_remote_copy(
                src_ref=src, dst_ref=dst, send_sem=send_sems.at[a, s], recv_sem=recv_sems.at[a, s], device_id=to,
                device_id_type=MESH))
    return cps


def _gather_start(*refs):
    for cp in _gather_copies("local", *refs) + _gather_copies("sends", *refs):
        cp.start()


def _gather_forward(*refs):
    for arrival, fw in zip(_gather_copies("landed", *refs), _gather_copies("forwards", *refs)):
        arrival.wait_recv()
        fw.start()


def _gather_finish(*refs):
    for cp in _gather_copies("passed", *refs):
        cp.wait_recv()
    for cp in _gather_copies("sends", *refs) + _gather_copies("forwards", *refs):
        cp.wait_send()
    for cp in _gather_copies("local", *refs):
        cp.wait()


DEV_COMBOS = tuple((dx, dy, dc) for dx in (0, 1) for dy in (0, 1) for dc in (0, 1))[1:]


def _allreduce_small(buf):
    R = buf.shape[0]

    def body(x_ref, o_ref, slots, send_sems, recv_sems):
        x, y, c = _mesh_pos()
        me = 4 * x + 2 * y + c
        cps = []
        for k, (dx, dy, dc) in enumerate(DEV_COMBOS):
            cp = pltpu.make_async_remote_copy(
                src_ref=x_ref, dst_ref=slots.at[k + 1], send_sem=send_sems.at[k], recv_sem=recv_sems.at[k],
                device_id=(_flip(x, dx), _flip(y, dy), _flip(c, dc)), device_id_type=MESH)
            cp.start()
            cps.append(cp)
        slots[0] = x_ref[...]
        for cp in cps:
            cp.wait_recv()
        acc = slots[jnp.bitwise_xor(me, 0)]
        for d in range(1, 8):
            acc = acc + slots[jnp.bitwise_xor(me, d)]
        o_ref[...] = acc
        for cp in cps:
            cp.wait_send()

    vm = pl.BlockSpec(memory_space=pltpu.VMEM)
    return pl.pallas_call(
        body, name="allreduce_small", in_specs=[vm], out_specs=vm, out_shape=SDS((R, LANES), f32),
        scratch_shapes=[pltpu.VMEM((8, R, LANES), f32), pltpu.SemaphoreType.DMA((7,)), pltpu.SemaphoreType.DMA((7,))],
        compiler_params=_params(None, 48),
    )(buf)


def _swap_halves(gs, tag):
    n = len(gs)

    def body(*refs):
        ins, outs = refs[:n], refs[n:2 * n]
        send_sems, recv_sems = refs[2 * n:]
        x, y, c = _mesh_pos()
        cps = []
        for a in range(n):
            h = ins[a].shape[1] // 2
            cp = pltpu.make_async_remote_copy(
                src_ref=_half(ins[a], 1, 1 - c, h), dst_ref=outs[a], send_sem=send_sems.at[a],
                recv_sem=recv_sems.at[a], device_id=(x, y, 1 - c), device_id_type=MESH)
            cp.start()
            cps.append(cp)
        for cp in cps:
            cp.wait()

    return pl.pallas_call(
        body, name=f"rs_swap_halves_{tag}", in_specs=[HBM_SPEC] * n, out_specs=[HBM_SPEC] * n,
        out_shape=[SDS((g.shape[0], g.shape[1] // 2, g.shape[2]), g.dtype) for g in gs],
        scratch_shapes=[pltpu.SemaphoreType.DMA((n,)), pltpu.SemaphoreType.DMA((n,))],
    )(*gs)


def _chips_copies(ins, outs, send_sems, recv_sems):
    x, y, c = _mesh_pos()
    cps = []
    for a in range(len(ins)):
        for j, (dx, dy) in enumerate(CHIP_COMBOS):
            px, py = _flip(x, dx), _flip(y, dy)
            cps.append(pltpu.make_async_remote_copy(
                src_ref=ins[a].at[2 * px + py], dst_ref=outs[a].at[j], send_sem=send_sems.at[a, j],
                recv_sem=recv_sems.at[a, j], device_id=(px, py, c), device_id_type=MESH))
    return cps


def _chips_out_shapes(ps):
    return [SDS((3,) + p.shape[1:], p.dtype) for p in ps]


def _chips_sems(n):
    return [pltpu.SemaphoreType.DMA((n, 3)), pltpu.SemaphoreType.DMA((n, 3))]


def _send_to_chips(ps, tag):
    n = len(ps)

    def body(*refs):
        cps = _chips_copies(refs[:n], refs[n:2 * n], *refs[2 * n:])
        for cp in cps:
            cp.start()
        for cp in cps:
            cp.wait()

    return pl.pallas_call(
        body, name=f"rs_send_to_chips_{tag}", in_specs=[HBM_SPEC] * n, out_specs=[HBM_SPEC] * n,
        out_shape=_chips_out_shapes(ps), scratch_shapes=_chips_sems(n),
    )(*ps)


def _share_halves(qs, tag):
    n = len(qs)

    def body(*refs):
        ins, outs = refs[:n], refs[n:2 * n]
        send_sems, recv_sems = refs[2 * n:]
        x, y, c = _mesh_pos()
        cps = []
        for a in range(n):
            cp = pltpu.make_async_remote_copy(
                src_ref=ins[a], dst_ref=outs[a], send_sem=send_sems.at[a], recv_sem=recv_sems.at[a],
                device_id=(x, y, 1 - c), device_id_type=MESH)
            cp.start()
            cps.append(cp)
        for cp in cps:
            cp.wait()

    return pl.pallas_call(
        body, name=f"rs_share_halves_{tag}", in_specs=[HBM_SPEC] * n, out_specs=[HBM_SPEC] * n,
        out_shape=[SDS(q.shape, q.dtype) for q in qs],
        scratch_shapes=[pltpu.SemaphoreType.DMA((n,)), pltpu.SemaphoreType.DMA((n,))],
    )(*qs)


def _row_tile(rows, cap):
    return max(t for t in range(16, cap + 1, 16) if rows % t == 0)


def _add_own_half(g, recv, c, name):
    S, R, C = g.shape
    h = R // 2
    tr = _row_tile(h, 256)
    nt = h // tr

    def body(c_ref, g_ref, r_ref, o_ref, o16_ref):
        del c_ref
        s = g_ref[...] + r_ref[...]
        o_ref[...] = s
        o16_ref[...] = s.astype(bf16)

    ospec = pl.BlockSpec((1, tr, C), lambda s, i, cr: (s, i, 0))
    return pl.pallas_call(
        body, name=name,
        grid_spec=pltpu.PrefetchScalarGridSpec(
            num_scalar_prefetch=1, grid=(S, nt),
            in_specs=[pl.BlockSpec((1, tr, C), lambda s, i, cr: (s, cr[0] * nt + i, 0)), ospec],
            out_specs=[ospec, ospec]),
        out_shape=[SDS((S, h, C), f32), SDS((S, h, C), bf16)],
        compiler_params=_params(("parallel", "parallel"), 40),
    )(c, g, recv)


def _add_chip_parts(p, recv, kc, name):
    _, R, C = p.shape
    tr = _row_tile(R, 256)

    def body(k_ref, p_ref, r0, r1, r2, o_ref):
        del k_ref
        o_ref[...] = ((p_ref[0] + r0[0].astype(f32)) + r1[0].astype(f32)) + r2[0].astype(f32)

    rspec = lambda j: pl.BlockSpec((1, tr, C), lambda i, kr: (j, i, 0))
    return pl.pallas_call(
        body, name=name,
        grid_spec=pltpu.PrefetchScalarGridSpec(
            num_scalar_prefetch=1, grid=(R // tr,),
            in_specs=[pl.BlockSpec((1, tr, C), lambda i, kr: (kr[0], i, 0)), rspec(0), rspec(1), rspec(2)],
            out_specs=pl.BlockSpec((tr, C), lambda i, kr: (i, 0))),
        out_shape=SDS((R, C), f32), compiler_params=_params(("parallel",), 40),
    )(kc, p, recv, recv, recv)


def _rs_pair(gs, c, tag):
    got = _swap_halves(gs, tag)
    pairs = [_add_own_half(g, r, c, f"rs_add_pair_{tag}_{a}") for a, (g, r) in enumerate(zip(gs, got))]
    return [p[0] for p in pairs], [p[1] for p in pairs]


def _rs_finish(ps, arrived, c, kc, tag):
    qs = [_add_chip_parts(p, r, kc, f"rs_add_chips_{tag}_{a}") for a, (p, r) in enumerate(zip(ps, arrived))]
    return qs, _share_halves(qs, tag)


def _adamw_math(w, g, m, v):
    m2 = ADAM_B1 * m + (1.0 - ADAM_B1) * g
    v2 = ADAM_B2 * v + (1.0 - ADAM_B2) * (g * g)
    m_hat = m2 / (1.0 - ADAM_B1 ** ADAM_STEP)
    v_hat = v2 / (1.0 - ADAM_B2 ** ADAM_STEP)
    return -ADAM_LR * (m_hat / (jnp.sqrt(v_hat) + ADAM_EPS) + ADAM_WD * w), m2, v2


def _adamw_shard(w, g_own, g_other, m, v, c, name):
    L, R, C = w.shape
    h = R // 2
    tr = _row_tile(h, 208)
    nh = h // tr

    def body(c_ref, w_ref, go0, gx0, go1, gx1, m_ref, v_ref, g_o, d_o, m_o, v_o):
        mine = (pl.program_id(1) // nh) == c_ref[0]
        first = pl.program_id(0) == 0
        g = jnp.where(mine, jnp.where(first, go0[...], go1[...]), jnp.where(first, gx0[...], gx1[...]))[None]
        d, m2, v2 = _adamw_math(w_ref[...], g, m_ref[...], v_ref[...])
        g_o[...], d_o[...], m_o[...], v_o[...] = g, d, m2, v2

    full = pl.BlockSpec((1, tr, C), lambda l, i, cr: (l, i, 0))
    half = pl.BlockSpec((tr, C), lambda l, i, cr: (i % nh, 0))
    return pl.pallas_call(
        body, name=name,
        grid_spec=pltpu.PrefetchScalarGridSpec(num_scalar_prefetch=1, grid=(L, R // tr),
                                               in_specs=[full, half, half, half, half, full, full],
                                               out_specs=[full] * 4),
        out_shape=[SDS((L, R, C), f32)] * 4, compiler_params=_params(("parallel", "parallel"), 48),
    )(c, w, g_own[0], g_other[0], g_own[1], g_other[1], m, v)


def _adamw(w, g, m, v, name):
    R, C = w.shape
    tr = 256 if R % 256 == 0 else R

    def body(w_ref, g_ref, m_ref, v_ref, d_o, m_o, v_o):
        d_o[...], m_o[...], v_o[...] = _adamw_math(w_ref[...], g_ref[...], m_ref[...], v_ref[...])

    blk = pl.BlockSpec((tr, C), lambda i: (i, 0))
    return pl.pallas_call(body, name=name, grid=(R // tr,), in_specs=[blk] * 4, out_specs=[blk] * 3,
                          out_shape=[SDS((R, C), f32)] * 3, compiler_params=_params(("parallel",), 40))(w, g, m, v)


def _pack(arrs):
    flat = [a.reshape(-1, LANES) for a in arrs]
    rows = sum(f.shape[0] for f in flat)
    pad = (-rows) % 16
    if pad:
        flat.append(jnp.zeros((pad, LANES), f32))
    return jnp.concatenate(flat, axis=0)


def _unpack(buf, shapes):
    out, r = [], 0
    for s in shapes:
        n = math.prod(s) // LANES
        out.append(buf[r:r + n].reshape(s))
        r += n
    return out


REPLICATED = ("shift_mu", "w_decay0", "a0", "k_k", "k_a", "r_k", "ln_x_w", "ln_x_b", "v_mix0", "lb_logits", "g_norm_w",
              "ln_w", "ln_b")
CHIP_SMALL = (("w_decay_up", 2), ("a_up", 2), ("v_mix_down", 1), ("v_mix_up", 2))
WEIGHTS = ("w_in", "shift_mu", "w_decay0", "w_decay_up", "a0", "a_up", "k_k", "k_a", "r_k", "ln_x_w", "ln_x_b",
           "v_mix0", "v_mix_down", "v_mix_up", "lb_logits", "g_norm_w", "w_out", "ln_w", "ln_b")


def kernel(x, w_in, shift_mu, w_decay0, w_decay_up, a0, a_up, k_k, k_a, r_k, ln_x_w, ln_x_b, v_mix0, v_mix_down, v_mix_up, lb_logits, g_norm_w, w_out, ln_w, ln_b, loss_target, m_w_in, m_shift_mu, m_w_decay0, m_w_decay_up, m_a0, m_a_up, m_k_k, m_k_a, m_r_k, m_ln_x_w, m_ln_x_b, m_v_mix0, m_v_mix_down, m_v_mix_up, m_lb_logits, m_g_norm_w, m_w_out, m_ln_w, m_ln_b, v_w_in, v_shift_mu, v_w_decay0, v_w_decay_up, v_a0, v_a_up, v_k_k, v_k_a, v_r_k, v_ln_x_w, v_ln_x_b, v_v_mix0, v_v_mix_down, v_v_mix_up, v_lb_logits, v_g_norm_w, v_w_out, v_ln_w, v_ln_b):
    W = dict(w_in=w_in, shift_mu=shift_mu, w_decay0=w_decay0, w_decay_up=w_decay_up, a0=a0, a_up=a_up, k_k=k_k, k_a=k_a,
             r_k=r_k, ln_x_w=ln_x_w, ln_x_b=ln_x_b, v_mix0=v_mix0, v_mix_down=v_mix_down, v_mix_up=v_mix_up,
             lb_logits=lb_logits, g_norm_w=g_norm_w, w_out=w_out, ln_w=ln_w, ln_b=ln_b)
    M = dict(w_in=m_w_in, shift_mu=m_shift_mu, w_decay0=m_w_decay0, w_decay_up=m_w_decay_up, a0=m_a0, a_up=m_a_up,
             k_k=m_k_k, k_a=m_k_a, r_k=m_r_k, ln_x_w=m_ln_x_w, ln_x_b=m_ln_x_b, v_mix0=m_v_mix0,
             v_mix_down=m_v_mix_down, v_mix_up=m_v_mix_up, lb_logits=m_lb_logits, g_norm_w=m_g_norm_w, w_out=m_w_out,
             ln_w=m_ln_w, ln_b=m_ln_b)
    V = dict(w_in=v_w_in, shift_mu=v_shift_mu, w_decay0=v_w_decay0, w_decay_up=v_w_decay_up, a0=v_a0, a_up=v_a_up,
             k_k=v_k_k, k_a=v_k_a, r_k=v_r_k, ln_x_w=v_ln_x_w, ln_x_b=v_ln_x_b, v_mix0=v_v_mix0,
             v_mix_down=v_v_mix_down, v_mix_up=v_v_mix_up, lb_logits=v_lb_logits, g_norm_w=v_g_norm_w, w_out=v_w_out,
             ln_w=v_ln_w, ln_b=v_ln_b)
    mx, my, mc = _mesh_pos()
    kc = (2 * mx + my).astype(jnp.int32)
    c_arr = jnp.reshape(mc.astype(jnp.int32), (1,))
    kc_arr = jnp.reshape(kc, (1,))

    small_shard = _pack([W[n] for n, _ in CHIP_SMALL])
    tr_in = lambda a: jnp.swapaxes(a, 1, 2)
    in_halves = tr_in(w_in).astype(bf16).reshape(DEPTH, 2, SHARD_COLS // 2, D_MODEL)
    out_halves = jnp.swapaxes(w_out.astype(bf16).reshape(DEPTH, 2, D_MODEL // 8, D_MODEL), 0, 1)
    g_in0, g_small = _gather_shards([in_halves[0], small_shard.reshape(2, -1, LANES)])
    in_full = lambda g: g.reshape(IN_COLS, D_MODEL)
    shard_shapes = [W[n].shape for n, _ in CHIP_SMALL]
    g_small = g_small.reshape(4, -1, LANES)
    per_chip = [_unpack(g_small[k], shard_shapes) for k in range(4)]
    full_small = {n: jnp.concatenate([per_chip[k][i] for k in range(4)], axis=ax)
                  for i, (n, ax) in enumerate(CHIP_SMALL)}

    def assemble_later(gathered):
        g_in1, g_out = gathered
        return in_full(g_in1), jnp.transpose(g_out, (2, 0, 1, 3, 4)).reshape(DEPTH, D_MODEL, D_MODEL)

    in_slabs = lambda g: g.reshape(4, SHARD_COLS, D_MODEL)
    out_slabs = lambda g: g.reshape(4, D_MODEL // 4, D_MODEL)
    layer1 = {}

    def on_layer1_grads(dw_in, dw_out):
        layer1["pairs"], to_send = _rs_pair([in_slabs(dw_in), out_slabs(dw_out)], c_arr, "l1")
        return to_send

    def on_layer1_recv(arrived):
        layer1["arrived"] = arrived

    loss_sum, gx, G = _device_step(
        x[0], loss_target[0], in_full(g_in0), [in_halves[1], out_halves], assemble_later, on_layer1_grads,
        on_layer1_recv, shift_mu, w_decay0, full_small["w_decay_up"], a0, full_small["a_up"], k_k, k_a, r_k, ln_x_w,
        ln_x_b, v_mix0, full_small["v_mix_down"], full_small["v_mix_up"], lb_logits, g_norm_w, ln_w, ln_b)
    loss = lax.psum(loss_sum[0, 0], ("x", "y", "c"))

    own1, other1 = _rs_finish(layer1["pairs"], layer1["arrived"], c_arr, kc_arr, "l1")
    pairs0, to_send0 = _rs_pair([in_slabs(G["w_in"][0]), out_slabs(G["w_out"][0])], c_arr, "l0")
    own0, other0 = _rs_finish(pairs0, _send_to_chips(to_send0, "l0"), c_arr, kc_arr, "l0")
    gin_own, gout_own = [own0[0], own1[0]], [own0[1], own1[1]]
    gin_other, gout_other = [other0[0], other1[0]], [other0[1], other1[1]]

    small_names = list(REPLICATED) + [n for n, _ in CHIP_SMALL]
    small_sum = _allreduce_small(_pack([G[n] for n in small_names]))
    small_full = dict(zip(small_names, _unpack(small_sum, [G[n].shape for n in small_names])))
    grads = {n: small_full[n] for n in REPLICATED}
    for n, ax in CHIP_SMALL:
        width = W[n].shape[ax]
        grads[n] = lax.dynamic_slice_in_dim(small_full[n], kc * width, width, axis=ax)

    delta, new_m, new_v = {}, {}, {}
    res = _adamw_shard(tr_in(w_in), gin_own, gin_other, tr_in(m_w_in), tr_in(v_w_in), c_arr, "adamw_w_in")
    grads["w_in"], delta["w_in"], new_m["w_in"], new_v["w_in"] = [tr_in(a) for a in res]
    grads["w_out"], delta["w_out"], new_m["w_out"], new_v["w_out"] = _adamw_shard(
        w_out, gout_own, gout_other, m_w_out, v_w_out, c_arr, "adamw_w_out")
    shapes = [W[n].shape for n in small_names]
    d_, m_, v_ = _adamw(_pack([W[n] for n in small_names]), _pack([grads[n] for n in small_names]),
                        _pack([M[n] for n in small_names]), _pack([V[n] for n in small_names]), "adamw_small")
    for n, d1, m1, v1 in zip(small_names, _unpack(d_, shapes), _unpack(m_, shapes), _unpack(v_, shapes)):
        delta[n], new_m[n], new_v[n] = d1, m1, v1

    return (loss, gx[None], *[grads[n] for n in WEIGHTS], *[delta[n] for n in WEIGHTS],
            *[new_m[n] for n in WEIGHTS], *[new_v[n] for n in WEIGHTS])
```

```python
import functools
import math

import jax
import jax.numpy as jnp
from jax import lax
from jax.experimental import pallas as pl
from jax.experimental.pallas import tpu as pltpu

f32 = jnp.float32
bf16 = jnp.bfloat16
SDS = jax.ShapeDtypeStruct
MESH = pl.DeviceIdType.MESH

D_MODEL = 2048
DEPTH = 2
D_RWKV = 1024
D_HGRN = 1024
RWKV_HEAD = 64
RWKV_HEADS = 16
LORA = 64
VRES = 32
HG_HEAD = 128
HG_HEADS = 8
CHUNK = 64
RWKV_COLS = 4 * D_RWKV + 2 * LORA
IN_COLS = RWKV_COLS + 4 * D_HGRN
SHARD_COLS = IN_COLS // 4
LANES = 128
RW_BLKS = RWKV_COLS // LANES
ALPHA = (2 * DEPTH) ** 0.25
LN_EPS = 1e-5
GN_EPS = 64e-5
RMS_EPS = 1e-5
LB_FLOOR = 1e-30
EXP_M05 = math.exp(-0.5)
ADAM_LR, ADAM_B1, ADAM_B2, ADAM_EPS, ADAM_WD, ADAM_STEP = 0.001, 0.9, 0.999, 1e-08, 0.01, 10

TB = 128
SCAN_TB = 8
SCAN_PARTS = 2
HSUB = 32
HIGHEST = lax.Precision.HIGHEST


def _params(sem=None, vmem_mb=None):
    kw = {}
    if sem is not None:
        kw["dimension_semantics"] = sem
    if vmem_mb is not None:
        kw["vmem_limit_bytes"] = vmem_mb << 20
    return pltpu.CompilerParams(**kw)


def _split(x, n):
    parts, r = [], x
    for i in range(n):
        p = r.astype(bf16)
        parts.append(p)
        if i + 1 < n:
            r = r - p.astype(f32)
    return parts


def _dot_parts(x, m, n=3):
    R = x.shape[0]
    res = jnp.dot(jnp.concatenate(_split(x, n), axis=0), m, preferred_element_type=f32)
    acc = res[0:R]
    for i in range(1, n):
        acc = acc + res[i * R:(i + 1) * R]
    return acc


def _seg_raw(x, ee, n=3, per_tile=False):
    R = x.shape[0]
    if per_tile:
        outs = []
        for c in range(4):
            tile_parts = _split(x[:, 256 * c:256 * c + 256], n)
            res = jnp.dot(jnp.concatenate(tile_parts, axis=0), ee, preferred_element_type=f32)
            acc = res[0:R]
            for i in range(1, n):
                acc = acc + res[i * R:(i + 1) * R]
            outs.append(acc)
        return jnp.concatenate(outs, axis=1)
    parts = _split(x, n)
    lhs = jnp.concatenate([p[:, 256 * c:256 * c + 256] for p in parts for c in range(4)], axis=0)
    res = jnp.dot(lhs, ee, preferred_element_type=f32)

    def tile(c):
        acc = res[c * R:(c + 1) * R]
        for i in range(1, n):
            acc = acc + res[(4 * i + c) * R:(4 * i + c + 1) * R]
        return acc

    return jnp.concatenate([tile(c) for c in range(4)], axis=1)


@jax.custom_vjp
def _seg(x, ee):
    return _seg_raw(x, ee)


def _seg_fwd(x, ee):
    return _seg_raw(x, ee), ee


def _seg_bwd(ee, g):
    return _seg_raw(g, ee), jnp.zeros_like(ee)


_seg.defvjp(_seg_fwd, _seg_bwd)


def _const_mats():
    i256 = jnp.arange(256) // RWKV_HEAD
    ee = (i256[:, None] == i256[None, :]).astype(bf16)
    head = jnp.arange(D_RWKV) // RWKV_HEAD
    lane = jnp.arange(LANES)
    e16 = (head[:, None] == lane[None, :]).astype(bf16)
    e16t3 = ((lane[:, None] % RWKV_HEADS == head[None, :]) & (lane[:, None] < 3 * RWKV_HEADS)).astype(bf16)
    return ee, e16, e16t3


def _matmul(a, b, *, mode, tm, tn, tk, name, add=None, b_layer=None):
    bs = b.shape if b_layer is None else b.shape[1:]
    lead = () if b_layer is None else (None,)
    bidx = (lambda *t: t) if b_layer is None else (lambda *t: (b_layer,) + t)
    if mode == "nn":
        (M, K), N = a.shape, bs[1]
        tm, tn, tk = min(tm, M), min(tn, N), min(tk, K)
        a_spec = pl.BlockSpec((tm, tk), lambda i, j, k: (i, k))
        b_spec = pl.BlockSpec(lead + (tk, tn), lambda i, j, k: bidx(k, j))
        dims = (((1,), (0,)), ((), ()))
    elif mode == "nt":
        (M, K), N = a.shape, bs[0]
        tm, tn, tk = min(tm, M), min(tn, N), min(tk, K)
        a_spec = pl.BlockSpec((tm, tk), lambda i, j, k: (i, k))
        b_spec = pl.BlockSpec(lead + (tn, tk), lambda i, j, k: bidx(j, k))
        dims = (((1,), (1,)), ((), ()))
    else:
        (K, M), N = a.shape, bs[1]
        tm, tn, tk = min(tm, M), min(tn, N), min(tk, K)
        a_spec = pl.BlockSpec((tk, tm), lambda i, j, k: (k, i))
        b_spec = pl.BlockSpec(lead + (tk, tn), lambda i, j, k: bidx(k, j))
        dims = (((0,), (0,)), ((), ()))
    assert M % tm == 0 and N % tn == 0 and K % tk == 0, (M, N, K, tm, tn, tk)
    has_add = add is not None

    def body(*refs):
        if has_add:
            a_ref, b_ref, c_ref, o_ref = refs
        else:
            a_ref, b_ref, o_ref = refs
        k = pl.program_id(2)
        p = lax.dot_general(a_ref[...].astype(bf16), b_ref[...].astype(bf16), dims, preferred_element_type=f32)

        @pl.when(k == 0)
        def _():
            o_ref[...] = p + c_ref[...] if has_add else p

        @pl.when(k > 0)
        def _():
            o_ref[...] += p

    o_spec = pl.BlockSpec((tm, tn), lambda i, j, k: (i, j))
    in_specs = [a_spec, b_spec] + ([o_spec] if has_add else [])
    args = (a, b) + ((add,) if has_add else ())
    return pl.pallas_call(
        body, name=name, grid=(M // tm, N // tn, K // tk), in_specs=in_specs, out_specs=o_spec,
        out_shape=SDS((M, N), f32),
        compiler_params=_params(("parallel", "parallel", "arbitrary"), 56),
    )(*args)


def _rwkv_core(k_in, v_in, wdad, w0, wup, a0, aup, kkw, kaw, ee, vmix):
    w_raw = w0 + jnp.dot(jnp.tanh(wdad), wup, preferred_element_type=f32)
    decay = jnp.exp(-EXP_M05 * jax.nn.sigmoid(w_raw))
    a = jax.nn.sigmoid(a0 + jnp.dot(wdad, aup, preferred_element_type=f32))
    if vmix is None:
        v = v_in
    else:
        v0, vdn, vup, vfirst = vmix
        gate = jax.nn.sigmoid(v0 + jnp.dot(jnp.dot(v_in, vdn, preferred_element_type=f32), vup,
                                           preferred_element_type=f32))
        v = v_in + (vfirst - v_in) * gate
    kk = k_in * kkw
    n2 = _seg(kk * kk, ee)
    kk = kk / jnp.maximum(jnp.sqrt(n2), 1e-12)
    k2 = k_in * (1.0 + (a - 1.0) * kaw)
    return decay, k2, v, -kk, kk * a


def _shifted(y, prev_row):
    yp = pltpu.roll(y, 1, 0)
    row = lax.broadcasted_iota(jnp.int32, y.shape, 0)
    return jnp.where(row == 0, prev_row, yp)


def _rwkv_pre_fwd(proj, mu, w0, wup, a0, aup, kkw, kaw, ee, vmix, layer):
    T = proj.shape[0]
    nb = T // TB
    has_mix = vmix is not None

    def body(*refs):
        y_ref, p8_ref, mu_ref, w0_ref, wup_ref, a0_ref, aup_ref, kkw_ref, kaw_ref, ee_ref = refs[:10]
        rest = refs[10:]
        if has_mix:
            v0_ref, vdn_ref, vup_ref, vf_ref = rest[:4]
            rest = rest[4:]
        r_o, w_o, k_o, v_o, a_o, b_o, z_o = rest
        i = pl.program_id(0)
        y = y_ref[...]
        prev = jnp.where(i == 0, 0.0, p8_ref[7:8, :])
        rw = y + mu_ref[...] * (_shifted(y, prev) - y)
        mix = (v0_ref[...], vdn_ref[...], vup_ref[...], vf_ref[...]) if has_mix else None
        dec, k2, v, av, bv = _rwkv_core(rw[:, 1024:2048], rw[:, 2048:3072], rw[:, 4096:4224], w0_ref[...],
                                        wup_ref[...], a0_ref[...], aup_ref[...], kkw_ref[...], kaw_ref[...],
                                        ee_ref[...], mix)
        r_o[...] = rw[:, 0:1024]
        w_o[...] = dec
        k_o[...] = k2
        v_o[...] = v
        a_o[...] = av
        b_o[...] = bv
        z_o[...] = rw[:, 3072:4096]

    row = lambda n: pl.BlockSpec((1, n), lambda i: (0, 0))
    full = lambda s: pl.BlockSpec(s, lambda i: (0,) * len(s))
    blk = pl.BlockSpec((TB, D_RWKV), lambda i: (i, 0))
    in_specs = [pl.BlockSpec((TB, RWKV_COLS), lambda i: (i, 0)),
                pl.BlockSpec((8, RWKV_COLS), lambda i: (jnp.maximum(i * (TB // 8) - 1, 0), 0)),
                row(RWKV_COLS), row(D_RWKV), full((LANES, D_RWKV)), row(D_RWKV), full((LANES, D_RWKV)),
                row(D_RWKV), row(D_RWKV), full((256, 256))]
    args = [proj, proj, mu, w0, wup, a0, aup, kkw, kaw, ee]
    if has_mix:
        v0, vdn, vup, vfirst = vmix
        in_specs += [row(D_RWKV), full((D_RWKV, LANES)), full((LANES, D_RWKV)), blk]
        args += [v0, vdn, vup, vfirst]
    return pl.pallas_call(
        body, name=f"rwkv_pre_fwd_l{layer}", grid=(nb,), in_specs=in_specs, out_specs=[blk] * 7,
        out_shape=[SDS((T, D_RWKV), f32)] * 7, compiler_params=_params(("parallel",), 48),
    )(*args)


def _rwkv_pre_bwd(proj, mu, w0, wup, a0, aup, kkw, kaw, ee, vmix, cots, layer):
    T = proj.shape[0]
    nb = T // TB
    has_mix = vmix is not None
    cot_list = cots["r"] + [cots["w"]] + cots["k"] + cots["v"] + [cots["a"], cots["b"], cots["z"]]
    n_r, n_k, n_v = len(cots["r"]), len(cots["k"]), len(cots["v"])
    n_cot = len(cot_list)

    def body(*refs):
        y_ref, p8_ref, mu_ref, w0_ref, wup_ref, a0_ref, aup_ref, kkw_ref, kaw_ref, ee_ref = refs[:10]
        rest = refs[10:]
        if has_mix:
            v0_ref, vdn_ref, vup_ref, vf_ref = rest[:4]
            rest = rest[4:]
        cot_refs, rest = rest[:n_cot], rest[n_cot:]
        if has_mix:
            (dproj_o, dmu_o, dw0_o, dwup_o, da0_o, daup_o, dkkw_o, dkaw_o,
             dv0_o, dvdn_o, dvup_o, dvf_o, carry) = rest
        else:
            dproj_o, dmu_o, dw0_o, dwup_o, da0_o, daup_o, dkkw_o, dkaw_o, carry = rest
        i = pl.program_id(0)
        blk_i = nb - 1 - i
        y = y_ref[...]
        prev = jnp.where(blk_i == 0, 0.0, p8_ref[7:8, :])
        yp = _shifted(y, prev)
        mu_v = mu_ref[...]
        rw = y + mu_v * (yp - y)

        def sum_refs(rs):
            acc = rs[0][...]
            for r_ in rs[1:]:
                acc = acc + r_[...]
            return acc

        c = list(cot_refs)
        g_r = sum_refs(c[:n_r]); c = c[n_r:]
        g_w = c[0][...]; c = c[1:]
        g_k = sum_refs(c[:n_k]); c = c[n_k:]
        g_v = sum_refs(c[:n_v]); c = c[n_v:]
        g_a, g_b, g_z = c[0][...], c[1][...], c[2][...]

        ee_v = ee_ref[...]
        if has_mix:
            def fn(k_in, v_in, wdad, w0_, wup_, a0_, aup_, kkw_, kaw_, v0_, vdn_, vup_, vf_):
                return _rwkv_core(k_in, v_in, wdad, w0_, wup_, a0_, aup_, kkw_, kaw_, ee_v, (v0_, vdn_, vup_, vf_))
            prim = (rw[:, 1024:2048], rw[:, 2048:3072], rw[:, 4096:4224], w0_ref[...], wup_ref[...], a0_ref[...],
                    aup_ref[...], kkw_ref[...], kaw_ref[...], v0_ref[...], vdn_ref[...], vup_ref[...], vf_ref[...])
        else:
            def fn(k_in, v_in, wdad, w0_, wup_, a0_, aup_, kkw_, kaw_):
                return _rwkv_core(k_in, v_in, wdad, w0_, wup_, a0_, aup_, kkw_, kaw_, ee_v, None)
            prim = (rw[:, 1024:2048], rw[:, 2048:3072], rw[:, 4096:4224], w0_ref[...], wup_ref[...], a0_ref[...],
                    aup_ref[...], kkw_ref[...], kaw_ref[...])
        _, vjp = jax.vjp(fn, *prim)
        gs = vjp((g_w, g_k, g_v, g_a, g_b))
        d_k, d_v, d_wdad = gs[0], gs[1], gs[2]
        g_rw = jnp.concatenate([g_r, d_k, d_v, g_z, d_wdad], axis=1)

        @pl.when(i == 0)
        def _():
            carry[...] = jnp.zeros_like(carry)
            dmu_o[...] = jnp.zeros_like(dmu_o)
            dw0_o[...] = jnp.zeros_like(dw0_o)
            dwup_o[...] = jnp.zeros_like(dwup_o)
            da0_o[...] = jnp.zeros_like(da0_o)
            daup_o[...] = jnp.zeros_like(daup_o)
            dkkw_o[...] = jnp.zeros_like(dkkw_o)
            dkaw_o[...] = jnp.zeros_like(dkaw_o)
            if has_mix:
                dv0_o[...] = jnp.zeros_like(dv0_o)
                dvdn_o[...] = jnp.zeros_like(dvdn_o)
                dvup_o[...] = jnp.zeros_like(dvup_o)

        dmu_o[...] += jnp.sum(g_rw * (yp - y), axis=0, keepdims=True)
        dw0_o[...] += gs[3]
        dwup_o[...] += gs[4]
        da0_o[...] += gs[5]
        daup_o[...] += gs[6]
        dkkw_o[...] += gs[7]
        dkaw_o[...] += gs[8]
        if has_mix:
            dv0_o[...] += gs[9]
            dvdn_o[...] += gs[10]
            dvup_o[...] += gs[11]
            dvf_o[...] = gs[12]
        gm = g_rw * mu_v
        nxt = pltpu.roll(gm, TB - 1, 0)
        rowi = lax.broadcasted_iota(jnp.int32, gm.shape, 0)
        nxt = jnp.where(rowi == TB - 1, carry[...], nxt)
        dproj_o[...] = g_rw - gm + nxt
        carry[...] = gm[0:1, :]

    rev = lambda i: (nb - 1 - i, 0)
    row = lambda n: pl.BlockSpec((1, n), lambda i: (0, 0))
    full = lambda s: pl.BlockSpec(s, lambda i: (0,) * len(s))
    blk = pl.BlockSpec((TB, D_RWKV), rev)
    in_specs = [pl.BlockSpec((TB, RWKV_COLS), rev),
                pl.BlockSpec((8, RWKV_COLS), lambda i: (jnp.maximum((nb - 1 - i) * (TB // 8) - 1, 0), 0)),
                row(RWKV_COLS), row(D_RWKV), full((LANES, D_RWKV)), row(D_RWKV), full((LANES, D_RWKV)),
                row(D_RWKV), row(D_RWKV), full((256, 256))]
    args = [proj, proj, mu, w0, wup, a0, aup, kkw, kaw, ee]
    out_specs = [pl.BlockSpec((TB, RWKV_COLS), rev), row(RWKV_COLS), row(D_RWKV), full((LANES, D_RWKV)),
                 row(D_RWKV), full((LANES, D_RWKV)), row(D_RWKV), row(D_RWKV)]
    out_shape = [SDS((T, RWKV_COLS), f32), SDS((1, RWKV_COLS), f32), SDS((1, D_RWKV), f32), SDS((LANES, D_RWKV), f32),
                 SDS((1, D_RWKV), f32), SDS((LANES, D_RWKV), f32), SDS((1, D_RWKV), f32), SDS((1, D_RWKV), f32)]
    if has_mix:
        v0, vdn, vup, vfirst = vmix
        in_specs += [row(D_RWKV), full((D_RWKV, LANES)), full((LANES, D_RWKV)), blk]
        args += [v0, vdn, vup, vfirst]
        out_specs += [row(D_RWKV), full((D_RWKV, LANES)), full((LANES, D_RWKV)), blk]
        out_shape += [SDS((1, D_RWKV), f32), SDS((D_RWKV, LANES), f32), SDS((LANES, D_RWKV), f32),
                      SDS((T, D_RWKV), f32)]
    in_specs += [blk] * n_cot
    args += cot_list
    return pl.pallas_call(
        body, name=f"rwkv_pre_bwd_l{layer}", grid=(nb,), in_specs=in_specs, out_specs=out_specs, out_shape=out_shape,
        scratch_shapes=[pltpu.VMEM((1, RWKV_COLS), f32)],
        compiler_params=_params(("arbitrary",), 60),
    )(*args)


def _to_col(v):
    T = v.shape[0]
    vc = jnp.swapaxes(v.reshape(T, RWKV_HEADS, RWKV_HEAD), 1, 2)
    hi = lax.reduce_precision(vc, 8, 7)
    r1 = vc - hi
    mid = lax.reduce_precision(r1, 8, 7)
    parts = jnp.concatenate([hi, mid, r1 - mid], axis=2).astype(bf16)
    return jnp.pad(parts, ((0, 0), (0, 0), (0, LANES - 3 * RWKV_HEADS)))


def _from_col(vc):
    T = vc.shape[0]
    return jnp.swapaxes(vc[:, :, :RWKV_HEADS], 1, 2).reshape(T, D_RWKV)


def _rwkv_scan_fwd(r, w, k, vcol, a, b, ee, e16, e16t, layer, rider, gather=()):
    T = r.shape[0]
    nb = T // SCAN_TB
    ng = len(gather)

    def body(*refs):
        r_ref, w_ref, k_ref, v_ref, a_ref, an_ref, b_ref, ee_ref, e16_ref, e16t_ref = refs[:10]
        rd_ins, refs = refs[10:10 + rider.n_in], refs[10 + rider.n_in:]
        g_ins, refs = refs[:ng], refs[ng:]
        o_ref, sp_ref = refs[:2]
        rd_outs, refs = refs[2:2 + rider.n_out], refs[2 + rider.n_out:]
        g_outs, refs = refs[:ng], refs[ng:]
        s_ref, sa_ref, vb_ref, po_ref = refs[:4]
        rd_scr, g_sems = refs[4:4 + rider.n_scr], refs[4 + rider.n_scr:]
        step = pl.program_id(0)
        rider.step(step, rd_ins, rd_outs, rd_scr)

        @pl.when(step == 0)
        def _():
            s_ref[...] = jnp.zeros_like(s_ref)
            sa_ref[...] = jnp.zeros_like(sa_ref)
            if ng:
                _gather_start(g_ins, g_outs, *g_sems)

        if ng:
            @pl.when(step == (3 * nb) // 4)
            def _():
                _gather_forward(g_ins, g_outs, *g_sems)

        ee_v = ee_ref[...]
        r_b, w_b, k_b, a_b, b_b = r_ref[...], w_ref[...], k_ref[...], a_ref[...], b_ref[...]
        rowi = lax.broadcasted_iota(jnp.int32, a_b.shape, 0)
        a_nx = jnp.where(rowi == SCAN_TB - 1, an_ref[0:1, :], pltpu.roll(a_b, SCAN_TB - 1, 0))
        wa = w_b * a_nx
        dots = _seg_raw(jnp.concatenate([b_b * a_nx, k_b * a_nx], axis=0), ee_v, 3)
        beta, kappa = dots[0:SCAN_TB], dots[SCAN_TB:2 * SCAN_TB]
        vb_ref[...] = jnp.dot(v_ref[...].reshape(SCAN_TB * RWKV_HEAD, LANES), e16t_ref[...],
                              preferred_element_type=f32)

        for t in range(SCAN_TB):
            row = lambda x: x[t:t + 1, :]
            blk = pl.ds(t * RWKV_HEAD, RWKV_HEAD)
            sp = s_ref[...]
            sa = sa_ref[...]
            sp_ref[t, 0] = sp
            sp_ref[t, 1] = sa
            vb = vb_ref[blk, :]
            ahead = _seg_raw(sp * row(wa), ee_v, SCAN_PARTS, per_tile=True)
            sn = sp * row(w_b) + sa * row(b_b) + vb * row(k_b)
            s_ref[...] = sn
            sa_ref[...] = ahead + sa * row(beta) + vb * row(kappa)
            po_ref[blk, :] = sn * row(r_b)
        o_ref[...] = _dot_parts(po_ref[...], e16_ref[...], 1).reshape(SCAN_TB, RWKV_HEAD, LANES)

        if ng:
            @pl.when(step == nb - 1)
            def _():
                _gather_finish(g_ins, g_outs, *g_sems)

    rows = pl.BlockSpec((SCAN_TB, D_RWKV), lambda i: (i, 0))
    rows_next = pl.BlockSpec((SCAN_TB, D_RWKV), lambda i: (jnp.minimum(i + 1, nb - 1), 0))
    col = pl.BlockSpec((SCAN_TB, RWKV_HEAD, LANES), lambda i: (i, 0, 0))
    full = lambda s: pl.BlockSpec(s, lambda i: (0,) * len(s))
    outs = pl.pallas_call(
        body, name=f"rwkv_scan_fwd_l{layer}", grid=(nb,),
        in_specs=[rows, rows, rows, col, rows, rows_next, rows, full((256, 256)), full((D_RWKV, LANES)),
                  full((LANES, D_RWKV))] + rider.in_specs + [HBM_SPEC] * ng,
        out_specs=[col, pl.BlockSpec((SCAN_TB, 2, RWKV_HEAD, D_RWKV), lambda i: (i, 0, 0, 0))] + rider.out_specs
        + [HBM_SPEC] * ng,
        out_shape=[SDS((T, RWKV_HEAD, LANES), f32), SDS((T, 2, RWKV_HEAD, D_RWKV), f32)] + rider.out_shape
        + _gather_out_shapes(gather),
        scratch_shapes=[pltpu.VMEM((RWKV_HEAD, D_RWKV), f32), pltpu.VMEM((RWKV_HEAD, D_RWKV), f32),
                        pltpu.VMEM((SCAN_TB * RWKV_HEAD, D_RWKV), f32),
                        pltpu.VMEM((SCAN_TB * RWKV_HEAD, D_RWKV), f32)] + rider.scratch
        + (_gather_sems(ng) if ng else []),
        compiler_params=_params(("arbitrary",), 52),
    )(r, w, k, vcol, a, a, b, ee, e16, e16t, *rider.args, *gather)
    return outs[0], outs[1], list(outs[2:2 + rider.n_out]), list(outs[2 + rider.n_out:])


def _rwkv_scan_bwd(r, w, k, vcol, a, b, sprev, docol, ee, e16, e16t, layer, send=()):
    T = r.shape[0]
    nb = T // SCAN_TB
    ns = len(send)

    def body(*refs):
        r_ref, w_ref, k_ref, v_ref, a_ref, b_ref, sp_ref, do_ref, ee_ref, e16_ref, e16t_ref = refs[:11]
        c_ins, refs = refs[11:11 + ns], refs[11 + ns:]
        dr_o, dw_o, dk_o, da_o, db_o, dv_o = refs[:6]
        c_outs, refs = refs[6:6 + ns], refs[6 + ns:]
        ds_ref, snext_ref, vb_ref, dob_ref, pdv_ref = refs[:5]
        c_sems = refs[5:]
        if ns:
            @pl.when(pl.program_id(0) == 0)
            def _():
                for cp in _chips_copies(c_ins, c_outs, *c_sems):
                    cp.start()

        ee_v, e16t_v = ee_ref[...], e16t_ref[...]
        csum = lambda x: jnp.sum(x, axis=0, keepdims=True)
        nrow = SCAN_TB * RWKV_HEAD
        last = pl.ds(nrow - RWKV_HEAD, RWKV_HEAD)
        r_b, w_b, k_b, a_b, b_b = r_ref[...], w_ref[...], k_ref[...], a_ref[...], b_ref[...]
        vb_ref[...] = jnp.dot(v_ref[...].reshape(nrow, LANES), e16t_v, preferred_element_type=f32)
        dob_ref[...] = jnp.dot(do_ref[...].reshape(nrow, LANES), e16t_v, preferred_element_type=f32)

        @pl.when(pl.program_id(0) == 0)
        def _():
            ds_ref[...] = jnp.zeros_like(ds_ref)
            snext_ref[...] = (sp_ref[SCAN_TB - 1, 0] * w_b[SCAN_TB - 1:SCAN_TB, :]
                              + sp_ref[SCAN_TB - 1, 1] * b_b[SCAN_TB - 1:SCAN_TB, :]
                              + vb_ref[last, :] * k_b[SCAN_TB - 1:SCAN_TB, :])

        for s in range(SCAN_TB):
            t = SCAN_TB - 1 - s
            row = lambda x: x[t:t + 1, :]
            blk = pl.ds(t * RWKV_HEAD, RWKV_HEAD)
            sp, sa = sp_ref[t, 0], sp_ref[t, 1]
            sn = snext_ref[...] if s == 0 else sp_ref[t + 1, 0]
            vb, dob = vb_ref[blk, :], dob_ref[blk, :]
            ds = ds_ref[...] + dob * row(r_b)
            dsa = _seg_raw(ds * row(b_b), ee_v, SCAN_PARTS, per_tile=True)
            ds_ref[...] = ds * row(w_b) + dsa * row(a_b)
            dr_o[t:t + 1, :] = csum(sn * dob)
            pdv_ref[blk, :] = ds * row(k_b)
            dk_o[t:t + 1, :] = csum(ds * vb)
            db_o[t:t + 1, :] = csum(ds * sa)
            dw_o[t:t + 1, :] = csum(ds * sp)
            da_o[t:t + 1, :] = csum(sp * dsa)
        snext_ref[...] = sp_ref[0, 0]
        dv_o[...] = _dot_parts(pdv_ref[...], e16_ref[...], 1).reshape(SCAN_TB, RWKV_HEAD, LANES)

        if ns:
            @pl.when(pl.program_id(0) == nb - 1)
            def _():
                for cp in _chips_copies(c_ins, c_outs, *c_sems):
                    cp.wait()

    rows = pl.BlockSpec((SCAN_TB, D_RWKV), lambda i: (nb - 1 - i, 0))
    col = pl.BlockSpec((SCAN_TB, RWKV_HEAD, LANES), lambda i: (nb - 1 - i, 0, 0))
    st = pl.BlockSpec((SCAN_TB, 2, RWKV_HEAD, D_RWKV), lambda i: (nb - 1 - i, 0, 0, 0))
    full = lambda s: pl.BlockSpec(s, lambda i: (0,) * len(s))
    big = pltpu.VMEM((SCAN_TB * RWKV_HEAD, D_RWKV), f32)
    outs = pl.pallas_call(
        body, name=f"rwkv_scan_bwd_l{layer}", grid=(nb,),
        in_specs=[rows, rows, rows, col, rows, rows, st, col, full((256, 256)), full((D_RWKV, LANES)),
                  full((LANES, D_RWKV))] + [HBM_SPEC] * ns,
        out_specs=[rows] * 5 + [col] + [HBM_SPEC] * ns,
        out_shape=[SDS((T, D_RWKV), f32)] * 5 + [SDS((T, RWKV_HEAD, LANES), f32)] + _chips_out_shapes(send),
        scratch_shapes=[pltpu.VMEM((RWKV_HEAD, D_RWKV), f32)] * 2 + [big] * 3 + (_chips_sems(ns) if ns else []),
        compiler_params=_params(("arbitrary",), 56),
    )(r, w, k, vcol, a, b, sprev, docol, ee, e16, e16t, *send)
    return (*outs[:6], list(outs[6:]))


def _post_core(o, r, k2, v, z, gw, gb, rk, ee):
    inv = 1.0 / RWKV_HEAD
    mu = _seg(o, ee) * inv
    d = o - mu
    var = _seg(d * d, ee) * inv
    on = d * lax.rsqrt(var + GN_EPS) * gw + gb
    bonus = _seg(r * k2 * rk, ee) * v
    return (on + bonus) * jax.nn.silu(z)


def _rwkv_post_fwd(o, r, k2, v, z, gw, gb, rk, ee, cat, layer):
    T = o.shape[0]

    def body(o_ref, r_ref, k_ref, v_ref, z_ref, gw_ref, gb_ref, rk_ref, ee_ref, cat_in, out_ref):
        del cat_in
        out_ref[...] = _post_core(o_ref[...], r_ref[...], k_ref[...], v_ref[...], z_ref[...], gw_ref[...],
                                  gb_ref[...], rk_ref[...], ee_ref[...])

    blk = pl.BlockSpec((TB, D_RWKV), lambda i: (i, 0))
    row = pl.BlockSpec((1, D_RWKV), lambda i: (0, 0))
    return pl.pallas_call(
        body, name=f"rwkv_post_fwd_l{layer}", grid=(T // TB,),
        in_specs=[blk] * 5 + [row] * 3 + [pl.BlockSpec((256, 256), lambda i: (0, 0)), HBM_SPEC],
        out_specs=blk, out_shape=SDS((T, D_MODEL), f32), input_output_aliases={9: 0},
        compiler_params=_params(("parallel",), 40),
    )(o, r, k2, v, z, gw, gb, rk, ee, cat)


def _rwkv_post_bwd(o, r, k2, v, z, gw, gb, rk, ee, dcat, layer):
    T = o.shape[0]

    def body(o_ref, r_ref, k_ref, v_ref, z_ref, gw_ref, gb_ref, rk_ref, ee_ref, g_ref,
             do_o, dr_o, dk_o, dv_o, dz_o, dgw_o, dgb_o, drk_o):
        ee_v = ee_ref[...]
        fn = lambda o_, r_, k_, v_, z_, gw_, gb_, rk_: _post_core(o_, r_, k_, v_, z_, gw_, gb_, rk_, ee_v)
        _, vjp = jax.vjp(fn, o_ref[...], r_ref[...], k_ref[...], v_ref[...], z_ref[...], gw_ref[...], gb_ref[...],
                         rk_ref[...])
        gs = vjp(g_ref[...])
        do_o[...], dr_o[...], dk_o[...], dv_o[...], dz_o[...] = gs[:5]

        @pl.when(pl.program_id(0) == 0)
        def _():
            dgw_o[...] = jnp.zeros_like(dgw_o)
            dgb_o[...] = jnp.zeros_like(dgb_o)
            drk_o[...] = jnp.zeros_like(drk_o)

        dgw_o[...] += gs[5]
        dgb_o[...] += gs[6]
        drk_o[...] += gs[7]

    blk = pl.BlockSpec((TB, D_RWKV), lambda i: (i, 0))
    row = pl.BlockSpec((1, D_RWKV), lambda i: (0, 0))
    return pl.pallas_call(
        body, name=f"rwkv_post_bwd_l{layer}", grid=(T // TB,),
        in_specs=[blk] * 5 + [row] * 3 + [pl.BlockSpec((256, 256), lambda i: (0, 0)), blk],
        out_specs=[blk] * 5 + [row] * 3,
        out_shape=[SDS((T, D_RWKV), f32)] * 5 + [SDS((1, D_RWKV), f32)] * 3,
        compiler_params=_params(("arbitrary",), 48),
    )(o, r, k2, v, z, gw, gb, rk, ee, dcat)


def _hgrn_chunk(qr, fr, ii, z, S, lbl, gw, layer):
    L = CHUNK
    m = jnp.max(lbl, axis=0, keepdims=True)
    e = jnp.exp(lbl - m)
    sm = e / jnp.sum(e, axis=0, keepdims=True)
    if layer == 0:
        lb = sm[0:1] - sm[0:1]
    else:
        lb = (sm[0:1] + sm[1:2]) - sm[0:1]
    q = jax.nn.silu(qr)
    log_lb = jnp.log(jnp.maximum(lb, LB_FLOOR))
    lf = jnp.logaddexp(log_lb, jnp.log1p(-lb) + jax.nn.log_sigmoid(fr))
    k = (1.0 - lb) * jax.nn.sigmoid(-fr)
    ti = lax.broadcasted_iota(jnp.int32, (L, L), 0)
    si = lax.broadcasted_iota(jnp.int32, (L, L), 1)
    ltri = (si <= ti).astype(f32)
    b = jnp.dot(ltri, lf, precision=HIGHEST, preferred_element_type=f32)
    btot = jnp.dot(jnp.ones((L, L), f32), lf, precision=HIGHEST, preferred_element_type=f32)
    o = jnp.dot(q * jnp.exp(b), S, preferred_element_type=f32)
    nsub = L // HSUB
    t3 = lax.broadcasted_iota(jnp.int32, (HSUB, HSUB, HG_HEAD), 0)
    s3 = lax.broadcasted_iota(jnp.int32, (HSUB, HSUB, HG_HEAD), 1)
    causal = s3 <= t3
    rows = []
    for I in range(nsub):
        sl = slice(I * HSUB, (I + 1) * HSUB)
        bI, qI, kI, iI = b[sl], q[sl], k[sl], ii[sl]
        d3 = bI[:, None, :] - bI[None, :, :]
        dec = jnp.where(causal, jnp.exp(jnp.where(causal, d3, 0.0)), 0.0)
        att = jnp.sum(qI[:, None, :] * dec * kI[None, :, :], axis=-1)
        oI = jnp.dot(att, iI, preferred_element_type=f32)
        for J in range(I):
            sj = slice(J * HSUB, (J + 1) * HSUB)
            bm = b[(J + 1) * HSUB - 1:(J + 1) * HSUB]
            qs = qI * jnp.exp(bI - bm)
            ks = k[sj] * jnp.exp(bm - b[sj])
            att_ij = lax.dot_general(qs, ks, (((1,), (1,)), ((), ())), preferred_element_type=f32)
            oI = oI + jnp.dot(att_ij, ii[sj], preferred_element_type=f32)
        rows.append(oI)
    o = o + jnp.concatenate(rows, axis=0)
    k_dec = k * jnp.exp(btot - b)
    bcol = lax.dot_general(lf, jnp.ones((L, HG_HEAD), f32), (((0,), (0,)), ((), ())), precision=HIGHEST,
                           preferred_element_type=f32)
    s_new = S * jnp.exp(bcol) + lax.dot_general(k_dec, ii, (((0,), (0,)), ((), ())), preferred_element_type=f32)
    ms = jnp.mean(o * o, axis=-1, keepdims=True)
    out = o * lax.rsqrt(ms + RMS_EPS) * gw * jax.nn.silu(z)
    return out, s_new


class _Part:
    def __init__(self, args, in_specs, out_specs, out_shape, scratch, step):
        self.args, self.in_specs, self.out_specs, self.out_shape = args, in_specs, out_specs, out_shape
        self.scratch, self.step = scratch, step
        self.n_in, self.n_out, self.n_scr = len(args), len(out_shape), len(scratch)


def _run_part(part, steps, name):
    def body(*refs):
        ins, outs = refs[:part.n_in], refs[part.n_in:part.n_in + part.n_out]
        part.step(pl.program_id(0), ins, outs, refs[part.n_in + part.n_out:])

    return pl.pallas_call(body, name=name, grid=(steps,), in_specs=part.in_specs, out_specs=part.out_specs,
                          out_shape=part.out_shape, scratch_shapes=part.scratch,
                          compiler_params=_params(("arbitrary",), 48))(*part.args)


def _hg_index(T, rev):
    nc = T // CHUNK
    head = lambda i: i // nc
    chunk = (lambda i: nc - 1 - i % nc) if rev else (lambda i: i % nc)
    first = lambda i: i % nc == 0
    pj = lambda off: pl.BlockSpec((CHUNK, HG_HEAD), lambda i: (chunk(i), RW_BLKS + off + head(i)))
    per_head = lambda rows: pl.BlockSpec((rows, HG_HEAD), lambda i: (0, head(i)))
    return nc, head, chunk, first, [pj(0), pj(HG_HEADS), pj(2 * HG_HEADS), pj(3 * HG_HEADS)], per_head


def _hgrn_fwd_part(proj, lbl, gw, layer):
    T = proj.shape[0]
    nc, head, chunk, first, pspecs, per_head = _hg_index(T, False)

    def step(i, ins, outs, scr):
        q_ref, f_ref, i_ref, z_ref, lbl_ref, gw_ref = ins
        out_ref, sh_ref = outs
        s_ref, = scr

        @pl.when(first(i))
        def _():
            s_ref[...] = jnp.zeros_like(s_ref)

        s0 = s_ref[...]
        sh_ref[0, 0] = s0
        out, s_new = _hgrn_chunk(q_ref[...], f_ref[...], i_ref[...], z_ref[...], s0, lbl_ref[...], gw_ref[...], layer)
        out_ref[...] = out
        s_ref[...] = s_new

    return _Part(
        [proj, proj, proj, proj, lbl, gw], pspecs + [per_head(2), per_head(1)],
        [pl.BlockSpec((CHUNK, HG_HEAD), lambda i: (chunk(i), HG_HEADS + head(i))),
         pl.BlockSpec((1, 1, HG_HEAD, HG_HEAD), lambda i: (head(i), chunk(i), 0, 0))],
        [SDS((T, D_MODEL), f32), SDS((HG_HEADS, nc, HG_HEAD, HG_HEAD), f32)],
        [pltpu.VMEM((HG_HEAD, HG_HEAD), f32)], step)


def _hgrn_bwd_part(proj, shist, dcat, lbl, gw, layer):
    T = proj.shape[0]
    nc, head, chunk, first, pspecs, per_head = _hg_index(T, True)

    def step(i, ins, outs, scr):
        q_ref, f_ref, i_ref, z_ref, lbl_ref, gw_ref, sh_ref, g_ref = ins
        dq_o, df_o, di_o, dz_o, dlbl_o, dgw_o = outs
        ds_ref, = scr

        @pl.when(first(i))
        def _():
            ds_ref[...] = jnp.zeros_like(ds_ref)
            dlbl_o[...] = jnp.zeros_like(dlbl_o)
            dgw_o[...] = jnp.zeros_like(dgw_o)

        fn = functools.partial(_hgrn_chunk, layer=layer)
        _, vjp = jax.vjp(fn, q_ref[...], f_ref[...], i_ref[...], z_ref[...], sh_ref[0, 0], lbl_ref[...], gw_ref[...])
        dq, df, di, dz, ds, dl, dg = vjp((g_ref[...], ds_ref[...]))
        dq_o[...], df_o[...], di_o[...], dz_o[...] = dq, df, di, dz
        ds_ref[...] = ds
        dlbl_o[...] += dl
        dgw_o[...] += dg

    ospec = pl.BlockSpec((CHUNK, HG_HEAD), lambda i: (chunk(i), head(i)))
    return _Part(
        [proj, proj, proj, proj, lbl, gw, shist, dcat],
        pspecs + [per_head(2), per_head(1),
                  pl.BlockSpec((1, 1, HG_HEAD, HG_HEAD), lambda i: (head(i), chunk(i), 0, 0)),
                  pl.BlockSpec((CHUNK, HG_HEAD), lambda i: (chunk(i), HG_HEADS + head(i)))],
        [ospec] * 4 + [per_head(2), per_head(1)],
        [SDS((T, D_HGRN), f32)] * 4 + [SDS((2, D_HGRN), f32), SDS((1, D_HGRN), f32)],
        [pltpu.VMEM((HG_HEAD, HG_HEAD), f32)], step)


def _ln_core(h, y, w, b):
    u = ALPHA * h + y
    mu = jnp.mean(u, axis=-1, keepdims=True)
    d = u - mu
    var = jnp.mean(d * d, axis=-1, keepdims=True)
    return d * lax.rsqrt(var + LN_EPS) * w + b


def _ln_fwd(h, y, w, b, layer):
    T = h.shape[0]

    def body(h_ref, y_ref, w_ref, b_ref, o_ref):
        o_ref[...] = _ln_core(h_ref[...], y_ref[...], w_ref[...], b_ref[...])

    blk = pl.BlockSpec((TB, D_MODEL), lambda i: (i, 0))
    row = pl.BlockSpec((1, D_MODEL), lambda i: (0, 0))
    return pl.pallas_call(body, name=f"ln_fwd_l{layer}", grid=(T // TB,), in_specs=[blk, blk, row, row],
                          out_specs=blk, out_shape=SDS((T, D_MODEL), f32),
                          compiler_params=_params(("parallel",), 40))(h, y, w, b)


def _ln_bwd(h, y, w, b, g, layer):
    T = h.shape[0]

    def body(h_ref, y_ref, w_ref, b_ref, g_ref, dh_o, dy_o, dw_o, db_o):
        _, vjp = jax.vjp(_ln_core, h_ref[...], y_ref[...], w_ref[...], b_ref[...])
        dh, dy, dw, db = vjp(g_ref[...])
        dh_o[...] = dh
        dy_o[...] = dy.astype(bf16)

        @pl.when(pl.program_id(0) == 0)
        def _():
            dw_o[...] = jnp.zeros_like(dw_o)
            db_o[...] = jnp.zeros_like(db_o)

        dw_o[...] += dw
        db_o[...] += db

    blk = pl.BlockSpec((TB, D_MODEL), lambda i: (i, 0))
    row = pl.BlockSpec((1, D_MODEL), lambda i: (0, 0))
    return pl.pallas_call(body, name=f"ln_bwd_l{layer}", grid=(T // TB,), in_specs=[blk, blk, row, row, blk],
                          out_specs=[blk, blk, row, row],
                          out_shape=[SDS((T, D_MODEL), f32), SDS((T, D_MODEL), bf16)] + [SDS((1, D_MODEL), f32)] * 2,
                          compiler_params=_params(("arbitrary",), 48))(h, y, w, b, g)


def _loss_head(hout, target):
    T = hout.shape[0]

    def body(h_ref, t_ref, loss_o, g_o):
        err = h_ref[...] - t_ref[...]
        g_o[...] = err * (1.0 / D_MODEL)

        @pl.when(pl.program_id(0) == 0)
        def _():
            loss_o[...] = jnp.zeros_like(loss_o)

        part = jnp.sum(jnp.sum(err * err, axis=-1, keepdims=True) * (1.0 / D_MODEL), axis=0, keepdims=True)
        loss_o[...] += 0.5 * part

    blk = pl.BlockSpec((TB, D_MODEL), lambda i: (i, 0))
    return pl.pallas_call(body, name="loss_head", grid=(T // TB,), in_specs=[blk, blk],
                          out_specs=[pl.BlockSpec((1, 1), lambda i: (0, 0)), blk],
                          out_shape=[SDS((1, 1), f32), SDS((T, D_MODEL), f32)],
                          compiler_params=_params(("arbitrary",), 40))(hout, target)


def _pad_rows(m, lo, total):
    return jnp.pad(m, ((0, 0), (lo, total - lo - m.shape[1]), (0, 0)))


def _device_step(x, target, w_in0, later_shards, assemble_later, on_layer1_grads, on_layer1_recv,
                 shift_mu, w_decay0, w_decay_up, a0, a_up, k_k, k_a, r_k, ln_x_w, ln_x_b,
                 v_mix0, v_mix_down, v_mix_up, lb_logits, g_norm_w, ln_w, ln_b):
    T = x.shape[0]
    ee, e16, e16t = _const_mats()
    wup_pad = _pad_rows(w_decay_up, 0, LANES)
    aup_pad = _pad_rows(a_up, LORA, LANES)
    vdn_pad = jnp.pad(v_mix_down, ((0, 0), (0, 0), (0, LANES - VRES)))
    vup_pad = _pad_rows(v_mix_up, 0, LANES)
    row = lambda p, l: p[l][None, :]

    def vmix_of(l, vfirst):
        if l == 0:
            return None
        return (row(v_mix0, l - 1), vdn_pad[l - 1], vup_pad[l - 1], vfirst)

    h = x
    saved = []
    vfirst = None
    w_in = [w_in0, None]
    w_out = None
    for l in range(DEPTH):
        hb = h.astype(bf16)
        proj = _matmul(hb, w_in[l], mode="nt", tm=512, tn=1664, tk=D_MODEL, name=f"proj_fwd_l{l}")
        pre_args = (proj, row(shift_mu, l), row(w_decay0, l), wup_pad[l], row(a0, l), aup_pad[l], row(k_k, l),
                    row(k_a, l), ee, vmix_of(l, vfirst))
        r, w, k2, v, av, bv, z = _rwkv_pre_fwd(*pre_args, layer=l)
        if l == 0:
            vfirst = v
        vcol = _to_col(v)
        ocol, sprev, (cat, shist), gathered = _rwkv_scan_fwd(
            r, w, k2, vcol, av, bv, ee, e16, e16t, l, _hgrn_fwd_part(proj, lb_logits, row(g_norm_w, l), l),
            gather=later_shards if l == 0 else [])
        if l == 0:
            w_in[1], w_out = assemble_later(gathered)
        o = _from_col(ocol)
        post_args = (o, r, k2, v, z, row(ln_x_w, l), row(ln_x_b, l), row(r_k, l), ee)
        cat = _rwkv_post_fwd(*post_args, cat, layer=l)
        catb = cat.astype(bf16)
        y = _matmul(catb, w_out, b_layer=l, mode="nn", tm=512, tn=1024, tk=D_MODEL, name=f"out_fwd_l{l}")
        h_new = _ln_fwd(h, y, row(ln_w, l), row(ln_b, l), l)
        saved.append(dict(h=h, hb=hb, proj=proj, pre_args=pre_args, post_args=post_args,
                          scan=(r, w, k2, vcol, av, bv, sprev), cat=catb, shist=shist, y=y))
        h = h_new

    loss_sum, g = _loss_head(h, target)

    grads = {n: [None] * DEPTH for n in ("w_in", "w_out", "shift_mu", "w_decay0", "w_decay_up", "a0", "a_up", "k_k",
                                         "k_a", "r_k", "ln_x_w", "ln_x_b", "g_norm_w", "ln_w", "ln_b")}
    dlbl = []
    dvfirst = None
    to_send = []
    for l in reversed(range(DEPTH)):
        s = saved[l]
        dh_res, dy, dlnw, dlnb = _ln_bwd(s["h"], s["y"], row(ln_w, l), row(ln_b, l), g, l)
        grads["ln_w"][l], grads["ln_b"][l] = dlnw[0], dlnb[0]
        dcat = _matmul(dy, w_out, b_layer=l, mode="nt", tm=512, tn=1024, tk=D_MODEL, name=f"out_bwd_dx_l{l}")
        grads["w_out"][l] = _matmul(s["cat"], dy, mode="tn", tm=1024, tn=1024, tk=512, name=f"out_bwd_dw_l{l}")
        do, dr_p, dk_p, dv_p, dz, dgw, dgb, drk = _rwkv_post_bwd(*s["post_args"], dcat, layer=l)
        grads["ln_x_w"][l], grads["ln_x_b"][l], grads["r_k"][l] = dgw[0], dgb[0], drk[0]
        r, w, k2, vcol, av, bv, sprev = s["scan"]
        dr_s, dw_s, dk_s, da_s, db_s, dvcol, arrived = _rwkv_scan_bwd(
            r, w, k2, vcol, av, bv, sprev, _to_col(do), ee, e16, e16t, l, send=to_send if l == 0 else [])
        dq, df, di, dzh, dl, dgn = _run_part(
            _hgrn_bwd_part(s["proj"], s["shist"], dcat, lb_logits, row(g_norm_w, l), l), T // SCAN_TB,
            f"hgrn_bwd_l{l}")
        if l == 0:
            on_layer1_recv(arrived)
        v_cots = [dv_p, _from_col(dvcol)] + ([dvfirst] if (l == 0 and dvfirst is not None) else [])
        cots = dict(r=[dr_p, dr_s], w=dw_s, k=[dk_p, dk_s], v=v_cots, a=da_s, b=db_s, z=dz)
        outs = _rwkv_pre_bwd(*s["pre_args"], cots, layer=l)
        dproj, dmu, dw0, dwup, da0, daup, dkkw, dkaw = outs[:8]
        grads["shift_mu"][l], grads["w_decay0"][l], grads["a0"][l] = dmu[0], dw0[0], da0[0]
        grads["k_k"][l], grads["k_a"][l] = dkkw[0], dkaw[0]
        grads["w_decay_up"][l], grads["a_up"][l] = dwup[:LORA], daup[LORA:]
        if l > 0:
            dv0, dvdn, dvup, dvfirst = outs[8:]
            g_vmix = (dv0, dvdn[:, :VRES][None], dvup[:VRES][None])
        dlbl.append(dl)
        grads["g_norm_w"][l] = dgn[0]
        dproj = jnp.concatenate([dproj, dq, df, di, dzh], axis=1).astype(bf16)
        grads["w_in"][l] = _matmul(dproj, s["hb"], mode="tn", tm=1664, tn=1024, tk=512, name=f"proj_bwd_dw_l{l}")
        g = _matmul(dproj, w_in[l], mode="nn", tm=512, tn=1024, tk=1664, name=f"proj_bwd_dx_l{l}", add=dh_res)
        if l == 1:
            to_send = on_layer1_grads(grads["w_in"][1], grads["w_out"][1])
    w_in_grads, w_out_grads = grads.pop("w_in"), grads.pop("w_out")
    out = {n: jnp.stack(v) for n, v in grads.items()}
    out["w_in"], out["w_out"] = w_in_grads, w_out_grads
    out["lb_logits"] = dlbl[0] + dlbl[1]
    out["v_mix0"], out["v_mix_down"], out["v_mix_up"] = g_vmix
    return loss_sum, g, out


CHIP_COMBOS = ((1, 0), (0, 1), (1, 1))
HBM_SPEC = pl.BlockSpec(memory_space=pl.ANY)


def _mesh_pos():
    return lax.axis_index("x"), lax.axis_index("y"), lax.axis_index("c")


def _flip(v, d):
    return 1 - v if d else v


def _half(ref, dim, cc, hs):
    idx = [slice(None)] * len(ref.shape)
    idx[dim] = pl.ds(cc * hs, hs)
    return ref.at[tuple(idx)]


def _gather_shards(arrs):
    n = len(arrs)

    def body(*refs):
        ins, outs = refs[:n], refs[n:2 * n]
        sems = refs[2 * n:]
        _gather_start(ins, outs, *sems)
        _gather_forward(ins, outs, *sems)
        _gather_finish(ins, outs, *sems)

    return pl.pallas_call(
        body, name="gather_shards", in_specs=[HBM_SPEC] * n, out_specs=[HBM_SPEC] * n,
        out_shape=_gather_out_shapes(arrs), scratch_shapes=_gather_sems(n),
    )(*arrs)


def _gather_out_shapes(arrs):
    return [SDS((4,) + a.shape, a.dtype) for a in arrs]


def _gather_sems(n):
    return [pltpu.SemaphoreType.DMA((n, 6)), pltpu.SemaphoreType.DMA((n, 6)), pltpu.SemaphoreType.DMA((n,))]


def _gather_copies(kind, ins, outs, send_sems, recv_sems, local_sems):
    x, y, c = _mesh_pos()
    kc = 2 * x + y
    cps = []
    for a in range(len(ins)):
        if kind == "local":
            cps.append(pltpu.make_async_copy(ins[a], outs[a].at[kc], local_sems.at[a]))
            continue
        for j, (dx, dy) in enumerate(CHIP_COMBOS):
            px, py = _flip(x, dx), _flip(y, dy)
            kj = 2 * px + py
            src, dst, s, to = {
                "sends": (ins[a].at[c], outs[a].at[kc, c], j, (px, py, c)),
                "landed": (outs[a].at[kj, c], outs[a].at[kj, c], j, (px, py, c)),
                "forwards": (outs[a].at[kj, c], outs[a].at[kj, c], 3 + j, (x, y, 1 - c)),
                "passed": (outs[a].at[kj, 1 - c], outs[a].at[kj, 1 - c], 3 + j, (x, y, 1 - c)),
            }[kind]
            cps.append(pltpu.make_async_remote_copy(
                src_ref=src, dst_ref=dst, send_sem=send_sems.at[a, s], recv_sem=recv_sems.at[a, s], device_id=to,
                device_id_type=MESH))
    return cps


def _gather_start(*refs):
    for cp in _gather_copies("local", *refs) + _gather_copies("sends", *refs):
        cp.start()


def _gather_forward(*refs):
    for arrival, fw in zip(_gather_copies("landed", *refs), _gather_copies("forwards", *refs)):
        arrival.wait_recv()
        fw.start()


def _gather_finish(*refs):
    for cp in _gather_copies("passed", *refs):
        cp.wait_recv()
    for cp in _gather_copies("sends", *refs) + _gather_copies("forwards", *refs):
        cp.wait_send()
    for cp in _gather_copies("local", *refs):
        cp.wait()


DEV_COMBOS = tuple((dx, dy, dc) for dx in (0, 1) for dy in (0, 1) for dc in (0, 1))[1:]


def _allreduce_small(buf):
    R = buf.shape[0]

    def body(x_ref, o_ref, slots, send_sems, recv_sems):
        x, y, c = _mesh_pos()
        me = 4 * x + 2 * y + c
        cps = []
        for k, (dx, dy, dc) in enumerate(DEV_COMBOS):
            cp = pltpu.make_async_remote_copy(
                src_ref=x_ref, dst_ref=slots.at[k + 1], send_sem=send_sems.at[k], recv_sem=recv_sems.at[k],
                device_id=(_flip(x, dx), _flip(y, dy), _flip(c, dc)), device_id_type=MESH)
            cp.start()
            cps.append(cp)
        slots[0] = x_ref[...]
        for cp in cps:
            cp.wait_recv()
        acc = slots[jnp.bitwise_xor(me, 0)]
        for d in range(1, 8):
            acc = acc + slots[jnp.bitwise_xor(me, d)]
        o_ref[...] = acc
        for cp in cps:
            cp.wait_send()

    vm = pl.BlockSpec(memory_space=pltpu.VMEM)
    return pl.pallas_call(
        body, name="allreduce_small", in_specs=[vm], out_specs=vm, out_shape=SDS((R, LANES), f32),
        scratch_shapes=[pltpu.VMEM((8, R, LANES), f32), pltpu.SemaphoreType.DMA((7,)), pltpu.SemaphoreType.DMA((7,))],
        compiler_params=_params(None, 48),
    )(buf)


def _swap_halves(gs, tag):
    n = len(gs)

    def body(*refs):
        ins, outs = refs[:n], refs[n:2 * n]
        send_sems, recv_sems = refs[2 * n:]
        x, y, c = _mesh_pos()
        cps = []
        for a in range(n):
            h = ins[a].shape[1] // 2
            cp = pltpu.make_async_remote_copy(
                src_ref=_half(ins[a], 1, 1 - c, h), dst_ref=outs[a], send_sem=send_sems.at[a],
                recv_sem=recv_sems.at[a], device_id=(x, y, 1 - c), device_id_type=MESH)
            cp.start()
            cps.append(cp)
        for cp in cps:
            cp.wait()

    return pl.pallas_call(
        body, name=f"rs_swap_halves_{tag}", in_specs=[HBM_SPEC] * n, out_specs=[HBM_SPEC] * n,
        out_shape=[SDS((g.shape[0], g.shape[1] // 2, g.shape[2]), g.dtype) for g in gs],
        scratch_shapes=[pltpu.SemaphoreType.DMA((n,)), pltpu.SemaphoreType.DMA((n,))],
    )(*gs)


def _chips_copies(ins, outs, send_sems, recv_sems):
    x, y, c = _mesh_pos()
    cps = []
    for a in range(len(ins)):
        for j, (dx, dy) in enumerate(CHIP_COMBOS):
            px, py = _flip(x, dx), _flip(y, dy)
            cps.append(pltpu.make_async_remote_copy(
                src_ref=ins[a].at[2 * px + py], dst_ref=outs[a].at[j], send_sem=send_sems.at[a, j],
                recv_sem=recv_sems.at[a, j], device_id=(px, py, c), device_id_type=MESH))
    return cps


def _chips_out_shapes(ps):
    return [SDS((3,) + p.shape[1:], p.dtype) for p in ps]


def _chips_sems(n):
    return [pltpu.SemaphoreType.DMA((n, 3)), pltpu.SemaphoreType.DMA((n, 3))]


def _send_to_chips(ps, tag):
    n = len(ps)

    def body(*refs):
        cps = _chips_copies(refs[:n], refs[n:2 * n], *refs[2 * n:])
        for cp in cps:
            cp.start()
        for cp in cps:
            cp.wait()

    return pl.pallas_call(
        body, name=f"rs_send_to_chips_{tag}", in_specs=[HBM_SPEC] * n, out_specs=[HBM_SPEC] * n,
        out_shape=_chips_out_shapes(ps), scratch_shapes=_chips_sems(n),
    )(*ps)


def _share_halves(qs, tag):
    n = len(qs)

    def body(*refs):
        ins, outs = refs[:n], refs[n:2 * n]
        send_sems, recv_sems = refs[2 * n:]
        x, y, c = _mesh_pos()
        cps = []
        for a in range(n):
            cp = pltpu.make_async_remote_copy(
                src_ref=ins[a], dst_ref=outs[a], send_sem=send_sems.at[a], recv_sem=recv_sems.at[a],
                device_id=(x, y, 1 - c), device_id_type=MESH)
            cp.start()
            cps.append(cp)
        for cp in cps:
            cp.wait()

    return pl.pallas_call(
        body, name=f"rs_share_halves_{tag}", in_specs=[HBM_SPEC] * n, out_specs=[HBM_SPEC] * n,
        out_shape=[SDS(q.shape, q.dtype) for q in qs],
        scratch_shapes=[pltpu.SemaphoreType.DMA((n,)), pltpu.SemaphoreType.DMA((n,))],
    )(*qs)


def _row_tile(rows, cap):
    return max(t for t in range(16, cap + 1, 16) if rows % t == 0)


def _add_own_half(g, recv, c, name):
    S, R, C = g.shape
    h = R // 2
    tr = _row_tile(h, 256)
    nt = h // tr

    def body(c_ref, g_ref, r_ref, o_ref, o16_ref):
        del c_ref
        s = g_ref[...] + r_ref[...]
        o_ref[...] = s
        o16_ref[...] = s.astype(bf16)

    ospec = pl.BlockSpec((1, tr, C), lambda s, i, cr: (s, i, 0))
    return pl.pallas_call(
        body, name=name,
        grid_spec=pltpu.PrefetchScalarGridSpec(
            num_scalar_prefetch=1, grid=(S, nt),
            in_specs=[pl.BlockSpec((1, tr, C), lambda s, i, cr: (s, cr[0] * nt + i, 0)), ospec],
            out_specs=[ospec, ospec]),
        out_shape=[SDS((S, h, C), f32), SDS((S, h, C), bf16)],
        compiler_params=_params(("parallel", "parallel"), 40),
    )(c, g, recv)


def _add_chip_parts(p, recv, kc, name):
    _, R, C = p.shape
    tr = _row_tile(R, 256)

    def body(k_ref, p_ref, r0, r1, r2, o_ref):
        del k_ref
        o_ref[...] = ((p_ref[0] + r0[0].astype(f32)) + r1[0].astype(f32)) + r2[0].astype(f32)

    rspec = lambda j: pl.BlockSpec((1, tr, C), lambda i, kr: (j, i, 0))
    return pl.pallas_call(
        body, name=name,
        grid_spec=pltpu.PrefetchScalarGridSpec(
            num_scalar_prefetch=1, grid=(R // tr,),
            in_specs=[pl.BlockSpec((1, tr, C), lambda i, kr: (kr[0], i, 0)), rspec(0), rspec(1), rspec(2)],
            out_specs=pl.BlockSpec((tr, C), lambda i, kr: (i, 0))),
        out_shape=SDS((R, C), f32), compiler_params=_params(("parallel",), 40),
    )(kc, p, recv, recv, recv)


def _rs_pair(gs, c, tag):
    got = _swap_halves(gs, tag)
    pairs = [_add_own_half(g, r, c, f"rs_add_pair_{tag}_{a}") for a, (g, r) in enumerate(zip(gs, got))]
    return [p[0] for p in pairs], [p[1] for p in pairs]


def _rs_finish(ps, arrived, c, kc, tag):
    qs = [_add_chip_parts(p, r, kc, f"rs_add_chips_{tag}_{a}") for a, (p, r) in enumerate(zip(ps, arrived))]
    return qs, _share_halves(qs, tag)


def _adamw_math(w, g, m, v):
    m2 = ADAM_B1 * m + (1.0 - ADAM_B1) * g
    v2 = ADAM_B2 * v + (1.0 - ADAM_B2) * (g * g)
    m_hat = m2 / (1.0 - ADAM_B1 ** ADAM_STEP)
    v_hat = v2 / (1.0 - ADAM_B2 ** ADAM_STEP)
    return -ADAM_LR * (m_hat / (jnp.sqrt(v_hat) + ADAM_EPS) + ADAM_WD * w), m2, v2


def _adamw_shard(w, g_own, g_other, m, v, c, name):
    L, R, C = w.shape
    h = R // 2
    tr = _row_tile(h, 208)
    nh = h // tr

    def body(c_ref, w_ref, go0, gx0, go1, gx1, m_ref, v_ref, g_o, d_o, m_o, v_o):
        mine = (pl.program_id(1) // nh) == c_ref[0]
        first = pl.program_id(0) == 0
        g = jnp.where(mine, jnp.where(first, go0[...], go1[...]), jnp.where(first, gx0[...], gx1[...]))[None]
        d, m2, v2 = _adamw_math(w_ref[...], g, m_ref[...], v_ref[...])
        g_o[...], d_o[...], m_o[...], v_o[...] = g, d, m2, v2

    full = pl.BlockSpec((1, tr, C), lambda l, i, cr: (l, i, 0))
    half = pl.BlockSpec((tr, C), lambda l, i, cr: (i % nh, 0))
    return pl.pallas_call(
        body, name=name,
        grid_spec=pltpu.PrefetchScalarGridSpec(num_scalar_prefetch=1, grid=(L, R // tr),
                                               in_specs=[full, half, half, half, half, full, full],
                                               out_specs=[full] * 4),
        out_shape=[SDS((L, R, C), f32)] * 4, compiler_params=_params(("parallel", "parallel"), 48),
    )(c, w, g_own[0], g_other[0], g_own[1], g_other[1], m, v)


def _adamw(w, g, m, v, name):
    R, C = w.shape
    tr = 256 if R % 256 == 0 else R

    def body(w_ref, g_ref, m_ref, v_ref, d_o, m_o, v_o):
        d_o[...], m_o[...], v_o[...] = _adamw_math(w_ref[...], g_ref[...], m_ref[...], v_ref[...])

    blk = pl.BlockSpec((tr, C), lambda i: (i, 0))
    return pl.pallas_call(body, name=name, grid=(R // tr,), in_specs=[blk] * 4, out_specs=[blk] * 3,
                          out_shape=[SDS((R, C), f32)] * 3, compiler_params=_params(("parallel",), 40))(w, g, m, v)


def _pack(arrs):
    flat = [a.reshape(-1, LANES) for a in arrs]
    rows = sum(f.shape[0] for f in flat)
    pad = (-rows) % 16
    if pad:
        flat.append(jnp.zeros((pad, LANES), f32))
    return jnp.concatenate(flat, axis=0)


def _unpack(buf, shapes):
    out, r = [], 0
    for s in shapes:
        n = math.prod(s) // LANES
        out.append(buf[r:r + n].reshape(s))
        r += n
    return out


REPLICATED = ("shift_mu", "w_decay0", "a0", "k_k", "k_a", "r_k", "ln_x_w", "ln_x_b", "v_mix0", "lb_logits", "g_norm_w",
              "ln_w", "ln_b")
CHIP_SMALL = (("w_decay_up", 2), ("a_up", 2), ("v_mix_down", 1), ("v_mix_up", 2))
WEIGHTS = ("w_in", "shift_mu", "w_decay0", "w_decay_up", "a0", "a_up", "k_k", "k_a", "r_k", "ln_x_w", "ln_x_b",
           "v_mix0", "v_mix_down", "v_mix_up", "lb_logits", "g_norm_w", "w_out", "ln_w", "ln_b")


def kernel(x, w_in, shift_mu, w_decay0, w_decay_up, a0, a_up, k_k, k_a, r_k, ln_x_w, ln_x_b, v_mix0, v_mix_down, v_mix_up, lb_logits, g_norm_w, w_out, ln_w, ln_b, loss_target, m_w_in, m_shift_mu, m_w_decay0, m_w_decay_up, m_a0, m_a_up, m_k_k, m_k_a, m_r_k, m_ln_x_w, m_ln_x_b, m_v_mix0, m_v_mix_down, m_v_mix_up, m_lb_logits, m_g_norm_w, m_w_out, m_ln_w, m_ln_b, v_w_in, v_shift_mu, v_w_decay0, v_w_decay_up, v_a0, v_a_up, v_k_k, v_k_a, v_r_k, v_ln_x_w, v_ln_x_b, v_v_mix0, v_v_mix_down, v_v_mix_up, v_lb_logits, v_g_norm_w, v_w_out, v_ln_w, v_ln_b):
    W = dict(w_in=w_in, shift_mu=shift_mu, w_decay0=w_decay0, w_decay_up=w_decay_up, a0=a0, a_up=a_up, k_k=k_k, k_a=k_a,
             r_k=r_k, ln_x_w=ln_x_w, ln_x_b=ln_x_b, v_mix0=v_mix0, v_mix_down=v_mix_down, v_mix_up=v_mix_up,
             lb_logits=lb_logits, g_norm_w=g_norm_w, w_out=w_out, ln_w=ln_w, ln_b=ln_b)
    M = dict(w_in=m_w_in, shift_mu=m_shift_mu, w_decay0=m_w_decay0, w_decay_up=m_w_decay_up, a0=m_a0, a_up=m_a_up,
             k_k=m_k_k, k_a=m_k_a, r_k=m_r_k, ln_x_w=m_ln_x_w, ln_x_b=m_ln_x_b, v_mix0=m_v_mix0,
             v_mix_down=m_v_mix_down, v_mix_up=m_v_mix_up, lb_logits=m_lb_logits, g_norm_w=m_g_norm_w, w_out=m_w_out,
             ln_w=m_ln_w, ln_b=m_ln_b)
    V = dict(w_in=v_w_in, shift_mu=v_shift_mu, w_decay0=v_w_decay0, w_decay_up=v_w_decay_up, a0=v_a0, a_up=v_a_up,
             k_k=v_k_k, k_a=v_k_a, r_k=v_r_k, ln_x_w=v_ln_x_w, ln_x_b=v_ln_x_b, v_mix0=v_v_mix0,
             v_mix_down=v_v_mix_down, v_mix_up=v_v_mix_up, lb_logits=v_lb_logits, g_norm_w=v_g_norm_w, w_out=v_w_out,
             ln_w=v_ln_w, ln_b=v_ln_b)
    mx, my, mc = _mesh_pos()
    kc = (2 * mx + my).astype(jnp.int32)
    c_arr = jnp.reshape(mc.astype(jnp.int32), (1,))
    kc_arr = jnp.reshape(kc, (1,))

    small_shard = _pack([W[n] for n, _ in CHIP_SMALL])
    tr_in = lambda a: jnp.swapaxes(a, 1, 2)
    in_halves = tr_in(w_in).astype(bf16).reshape(DEPTH, 2, SHARD_COLS // 2, D_MODEL)
    out_halves = jnp.swapaxes(w_out.astype(bf16).reshape(DEPTH, 2, D_MODEL // 8, D_MODEL), 0, 1)
    g_in0, g_small = _gather_shards([in_halves[0], small_shard.reshape(2, -1, LANES)])
    in_full = lambda g: g.reshape(IN_COLS, D_MODEL)
    shard_shapes = [W[n].shape for n, _ in CHIP_SMALL]
    g_small = g_small.reshape(4, -1, LANES)
    per_chip = [_unpack(g_small[k], shard_shapes) for k in range(4)]
    full_small = {n: jnp.concatenate([per_chip[k][i] for k in range(4)], axis=ax)
                  for i, (n, ax) in enumerate(CHIP_SMALL)}

    def assemble_later(gathered):
        g_in1, g_out = gathered
        return in_full(g_in1), jnp.transpose(g_out, (2, 0, 1, 3, 4)).reshape(DEPTH, D_MODEL, D_MODEL)

    in_slabs = lambda g: g.reshape(4, SHARD_COLS, D_MODEL)
    out_slabs = lambda g: g.reshape(4, D_MODEL // 4, D_MODEL)
    layer1 = {}

    def on_layer1_grads(dw_in, dw_out):
        layer1["pairs"], to_send = _rs_pair([in_slabs(dw_in), out_slabs(dw_out)], c_arr, "l1")
        return to_send

    def on_layer1_recv(arrived):
        layer1["arrived"] = arrived

    loss_sum, gx, G = _device_step(
        x[0], loss_target[0], in_full(g_in0), [in_halves[1], out_halves], assemble_later, on_layer1_grads,
        on_layer1_recv, shift_mu, w_decay0, full_small["w_decay_up"], a0, full_small["a_up"], k_k, k_a, r_k, ln_x_w,
        ln_x_b, v_mix0, full_small["v_mix_down"], full_small["v_mix_up"], lb_logits, g_norm_w, ln_w, ln_b)
    loss = lax.psum(loss_sum[0, 0], ("x", "y", "c"))

    own1, other1 = _rs_finish(layer1["pairs"], layer1["arrived"], c_arr, kc_arr, "l1")
    pairs0, to_send0 = _rs_pair([in_slabs(G["w_in"][0]), out_slabs(G["w_out"][0])], c_arr, "l0")
    own0, other0 = _rs_finish(pairs0, _send_to_chips(to_send0, "l0"), c_arr, kc_arr, "l0")
    gin_own, gout_own = [own0[0], own1[0]], [own0[1], own1[1]]
    gin_other, gout_other = [other0[0], other1[0]], [other0[1], other1[1]]

    small_names = list(REPLICATED) + [n for n, _ in CHIP_SMALL]
    small_sum = _allreduce_small(_pack([G[n] for n in small_names]))
    small_full = dict(zip(small_names, _unpack(small_sum, [G[n].shape for n in small_names])))
    grads = {n: small_full[n] for n in REPLICATED}
    for n, ax in CHIP_SMALL:
        width = W[n].shape[ax]
        grads[n] = lax.dynamic_slice_in_dim(small_full[n], kc * width, width, axis=ax)

    delta, new_m, new_v = {}, {}, {}
    res = _adamw_shard(tr_in(w_in), gin_own, gin_other, tr_in(m_w_in), tr_in(v_w_in), c_arr, "adamw_w_in")
    grads["w_in"], delta["w_in"], new_m["w_in"], new_v["w_in"] = [tr_in(a) for a in res]
    grads["w_out"], delta["w_out"], new_m["w_out"], new_v["w_out"] = _adamw_shard(
        w_out, gout_own, gout_other, m_w_out, v_w_out, c_arr, "adamw_w_out")
    shapes = [W[n].shape for n in small_names]
    d_, m_, v_ = _adamw(_pack([W[n] for n in small_names]), _pack([grads[n] for n in small_names]),
                        _pack([M[n] for n in small_names]), _pack([V[n] for n in small_names]), "adamw_small")
    for n, d1, m1, v1 in zip(small_names, _unpack(d_, shapes), _unpack(m_, shapes), _unpack(v_, shapes)):
        delta[n], new_m[n], new_v[n] = d1, m1, v1

    return (loss, gx[None], *[grads[n] for n in WEIGHTS], *[delta[n] for n in WEIGHTS],
            *[new_m[n] for n in WEIGHTS], *[new_v[n] for n in WEIGHTS])
```

```python
import functools
import math

import jax
import jax.numpy as jnp
from jax import lax
from jax.experimental import pallas as pl
from jax.experimental.pallas import tpu as pltpu

f32 = jnp.float32
bf16 = jnp.bfloat16
SDS = jax.ShapeDtypeStruct
MESH = pl.DeviceIdType.MESH

D_MODEL = 2048
DEPTH = 2
D_RWKV = 1024
D_HGRN = 1024
RWKV_HEAD = 64
RWKV_HEADS = 16
LORA = 64
VRES = 32
HG_HEAD = 128
HG_HEADS = 8
CHUNK = 64
RWKV_COLS = 4 * D_RWKV + 2 * LORA
IN_COLS = RWKV_COLS + 4 * D_HGRN
SHARD_COLS = IN_COLS // 4
LANES = 128
RW_BLKS = RWKV_COLS // LANES
ALPHA = (2 * DEPTH) ** 0.25
LN_EPS = 1e-5
GN_EPS = 64e-5
RMS_EPS = 1e-5
LB_FLOOR = 1e-30
EXP_M05 = math.exp(-0.5)
ADAM_LR, ADAM_B1, ADAM_B2, ADAM_EPS, ADAM_WD, ADAM_STEP = 0.001, 0.9, 0.999, 1e-08, 0.01, 10

TB = 128
SCAN_TB = 8
SCAN_PARTS = 2
HSUB = 32
HIGHEST = lax.Precision.HIGHEST


def _params(sem=None, vmem_mb=None):
    kw = {}
    if sem is not None:
        kw["dimension_semantics"] = sem
    if vmem_mb is not None:
        kw["vmem_limit_bytes"] = vmem_mb << 20
    return pltpu.CompilerParams(**kw)


def _split(x, n):
    parts, r = [], x
    for i in range(n):
        p = r.astype(bf16)
        parts.append(p)
        if i + 1 < n:
            r = r - p.astype(f32)
    return parts


def _dot_parts(x, m, n=3):
    R = x.shape[0]
    res = jnp.dot(jnp.concatenate(_split(x, n), axis=0), m, preferred_element_type=f32)
    acc = res[0:R]
    for i in range(1, n):
        acc = acc + res[i * R:(i + 1) * R]
    return acc


def _seg_raw(x, ee, n=3, per_tile=False):
    R = x.shape[0]
    if per_tile:
        outs = []
        for c in range(4):
            tile_parts = _split(x[:, 256 * c:256 * c + 256], n)
            res = jnp.dot(jnp.concatenate(tile_parts, axis=0), ee, preferred_element_type=f32)
            acc = res[0:R]
            for i in range(1, n):
                acc = acc + res[i * R:(i + 1) * R]
            outs.append(acc)
        return jnp.concatenate(outs, axis=1)
    parts = _split(x, n)
    lhs = jnp.concatenate([p[:, 256 * c:256 * c + 256] for p in parts for c in range(4)], axis=0)
    res = jnp.dot(lhs, ee, preferred_element_type=f32)

    def tile(c):
        acc = res[c * R:(c + 1) * R]
        for i in range(1, n):
            acc = acc + res[(4 * i + c) * R:(4 * i + c + 1) * R]
        return acc

    return jnp.concatenate([tile(c) for c in range(4)], axis=1)


@jax.custom_vjp
def _seg(x, ee):
    return _seg_raw(x, ee)


def _seg_fwd(x, ee):
    return _seg_raw(x, ee), ee


def _seg_bwd(ee, g):
    return _seg_raw(g, ee), jnp.zeros_like(ee)


_seg.defvjp(_seg_fwd, _seg_bwd)


def _const_mats():
    i256 = jnp.arange(256) // RWKV_HEAD
    ee = (i256[:, None] == i256[None, :]).astype(bf16)
    head = jnp.arange(D_RWKV) // RWKV_HEAD
    lane = jnp.arange(LANES)
    red = (head[:, None] == lane[None, :]).astype(bf16)
    exp = ((lane[:, None] % RWKV_HEADS == head[None, :]) & (lane[:, None] < 2 * RWKV_HEADS)).astype(bf16)
    return ee, red, exp


def _matmul(a, b, *, mode, tm, tn, tk, name, add=None, b_layer=None):
    bs = b.shape if b_layer is None else b.shape[1:]
    lead = () if b_layer is None else (None,)
    bidx = (lambda *t: t) if b_layer is None else (lambda *t: (b_layer,) + t)
    if mode == "nn":
        (M, K), N = a.shape, bs[1]
        tm, tn, tk = min(tm, M), min(tn, N), min(tk, K)
        a_spec = pl.BlockSpec((tm, tk), lambda i, j, k: (i, k))
        b_spec = pl.BlockSpec(lead + (tk, tn), lambda i, j, k: bidx(k, j))
        dims = (((1,), (0,)), ((), ()))
    elif mode == "nt":
        (M, K), N = a.shape, bs[0]
        tm, tn, tk = min(tm, M), min(tn, N), min(tk, K)
        a_spec = pl.BlockSpec((tm, tk), lambda i, j, k: (i, k))
        b_spec = pl.BlockSpec(lead + (tn, tk), lambda i, j, k: bidx(j, k))
        dims = (((1,), (1,)), ((), ()))
    else:
        (K, M), N = a.shape, bs[1]
        tm, tn, tk = min(tm, M), min(tn, N), min(tk, K)
        a_spec = pl.BlockSpec((tk, tm), lambda i, j, k: (k, i))
        b_spec = pl.BlockSpec(lead + (tk, tn), lambda i, j, k: bidx(k, j))
        dims = (((0,), (0,)), ((), ()))
    assert M % tm == 0 and N % tn == 0 and K % tk == 0, (M, N, K, tm, tn, tk)
    has_add = add is not None

    def body(*refs):
        if has_add:
            a_ref, b_ref, c_ref, o_ref = refs
        else:
            a_ref, b_ref, o_ref = refs
        k = pl.program_id(2)
        p = lax.dot_general(a_ref[...].astype(bf16), b_ref[...].astype(bf16), dims, preferred_element_type=f32)

        @pl.when(k == 0)
        def _():
            o_ref[...] = p + c_ref[...] if has_add else p

        @pl.when(k > 0)
        def _():
            o_ref[...] += p

    o_spec = pl.BlockSpec((tm, tn), lambda i, j, k: (i, j))
    in_specs = [a_spec, b_spec] + ([o_spec] if has_add else [])
    args = (a, b) + ((add,) if has_add else ())
    return pl.pallas_call(
        body, name=name, grid=(M // tm, N // tn, K // tk), in_specs=in_specs, out_specs=o_spec,
        out_shape=SDS((M, N), f32),
        compiler_params=_params(("parallel", "parallel", "arbitrary"), 56),
    )(*args)


def _rwkv_core(k_in, v_in, wdad, w0, wup, a0, aup, kkw, kaw, ee, vmix):
    w_raw = w0 + jnp.dot(jnp.tanh(wdad), wup, preferred_element_type=f32)
    decay = jnp.exp(-EXP_M05 * jax.nn.sigmoid(w_raw))
    a = jax.nn.sigmoid(a0 + jnp.dot(wdad, aup, preferred_element_type=f32))
    if vmix is None:
        v = v_in
    else:
        v0, vdn, vup, vfirst = vmix
        gate = jax.nn.sigmoid(v0 + jnp.dot(jnp.dot(v_in, vdn, preferred_element_type=f32), vup,
                                           preferred_element_type=f32))
        v = v_in + (vfirst - v_in) * gate
    kk = k_in * kkw
    n2 = _seg(kk * kk, ee)
    kk = kk / jnp.maximum(jnp.sqrt(n2), 1e-12)
    k2 = k_in * (1.0 + (a - 1.0) * kaw)
    return decay, k2, v, -kk, kk * a


def _shifted(y, prev_row):
    yp = pltpu.roll(y, 1, 0)
    row = lax.broadcasted_iota(jnp.int32, y.shape, 0)
    return jnp.where(row == 0, prev_row, yp)


def _rwkv_pre_fwd(proj, mu, w0, wup, a0, aup, kkw, kaw, ee, vmix, layer):
    T = proj.shape[0]
    nb = T // TB
    has_mix = vmix is not None

    def body(*refs):
        y_ref, p8_ref, mu_ref, w0_ref, wup_ref, a0_ref, aup_ref, kkw_ref, kaw_ref, ee_ref = refs[:10]
        rest = refs[10:]
        if has_mix:
            v0_ref, vdn_ref, vup_ref, vf_ref = rest[:4]
            rest = rest[4:]
        r_o, w_o, k_o, v_o, a_o, b_o, z_o = rest
        i = pl.program_id(0)
        y = y_ref[...]
        prev = jnp.where(i == 0, 0.0, p8_ref[7:8, :])
        rw = y + mu_ref[...] * (_shifted(y, prev) - y)
        mix = (v0_ref[...], vdn_ref[...], vup_ref[...], vf_ref[...]) if has_mix else None
        dec, k2, v, av, bv = _rwkv_core(rw[:, 1024:2048], rw[:, 2048:3072], rw[:, 4096:4224], w0_ref[...],
                                        wup_ref[...], a0_ref[...], aup_ref[...], kkw_ref[...], kaw_ref[...],
                                        ee_ref[...], mix)
        r_o[...] = rw[:, 0:1024]
        w_o[...] = dec
        k_o[...] = k2
        v_o[...] = v
        a_o[...] = av
        b_o[...] = bv
        z_o[...] = rw[:, 3072:4096]

    row = lambda n: pl.BlockSpec((1, n), lambda i: (0, 0))
    full = lambda s: pl.BlockSpec(s, lambda i: (0,) * len(s))
    blk = pl.BlockSpec((TB, D_RWKV), lambda i: (i, 0))
    in_specs = [pl.BlockSpec((TB, RWKV_COLS), lambda i: (i, 0)),
                pl.BlockSpec((8, RWKV_COLS), lambda i: (jnp.maximum(i * (TB // 8) - 1, 0), 0)),
                row(RWKV_COLS), row(D_RWKV), full((LANES, D_RWKV)), row(D_RWKV), full((LANES, D_RWKV)),
                row(D_RWKV), row(D_RWKV), full((256, 256))]
    args = [proj, proj, mu, w0, wup, a0, aup, kkw, kaw, ee]
    if has_mix:
        v0, vdn, vup, vfirst = vmix
        in_specs += [row(D_RWKV), full((D_RWKV, LANES)), full((LANES, D_RWKV)), blk]
        args += [v0, vdn, vup, vfirst]
    return pl.pallas_call(
        body, name=f"rwkv_pre_fwd_l{layer}", grid=(nb,), in_specs=in_specs, out_specs=[blk] * 7,
        out_shape=[SDS((T, D_RWKV), f32)] * 7, compiler_params=_params(("parallel",), 48),
    )(*args)


def _rwkv_pre_bwd(proj, mu, w0, wup, a0, aup, kkw, kaw, ee, vmix, cots, layer):
    T = proj.shape[0]
    nb = T // TB
    has_mix = vmix is not None
    cot_list = cots["r"] + [cots["w"]] + cots["k"] + cots["v"] + [cots["a"], cots["b"], cots["z"]]
    n_r, n_k, n_v = len(cots["r"]), len(cots["k"]), len(cots["v"])
    n_cot = len(cot_list)

    def body(*refs):
        y_ref, p8_ref, mu_ref, w0_ref, wup_ref, a0_ref, aup_ref, kkw_ref, kaw_ref, ee_ref = refs[:10]
        rest = refs[10:]
        if has_mix:
            v0_ref, vdn_ref, vup_ref, vf_ref = rest[:4]
            rest = rest[4:]
        cot_refs, rest = rest[:n_cot], rest[n_cot:]
        if has_mix:
            (dproj_o, dmu_o, dw0_o, dwup_o, da0_o, daup_o, dkkw_o, dkaw_o,
             dv0_o, dvdn_o, dvup_o, dvf_o, carry) = rest
        else:
            dproj_o, dmu_o, dw0_o, dwup_o, da0_o, daup_o, dkkw_o, dkaw_o, carry = rest
        i = pl.program_id(0)
        blk_i = nb - 1 - i
        y = y_ref[...]
        prev = jnp.where(blk_i == 0, 0.0, p8_ref[7:8, :])
        yp = _shifted(y, prev)
        mu_v = mu_ref[...]
        rw = y + mu_v * (yp - y)

        def sum_refs(rs):
            acc = rs[0][...]
            for r_ in rs[1:]:
                acc = acc + r_[...]
            return acc

        c = list(cot_refs)
        g_r = sum_refs(c[:n_r]); c = c[n_r:]
        g_w = c[0][...]; c = c[1:]
        g_k = sum_refs(c[:n_k]); c = c[n_k:]
        g_v = sum_refs(c[:n_v]); c = c[n_v:]
        g_a, g_b, g_z = c[0][...], c[1][...], c[2][...]

        ee_v = ee_ref[...]
        if has_mix:
            def fn(k_in, v_in, wdad, w0_, wup_, a0_, aup_, kkw_, kaw_, v0_, vdn_, vup_, vf_):
                return _rwkv_core(k_in, v_in, wdad, w0_, wup_, a0_, aup_, kkw_, kaw_, ee_v, (v0_, vdn_, vup_, vf_))
            prim = (rw[:, 1024:2048], rw[:, 2048:3072], rw[:, 4096:4224], w0_ref[...], wup_ref[...], a0_ref[...],
                    aup_ref[...], kkw_ref[...], kaw_ref[...], v0_ref[...], vdn_ref[...], vup_ref[...], vf_ref[...])
        else:
            def fn(k_in, v_in, wdad, w0_, wup_, a0_, aup_, kkw_, kaw_):
                return _rwkv_core(k_in, v_in, wdad, w0_, wup_, a0_, aup_, kkw_, kaw_, ee_v, None)
            prim = (rw[:, 1024:2048], rw[:, 2048:3072], rw[:, 4096:4224], w0_ref[...], wup_ref[...], a0_ref[...],
                    aup_ref[...], kkw_ref[...], kaw_ref[...])
        _, vjp = jax.vjp(fn, *prim)
        gs = vjp((g_w, g_k, g_v, g_a, g_b))
        d_k, d_v, d_wdad = gs[0], gs[1], gs[2]
        g_rw = jnp.concatenate([g_r, d_k, d_v, g_z, d_wdad], axis=1)

        @pl.when(i == 0)
        def _():
            carry[...] = jnp.zeros_like(carry)
            dmu_o[...] = jnp.zeros_like(dmu_o)
            dw0_o[...] = jnp.zeros_like(dw0_o)
            dwup_o[...] = jnp.zeros_like(dwup_o)
            da0_o[...] = jnp.zeros_like(da0_o)
            daup_o[...] = jnp.zeros_like(daup_o)
            dkkw_o[...] = jnp.zeros_like(dkkw_o)
            dkaw_o[...] = jnp.zeros_like(dkaw_o)
            if has_mix:
                dv0_o[...] = jnp.zeros_like(dv0_o)
                dvdn_o[...] = jnp.zeros_like(dvdn_o)
                dvup_o[...] = jnp.zeros_like(dvup_o)

        dmu_o[...] += jnp.sum(g_rw * (yp - y), axis=0, keepdims=True)
        dw0_o[...] += gs[3]
        dwup_o[...] += gs[4]
        da0_o[...] += gs[5]
        daup_o[...] += gs[6]
        dkkw_o[...] += gs[7]
        dkaw_o[...] += gs[8]
        if has_mix:
            dv0_o[...] += gs[9]
            dvdn_o[...] += gs[10]
            dvup_o[...] += gs[11]
            dvf_o[...] = gs[12]
        gm = g_rw * mu_v
        nxt = pltpu.roll(gm, TB - 1, 0)
        rowi = lax.broadcasted_iota(jnp.int32, gm.shape, 0)
        nxt = jnp.where(rowi == TB - 1, carry[...], nxt)
        dproj_o[...] = g_rw - gm + nxt
        carry[...] = gm[0:1, :]

    rev = lambda i: (nb - 1 - i, 0)
    row = lambda n: pl.BlockSpec((1, n), lambda i: (0, 0))
    full = lambda s: pl.BlockSpec(s, lambda i: (0,) * len(s))
    blk = pl.BlockSpec((TB, D_RWKV), rev)
    in_specs = [pl.BlockSpec((TB, RWKV_COLS), rev),
                pl.BlockSpec((8, RWKV_COLS), lambda i: (jnp.maximum((nb - 1 - i) * (TB // 8) - 1, 0), 0)),
                row(RWKV_COLS), row(D_RWKV), full((LANES, D_RWKV)), row(D_RWKV), full((LANES, D_RWKV)),
                row(D_RWKV), row(D_RWKV), full((256, 256))]
    args = [proj, proj, mu, w0, wup, a0, aup, kkw, kaw, ee]
    out_specs = [pl.BlockSpec((TB, RWKV_COLS), rev), row(RWKV_COLS), row(D_RWKV), full((LANES, D_RWKV)),
                 row(D_RWKV), full((LANES, D_RWKV)), row(D_RWKV), row(D_RWKV)]
    out_shape = [SDS((T, RWKV_COLS), f32), SDS((1, RWKV_COLS), f32), SDS((1, D_RWKV), f32), SDS((LANES, D_RWKV), f32),
                 SDS((1, D_RWKV), f32), SDS((LANES, D_RWKV), f32), SDS((1, D_RWKV), f32), SDS((1, D_RWKV), f32)]
    if has_mix:
        v0, vdn, vup, vfirst = vmix
        in_specs += [row(D_RWKV), full((D_RWKV, LANES)), full((LANES, D_RWKV)), blk]
        args += [v0, vdn, vup, vfirst]
        out_specs += [row(D_RWKV), full((D_RWKV, LANES)), full((LANES, D_RWKV)), blk]
        out_shape += [SDS((1, D_RWKV), f32), SDS((D_RWKV, LANES), f32), SDS((LANES, D_RWKV), f32),
                      SDS((T, D_RWKV), f32)]
    in_specs += [blk] * n_cot
    args += cot_list
    return pl.pallas_call(
        body, name=f"rwkv_pre_bwd_l{layer}", grid=(nb,), in_specs=in_specs, out_specs=out_specs, out_shape=out_shape,
        scratch_shapes=[pltpu.VMEM((1, RWKV_COLS), f32)],
        compiler_params=_params(("arbitrary",), 60),
    )(*args)


def _pack_cols(prod_ref, red):
    res = jnp.dot(prod_ref[...].astype(bf16), red, preferred_element_type=f32)
    out = res[0:RWKV_HEAD]
    for s in range(1, SCAN_TB):
        out = out + pltpu.roll(res[s * RWKV_HEAD:(s + 1) * RWKV_HEAD], RWKV_HEADS * s, 1)
    return out


def _unpack_cols(tile_ref, exp):
    rows = []
    for s in range(SCAN_TB):
        half = tile_ref[0, :, pl.ds((s // 4) * LANES, LANES)]
        rows.append(half if s % 4 == 0 else pltpu.roll(half, LANES - 2 * RWKV_HEADS * (s % 4), 1))
    return jnp.dot(jnp.concatenate(rows, axis=0), exp, preferred_element_type=f32)


def _to_col(v):
    T = v.shape[0]
    hi = lax.reduce_precision(v, 8, 7)
    parts = jnp.stack([hi, v - hi], axis=1).astype(bf16)
    parts = parts.reshape(T // SCAN_TB, SCAN_TB, 2, RWKV_HEADS, RWKV_HEAD)
    return jnp.transpose(parts, (0, 4, 1, 2, 3)).reshape(T // SCAN_TB, RWKV_HEAD, 2 * LANES)


def _from_col(vc):
    nb = vc.shape[0]
    tiles = vc.reshape(nb, RWKV_HEAD, SCAN_TB, RWKV_HEADS)
    return jnp.transpose(tiles, (0, 2, 3, 1)).reshape(nb * SCAN_TB, D_RWKV)


def _rwkv_scan_fwd(r, w, k, vcol, a, b, ee, e16, e16t, layer, rider, gather=()):
    T = r.shape[0]
    nb = T // SCAN_TB
    ng = len(gather)

    def body(*refs):
        r_ref, w_ref, k_ref, v_ref, a_ref, an_ref, b_ref, ee_ref, red_ref, exp_ref = refs[:10]
        rd_ins, refs = refs[10:10 + rider.n_in], refs[10 + rider.n_in:]
        g_ins, refs = refs[:ng], refs[ng:]
        o_ref, sp_ref = refs[:2]
        rd_outs, refs = refs[2:2 + rider.n_out], refs[2 + rider.n_out:]
        g_outs, refs = refs[:ng], refs[ng:]
        s_ref, sa_ref, vb_ref, po_ref = refs[:4]
        rd_scr, g_sems = refs[4:4 + rider.n_scr], refs[4 + rider.n_scr:]
        step = pl.program_id(0)
        rider.step(step, rd_ins, rd_outs, rd_scr)

        @pl.when(step == 0)
        def _():
            s_ref[...] = jnp.zeros_like(s_ref)
            sa_ref[...] = jnp.zeros_like(sa_ref)
            if ng:
                _gather_start(g_ins, g_outs, *g_sems)

        if ng:
            @pl.when(step == (3 * nb) // 4)
            def _():
                _gather_forward(g_ins, g_outs, *g_sems)

        ee_v = ee_ref[...]
        r_b, w_b, k_b, a_b, b_b = r_ref[...], w_ref[...], k_ref[...], a_ref[...], b_ref[...]
        rowi = lax.broadcasted_iota(jnp.int32, a_b.shape, 0)
        a_nx = jnp.where(rowi == SCAN_TB - 1, an_ref[0:1, :], pltpu.roll(a_b, SCAN_TB - 1, 0))
        wa = w_b * a_nx
        dots = _seg_raw(jnp.concatenate([b_b * a_nx, k_b * a_nx], axis=0), ee_v, 3)
        beta, kappa = dots[0:SCAN_TB], dots[SCAN_TB:2 * SCAN_TB]
        vb_ref[...] = _unpack_cols(v_ref, exp_ref[...])

        for t in range(SCAN_TB):
            row = lambda x: x[t:t + 1, :]
            blk = pl.ds(t * RWKV_HEAD, RWKV_HEAD)
            sp = s_ref[...]
            sa = sa_ref[...]
            sp_ref[t, 0] = sp
            sp_ref[t, 1] = sa
            vb = vb_ref[blk, :]
            ahead = _seg_raw(sp * row(wa), ee_v, SCAN_PARTS, per_tile=True)
            sn = sp * row(w_b) + sa * row(b_b) + vb * row(k_b)
            s_ref[...] = sn
            sa_ref[...] = ahead + sa * row(beta) + vb * row(kappa)
            po_ref[blk, :] = sn * row(r_b)
        o_ref[0] = _pack_cols(po_ref, red_ref[...])

        if ng:
            @pl.when(step == nb - 1)
            def _():
                _gather_finish(g_ins, g_outs, *g_sems)

    rows = pl.BlockSpec((SCAN_TB, D_RWKV), lambda i: (i, 0))
    rows_next = pl.BlockSpec((SCAN_TB, D_RWKV), lambda i: (jnp.minimum(i + 1, nb - 1), 0))
    col = pl.BlockSpec((1, RWKV_HEAD, LANES), lambda i: (i, 0, 0))
    col2 = pl.BlockSpec((1, RWKV_HEAD, 2 * LANES), lambda i: (i, 0, 0))
    full = lambda s: pl.BlockSpec(s, lambda i: (0,) * len(s))
    outs = pl.pallas_call(
        body, name=f"rwkv_scan_fwd_l{layer}", grid=(nb,),
        in_specs=[rows, rows, rows, col2, rows, rows_next, rows, full((256, 256)), full((D_RWKV, LANES)),
                  full((LANES, D_RWKV))] + rider.in_specs + [HBM_SPEC] * ng,
        out_specs=[col, pl.BlockSpec((SCAN_TB, 2, RWKV_HEAD, D_RWKV), lambda i: (i, 0, 0, 0))] + rider.out_specs
        + [HBM_SPEC] * ng,
        out_shape=[SDS((nb, RWKV_HEAD, LANES), f32), SDS((T, 2, RWKV_HEAD, D_RWKV), f32)] + rider.out_shape
        + _gather_out_shapes(gather),
        scratch_shapes=[pltpu.VMEM((RWKV_HEAD, D_RWKV), f32), pltpu.VMEM((RWKV_HEAD, D_RWKV), f32),
                        pltpu.VMEM((SCAN_TB * RWKV_HEAD, D_RWKV), f32),
                        pltpu.VMEM((SCAN_TB * RWKV_HEAD, D_RWKV), f32)] + rider.scratch
        + (_gather_sems(ng) if ng else []),
        compiler_params=_params(("arbitrary",), 52),
    )(r, w, k, vcol, a, a, b, ee, e16, e16t, *rider.args, *gather)
    return outs[0], outs[1], list(outs[2:2 + rider.n_out]), list(outs[2 + rider.n_out:])


def _rwkv_scan_bwd(r, w, k, vcol, a, b, sprev, docol, ee, e16, e16t, layer, send=()):
    T = r.shape[0]
    nb = T // SCAN_TB
    ns = len(send)

    def body(*refs):
        r_ref, w_ref, k_ref, v_ref, a_ref, b_ref, sp_ref, do_ref, ee_ref, red_ref, exp_ref = refs[:11]
        c_ins, refs = refs[11:11 + ns], refs[11 + ns:]
        dr_o, dw_o, dk_o, da_o, db_o, dv_o = refs[:6]
        c_outs, refs = refs[6:6 + ns], refs[6 + ns:]
        ds_ref, snext_ref, vb_ref, dob_ref, pdv_ref = refs[:5]
        c_sems = refs[5:]
        if ns:
            @pl.when(pl.program_id(0) == 0)
            def _():
                for cp in _chips_copies(c_ins, c_outs, *c_sems):
                    cp.start()

        ee_v = ee_ref[...]
        csum = lambda x: jnp.sum(x, axis=0, keepdims=True)
        nrow = SCAN_TB * RWKV_HEAD
        last = pl.ds(nrow - RWKV_HEAD, RWKV_HEAD)
        r_b, w_b, k_b, a_b, b_b = r_ref[...], w_ref[...], k_ref[...], a_ref[...], b_ref[...]
        vb_ref[...] = _unpack_cols(v_ref, exp_ref[...])
        dob_ref[...] = _unpack_cols(do_ref, exp_ref[...])

        @pl.when(pl.program_id(0) == 0)
        def _():
            ds_ref[...] = jnp.zeros_like(ds_ref)
            snext_ref[...] = (sp_ref[SCAN_TB - 1, 0] * w_b[SCAN_TB - 1:SCAN_TB, :]
                              + sp_ref[SCAN_TB - 1, 1] * b_b[SCAN_TB - 1:SCAN_TB, :]
                              + vb_ref[last, :] * k_b[SCAN_TB - 1:SCAN_TB, :])

        for s in range(SCAN_TB):
            t = SCAN_TB - 1 - s
            row = lambda x: x[t:t + 1, :]
            blk = pl.ds(t * RWKV_HEAD, RWKV_HEAD)
            sp, sa = sp_ref[t, 0], sp_ref[t, 1]
            sn = snext_ref[...] if s == 0 else sp_ref[t + 1, 0]
            vb, dob = vb_ref[blk, :], dob_ref[blk, :]
            ds = ds_ref[...] + dob * row(r_b)
            dsa = _seg_raw(ds * row(b_b), ee_v, SCAN_PARTS, per_tile=True)
            ds_ref[...] = ds * row(w_b) + dsa * row(a_b)
            dr_o[t:t + 1, :] = csum(sn * dob)
            pdv_ref[blk, :] = ds * row(k_b)
            dk_o[t:t + 1, :] = csum(ds * vb)
            db_o[t:t + 1, :] = csum(ds * sa)
            dw_o[t:t + 1, :] = csum(ds * sp)
            da_o[t:t + 1, :] = csum(sp * dsa)
        snext_ref[...] = sp_ref[0, 0]
        dv_o[0] = _pack_cols(pdv_ref, red_ref[...])

        if ns:
            @pl.when(pl.program_id(0) == nb - 1)
            def _():
                for cp in _chips_copies(c_ins, c_outs, *c_sems):
                    cp.wait()

    rows = pl.BlockSpec((SCAN_TB, D_RWKV), lambda i: (nb - 1 - i, 0))
    col = pl.BlockSpec((1, RWKV_HEAD, LANES), lambda i: (nb - 1 - i, 0, 0))
    col2 = pl.BlockSpec((1, RWKV_HEAD, 2 * LANES), lambda i: (nb - 1 - i, 0, 0))
    st = pl.BlockSpec((SCAN_TB, 2, RWKV_HEAD, D_RWKV), lambda i: (nb - 1 - i, 0, 0, 0))
    full = lambda s: pl.BlockSpec(s, lambda i: (0,) * len(s))
    big = pltpu.VMEM((SCAN_TB * RWKV_HEAD, D_RWKV), f32)
    outs = pl.pallas_call(
        body, name=f"rwkv_scan_bwd_l{layer}", grid=(nb,),
        in_specs=[rows, rows, rows, col2, rows, rows, st, col2, full((256, 256)), full((D_RWKV, LANES)),
                  full((LANES, D_RWKV))] + [HBM_SPEC] * ns,
        out_specs=[rows] * 5 + [col] + [HBM_SPEC] * ns,
        out_shape=[SDS((T, D_RWKV), f32)] * 5 + [SDS((nb, RWKV_HEAD, LANES), f32)] + _chips_out_shapes(send),
        scratch_shapes=[pltpu.VMEM((RWKV_HEAD, D_RWKV), f32)] * 2 + [big] * 3 + (_chips_sems(ns) if ns else []),
        compiler_params=_params(("arbitrary",), 56),
    )(r, w, k, vcol, a, b, sprev, docol, ee, e16, e16t, *send)
    return (*outs[:6], list(outs[6:]))


def _post_core(o, r, k2, v, z, gw, gb, rk, ee):
    inv = 1.0 / RWKV_HEAD
    mu = _seg(o, ee) * inv
    d = o - mu
    var = _seg(d * d, ee) * inv
    on = d * lax.rsqrt(var + GN_EPS) * gw + gb
    bonus = _seg(r * k2 * rk, ee) * v
    return (on + bonus) * jax.nn.silu(z)


def _rwkv_post_fwd(o, r, k2, v, z, gw, gb, rk, ee, cat, layer):
    T = o.shape[0]

    def body(o_ref, r_ref, k_ref, v_ref, z_ref, gw_ref, gb_ref, rk_ref, ee_ref, cat_in, out_ref):
        del cat_in
        out_ref[...] = _post_core(o_ref[...], r_ref[...], k_ref[...], v_ref[...], z_ref[...], gw_ref[...],
                                  gb_ref[...], rk_ref[...], ee_ref[...])

    blk = pl.BlockSpec((TB, D_RWKV), lambda i: (i, 0))
    row = pl.BlockSpec((1, D_RWKV), lambda i: (0, 0))
    return pl.pallas_call(
        body, name=f"rwkv_post_fwd_l{layer}", grid=(T // TB,),
        in_specs=[blk] * 5 + [row] * 3 + [pl.BlockSpec((256, 256), lambda i: (0, 0)), HBM_SPEC],
        out_specs=blk, out_shape=SDS((T, D_MODEL), f32), input_output_aliases={9: 0},
        compiler_params=_params(("parallel",), 40),
    )(o, r, k2, v, z, gw, gb, rk, ee, cat)


def _rwkv_post_bwd(o, r, k2, v, z, gw, gb, rk, ee, dcat, layer):
    T = o.shape[0]

    def body(o_ref, r_ref, k_ref, v_ref, z_ref, gw_ref, gb_ref, rk_ref, ee_ref, g_ref,
             do_o, dr_o, dk_o, dv_o, dz_o, dgw_o, dgb_o, drk_o):
        ee_v = ee_ref[...]
        fn = lambda o_, r_, k_, v_, z_, gw_, gb_, rk_: _post_core(o_, r_, k_, v_, z_, gw_, gb_, rk_, ee_v)
        _, vjp = jax.vjp(fn, o_ref[...], r_ref[...], k_ref[...], v_ref[...], z_ref[...], gw_ref[...], gb_ref[...],
                         rk_ref[...])
        gs = vjp(g_ref[...])
        do_o[...], dr_o[...], dk_o[...], dv_o[...], dz_o[...] = gs[:5]

        @pl.when(pl.program_id(0) == 0)
        def _():
            dgw_o[...] = jnp.zeros_like(dgw_o)
            dgb_o[...] = jnp.zeros_like(dgb_o)
            drk_o[...] = jnp.zeros_like(drk_o)

        dgw_o[...] += gs[5]
        dgb_o[...] += gs[6]
        drk_o[...] += gs[7]

    blk = pl.BlockSpec((TB, D_RWKV), lambda i: (i, 0))
    row = pl.BlockSpec((1, D_RWKV), lambda i: (0, 0))
    return pl.pallas_call(
        body, name=f"rwkv_post_bwd_l{layer}", grid=(T // TB,),
        in_specs=[blk] * 5 + [row] * 3 + [pl.BlockSpec((256, 256), lambda i: (0, 0)), blk],
        out_specs=[blk] * 5 + [row] * 3,
        out_shape=[SDS((T, D_RWKV), f32)] * 5 + [SDS((1, D_RWKV), f32)] * 3,
        compiler_params=_params(("arbitrary",), 48),
    )(o, r, k2, v, z, gw, gb, rk, ee, dcat)


def _hgrn_chunk(qr, fr, ii, z, S, lbl, gw, layer):
    L = CHUNK
    m = jnp.max(lbl, axis=0, keepdims=True)
    e = jnp.exp(lbl - m)
    sm = e / jnp.sum(e, axis=0, keepdims=True)
    if layer == 0:
        lb = sm[0:1] - sm[0:1]
    else:
        lb = (sm[0:1] + sm[1:2]) - sm[0:1]
    q = jax.nn.silu(qr)
    log_lb = jnp.log(jnp.maximum(lb, LB_FLOOR))
    lf = jnp.logaddexp(log_lb, jnp.log1p(-lb) + jax.nn.log_sigmoid(fr))
    k = (1.0 - lb) * jax.nn.sigmoid(-fr)
    ti = lax.broadcasted_iota(jnp.int32, (L, L), 0)
    si = lax.broadcasted_iota(jnp.int32, (L, L), 1)
    ltri = (si <= ti).astype(f32)
    b = jnp.dot(ltri, lf, precision=HIGHEST, preferred_element_type=f32)
    btot = jnp.dot(jnp.ones((L, L), f32), lf, precision=HIGHEST, preferred_element_type=f32)
    o = jnp.dot(q * jnp.exp(b), S, preferred_element_type=f32)
    nsub = L // HSUB
    t3 = lax.broadcasted_iota(jnp.int32, (HSUB, HSUB, HG_HEAD), 0)
    s3 = lax.broadcasted_iota(jnp.int32, (HSUB, HSUB, HG_HEAD), 1)
    causal = s3 <= t3
    rows = []
    for I in range(nsub):
        sl = slice(I * HSUB, (I + 1) * HSUB)
        bI, qI, kI, iI = b[sl], q[sl], k[sl], ii[sl]
        d3 = bI[:, None, :] - bI[None, :, :]
        dec = jnp.where(causal, jnp.exp(jnp.where(causal, d3, 0.0)), 0.0)
        att = jnp.sum(qI[:, None, :] * dec * kI[None, :, :], axis=-1)
        oI = jnp.dot(att, iI, preferred_element_type=f32)
        for J in range(I):
            sj = slice(J * HSUB, (J + 1) * HSUB)
            bm = b[(J + 1) * HSUB - 1:(J + 1) * HSUB]
            qs = qI * jnp.exp(bI - bm)
            ks = k[sj] * jnp.exp(bm - b[sj])
            att_ij = lax.dot_general(qs, ks, (((1,), (1,)), ((), ())), preferred_element_type=f32)
            oI = oI + jnp.dot(att_ij, ii[sj], preferred_element_type=f32)
        rows.append(oI)
    o = o + jnp.concatenate(rows, axis=0)
    k_dec = k * jnp.exp(btot - b)
    bcol = lax.dot_general(lf, jnp.ones((L, HG_HEAD), f32), (((0,), (0,)), ((), ())), precision=HIGHEST,
                           preferred_element_type=f32)
    s_new = S * jnp.exp(bcol) + lax.dot_general(k_dec, ii, (((0,), (0,)), ((), ())), preferred_element_type=f32)
    ms = jnp.mean(o * o, axis=-1, keepdims=True)
    out = o * lax.rsqrt(ms + RMS_EPS) * gw * jax.nn.silu(z)
    return out, s_new


class _Part:
    def __init__(self, args, in_specs, out_specs, out_shape, scratch, step):
        self.args, self.in_specs, self.out_specs, self.out_shape = args, in_specs, out_specs, out_shape
        self.scratch, self.step = scratch, step
        self.n_in, self.n_out, self.n_scr = len(args), len(out_shape), len(scratch)


def _run_part(part, steps, name):
    def body(*refs):
        ins, outs = refs[:part.n_in], refs[part.n_in:part.n_in + part.n_out]
        part.step(pl.program_id(0), ins, outs, refs[part.n_in + part.n_out:])

    return pl.pallas_call(body, name=name, grid=(steps,), in_specs=part.in_specs, out_specs=part.out_specs,
                          out_shape=part.out_shape, scratch_shapes=part.scratch,
                          compiler_params=_params(("arbitrary",), 48))(*part.args)


def _hg_index(T, rev):
    nc = T // CHUNK
    head = lambda i: i // nc
    chunk = (lambda i: nc - 1 - i % nc) if rev else (lambda i: i % nc)
    first = lambda i: i % nc == 0
    pj = lambda off: pl.BlockSpec((CHUNK, HG_HEAD), lambda i: (chunk(i), RW_BLKS + off + head(i)))
    per_head = lambda rows: pl.BlockSpec((rows, HG_HEAD), lambda i: (0, head(i)))
    return nc, head, chunk, first, [pj(0), pj(HG_HEADS), pj(2 * HG_HEADS), pj(3 * HG_HEADS)], per_head


def _hgrn_fwd_part(proj, lbl, gw, layer):
    T = proj.shape[0]
    nc, head, chunk, first, pspecs, per_head = _hg_index(T, False)

    def step(i, ins, outs, scr):
        q_ref, f_ref, i_ref, z_ref, lbl_ref, gw_ref = ins
        out_ref, sh_ref = outs
        s_ref, = scr

        @pl.when(first(i))
        def _():
            s_ref[...] = jnp.zeros_like(s_ref)

        s0 = s_ref[...]
        sh_ref[0, 0] = s0
        out, s_new = _hgrn_chunk(q_ref[...], f_ref[...], i_ref[...], z_ref[...], s0, lbl_ref[...], gw_ref[...], layer)
        out_ref[...] = out
        s_ref[...] = s_new

    return _Part(
        [proj, proj, proj, proj, lbl, gw], pspecs + [per_head(2), per_head(1)],
        [pl.BlockSpec((CHUNK, HG_HEAD), lambda i: (chunk(i), HG_HEADS + head(i))),
         pl.BlockSpec((1, 1, HG_HEAD, HG_HEAD), lambda i: (head(i), chunk(i), 0, 0))],
        [SDS((T, D_MODEL), f32), SDS((HG_HEADS, nc, HG_HEAD, HG_HEAD), f32)],
        [pltpu.VMEM((HG_HEAD, HG_HEAD), f32)], step)


def _hgrn_bwd_part(proj, shist, dcat, lbl, gw, layer):
    T = proj.shape[0]
    nc, head, chunk, first, pspecs, per_head = _hg_index(T, True)

    def step(i, ins, outs, scr):
        q_ref, f_ref, i_ref, z_ref, lbl_ref, gw_ref, sh_ref, g_ref = ins
        dq_o, df_o, di_o, dz_o, dlbl_o, dgw_o = outs
        ds_ref, = scr

        @pl.when(first(i))
        def _():
            ds_ref[...] = jnp.zeros_like(ds_ref)
            dlbl_o[...] = jnp.zeros_like(dlbl_o)
            dgw_o[...] = jnp.zeros_like(dgw_o)

        fn = functools.partial(_hgrn_chunk, layer=layer)
        _, vjp = jax.vjp(fn, q_ref[...], f_ref[...], i_ref[...], z_ref[...], sh_ref[0, 0], lbl_ref[...], gw_ref[...])
        dq, df, di, dz, ds, dl, dg = vjp((g_ref[...], ds_ref[...]))
        dq_o[...], df_o[...], di_o[...], dz_o[...] = dq, df, di, dz
        ds_ref[...] = ds
        dlbl_o[...] += dl
        dgw_o[...] += dg

    ospec = pl.BlockSpec((CHUNK, HG_HEAD), lambda i: (chunk(i), head(i)))
    return _Part(
        [proj, proj, proj, proj, lbl, gw, shist, dcat],
        pspecs + [per_head(2), per_head(1),
                  pl.BlockSpec((1, 1, HG_HEAD, HG_HEAD), lambda i: (head(i), chunk(i), 0, 0)),
                  pl.BlockSpec((CHUNK, HG_HEAD), lambda i: (chunk(i), HG_HEADS + head(i)))],
        [ospec] * 4 + [per_head(2), per_head(1)],
        [SDS((T, D_HGRN), f32)] * 4 + [SDS((2, D_HGRN), f32), SDS((1, D_HGRN), f32)],
        [pltpu.VMEM((HG_HEAD, HG_HEAD), f32)], step)


def _ln_core(h, y, w, b):
    u = ALPHA * h + y
    mu = jnp.mean(u, axis=-1, keepdims=True)
    d = u - mu
    var = jnp.mean(d * d, axis=-1, keepdims=True)
    return d * lax.rsqrt(var + LN_EPS) * w + b


def _ln_fwd(h, y, w, b, layer):
    T = h.shape[0]

    def body(h_ref, y_ref, w_ref, b_ref, o_ref):
        o_ref[...] = _ln_core(h_ref[...], y_ref[...], w_ref[...], b_ref[...])

    blk = pl.BlockSpec((TB, D_MODEL), lambda i: (i, 0))
    row = pl.BlockSpec((1, D_MODEL), lambda i: (0, 0))
    return pl.pallas_call(body, name=f"ln_fwd_l{layer}", grid=(T // TB,), in_specs=[blk, blk, row, row],
                          out_specs=blk, out_shape=SDS((T, D_MODEL), f32),
                          compiler_params=_params(("parallel",), 40))(h, y, w, b)


def _ln_bwd(h, y, w, b, g, layer):
    T = h.shape[0]

    def body(h_ref, y_ref, w_ref, b_ref, g_ref, dh_o, dy_o, dw_o, db_o):
        _, vjp = jax.vjp(_ln_core, h_ref[...], y_ref[...], w_ref[...], b_ref[...])
        dh, dy, dw, db = vjp(g_ref[...])
        dh_o[...] = dh
        dy_o[...] = dy.astype(bf16)

        @pl.when(pl.program_id(0) == 0)
        def _():
            dw_o[...] = jnp.zeros_like(dw_o)
            db_o[...] = jnp.zeros_like(db_o)

        dw_o[...] += dw
        db_o[...] += db

    blk = pl.BlockSpec((TB, D_MODEL), lambda i: (i, 0))
    row = pl.BlockSpec((1, D_MODEL), lambda i: (0, 0))
    return pl.pallas_call(body, name=f"ln_bwd_l{layer}", grid=(T // TB,), in_specs=[blk, blk, row, row, blk],
                          out_specs=[blk, blk, row, row],
                          out_shape=[SDS((T, D_MODEL), f32), SDS((T, D_MODEL), bf16)] + [SDS((1, D_MODEL), f32)] * 2,
                          compiler_params=_params(("arbitrary",), 48))(h, y, w, b, g)


def _loss_head(hout, target):
    T = hout.shape[0]

    def body(h_ref, t_ref, loss_o, g_o):
        err = h_ref[...] - t_ref[...]
        g_o[...] = err * (1.0 / D_MODEL)

        @pl.when(pl.program_id(0) == 0)
        def _():
            loss_o[...] = jnp.zeros_like(loss_o)

        part = jnp.sum(jnp.sum(err * err, axis=-1, keepdims=True) * (1.0 / D_MODEL), axis=0, keepdims=True)
        loss_o[...] += 0.5 * part

    blk = pl.BlockSpec((TB, D_MODEL), lambda i: (i, 0))
    return pl.pallas_call(body, name="loss_head", grid=(T // TB,), in_specs=[blk, blk],
                          out_specs=[pl.BlockSpec((1, 1), lambda i: (0, 0)), blk],
                          out_shape=[SDS((1, 1), f32), SDS((T, D_MODEL), f32)],
                          compiler_params=_params(("arbitrary",), 40))(hout, target)


def _pad_rows(m, lo, total):
    return jnp.pad(m, ((0, 0), (lo, total - lo - m.shape[1]), (0, 0)))


def _device_step(x, target, w_in0, later_shards, assemble_later, on_layer1_grads, on_layer1_recv,
                 shift_mu, w_decay0, w_decay_up, a0, a_up, k_k, k_a, r_k, ln_x_w, ln_x_b,
                 v_mix0, v_mix_down, v_mix_up, lb_logits, g_norm_w, ln_w, ln_b):
    T = x.shape[0]
    ee, e16, e16t = _const_mats()
    wup_pad = _pad_rows(w_decay_up, 0, LANES)
    aup_pad = _pad_rows(a_up, LORA, LANES)
    vdn_pad = jnp.pad(v_mix_down, ((0, 0), (0, 0), (0, LANES - VRES)))
    vup_pad = _pad_rows(v_mix_up, 0, LANES)
    row = lambda p, l: p[l][None, :]

    def vmix_of(l, vfirst):
        if l == 0:
            return None
        return (row(v_mix0, l - 1), vdn_pad[l - 1], vup_pad[l - 1], vfirst)

    h = x
    saved = []
    vfirst = None
    w_in = [w_in0, None]
    w_out = None
    for l in range(DEPTH):
        hb = h.astype(bf16)
        proj = _matmul(hb, w_in[l], mode="nt", tm=512, tn=1664, tk=D_MODEL, name=f"proj_fwd_l{l}")
        pre_args = (proj, row(shift_mu, l), row(w_decay0, l), wup_pad[l], row(a0, l), aup_pad[l], row(k_k, l),
                    row(k_a, l), ee, vmix_of(l, vfirst))
        r, w, k2, v, av, bv, z = _rwkv_pre_fwd(*pre_args, layer=l)
        if l == 0:
            vfirst = v
        vcol = _to_col(v)
        ocol, sprev, (cat, shist), gathered = _rwkv_scan_fwd(
            r, w, k2, vcol, av, bv, ee, e16, e16t, l, _hgrn_fwd_part(proj, lb_logits, row(g_norm_w, l), l),
            gather=later_shards if l == 0 else [])
        if l == 0:
            w_in[1], w_out = assemble_later(gathered)
        o = _from_col(ocol)
        post_args = (o, r, k2, v, z, row(ln_x_w, l), row(ln_x_b, l), row(r_k, l), ee)
        cat = _rwkv_post_fwd(*post_args, cat, layer=l)
        catb = cat.astype(bf16)
        y = _matmul(catb, w_out, b_layer=l, mode="nn", tm=512, tn=1024, tk=D_MODEL, name=f"out_fwd_l{l}")
        h_new = _ln_fwd(h, y, row(ln_w, l), row(ln_b, l), l)
        saved.append(dict(h=h, hb=hb, proj=proj, pre_args=pre_args, post_args=post_args,
                          scan=(r, w, k2, vcol, av, bv, sprev), cat=catb, shist=shist, y=y))
        h = h_new

    loss_sum, g = _loss_head(h, target)

    grads = {n: [None] * DEPTH for n in ("w_in", "w_out", "shift_mu", "w_decay0", "w_decay_up", "a0", "a_up", "k_k",
                                         "k_a", "r_k", "ln_x_w", "ln_x_b", "g_norm_w", "ln_w", "ln_b")}
    dlbl = []
    dvfirst = None
    to_send = []
    for l in reversed(range(DEPTH)):
        s = saved[l]
        dh_res, dy, dlnw, dlnb = _ln_bwd(s["h"], s["y"], row(ln_w, l), row(ln_b, l), g, l)
        grads["ln_w"][l], grads["ln_b"][l] = dlnw[0], dlnb[0]
        dcat = _matmul(dy, w_out, b_layer=l, mode="nt", tm=512, tn=1024, tk=D_MODEL, name=f"out_bwd_dx_l{l}")
        grads["w_out"][l] = _matmul(s["cat"], dy, mode="tn", tm=1024, tn=1024, tk=512, name=f"out_bwd_dw_l{l}")
        do, dr_p, dk_p, dv_p, dz, dgw, dgb, drk = _rwkv_post_bwd(*s["post_args"], dcat, layer=l)
        grads["ln_x_w"][l], grads["ln_x_b"][l], grads["r_k"][l] = dgw[0], dgb[0], drk[0]
        r, w, k2, vcol, av, bv, sprev = s["scan"]
        dr_s, dw_s, dk_s, da_s, db_s, dvcol, arrived = _rwkv_scan_bwd(
            r, w, k2, vcol, av, bv, sprev, _to_col(do), ee, e16, e16t, l, send=to_send if l == 0 else [])
        dq, df, di, dzh, dl, dgn = _run_part(
            _hgrn_bwd_part(s["proj"], s["shist"], dcat, lb_logits, row(g_norm_w, l), l), T // SCAN_TB,
            f"hgrn_bwd_l{l}")
        if l == 0:
            on_layer1_recv(arrived)
        v_cots = [dv_p, _from_col(dvcol)] + ([dvfirst] if (l == 0 and dvfirst is not None) else [])
        cots = dict(r=[dr_p, dr_s], w=dw_s, k=[dk_p, dk_s], v=v_cots, a=da_s, b=db_s, z=dz)
        outs = _rwkv_pre_bwd(*s["pre_args"], cots, layer=l)
        dproj, dmu, dw0, dwup, da0, daup, dkkw, dkaw = outs[:8]
        grads["shift_mu"][l], grads["w_decay0"][l], grads["a0"][l] = dmu[0], dw0[0], da0[0]
        grads["k_k"][l], grads["k_a"][l] = dkkw[0], dkaw[0]
        grads["w_decay_up"][l], grads["a_up"][l] = dwup[:LORA], daup[LORA:]
        if l > 0:
            dv0, dvdn, dvup, dvfirst = outs[8:]
            g_vmix = (dv0, dvdn[:, :VRES][None], dvup[:VRES][None])
        dlbl.append(dl)
        grads["g_norm_w"][l] = dgn[0]
        dproj = jnp.concatenate([dproj, dq, df, di, dzh], axis=1).astype(bf16)
        grads["w_in"][l] = _matmul(dproj, s["hb"], mode="tn", tm=1664, tn=1024, tk=512, name=f"proj_bwd_dw_l{l}")
        g = _matmul(dproj, w_in[l], mode="nn", tm=512, tn=1024, tk=1664, name=f"proj_bwd_dx_l{l}", add=dh_res)
        if l == 1:
            to_send = on_layer1_grads(grads["w_in"][1], grads["w_out"][1])
    w_in_grads, w_out_grads = grads.pop("w_in"), grads.pop("w_out")
    out = {n: jnp.stack(v) for n, v in grads.items()}
    out["w_in"], out["w_out"] = w_in_grads, w_out_grads
    out["lb_logits"] = dlbl[0] + dlbl[1]
    out["v_mix0"], out["v_mix_down"], out["v_mix_up"] = g_vmix
    return loss_sum, g, out


CHIP_COMBOS = ((1, 0), (0, 1), (1, 1))
HBM_SPEC = pl.BlockSpec(memory_space=pl.ANY)


def _mesh_pos():
    return lax.axis_index("x"), lax.axis_index("y"), lax.axis_index("c")


def _flip(v, d):
    return 1 - v if d else v


def _half(ref, dim, cc, hs):
    idx = [slice(None)] * len(ref.shape)
    idx[dim] = pl.ds(cc * hs, hs)
    return ref.at[tuple(idx)]


def _gather_shards(arrs):
    n = len(arrs)

    def body(*refs):
        ins, outs = refs[:n], refs[n:2 * n]
        sems = refs[2 * n:]
        _gather_start(ins, outs, *sems)
        _gather_forward(ins, outs, *sems)
        _gather_finish(ins, outs, *sems)

    return pl.pallas_call(
        body, name="gather_shards", in_specs=[HBM_SPEC] * n, out_specs=[HBM_SPEC] * n,
        out_shape=_gather_out_shapes(arrs), scratch_shapes=_gather_sems(n),
    )(*arrs)


def _gather_out_shapes(arrs):
    return [SDS((4,) + a.shape, a.dtype) for a in arrs]


def _gather_sems(n):
    return [pltpu.SemaphoreType.DMA((n, 6)), pltpu.SemaphoreType.DMA((n, 6)), pltpu.SemaphoreType.DMA((n,))]


def _gather_copies(kind, ins, outs, send_sems, recv_sems, local_sems):
    x, y, c = _mesh_pos()
    kc = 2 * x + y
    cps = []
    for a in range(len(ins)):
        if kind == "local":
            cps.append(pltpu.make_async_copy(ins[a], outs[a].at[kc], local_sems.at[a]))
            continue
        for j, (dx, dy) in enumerate(CHIP_COMBOS):
            px, py = _flip(x, dx), _flip(y, dy)
            kj = 2 * px + py
            src, dst, s, to = {
                "sends": (ins[a].at[c], outs[a].at[kc, c], j, (px, py, c)),
                "landed": (outs[a].at[kj, c], outs[a].at[kj, c], j, (px, py, c)),
                "forwards": (outs[a].at[kj, c], outs[a].at[kj, c], 3 + j, (x, y, 1 - c)),
                "passed": (outs[a].at[kj, 1 - c], outs[a].at[kj, 1 - c], 3 + j, (x, y, 1 - c)),
            }[kind]
            cps.append(pltpu.make_async_remote_copy(
                src_ref=src, dst_ref=dst, send_sem=send_sems.at[a, s], recv_sem=recv_sems.at[a, s], device_id=to,
                device_id_type=MESH))
    return cps


def _gather_start(*refs):
    for cp in _gather_copies("local", *refs) + _gather_copies("sends", *refs):
        cp.start()


def _gather_forward(*refs):
    for arrival, fw in zip(_gather_copies("landed", *refs), _gather_copies("forwards", *refs)):
        arrival.wait_recv()
        fw.start()


def _gather_finish(*refs):
    for cp in _gather_copies("passed", *refs):
        cp.wait_recv()
    for cp in _gather_copies("sends", *refs) + _gather_copies("forwards", *refs):
        cp.wait_send()
    for cp in _gather_copies("local", *refs):
        cp.wait()


DEV_COMBOS = tuple((dx, dy, dc) for dx in (0, 1) for dy in (0, 1) for dc in (0, 1))[1:]


def _allreduce_small(buf):
    R = buf.shape[0]

    def body(x_ref, o_ref, slots, send_sems, recv_sems):
        x, y, c = _mesh_pos()
        me = 4 * x + 2 * y + c
        cps = []
        for k, (dx, dy, dc) in enumerate(DEV_COMBOS):
            cp = pltpu.make_async_remote_copy(
                src_ref=x_ref, dst_ref=slots.at[k + 1], send_sem=send_sems.at[k], recv_sem=recv_sems.at[k],
                device_id=(_flip(x, dx), _flip(y, dy), _flip(c, dc)), device_id_type=MESH)
            cp.start()
            cps.append(cp)
        slots[0] = x_ref[...]
        for cp in cps:
            cp.wait_recv()
        acc = slots[jnp.bitwise_xor(me, 0)]
        for d in range(1, 8):
            acc = acc + slots[jnp.bitwise_xor(me, d)]
        o_ref[...] = acc
        for cp in cps:
            cp.wait_send()

    vm = pl.BlockSpec(memory_space=pltpu.VMEM)
    return pl.pallas_call(
        body, name="allreduce_small", in_specs=[vm], out_specs=vm, out_shape=SDS((R, LANES), f32),
        scratch_shapes=[pltpu.VMEM((8, R, LANES), f32), pltpu.SemaphoreType.DMA((7,)), pltpu.SemaphoreType.DMA((7,))],
        compiler_params=_params(None, 48),
    )(buf)


def _swap_halves(gs, tag):
    n = len(gs)

    def body(*refs):
        ins, outs = refs[:n], refs[n:2 * n]
        send_sems, recv_sems = refs[2 * n:]
        x, y, c = _mesh_pos()
        cps = []
        for a in range(n):
            h = ins[a].shape[1] // 2
            cp = pltpu.make_async_remote_copy(
                src_ref=_half(ins[a], 1, 1 - c, h), dst_ref=outs[a], send_sem=send_sems.at[a],
                recv_sem=recv_sems.at[a], device_id=(x, y, 1 - c), device_id_type=MESH)
            cp.start()
            cps.append(cp)
        for cp in cps:
            cp.wait()

    return pl.pallas_call(
        body, name=f"rs_swap_halves_{tag}", in_specs=[HBM_SPEC] * n, out_specs=[HBM_SPEC] * n,
        out_shape=[SDS((g.shape[0], g.shape[1] // 2, g.shape[2]), g.dtype) for g in gs],
        scratch_shapes=[pltpu.SemaphoreType.DMA((n,)), pltpu.SemaphoreType.DMA((n,))],
    )(*gs)


def _chips_copies(ins, outs, send_sems, recv_sems):
    x, y, c = _mesh_pos()
    cps = []
    for a in range(len(ins)):
        for j, (dx, dy) in enumerate(CHIP_COMBOS):
            px, py = _flip(x, dx), _flip(y, dy)
            cps.append(pltpu.make_async_remote_copy(
                src_ref=ins[a].at[2 * px + py], dst_ref=outs[a].at[j], send_sem=send_sems.at[a, j],
                recv_sem=recv_sems.at[a, j], device_id=(px, py, c), device_id_type=MESH))
    return cps


def _chips_out_shapes(ps):
    return [SDS((3,) + p.shape[1:], p.dtype) for p in ps]


def _chips_sems(n):
    return [pltpu.SemaphoreType.DMA((n, 3)), pltpu.SemaphoreType.DMA((n, 3))]


def _send_to_chips(ps, tag):
    n = len(ps)

    def body(*refs):
        cps = _chips_copies(refs[:n], refs[n:2 * n], *refs[2 * n:])
        for cp in cps:
            cp.start()
        for cp in cps:
            cp.wait()

    return pl.pallas_call(
        body, name=f"rs_send_to_chips_{tag}", in_specs=[HBM_SPEC] * n, out_specs=[HBM_SPEC] * n,
        out_shape=_chips_out_shapes(ps), scratch_shapes=_chips_sems(n),
    )(*ps)


def _share_halves(qs, tag):
    n = len(qs)

    def body(*refs):
        ins, outs = refs[:n], refs[n:2 * n]
        send_sems, recv_sems = refs[2 * n:]
        x, y, c = _mesh_pos()
        cps = []
        for a in range(n):
            cp = pltpu.make_async_remote_copy(
                src_ref=ins[a], dst_ref=outs[a], send_sem=send_sems.at[a], recv_sem=recv_sems.at[a],
                device_id=(x, y, 1 - c), device_id_type=MESH)
            cp.start()
            cps.append(cp)
        for cp in cps:
            cp.wait()

    return pl.pallas_call(
        body, name=f"rs_share_halves_{tag}", in_specs=[HBM_SPEC] * n, out_specs=[HBM_SPEC] * n,
        out_shape=[SDS(q.shape, q.dtype) for q in qs],
        scratch_shapes=[pltpu.SemaphoreType.DMA((n,)), pltpu.SemaphoreType.DMA((n,))],
    )(*qs)


def _row_tile(rows, cap):
    return max(t for t in range(16, cap + 1, 16) if rows % t == 0)


def _add_own_half(g, recv, c, name):
    S, R, C = g.shape
    h = R // 2
    tr = _row_tile(h, 256)
    nt = h // tr

    def body(c_ref, g_ref, r_ref, o_ref, o16_ref):
        del c_ref
        s = g_ref[...] + r_ref[...]
        o_ref[...] = s
        o16_ref[...] = s.astype(bf16)

    ospec = pl.BlockSpec((1, tr, C), lambda s, i, cr: (s, i, 0))
    return pl.pallas_call(
        body, name=name,
        grid_spec=pltpu.PrefetchScalarGridSpec(
            num_scalar_prefetch=1, grid=(S, nt),
            in_specs=[pl.BlockSpec((1, tr, C), lambda s, i, cr: (s, cr[0] * nt + i, 0)), ospec],
            out_specs=[ospec, ospec]),
        out_shape=[SDS((S, h, C), f32), SDS((S, h, C), bf16)],
        compiler_params=_params(("parallel", "parallel"), 40),
    )(c, g, recv)


def _add_chip_parts(p, recv, kc, name):
    _, R, C = p.shape
    tr = _row_tile(R, 256)

    def body(k_ref, p_ref, r0, r1, r2, o_ref):
        del k_ref
        o_ref[...] = ((p_ref[0] + r0[0].astype(f32)) + r1[0].astype(f32)) + r2[0].astype(f32)

    rspec = lambda j: pl.BlockSpec((1, tr, C), lambda i, kr: (j, i, 0))
    return pl.pallas_call(
        body, name=name,
        grid_spec=pltpu.PrefetchScalarGridSpec(
            num_scalar_prefetch=1, grid=(R // tr,),
            in_specs=[pl.BlockSpec((1, tr, C), lambda i, kr: (kr[0], i, 0)), rspec(0), rspec(1), rspec(2)],
            out_specs=pl.BlockSpec((tr, C), lambda i, kr: (i, 0))),
        out_shape=SDS((R, C), f32), compiler_params=_params(("parallel",), 40),
    )(kc, p, recv, recv, recv)


def _rs_pair(gs, c, tag):
    got = _swap_halves(gs, tag)
    pairs = [_add_own_half(g, r, c, f"rs_add_pair_{tag}_{a}") for a, (g, r) in enumerate(zip(gs, got))]
    return [p[0] for p in pairs], [p[1] for p in pairs]


def _rs_finish(ps, arrived, c, kc, tag):
    qs = [_add_chip_parts(p, r, kc, f"rs_add_chips_{tag}_{a}") for a, (p, r) in enumerate(zip(ps, arrived))]
    return qs, _share_halves(qs, tag)


def _adamw_math(w, g, m, v):
    m2 = ADAM_B1 * m + (1.0 - ADAM_B1) * g
    v2 = ADAM_B2 * v + (1.0 - ADAM_B2) * (g * g)
    m_hat = m2 / (1.0 - ADAM_B1 ** ADAM_STEP)
    v_hat = v2 / (1.0 - ADAM_B2 ** ADAM_STEP)
    return -ADAM_LR * (m_hat / (jnp.sqrt(v_hat) + ADAM_EPS) + ADAM_WD * w), m2, v2


def _adamw_shard(w, g_own, g_other, m, v, c, name):
    L, R, C = w.shape
    h = R // 2
    tr = _row_tile(h, 208)
    nh = h // tr

    def body(c_ref, w_ref, go0, gx0, go1, gx1, m_ref, v_ref, g_o, d_o, m_o, v_o):
        mine = (pl.program_id(1) // nh) == c_ref[0]
        first = pl.program_id(0) == 0
        g = jnp.where(mine, jnp.where(first, go0[...], go1[...]), jnp.where(first, gx0[...], gx1[...]))[None]
        d, m2, v2 = _adamw_math(w_ref[...], g, m_ref[...], v_ref[...])
        g_o[...], d_o[...], m_o[...], v_o[...] = g, d, m2, v2

    full = pl.BlockSpec((1, tr, C), lambda l, i, cr: (l, i, 0))
    half = pl.BlockSpec((tr, C), lambda l, i, cr: (i % nh, 0))
    return pl.pallas_call(
        body, name=name,
        grid_spec=pltpu.PrefetchScalarGridSpec(num_scalar_prefetch=1, grid=(L, R // tr),
                                               in_specs=[full, half, half, half, half, full, full],
                                               out_specs=[full] * 4),
        out_shape=[SDS((L, R, C), f32)] * 4, compiler_params=_params(("parallel", "parallel"), 48),
    )(c, w, g_own[0], g_other[0], g_own[1], g_other[1], m, v)


def _adamw(w, g, m, v, name):
    R, C = w.shape
    tr = 256 if R % 256 == 0 else R

    def body(w_ref, g_ref, m_ref, v_ref, d_o, m_o, v_o):
        d_o[...], m_o[...], v_o[...] = _adamw_math(w_ref[...], g_ref[...], m_ref[...], v_ref[...])

    blk = pl.BlockSpec((tr, C), lambda i: (i, 0))
    return pl.pallas_call(body, name=name, grid=(R // tr,), in_specs=[blk] * 4, out_specs=[blk] * 3,
                          out_shape=[SDS((R, C), f32)] * 3, compiler_params=_params(("parallel",), 40))(w, g, m, v)


def _pack(arrs):
    flat = [a.reshape(-1, LANES) for a in arrs]
    rows = sum(f.shape[0] for f in flat)
    pad = (-rows) % 16
    if pad:
        flat.append(jnp.zeros((pad, LANES), f32))
    return jnp.concatenate(flat, axis=0)


def _unpack(buf, shapes):
    out, r = [], 0
    for s in shapes:
        n = math.prod(s) // LANES
        out.append(buf[r:r + n].reshape(s))
        r += n
    return out


REPLICATED = ("shift_mu", "w_decay0", "a0", "k_k", "k_a", "r_k", "ln_x_w", "ln_x_b", "v_mix0", "lb_logits", "g_norm_w",
              "ln_w", "ln_b")
CHIP_SMALL = (("w_decay_up", 2), ("a_up", 2), ("v_mix_down", 1), ("v_mix_up", 2))
WEIGHTS = ("w_in", "shift_mu", "w_decay0", "w_decay_up", "a0", "a_up", "k_k", "k_a", "r_k", "ln_x_w", "ln_x_b",
           "v_mix0", "v_mix_down", "v_mix_up", "lb_logits", "g_norm_w", "w_out", "ln_w", "ln_b")


def kernel(x, w_in, shift_mu, w_decay0, w_decay_up, a0, a_up, k_k, k_a, r_k, ln_x_w, ln_x_b, v_mix0, v_mix_down, v_mix_up, lb_logits, g_norm_w, w_out, ln_w, ln_b, loss_target, m_w_in, m_shift_mu, m_w_decay0, m_w_decay_up, m_a0, m_a_up, m_k_k, m_k_a, m_r_k, m_ln_x_w, m_ln_x_b, m_v_mix0, m_v_mix_down, m_v_mix_up, m_lb_logits, m_g_norm_w, m_w_out, m_ln_w, m_ln_b, v_w_in, v_shift_mu, v_w_decay0, v_w_decay_up, v_a0, v_a_up, v_k_k, v_k_a, v_r_k, v_ln_x_w, v_ln_x_b, v_v_mix0, v_v_mix_down, v_v_mix_up, v_lb_logits, v_g_norm_w, v_w_out, v_ln_w, v_ln_b):
    W = dict(w_in=w_in, shift_mu=shift_mu, w_decay0=w_decay0, w_decay_up=w_decay_up, a0=a0, a_up=a_up, k_k=k_k, k_a=k_a,
             r_k=r_k, ln_x_w=ln_x_w, ln_x_b=ln_x_b, v_mix0=v_mix0, v_mix_down=v_mix_down, v_mix_up=v_mix_up,
             lb_logits=lb_logits, g_norm_w=g_norm_w, w_out=w_out, ln_w=ln_w, ln_b=ln_b)
    M = dict(w_in=m_w_in, shift_mu=m_shift_mu, w_decay0=m_w_decay0, w_decay_up=m_w_decay_up, a0=m_a0, a_up=m_a_up,
             k_k=m_k_k, k_a=m_k_a, r_k=m_r_k, ln_x_w=m_ln_x_w, ln_x_b=m_ln_x_b, v_mix0=m_v_mix0,
             v_mix_down=m_v_mix_down, v_mix_up=m_v_mix_up, lb_logits=m_lb_logits, g_norm_w=m_g_norm_w, w_out=m_w_out,
             ln_w=m_ln_w, ln_b=m_ln_b)
    V = dict(w_in=v_w_in, shift_mu=v_shift_mu, w_decay0=v_w_decay0, w_decay_up=v_w_decay_up, a0=v_a0, a_up=v_a_up,
             k_k=v_k_k, k_a=v_k_a, r_k=v_r_k, ln_x_w=v_ln_x_w, ln_x_b=v_ln_x_b, v_mix0=v_v_mix0,
             v_mix_down=v_v_mix_down, v_mix_up=v_v_mix_up, lb_logits=v_lb_logits, g_norm_w=v_g_norm_w, w_out=v_w_out,
             ln_w=v_ln_w, ln_b=v_ln_b)
    mx, my, mc = _mesh_pos()
    kc = (2 * mx + my).astype(jnp.int32)
    c_arr = jnp.reshape(mc.astype(jnp.int32), (1,))
    kc_arr = jnp.reshape(kc, (1,))

    small_shard = _pack([W[n] for n, _ in CHIP_SMALL])
    tr_in = lambda a: jnp.swapaxes(a, 1, 2)
    in_halves = tr_in(w_in).astype(bf16).reshape(DEPTH, 2, SHARD_COLS // 2, D_MODEL)
    out_halves = jnp.swapaxes(w_out.astype(bf16).reshape(DEPTH, 2, D_MODEL // 8, D_MODEL), 0, 1)
    g_in0, g_small = _gather_shards([in_halves[0], small_shard.reshape(2, -1, LANES)])
    in_full = lambda g: g.reshape(IN_COLS, D_MODEL)
    shard_shapes = [W[n].shape for n, _ in CHIP_SMALL]
    g_small = g_small.reshape(4, -1, LANES)
    per_chip = [_unpack(g_small[k], shard_shapes) for k in range(4)]
    full_small = {n: jnp.concatenate([per_chip[k][i] for k in range(4)], axis=ax)
                  for i, (n, ax) in enumerate(CHIP_SMALL)}

    def assemble_later(gathered):
        g_in1, g_out = gathered
        return in_full(g_in1), jnp.transpose(g_out, (2, 0, 1, 3, 4)).reshape(DEPTH, D_MODEL, D_MODEL)

    in_slabs = lambda g: g.reshape(4, SHARD_COLS, D_MODEL)
    out_slabs = lambda g: g.reshape(4, D_MODEL // 4, D_MODEL)
    layer1 = {}

    def on_layer1_grads(dw_in, dw_out):
        layer1["pairs"], to_send = _rs_pair([in_slabs(dw_in), out_slabs(dw_out)], c_arr, "l1")
        return to_send

    def on_layer1_recv(arrived):
        layer1["arrived"] = arrived

    loss_sum, gx, G = _device_step(
        x[0], loss_target[0], in_full(g_in0), [in_halves[1], out_halves], assemble_later, on_layer1_grads,
        on_layer1_recv, shift_mu, w_decay0, full_small["w_decay_up"], a0, full_small["a_up"], k_k, k_a, r_k, ln_x_w,
        ln_x_b, v_mix0, full_small["v_mix_down"], full_small["v_mix_up"], lb_logits, g_norm_w, ln_w, ln_b)
    loss = lax.psum(loss_sum[0, 0], ("x", "y", "c"))

    own1, other1 = _rs_finish(layer1["pairs"], layer1["arrived"], c_arr, kc_arr, "l1")
    pairs0, to_send0 = _rs_pair([in_slabs(G["w_in"][0]), out_slabs(G["w_out"][0])], c_arr, "l0")
    own0, other0 = _rs_finish(pairs0, _send_to_chips(to_send0, "l0"), c_arr, kc_arr, "l0")
    gin_own, gout_own = [own0[0], own1[0]], [own0[1], own1[1]]
    gin_other, gout_other = [other0[0], other1[0]], [other0[1], other1[1]]

    small_names = list(REPLICATED) + [n for n, _ in CHIP_SMALL]
    small_sum = _allreduce_small(_pack([G[n] for n in small_names]))
    small_full = dict(zip(small_names, _unpack(small_sum, [G[n].shape for n in small_names])))
    grads = {n: small_full[n] for n in REPLICATED}
    for n, ax in CHIP_SMALL:
        width = W[n].shape[ax]
        grads[n] = lax.dynamic_slice_in_dim(small_full[n], kc * width, width, axis=ax)

    delta, new_m, new_v = {}, {}, {}
    res = _adamw_shard(tr_in(w_in), gin_own, gin_other, tr_in(m_w_in), tr_in(v_w_in), c_arr, "adamw_w_in")
    grads["w_in"], delta["w_in"], new_m["w_in"], new_v["w_in"] = [tr_in(a) for a in res]
    grads["w_out"], delta["w_out"], new_m["w_out"], new_v["w_out"] = _adamw_shard(
        w_out, gout_own, gout_other, m_w_out, v_w_out, c_arr, "adamw_w_out")
    shapes = [W[n].shape for n in small_names]
    d_, m_, v_ = _adamw(_pack([W[n] for n in small_names]), _pack([grads[n] for n in small_names]),
                        _pack([M[n] for n in small_names]), _pack([V[n] for n in small_names]), "adamw_small")
    for n, d1, m1, v1 in zip(small_names, _unpack(d_, shapes), _unpack(m_, shapes), _unpack(v_, shapes)):
        delta[n], new_m[n], new_v[n] = d1, m1, v1

    return (loss, gx[None], *[grads[n] for n in WEIGHTS], *[delta[n] for n in WEIGHTS],
            *[new_m[n] for n in WEIGHTS], *[new_v[n] for n in WEIGHTS])
```

```python
import functools
import math

import jax
import jax.numpy as jnp
from jax import lax
from jax.experimental import pallas as pl
from jax.experimental.pallas import tpu as pltpu

f32 = jnp.float32
bf16 = jnp.bfloat16
SDS = jax.ShapeDtypeStruct
MESH = pl.DeviceIdType.MESH

D_MODEL = 2048
DEPTH = 2
D_RWKV = 1024
D_HGRN = 1024
RWKV_HEAD = 64
RWKV_HEADS = 16
LORA = 64
VRES = 32
HG_HEAD = 128
HG_HEADS = 8
CHUNK = 64
RWKV_COLS = 4 * D_RWKV + 2 * LORA
IN_COLS = RWKV_COLS + 4 * D_HGRN
SHARD_COLS = IN_COLS // 4
LANES = 128
RW_BLKS = RWKV_COLS // LANES
ALPHA = (2 * DEPTH) ** 0.25
LN_EPS = 1e-5
GN_EPS = 64e-5
RMS_EPS = 1e-5
LB_FLOOR = 1e-30
EXP_M05 = math.exp(-0.5)
ADAM_LR, ADAM_B1, ADAM_B2, ADAM_EPS, ADAM_WD, ADAM_STEP = 0.001, 0.9, 0.999, 1e-08, 0.01, 10

TB = 128
SCAN_TB = 8
SCAN_PARTS = 2
HSUB = 32
HIGHEST = lax.Precision.HIGHEST


def _params(sem=None, vmem_mb=None):
    kw = {}
    if sem is not None:
        kw["dimension_semantics"] = sem
    if vmem_mb is not None:
        kw["vmem_limit_bytes"] = vmem_mb << 20
    return pltpu.CompilerParams(**kw)


def _split(x, n):
    parts, r = [], x
    for i in range(n):
        p = r.astype(bf16)
        parts.append(p)
        if i + 1 < n:
            r = r - p.astype(f32)
    return parts


def _dot_parts(x, m, n=3):
    R = x.shape[0]
    res = jnp.dot(jnp.concatenate(_split(x, n), axis=0), m, preferred_element_type=f32)
    acc = res[0:R]
    for i in range(1, n):
        acc = acc + res[i * R:(i + 1) * R]
    return acc


def _seg_raw(x, ee, n=3, per_tile=False):
    R = x.shape[0]
    if per_tile:
        outs = []
        for c in range(4):
            tile_parts = _split(x[:, 256 * c:256 * c + 256], n)
            res = jnp.dot(jnp.concatenate(tile_parts, axis=0), ee, preferred_element_type=f32)
            acc = res[0:R]
            for i in range(1, n):
                acc = acc + res[i * R:(i + 1) * R]
            outs.append(acc)
        return jnp.concatenate(outs, axis=1)
    parts = _split(x, n)
    lhs = jnp.concatenate([p[:, 256 * c:256 * c + 256] for p in parts for c in range(4)], axis=0)
    res = jnp.dot(lhs, ee, preferred_element_type=f32)

    def tile(c):
        acc = res[c * R:(c + 1) * R]
        for i in range(1, n):
            acc = acc + res[(4 * i + c) * R:(4 * i + c + 1) * R]
        return acc

    return jnp.concatenate([tile(c) for c in range(4)], axis=1)


@jax.custom_vjp
def _seg(x, ee):
    return _seg_raw(x, ee)


def _seg_fwd(x, ee):
    return _seg_raw(x, ee), ee


def _seg_bwd(ee, g):
    return _seg_raw(g, ee), jnp.zeros_like(ee)


_seg.defvjp(_seg_fwd, _seg_bwd)


def _const_mats():
    i256 = jnp.arange(256) // RWKV_HEAD
    ee = (i256[:, None] == i256[None, :]).astype(bf16)
    head = jnp.arange(D_RWKV) // RWKV_HEAD
    lane = jnp.arange(LANES)
    red = (head[:, None] == lane[None, :]).astype(bf16)
    exp = ((lane[:, None] % RWKV_HEADS == head[None, :]) & (lane[:, None] < 2 * RWKV_HEADS)).astype(bf16)
    return ee, red, exp


def _matmul(a, b, *, mode, tm, tn, tk, name, add=None, b_layer=None, send=None):
    bs = b.shape if b_layer is None else b.shape[1:]
    lead = () if b_layer is None else (None,)
    bidx = (lambda *t: t) if b_layer is None else (lambda *t: (b_layer,) + t)
    if mode == "nn":
        (M, K), N = a.shape, bs[1]
        tm, tn, tk = min(tm, M), min(tn, N), min(tk, K)
        a_spec = pl.BlockSpec((tm, tk), lambda i, j, k: (i, k))
        b_spec = pl.BlockSpec(lead + (tk, tn), lambda i, j, k: bidx(k, j))
        dims = (((1,), (0,)), ((), ()))
    elif mode == "nt":
        (M, K), N = a.shape, bs[0]
        tm, tn, tk = min(tm, M), min(tn, N), min(tk, K)
        a_spec = pl.BlockSpec((tm, tk), lambda i, j, k: (i, k))
        b_spec = pl.BlockSpec(lead + (tn, tk), lambda i, j, k: bidx(j, k))
        dims = (((1,), (1,)), ((), ()))
    else:
        (K, M), N = a.shape, bs[1]
        tm, tn, tk = min(tm, M), min(tn, N), min(tk, K)
        a_spec = pl.BlockSpec((tk, tm), lambda i, j, k: (k, i))
        b_spec = pl.BlockSpec(lead + (tk, tn), lambda i, j, k: bidx(k, j))
        dims = (((0,), (0,)), ((), ()))
    assert M % tm == 0 and N % tn == 0 and K % tk == 0, (M, N, K, tm, tn, tk)
    has_add = add is not None
    n_in = 3 if has_add else 2
    slabs = list(send or ())
    ns = len(slabs)
    grid = (M // tm, N // tn, K // tk)

    def body(*refs):
        a_ref, b_ref = refs[:2]
        c_ref = refs[2] if has_add else None
        c_ins, o_ref = refs[n_in:n_in + ns], refs[n_in + ns]
        c_outs, c_sems = refs[n_in + ns + 1:n_in + 2 * ns + 1], refs[n_in + 2 * ns + 1:]
        i, j, k = pl.program_id(0), pl.program_id(1), pl.program_id(2)
        if ns:
            @pl.when((i == 0) & (j == 0) & (k == 0))
            def _():
                for cp in _chips_copies(c_ins, c_outs, *c_sems):
                    cp.start()

        p = lax.dot_general(a_ref[...].astype(bf16), b_ref[...].astype(bf16), dims, preferred_element_type=f32)

        @pl.when(k == 0)
        def _():
            o_ref[...] = p + c_ref[...] if has_add else p

        @pl.when(k > 0)
        def _():
            o_ref[...] += p

        if ns:
            @pl.when((i == grid[0] - 1) & (j == grid[1] - 1) & (k == grid[2] - 1))
            def _():
                for cp in _chips_copies(c_ins, c_outs, *c_sems):
                    cp.wait()

    o_spec = pl.BlockSpec((tm, tn), lambda i, j, k: (i, j))
    in_specs = [a_spec, b_spec] + ([o_spec] if has_add else []) + [HBM_SPEC] * ns
    args = (a, b) + ((add,) if has_add else ()) + tuple(slabs)
    outs = pl.pallas_call(
        body, name=name, grid=grid, in_specs=in_specs, out_specs=[o_spec] + [HBM_SPEC] * ns,
        out_shape=[SDS((M, N), f32)] + _chips_out_shapes(slabs), scratch_shapes=_chips_sems(ns) if ns else [],
        compiler_params=_params(("arbitrary",) * 3 if ns else ("parallel", "parallel", "arbitrary"), 56),
    )(*args)
    return outs[0] if send is None else (outs[0], list(outs[1:]))


def _rwkv_core(k_in, v_in, wdad, w0, wup, a0, aup, kkw, kaw, ee, vmix):
    w_raw = w0 + jnp.dot(jnp.tanh(wdad), wup, preferred_element_type=f32)
    decay = jnp.exp(-EXP_M05 * jax.nn.sigmoid(w_raw))
    a = jax.nn.sigmoid(a0 + jnp.dot(wdad, aup, preferred_element_type=f32))
    if vmix is None:
        v = v_in
    else:
        v0, vdn, vup, vfirst = vmix
        gate = jax.nn.sigmoid(v0 + jnp.dot(jnp.dot(v_in, vdn, preferred_element_type=f32), vup,
                                           preferred_element_type=f32))
        v = v_in + (vfirst - v_in) * gate
    kk = k_in * kkw
    n2 = _seg(kk * kk, ee)
    kk = kk / jnp.maximum(jnp.sqrt(n2), 1e-12)
    k2 = k_in * (1.0 + (a - 1.0) * kaw)
    return decay, k2, v, -kk, kk * a


def _shifted(y, prev_row):
    yp = pltpu.roll(y, 1, 0)
    row = lax.broadcasted_iota(jnp.int32, y.shape, 0)
    return jnp.where(row == 0, prev_row, yp)


def _rwkv_pre_fwd(proj, mu, w0, wup, a0, aup, kkw, kaw, ee, vmix, layer):
    T = proj.shape[0]
    nb = T // TB
    has_mix = vmix is not None

    def body(*refs):
        y_ref, p8_ref, mu_ref, w0_ref, wup_ref, a0_ref, aup_ref, kkw_ref, kaw_ref, ee_ref = refs[:10]
        rest = refs[10:]
        if has_mix:
            v0_ref, vdn_ref, vup_ref, vf_ref = rest[:4]
            rest = rest[4:]
        r_o, w_o, k_o, v_o, a_o, b_o, z_o = rest
        i = pl.program_id(0)
        y = y_ref[...]
        prev = jnp.where(i == 0, 0.0, p8_ref[7:8, :])
        rw = y + mu_ref[...] * (_shifted(y, prev) - y)
        mix = (v0_ref[...], vdn_ref[...], vup_ref[...], vf_ref[...]) if has_mix else None
        dec, k2, v, av, bv = _rwkv_core(rw[:, 1024:2048], rw[:, 2048:3072], rw[:, 4096:4224], w0_ref[...],
                                        wup_ref[...], a0_ref[...], aup_ref[...], kkw_ref[...], kaw_ref[...],
                                        ee_ref[...], mix)
        r_o[...] = rw[:, 0:1024]
        w_o[...] = dec
        k_o[...] = k2
        v_o[...] = v
        a_o[...] = av
        b_o[...] = bv
        z_o[...] = rw[:, 3072:4096]

    row = lambda n: pl.BlockSpec((1, n), lambda i: (0, 0))
    full = lambda s: pl.BlockSpec(s, lambda i: (0,) * len(s))
    blk = pl.BlockSpec((TB, D_RWKV), lambda i: (i, 0))
    in_specs = [pl.BlockSpec((TB, RWKV_COLS), lambda i: (i, 0)),
                pl.BlockSpec((8, RWKV_COLS), lambda i: (jnp.maximum(i * (TB // 8) - 1, 0), 0)),
                row(RWKV_COLS), row(D_RWKV), full((LANES, D_RWKV)), row(D_RWKV), full((LANES, D_RWKV)),
                row(D_RWKV), row(D_RWKV), full((256, 256))]
    args = [proj, proj, mu, w0, wup, a0, aup, kkw, kaw, ee]
    if has_mix:
        v0, vdn, vup, vfirst = vmix
        in_specs += [row(D_RWKV), full((D_RWKV, LANES)), full((LANES, D_RWKV)), blk]
        args += [v0, vdn, vup, vfirst]
    return pl.pallas_call(
        body, name=f"rwkv_pre_fwd_l{layer}", grid=(nb,), in_specs=in_specs, out_specs=[blk] * 7,
        out_shape=[SDS((T, D_RWKV), f32)] * 7, compiler_params=_params(("parallel",), 48),
    )(*args)


def _rwkv_pre_bwd(proj, mu, w0, wup, a0, aup, kkw, kaw, ee, vmix, cots, layer):
    T = proj.shape[0]
    nb = T // TB
    has_mix = vmix is not None
    cot_list = cots["r"] + [cots["w"]] + cots["k"] + cots["v"] + [cots["a"], cots["b"], cots["z"]]
    n_r, n_k, n_v = len(cots["r"]), len(cots["k"]), len(cots["v"])
    n_cot = len(cot_list)

    def body(*refs):
        y_ref, p8_ref, mu_ref, w0_ref, wup_ref, a0_ref, aup_ref, kkw_ref, kaw_ref, ee_ref = refs[:10]
        rest = refs[10:]
        if has_mix:
            v0_ref, vdn_ref, vup_ref, vf_ref = rest[:4]
            rest = rest[4:]
        cot_refs, rest = rest[:n_cot], rest[n_cot:]
        if has_mix:
            (dproj_o, dmu_o, dw0_o, dwup_o, da0_o, daup_o, dkkw_o, dkaw_o,
             dv0_o, dvdn_o, dvup_o, dvf_o, carry) = rest
        else:
            dproj_o, dmu_o, dw0_o, dwup_o, da0_o, daup_o, dkkw_o, dkaw_o, carry = rest
        i = pl.program_id(0)
        blk_i = nb - 1 - i
        y = y_ref[...]
        prev = jnp.where(blk_i == 0, 0.0, p8_ref[7:8, :])
        yp = _shifted(y, prev)
        mu_v = mu_ref[...]
        rw = y + mu_v * (yp - y)

        def sum_refs(rs):
            acc = rs[0][...]
            for r_ in rs[1:]:
                acc = acc + r_[...]
            return acc

        c = list(cot_refs)
        g_r = sum_refs(c[:n_r]); c = c[n_r:]
        g_w = c[0][...]; c = c[1:]
        g_k = sum_refs(c[:n_k]); c = c[n_k:]
        g_v = sum_refs(c[:n_v]); c = c[n_v:]
        g_a, g_b, g_z = c[0][...], c[1][...], c[2][...]

        ee_v = ee_ref[...]
        if has_mix:
            def fn(k_in, v_in, wdad, w0_, wup_, a0_, aup_, kkw_, kaw_, v0_, vdn_, vup_, vf_):
                return _rwkv_core(k_in, v_in, wdad, w0_, wup_, a0_, aup_, kkw_, kaw_, ee_v, (v0_, vdn_, vup_, vf_))
            prim = (rw[:, 1024:2048], rw[:, 2048:3072], rw[:, 4096:4224], w0_ref[...], wup_ref[...], a0_ref[...],
                    aup_ref[...], kkw_ref[...], kaw_ref[...], v0_ref[...], vdn_ref[...], vup_ref[...], vf_ref[...])
        else:
            def fn(k_in, v_in, wdad, w0_, wup_, a0_, aup_, kkw_, kaw_):
                return _rwkv_core(k_in, v_in, wdad, w0_, wup_, a0_, aup_, kkw_, kaw_, ee_v, None)
            prim = (rw[:, 1024:2048], rw[:, 2048:3072], rw[:, 4096:4224], w0_ref[...], wup_ref[...], a0_ref[...],
                    aup_ref[...], kkw_ref[...], kaw_ref[...])
        _, vjp = jax.vjp(fn, *prim)
        gs = vjp((g_w, g_k, g_v, g_a, g_b))
        d_k, d_v, d_wdad = gs[0], gs[1], gs[2]
        g_rw = jnp.concatenate([g_r, d_k, d_v, g_z, d_wdad], axis=1)

        @pl.when(i == 0)
        def _():
            carry[...] = jnp.zeros_like(carry)
            dmu_o[...] = jnp.zeros_like(dmu_o)
            dw0_o[...] = jnp.zeros_like(dw0_o)
            dwup_o[...] = jnp.zeros_like(dwup_o)
            da0_o[...] = jnp.zeros_like(da0_o)
            daup_o[...] = jnp.zeros_like(daup_o)
            dkkw_o[...] = jnp.zeros_like(dkkw_o)
            dkaw_o[...] = jnp.zeros_like(dkaw_o)
            if has_mix:
                dv0_o[...] = jnp.zeros_like(dv0_o)
                dvdn_o[...] = jnp.zeros_like(dvdn_o)
                dvup_o[...] = jnp.zeros_like(dvup_o)

        dmu_o[...] += jnp.sum(g_rw * (yp - y), axis=0, keepdims=True)
        dw0_o[...] += gs[3]
        dwup_o[...] += gs[4]
        da0_o[...] += gs[5]
        daup_o[...] += gs[6]
        dkkw_o[...] += gs[7]
        dkaw_o[...] += gs[8]
        if has_mix:
            dv0_o[...] += gs[9]
            dvdn_o[...] += gs[10]
            dvup_o[...] += gs[11]
            dvf_o[...] = gs[12]
        gm = g_rw * mu_v
        nxt = pltpu.roll(gm, TB - 1, 0)
        rowi = lax.broadcasted_iota(jnp.int32, gm.shape, 0)
        nxt = jnp.where(rowi == TB - 1, carry[...], nxt)
        dproj_o[...] = g_rw - gm + nxt
        carry[...] = gm[0:1, :]

    rev = lambda i: (nb - 1 - i, 0)
    row = lambda n: pl.BlockSpec((1, n), lambda i: (0, 0))
    full = lambda s: pl.BlockSpec(s, lambda i: (0,) * len(s))
    blk = pl.BlockSpec((TB, D_RWKV), rev)
    in_specs = [pl.BlockSpec((TB, RWKV_COLS), rev),
                pl.BlockSpec((8, RWKV_COLS), lambda i: (jnp.maximum((nb - 1 - i) * (TB // 8) - 1, 0), 0)),
                row(RWKV_COLS), row(D_RWKV), full((LANES, D_RWKV)), row(D_RWKV), full((LANES, D_RWKV)),
                row(D_RWKV), row(D_RWKV), full((256, 256))]
    args = [proj, proj, mu, w0, wup, a0, aup, kkw, kaw, ee]
    out_specs = [pl.BlockSpec((TB, RWKV_COLS), rev), row(RWKV_COLS), row(D_RWKV), full((LANES, D_RWKV)),
                 row(D_RWKV), full((LANES, D_RWKV)), row(D_RWKV), row(D_RWKV)]
    out_shape = [SDS((T, RWKV_COLS), f32), SDS((1, RWKV_COLS), f32), SDS((1, D_RWKV), f32), SDS((LANES, D_RWKV), f32),
                 SDS((1, D_RWKV), f32), SDS((LANES, D_RWKV), f32), SDS((1, D_RWKV), f32), SDS((1, D_RWKV), f32)]
    if has_mix:
        v0, vdn, vup, vfirst = vmix
        in_specs += [row(D_RWKV), full((D_RWKV, LANES)), full((LANES, D_RWKV)), blk]
        args += [v0, vdn, vup, vfirst]
        out_specs += [row(D_RWKV), full((D_RWKV, LANES)), full((LANES, D_RWKV)), blk]
        out_shape += [SDS((1, D_RWKV), f32), SDS((D_RWKV, LANES), f32), SDS((LANES, D_RWKV), f32),
                      SDS((T, D_RWKV), f32)]
    in_specs += [blk] * n_cot
    args += cot_list
    return pl.pallas_call(
        body, name=f"rwkv_pre_bwd_l{layer}", grid=(nb,), in_specs=in_specs, out_specs=out_specs, out_shape=out_shape,
        scratch_shapes=[pltpu.VMEM((1, RWKV_COLS), f32)],
        compiler_params=_params(("arbitrary",), 60),
    )(*args)


def _pack_cols(prod_ref, red):
    res = jnp.dot(prod_ref[...].astype(bf16), red, preferred_element_type=f32)
    out = res[0:RWKV_HEAD]
    for s in range(1, SCAN_TB):
        out = out + pltpu.roll(res[s * RWKV_HEAD:(s + 1) * RWKV_HEAD], RWKV_HEADS * s, 1)
    return out


def _unpack_cols(tile_ref, exp):
    rows = []
    for s in range(SCAN_TB):
        half = tile_ref[0, :, pl.ds((s // 4) * LANES, LANES)]
        rows.append(half if s % 4 == 0 else pltpu.roll(half, LANES - 2 * RWKV_HEADS * (s % 4), 1))
    return jnp.dot(jnp.concatenate(rows, axis=0), exp, preferred_element_type=f32)


def _to_col(v):
    T = v.shape[0]
    hi = lax.reduce_precision(v, 8, 7)
    parts = jnp.stack([hi, v - hi], axis=1).astype(bf16)
    parts = parts.reshape(T // SCAN_TB, SCAN_TB, 2, RWKV_HEADS, RWKV_HEAD)
    return jnp.transpose(parts, (0, 4, 1, 2, 3)).reshape(T // SCAN_TB, RWKV_HEAD, 2 * LANES)


def _from_col(vc):
    nb = vc.shape[0]
    tiles = vc.reshape(nb, RWKV_HEAD, SCAN_TB, RWKV_HEADS)
    return jnp.transpose(tiles, (0, 2, 3, 1)).reshape(nb * SCAN_TB, D_RWKV)


def _rwkv_scan_fwd(r, w, k, vcol, a, b, ee, e16, e16t, layer, rider, gather=()):
    T = r.shape[0]
    nb = T // SCAN_TB
    ng = len(gather)

    def body(*refs):
        r_ref, w_ref, k_ref, v_ref, a_ref, an_ref, b_ref, ee_ref, red_ref, exp_ref = refs[:10]
        rd_ins, refs = refs[10:10 + rider.n_in], refs[10 + rider.n_in:]
        g_ins, refs = refs[:ng], refs[ng:]
        o_ref, sp_ref = refs[:2]
        rd_outs, refs = refs[2:2 + rider.n_out], refs[2 + rider.n_out:]
        g_outs, refs = refs[:ng], refs[ng:]
        s_ref, sa_ref, vb_ref, po_ref = refs[:4]
        rd_scr, g_sems = refs[4:4 + rider.n_scr], refs[4 + rider.n_scr:]
        step = pl.program_id(0)
        rider.step(step, rd_ins, rd_outs, rd_scr)

        @pl.when(step == 0)
        def _():
            s_ref[...] = jnp.zeros_like(s_ref)
            sa_ref[...] = jnp.zeros_like(sa_ref)
            if ng:
                _gather_start(g_ins, g_outs, *g_sems)

        if ng:
            @pl.when(step == (3 * nb) // 4)
            def _():
                _gather_forward(g_ins, g_outs, *g_sems)

        ee_v = ee_ref[...]
        r_b, w_b, k_b, a_b, b_b = r_ref[...], w_ref[...], k_ref[...], a_ref[...], b_ref[...]
        rowi = lax.broadcasted_iota(jnp.int32, a_b.shape, 0)
        a_nx = jnp.where(rowi == SCAN_TB - 1, an_ref[0:1, :], pltpu.roll(a_b, SCAN_TB - 1, 0))
        wa = w_b * a_nx
        dots = _seg_raw(jnp.concatenate([b_b * a_nx, k_b * a_nx], axis=0), ee_v, 3)
        beta, kappa = dots[0:SCAN_TB], dots[SCAN_TB:2 * SCAN_TB]
        vb_ref[...] = _unpack_cols(v_ref, exp_ref[...])

        for t in range(SCAN_TB):
            row = lambda x: x[t:t + 1, :]
            blk = pl.ds(t * RWKV_HEAD, RWKV_HEAD)
            sp = s_ref[...]
            sa = sa_ref[...]
            sp_ref[t, 0] = sp
            sp_ref[t, 1] = sa
            vb = vb_ref[blk, :]
            ahead = _seg_raw(sp * row(wa), ee_v, SCAN_PARTS, per_tile=True)
            sn = sp * row(w_b) + sa * row(b_b) + vb * row(k_b)
            s_ref[...] = sn
            sa_ref[...] = ahead + sa * row(beta) + vb * row(kappa)
            po_ref[blk, :] = sn * row(r_b)
        o_ref[0] = _pack_cols(po_ref, red_ref[...])

        if ng:
            @pl.when(step == nb - 1)
            def _():
                _gather_finish(g_ins, g_outs, *g_sems)

    rows = pl.BlockSpec((SCAN_TB, D_RWKV), lambda i: (i, 0))
    rows_next = pl.BlockSpec((SCAN_TB, D_RWKV), lambda i: (jnp.minimum(i + 1, nb - 1), 0))
    col = pl.BlockSpec((1, RWKV_HEAD, LANES), lambda i: (i, 0, 0))
    col2 = pl.BlockSpec((1, RWKV_HEAD, 2 * LANES), lambda i: (i, 0, 0))
    full = lambda s: pl.BlockSpec(s, lambda i: (0,) * len(s))
    outs = pl.pallas_call(
        body, name=f"rwkv_scan_fwd_l{layer}", grid=(nb,),
        in_specs=[rows, rows, rows, col2, rows, rows_next, rows, full((256, 256)), full((D_RWKV, LANES)),
                  full((LANES, D_RWKV))] + rider.in_specs + [HBM_SPEC] * ng,
        out_specs=[col, pl.BlockSpec((SCAN_TB, 2, RWKV_HEAD, D_RWKV), lambda i: (i, 0, 0, 0))] + rider.out_specs
        + [HBM_SPEC] * ng,
        out_shape=[SDS((nb, RWKV_HEAD, LANES), f32), SDS((T, 2, RWKV_HEAD, D_RWKV), f32)] + rider.out_shape
        + _gather_out_shapes(gather),
        scratch_shapes=[pltpu.VMEM((RWKV_HEAD, D_RWKV), f32), pltpu.VMEM((RWKV_HEAD, D_RWKV), f32),
                        pltpu.VMEM((SCAN_TB * RWKV_HEAD, D_RWKV), f32),
                        pltpu.VMEM((SCAN_TB * RWKV_HEAD, D_RWKV), f32)] + rider.scratch
        + (_gather_sems(ng) if ng else []),
        compiler_params=_params(("arbitrary",), 52),
    )(r, w, k, vcol, a, a, b, ee, e16, e16t, *rider.args, *gather)
    return outs[0], outs[1], list(outs[2:2 + rider.n_out]), list(outs[2 + rider.n_out:])


def _rwkv_scan_bwd(r, w, k, vcol, a, b, sprev, docol, ee, e16, e16t, layer, send=()):
    T = r.shape[0]
    nb = T // SCAN_TB
    ns = len(send)

    def body(*refs):
        r_ref, w_ref, k_ref, v_ref, a_ref, b_ref, sp_ref, do_ref, ee_ref, red_ref, exp_ref = refs[:11]
        c_ins, refs = refs[11:11 + ns], refs[11 + ns:]
        dr_o, dw_o, dk_o, da_o, db_o, dv_o = refs[:6]
        c_outs, refs = refs[6:6 + ns], refs[6 + ns:]
        ds_ref, snext_ref, vb_ref, dob_ref, pdv_ref = refs[:5]
        c_sems = refs[5:]
        if ns:
            @pl.when(pl.program_id(0) == 0)
            def _():
                for cp in _chips_copies(c_ins, c_outs, *c_sems):
                    cp.start()

        ee_v = ee_ref[...]
        csum = lambda x: jnp.sum(x, axis=0, keepdims=True)
        nrow = SCAN_TB * RWKV_HEAD
        last = pl.ds(nrow - RWKV_HEAD, RWKV_HEAD)
        r_b, w_b, k_b, a_b, b_b = r_ref[...], w_ref[...], k_ref[...], a_ref[...], b_ref[...]
        vb_ref[...] = _unpack_cols(v_ref, exp_ref[...])
        dob_ref[...] = _unpack_cols(do_ref, exp_ref[...])

        @pl.when(pl.program_id(0) == 0)
        def _():
            ds_ref[...] = jnp.zeros_like(ds_ref)
            snext_ref[...] = (sp_ref[SCAN_TB - 1, 0] * w_b[SCAN_TB - 1:SCAN_TB, :]
                              + sp_ref[SCAN_TB - 1, 1] * b_b[SCAN_TB - 1:SCAN_TB, :]
                              + vb_ref[last, :] * k_b[SCAN_TB - 1:SCAN_TB, :])

        for s in range(SCAN_TB):
            t = SCAN_TB - 1 - s
            row = lambda x: x[t:t + 1, :]
            blk = pl.ds(t * RWKV_HEAD, RWKV_HEAD)
            sp, sa = sp_ref[t, 0], sp_ref[t, 1]
            sn = snext_ref[...] if s == 0 else sp_ref[t + 1, 0]
            vb, dob = vb_ref[blk, :], dob_ref[blk, :]
            ds = ds_ref[...] + dob * row(r_b)
            dsa = _seg_raw(ds * row(b_b), ee_v, SCAN_PARTS, per_tile=True)
            ds_ref[...] = ds * row(w_b) + dsa * row(a_b)
            dr_o[t:t + 1, :] = csum(sn * dob)
            pdv_ref[blk, :] = ds * row(k_b)
            dk_o[t:t + 1, :] = csum(ds * vb)
            db_o[t:t + 1, :] = csum(ds * sa)
            dw_o[t:t + 1, :] = csum(ds * sp)
            da_o[t:t + 1, :] = csum(sp * dsa)
        snext_ref[...] = sp_ref[0, 0]
        dv_o[0] = _pack_cols(pdv_ref, red_ref[...])

        if ns:
            @pl.when(pl.program_id(0) == nb - 1)
            def _():
                for cp in _chips_copies(c_ins, c_outs, *c_sems):
                    cp.wait()

    rows = pl.BlockSpec((SCAN_TB, D_RWKV), lambda i: (nb - 1 - i, 0))
    col = pl.BlockSpec((1, RWKV_HEAD, LANES), lambda i: (nb - 1 - i, 0, 0))
    col2 = pl.BlockSpec((1, RWKV_HEAD, 2 * LANES), lambda i: (nb - 1 - i, 0, 0))
    st = pl.BlockSpec((SCAN_TB, 2, RWKV_HEAD, D_RWKV), lambda i: (nb - 1 - i, 0, 0, 0))
    full = lambda s: pl.BlockSpec(s, lambda i: (0,) * len(s))
    big = pltpu.VMEM((SCAN_TB * RWKV_HEAD, D_RWKV), f32)
    outs = pl.pallas_call(
        body, name=f"rwkv_scan_bwd_l{layer}", grid=(nb,),
        in_specs=[rows, rows, rows, col2, rows, rows, st, col2, full((256, 256)), full((D_RWKV, LANES)),
                  full((LANES, D_RWKV))] + [HBM_SPEC] * ns,
        out_specs=[rows] * 5 + [col] + [HBM_SPEC] * ns,
        out_shape=[SDS((T, D_RWKV), f32)] * 5 + [SDS((nb, RWKV_HEAD, LANES), f32)] + _chips_out_shapes(send),
        scratch_shapes=[pltpu.VMEM((RWKV_HEAD, D_RWKV), f32)] * 2 + [big] * 3 + (_chips_sems(ns) if ns else []),
        compiler_params=_params(("arbitrary",), 56),
    )(r, w, k, vcol, a, b, sprev, docol, ee, e16, e16t, *send)
    return (*outs[:6], list(outs[6:]))


def _post_core(o, r, k2, v, z, gw, gb, rk, ee):
    inv = 1.0 / RWKV_HEAD
    mu = _seg(o, ee) * inv
    d = o - mu
    var = _seg(d * d, ee) * inv
    on = d * lax.rsqrt(var + GN_EPS) * gw + gb
    bonus = _seg(r * k2 * rk, ee) * v
    return (on + bonus) * jax.nn.silu(z)


def _rwkv_post_fwd(o, r, k2, v, z, gw, gb, rk, ee, cat, layer):
    T = o.shape[0]

    def body(o_ref, r_ref, k_ref, v_ref, z_ref, gw_ref, gb_ref, rk_ref, ee_ref, cat_in, out_ref):
        del cat_in
        out_ref[...] = _post_core(o_ref[...], r_ref[...], k_ref[...], v_ref[...], z_ref[...], gw_ref[...],
                                  gb_ref[...], rk_ref[...], ee_ref[...])

    blk = pl.BlockSpec((TB, D_RWKV), lambda i: (i, 0))
    row = pl.BlockSpec((1, D_RWKV), lambda i: (0, 0))
    return pl.pallas_call(
        body, name=f"rwkv_post_fwd_l{layer}", grid=(T // TB,),
        in_specs=[blk] * 5 + [row] * 3 + [pl.BlockSpec((256, 256), lambda i: (0, 0)), HBM_SPEC],
        out_specs=blk, out_shape=SDS((T, D_MODEL), f32), input_output_aliases={9: 0},
        compiler_params=_params(("parallel",), 40),
    )(o, r, k2, v, z, gw, gb, rk, ee, cat)


def _rwkv_post_bwd(o, r, k2, v, z, gw, gb, rk, ee, dcat, layer):
    T = o.shape[0]

    def body(o_ref, r_ref, k_ref, v_ref, z_ref, gw_ref, gb_ref, rk_ref, ee_ref, g_ref,
             do_o, dr_o, dk_o, dv_o, dz_o, dgw_o, dgb_o, drk_o):
        ee_v = ee_ref[...]
        fn = lambda o_, r_, k_, v_, z_, gw_, gb_, rk_: _post_core(o_, r_, k_, v_, z_, gw_, gb_, rk_, ee_v)
        _, vjp = jax.vjp(fn, o_ref[...], r_ref[...], k_ref[...], v_ref[...], z_ref[...], gw_ref[...], gb_ref[...],
                         rk_ref[...])
        gs = vjp(g_ref[...])
        do_o[...], dr_o[...], dk_o[...], dv_o[...], dz_o[...] = gs[:5]

        @pl.when(pl.program_id(0) == 0)
        def _():
            dgw_o[...] = jnp.zeros_like(dgw_o)
            dgb_o[...] = jnp.zeros_like(dgb_o)
            drk_o[...] = jnp.zeros_like(drk_o)

        dgw_o[...] += gs[5]
        dgb_o[...] += gs[6]
        drk_o[...] += gs[7]

    blk = pl.BlockSpec((TB, D_RWKV), lambda i: (i, 0))
    row = pl.BlockSpec((1, D_RWKV), lambda i: (0, 0))
    return pl.pallas_call(
        body, name=f"rwkv_post_bwd_l{layer}", grid=(T // TB,),
        in_specs=[blk] * 5 + [row] * 3 + [pl.BlockSpec((256, 256), lambda i: (0, 0)), blk],
        out_specs=[blk] * 5 + [row] * 3,
        out_shape=[SDS((T, D_RWKV), f32)] * 5 + [SDS((1, D_RWKV), f32)] * 3,
        compiler_params=_params(("arbitrary",), 48),
    )(o, r, k2, v, z, gw, gb, rk, ee, dcat)


def _hgrn_chunk(qr, fr, ii, z, S, lbl, gw, layer):
    L = CHUNK
    G = qr.shape[1] // HG_HEAD
    hd = lambda x, g: x[..., g * HG_HEAD:(g + 1) * HG_HEAD]
    heads = lambda f: jnp.concatenate([f(g) for g in range(G)], axis=-1)
    m = jnp.max(lbl, axis=0, keepdims=True)
    e = jnp.exp(lbl - m)
    sm = e / jnp.sum(e, axis=0, keepdims=True)
    if layer == 0:
        lb = sm[0:1] - sm[0:1]
    else:
        lb = (sm[0:1] + sm[1:2]) - sm[0:1]
    q = jax.nn.silu(qr)
    log_lb = jnp.log(jnp.maximum(lb, LB_FLOOR))
    lf = jnp.logaddexp(log_lb, jnp.log1p(-lb) + jax.nn.log_sigmoid(fr))
    k = (1.0 - lb) * jax.nn.sigmoid(-fr)
    ti = lax.broadcasted_iota(jnp.int32, (L, L), 0)
    si = lax.broadcasted_iota(jnp.int32, (L, L), 1)
    ltri = (si <= ti).astype(f32)
    b = jnp.dot(ltri, lf, precision=HIGHEST, preferred_element_type=f32)
    btot = jnp.dot(jnp.ones((L, L), f32), lf, precision=HIGHEST, preferred_element_type=f32)
    qd = q * jnp.exp(b)
    o = heads(lambda g: jnp.dot(hd(qd, g), S[g], preferred_element_type=f32))
    nsub = L // HSUB
    t3 = lax.broadcasted_iota(jnp.int32, (HSUB, HSUB, G * HG_HEAD), 0)
    s3 = lax.broadcasted_iota(jnp.int32, (HSUB, HSUB, G * HG_HEAD), 1)
    causal = s3 <= t3
    nt = (((1,), (1,)), ((), ()))
    tn = (((0,), (0,)), ((), ()))
    rows = []
    for I in range(nsub):
        sl = slice(I * HSUB, (I + 1) * HSUB)
        bI, qI, kI, iI = b[sl], q[sl], k[sl], ii[sl]
        d3 = bI[:, None, :] - bI[None, :, :]
        dec = jnp.where(causal, jnp.exp(jnp.where(causal, d3, 0.0)), 0.0)
        prod = qI[:, None, :] * dec * kI[None, :, :]
        oI = heads(lambda g: jnp.dot(jnp.sum(hd(prod, g), axis=-1), hd(iI, g), preferred_element_type=f32))
        for J in range(I):
            sj = slice(J * HSUB, (J + 1) * HSUB)
            bm = b[(J + 1) * HSUB - 1:(J + 1) * HSUB]
            qs = qI * jnp.exp(bI - bm)
            ks = k[sj] * jnp.exp(bm - b[sj])
            iJ = ii[sj]
            oI = oI + heads(lambda g: jnp.dot(
                lax.dot_general(hd(qs, g), hd(ks, g), nt, preferred_element_type=f32), hd(iJ, g),
                preferred_element_type=f32))
        rows.append(oI)
    o = o + jnp.concatenate(rows, axis=0)
    k_dec = k * jnp.exp(btot - b)
    ones = jnp.ones((L, HG_HEAD), f32)
    s_new = jnp.stack([
        S[g] * jnp.exp(lax.dot_general(hd(lf, g), ones, tn, precision=HIGHEST, preferred_element_type=f32))
        + lax.dot_general(hd(k_dec, g), hd(ii, g), tn, preferred_element_type=f32) for g in range(G)])
    ms = heads(lambda g: jnp.broadcast_to(jnp.mean(hd(o * o, g), axis=-1, keepdims=True), (L, HG_HEAD)))
    out = o * lax.rsqrt(ms + RMS_EPS) * gw * jax.nn.silu(z)
    return out, s_new


class _Part:
    def __init__(self, args, in_specs, out_specs, out_shape, scratch, step):
        self.args, self.in_specs, self.out_specs, self.out_shape = args, in_specs, out_specs, out_shape
        self.scratch, self.step = scratch, step
        self.n_in, self.n_out, self.n_scr = len(args), len(out_shape), len(scratch)


def _run_part(part, steps, name):
    def body(*refs):
        ins, outs = refs[:part.n_in], refs[part.n_in:part.n_in + part.n_out]
        part.step(pl.program_id(0), ins, outs, refs[part.n_in + part.n_out:])

    return pl.pallas_call(body, name=name, grid=(steps,), in_specs=part.in_specs, out_specs=part.out_specs,
                          out_shape=part.out_shape, scratch_shapes=part.scratch,
                          compiler_params=_params(("arbitrary",), 48))(*part.args)


def _hg_index(T, rev, group=1):
    nc = T // CHUNK
    width = group * HG_HEAD
    head = lambda i: i // nc
    chunk = (lambda i: nc - 1 - i % nc) if rev else (lambda i: i % nc)
    first = lambda i: i % nc == 0
    pj = lambda off, g: pl.BlockSpec((CHUNK, HG_HEAD),
                                     lambda i: (chunk(i), RW_BLKS + off + group * head(i) + g))
    pspecs = [pj(kind * HG_HEADS, g) for kind in range(4) for g in range(group)]
    per_head = lambda rows: pl.BlockSpec((rows, width), lambda i: (0, head(i)))
    return nc, head, chunk, first, pspecs, per_head


def _hgrn_fwd_part(proj, lbl, gw, layer):
    T = proj.shape[0]
    nc, head, chunk, first, pspecs, per_head = _hg_index(T, False)

    def step(i, ins, outs, scr):
        q_ref, f_ref, i_ref, z_ref, lbl_ref, gw_ref = ins
        out_ref, sh_ref = outs
        s_ref, = scr

        @pl.when(first(i))
        def _():
            s_ref[...] = jnp.zeros_like(s_ref)

        s0 = s_ref[...]
        sh_ref[:, 0] = s0
        out, s_new = _hgrn_chunk(q_ref[...], f_ref[...], i_ref[...], z_ref[...], s0, lbl_ref[...], gw_ref[...], layer)
        out_ref[...] = out
        s_ref[...] = s_new

    return _Part(
        [proj, proj, proj, proj, lbl, gw], pspecs + [per_head(2), per_head(1)],
        [pl.BlockSpec((CHUNK, HG_HEAD), lambda i: (chunk(i), HG_HEADS + head(i))),
         pl.BlockSpec((1, 1, HG_HEAD, HG_HEAD), lambda i: (head(i), chunk(i), 0, 0))],
        [SDS((T, D_MODEL), f32), SDS((HG_HEADS, nc, HG_HEAD, HG_HEAD), f32)],
        [pltpu.VMEM((1, HG_HEAD, HG_HEAD), f32)], step)


HG_GROUP = 2


def _hgrn_bwd_part(proj, shist, dcat, lbl, gw, layer):
    T = proj.shape[0]
    G = HG_GROUP
    nc, head, chunk, first, pspecs, per_head = _hg_index(T, True, G)

    def step(i, ins, outs, scr):
        p_refs, (lbl_ref, gw_ref, sh_ref, g_ref) = ins[:4 * G], ins[4 * G:]
        dq_o, df_o, di_o, dz_o, dlbl_o, dgw_o = outs
        ds_ref, = scr

        @pl.when(first(i))
        def _():
            ds_ref[...] = jnp.zeros_like(ds_ref)
            dlbl_o[...] = jnp.zeros_like(dlbl_o)
            dgw_o[...] = jnp.zeros_like(dgw_o)

        fn = functools.partial(_hgrn_chunk, layer=layer)
        q, f, ii, z = (jnp.concatenate([p_refs[kind * G + g][...] for g in range(G)], axis=1) for kind in range(4))
        _, vjp = jax.vjp(fn, q, f, ii, z, sh_ref[:, 0], lbl_ref[...], gw_ref[...])
        dq, df, di, dz, ds, dl, dg = vjp((g_ref[...], ds_ref[...]))
        dq_o[...], df_o[...], di_o[...], dz_o[...] = dq, df, di, dz
        ds_ref[...] = ds
        dlbl_o[...] += dl
        dgw_o[...] += dg

    ospec = pl.BlockSpec((CHUNK, G * HG_HEAD), lambda i: (chunk(i), head(i)))
    return _Part(
        [proj] * (4 * G) + [lbl, gw, shist, dcat],
        pspecs + [per_head(2), per_head(1),
                  pl.BlockSpec((G, 1, HG_HEAD, HG_HEAD), lambda i: (head(i), chunk(i), 0, 0)),
                  pl.BlockSpec((CHUNK, G * HG_HEAD), lambda i: (chunk(i), HG_HEADS // G + head(i)))],
        [ospec] * 4 + [per_head(2), per_head(1)],
        [SDS((T, D_HGRN), f32)] * 4 + [SDS((2, D_HGRN), f32), SDS((1, D_HGRN), f32)],
        [pltpu.VMEM((G, HG_HEAD, HG_HEAD), f32)], step)


def _ln_core(h, y, w, b):
    u = ALPHA * h + y
    mu = jnp.mean(u, axis=-1, keepdims=True)
    d = u - mu
    var = jnp.mean(d * d, axis=-1, keepdims=True)
    return d * lax.rsqrt(var + LN_EPS) * w + b


def _ln_fwd(h, y, w, b, layer):
    T = h.shape[0]

    def body(h_ref, y_ref, w_ref, b_ref, o_ref):
        o_ref[...] = _ln_core(h_ref[...], y_ref[...], w_ref[...], b_ref[...])

    blk = pl.BlockSpec((TB, D_MODEL), lambda i: (i, 0))
    row = pl.BlockSpec((1, D_MODEL), lambda i: (0, 0))
    return pl.pallas_call(body, name=f"ln_fwd_l{layer}", grid=(T // TB,), in_specs=[blk, blk, row, row],
                          out_specs=blk, out_shape=SDS((T, D_MODEL), f32),
                          compiler_params=_params(("parallel",), 40))(h, y, w, b)


def _ln_bwd(h, y, w, b, g, layer):
    T = h.shape[0]

    def body(h_ref, y_ref, w_ref, b_ref, g_ref, dh_o, dy_o, dw_o, db_o):
        _, vjp = jax.vjp(_ln_core, h_ref[...], y_ref[...], w_ref[...], b_ref[...])
        dh, dy, dw, db = vjp(g_ref[...])
        dh_o[...] = dh
        dy_o[...] = dy.astype(bf16)

        @pl.when(pl.program_id(0) == 0)
        def _():
            dw_o[...] = jnp.zeros_like(dw_o)
            db_o[...] = jnp.zeros_like(db_o)

        dw_o[...] += dw
        db_o[...] += db

    blk = pl.BlockSpec((TB, D_MODEL), lambda i: (i, 0))
    row = pl.BlockSpec((1, D_MODEL), lambda i: (0, 0))
    return pl.pallas_call(body, name=f"ln_bwd_l{layer}", grid=(T // TB,), in_specs=[blk, blk, row, row, blk],
                          out_specs=[blk, blk, row, row],
                          out_shape=[SDS((T, D_MODEL), f32), SDS((T, D_MODEL), bf16)] + [SDS((1, D_MODEL), f32)] * 2,
                          compiler_params=_params(("arbitrary",), 48))(h, y, w, b, g)


def _loss_head(hout, target):
    T = hout.shape[0]

    def body(h_ref, t_ref, loss_o, g_o):
        err = h_ref[...] - t_ref[...]
        g_o[...] = err * (1.0 / D_MODEL)

        @pl.when(pl.program_id(0) == 0)
        def _():
            loss_o[...] = jnp.zeros_like(loss_o)

        part = jnp.sum(jnp.sum(err * err, axis=-1, keepdims=True) * (1.0 / D_MODEL), axis=0, keepdims=True)
        loss_o[...] += 0.5 * part

    blk = pl.BlockSpec((TB, D_MODEL), lambda i: (i, 0))
    return pl.pallas_call(body, name="loss_head", grid=(T // TB,), in_specs=[blk, blk],
                          out_specs=[pl.BlockSpec((1, 1), lambda i: (0, 0)), blk],
                          out_shape=[SDS((1, 1), f32), SDS((T, D_MODEL), f32)],
                          compiler_params=_params(("arbitrary",), 40))(hout, target)


def _pad_rows(m, lo, total):
    return jnp.pad(m, ((0, 0), (lo, total - lo - m.shape[1]), (0, 0)))


def _device_step(x, target, w_in0, later_shards, assemble_later, on_grads, on_recv,
                 shift_mu, w_decay0, w_decay_up, a0, a_up, k_k, k_a, r_k, ln_x_w, ln_x_b,
                 v_mix0, v_mix_down, v_mix_up, lb_logits, g_norm_w, ln_w, ln_b):
    T = x.shape[0]
    ee, e16, e16t = _const_mats()
    wup_pad = _pad_rows(w_decay_up, 0, LANES)
    aup_pad = _pad_rows(a_up, LORA, LANES)
    vdn_pad = jnp.pad(v_mix_down, ((0, 0), (0, 0), (0, LANES - VRES)))
    vup_pad = _pad_rows(v_mix_up, 0, LANES)
    row = lambda p, l: p[l][None, :]

    def vmix_of(l, vfirst):
        if l == 0:
            return None
        return (row(v_mix0, l - 1), vdn_pad[l - 1], vup_pad[l - 1], vfirst)

    h = x
    saved = []
    vfirst = None
    w_in = [w_in0, None]
    w_out = None
    for l in range(DEPTH):
        hb = h.astype(bf16)
        proj = _matmul(hb, w_in[l], mode="nt", tm=512, tn=1664, tk=D_MODEL, name=f"proj_fwd_l{l}")
        pre_args = (proj, row(shift_mu, l), row(w_decay0, l), wup_pad[l], row(a0, l), aup_pad[l], row(k_k, l),
                    row(k_a, l), ee, vmix_of(l, vfirst))
        r, w, k2, v, av, bv, z = _rwkv_pre_fwd(*pre_args, layer=l)
        if l == 0:
            vfirst = v
        vcol = _to_col(v)
        ocol, sprev, (cat, shist), gathered = _rwkv_scan_fwd(
            r, w, k2, vcol, av, bv, ee, e16, e16t, l, _hgrn_fwd_part(proj, lb_logits, row(g_norm_w, l), l),
            gather=later_shards if l == 0 else [])
        if l == 0:
            w_in[1], w_out = assemble_later(gathered)
        o = _from_col(ocol)
        post_args = (o, r, k2, v, z, row(ln_x_w, l), row(ln_x_b, l), row(r_k, l), ee)
        cat = _rwkv_post_fwd(*post_args, cat, layer=l)
        catb = cat.astype(bf16)
        y = _matmul(catb, w_out, b_layer=l, mode="nn", tm=512, tn=1024, tk=D_MODEL, name=f"out_fwd_l{l}")
        h_new = _ln_fwd(h, y, row(ln_w, l), row(ln_b, l), l)
        saved.append(dict(h=h, hb=hb, proj=proj, pre_args=pre_args, post_args=post_args,
                          scan=(r, w, k2, vcol, av, bv, sprev), cat=catb, shist=shist, y=y))
        h = h_new

    loss_sum, g = _loss_head(h, target)

    grads = {n: [None] * DEPTH for n in ("w_in", "w_out", "shift_mu", "w_decay0", "w_decay_up", "a0", "a_up", "k_k",
                                         "k_a", "r_k", "ln_x_w", "ln_x_b", "g_norm_w", "ln_w", "ln_b")}
    dlbl = []
    dvfirst = None
    to_send = []
    for l in reversed(range(DEPTH)):
        s = saved[l]
        dh_res, dy, dlnw, dlnb = _ln_bwd(s["h"], s["y"], row(ln_w, l), row(ln_b, l), g, l)
        grads["ln_w"][l], grads["ln_b"][l] = dlnw[0], dlnb[0]
        dcat = _matmul(dy, w_out, b_layer=l, mode="nt", tm=512, tn=1024, tk=D_MODEL, name=f"out_bwd_dx_l{l}")
        grads["w_out"][l] = _matmul(s["cat"], dy, mode="tn", tm=1024, tn=1024, tk=512, name=f"out_bwd_dw_l{l}")
        do, dr_p, dk_p, dv_p, dz, dgw, dgb, drk = _rwkv_post_bwd(*s["post_args"], dcat, layer=l)
        grads["ln_x_w"][l], grads["ln_x_b"][l], grads["r_k"][l] = dgw[0], dgb[0], drk[0]
        r, w, k2, vcol, av, bv, sprev = s["scan"]
        dr_s, dw_s, dk_s, da_s, db_s, dvcol, arrived = _rwkv_scan_bwd(
            r, w, k2, vcol, av, bv, sprev, _to_col(do), ee, e16, e16t, l, send=to_send if l == 0 else [])
        dq, df, di, dzh, dl, dgn = _run_part(
            _hgrn_bwd_part(s["proj"], s["shist"], dcat, lb_logits, row(g_norm_w, l), l),
            (T // CHUNK) * HG_HEADS // HG_GROUP, f"hgrn_bwd_l{l}")
        if l == 0:
            on_recv(1, arrived)
        v_cots = [dv_p, _from_col(dvcol)] + ([dvfirst] if (l == 0 and dvfirst is not None) else [])
        cots = dict(r=[dr_p, dr_s], w=dw_s, k=[dk_p, dk_s], v=v_cots, a=da_s, b=db_s, z=dz)
        outs = _rwkv_pre_bwd(*s["pre_args"], cots, layer=l)
        dproj, dmu, dw0, dwup, da0, daup, dkkw, dkaw = outs[:8]
        grads["shift_mu"][l], grads["w_decay0"][l], grads["a0"][l] = dmu[0], dw0[0], da0[0]
        grads["k_k"][l], grads["k_a"][l] = dkkw[0], dkaw[0]
        grads["w_decay_up"][l], grads["a_up"][l] = dwup[:LORA], daup[LORA:]
        if l > 0:
            dv0, dvdn, dvup, dvfirst = outs[8:]
            g_vmix = (dv0, dvdn[:, :VRES][None], dvup[:VRES][None])
        dlbl.append(dl)
        grads["g_norm_w"][l] = dgn[0]
        dproj = jnp.concatenate([dproj, dq, df, di, dzh], axis=1).astype(bf16)
        grads["w_in"][l] = _matmul(dproj, s["hb"], mode="tn", tm=1664, tn=1024, tk=512, name=f"proj_bwd_dw_l{l}")
        to_send = on_grads(l, grads["w_in"][l], grads["w_out"][l])
        if l == 1:
            g = _matmul(dproj, w_in[l], mode="nn", tm=512, tn=1024, tk=1664, name=f"proj_bwd_dx_l{l}", add=dh_res)
        else:
            g, arrived = _matmul(dproj, w_in[l], mode="nn", tm=512, tn=1024, tk=1664, name=f"proj_bwd_dx_l{l}",
                                 add=dh_res, send=to_send)
            on_recv(0, arrived)
    w_in_grads, w_out_grads = grads.pop("w_in"), grads.pop("w_out")
    out = {n: jnp.stack(v) for n, v in grads.items()}
    out["w_in"], out["w_out"] = w_in_grads, w_out_grads
    out["lb_logits"] = dlbl[0] + dlbl[1]
    out["v_mix0"], out["v_mix_down"], out["v_mix_up"] = g_vmix
    return loss_sum, g, out


CHIP_COMBOS = ((1, 0), (0, 1), (1, 1))
HBM_SPEC = pl.BlockSpec(memory_space=pl.ANY)


def _mesh_pos():
    return lax.axis_index("x"), lax.axis_index("y"), lax.axis_index("c")


def _flip(v, d):
    return 1 - v if d else v


def _half(ref, dim, cc, hs):
    idx = [slice(None)] * len(ref.shape)
    idx[dim] = pl.ds(cc * hs, hs)
    return ref.at[tuple(idx)]


def _gather_shards(arrs):
    n = len(arrs)

    def body(*refs):
        ins, outs = refs[:n], refs[n:2 * n]
        sems = refs[2 * n:]
        _gather_start(ins, outs, *sems)
        _gather_forward(ins, outs, *sems)
        _gather_finish(ins, outs, *sems)

    return pl.pallas_call(
        body, name="gather_shards", in_specs=[HBM_SPEC] * n, out_specs=[HBM_SPEC] * n,
        out_shape=_gather_out_shapes(arrs), scratch_shapes=_gather_sems(n),
    )(*arrs)


def _gather_out_shapes(arrs):
    return [SDS((4,) + a.shape, a.dtype) for a in arrs]


def _gather_sems(n):
    return [pltpu.SemaphoreType.DMA((n, 6)), pltpu.SemaphoreType.DMA((n, 6)), pltpu.SemaphoreType.DMA((n,))]


def _gather_copies(kind, ins, outs, send_sems, recv_sems, local_sems):
    x, y, c = _mesh_pos()
    kc = 2 * x + y
    cps = []
    for a in range(len(ins)):
        if kind == "local":
            cps.append(pltpu.make_async_copy(ins[a], outs[a].at[kc], local_sems.at[a]))
            continue
        for j, (dx, dy) in enumerate(CHIP_COMBOS):
            px, py = _flip(x, dx), _flip(y, dy)
            kj = 2 * px + py
            src, dst, s, to = {
                "sends": (ins[a].at[c], outs[a].at[kc, c], j, (px, py, c)),
                "landed": (outs[a].at[kj, c], outs[a].at[kj, c], j, (px, py, c)),
                "forwards": (outs[a].at[kj, c], outs[a].at[kj, c], 3 + j, (x, y, 1 - c)),
                "passed": (outs[a].at[kj, 1 - c], outs[a].at[kj, 1 - c], 3 + j, (x, y, 1 - c)),
            }[kind]
            cps.append(pltpu.make_async_remote_copy(
                src_ref=src, dst_ref=dst, send_sem=send_sems.at[a, s], recv_sem=recv_sems.at[a, s], device_id=to,
                device_id_type=MESH))
    return cps


def _gather_start(*refs):
    for cp in _gather_copies("local", *refs) + _gather_copies("sends", *refs):
        cp.start()


def _gather_forward(*refs):
    for arrival, fw in zip(_gather_copies("landed", *refs), _gather_copies("forwards", *refs)):
        arrival.wait_recv()
        fw.start()


def _gather_finish(*refs):
    for cp in _gather_copies("passed", *refs):
        cp.wait_recv()
    for cp in _gather_copies("sends", *refs) + _gather_copies("forwards", *refs):
        cp.wait_send()
    for cp in _gather_copies("local", *refs):
        cp.wait()


DEV_COMBOS = tuple((dx, dy, dc) for dx in (0, 1) for dy in (0, 1) for dc in (0, 1))[1:]


def _allreduce_small(buf):
    R = buf.shape[0]

    def body(x_ref, o_ref, slots, send_sems, recv_sems):
        x, y, c = _mesh_pos()
        me = 4 * x + 2 * y + c
        cps = []
        for k, (dx, dy, dc) in enumerate(DEV_COMBOS):
            cp = pltpu.make_async_remote_copy(
                src_ref=x_ref, dst_ref=slots.at[k + 1], send_sem=send_sems.at[k], recv_sem=recv_sems.at[k],
                device_id=(_flip(x, dx), _flip(y, dy), _flip(c, dc)), device_id_type=MESH)
            cp.start()
            cps.append(cp)
        slots[0] = x_ref[...]
        for cp in cps:
            cp.wait_recv()
        acc = slots[jnp.bitwise_xor(me, 0)]
        for d in range(1, 8):
            acc = acc + slots[jnp.bitwise_xor(me, d)]
        o_ref[...] = acc
        for cp in cps:
            cp.wait_send()

    vm = pl.BlockSpec(memory_space=pltpu.VMEM)
    return pl.pallas_call(
        body, name="allreduce_small", in_specs=[vm], out_specs=vm, out_shape=SDS((R, LANES), f32),
        scratch_shapes=[pltpu.VMEM((8, R, LANES), f32), pltpu.SemaphoreType.DMA((7,)), pltpu.SemaphoreType.DMA((7,))],
        compiler_params=_params(None, 48),
    )(buf)


def _swap_halves(gs, tag):
    n = len(gs)

    def body(*refs):
        ins, outs = refs[:n], refs[n:2 * n]
        send_sems, recv_sems = refs[2 * n:]
        x, y, c = _mesh_pos()
        cps = []
        for a in range(n):
            h = ins[a].shape[1] // 2
            cp = pltpu.make_async_remote_copy(
                src_ref=_half(ins[a], 1, 1 - c, h), dst_ref=outs[a], send_sem=send_sems.at[a],
                recv_sem=recv_sems.at[a], device_id=(x, y, 1 - c), device_id_type=MESH)
            cp.start()
            cps.append(cp)
        for cp in cps:
            cp.wait()

    return pl.pallas_call(
        body, name=f"rs_swap_halves_{tag}", in_specs=[HBM_SPEC] * n, out_specs=[HBM_SPEC] * n,
        out_shape=[SDS((g.shape[0], g.shape[1] // 2, g.shape[2]), g.dtype) for g in gs],
        scratch_shapes=[pltpu.SemaphoreType.DMA((n,)), pltpu.SemaphoreType.DMA((n,))],
    )(*gs)


def _chips_copies(ins, outs, send_sems, recv_sems):
    x, y, c = _mesh_pos()
    cps = []
    for a in range(len(ins)):
        for j, (dx, dy) in enumerate(CHIP_COMBOS):
            px, py = _flip(x, dx), _flip(y, dy)
            cps.append(pltpu.make_async_remote_copy(
                src_ref=ins[a].at[2 * px + py], dst_ref=outs[a].at[j], send_sem=send_sems.at[a, j],
                recv_sem=recv_sems.at[a, j], device_id=(px, py, c), device_id_type=MESH))
    return cps


def _chips_out_shapes(ps):
    return [SDS((3,) + p.shape[1:], p.dtype) for p in ps]


def _chips_sems(n):
    return [pltpu.SemaphoreType.DMA((n, 3)), pltpu.SemaphoreType.DMA((n, 3))]


def _share_halves(qs, tag):
    n = len(qs)

    def body(*refs):
        ins, outs = refs[:n], refs[n:2 * n]
        send_sems, recv_sems = refs[2 * n:]
        x, y, c = _mesh_pos()
        cps = []
        for a in range(n):
            cp = pltpu.make_async_remote_copy(
                src_ref=ins[a], dst_ref=outs[a], send_sem=send_sems.at[a], recv_sem=recv_sems.at[a],
                device_id=(x, y, 1 - c), device_id_type=MESH)
            cp.start()
            cps.append(cp)
        for cp in cps:
            cp.wait()

    return pl.pallas_call(
        body, name=f"rs_share_halves_{tag}", in_specs=[HBM_SPEC] * n, out_specs=[HBM_SPEC] * n,
        out_shape=[SDS(q.shape, q.dtype) for q in qs],
        scratch_shapes=[pltpu.SemaphoreType.DMA((n,)), pltpu.SemaphoreType.DMA((n,))],
    )(*qs)


def _row_tile(rows, cap):
    return max(t for t in range(16, cap + 1, 16) if rows % t == 0)


def _add_own_half(g, recv, c, name):
    S, R, C = g.shape
    h = R // 2
    tr = _row_tile(h, 256)
    nt = h // tr

    def body(c_ref, g_ref, r_ref, o_ref, o16_ref):
        del c_ref
        s = g_ref[...] + r_ref[...]
        o_ref[...] = s
        o16_ref[...] = s.astype(bf16)

    ospec = pl.BlockSpec((1, tr, C), lambda s, i, cr: (s, i, 0))
    return pl.pallas_call(
        body, name=name,
        grid_spec=pltpu.PrefetchScalarGridSpec(
            num_scalar_prefetch=1, grid=(S, nt),
            in_specs=[pl.BlockSpec((1, tr, C), lambda s, i, cr: (s, cr[0] * nt + i, 0)), ospec],
            out_specs=[ospec, ospec]),
        out_shape=[SDS((S, h, C), f32), SDS((S, h, C), bf16)],
        compiler_params=_params(("parallel", "parallel"), 40),
    )(c, g, recv)


def _add_chip_parts(p, recv, kc, name):
    _, R, C = p.shape
    tr = _row_tile(R, 256)

    def body(k_ref, p_ref, r0, r1, r2, o_ref):
        del k_ref
        o_ref[...] = ((p_ref[0] + r0[0].astype(f32)) + r1[0].astype(f32)) + r2[0].astype(f32)

    rspec = lambda j: pl.BlockSpec((1, tr, C), lambda i, kr: (j, i, 0))
    return pl.pallas_call(
        body, name=name,
        grid_spec=pltpu.PrefetchScalarGridSpec(
            num_scalar_prefetch=1, grid=(R // tr,),
            in_specs=[pl.BlockSpec((1, tr, C), lambda i, kr: (kr[0], i, 0)), rspec(0), rspec(1), rspec(2)],
            out_specs=pl.BlockSpec((tr, C), lambda i, kr: (i, 0))),
        out_shape=SDS((R, C), f32), compiler_params=_params(("parallel",), 40),
    )(kc, p, recv, recv, recv)


def _rs_pair(gs, c, tag):
    got = _swap_halves(gs, tag)
    pairs = [_add_own_half(g, r, c, f"rs_add_pair_{tag}_{a}") for a, (g, r) in enumerate(zip(gs, got))]
    return [p[0] for p in pairs], [p[1] for p in pairs]


def _rs_finish(ps, arrived, c, kc, tag):
    qs = [_add_chip_parts(p, r, kc, f"rs_add_chips_{tag}_{a}") for a, (p, r) in enumerate(zip(ps, arrived))]
    return qs, _share_halves(qs, tag)


def _adamw_math(w, g, m, v):
    m2 = ADAM_B1 * m + (1.0 - ADAM_B1) * g
    v2 = ADAM_B2 * v + (1.0 - ADAM_B2) * (g * g)
    m_hat = m2 / (1.0 - ADAM_B1 ** ADAM_STEP)
    v_hat = v2 / (1.0 - ADAM_B2 ** ADAM_STEP)
    return -ADAM_LR * (m_hat / (jnp.sqrt(v_hat) + ADAM_EPS) + ADAM_WD * w), m2, v2


def _adamw_shard(w, g_own, g_other, m, v, c, name):
    L, R, C = w.shape
    h = R // 2
    tr = _row_tile(h, 208)
    nh = h // tr

    def body(c_ref, w_ref, go0, gx0, go1, gx1, m_ref, v_ref, g_o, d_o, m_o, v_o):
        mine = (pl.program_id(1) // nh) == c_ref[0]
        first = pl.program_id(0) == 0
        g = jnp.where(mine, jnp.where(first, go0[...], go1[...]), jnp.where(first, gx0[...], gx1[...]))[None]
        d, m2, v2 = _adamw_math(w_ref[...], g, m_ref[...], v_ref[...])
        g_o[...], d_o[...], m_o[...], v_o[...] = g, d, m2, v2

    full = pl.BlockSpec((1, tr, C), lambda l, i, cr: (l, i, 0))
    half = pl.BlockSpec((tr, C), lambda l, i, cr: (i % nh, 0))
    return pl.pallas_call(
        body, name=name,
        grid_spec=pltpu.PrefetchScalarGridSpec(num_scalar_prefetch=1, grid=(L, R // tr),
                                               in_specs=[full, half, half, half, half, full, full],
                                               out_specs=[full] * 4),
        out_shape=[SDS((L, R, C), f32)] * 4, compiler_params=_params(("parallel", "parallel"), 48),
    )(c, w, g_own[0], g_other[0], g_own[1], g_other[1], m, v)


def _adamw(w, g, m, v, name):
    R, C = w.shape
    tr = 256 if R % 256 == 0 else R

    def body(w_ref, g_ref, m_ref, v_ref, d_o, m_o, v_o):
        d_o[...], m_o[...], v_o[...] = _adamw_math(w_ref[...], g_ref[...], m_ref[...], v_ref[...])

    blk = pl.BlockSpec((tr, C), lambda i: (i, 0))
    return pl.pallas_call(body, name=name, grid=(R // tr,), in_specs=[blk] * 4, out_specs=[blk] * 3,
                          out_shape=[SDS((R, C), f32)] * 3, compiler_params=_params(("parallel",), 40))(w, g, m, v)


def _pack(arrs):
    flat = [a.reshape(-1, LANES) for a in arrs]
    rows = sum(f.shape[0] for f in flat)
    pad = (-rows) % 16
    if pad:
        flat.append(jnp.zeros((pad, LANES), f32))
    return jnp.concatenate(flat, axis=0)


def _unpack(buf, shapes):
    out, r = [], 0
    for s in shapes:
        n = math.prod(s) // LANES
        out.append(buf[r:r + n].reshape(s))
        r += n
    return out


REPLICATED = ("shift_mu", "w_decay0", "a0", "k_k", "k_a", "r_k", "ln_x_w", "ln_x_b", "v_mix0", "lb_logits", "g_norm_w",
              "ln_w", "ln_b")
CHIP_SMALL = (("w_decay_up", 2), ("a_up", 2), ("v_mix_down", 1), ("v_mix_up", 2))
WEIGHTS = ("w_in", "shift_mu", "w_decay0", "w_decay_up", "a0", "a_up", "k_k", "k_a", "r_k", "ln_x_w", "ln_x_b",
           "v_mix0", "v_mix_down", "v_mix_up", "lb_logits", "g_norm_w", "w_out", "ln_w", "ln_b")


def kernel(x, w_in, shift_mu, w_decay0, w_decay_up, a0, a_up, k_k, k_a, r_k, ln_x_w, ln_x_b, v_mix0, v_mix_down, v_mix_up, lb_logits, g_norm_w, w_out, ln_w, ln_b, loss_target, m_w_in, m_shift_mu, m_w_decay0, m_w_decay_up, m_a0, m_a_up, m_k_k, m_k_a, m_r_k, m_ln_x_w, m_ln_x_b, m_v_mix0, m_v_mix_down, m_v_mix_up, m_lb_logits, m_g_norm_w, m_w_out, m_ln_w, m_ln_b, v_w_in, v_shift_mu, v_w_decay0, v_w_decay_up, v_a0, v_a_up, v_k_k, v_k_a, v_r_k, v_ln_x_w, v_ln_x_b, v_v_mix0, v_v_mix_down, v_v_mix_up, v_lb_logits, v_g_norm_w, v_w_out, v_ln_w, v_ln_b):
    W = dict(w_in=w_in, shift_mu=shift_mu, w_decay0=w_decay0, w_decay_up=w_decay_up, a0=a0, a_up=a_up, k_k=k_k, k_a=k_a,
             r_k=r_k, ln_x_w=ln_x_w, ln_x_b=ln_x_b, v_mix0=v_mix0, v_mix_down=v_mix_down, v_mix_up=v_mix_up,
             lb_logits=lb_logits, g_norm_w=g_norm_w, w_out=w_out, ln_w=ln_w, ln_b=ln_b)
    M = dict(w_in=m_w_in, shift_mu=m_shift_mu, w_decay0=m_w_decay0, w_decay_up=m_w_decay_up, a0=m_a0, a_up=m_a_up,
             k_k=m_k_k, k_a=m_k_a, r_k=m_r_k, ln_x_w=m_ln_x_w, ln_x_b=m_ln_x_b, v_mix0=m_v_mix0,
             v_mix_down=m_v_mix_down, v_mix_up=m_v_mix_up, lb_logits=m_lb_logits, g_norm_w=m_g_norm_w, w_out=m_w_out,
             ln_w=m_ln_w, ln_b=m_ln_b)
    V = dict(w_in=v_w_in, shift_mu=v_shift_mu, w_decay0=v_w_decay0, w_decay_up=v_w_decay_up, a0=v_a0, a_up=v_a_up,
             k_k=v_k_k, k_a=v_k_a, r_k=v_r_k, ln_x_w=v_ln_x_w, ln_x_b=v_ln_x_b, v_mix0=v_v_mix0,
             v_mix_down=v_v_mix_down, v_mix_up=v_v_mix_up, lb_logits=v_lb_logits, g_norm_w=v_g_norm_w, w_out=v_w_out,
             ln_w=v_ln_w, ln_b=v_ln_b)
    mx, my, mc = _mesh_pos()
    kc = (2 * mx + my).astype(jnp.int32)
    c_arr = jnp.reshape(mc.astype(jnp.int32), (1,))
    kc_arr = jnp.reshape(kc, (1,))

    small_shard = _pack([W[n] for n, _ in CHIP_SMALL])
    tr_in = lambda a: jnp.swapaxes(a, 1, 2)
    in_halves = tr_in(w_in).astype(bf16).reshape(DEPTH, 2, SHARD_COLS // 2, D_MODEL)
    out_halves = jnp.swapaxes(w_out.astype(bf16).reshape(DEPTH, 2, D_MODEL // 8, D_MODEL), 0, 1)
    g_in0, g_small = _gather_shards([in_halves[0], small_shard.reshape(2, -1, LANES)])
    in_full = lambda g: g.reshape(IN_COLS, D_MODEL)
    shard_shapes = [W[n].shape for n, _ in CHIP_SMALL]
    g_small = g_small.reshape(4, -1, LANES)
    per_chip = [_unpack(g_small[k], shard_shapes) for k in range(4)]
    full_small = {n: jnp.concatenate([per_chip[k][i] for k in range(4)], axis=ax)
                  for i, (n, ax) in enumerate(CHIP_SMALL)}

    def assemble_later(gathered):
        g_in1, g_out = gathered
        return in_full(g_in1), jnp.transpose(g_out, (2, 0, 1, 3, 4)).reshape(DEPTH, D_MODEL, D_MODEL)

    in_slabs = lambda g: g.reshape(4, SHARD_COLS, D_MODEL)
    out_slabs = lambda g: g.reshape(4, D_MODEL // 4, D_MODEL)
    pairs, arrivals = {}, {}

    def on_grads(l, dw_in, dw_out):
        pairs[l], to_send = _rs_pair([in_slabs(dw_in), out_slabs(dw_out)], c_arr, f"l{l}")
        return to_send

    def on_recv(l, arrived):
        arrivals[l] = arrived

    loss_sum, gx, G = _device_step(
        x[0], loss_target[0], in_full(g_in0), [in_halves[1], out_halves], assemble_later, on_grads, on_recv,
        shift_mu, w_decay0, full_small["w_decay_up"], a0, full_small["a_up"], k_k, k_a, r_k, ln_x_w,
        ln_x_b, v_mix0, full_small["v_mix_down"], full_small["v_mix_up"], lb_logits, g_norm_w, ln_w, ln_b)
    loss = lax.psum(loss_sum[0, 0], ("x", "y", "c"))

    own1, other1 = _rs_finish(pairs[1], arrivals[1], c_arr, kc_arr, "l1")
    own0, other0 = _rs_finish(pairs[0], arrivals[0], c_arr, kc_arr, "l0")
    gin_own, gout_own = [own0[0], own1[0]], [own0[1], own1[1]]
    gin_other, gout_other = [other0[0], other1[0]], [other0[1], other1[1]]

    small_names = list(REPLICATED) + [n for n, _ in CHIP_SMALL]
    small_sum = _allreduce_small(_pack([G[n] for n in small_names]))
    small_full = dict(zip(small_names, _unpack(small_sum, [G[n].shape for n in small_names])))
    grads = {n: small_full[n] for n in REPLICATED}
    for n, ax in CHIP_SMALL:
        width = W[n].shape[ax]
        grads[n] = lax.dynamic_slice_in_dim(small_full[n], kc * width, width, axis=ax)

    delta, new_m, new_v = {}, {}, {}
    res = _adamw_shard(tr_in(w_in), gin_own, gin_other, tr_in(m_w_in), tr_in(v_w_in), c_arr, "adamw_w_in")
    grads["w_in"], delta["w_in"], new_m["w_in"], new_v["w_in"] = [tr_in(a) for a in res]
    grads["w_out"], delta["w_out"], new_m["w_out"], new_v["w_out"] = _adamw_shard(
        w_out, gout_own, gout_other, m_w_out, v_w_out, c_arr, "adamw_w_out")
    shapes = [W[n].shape for n in small_names]
    d_, m_, v_ = _adamw(_pack([W[n] for n in small_names]), _pack([grads[n] for n in small_names]),
                        _pack([M[n] for n in small_names]), _pack([V[n] for n in small_names]), "adamw_small")
    for n, d1, m1, v1 in zip(small_names, _unpack(d_, shapes), _unpack(m_, shapes), _unpack(v_, shapes)):
        delta[n], new_m[n], new_v[n] = d1, m1, v1

    return (loss, gx[None], *[grads[n] for n in WEIGHTS], *[delta[n] for n in WEIGHTS],
            *[new_m[n] for n in WEIGHTS], *[new_v[n] for n in WEIGHTS])
```

```python
import functools
import math

import jax
import jax.numpy as jnp
from jax import lax
from jax.experimental import pallas as pl
from jax.experimental.pallas import tpu as pltpu

f32 = jnp.float32
bf16 = jnp.bfloat16
SDS = jax.ShapeDtypeStruct
MESH = pl.DeviceIdType.MESH

D_MODEL = 2048
DEPTH = 2
D_RWKV = 1024
D_HGRN = 1024
RWKV_HEAD = 64
RWKV_HEADS = 16
LORA = 64
VRES = 32
HG_HEAD = 128
HG_HEADS = 8
CHUNK = 64
RWKV_COLS = 4 * D_RWKV + 2 * LORA
IN_COLS = RWKV_COLS + 4 * D_HGRN
SHARD_COLS = IN_COLS // 4
LANES = 128
RW_BLKS = RWKV_COLS // LANES
ALPHA = (2 * DEPTH) ** 0.25
LN_EPS = 1e-5
GN_EPS = 64e-5
RMS_EPS = 1e-5
LB_FLOOR = 1e-30
EXP_M05 = math.exp(-0.5)
ADAM_LR, ADAM_B1, ADAM_B2, ADAM_EPS, ADAM_WD, ADAM_STEP = 0.001, 0.9, 0.999, 1e-08, 0.01, 10

TB = 128
SCAN_TB = 8
SCAN_PARTS = 2
HSUB = 32
HIGHEST = lax.Precision.HIGHEST


def _params(sem=None, vmem_mb=None):
    kw = {}
    if sem is not None:
        kw["dimension_semantics"] = sem
    if vmem_mb is not None:
        kw["vmem_limit_bytes"] = vmem_mb << 20
    return pltpu.CompilerParams(**kw)


def _split(x, n):
    parts, r = [], x
    for i in range(n):
        p = r.astype(bf16)
        parts.append(p)
        if i + 1 < n:
            r = r - p.astype(f32)
    return parts


def _dot_parts(x, m, n=3):
    R = x.shape[0]
    res = jnp.dot(jnp.concatenate(_split(x, n), axis=0), m, preferred_element_type=f32)
    acc = res[0:R]
    for i in range(1, n):
        acc = acc + res[i * R:(i + 1) * R]
    return acc


def _seg_raw(x, ee, n=3, per_tile=False):
    R = x.shape[0]
    if per_tile:
        outs = []
        for c in range(4):
            tile_parts = _split(x[:, 256 * c:256 * c + 256], n)
            res = jnp.dot(jnp.concatenate(tile_parts, axis=0), ee, preferred_element_type=f32)
            acc = res[0:R]
            for i in range(1, n):
                acc = acc + res[i * R:(i + 1) * R]
            outs.append(acc)
        return jnp.concatenate(outs, axis=1)
    parts = _split(x, n)
    lhs = jnp.concatenate([p[:, 256 * c:256 * c + 256] for p in parts for c in range(4)], axis=0)
    res = jnp.dot(lhs, ee, preferred_element_type=f32)

    def tile(c):
        acc = res[c * R:(c + 1) * R]
        for i in range(1, n):
            acc = acc + res[(4 * i + c) * R:(4 * i + c + 1) * R]
        return acc

    return jnp.concatenate([tile(c) for c in range(4)], axis=1)


@jax.custom_vjp
def _seg(x, ee):
    return _seg_raw(x, ee)


def _seg_fwd(x, ee):
    return _seg_raw(x, ee), ee


def _seg_bwd(ee, g):
    return _seg_raw(g, ee), jnp.zeros_like(ee)


_seg.defvjp(_seg_fwd, _seg_bwd)


def _const_mats():
    i256 = jnp.arange(256) // RWKV_HEAD
    ee = (i256[:, None] == i256[None, :]).astype(bf16)
    head = jnp.arange(D_RWKV) // RWKV_HEAD
    lane = jnp.arange(LANES)
    red = (head[:, None] == lane[None, :]).astype(bf16)
    exp = ((lane[:, None] % RWKV_HEADS == head[None, :]) & (lane[:, None] < 2 * RWKV_HEADS)).astype(bf16)
    return ee, red, exp


def _matmul(a, b, *, mode, tm, tn, tk, name, add=None, b_layer=None, rider=None):
    bs = b.shape if b_layer is None else b.shape[1:]
    lead = () if b_layer is None else (None,)
    bidx = (lambda *t: t) if b_layer is None else (lambda *t: (b_layer,) + t)
    if mode == "nn":
        (M, K), N = a.shape, bs[1]
        tm, tn, tk = min(tm, M), min(tn, N), min(tk, K)
        a_spec = pl.BlockSpec((tm, tk), lambda i, j, k: (i, k))
        b_spec = pl.BlockSpec(lead + (tk, tn), lambda i, j, k: bidx(k, j))
        dims = (((1,), (0,)), ((), ()))
    elif mode == "nt":
        (M, K), N = a.shape, bs[0]
        tm, tn, tk = min(tm, M), min(tn, N), min(tk, K)
        a_spec = pl.BlockSpec((tm, tk), lambda i, j, k: (i, k))
        b_spec = pl.BlockSpec(lead + (tn, tk), lambda i, j, k: bidx(j, k))
        dims = (((1,), (1,)), ((), ()))
    else:
        (K, M), N = a.shape, bs[1]
        tm, tn, tk = min(tm, M), min(tn, N), min(tk, K)
        a_spec = pl.BlockSpec((tk, tm), lambda i, j, k: (k, i))
        b_spec = pl.BlockSpec(lead + (tk, tn), lambda i, j, k: bidx(k, j))
        dims = (((0,), (0,)), ((), ()))
    assert M % tm == 0 and N % tn == 0 and K % tk == 0, (M, N, K, tm, tn, tk)
    has_add = add is not None
    n_in = 3 if has_add else 2
    grid = (M // tm, N // tn, K // tk)
    rd = rider if rider is not None else _Part([], [], [], [], [], lambda *a: None)
    rd.total = grid[0] * grid[1] * grid[2]

    def body(*refs):
        a_ref, b_ref = refs[:2]
        c_ref = refs[2] if has_add else None
        rd_ins, o_ref = refs[n_in:n_in + rd.n_in], refs[n_in + rd.n_in]
        rest = refs[n_in + rd.n_in + 1:]
        i, j, k = pl.program_id(0), pl.program_id(1), pl.program_id(2)
        rd.step((i * grid[1] + j) * grid[2] + k, rd_ins, rest[:rd.n_out], rest[rd.n_out:])
        p = lax.dot_general(a_ref[...].astype(bf16), b_ref[...].astype(bf16), dims, preferred_element_type=f32)

        @pl.when(k == 0)
        def _():
            o_ref[...] = p + c_ref[...] if has_add else p

        @pl.when(k > 0)
        def _():
            o_ref[...] += p

    o_spec = pl.BlockSpec((tm, tn), lambda i, j, k: (i, j))
    in_specs = [a_spec, b_spec] + ([o_spec] if has_add else []) + rd.in_specs
    args = (a, b) + ((add,) if has_add else ()) + tuple(rd.args)
    outs = pl.pallas_call(
        body, name=name, grid=grid, in_specs=in_specs, out_specs=[o_spec] + rd.out_specs,
        out_shape=[SDS((M, N), f32)] + rd.out_shape, scratch_shapes=rd.scratch,
        compiler_params=_params(("parallel", "parallel", "arbitrary") if rider is None else ("arbitrary",) * 3, 60),
    )(*args)
    return outs[0] if rider is None else (outs[0], list(outs[1:]))


def _rwkv_core(k_in, v_in, wdad, w0, wup, a0, aup, kkw, kaw, ee, vmix):
    w_raw = w0 + jnp.dot(jnp.tanh(wdad), wup, preferred_element_type=f32)
    decay = jnp.exp(-EXP_M05 * jax.nn.sigmoid(w_raw))
    a = jax.nn.sigmoid(a0 + jnp.dot(wdad, aup, preferred_element_type=f32))
    if vmix is None:
        v = v_in
    else:
        v0, vdn, vup, vfirst = vmix
        gate = jax.nn.sigmoid(v0 + jnp.dot(jnp.dot(v_in, vdn, preferred_element_type=f32), vup,
                                           preferred_element_type=f32))
        v = v_in + (vfirst - v_in) * gate
    kk = k_in * kkw
    n2 = _seg(kk * kk, ee)
    kk = kk / jnp.maximum(jnp.sqrt(n2), 1e-12)
    k2 = k_in * (1.0 + (a - 1.0) * kaw)
    return decay, k2, v, -kk, kk * a


def _shifted(y, prev_row):
    yp = pltpu.roll(y, 1, 0)
    row = lax.broadcasted_iota(jnp.int32, y.shape, 0)
    return jnp.where(row == 0, prev_row, yp)


def _rwkv_pre_fwd(proj, mu, w0, wup, a0, aup, kkw, kaw, ee, vmix, layer):
    T = proj.shape[0]
    nb = T // TB
    has_mix = vmix is not None

    def body(*refs):
        y_ref, p8_ref, mu_ref, w0_ref, wup_ref, a0_ref, aup_ref, kkw_ref, kaw_ref, ee_ref = refs[:10]
        rest = refs[10:]
        if has_mix:
            v0_ref, vdn_ref, vup_ref, vf_ref = rest[:4]
            rest = rest[4:]
        r_o, w_o, k_o, v_o, a_o, b_o, z_o = rest
        i = pl.program_id(0)
        y = y_ref[...]
        prev = jnp.where(i == 0, 0.0, p8_ref[7:8, :])
        rw = y + mu_ref[...] * (_shifted(y, prev) - y)
        mix = (v0_ref[...], vdn_ref[...], vup_ref[...], vf_ref[...]) if has_mix else None
        dec, k2, v, av, bv = _rwkv_core(rw[:, 1024:2048], rw[:, 2048:3072], rw[:, 4096:4224], w0_ref[...],
                                        wup_ref[...], a0_ref[...], aup_ref[...], kkw_ref[...], kaw_ref[...],
                                        ee_ref[...], mix)
        r_o[...] = rw[:, 0:1024]
        w_o[...] = dec
        k_o[...] = k2
        v_o[...] = v
        a_o[...] = av
        b_o[...] = bv
        z_o[...] = rw[:, 3072:4096]

    row = lambda n: pl.BlockSpec((1, n), lambda i: (0, 0))
    full = lambda s: pl.BlockSpec(s, lambda i: (0,) * len(s))
    blk = pl.BlockSpec((TB, D_RWKV), lambda i: (i, 0))
    in_specs = [pl.BlockSpec((TB, RWKV_COLS), lambda i: (i, 0)),
                pl.BlockSpec((8, RWKV_COLS), lambda i: (jnp.maximum(i * (TB // 8) - 1, 0), 0)),
                row(RWKV_COLS), row(D_RWKV), full((LANES, D_RWKV)), row(D_RWKV), full((LANES, D_RWKV)),
                row(D_RWKV), row(D_RWKV), full((256, 256))]
    args = [proj, proj, mu, w0, wup, a0, aup, kkw, kaw, ee]
    if has_mix:
        v0, vdn, vup, vfirst = vmix
        in_specs += [row(D_RWKV), full((D_RWKV, LANES)), full((LANES, D_RWKV)), blk]
        args += [v0, vdn, vup, vfirst]
    return pl.pallas_call(
        body, name=f"rwkv_pre_fwd_l{layer}", grid=(nb,), in_specs=in_specs, out_specs=[blk] * 7,
        out_shape=[SDS((T, D_RWKV), f32)] * 7, compiler_params=_params(("parallel",), 48),
    )(*args)


def _rwkv_pre_bwd(proj, mu, w0, wup, a0, aup, kkw, kaw, ee, vmix, cots, layer):
    T = proj.shape[0]
    nb = T // TB
    has_mix = vmix is not None
    cot_list = cots["r"] + [cots["w"]] + cots["k"] + cots["v"] + [cots["a"], cots["b"], cots["z"]]
    n_r, n_k, n_v = len(cots["r"]), len(cots["k"]), len(cots["v"])
    n_cot = len(cot_list)

    def body(*refs):
        y_ref, p8_ref, mu_ref, w0_ref, wup_ref, a0_ref, aup_ref, kkw_ref, kaw_ref, ee_ref = refs[:10]
        rest = refs[10:]
        if has_mix:
            v0_ref, vdn_ref, vup_ref, vf_ref = rest[:4]
            rest = rest[4:]
        cot_refs, rest = rest[:n_cot], rest[n_cot:]
        if has_mix:
            (dproj_o, dmu_o, dw0_o, dwup_o, da0_o, daup_o, dkkw_o, dkaw_o,
             dv0_o, dvdn_o, dvup_o, dvf_o, carry) = rest
        else:
            dproj_o, dmu_o, dw0_o, dwup_o, da0_o, daup_o, dkkw_o, dkaw_o, carry = rest
        i = pl.program_id(0)
        blk_i = nb - 1 - i
        y = y_ref[...]
        prev = jnp.where(blk_i == 0, 0.0, p8_ref[7:8, :])
        yp = _shifted(y, prev)
        mu_v = mu_ref[...]
        rw = y + mu_v * (yp - y)

        def sum_refs(rs):
            acc = rs[0][...]
            for r_ in rs[1:]:
                acc = acc + r_[...]
            return acc

        c = list(cot_refs)
        g_r = sum_refs(c[:n_r]); c = c[n_r:]
        g_w = c[0][...]; c = c[1:]
        g_k = sum_refs(c[:n_k]); c = c[n_k:]
        g_v = sum_refs(c[:n_v]); c = c[n_v:]
        g_a, g_b, g_z = c[0][...], c[1][...], c[2][...]

        ee_v = ee_ref[...]
        if has_mix:
            def fn(k_in, v_in, wdad, w0_, wup_, a0_, aup_, kkw_, kaw_, v0_, vdn_, vup_, vf_):
                return _rwkv_core(k_in, v_in, wdad, w0_, wup_, a0_, aup_, kkw_, kaw_, ee_v, (v0_, vdn_, vup_, vf_))
            prim = (rw[:, 1024:2048], rw[:, 2048:3072], rw[:, 4096:4224], w0_ref[...], wup_ref[...], a0_ref[...],
                    aup_ref[...], kkw_ref[...], kaw_ref[...], v0_ref[...], vdn_ref[...], vup_ref[...], vf_ref[...])
        else:
            def fn(k_in, v_in, wdad, w0_, wup_, a0_, aup_, kkw_, kaw_):
                return _rwkv_core(k_in, v_in, wdad, w0_, wup_, a0_, aup_, kkw_, kaw_, ee_v, None)
            prim = (rw[:, 1024:2048], rw[:, 2048:3072], rw[:, 4096:4224], w0_ref[...], wup_ref[...], a0_ref[...],
                    aup_ref[...], kkw_ref[...], kaw_ref[...])
        _, vjp = jax.vjp(fn, *prim)
        gs = vjp((g_w, g_k, g_v, g_a, g_b))
        d_k, d_v, d_wdad = gs[0], gs[1], gs[2]
        g_rw = jnp.concatenate([g_r, d_k, d_v, g_z, d_wdad], axis=1)

        @pl.when(i == 0)
        def _():
            carry[...] = jnp.zeros_like(carry)
            dmu_o[...] = jnp.zeros_like(dmu_o)
            dw0_o[...] = jnp.zeros_like(dw0_o)
            dwup_o[...] = jnp.zeros_like(dwup_o)
            da0_o[...] = jnp.zeros_like(da0_o)
            daup_o[...] = jnp.zeros_like(daup_o)
            dkkw_o[...] = jnp.zeros_like(dkkw_o)
            dkaw_o[...] = jnp.zeros_like(dkaw_o)
            if has_mix:
                dv0_o[...] = jnp.zeros_like(dv0_o)
                dvdn_o[...] = jnp.zeros_like(dvdn_o)
                dvup_o[...] = jnp.zeros_like(dvup_o)

        dmu_o[...] += jnp.sum(g_rw * (yp - y), axis=0, keepdims=True)
        dw0_o[...] += gs[3]
        dwup_o[...] += gs[4]
        da0_o[...] += gs[5]
        daup_o[...] += gs[6]
        dkkw_o[...] += gs[7]
        dkaw_o[...] += gs[8]
        if has_mix:
            dv0_o[...] += gs[9]
            dvdn_o[...] += gs[10]
            dvup_o[...] += gs[11]
            dvf_o[...] = gs[12]
        gm = g_rw * mu_v
        nxt = pltpu.roll(gm, TB - 1, 0)
        rowi = lax.broadcasted_iota(jnp.int32, gm.shape, 0)
        nxt = jnp.where(rowi == TB - 1, carry[...], nxt)
        dproj_o[...] = g_rw - gm + nxt
        carry[...] = gm[0:1, :]

    rev = lambda i: (nb - 1 - i, 0)
    row = lambda n: pl.BlockSpec((1, n), lambda i: (0, 0))
    full = lambda s: pl.BlockSpec(s, lambda i: (0,) * len(s))
    blk = pl.BlockSpec((TB, D_RWKV), rev)
    in_specs = [pl.BlockSpec((TB, RWKV_COLS), rev),
                pl.BlockSpec((8, RWKV_COLS), lambda i: (jnp.maximum((nb - 1 - i) * (TB // 8) - 1, 0), 0)),
                row(RWKV_COLS), row(D_RWKV), full((LANES, D_RWKV)), row(D_RWKV), full((LANES, D_RWKV)),
                row(D_RWKV), row(D_RWKV), full((256, 256))]
    args = [proj, proj, mu, w0, wup, a0, aup, kkw, kaw, ee]
    out_specs = [pl.BlockSpec((TB, RWKV_COLS), rev), row(RWKV_COLS), row(D_RWKV), full((LANES, D_RWKV)),
                 row(D_RWKV), full((LANES, D_RWKV)), row(D_RWKV), row(D_RWKV)]
    out_shape = [SDS((T, RWKV_COLS), f32), SDS((1, RWKV_COLS), f32), SDS((1, D_RWKV), f32), SDS((LANES, D_RWKV), f32),
                 SDS((1, D_RWKV), f32), SDS((LANES, D_RWKV), f32), SDS((1, D_RWKV), f32), SDS((1, D_RWKV), f32)]
    if has_mix:
        v0, vdn, vup, vfirst = vmix
        in_specs += [row(D_RWKV), full((D_RWKV, LANES)), full((LANES, D_RWKV)), blk]
        args += [v0, vdn, vup, vfirst]
        out_specs += [row(D_RWKV), full((D_RWKV, LANES)), full((LANES, D_RWKV)), blk]
        out_shape += [SDS((1, D_RWKV), f32), SDS((D_RWKV, LANES), f32), SDS((LANES, D_RWKV), f32),
                      SDS((T, D_RWKV), f32)]
    in_specs += [blk] * n_cot
    args += cot_list
    return pl.pallas_call(
        body, name=f"rwkv_pre_bwd_l{layer}", grid=(nb,), in_specs=in_specs, out_specs=out_specs, out_shape=out_shape,
        scratch_shapes=[pltpu.VMEM((1, RWKV_COLS), f32)],
        compiler_params=_params(("arbitrary",), 60),
    )(*args)


def _pack_cols(prod_ref, red):
    res = jnp.dot(prod_ref[...].astype(bf16), red, preferred_element_type=f32)
    out = res[0:RWKV_HEAD]
    for s in range(1, SCAN_TB):
        out = out + pltpu.roll(res[s * RWKV_HEAD:(s + 1) * RWKV_HEAD], RWKV_HEADS * s, 1)
    return out


def _unpack_cols(tile_ref, exp):
    rows = []
    for s in range(SCAN_TB):
        half = tile_ref[0, :, pl.ds((s // 4) * LANES, LANES)]
        rows.append(half if s % 4 == 0 else pltpu.roll(half, LANES - 2 * RWKV_HEADS * (s % 4), 1))
    return jnp.dot(jnp.concatenate(rows, axis=0), exp, preferred_element_type=f32)


def _to_col(v):
    T = v.shape[0]
    hi = lax.reduce_precision(v, 8, 7)
    parts = jnp.stack([hi, v - hi], axis=1).astype(bf16)
    parts = parts.reshape(T // SCAN_TB, SCAN_TB, 2, RWKV_HEADS, RWKV_HEAD)
    return jnp.transpose(parts, (0, 4, 1, 2, 3)).reshape(T // SCAN_TB, RWKV_HEAD, 2 * LANES)


def _from_col(vc):
    nb = vc.shape[0]
    tiles = vc.reshape(nb, RWKV_HEAD, SCAN_TB, RWKV_HEADS)
    return jnp.transpose(tiles, (0, 2, 3, 1)).reshape(nb * SCAN_TB, D_RWKV)


def _rwkv_scan_fwd(r, w, k, vcol, a, b, ee, e16, e16t, layer, rider, gather=()):
    T = r.shape[0]
    nb = T // SCAN_TB
    ng = len(gather)

    def body(*refs):
        r_ref, w_ref, k_ref, v_ref, a_ref, an_ref, b_ref, ee_ref, red_ref, exp_ref = refs[:10]
        rd_ins, refs = refs[10:10 + rider.n_in], refs[10 + rider.n_in:]
        g_ins, refs = refs[:ng], refs[ng:]
        o_ref, sp_ref = refs[:2]
        rd_outs, refs = refs[2:2 + rider.n_out], refs[2 + rider.n_out:]
        g_outs, refs = refs[:ng], refs[ng:]
        s_ref, sa_ref, vb_ref, po_ref = refs[:4]
        rd_scr, g_sems = refs[4:4 + rider.n_scr], refs[4 + rider.n_scr:]
        step = pl.program_id(0)
        rider.step(step, rd_ins, rd_outs, rd_scr)

        @pl.when(step == 0)
        def _():
            s_ref[...] = jnp.zeros_like(s_ref)
            sa_ref[...] = jnp.zeros_like(sa_ref)
            if ng:
                _gather_start(g_ins, g_outs, *g_sems)

        if ng:
            @pl.when(step == (3 * nb) // 4)
            def _():
                _gather_forward(g_ins, g_outs, *g_sems)

        ee_v = ee_ref[...]
        r_b, w_b, k_b, a_b, b_b = r_ref[...], w_ref[...], k_ref[...], a_ref[...], b_ref[...]
        rowi = lax.broadcasted_iota(jnp.int32, a_b.shape, 0)
        a_nx = jnp.where(rowi == SCAN_TB - 1, an_ref[0:1, :], pltpu.roll(a_b, SCAN_TB - 1, 0))
        wa = w_b * a_nx
        dots = _seg_raw(jnp.concatenate([b_b * a_nx, k_b * a_nx], axis=0), ee_v, 3)
        beta, kappa = dots[0:SCAN_TB], dots[SCAN_TB:2 * SCAN_TB]
        vb_ref[...] = _unpack_cols(v_ref, exp_ref[...])

        for t in range(SCAN_TB):
            row = lambda x: x[t:t + 1, :]
            blk = pl.ds(t * RWKV_HEAD, RWKV_HEAD)
            sp = s_ref[...]
            sa = sa_ref[...]
            sp_ref[t, 0] = sp
            sp_ref[t, 1] = sa
            vb = vb_ref[blk, :]
            ahead = _seg_raw(sp * row(wa), ee_v, SCAN_PARTS, per_tile=True)
            sn = sp * row(w_b) + sa * row(b_b) + vb * row(k_b)
            s_ref[...] = sn
            sa_ref[...] = ahead + sa * row(beta) + vb * row(kappa)
            po_ref[blk, :] = sn * row(r_b)
        o_ref[0] = _pack_cols(po_ref, red_ref[...])

        if ng:
            @pl.when(step == nb - 1)
            def _():
                _gather_finish(g_ins, g_outs, *g_sems)

    rows = pl.BlockSpec((SCAN_TB, D_RWKV), lambda i: (i, 0))
    rows_next = pl.BlockSpec((SCAN_TB, D_RWKV), lambda i: (jnp.minimum(i + 1, nb - 1), 0))
    col = pl.BlockSpec((1, RWKV_HEAD, LANES), lambda i: (i, 0, 0))
    col2 = pl.BlockSpec((1, RWKV_HEAD, 2 * LANES), lambda i: (i, 0, 0))
    full = lambda s: pl.BlockSpec(s, lambda i: (0,) * len(s))
    outs = pl.pallas_call(
        body, name=f"rwkv_scan_fwd_l{layer}", grid=(nb,),
        in_specs=[rows, rows, rows, col2, rows, rows_next, rows, full((256, 256)), full((D_RWKV, LANES)),
                  full((LANES, D_RWKV))] + rider.in_specs + [HBM_SPEC] * ng,
        out_specs=[col, pl.BlockSpec((SCAN_TB, 2, RWKV_HEAD, D_RWKV), lambda i: (i, 0, 0, 0))] + rider.out_specs
        + [HBM_SPEC] * ng,
        out_shape=[SDS((nb, RWKV_HEAD, LANES), f32), SDS((T, 2, RWKV_HEAD, D_RWKV), f32)] + rider.out_shape
        + _gather_out_shapes(gather),
        scratch_shapes=[pltpu.VMEM((RWKV_HEAD, D_RWKV), f32), pltpu.VMEM((RWKV_HEAD, D_RWKV), f32),
                        pltpu.VMEM((SCAN_TB * RWKV_HEAD, D_RWKV), f32),
                        pltpu.VMEM((SCAN_TB * RWKV_HEAD, D_RWKV), f32)] + rider.scratch
        + (_gather_sems(ng) if ng else []),
        compiler_params=_params(("arbitrary",), 52),
    )(r, w, k, vcol, a, a, b, ee, e16, e16t, *rider.args, *gather)
    return outs[0], outs[1], list(outs[2:2 + rider.n_out]), list(outs[2 + rider.n_out:])


def _rwkv_scan_bwd(r, w, k, vcol, a, b, sprev, docol, ee, e16, e16t, layer, send=()):
    T = r.shape[0]
    nb = T // SCAN_TB
    ns = len(send)

    def body(*refs):
        r_ref, w_ref, k_ref, v_ref, a_ref, b_ref, sp_ref, do_ref, ee_ref, red_ref, exp_ref = refs[:11]
        c_ins, refs = refs[11:11 + ns], refs[11 + ns:]
        dr_o, dw_o, dk_o, da_o, db_o, dv_o = refs[:6]
        c_outs, refs = refs[6:6 + ns], refs[6 + ns:]
        ds_ref, snext_ref, vb_ref, dob_ref, pdv_ref = refs[:5]
        c_sems = refs[5:]
        if ns:
            @pl.when(pl.program_id(0) == 0)
            def _():
                for cp in _chips_copies(c_ins, c_outs, *c_sems):
                    cp.start()

        ee_v = ee_ref[...]
        csum = lambda x: jnp.sum(x, axis=0, keepdims=True)
        nrow = SCAN_TB * RWKV_HEAD
        last = pl.ds(nrow - RWKV_HEAD, RWKV_HEAD)
        r_b, w_b, k_b, a_b, b_b = r_ref[...], w_ref[...], k_ref[...], a_ref[...], b_ref[...]
        vb_ref[...] = _unpack_cols(v_ref, exp_ref[...])
        dob_ref[...] = _unpack_cols(do_ref, exp_ref[...])

        @pl.when(pl.program_id(0) == 0)
        def _():
            ds_ref[...] = jnp.zeros_like(ds_ref)
            snext_ref[...] = (sp_ref[SCAN_TB - 1, 0] * w_b[SCAN_TB - 1:SCAN_TB, :]
                              + sp_ref[SCAN_TB - 1, 1] * b_b[SCAN_TB - 1:SCAN_TB, :]
                              + vb_ref[last, :] * k_b[SCAN_TB - 1:SCAN_TB, :])

        for s in range(SCAN_TB):
            t = SCAN_TB - 1 - s
            row = lambda x: x[t:t + 1, :]
            blk = pl.ds(t * RWKV_HEAD, RWKV_HEAD)
            sp, sa = sp_ref[t, 0], sp_ref[t, 1]
            sn = snext_ref[...] if s == 0 else sp_ref[t + 1, 0]
            vb, dob = vb_ref[blk, :], dob_ref[blk, :]
            ds = ds_ref[...] + dob * row(r_b)
            dsa = _seg_raw(ds * row(b_b), ee_v, SCAN_PARTS, per_tile=True)
            ds_ref[...] = ds * row(w_b) + dsa * row(a_b)
            dr_o[t:t + 1, :] = csum(sn * dob)
            pdv_ref[blk, :] = ds * row(k_b)
            dk_o[t:t + 1, :] = csum(ds * vb)
            db_o[t:t + 1, :] = csum(ds * sa)
            dw_o[t:t + 1, :] = csum(ds * sp)
            da_o[t:t + 1, :] = csum(sp * dsa)
        snext_ref[...] = sp_ref[0, 0]
        dv_o[0] = _pack_cols(pdv_ref, red_ref[...])

        if ns:
            @pl.when(pl.program_id(0) == nb - 1)
            def _():
                for cp in _chips_copies(c_ins, c_outs, *c_sems):
                    cp.wait()

    rows = pl.BlockSpec((SCAN_TB, D_RWKV), lambda i: (nb - 1 - i, 0))
    col = pl.BlockSpec((1, RWKV_HEAD, LANES), lambda i: (nb - 1 - i, 0, 0))
    col2 = pl.BlockSpec((1, RWKV_HEAD, 2 * LANES), lambda i: (nb - 1 - i, 0, 0))
    st = pl.BlockSpec((SCAN_TB, 2, RWKV_HEAD, D_RWKV), lambda i: (nb - 1 - i, 0, 0, 0))
    full = lambda s: pl.BlockSpec(s, lambda i: (0,) * len(s))
    big = pltpu.VMEM((SCAN_TB * RWKV_HEAD, D_RWKV), f32)
    outs = pl.pallas_call(
        body, name=f"rwkv_scan_bwd_l{layer}", grid=(nb,),
        in_specs=[rows, rows, rows, col2, rows, rows, st, col2, full((256, 256)), full((D_RWKV, LANES)),
                  full((LANES, D_RWKV))] + [HBM_SPEC] * ns,
        out_specs=[rows] * 5 + [col] + [HBM_SPEC] * ns,
        out_shape=[SDS((T, D_RWKV), f32)] * 5 + [SDS((nb, RWKV_HEAD, LANES), f32)] + _chips_out_shapes(send),
        scratch_shapes=[pltpu.VMEM((RWKV_HEAD, D_RWKV), f32)] * 2 + [big] * 3 + (_chips_sems(ns) if ns else []),
        compiler_params=_params(("arbitrary",), 56),
    )(r, w, k, vcol, a, b, sprev, docol, ee, e16, e16t, *send)
    return (*outs[:6], list(outs[6:]))


def _post_core(o, r, k2, v, z, gw, gb, rk, ee):
    inv = 1.0 / RWKV_HEAD
    mu = _seg(o, ee) * inv
    d = o - mu
    var = _seg(d * d, ee) * inv
    on = d * lax.rsqrt(var + GN_EPS) * gw + gb
    bonus = _seg(r * k2 * rk, ee) * v
    return (on + bonus) * jax.nn.silu(z)


def _rwkv_post_fwd(o, r, k2, v, z, gw, gb, rk, ee, cat, layer):
    T = o.shape[0]

    def body(o_ref, r_ref, k_ref, v_ref, z_ref, gw_ref, gb_ref, rk_ref, ee_ref, cat_in, out_ref):
        del cat_in
        out_ref[...] = _post_core(o_ref[...], r_ref[...], k_ref[...], v_ref[...], z_ref[...], gw_ref[...],
                                  gb_ref[...], rk_ref[...], ee_ref[...])

    blk = pl.BlockSpec((TB, D_RWKV), lambda i: (i, 0))
    row = pl.BlockSpec((1, D_RWKV), lambda i: (0, 0))
    return pl.pallas_call(
        body, name=f"rwkv_post_fwd_l{layer}", grid=(T // TB,),
        in_specs=[blk] * 5 + [row] * 3 + [pl.BlockSpec((256, 256), lambda i: (0, 0)), HBM_SPEC],
        out_specs=blk, out_shape=SDS((T, D_MODEL), f32), input_output_aliases={9: 0},
        compiler_params=_params(("parallel",), 40),
    )(o, r, k2, v, z, gw, gb, rk, ee, cat)


def _rwkv_post_bwd(o, r, k2, v, z, gw, gb, rk, ee, dcat, layer):
    T = o.shape[0]

    def body(o_ref, r_ref, k_ref, v_ref, z_ref, gw_ref, gb_ref, rk_ref, ee_ref, g_ref,
             do_o, dr_o, dk_o, dv_o, dz_o, dgw_o, dgb_o, drk_o):
        ee_v = ee_ref[...]
        fn = lambda o_, r_, k_, v_, z_, gw_, gb_, rk_: _post_core(o_, r_, k_, v_, z_, gw_, gb_, rk_, ee_v)
        _, vjp = jax.vjp(fn, o_ref[...], r_ref[...], k_ref[...], v_ref[...], z_ref[...], gw_ref[...], gb_ref[...],
                         rk_ref[...])
        gs = vjp(g_ref[...])
        do_o[...], dr_o[...], dk_o[...], dv_o[...], dz_o[...] = gs[:5]

        @pl.when(pl.program_id(0) == 0)
        def _():
            dgw_o[...] = jnp.zeros_like(dgw_o)
            dgb_o[...] = jnp.zeros_like(dgb_o)
            drk_o[...] = jnp.zeros_like(drk_o)

        dgw_o[...] += gs[5]
        dgb_o[...] += gs[6]
        drk_o[...] += gs[7]

    blk = pl.BlockSpec((TB, D_RWKV), lambda i: (i, 0))
    row = pl.BlockSpec((1, D_RWKV), lambda i: (0, 0))
    return pl.pallas_call(
        body, name=f"rwkv_post_bwd_l{layer}", grid=(T // TB,),
        in_specs=[blk] * 5 + [row] * 3 + [pl.BlockSpec((256, 256), lambda i: (0, 0)), blk],
        out_specs=[blk] * 5 + [row] * 3,
        out_shape=[SDS((T, D_RWKV), f32)] * 5 + [SDS((1, D_RWKV), f32)] * 3,
        compiler_params=_params(("arbitrary",), 48),
    )(o, r, k2, v, z, gw, gb, rk, ee, dcat)


def _hgrn_chunk(qr, fr, ii, z, S, lbl, gw, layer):
    L = CHUNK
    G = qr.shape[1] // HG_HEAD
    hd = lambda x, g: x[..., g * HG_HEAD:(g + 1) * HG_HEAD]
    heads = lambda f: jnp.concatenate([f(g) for g in range(G)], axis=-1)
    m = jnp.max(lbl, axis=0, keepdims=True)
    e = jnp.exp(lbl - m)
    sm = e / jnp.sum(e, axis=0, keepdims=True)
    if layer == 0:
        lb = sm[0:1] - sm[0:1]
    else:
        lb = (sm[0:1] + sm[1:2]) - sm[0:1]
    q = jax.nn.silu(qr)
    log_lb = jnp.log(jnp.maximum(lb, LB_FLOOR))
    lf = jnp.logaddexp(log_lb, jnp.log1p(-lb) + jax.nn.log_sigmoid(fr))
    k = (1.0 - lb) * jax.nn.sigmoid(-fr)
    ti = lax.broadcasted_iota(jnp.int32, (L, L), 0)
    si = lax.broadcasted_iota(jnp.int32, (L, L), 1)
    ltri = (si <= ti).astype(f32)
    b = jnp.dot(ltri, lf, precision=HIGHEST, preferred_element_type=f32)
    btot = jnp.dot(jnp.ones((L, L), f32), lf, precision=HIGHEST, preferred_element_type=f32)
    qd = q * jnp.exp(b)
    o = heads(lambda g: jnp.dot(hd(qd, g), S[g], preferred_element_type=f32))
    nsub = L // HSUB
    t3 = lax.broadcasted_iota(jnp.int32, (HSUB, HSUB, G * HG_HEAD), 0)
    s3 = lax.broadcasted_iota(jnp.int32, (HSUB, HSUB, G * HG_HEAD), 1)
    causal = s3 <= t3
    nt = (((1,), (1,)), ((), ()))
    tn = (((0,), (0,)), ((), ()))
    rows = []
    for I in range(nsub):
        sl = slice(I * HSUB, (I + 1) * HSUB)
        bI, qI, kI, iI = b[sl], q[sl], k[sl], ii[sl]
        d3 = bI[:, None, :] - bI[None, :, :]
        dec = jnp.where(causal, jnp.exp(jnp.where(causal, d3, 0.0)), 0.0)
        prod = qI[:, None, :] * dec * kI[None, :, :]
        oI = heads(lambda g: jnp.dot(jnp.sum(hd(prod, g), axis=-1), hd(iI, g), preferred_element_type=f32))
        for J in range(I):
            sj = slice(J * HSUB, (J + 1) * HSUB)
            bm = b[(J + 1) * HSUB - 1:(J + 1) * HSUB]
            qs = qI * jnp.exp(bI - bm)
            ks = k[sj] * jnp.exp(bm - b[sj])
            iJ = ii[sj]
            oI = oI + heads(lambda g: jnp.dot(
                lax.dot_general(hd(qs, g), hd(ks, g), nt, preferred_element_type=f32), hd(iJ, g),
                preferred_element_type=f32))
        rows.append(oI)
    o = o + jnp.concatenate(rows, axis=0)
    k_dec = k * jnp.exp(btot - b)
    ones = jnp.ones((L, HG_HEAD), f32)
    s_new = jnp.concatenate([
        (S[g] * jnp.exp(lax.dot_general(hd(lf, g), ones, tn, precision=HIGHEST, preferred_element_type=f32))
         + lax.dot_general(hd(k_dec, g), hd(ii, g), tn, preferred_element_type=f32))[None] for g in range(G)], axis=0)
    ms = heads(lambda g: jnp.broadcast_to(jnp.mean(hd(o * o, g), axis=-1, keepdims=True), (L, HG_HEAD)))
    out = o * lax.rsqrt(ms + RMS_EPS) * gw * jax.nn.silu(z)
    return out, s_new


class _Part:
    def __init__(self, args, in_specs, out_specs, out_shape, scratch, step):
        self.args, self.in_specs, self.out_specs, self.out_shape = args, in_specs, out_specs, out_shape
        self.scratch, self.step, self.total = scratch, step, None
        self.n_in, self.n_out, self.n_scr = len(args), len(out_shape), len(scratch)


def _run_part(part, steps, name):
    part.total = steps

    def body(*refs):
        ins, outs = refs[:part.n_in], refs[part.n_in:part.n_in + part.n_out]
        part.step(pl.program_id(0), ins, outs, refs[part.n_in + part.n_out:])

    return pl.pallas_call(body, name=name, grid=(steps,), in_specs=part.in_specs, out_specs=part.out_specs,
                          out_shape=part.out_shape, scratch_shapes=part.scratch,
                          compiler_params=_params(("arbitrary",), 48))(*part.args)


def _hg_index(T, rev, group=1):
    nc = T // CHUNK
    width = group * HG_HEAD
    head = lambda i: i // nc
    chunk = (lambda i: nc - 1 - i % nc) if rev else (lambda i: i % nc)
    first = lambda i: i % nc == 0
    pj = lambda off, g: pl.BlockSpec((CHUNK, HG_HEAD),
                                     lambda i: (chunk(i), RW_BLKS + off + group * head(i) + g))
    pspecs = [pj(kind * HG_HEADS, g) for kind in range(4) for g in range(group)]
    per_head = lambda rows: pl.BlockSpec((rows, width), lambda i: (0, head(i)))
    return nc, head, chunk, first, pspecs, per_head


def _hgrn_fwd_part(proj, lbl, gw, layer):
    T = proj.shape[0]
    nc, head, chunk, first, pspecs, per_head = _hg_index(T, False)

    def step(i, ins, outs, scr):
        q_ref, f_ref, i_ref, z_ref, lbl_ref, gw_ref = ins
        out_ref, sh_ref = outs
        s_ref, = scr

        @pl.when(first(i))
        def _():
            s_ref[...] = jnp.zeros_like(s_ref)

        s0 = s_ref[...]
        sh_ref[:, 0] = s0
        out, s_new = _hgrn_chunk(q_ref[...], f_ref[...], i_ref[...], z_ref[...], s0, lbl_ref[...], gw_ref[...], layer)
        out_ref[...] = out
        s_ref[...] = s_new

    return _Part(
        [proj, proj, proj, proj, lbl, gw], pspecs + [per_head(2), per_head(1)],
        [pl.BlockSpec((CHUNK, HG_HEAD), lambda i: (chunk(i), HG_HEADS + head(i))),
         pl.BlockSpec((1, 1, HG_HEAD, HG_HEAD), lambda i: (head(i), chunk(i), 0, 0))],
        [SDS((T, D_MODEL), f32), SDS((HG_HEADS, nc, HG_HEAD, HG_HEAD), f32)],
        [pltpu.VMEM((1, HG_HEAD, HG_HEAD), f32)], step)


HG_GROUP = 2


def _hgrn_bwd_part(proj, shist, dcat, lbl, gw, layer):
    T = proj.shape[0]
    G = HG_GROUP
    nc, head, chunk, first, pspecs, per_head = _hg_index(T, True, G)

    def step(i, ins, outs, scr):
        p_refs, (lbl_ref, gw_ref, sh_ref, g_ref) = ins[:4 * G], ins[4 * G:]
        dq_o, df_o, di_o, dz_o, dlbl_o, dgw_o = outs
        ds_ref, = scr

        @pl.when(first(i))
        def _():
            ds_ref[...] = jnp.zeros_like(ds_ref)
            dlbl_o[...] = jnp.zeros_like(dlbl_o)
            dgw_o[...] = jnp.zeros_like(dgw_o)

        fn = functools.partial(_hgrn_chunk, layer=layer)
        q, f, ii, z = (jnp.concatenate([p_refs[kind * G + g][...] for g in range(G)], axis=1) for kind in range(4))
        _, vjp = jax.vjp(fn, q, f, ii, z, sh_ref[:, 0], lbl_ref[...], gw_ref[...])
        dq, df, di, dz, ds, dl, dg = vjp((g_ref[...], ds_ref[...]))
        dq_o[...], df_o[...], di_o[...], dz_o[...] = dq, df, di, dz
        ds_ref[...] = ds
        dlbl_o[...] += dl
        dgw_o[...] += dg

    ospec = pl.BlockSpec((CHUNK, G * HG_HEAD), lambda i: (chunk(i), head(i)))
    return _Part(
        [proj] * (4 * G) + [lbl, gw, shist, dcat],
        pspecs + [per_head(2), per_head(1),
                  pl.BlockSpec((G, 1, HG_HEAD, HG_HEAD), lambda i: (head(i), chunk(i), 0, 0)),
                  pl.BlockSpec((CHUNK, G * HG_HEAD), lambda i: (chunk(i), HG_HEADS // G + head(i)))],
        [ospec] * 4 + [per_head(2), per_head(1)],
        [SDS((T, D_HGRN), f32)] * 4 + [SDS((2, D_HGRN), f32), SDS((1, D_HGRN), f32)],
        [pltpu.VMEM((G, HG_HEAD, HG_HEAD), f32)], step)


def _ln_core(h, y, w, b):
    u = ALPHA * h + y
    mu = jnp.mean(u, axis=-1, keepdims=True)
    d = u - mu
    var = jnp.mean(d * d, axis=-1, keepdims=True)
    return d * lax.rsqrt(var + LN_EPS) * w + b


def _ln_fwd(h, y, w, b, layer):
    T = h.shape[0]

    def body(h_ref, y_ref, w_ref, b_ref, o_ref):
        o_ref[...] = _ln_core(h_ref[...], y_ref[...], w_ref[...], b_ref[...])

    blk = pl.BlockSpec((TB, D_MODEL), lambda i: (i, 0))
    row = pl.BlockSpec((1, D_MODEL), lambda i: (0, 0))
    return pl.pallas_call(body, name=f"ln_fwd_l{layer}", grid=(T // TB,), in_specs=[blk, blk, row, row],
                          out_specs=blk, out_shape=SDS((T, D_MODEL), f32),
                          compiler_params=_params(("parallel",), 40))(h, y, w, b)


def _ln_bwd(h, y, w, b, g, layer):
    T = h.shape[0]

    def body(h_ref, y_ref, w_ref, b_ref, g_ref, dh_o, dy_o, dw_o, db_o):
        _, vjp = jax.vjp(_ln_core, h_ref[...], y_ref[...], w_ref[...], b_ref[...])
        dh, dy, dw, db = vjp(g_ref[...])
        dh_o[...] = dh
        dy_o[...] = dy.astype(bf16)

        @pl.when(pl.program_id(0) == 0)
        def _():
            dw_o[...] = jnp.zeros_like(dw_o)
            db_o[...] = jnp.zeros_like(db_o)

        dw_o[...] += dw
        db_o[...] += db

    blk = pl.BlockSpec((TB, D_MODEL), lambda i: (i, 0))
    row = pl.BlockSpec((1, D_MODEL), lambda i: (0, 0))
    return pl.pallas_call(body, name=f"ln_bwd_l{layer}", grid=(T // TB,), in_specs=[blk, blk, row, row, blk],
                          out_specs=[blk, blk, row, row],
                          out_shape=[SDS((T, D_MODEL), f32), SDS((T, D_MODEL), bf16)] + [SDS((1, D_MODEL), f32)] * 2,
                          compiler_params=_params(("arbitrary",), 48))(h, y, w, b, g)


def _loss_head(hout, target):
    T = hout.shape[0]

    def body(h_ref, t_ref, loss_o, g_o):
        err = h_ref[...] - t_ref[...]
        g_o[...] = err * (1.0 / D_MODEL)

        @pl.when(pl.program_id(0) == 0)
        def _():
            loss_o[...] = jnp.zeros_like(loss_o)

        part = jnp.sum(jnp.sum(err * err, axis=-1, keepdims=True) * (1.0 / D_MODEL), axis=0, keepdims=True)
        loss_o[...] += 0.5 * part

    blk = pl.BlockSpec((TB, D_MODEL), lambda i: (i, 0))
    return pl.pallas_call(body, name="loss_head", grid=(T // TB,), in_specs=[blk, blk],
                          out_specs=[pl.BlockSpec((1, 1), lambda i: (0, 0)), blk],
                          out_shape=[SDS((1, 1), f32), SDS((T, D_MODEL), f32)],
                          compiler_params=_params(("arbitrary",), 40))(hout, target)


def _pad_rows(m, lo, total):
    return jnp.pad(m, ((0, 0), (lo, total - lo - m.shape[1]), (0, 0)))


def _device_step(x, target, w_in0, later_shards, assemble_later, on_grads, on_recv, on_small_grads,
                 shift_mu, w_decay0, w_decay_up, a0, a_up, k_k, k_a, r_k, ln_x_w, ln_x_b,
                 v_mix0, v_mix_down, v_mix_up, lb_logits, g_norm_w, ln_w, ln_b):
    T = x.shape[0]
    ee, e16, e16t = _const_mats()
    wup_pad = _pad_rows(w_decay_up, 0, LANES)
    aup_pad = _pad_rows(a_up, LORA, LANES)
    vdn_pad = jnp.pad(v_mix_down, ((0, 0), (0, 0), (0, LANES - VRES)))
    vup_pad = _pad_rows(v_mix_up, 0, LANES)
    row = lambda p, l: p[l][None, :]

    def vmix_of(l, vfirst):
        if l == 0:
            return None
        return (row(v_mix0, l - 1), vdn_pad[l - 1], vup_pad[l - 1], vfirst)

    h = x
    saved = []
    vfirst = None
    w_in = [w_in0, None]
    w_out = None
    for l in range(DEPTH):
        hb = h.astype(bf16)
        proj = _matmul(hb, w_in[l], mode="nt", tm=512, tn=1664, tk=D_MODEL, name=f"proj_fwd_l{l}")
        pre_args = (proj, row(shift_mu, l), row(w_decay0, l), wup_pad[l], row(a0, l), aup_pad[l], row(k_k, l),
                    row(k_a, l), ee, vmix_of(l, vfirst))
        r, w, k2, v, av, bv, z = _rwkv_pre_fwd(*pre_args, layer=l)
        if l == 0:
            vfirst = v
        vcol = _to_col(v)
        ocol, sprev, (cat, shist), gathered = _rwkv_scan_fwd(
            r, w, k2, vcol, av, bv, ee, e16, e16t, l, _hgrn_fwd_part(proj, lb_logits, row(g_norm_w, l), l),
            gather=later_shards if l == 0 else [])
        if l == 0:
            w_in[1], w_out = assemble_later(gathered)
        o = _from_col(ocol)
        post_args = (o, r, k2, v, z, row(ln_x_w, l), row(ln_x_b, l), row(r_k, l), ee)
        cat = _rwkv_post_fwd(*post_args, cat, layer=l)
        catb = cat.astype(bf16)
        y = _matmul(catb, w_out, b_layer=l, mode="nn", tm=512, tn=1024, tk=D_MODEL, name=f"out_fwd_l{l}")
        h_new = _ln_fwd(h, y, row(ln_w, l), row(ln_b, l), l)
        saved.append(dict(h=h, hb=hb, proj=proj, pre_args=pre_args, post_args=post_args,
                          scan=(r, w, k2, vcol, av, bv, sprev), cat=catb, shist=shist, y=y))
        h = h_new

    loss_sum, g = _loss_head(h, target)

    grads = {n: [None] * DEPTH for n in ("w_in", "w_out", "shift_mu", "w_decay0", "w_decay_up", "a0", "a_up", "k_k",
                                         "k_a", "r_k", "ln_x_w", "ln_x_b", "g_norm_w", "ln_w", "ln_b")}
    dlbl = []
    dvfirst = None
    to_send = []
    for l in reversed(range(DEPTH)):
        s = saved[l]
        dh_res, dy, dlnw, dlnb = _ln_bwd(s["h"], s["y"], row(ln_w, l), row(ln_b, l), g, l)
        grads["ln_w"][l], grads["ln_b"][l] = dlnw[0], dlnb[0]
        dcat = _matmul(dy, w_out, b_layer=l, mode="nt", tm=512, tn=1024, tk=D_MODEL, name=f"out_bwd_dx_l{l}")
        grads["w_out"][l] = _matmul(s["cat"], dy, mode="tn", tm=1024, tn=1024, tk=512, name=f"out_bwd_dw_l{l}")
        do, dr_p, dk_p, dv_p, dz, dgw, dgb, drk = _rwkv_post_bwd(*s["post_args"], dcat, layer=l)
        grads["ln_x_w"][l], grads["ln_x_b"][l], grads["r_k"][l] = dgw[0], dgb[0], drk[0]
        r, w, k2, vcol, av, bv, sprev = s["scan"]
        dr_s, dw_s, dk_s, da_s, db_s, dvcol, arrived = _rwkv_scan_bwd(
            r, w, k2, vcol, av, bv, sprev, _to_col(do), ee, e16, e16t, l, send=to_send if l == 0 else [])
        dq, df, di, dzh, dl, dgn = _run_part(
            _hgrn_bwd_part(s["proj"], s["shist"], dcat, lb_logits, row(g_norm_w, l), l),
            (T // CHUNK) * HG_HEADS // HG_GROUP, f"hgrn_bwd_l{l}")
        if l == 0:
            on_recv(1, arrived)
        v_cots = [dv_p, _from_col(dvcol)] + ([dvfirst] if (l == 0 and dvfirst is not None) else [])
        cots = dict(r=[dr_p, dr_s], w=dw_s, k=[dk_p, dk_s], v=v_cots, a=da_s, b=db_s, z=dz)
        outs = _rwkv_pre_bwd(*s["pre_args"], cots, layer=l)
        dproj, dmu, dw0, dwup, da0, daup, dkkw, dkaw = outs[:8]
        grads["shift_mu"][l], grads["w_decay0"][l], grads["a0"][l] = dmu[0], dw0[0], da0[0]
        grads["k_k"][l], grads["k_a"][l] = dkkw[0], dkaw[0]
        grads["w_decay_up"][l], grads["a_up"][l] = dwup[:LORA], daup[LORA:]
        if l > 0:
            dv0, dvdn, dvup, dvfirst = outs[8:]
            g_vmix = (dv0, dvdn[:, :VRES][None], dvup[:VRES][None])
        dlbl.append(dl)
        grads["g_norm_w"][l] = dgn[0]
        dproj = jnp.concatenate([dproj, dq, df, di, dzh], axis=1).astype(bf16)
        dw_args = dict(mode="tn", tm=1664, tn=1024, tk=512, name=f"proj_bwd_dw_l{l}")
        dx_args = dict(mode="nn", tm=512, tn=1024, tk=1664, name=f"proj_bwd_dx_l{l}", add=dh_res)
        if l == 1:
            grads["w_in"][l] = _matmul(dproj, s["hb"], **dw_args)
            to_send = on_grads(l, grads["w_in"][l], grads["w_out"][l])
            g = _matmul(dproj, w_in[l], **dx_args)
        else:
            small = {n: jnp.stack(v) for n, v in grads.items() if n not in ("w_in", "w_out")}
            small["lb_logits"] = dlbl[0] + dlbl[1]
            small["v_mix0"], small["v_mix_down"], small["v_mix_up"] = g_vmix
            grads["w_in"][l], small_out = _matmul(dproj, s["hb"], rider=on_small_grads(small), **dw_args)
            g, arrived = _matmul(dproj, w_in[l], rider=_chips_part(on_grads(l, grads["w_in"][l], grads["w_out"][l])),
                                 **dx_args)
            on_recv(0, arrived)
    small["w_in"], small["w_out"] = grads["w_in"], grads["w_out"]
    return loss_sum, g, small, small_out


CHIP_COMBOS = ((1, 0), (0, 1), (1, 1))
HBM_SPEC = pl.BlockSpec(memory_space=pl.ANY)


def _mesh_pos():
    return lax.axis_index("x"), lax.axis_index("y"), lax.axis_index("c")


def _flip(v, d):
    return 1 - v if d else v


def _half(ref, dim, cc, hs):
    idx = [slice(None)] * len(ref.shape)
    idx[dim] = pl.ds(cc * hs, hs)
    return ref.at[tuple(idx)]


def _gather_shards(arrs):
    n = len(arrs)

    def body(*refs):
        ins, outs = refs[:n], refs[n:2 * n]
        sems = refs[2 * n:]
        _gather_start(ins, outs, *sems)
        _gather_forward(ins, outs, *sems)
        _gather_finish(ins, outs, *sems)

    return pl.pallas_call(
        body, name="gather_shards", in_specs=[HBM_SPEC] * n, out_specs=[HBM_SPEC] * n,
        out_shape=_gather_out_shapes(arrs), scratch_shapes=_gather_sems(n),
    )(*arrs)


def _gather_out_shapes(arrs):
    return [SDS((4,) + a.shape, a.dtype) for a in arrs]


def _gather_sems(n):
    return [pltpu.SemaphoreType.DMA((n, 6)), pltpu.SemaphoreType.DMA((n, 6)), pltpu.SemaphoreType.DMA((n,))]


def _gather_copies(kind, ins, outs, send_sems, recv_sems, local_sems):
    x, y, c = _mesh_pos()
    kc = 2 * x + y
    cps = []
    for a in range(len(ins)):
        if kind == "local":
            cps.append(pltpu.make_async_copy(ins[a], outs[a].at[kc], local_sems.at[a]))
            continue
        for j, (dx, dy) in enumerate(CHIP_COMBOS):
            px, py = _flip(x, dx), _flip(y, dy)
            kj = 2 * px + py
            src, dst, s, to = {
                "sends": (ins[a].at[c], outs[a].at[kc, c], j, (px, py, c)),
                "landed": (outs[a].at[kj, c], outs[a].at[kj, c], j, (px, py, c)),
                "forwards": (outs[a].at[kj, c], outs[a].at[kj, c], 3 + j, (x, y, 1 - c)),
                "passed": (outs[a].at[kj, 1 - c], outs[a].at[kj, 1 - c], 3 + j, (x, y, 1 - c)),
            }[kind]
            cps.append(pltpu.make_async_remote_copy(
                src_ref=src, dst_ref=dst, send_sem=send_sems.at[a, s], recv_sem=recv_sems.at[a, s], device_id=to,
                device_id_type=MESH))
    return cps


def _gather_start(*refs):
    for cp in _gather_copies("local", *refs) + _gather_copies("sends", *refs):
        cp.start()


def _gather_forward(*refs):
    for arrival, fw in zip(_gather_copies("landed", *refs), _gather_copies("forwards", *refs)):
        arrival.wait_recv()
        fw.start()


def _gather_finish(*refs):
    for cp in _gather_copies("passed", *refs):
        cp.wait_recv()
    for cp in _gather_copies("sends", *refs) + _gather_copies("forwards", *refs):
        cp.wait_send()
    for cp in _gather_copies("local", *refs):
        cp.wait()


DEV_COMBOS = tuple((dx, dy, dc) for dx in (0, 1) for dy in (0, 1) for dc in (0, 1))[1:]


def _allreduce_part(buf):
    R = buf.shape[0]
    whole = pl.BlockSpec((R, LANES), lambda *_: (0, 0))

    def copies(x_ref, slots, send_sems, recv_sems):
        x, y, c = _mesh_pos()
        return [pltpu.make_async_remote_copy(
            src_ref=x_ref, dst_ref=slots.at[k + 1], send_sem=send_sems.at[k], recv_sem=recv_sems.at[k],
            device_id=(_flip(x, dx), _flip(y, dy), _flip(c, dc)), device_id_type=MESH)
            for k, (dx, dy, dc) in enumerate(DEV_COMBOS)]

    def step(i, ins, outs, scr):
        x_ref, = ins
        o_ref, = outs
        slots = scr[0]

        @pl.when(i == 0)
        def _():
            for cp in copies(x_ref, *scr):
                cp.start()
            slots[0] = x_ref[...]

        @pl.when(i == part.total - 1)
        def _():
            x, y, c = _mesh_pos()
            me = 4 * x + 2 * y + c
            cps = copies(x_ref, *scr)
            for cp in cps:
                cp.wait_recv()
            acc = slots[jnp.bitwise_xor(me, 0)]
            for d in range(1, 8):
                acc = acc + slots[jnp.bitwise_xor(me, d)]
            o_ref[...] = acc
            for cp in cps:
                cp.wait_send()

    part = _Part([buf], [whole], [whole], [SDS((R, LANES), f32)],
                 [pltpu.VMEM((8, R, LANES), f32), pltpu.SemaphoreType.DMA((7,)), pltpu.SemaphoreType.DMA((7,))], step)
    return part


def _chips_part(slabs):
    n = len(slabs)

    def step(i, ins, outs, scr):
        if not n:
            return

        @pl.when(i == 0)
        def _():
            for cp in _chips_copies(ins, outs, *scr):
                cp.start()

        @pl.when(i == part.total - 1)
        def _():
            for cp in _chips_copies(ins, outs, *scr):
                cp.wait()

    part = _Part(list(slabs), [HBM_SPEC] * n, [HBM_SPEC] * n, _chips_out_shapes(slabs),
                 _chips_sems(n) if n else [], step)
    return part


def _swap_halves(gs, tag):
    n = len(gs)

    def body(*refs):
        ins, outs = refs[:n], refs[n:2 * n]
        send_sems, recv_sems = refs[2 * n:]
        x, y, c = _mesh_pos()
        cps = []
        for a in range(n):
            h = ins[a].shape[1] // 2
            cp = pltpu.make_async_remote_copy(
                src_ref=_half(ins[a], 1, 1 - c, h), dst_ref=outs[a], send_sem=send_sems.at[a],
                recv_sem=recv_sems.at[a], device_id=(x, y, 1 - c), device_id_type=MESH)
            cp.start()
            cps.append(cp)
        for cp in cps:
            cp.wait()

    return pl.pallas_call(
        body, name=f"rs_swap_halves_{tag}", in_specs=[HBM_SPEC] * n, out_specs=[HBM_SPEC] * n,
        out_shape=[SDS((g.shape[0], g.shape[1] // 2, g.shape[2]), g.dtype) for g in gs],
        scratch_shapes=[pltpu.SemaphoreType.DMA((n,)), pltpu.SemaphoreType.DMA((n,))],
    )(*gs)


def _chips_copies(ins, outs, send_sems, recv_sems):
    x, y, c = _mesh_pos()
    cps = []
    for a in range(len(ins)):
        for j, (dx, dy) in enumerate(CHIP_COMBOS):
            px, py = _flip(x, dx), _flip(y, dy)
            cps.append(pltpu.make_async_remote_copy(
                src_ref=ins[a].at[2 * px + py], dst_ref=outs[a].at[j], send_sem=send_sems.at[a, j],
                recv_sem=recv_sems.at[a, j], device_id=(px, py, c), device_id_type=MESH))
    return cps


def _chips_out_shapes(ps):
    return [SDS((3,) + p.shape[1:], p.dtype) for p in ps]


def _chips_sems(n):
    return [pltpu.SemaphoreType.DMA((n, 3)), pltpu.SemaphoreType.DMA((n, 3))]


def _share_halves(qs, tag):
    n = len(qs)

    def body(*refs):
        ins, outs = refs[:n], refs[n:2 * n]
        send_sems, recv_sems = refs[2 * n:]
        x, y, c = _mesh_pos()
        cps = []
        for a in range(n):
            cp = pltpu.make_async_remote_copy(
                src_ref=ins[a], dst_ref=outs[a], send_sem=send_sems.at[a], recv_sem=recv_sems.at[a],
                device_id=(x, y, 1 - c), device_id_type=MESH)
            cp.start()
            cps.append(cp)
        for cp in cps:
            cp.wait()

    return pl.pallas_call(
        body, name=f"rs_share_halves_{tag}", in_specs=[HBM_SPEC] * n, out_specs=[HBM_SPEC] * n,
        out_shape=[SDS(q.shape, q.dtype) for q in qs],
        scratch_shapes=[pltpu.SemaphoreType.DMA((n,)), pltpu.SemaphoreType.DMA((n,))],
    )(*qs)


def _row_tile(rows, cap):
    return max(t for t in range(16, cap + 1, 16) if rows % t == 0)


def _add_own_half(g, recv, c, name):
    S, R, C = g.shape
    h = R // 2
    tr = _row_tile(h, 256)
    nt = h // tr

    def body(c_ref, g_ref, r_ref, o_ref, o16_ref):
        del c_ref
        s = g_ref[...] + r_ref[...]
        o_ref[...] = s
        o16_ref[...] = s.astype(bf16)

    ospec = pl.BlockSpec((1, tr, C), lambda s, i, cr: (s, i, 0))
    return pl.pallas_call(
        body, name=name,
        grid_spec=pltpu.PrefetchScalarGridSpec(
            num_scalar_prefetch=1, grid=(S, nt),
            in_specs=[pl.BlockSpec((1, tr, C), lambda s, i, cr: (s, cr[0] * nt + i, 0)), ospec],
            out_specs=[ospec, ospec]),
        out_shape=[SDS((S, h, C), f32), SDS((S, h, C), bf16)],
        compiler_params=_params(("parallel", "parallel"), 40),
    )(c, g, recv)


def _add_chip_parts(p, recv, kc, name):
    _, R, C = p.shape
    tr = _row_tile(R, 256)

    def body(k_ref, p_ref, r0, r1, r2, o_ref):
        del k_ref
        o_ref[...] = ((p_ref[0] + r0[0].astype(f32)) + r1[0].astype(f32)) + r2[0].astype(f32)

    rspec = lambda j: pl.BlockSpec((1, tr, C), lambda i, kr: (j, i, 0))
    return pl.pallas_call(
        body, name=name,
        grid_spec=pltpu.PrefetchScalarGridSpec(
            num_scalar_prefetch=1, grid=(R // tr,),
            in_specs=[pl.BlockSpec((1, tr, C), lambda i, kr: (kr[0], i, 0)), rspec(0), rspec(1), rspec(2)],
            out_specs=pl.BlockSpec((tr, C), lambda i, kr: (i, 0))),
        out_shape=SDS((R, C), f32), compiler_params=_params(("parallel",), 40),
    )(kc, p, recv, recv, recv)


def _rs_pair(gs, c, tag):
    got = _swap_halves(gs, tag)
    pairs = [_add_own_half(g, r, c, f"rs_add_pair_{tag}_{a}") for a, (g, r) in enumerate(zip(gs, got))]
    return [p[0] for p in pairs], [p[1] for p in pairs]


def _rs_finish(ps, arrived, c, kc, tag):
    qs = [_add_chip_parts(p, r, kc, f"rs_add_chips_{tag}_{a}") for a, (p, r) in enumerate(zip(ps, arrived))]
    return qs, _share_halves(qs, tag)


def _adamw_math(w, g, m, v):
    m2 = ADAM_B1 * m + (1.0 - ADAM_B1) * g
    v2 = ADAM_B2 * v + (1.0 - ADAM_B2) * (g * g)
    m_hat = m2 / (1.0 - ADAM_B1 ** ADAM_STEP)
    v_hat = v2 / (1.0 - ADAM_B2 ** ADAM_STEP)
    return -ADAM_LR * (m_hat / (jnp.sqrt(v_hat) + ADAM_EPS) + ADAM_WD * w), m2, v2


def _adamw_shard(w, g_own, g_other, m, v, c, name):
    L, R, C = w.shape
    h = R // 2
    tr = _row_tile(h, 208)
    nh = h // tr

    def body(c_ref, w_ref, go0, gx0, go1, gx1, m_ref, v_ref, g_o, d_o, m_o, v_o):
        mine = (pl.program_id(1) // nh) == c_ref[0]
        first = pl.program_id(0) == 0
        g = jnp.where(mine, jnp.where(first, go0[...], go1[...]), jnp.where(first, gx0[...], gx1[...]))[None]
        d, m2, v2 = _adamw_math(w_ref[...], g, m_ref[...], v_ref[...])
        g_o[...], d_o[...], m_o[...], v_o[...] = g, d, m2, v2

    full = pl.BlockSpec((1, tr, C), lambda l, i, cr: (l, i, 0))
    half = pl.BlockSpec((tr, C), lambda l, i, cr: (i % nh, 0))
    return pl.pallas_call(
        body, name=name,
        grid_spec=pltpu.PrefetchScalarGridSpec(num_scalar_prefetch=1, grid=(L, R // tr),
                                               in_specs=[full, half, half, half, half, full, full],
                                               out_specs=[full] * 4),
        out_shape=[SDS((L, R, C), f32)] * 4, compiler_params=_params(("parallel", "parallel"), 48),
    )(c, w, g_own[0], g_other[0], g_own[1], g_other[1], m, v)


def _adamw(w, g, m, v, name):
    R, C = w.shape
    tr = 256 if R % 256 == 0 else R

    def body(w_ref, g_ref, m_ref, v_ref, d_o, m_o, v_o):
        d_o[...], m_o[...], v_o[...] = _adamw_math(w_ref[...], g_ref[...], m_ref[...], v_ref[...])

    blk = pl.BlockSpec((tr, C), lambda i: (i, 0))
    return pl.pallas_call(body, name=name, grid=(R // tr,), in_specs=[blk] * 4, out_specs=[blk] * 3,
                          out_shape=[SDS((R, C), f32)] * 3, compiler_params=_params(("parallel",), 40))(w, g, m, v)


def _pack(arrs):
    flat = [a.reshape(-1, LANES) for a in arrs]
    rows = sum(f.shape[0] for f in flat)
    pad = (-rows) % 16
    if pad:
        flat.append(jnp.zeros((pad, LANES), f32))
    return jnp.concatenate(flat, axis=0)


def _unpack(buf, shapes):
    out, r = [], 0
    for s in shapes:
        n = math.prod(s) // LANES
        out.append(buf[r:r + n].reshape(s))
        r += n
    return out


REPLICATED = ("shift_mu", "w_decay0", "a0", "k_k", "k_a", "r_k", "ln_x_w", "ln_x_b", "v_mix0", "lb_logits", "g_norm_w",
              "ln_w", "ln_b")
CHIP_SMALL = (("w_decay_up", 2), ("a_up", 2), ("v_mix_down", 1), ("v_mix_up", 2))
WEIGHTS = ("w_in", "shift_mu", "w_decay0", "w_decay_up", "a0", "a_up", "k_k", "k_a", "r_k", "ln_x_w", "ln_x_b",
           "v_mix0", "v_mix_down", "v_mix_up", "lb_logits", "g_norm_w", "w_out", "ln_w", "ln_b")


def kernel(x, w_in, shift_mu, w_decay0, w_decay_up, a0, a_up, k_k, k_a, r_k, ln_x_w, ln_x_b, v_mix0, v_mix_down, v_mix_up, lb_logits, g_norm_w, w_out, ln_w, ln_b, loss_target, m_w_in, m_shift_mu, m_w_decay0, m_w_decay_up, m_a0, m_a_up, m_k_k, m_k_a, m_r_k, m_ln_x_w, m_ln_x_b, m_v_mix0, m_v_mix_down, m_v_mix_up, m_lb_logits, m_g_norm_w, m_w_out, m_ln_w, m_ln_b, v_w_in, v_shift_mu, v_w_decay0, v_w_decay_up, v_a0, v_a_up, v_k_k, v_k_a, v_r_k, v_ln_x_w, v_ln_x_b, v_v_mix0, v_v_mix_down, v_v_mix_up, v_lb_logits, v_g_norm_w, v_w_out, v_ln_w, v_ln_b):
    W = dict(w_in=w_in, shift_mu=shift_mu, w_decay0=w_decay0, w_decay_up=w_decay_up, a0=a0, a_up=a_up, k_k=k_k, k_a=k_a,
             r_k=r_k, ln_x_w=ln_x_w, ln_x_b=ln_x_b, v_mix0=v_mix0, v_mix_down=v_mix_down, v_mix_up=v_mix_up,
             lb_logits=lb_logits, g_norm_w=g_norm_w, w_out=w_out, ln_w=ln_w, ln_b=ln_b)
    M = dict(w_in=m_w_in, shift_mu=m_shift_mu, w_decay0=m_w_decay0, w_decay_up=m_w_decay_up, a0=m_a0, a_up=m_a_up,
             k_k=m_k_k, k_a=m_k_a, r_k=m_r_k, ln_x_w=m_ln_x_w, ln_x_b=m_ln_x_b, v_mix0=m_v_mix0,
             v_mix_down=m_v_mix_down, v_mix_up=m_v_mix_up, lb_logits=m_lb_logits, g_norm_w=m_g_norm_w, w_out=m_w_out,
             ln_w=m_ln_w, ln_b=m_ln_b)
    V = dict(w_in=v_w_in, shift_mu=v_shift_mu, w_decay0=v_w_decay0, w_decay_up=v_w_decay_up, a0=v_a0, a_up=v_a_up,
             k_k=v_k_k, k_a=v_k_a, r_k=v_r_k, ln_x_w=v_ln_x_w, ln_x_b=v_ln_x_b, v_mix0=v_v_mix0,
             v_mix_down=v_v_mix_down, v_mix_up=v_v_mix_up, lb_logits=v_lb_logits, g_norm_w=v_g_norm_w, w_out=v_w_out,
             ln_w=v_ln_w, ln_b=v_ln_b)
    mx, my, mc = _mesh_pos()
    kc = (2 * mx + my).astype(jnp.int32)
    c_arr = jnp.reshape(mc.astype(jnp.int32), (1,))
    kc_arr = jnp.reshape(kc, (1,))

    small_shard = _pack([W[n] for n, _ in CHIP_SMALL])
    tr_in = lambda a: jnp.swapaxes(a, 1, 2)
    in_halves = tr_in(w_in).astype(bf16).reshape(DEPTH, 2, SHARD_COLS // 2, D_MODEL)
    out_halves = jnp.swapaxes(w_out.astype(bf16).reshape(DEPTH, 2, D_MODEL // 8, D_MODEL), 0, 1)
    g_in0, g_small = _gather_shards([in_halves[0], small_shard.reshape(2, -1, LANES)])
    in_full = lambda g: g.reshape(IN_COLS, D_MODEL)
    shard_shapes = [W[n].shape for n, _ in CHIP_SMALL]
    g_small = g_small.reshape(4, -1, LANES)
    per_chip = [_unpack(g_small[k], shard_shapes) for k in range(4)]
    full_small = {n: jnp.concatenate([per_chip[k][i] for k in range(4)], axis=ax)
                  for i, (n, ax) in enumerate(CHIP_SMALL)}

    def assemble_later(gathered):
        g_in1, g_out = gathered
        return in_full(g_in1), jnp.transpose(g_out, (2, 0, 1, 3, 4)).reshape(DEPTH, D_MODEL, D_MODEL)

    in_slabs = lambda g: g.reshape(4, SHARD_COLS, D_MODEL)
    out_slabs = lambda g: g.reshape(4, D_MODEL // 4, D_MODEL)
    pairs, arrivals = {}, {}

    def on_grads(l, dw_in, dw_out):
        pairs[l], to_send = _rs_pair([in_slabs(dw_in), out_slabs(dw_out)], c_arr, f"l{l}")
        return to_send

    def on_recv(l, arrived):
        arrivals[l] = arrived

    small_names = list(REPLICATED) + [n for n, _ in CHIP_SMALL]

    def on_small_grads(g):
        return _allreduce_part(_pack([g[n] for n in small_names]))

    loss_sum, gx, G, (small_sum,) = _device_step(
        x[0], loss_target[0], in_full(g_in0), [in_halves[1], out_halves], assemble_later, on_grads, on_recv,
        on_small_grads, shift_mu, w_decay0, full_small["w_decay_up"], a0, full_small["a_up"], k_k, k_a, r_k, ln_x_w,
        ln_x_b, v_mix0, full_small["v_mix_down"], full_small["v_mix_up"], lb_logits, g_norm_w, ln_w, ln_b)
    loss = lax.psum(loss_sum[0, 0], ("x", "y", "c"))

    own1, other1 = _rs_finish(pairs[1], arrivals[1], c_arr, kc_arr, "l1")
    own0, other0 = _rs_finish(pairs[0], arrivals[0], c_arr, kc_arr, "l0")
    gin_own, gout_own = [own0[0], own1[0]], [own0[1], own1[1]]
    gin_other, gout_other = [other0[0], other1[0]], [other0[1], other1[1]]

    small_full = dict(zip(small_names, _unpack(small_sum, [G[n].shape for n in small_names])))
    grads = {n: small_full[n] for n in REPLICATED}
    for n, ax in CHIP_SMALL:
        width = W[n].shape[ax]
        grads[n] = lax.dynamic_slice_in_dim(small_full[n], kc * width, width, axis=ax)

    delta, new_m, new_v = {}, {}, {}
    res = _adamw_shard(tr_in(w_in), gin_own, gin_other, tr_in(m_w_in), tr_in(v_w_in), c_arr, "adamw_w_in")
    grads["w_in"], delta["w_in"], new_m["w_in"], new_v["w_in"] = [tr_in(a) for a in res]
    grads["w_out"], delta["w_out"], new_m["w_out"], new_v["w_out"] = _adamw_shard(
        w_out, gout_own, gout_other, m_w_out, v_w_out, c_arr, "adamw_w_out")
    shapes = [W[n].shape for n in small_names]
    d_, m_, v_ = _adamw(_pack([W[n] for n in small_names]), _pack([grads[n] for n in small_names]),
                        _pack([M[n] for n in small_names]), _pack([V[n] for n in small_names]), "adamw_small")
    for n, d1, m1, v1 in zip(small_names, _unpack(d_, shapes), _unpack(m_, shapes), _unpack(v_, shapes)):
        delta[n], new_m[n], new_v[n] = d1, m1, v1

    return (loss, gx[None], *[grads[n] for n in WEIGHTS], *[delta[n] for n in WEIGHTS],
            *[new_m[n] for n in WEIGHTS], *[new_v[n] for n in WEIGHTS])
```

```python
import functools
import math

import jax
import jax.numpy as jnp
from jax import lax
from jax.experimental import pallas as pl
from jax.experimental.pallas import tpu as pltpu

f32 = jnp.float32
bf16 = jnp.bfloat16
SDS = jax.ShapeDtypeStruct
MESH = pl.DeviceIdType.MESH

D_MODEL = 2048
DEPTH = 2
D_RWKV = 1024
D_HGRN = 1024
RWKV_HEAD = 64
RWKV_HEADS = 16
LORA = 64
VRES = 32
HG_HEAD = 128
HG_HEADS = 8
CHUNK = 64
RWKV_COLS = 4 * D_RWKV + 2 * LORA
IN_COLS = RWKV_COLS + 4 * D_HGRN
SHARD_COLS = IN_COLS // 4
LANES = 128
RW_BLKS = RWKV_COLS // LANES
ALPHA = (2 * DEPTH) ** 0.25
LN_EPS = 1e-5
GN_EPS = 64e-5
RMS_EPS = 1e-5
LB_FLOOR = 1e-30
EXP_M05 = math.exp(-0.5)
ADAM_LR, ADAM_B1, ADAM_B2, ADAM_EPS, ADAM_WD, ADAM_STEP = 0.001, 0.9, 0.999, 1e-08, 0.01, 10

TB = 128
SCAN_TB = 8
SCAN_PARTS = 2
HSUB = 32
HIGHEST = lax.Precision.HIGHEST


def _params(sem=None, vmem_mb=None):
    kw = {}
    if sem is not None:
        kw["dimension_semantics"] = sem
    if vmem_mb is not None:
        kw["vmem_limit_bytes"] = vmem_mb << 20
    return pltpu.CompilerParams(**kw)


def _split(x, n):
    parts, r = [], x
    for i in range(n):
        p = r.astype(bf16)
        parts.append(p)
        if i + 1 < n:
            r = r - p.astype(f32)
    return parts


def _dot_parts(x, m, n=3):
    R = x.shape[0]
    res = jnp.dot(jnp.concatenate(_split(x, n), axis=0), m, preferred_element_type=f32)
    acc = res[0:R]
    for i in range(1, n):
        acc = acc + res[i * R:(i + 1) * R]
    return acc


def _seg_raw(x, ee, n=3, per_tile=False):
    R = x.shape[0]
    if per_tile:
        outs = []
        for c in range(4):
            tile_parts = _split(x[:, 256 * c:256 * c + 256], n)
            res = jnp.dot(jnp.concatenate(tile_parts, axis=0), ee, preferred_element_type=f32)
            acc = res[0:R]
            for i in range(1, n):
                acc = acc + res[i * R:(i + 1) * R]
            outs.append(acc)
        return jnp.concatenate(outs, axis=1)
    parts = _split(x, n)
    lhs = jnp.concatenate([p[:, 256 * c:256 * c + 256] for p in parts for c in range(4)], axis=0)
    res = jnp.dot(lhs, ee, preferred_element_type=f32)

    def tile(c):
        acc = res[c * R:(c + 1) * R]
        for i in range(1, n):
            acc = acc + res[(4 * i + c) * R:(4 * i + c + 1) * R]
        return acc

    return jnp.concatenate([tile(c) for c in range(4)], axis=1)


@jax.custom_vjp
def _seg(x, ee):
    return _seg_raw(x, ee)


def _seg_fwd(x, ee):
    return _seg_raw(x, ee), ee


def _seg_bwd(ee, g):
    return _seg_raw(g, ee), jnp.zeros_like(ee)


_seg.defvjp(_seg_fwd, _seg_bwd)


def _const_mats():
    i256 = jnp.arange(256) // RWKV_HEAD
    ee = (i256[:, None] == i256[None, :]).astype(bf16)
    head = jnp.arange(D_RWKV) // RWKV_HEAD
    lane = jnp.arange(LANES)
    red = (head[:, None] == lane[None, :]).astype(bf16)
    exp = ((lane[:, None] % RWKV_HEADS == head[None, :]) & (lane[:, None] < 2 * RWKV_HEADS)).astype(bf16)
    return ee, red, exp


def _matmul(a, b, *, mode, tm, tn, tk, name, add=None, b_layer=None, rider=None):
    bs = b.shape if b_layer is None else b.shape[1:]
    lead = () if b_layer is None else (None,)
    bidx = (lambda *t: t) if b_layer is None else (lambda *t: (b_layer,) + t)
    if mode == "nn":
        (M, K), N = a.shape, bs[1]
        tm, tn, tk = min(tm, M), min(tn, N), min(tk, K)
        a_spec = pl.BlockSpec((tm, tk), lambda i, j, k: (i, k))
        b_spec = pl.BlockSpec(lead + (tk, tn), lambda i, j, k: bidx(k, j))
        dims = (((1,), (0,)), ((), ()))
    elif mode == "nt":
        (M, K), N = a.shape, bs[0]
        tm, tn, tk = min(tm, M), min(tn, N), min(tk, K)
        a_spec = pl.BlockSpec((tm, tk), lambda i, j, k: (i, k))
        b_spec = pl.BlockSpec(lead + (tn, tk), lambda i, j, k: bidx(j, k))
        dims = (((1,), (1,)), ((), ()))
    else:
        (K, M), N = a.shape, bs[1]
        tm, tn, tk = min(tm, M), min(tn, N), min(tk, K)
        a_spec = pl.BlockSpec((tk, tm), lambda i, j, k: (k, i))
        b_spec = pl.BlockSpec(lead + (tk, tn), lambda i, j, k: bidx(k, j))
        dims = (((0,), (0,)), ((), ()))
    assert M % tm == 0 and N % tn == 0 and K % tk == 0, (M, N, K, tm, tn, tk)
    has_add = add is not None
    n_in = 3 if has_add else 2
    grid = (M // tm, N // tn, K // tk)
    rd = rider if rider is not None else _Part([], [], [], [], [], lambda *a: None)
    rd.total = grid[0] * grid[1] * grid[2]

    def body(*refs):
        a_ref, b_ref = refs[:2]
        c_ref = refs[2] if has_add else None
        rd_ins, o_ref = refs[n_in:n_in + rd.n_in], refs[n_in + rd.n_in]
        rest = refs[n_in + rd.n_in + 1:]
        i, j, k = pl.program_id(0), pl.program_id(1), pl.program_id(2)
        rd.step((i * grid[1] + j) * grid[2] + k, rd_ins, rest[:rd.n_out], rest[rd.n_out:])
        p = lax.dot_general(a_ref[...].astype(bf16), b_ref[...].astype(bf16), dims, preferred_element_type=f32)

        @pl.when(k == 0)
        def _():
            o_ref[...] = p + c_ref[...] if has_add else p

        @pl.when(k > 0)
        def _():
            o_ref[...] += p

    o_spec = pl.BlockSpec((tm, tn), lambda i, j, k: (i, j))
    in_specs = [a_spec, b_spec] + ([o_spec] if has_add else []) + rd.in_specs
    args = (a, b) + ((add,) if has_add else ()) + tuple(rd.args)
    outs = pl.pallas_call(
        body, name=name, grid=grid, in_specs=in_specs, out_specs=[o_spec] + rd.out_specs,
        out_shape=[SDS((M, N), f32)] + rd.out_shape, scratch_shapes=rd.scratch,
        compiler_params=_params(("parallel", "parallel", "arbitrary") if rider is None else ("arbitrary",) * 3, 60),
    )(*args)
    return outs[0] if rider is None else (outs[0], list(outs[1:]))


def _rwkv_core(k_in, v_in, wdad, w0, wup, a0, aup, kkw, kaw, ee, vmix):
    w_raw = w0 + jnp.dot(jnp.tanh(wdad), wup, preferred_element_type=f32)
    decay = jnp.exp(-EXP_M05 * jax.nn.sigmoid(w_raw))
    a = jax.nn.sigmoid(a0 + jnp.dot(wdad, aup, preferred_element_type=f32))
    if vmix is None:
        v = v_in
    else:
        v0, vdn, vup, vfirst = vmix
        gate = jax.nn.sigmoid(v0 + jnp.dot(jnp.dot(v_in, vdn, preferred_element_type=f32), vup,
                                           preferred_element_type=f32))
        v = v_in + (vfirst - v_in) * gate
    kk = k_in * kkw
    n2 = _seg(kk * kk, ee)
    kk = kk / jnp.maximum(jnp.sqrt(n2), 1e-12)
    k2 = k_in * (1.0 + (a - 1.0) * kaw)
    return decay, k2, v, -kk, kk * a


def _shifted(y, prev_row):
    yp = pltpu.roll(y, 1, 0)
    row = lax.broadcasted_iota(jnp.int32, y.shape, 0)
    return jnp.where(row == 0, prev_row, yp)


def _rwkv_pre_fwd(proj, mu, w0, wup, a0, aup, kkw, kaw, ee, vmix, layer):
    T = proj.shape[0]
    nb = T // TB
    has_mix = vmix is not None

    def body(*refs):
        y_ref, p8_ref, mu_ref, w0_ref, wup_ref, a0_ref, aup_ref, kkw_ref, kaw_ref, ee_ref = refs[:10]
        rest = refs[10:]
        if has_mix:
            v0_ref, vdn_ref, vup_ref, vf_ref = rest[:4]
            rest = rest[4:]
        r_o, w_o, k_o, v_o, a_o, b_o, z_o = rest
        i = pl.program_id(0)
        y = y_ref[...]
        prev = jnp.where(i == 0, 0.0, p8_ref[7:8, :])
        rw = y + mu_ref[...] * (_shifted(y, prev) - y)
        mix = (v0_ref[...], vdn_ref[...], vup_ref[...], vf_ref[...]) if has_mix else None
        dec, k2, v, av, bv = _rwkv_core(rw[:, 1024:2048], rw[:, 2048:3072], rw[:, 4096:4224], w0_ref[...],
                                        wup_ref[...], a0_ref[...], aup_ref[...], kkw_ref[...], kaw_ref[...],
                                        ee_ref[...], mix)
        r_o[...] = rw[:, 0:1024]
        w_o[...] = dec
        k_o[...] = k2
        v_o[...] = v
        a_o[...] = av
        b_o[...] = bv
        z_o[...] = rw[:, 3072:4096]

    row = lambda n: pl.BlockSpec((1, n), lambda i: (0, 0))
    full = lambda s: pl.BlockSpec(s, lambda i: (0,) * len(s))
    blk = pl.BlockSpec((TB, D_RWKV), lambda i: (i, 0))
    in_specs = [pl.BlockSpec((TB, RWKV_COLS), lambda i: (i, 0)),
                pl.BlockSpec((8, RWKV_COLS), lambda i: (jnp.maximum(i * (TB // 8) - 1, 0), 0)),
                row(RWKV_COLS), row(D_RWKV), full((LANES, D_RWKV)), row(D_RWKV), full((LANES, D_RWKV)),
                row(D_RWKV), row(D_RWKV), full((256, 256))]
    args = [proj, proj, mu, w0, wup, a0, aup, kkw, kaw, ee]
    if has_mix:
        v0, vdn, vup, vfirst = vmix
        in_specs += [row(D_RWKV), full((D_RWKV, LANES)), full((LANES, D_RWKV)), blk]
        args += [v0, vdn, vup, vfirst]
    return pl.pallas_call(
        body, name=f"rwkv_pre_fwd_l{layer}", grid=(nb,), in_specs=in_specs, out_specs=[blk] * 7,
        out_shape=[SDS((T, D_RWKV), f32)] * 7, compiler_params=_params(("parallel",), 48),
    )(*args)


def _rwkv_pre_bwd(proj, mu, w0, wup, a0, aup, kkw, kaw, ee, vmix, cots, layer):
    T = proj.shape[0]
    nb = T // TB
    has_mix = vmix is not None
    cot_list = cots["r"] + [cots["w"]] + cots["k"] + cots["v"] + [cots["a"], cots["b"], cots["z"]]
    n_r, n_k, n_v = len(cots["r"]), len(cots["k"]), len(cots["v"])
    n_cot = len(cot_list)

    def body(*refs):
        y_ref, p8_ref, mu_ref, w0_ref, wup_ref, a0_ref, aup_ref, kkw_ref, kaw_ref, ee_ref = refs[:10]
        rest = refs[10:]
        if has_mix:
            v0_ref, vdn_ref, vup_ref, vf_ref = rest[:4]
            rest = rest[4:]
        cot_refs, rest = rest[:n_cot], rest[n_cot:]
        if has_mix:
            (dproj_o, dmu_o, dw0_o, dwup_o, da0_o, daup_o, dkkw_o, dkaw_o,
             dv0_o, dvdn_o, dvup_o, dvf_o, carry) = rest
        else:
            dproj_o, dmu_o, dw0_o, dwup_o, da0_o, daup_o, dkkw_o, dkaw_o, carry = rest
        i = pl.program_id(0)
        blk_i = nb - 1 - i
        y = y_ref[...]
        prev = jnp.where(blk_i == 0, 0.0, p8_ref[7:8, :])
        yp = _shifted(y, prev)
        mu_v = mu_ref[...]
        rw = y + mu_v * (yp - y)

        def sum_refs(rs):
            acc = rs[0][...]
            for r_ in rs[1:]:
                acc = acc + r_[...]
            return acc

        c = list(cot_refs)
        g_r = sum_refs(c[:n_r]); c = c[n_r:]
        g_w = c[0][...]; c = c[1:]
        g_k = sum_refs(c[:n_k]); c = c[n_k:]
        g_v = sum_refs(c[:n_v]); c = c[n_v:]
        g_a, g_b, g_z = c[0][...], c[1][...], c[2][...]

        ee_v = ee_ref[...]
        if has_mix:
            def fn(k_in, v_in, wdad, w0_, wup_, a0_, aup_, kkw_, kaw_, v0_, vdn_, vup_, vf_):
                return _rwkv_core(k_in, v_in, wdad, w0_, wup_, a0_, aup_, kkw_, kaw_, ee_v, (v0_, vdn_, vup_, vf_))
            prim = (rw[:, 1024:2048], rw[:, 2048:3072], rw[:, 4096:4224], w0_ref[...], wup_ref[...], a0_ref[...],
                    aup_ref[...], kkw_ref[...], kaw_ref[...], v0_ref[...], vdn_ref[...], vup_ref[...], vf_ref[...])
        else:
            def fn(k_in, v_in, wdad, w0_, wup_, a0_, aup_, kkw_, kaw_):
                return _rwkv_core(k_in, v_in, wdad, w0_, wup_, a0_, aup_, kkw_, kaw_, ee_v, None)
            prim = (rw[:, 1024:2048], rw[:, 2048:3072], rw[:, 4096:4224], w0_ref[...], wup_ref[...], a0_ref[...],
                    aup_ref[...], kkw_ref[...], kaw_ref[...])
        _, vjp = jax.vjp(fn, *prim)
        gs = vjp((g_w, g_k, g_v, g_a, g_b))
        d_k, d_v, d_wdad = gs[0], gs[1], gs[2]
        g_rw = jnp.concatenate([g_r, d_k, d_v, g_z, d_wdad], axis=1)

        @pl.when(i == 0)
        def _():
            carry[...] = jnp.zeros_like(carry)
            dmu_o[...] = jnp.zeros_like(dmu_o)
            dw0_o[...] = jnp.zeros_like(dw0_o)
            dwup_o[...] = jnp.zeros_like(dwup_o)
            da0_o[...] = jnp.zeros_like(da0_o)
            daup_o[...] = jnp.zeros_like(daup_o)
            dkkw_o[...] = jnp.zeros_like(dkkw_o)
            dkaw_o[...] = jnp.zeros_like(dkaw_o)
            if has_mix:
                dv0_o[...] = jnp.zeros_like(dv0_o)
                dvdn_o[...] = jnp.zeros_like(dvdn_o)
                dvup_o[...] = jnp.zeros_like(dvup_o)

        dmu_o[...] += jnp.sum(g_rw * (yp - y), axis=0, keepdims=True)
        dw0_o[...] += gs[3]
        dwup_o[...] += gs[4]
        da0_o[...] += gs[5]
        daup_o[...] += gs[6]
        dkkw_o[...] += gs[7]
        dkaw_o[...] += gs[8]
        if has_mix:
            dv0_o[...] += gs[9]
            dvdn_o[...] += gs[10]
            dvup_o[...] += gs[11]
            dvf_o[...] = gs[12]
        gm = g_rw * mu_v
        nxt = pltpu.roll(gm, TB - 1, 0)
        rowi = lax.broadcasted_iota(jnp.int32, gm.shape, 0)
        nxt = jnp.where(rowi == TB - 1, carry[...], nxt)
        dproj_o[...] = g_rw - gm + nxt
        carry[...] = gm[0:1, :]

    rev = lambda i: (nb - 1 - i, 0)
    row = lambda n: pl.BlockSpec((1, n), lambda i: (0, 0))
    full = lambda s: pl.BlockSpec(s, lambda i: (0,) * len(s))
    blk = pl.BlockSpec((TB, D_RWKV), rev)
    in_specs = [pl.BlockSpec((TB, RWKV_COLS), rev),
                pl.BlockSpec((8, RWKV_COLS), lambda i: (jnp.maximum((nb - 1 - i) * (TB // 8) - 1, 0), 0)),
                row(RWKV_COLS), row(D_RWKV), full((LANES, D_RWKV)), row(D_RWKV), full((LANES, D_RWKV)),
                row(D_RWKV), row(D_RWKV), full((256, 256))]
    args = [proj, proj, mu, w0, wup, a0, aup, kkw, kaw, ee]
    out_specs = [pl.BlockSpec((TB, RWKV_COLS), rev), row(RWKV_COLS), row(D_RWKV), full((LANES, D_RWKV)),
                 row(D_RWKV), full((LANES, D_RWKV)), row(D_RWKV), row(D_RWKV)]
    out_shape = [SDS((T, RWKV_COLS), f32), SDS((1, RWKV_COLS), f32), SDS((1, D_RWKV), f32), SDS((LANES, D_RWKV), f32),
                 SDS((1, D_RWKV), f32), SDS((LANES, D_RWKV), f32), SDS((1, D_RWKV), f32), SDS((1, D_RWKV), f32)]
    if has_mix:
        v0, vdn, vup, vfirst = vmix
        in_specs += [row(D_RWKV), full((D_RWKV, LANES)), full((LANES, D_RWKV)), blk]
        args += [v0, vdn, vup, vfirst]
        out_specs += [row(D_RWKV), full((D_RWKV, LANES)), full((LANES, D_RWKV)), blk]
        out_shape += [SDS((1, D_RWKV), f32), SDS((D_RWKV, LANES), f32), SDS((LANES, D_RWKV), f32),
                      SDS((T, D_RWKV), f32)]
    in_specs += [blk] * n_cot
    args += cot_list
    return pl.pallas_call(
        body, name=f"rwkv_pre_bwd_l{layer}", grid=(nb,), in_specs=in_specs, out_specs=out_specs, out_shape=out_shape,
        scratch_shapes=[pltpu.VMEM((1, RWKV_COLS), f32)],
        compiler_params=_params(("arbitrary",), 60),
    )(*args)


def _pack_cols(prod_ref, red):
    res = jnp.dot(prod_ref[...].astype(bf16), red, preferred_element_type=f32)
    out = res[0:RWKV_HEAD]
    for s in range(1, SCAN_TB):
        out = out + pltpu.roll(res[s * RWKV_HEAD:(s + 1) * RWKV_HEAD], RWKV_HEADS * s, 1)
    return out


def _unpack_cols(tile_ref, exp):
    rows = []
    for s in range(SCAN_TB):
        half = tile_ref[0, :, pl.ds((s // 4) * LANES, LANES)]
        rows.append(half if s % 4 == 0 else pltpu.roll(half, LANES - 2 * RWKV_HEADS * (s % 4), 1))
    return jnp.dot(jnp.concatenate(rows, axis=0), exp, preferred_element_type=f32)


def _to_col(v):
    T = v.shape[0]
    hi = lax.reduce_precision(v, 8, 7)
    parts = jnp.stack([hi, v - hi], axis=1).astype(bf16)
    parts = parts.reshape(T // SCAN_TB, SCAN_TB, 2, RWKV_HEADS, RWKV_HEAD)
    return jnp.transpose(parts, (0, 4, 1, 2, 3)).reshape(T // SCAN_TB, RWKV_HEAD, 2 * LANES)


def _from_col(vc):
    nb = vc.shape[0]
    tiles = vc.reshape(nb, RWKV_HEAD, SCAN_TB, RWKV_HEADS)
    return jnp.transpose(tiles, (0, 2, 3, 1)).reshape(nb * SCAN_TB, D_RWKV)


def _rwkv_scan_fwd(r, w, k, vcol, a, b, ee, e16, e16t, layer, rider, gather=()):
    T = r.shape[0]
    nb = T // SCAN_TB
    ng = len(gather)

    def body(*refs):
        r_ref, w_ref, k_ref, v_ref, a_ref, an_ref, b_ref, ee_ref, red_ref, exp_ref = refs[:10]
        rd_ins, refs = refs[10:10 + rider.n_in], refs[10 + rider.n_in:]
        g_ins, refs = refs[:ng], refs[ng:]
        o_ref, sp_ref = refs[:2]
        rd_outs, refs = refs[2:2 + rider.n_out], refs[2 + rider.n_out:]
        g_outs, refs = refs[:ng], refs[ng:]
        s_ref, sa_ref, vb_ref, po_ref = refs[:4]
        rd_scr, g_sems = refs[4:4 + rider.n_scr], refs[4 + rider.n_scr:]
        step = pl.program_id(0)
        rider.step(step, rd_ins, rd_outs, rd_scr)

        @pl.when(step == 0)
        def _():
            s_ref[...] = jnp.zeros_like(s_ref)
            sa_ref[...] = jnp.zeros_like(sa_ref)
            if ng:
                _gather_start(g_ins, g_outs, *g_sems)

        if ng:
            @pl.when(step == (3 * nb) // 4)
            def _():
                _gather_forward(g_ins, g_outs, *g_sems)

        ee_v = ee_ref[...]
        r_b, w_b, k_b, a_b, b_b = r_ref[...], w_ref[...], k_ref[...], a_ref[...], b_ref[...]
        rowi = lax.broadcasted_iota(jnp.int32, a_b.shape, 0)
        a_nx = jnp.where(rowi == SCAN_TB - 1, an_ref[0:1, :], pltpu.roll(a_b, SCAN_TB - 1, 0))
        wa = w_b * a_nx
        dots = _seg_raw(jnp.concatenate([b_b * a_nx, k_b * a_nx], axis=0), ee_v, 3)
        beta, kappa = dots[0:SCAN_TB], dots[SCAN_TB:2 * SCAN_TB]
        vb_ref[...] = _unpack_cols(v_ref, exp_ref[...])

        for t in range(SCAN_TB):
            row = lambda x: x[t:t + 1, :]
            blk = pl.ds(t * RWKV_HEAD, RWKV_HEAD)
            sp = s_ref[...]
            sa = sa_ref[...]
            sp_ref[t, 0] = sp
            sp_ref[t, 1] = sa
            vb = vb_ref[blk, :]
            ahead = _seg_raw(sp * row(wa), ee_v, SCAN_PARTS, per_tile=True)
            sn = sp * row(w_b) + sa * row(b_b) + vb * row(k_b)
            s_ref[...] = sn
            sa_ref[...] = ahead + sa * row(beta) + vb * row(kappa)
            po_ref[blk, :] = sn * row(r_b)
        o_ref[0] = _pack_cols(po_ref, red_ref[...])

        if ng:
            @pl.when(step == nb - 1)
            def _():
                _gather_finish(g_ins, g_outs, *g_sems)

    rows = pl.BlockSpec((SCAN_TB, D_RWKV), lambda i: (i, 0))
    rows_next = pl.BlockSpec((SCAN_TB, D_RWKV), lambda i: (jnp.minimum(i + 1, nb - 1), 0))
    col = pl.BlockSpec((1, RWKV_HEAD, LANES), lambda i: (i, 0, 0))
    col2 = pl.BlockSpec((1, RWKV_HEAD, 2 * LANES), lambda i: (i, 0, 0))
    full = lambda s: pl.BlockSpec(s, lambda i: (0,) * len(s))
    outs = pl.pallas_call(
        body, name=f"rwkv_scan_fwd_l{layer}", grid=(nb,),
        in_specs=[rows, rows, rows, col2, rows, rows_next, rows, full((256, 256)), full((D_RWKV, LANES)),
                  full((LANES, D_RWKV))] + rider.in_specs + [HBM_SPEC] * ng,
        out_specs=[col, pl.BlockSpec((SCAN_TB, 2, RWKV_HEAD, D_RWKV), lambda i: (i, 0, 0, 0))] + rider.out_specs
        + [HBM_SPEC] * ng,
        out_shape=[SDS((nb, RWKV_HEAD, LANES), f32), SDS((T, 2, RWKV_HEAD, D_RWKV), f32)] + rider.out_shape
        + _gather_out_shapes(gather),
        scratch_shapes=[pltpu.VMEM((RWKV_HEAD, D_RWKV), f32), pltpu.VMEM((RWKV_HEAD, D_RWKV), f32),
                        pltpu.VMEM((SCAN_TB * RWKV_HEAD, D_RWKV), f32),
                        pltpu.VMEM((SCAN_TB * RWKV_HEAD, D_RWKV), f32)] + rider.scratch
        + (_gather_sems(ng) if ng else []),
        compiler_params=_params(("arbitrary",), 52),
    )(r, w, k, vcol, a, a, b, ee, e16, e16t, *rider.args, *gather)
    return outs[0], outs[1], list(outs[2:2 + rider.n_out]), list(outs[2 + rider.n_out:])


def _rwkv_scan_bwd(r, w, k, vcol, a, b, sprev, docol, ee, e16, e16t, layer, send=()):
    T = r.shape[0]
    nb = T // SCAN_TB
    ns = len(send)

    def body(*refs):
        r_ref, w_ref, k_ref, v_ref, a_ref, b_ref, sp_ref, do_ref, ee_ref, red_ref, exp_ref = refs[:11]
        c_ins, refs = refs[11:11 + ns], refs[11 + ns:]
        dr_o, dw_o, dk_o, da_o, db_o, dv_o = refs[:6]
        c_outs, refs = refs[6:6 + ns], refs[6 + ns:]
        ds_ref, snext_ref, vb_ref, dob_ref, pdv_ref = refs[:5]
        c_sems = refs[5:]
        if ns:
            @pl.when(pl.program_id(0) == 0)
            def _():
                for cp in _chips_copies(c_ins, c_outs, *c_sems):
                    cp.start()

        ee_v = ee_ref[...]
        csum = lambda x: jnp.sum(x, axis=0, keepdims=True)
        nrow = SCAN_TB * RWKV_HEAD
        last = pl.ds(nrow - RWKV_HEAD, RWKV_HEAD)
        r_b, w_b, k_b, a_b, b_b = r_ref[...], w_ref[...], k_ref[...], a_ref[...], b_ref[...]
        vb_ref[...] = _unpack_cols(v_ref, exp_ref[...])
        dob_ref[...] = _unpack_cols(do_ref, exp_ref[...])

        @pl.when(pl.program_id(0) == 0)
        def _():
            ds_ref[...] = jnp.zeros_like(ds_ref)
            snext_ref[...] = (sp_ref[SCAN_TB - 1, 0] * w_b[SCAN_TB - 1:SCAN_TB, :]
                              + sp_ref[SCAN_TB - 1, 1] * b_b[SCAN_TB - 1:SCAN_TB, :]
                              + vb_ref[last, :] * k_b[SCAN_TB - 1:SCAN_TB, :])

        for s in range(SCAN_TB):
            t = SCAN_TB - 1 - s
            row = lambda x: x[t:t + 1, :]
            blk = pl.ds(t * RWKV_HEAD, RWKV_HEAD)
            sp, sa = sp_ref[t, 0], sp_ref[t, 1]
            sn = snext_ref[...] if s == 0 else sp_ref[t + 1, 0]
            vb, dob = vb_ref[blk, :], dob_ref[blk, :]
            ds = ds_ref[...] + dob * row(r_b)
            dsa = _seg_raw(ds * row(b_b), ee_v, SCAN_PARTS, per_tile=True)
            ds_ref[...] = ds * row(w_b) + dsa * row(a_b)
            dr_o[t:t + 1, :] = csum(sn * dob)
            pdv_ref[blk, :] = ds * row(k_b)
            dk_o[t:t + 1, :] = csum(ds * vb)
            db_o[t:t + 1, :] = csum(ds * sa)
            dw_o[t:t + 1, :] = csum(ds * sp)
            da_o[t:t + 1, :] = csum(sp * dsa)
        snext_ref[...] = sp_ref[0, 0]
        dv_o[0] = _pack_cols(pdv_ref, red_ref[...])

        if ns:
            @pl.when(pl.program_id(0) == nb - 1)
            def _():
                for cp in _chips_copies(c_ins, c_outs, *c_sems):
                    cp.wait()

    rows = pl.BlockSpec((SCAN_TB, D_RWKV), lambda i: (nb - 1 - i, 0))
    col = pl.BlockSpec((1, RWKV_HEAD, LANES), lambda i: (nb - 1 - i, 0, 0))
    col2 = pl.BlockSpec((1, RWKV_HEAD, 2 * LANES), lambda i: (nb - 1 - i, 0, 0))
    st = pl.BlockSpec((SCAN_TB, 2, RWKV_HEAD, D_RWKV), lambda i: (nb - 1 - i, 0, 0, 0))
    full = lambda s: pl.BlockSpec(s, lambda i: (0,) * len(s))
    big = pltpu.VMEM((SCAN_TB * RWKV_HEAD, D_RWKV), f32)
    outs = pl.pallas_call(
        body, name=f"rwkv_scan_bwd_l{layer}", grid=(nb,),
        in_specs=[rows, rows, rows, col2, rows, rows, st, col2, full((256, 256)), full((D_RWKV, LANES)),
                  full((LANES, D_RWKV))] + [HBM_SPEC] * ns,
        out_specs=[rows] * 5 + [col] + [HBM_SPEC] * ns,
        out_shape=[SDS((T, D_RWKV), f32)] * 5 + [SDS((nb, RWKV_HEAD, LANES), f32)] + _chips_out_shapes(send),
        scratch_shapes=[pltpu.VMEM((RWKV_HEAD, D_RWKV), f32)] * 2 + [big] * 3 + (_chips_sems(ns) if ns else []),
        compiler_params=_params(("arbitrary",), 56),
    )(r, w, k, vcol, a, b, sprev, docol, ee, e16, e16t, *send)
    return (*outs[:6], list(outs[6:]))


def _post_core(o, r, k2, v, z, gw, gb, rk, ee):
    inv = 1.0 / RWKV_HEAD
    mu = _seg(o, ee) * inv
    d = o - mu
    var = _seg(d * d, ee) * inv
    on = d * lax.rsqrt(var + GN_EPS) * gw + gb
    bonus = _seg(r * k2 * rk, ee) * v
    return (on + bonus) * jax.nn.silu(z)


def _rwkv_post_fwd(o, r, k2, v, z, gw, gb, rk, ee, cat, layer):
    T = o.shape[0]

    def body(o_ref, r_ref, k_ref, v_ref, z_ref, gw_ref, gb_ref, rk_ref, ee_ref, cat_in, out_ref):
        del cat_in
        out_ref[...] = _post_core(o_ref[...], r_ref[...], k_ref[...], v_ref[...], z_ref[...], gw_ref[...],
                                  gb_ref[...], rk_ref[...], ee_ref[...])

    blk = pl.BlockSpec((TB, D_RWKV), lambda i: (i, 0))
    row = pl.BlockSpec((1, D_RWKV), lambda i: (0, 0))
    return pl.pallas_call(
        body, name=f"rwkv_post_fwd_l{layer}", grid=(T // TB,),
        in_specs=[blk] * 5 + [row] * 3 + [pl.BlockSpec((256, 256), lambda i: (0, 0)), HBM_SPEC],
        out_specs=blk, out_shape=SDS((T, D_MODEL), f32), input_output_aliases={9: 0},
        compiler_params=_params(("parallel",), 40),
    )(o, r, k2, v, z, gw, gb, rk, ee, cat)


def _rwkv_post_bwd(o, r, k2, v, z, gw, gb, rk, ee, dcat, layer):
    T = o.shape[0]

    def body(o_ref, r_ref, k_ref, v_ref, z_ref, gw_ref, gb_ref, rk_ref, ee_ref, g_ref,
             do_o, dr_o, dk_o, dv_o, dz_o, dgw_o, dgb_o, drk_o):
        ee_v = ee_ref[...]
        fn = lambda o_, r_, k_, v_, z_, gw_, gb_, rk_: _post_core(o_, r_, k_, v_, z_, gw_, gb_, rk_, ee_v)
        _, vjp = jax.vjp(fn, o_ref[...], r_ref[...], k_ref[...], v_ref[...], z_ref[...], gw_ref[...], gb_ref[...],
                         rk_ref[...])
        gs = vjp(g_ref[...])
        do_o[...], dr_o[...], dk_o[...], dv_o[...], dz_o[...] = gs[:5]

        @pl.when(pl.program_id(0) == 0)
        def _():
            dgw_o[...] = jnp.zeros_like(dgw_o)
            dgb_o[...] = jnp.zeros_like(dgb_o)
            drk_o[...] = jnp.zeros_like(drk_o)

        dgw_o[...] += gs[5]
        dgb_o[...] += gs[6]
        drk_o[...] += gs[7]

    blk = pl.BlockSpec((TB, D_RWKV), lambda i: (i, 0))
    row = pl.BlockSpec((1, D_RWKV), lambda i: (0, 0))
    return pl.pallas_call(
        body, name=f"rwkv_post_bwd_l{layer}", grid=(T // TB,),
        in_specs=[blk] * 5 + [row] * 3 + [pl.BlockSpec((256, 256), lambda i: (0, 0)), blk],
        out_specs=[blk] * 5 + [row] * 3,
        out_shape=[SDS((T, D_RWKV), f32)] * 5 + [SDS((1, D_RWKV), f32)] * 3,
        compiler_params=_params(("arbitrary",), 48),
    )(o, r, k2, v, z, gw, gb, rk, ee, dcat)


def _hgrn_chunk(qr, fr, ii, z, S, lbl, gw, layer):
    L = CHUNK
    G = qr.shape[1] // HG_HEAD
    hd = lambda x, g: x[..., g * HG_HEAD:(g + 1) * HG_HEAD]
    heads = lambda f: jnp.concatenate([f(g) for g in range(G)], axis=-1)
    m = jnp.max(lbl, axis=0, keepdims=True)
    e = jnp.exp(lbl - m)
    sm = e / jnp.sum(e, axis=0, keepdims=True)
    if layer == 0:
        lb = sm[0:1] - sm[0:1]
    else:
        lb = (sm[0:1] + sm[1:2]) - sm[0:1]
    q = jax.nn.silu(qr)
    log_lb = jnp.log(jnp.maximum(lb, LB_FLOOR))
    lf = jnp.logaddexp(log_lb, jnp.log1p(-lb) + jax.nn.log_sigmoid(fr))
    k = (1.0 - lb) * jax.nn.sigmoid(-fr)
    ti = lax.broadcasted_iota(jnp.int32, (L, L), 0)
    si = lax.broadcasted_iota(jnp.int32, (L, L), 1)
    ltri = (si <= ti).astype(f32)
    b = jnp.dot(ltri, lf, precision=HIGHEST, preferred_element_type=f32)
    btot = jnp.dot(jnp.ones((L, L), f32), lf, precision=HIGHEST, preferred_element_type=f32)
    qd = q * jnp.exp(b)
    o = heads(lambda g: jnp.dot(hd(qd, g), S[g], preferred_element_type=f32))
    nsub = L // HSUB
    t3 = lax.broadcasted_iota(jnp.int32, (HSUB, HSUB, G * HG_HEAD), 0)
    s3 = lax.broadcasted_iota(jnp.int32, (HSUB, HSUB, G * HG_HEAD), 1)
    causal = s3 <= t3
    nt = (((1,), (1,)), ((), ()))
    tn = (((0,), (0,)), ((), ()))
    rows = []
    for I in range(nsub):
        sl = slice(I * HSUB, (I + 1) * HSUB)
        bI, qI, kI, iI = b[sl], q[sl], k[sl], ii[sl]
        d3 = bI[:, None, :] - bI[None, :, :]
        dec = jnp.where(causal, jnp.exp(jnp.where(causal, d3, 0.0)), 0.0)
        prod = qI[:, None, :] * dec * kI[None, :, :]
        oI = heads(lambda g: jnp.dot(jnp.sum(hd(prod, g), axis=-1), hd(iI, g), preferred_element_type=f32))
        for J in range(I):
            sj = slice(J * HSUB, (J + 1) * HSUB)
            bm = b[(J + 1) * HSUB - 1:(J + 1) * HSUB]
            qs = qI * jnp.exp(bI - bm)
            ks = k[sj] * jnp.exp(bm - b[sj])
            iJ = ii[sj]
            oI = oI + heads(lambda g: jnp.dot(
                lax.dot_general(hd(qs, g), hd(ks, g), nt, preferred_element_type=f32), hd(iJ, g),
                preferred_element_type=f32))
        rows.append(oI)
    o = o + jnp.concatenate(rows, axis=0)
    k_dec = k * jnp.exp(btot - b)
    ones = jnp.ones((L, HG_HEAD), f32)
    s_new = jnp.concatenate([
        (S[g] * jnp.exp(lax.dot_general(hd(lf, g), ones, tn, precision=HIGHEST, preferred_element_type=f32))
         + lax.dot_general(hd(k_dec, g), hd(ii, g), tn, preferred_element_type=f32))[None] for g in range(G)], axis=0)
    ms = heads(lambda g: jnp.broadcast_to(jnp.mean(hd(o * o, g), axis=-1, keepdims=True), (L, HG_HEAD)))
    out = o * lax.rsqrt(ms + RMS_EPS) * gw * jax.nn.silu(z)
    return out, s_new


class _Part:
    def __init__(self, args, in_specs, out_specs, out_shape, scratch, step):
        self.args, self.in_specs, self.out_specs, self.out_shape = args, in_specs, out_specs, out_shape
        self.scratch, self.step, self.total = scratch, step, None
        self.n_in, self.n_out, self.n_scr = len(args), len(out_shape), len(scratch)


def _run_part(part, steps, name):
    part.total = steps

    def body(*refs):
        ins, outs = refs[:part.n_in], refs[part.n_in:part.n_in + part.n_out]
        part.step(pl.program_id(0), ins, outs, refs[part.n_in + part.n_out:])

    return pl.pallas_call(body, name=name, grid=(steps,), in_specs=part.in_specs, out_specs=part.out_specs,
                          out_shape=part.out_shape, scratch_shapes=part.scratch,
                          compiler_params=_params(("arbitrary",), 48))(*part.args)


def _hg_index(T, rev, group=1):
    nc = T // CHUNK
    width = group * HG_HEAD
    head = lambda i: i // nc
    chunk = (lambda i: nc - 1 - i % nc) if rev else (lambda i: i % nc)
    first = lambda i: i % nc == 0
    pj = lambda off, g: pl.BlockSpec((CHUNK, HG_HEAD),
                                     lambda i: (chunk(i), RW_BLKS + off + group * head(i) + g))
    pspecs = [pj(kind * HG_HEADS, g) for kind in range(4) for g in range(group)]
    per_head = lambda rows: pl.BlockSpec((rows, width), lambda i: (0, head(i)))
    return nc, head, chunk, first, pspecs, per_head


def _hgrn_fwd_part(proj, lbl, gw, layer):
    T = proj.shape[0]
    nc, head, chunk, first, pspecs, per_head = _hg_index(T, False)

    def step(i, ins, outs, scr):
        q_ref, f_ref, i_ref, z_ref, lbl_ref, gw_ref = ins
        out_ref, sh_ref = outs
        s_ref, = scr

        @pl.when(first(i))
        def _():
            s_ref[...] = jnp.zeros_like(s_ref)

        s0 = s_ref[...]
        sh_ref[:, 0] = s0
        out, s_new = _hgrn_chunk(q_ref[...], f_ref[...], i_ref[...], z_ref[...], s0, lbl_ref[...], gw_ref[...], layer)
        out_ref[...] = out
        s_ref[...] = s_new

    return _Part(
        [proj, proj, proj, proj, lbl, gw], pspecs + [per_head(2), per_head(1)],
        [pl.BlockSpec((CHUNK, HG_HEAD), lambda i: (chunk(i), HG_HEADS + head(i))),
         pl.BlockSpec((1, 1, HG_HEAD, HG_HEAD), lambda i: (head(i), chunk(i), 0, 0))],
        [SDS((T, D_MODEL), f32), SDS((HG_HEADS, nc, HG_HEAD, HG_HEAD), f32)],
        [pltpu.VMEM((1, HG_HEAD, HG_HEAD), f32)], step)


HG_GROUP = 2


def _hgrn_bwd_part(proj, shist, dcat, lbl, gw, layer):
    T = proj.shape[0]
    G = HG_GROUP
    nc, head, chunk, first, pspecs, per_head = _hg_index(T, True, G)

    def step(i, ins, outs, scr):
        p_refs, (lbl_ref, gw_ref, sh_ref, g_ref) = ins[:4 * G], ins[4 * G:]
        dq_o, df_o, di_o, dz_o, dlbl_o, dgw_o = outs
        ds_ref, = scr

        @pl.when(first(i))
        def _():
            ds_ref[...] = jnp.zeros_like(ds_ref)
            dlbl_o[...] = jnp.zeros_like(dlbl_o)
            dgw_o[...] = jnp.zeros_like(dgw_o)

        fn = functools.partial(_hgrn_chunk, layer=layer)
        q, f, ii, z = (jnp.concatenate([p_refs[kind * G + g][...] for g in range(G)], axis=1) for kind in range(4))
        _, vjp = jax.vjp(fn, q, f, ii, z, sh_ref[:, 0], lbl_ref[...], gw_ref[...])
        dq, df, di, dz, ds, dl, dg = vjp((g_ref[...], ds_ref[...]))
        dq_o[...], df_o[...], di_o[...], dz_o[...] = dq, df, di, dz
        ds_ref[...] = ds
        dlbl_o[...] += dl
        dgw_o[...] += dg

    ospec = pl.BlockSpec((CHUNK, G * HG_HEAD), lambda i: (chunk(i), head(i)))
    return _Part(
        [proj] * (4 * G) + [lbl, gw, shist, dcat],
        pspecs + [per_head(2), per_head(1),
                  pl.BlockSpec((G, 1, HG_HEAD, HG_HEAD), lambda i: (head(i), chunk(i), 0, 0)),
                  pl.BlockSpec((CHUNK, G * HG_HEAD), lambda i: (chunk(i), HG_HEADS // G + head(i)))],
        [ospec] * 4 + [per_head(2), per_head(1)],
        [SDS((T, D_HGRN), f32)] * 4 + [SDS((2, D_HGRN), f32), SDS((1, D_HGRN), f32)],
        [pltpu.VMEM((G, HG_HEAD, HG_HEAD), f32)], step)


def _ln_core(h, y, w, b):
    u = ALPHA * h + y
    mu = jnp.mean(u, axis=-1, keepdims=True)
    d = u - mu
    var = jnp.mean(d * d, axis=-1, keepdims=True)
    return d * lax.rsqrt(var + LN_EPS) * w + b


def _ln_fwd(h, y, w, b, layer):
    T = h.shape[0]

    def body(h_ref, y_ref, w_ref, b_ref, o_ref):
        o_ref[...] = _ln_core(h_ref[...], y_ref[...], w_ref[...], b_ref[...])

    blk = pl.BlockSpec((TB, D_MODEL), lambda i: (i, 0))
    row = pl.BlockSpec((1, D_MODEL), lambda i: (0, 0))
    return pl.pallas_call(body, name=f"ln_fwd_l{layer}", grid=(T // TB,), in_specs=[blk, blk, row, row],
                          out_specs=blk, out_shape=SDS((T, D_MODEL), f32),
                          compiler_params=_params(("parallel",), 40))(h, y, w, b)


def _ln_bwd(h, y, w, b, g, layer):
    T = h.shape[0]

    def body(h_ref, y_ref, w_ref, b_ref, g_ref, dh_o, dy_o, dw_o, db_o):
        _, vjp = jax.vjp(_ln_core, h_ref[...], y_ref[...], w_ref[...], b_ref[...])
        dh, dy, dw, db = vjp(g_ref[...])
        dh_o[...] = dh
        dy_o[...] = dy.astype(bf16)

        @pl.when(pl.program_id(0) == 0)
        def _():
            dw_o[...] = jnp.zeros_like(dw_o)
            db_o[...] = jnp.zeros_like(db_o)

        dw_o[...] += dw
        db_o[...] += db

    blk = pl.BlockSpec((TB, D_MODEL), lambda i: (i, 0))
    row = pl.BlockSpec((1, D_MODEL), lambda i: (0, 0))
    return pl.pallas_call(body, name=f"ln_bwd_l{layer}", grid=(T // TB,), in_specs=[blk, blk, row, row, blk],
                          out_specs=[blk, blk, row, row],
                          out_shape=[SDS((T, D_MODEL), f32), SDS((T, D_MODEL), bf16)] + [SDS((1, D_MODEL), f32)] * 2,
                          compiler_params=_params(("arbitrary",), 48))(h, y, w, b, g)


def _loss_head(hout, target):
    T = hout.shape[0]

    def body(h_ref, t_ref, loss_o, g_o):
        err = h_ref[...] - t_ref[...]
        g_o[...] = err * (1.0 / D_MODEL)

        @pl.when(pl.program_id(0) == 0)
        def _():
            loss_o[...] = jnp.zeros_like(loss_o)

        part = jnp.sum(jnp.sum(err * err, axis=-1, keepdims=True) * (1.0 / D_MODEL), axis=0, keepdims=True)
        loss_o[...] += 0.5 * part

    blk = pl.BlockSpec((TB, D_MODEL), lambda i: (i, 0))
    return pl.pallas_call(body, name="loss_head", grid=(T // TB,), in_specs=[blk, blk],
                          out_specs=[pl.BlockSpec((1, 1), lambda i: (0, 0)), blk],
                          out_shape=[SDS((1, 1), f32), SDS((T, D_MODEL), f32)],
                          compiler_params=_params(("arbitrary",), 40))(hout, target)


def _pad_rows(m, lo, total):
    return jnp.pad(m, ((0, 0), (lo, total - lo - m.shape[1]), (0, 0)))


def _device_step(x, target, w_in0, later_shards, assemble_later, on_grads, on_swapped, on_recv, on_small_grads,
                 shift_mu, w_decay0, w_decay_up, a0, a_up, k_k, k_a, r_k, ln_x_w, ln_x_b,
                 v_mix0, v_mix_down, v_mix_up, lb_logits, g_norm_w, ln_w, ln_b):
    T = x.shape[0]
    ee, e16, e16t = _const_mats()
    wup_pad = _pad_rows(w_decay_up, 0, LANES)
    aup_pad = _pad_rows(a_up, LORA, LANES)
    vdn_pad = jnp.pad(v_mix_down, ((0, 0), (0, 0), (0, LANES - VRES)))
    vup_pad = _pad_rows(v_mix_up, 0, LANES)
    row = lambda p, l: p[l][None, :]

    def vmix_of(l, vfirst):
        if l == 0:
            return None
        return (row(v_mix0, l - 1), vdn_pad[l - 1], vup_pad[l - 1], vfirst)

    h = x
    saved = []
    vfirst = None
    w_in = [w_in0, None]
    w_out = None
    for l in range(DEPTH):
        hb = h.astype(bf16)
        proj = _matmul(hb, w_in[l], mode="nt", tm=1024, tn=1664, tk=D_MODEL, name=f"proj_fwd_l{l}")
        pre_args = (proj, row(shift_mu, l), row(w_decay0, l), wup_pad[l], row(a0, l), aup_pad[l], row(k_k, l),
                    row(k_a, l), ee, vmix_of(l, vfirst))
        r, w, k2, v, av, bv, z = _rwkv_pre_fwd(*pre_args, layer=l)
        if l == 0:
            vfirst = v
        vcol = _to_col(v)
        ocol, sprev, (cat, shist), gathered = _rwkv_scan_fwd(
            r, w, k2, vcol, av, bv, ee, e16, e16t, l, _hgrn_fwd_part(proj, lb_logits, row(g_norm_w, l), l),
            gather=later_shards if l == 0 else [])
        if l == 0:
            w_in[1], w_out = assemble_later(gathered)
        o = _from_col(ocol)
        post_args = (o, r, k2, v, z, row(ln_x_w, l), row(ln_x_b, l), row(r_k, l), ee)
        cat = _rwkv_post_fwd(*post_args, cat, layer=l)
        catb = cat.astype(bf16)
        y = _matmul(catb, w_out, b_layer=l, mode="nn", tm=1024, tn=1024, tk=D_MODEL, name=f"out_fwd_l{l}")
        h_new = _ln_fwd(h, y, row(ln_w, l), row(ln_b, l), l)
        saved.append(dict(h=h, hb=hb, proj=proj, pre_args=pre_args, post_args=post_args,
                          scan=(r, w, k2, vcol, av, bv, sprev), cat=catb, shist=shist, y=y))
        h = h_new

    loss_sum, g = _loss_head(h, target)

    grads = {n: [None] * DEPTH for n in ("w_in", "w_out", "shift_mu", "w_decay0", "w_decay_up", "a0", "a_up", "k_k",
                                         "k_a", "r_k", "ln_x_w", "ln_x_b", "g_norm_w", "ln_w", "ln_b")}
    dlbl = []
    dvfirst = None
    to_send = []
    for l in reversed(range(DEPTH)):
        s = saved[l]
        dh_res, dy, dlnw, dlnb = _ln_bwd(s["h"], s["y"], row(ln_w, l), row(ln_b, l), g, l)
        grads["ln_w"][l], grads["ln_b"][l] = dlnw[0], dlnb[0]
        dcat = _matmul(dy, w_out, b_layer=l, mode="nt", tm=1024, tn=1024, tk=D_MODEL, name=f"out_bwd_dx_l{l}")
        grads["w_out"][l] = _matmul(s["cat"], dy, mode="tn", tm=1024, tn=1024, tk=512, name=f"out_bwd_dw_l{l}")
        do, dr_p, dk_p, dv_p, dz, dgw, dgb, drk = _rwkv_post_bwd(*s["post_args"], dcat, layer=l)
        grads["ln_x_w"][l], grads["ln_x_b"][l], grads["r_k"][l] = dgw[0], dgb[0], drk[0]
        r, w, k2, vcol, av, bv, sprev = s["scan"]
        dr_s, dw_s, dk_s, da_s, db_s, dvcol, arrived = _rwkv_scan_bwd(
            r, w, k2, vcol, av, bv, sprev, _to_col(do), ee, e16, e16t, l, send=to_send if l == 0 else [])
        dq, df, di, dzh, dl, dgn = _run_part(
            _hgrn_bwd_part(s["proj"], s["shist"], dcat, lb_logits, row(g_norm_w, l), l),
            (T // CHUNK) * HG_HEADS // HG_GROUP, f"hgrn_bwd_l{l}")
        if l == 0:
            on_recv(1, arrived)
        v_cots = [dv_p, _from_col(dvcol)] + ([dvfirst] if (l == 0 and dvfirst is not None) else [])
        cots = dict(r=[dr_p, dr_s], w=dw_s, k=[dk_p, dk_s], v=v_cots, a=da_s, b=db_s, z=dz)
        outs = _rwkv_pre_bwd(*s["pre_args"], cots, layer=l)
        dproj, dmu, dw0, dwup, da0, daup, dkkw, dkaw = outs[:8]
        grads["shift_mu"][l], grads["w_decay0"][l], grads["a0"][l] = dmu[0], dw0[0], da0[0]
        grads["k_k"][l], grads["k_a"][l] = dkkw[0], dkaw[0]
        grads["w_decay_up"][l], grads["a_up"][l] = dwup[:LORA], daup[LORA:]
        if l > 0:
            dv0, dvdn, dvup, dvfirst = outs[8:]
            g_vmix = (dv0, dvdn[:, :VRES][None], dvup[:VRES][None])
        dlbl.append(dl)
        grads["g_norm_w"][l] = dgn[0]
        dproj = jnp.concatenate([dproj, dq, df, di, dzh], axis=1).astype(bf16)
        dw_args = dict(mode="tn", tm=1664, tn=1024, tk=512, name=f"proj_bwd_dw_l{l}")
        dx_args = dict(mode="nn", tm=1024, tn=1024, tk=1664, name=f"proj_bwd_dx_l{l}", add=dh_res)
        if l == 1:
            grads["w_in"][l] = _matmul(dproj, s["hb"], **dw_args)
            g, swapped = _matmul(dproj, w_in[l], rider=_swap_part(on_grads(l, grads["w_in"][l], grads["w_out"][l])),
                                 **dx_args)
            to_send = on_swapped(l, swapped)
        else:
            small = {n: jnp.stack(v) for n, v in grads.items() if n not in ("w_in", "w_out")}
            small["lb_logits"] = dlbl[0] + dlbl[1]
            small["v_mix0"], small["v_mix_down"], small["v_mix_up"] = g_vmix
            grads["w_in"][l], small_out = _matmul(dproj, s["hb"], rider=on_small_grads(small), **dw_args)
            swapped = _run_part(_swap_part(on_grads(l, grads["w_in"][l], grads["w_out"][l])), 1, "rs_swap_halves_l0")
            g, arrived = _matmul(dproj, w_in[l], rider=_chips_part(on_swapped(l, swapped)), **dx_args)
            on_recv(0, arrived)
    small["w_in"], small["w_out"] = grads["w_in"], grads["w_out"]
    return loss_sum, g, small, small_out


CHIP_COMBOS = ((1, 0), (0, 1), (1, 1))
HBM_SPEC = pl.BlockSpec(memory_space=pl.ANY)


def _mesh_pos():
    return lax.axis_index("x"), lax.axis_index("y"), lax.axis_index("c")


def _flip(v, d):
    return 1 - v if d else v


def _half(ref, dim, cc, hs):
    idx = [slice(None)] * len(ref.shape)
    idx[dim] = pl.ds(cc * hs, hs)
    return ref.at[tuple(idx)]


def _gather_shards(arrs):
    n = len(arrs)

    def body(*refs):
        ins, outs = refs[:n], refs[n:2 * n]
        sems = refs[2 * n:]
        _gather_start(ins, outs, *sems)
        _gather_forward(ins, outs, *sems)
        _gather_finish(ins, outs, *sems)

    return pl.pallas_call(
        body, name="gather_shards", in_specs=[HBM_SPEC] * n, out_specs=[HBM_SPEC] * n,
        out_shape=_gather_out_shapes(arrs), scratch_shapes=_gather_sems(n),
    )(*arrs)


def _gather_out_shapes(arrs):
    return [SDS((4,) + a.shape, a.dtype) for a in arrs]


def _gather_sems(n):
    return [pltpu.SemaphoreType.DMA((n, 6)), pltpu.SemaphoreType.DMA((n, 6)), pltpu.SemaphoreType.DMA((n,))]


def _gather_copies(kind, ins, outs, send_sems, recv_sems, local_sems):
    x, y, c = _mesh_pos()
    kc = 2 * x + y
    cps = []
    for a in range(len(ins)):
        if kind == "local":
            cps.append(pltpu.make_async_copy(ins[a], outs[a].at[kc], local_sems.at[a]))
            continue
        for j, (dx, dy) in enumerate(CHIP_COMBOS):
            px, py = _flip(x, dx), _flip(y, dy)
            kj = 2 * px + py
            src, dst, s, to = {
                "sends": (ins[a].at[c], outs[a].at[kc, c], j, (px, py, c)),
                "landed": (outs[a].at[kj, c], outs[a].at[kj, c], j, (px, py, c)),
                "forwards": (outs[a].at[kj, c], outs[a].at[kj, c], 3 + j, (x, y, 1 - c)),
                "passed": (outs[a].at[kj, 1 - c], outs[a].at[kj, 1 - c], 3 + j, (x, y, 1 - c)),
            }[kind]
            cps.append(pltpu.make_async_remote_copy(
                src_ref=src, dst_ref=dst, send_sem=send_sems.at[a, s], recv_sem=recv_sems.at[a, s], device_id=to,
                device_id_type=MESH))
    return cps


def _gather_start(*refs):
    for cp in _gather_copies("local", *refs) + _gather_copies("sends", *refs):
        cp.start()


def _gather_forward(*refs):
    for arrival, fw in zip(_gather_copies("landed", *refs), _gather_copies("forwards", *refs)):
        arrival.wait_recv()
        fw.start()


def _gather_finish(*refs):
    for cp in _gather_copies("passed", *refs):
        cp.wait_recv()
    for cp in _gather_copies("sends", *refs) + _gather_copies("forwards", *refs):
        cp.wait_send()
    for cp in _gather_copies("local", *refs):
        cp.wait()


DEV_COMBOS = tuple((dx, dy, dc) for dx in (0, 1) for dy in (0, 1) for dc in (0, 1))[1:]


def _allreduce_part(buf):
    R = buf.shape[0]
    whole = pl.BlockSpec((R, LANES), lambda *_: (0, 0))

    def copies(x_ref, slots, send_sems, recv_sems):
        x, y, c = _mesh_pos()
        return [pltpu.make_async_remote_copy(
            src_ref=x_ref, dst_ref=slots.at[k + 1], send_sem=send_sems.at[k], recv_sem=recv_sems.at[k],
            device_id=(_flip(x, dx), _flip(y, dy), _flip(c, dc)), device_id_type=MESH)
            for k, (dx, dy, dc) in enumerate(DEV_COMBOS)]

    def step(i, ins, outs, scr):
        x_ref, = ins
        o_ref, = outs
        slots = scr[0]

        @pl.when(i == 0)
        def _():
            for cp in copies(x_ref, *scr):
                cp.start()
            slots[0] = x_ref[...]

        @pl.when(i == part.total - 1)
        def _():
            x, y, c = _mesh_pos()
            me = 4 * x + 2 * y + c
            cps = copies(x_ref, *scr)
            for cp in cps:
                cp.wait_recv()
            acc = slots[jnp.bitwise_xor(me, 0)]
            for d in range(1, 8):
                acc = acc + slots[jnp.bitwise_xor(me, d)]
            o_ref[...] = acc
            for cp in cps:
                cp.wait_send()

    part = _Part([buf], [whole], [whole], [SDS((R, LANES), f32)],
                 [pltpu.VMEM((8, R, LANES), f32), pltpu.SemaphoreType.DMA((7,)), pltpu.SemaphoreType.DMA((7,))], step)
    return part


def _chips_part(slabs):
    n = len(slabs)

    def step(i, ins, outs, scr):
        if not n:
            return

        @pl.when(i == 0)
        def _():
            for cp in _chips_copies(ins, outs, *scr):
                cp.start()

        @pl.when(i == part.total - 1)
        def _():
            for cp in _chips_copies(ins, outs, *scr):
                cp.wait()

    part = _Part(list(slabs), [HBM_SPEC] * n, [HBM_SPEC] * n, _chips_out_shapes(slabs),
                 _chips_sems(n) if n else [], step)
    return part


def _swap_part(gs):
    n = len(gs)

    def copies(ins, outs, send_sems, recv_sems):
        x, y, c = _mesh_pos()
        return [pltpu.make_async_remote_copy(
            src_ref=_half(ins[a], 1, 1 - c, ins[a].shape[1] // 2), dst_ref=outs[a], send_sem=send_sems.at[a],
            recv_sem=recv_sems.at[a], device_id=(x, y, 1 - c), device_id_type=MESH) for a in range(n)]

    def step(i, ins, outs, scr):
        if not n:
            return

        @pl.when(i == 0)
        def _():
            for cp in copies(ins, outs, *scr):
                cp.start()

        @pl.when(i == part.total - 1)
        def _():
            for cp in copies(ins, outs, *scr):
                cp.wait()

    part = _Part(list(gs), [HBM_SPEC] * n, [HBM_SPEC] * n,
                 [SDS((g.shape[0], g.shape[1] // 2, g.shape[2]), g.dtype) for g in gs],
                 [pltpu.SemaphoreType.DMA((n,)), pltpu.SemaphoreType.DMA((n,))] if n else [], step)
    return part


def _chips_copies(ins, outs, send_sems, recv_sems):
    x, y, c = _mesh_pos()
    cps = []
    for a in range(len(ins)):
        for j, (dx, dy) in enumerate(CHIP_COMBOS):
            px, py = _flip(x, dx), _flip(y, dy)
            cps.append(pltpu.make_async_remote_copy(
                src_ref=ins[a].at[2 * px + py], dst_ref=outs[a].at[j], send_sem=send_sems.at[a, j],
                recv_sem=recv_sems.at[a, j], device_id=(px, py, c), device_id_type=MESH))
    return cps


def _chips_out_shapes(ps):
    return [SDS((3,) + p.shape[1:], p.dtype) for p in ps]


def _chips_sems(n):
    return [pltpu.SemaphoreType.DMA((n, 3)), pltpu.SemaphoreType.DMA((n, 3))]


def _share_halves(qs, tag):
    n = len(qs)

    def body(*refs):
        ins, outs = refs[:n], refs[n:2 * n]
        send_sems, recv_sems = refs[2 * n:]
        x, y, c = _mesh_pos()
        cps = []
        for a in range(n):
            cp = pltpu.make_async_remote_copy(
                src_ref=ins[a], dst_ref=outs[a], send_sem=send_sems.at[a], recv_sem=recv_sems.at[a],
                device_id=(x, y, 1 - c), device_id_type=MESH)
            cp.start()
            cps.append(cp)
        for cp in cps:
            cp.wait()

    return pl.pallas_call(
        body, name=f"rs_share_halves_{tag}", in_specs=[HBM_SPEC] * n, out_specs=[HBM_SPEC] * n,
        out_shape=[SDS(q.shape, q.dtype) for q in qs],
        scratch_shapes=[pltpu.SemaphoreType.DMA((n,)), pltpu.SemaphoreType.DMA((n,))],
    )(*qs)


def _row_tile(rows, cap):
    return max(t for t in range(16, cap + 1, 16) if rows % t == 0)


def _add_own_half(g, recv, c, name):
    S, R, C = g.shape
    h = R // 2
    tr = _row_tile(h, 256)
    nt = h // tr

    def body(c_ref, g_ref, r_ref, o_ref, o16_ref):
        del c_ref
        s = g_ref[...] + r_ref[...]
        o_ref[...] = s
        o16_ref[...] = s.astype(bf16)

    ospec = pl.BlockSpec((1, tr, C), lambda s, i, cr: (s, i, 0))
    return pl.pallas_call(
        body, name=name,
        grid_spec=pltpu.PrefetchScalarGridSpec(
            num_scalar_prefetch=1, grid=(S, nt),
            in_specs=[pl.BlockSpec((1, tr, C), lambda s, i, cr: (s, cr[0] * nt + i, 0)), ospec],
            out_specs=[ospec, ospec]),
        out_shape=[SDS((S, h, C), f32), SDS((S, h, C), bf16)],
        compiler_params=_params(("parallel", "parallel"), 40),
    )(c, g, recv)


def _add_chip_parts(p, recv, kc, name):
    _, R, C = p.shape
    tr = _row_tile(R, 256)

    def body(k_ref, p_ref, r0, r1, r2, o_ref):
        del k_ref
        o_ref[...] = ((p_ref[0] + r0[0].astype(f32)) + r1[0].astype(f32)) + r2[0].astype(f32)

    rspec = lambda j: pl.BlockSpec((1, tr, C), lambda i, kr: (j, i, 0))
    return pl.pallas_call(
        body, name=name,
        grid_spec=pltpu.PrefetchScalarGridSpec(
            num_scalar_prefetch=1, grid=(R // tr,),
            in_specs=[pl.BlockSpec((1, tr, C), lambda i, kr: (kr[0], i, 0)), rspec(0), rspec(1), rspec(2)],
            out_specs=pl.BlockSpec((tr, C), lambda i, kr: (i, 0))),
        out_shape=SDS((R, C), f32), compiler_params=_params(("parallel",), 40),
    )(kc, p, recv, recv, recv)


def _rs_pair(gs, got, c, tag):
    pairs = [_add_own_half(g, r, c, f"rs_add_pair_{tag}_{a}") for a, (g, r) in enumerate(zip(gs, got))]
    return [p[0] for p in pairs], [p[1] for p in pairs]


def _rs_finish(ps, arrived, c, kc, tag):
    qs = [_add_chip_parts(p, r, kc, f"rs_add_chips_{tag}_{a}") for a, (p, r) in enumerate(zip(ps, arrived))]
    return qs, _share_halves(qs, tag)


def _adamw_math(w, g, m, v):
    m2 = ADAM_B1 * m + (1.0 - ADAM_B1) * g
    v2 = ADAM_B2 * v + (1.0 - ADAM_B2) * (g * g)
    m_hat = m2 / (1.0 - ADAM_B1 ** ADAM_STEP)
    v_hat = v2 / (1.0 - ADAM_B2 ** ADAM_STEP)
    return -ADAM_LR * (m_hat / (jnp.sqrt(v_hat) + ADAM_EPS) + ADAM_WD * w), m2, v2


def _adamw_shard(w, g_own, g_other, m, v, c, name):
    L, R, C = w.shape
    h = R // 2
    tr = _row_tile(h, 208)
    nh = h // tr

    def body(c_ref, w_ref, go0, gx0, go1, gx1, m_ref, v_ref, g_o, d_o, m_o, v_o):
        mine = (pl.program_id(1) // nh) == c_ref[0]
        first = pl.program_id(0) == 0
        g = jnp.where(mine, jnp.where(first, go0[...], go1[...]), jnp.where(first, gx0[...], gx1[...]))[None]
        d, m2, v2 = _adamw_math(w_ref[...], g, m_ref[...], v_ref[...])
        g_o[...], d_o[...], m_o[...], v_o[...] = g, d, m2, v2

    full = pl.BlockSpec((1, tr, C), lambda l, i, cr: (l, i, 0))
    half = pl.BlockSpec((tr, C), lambda l, i, cr: (i % nh, 0))
    return pl.pallas_call(
        body, name=name,
        grid_spec=pltpu.PrefetchScalarGridSpec(num_scalar_prefetch=1, grid=(L, R // tr),
                                               in_specs=[full, half, half, half, half, full, full],
                                               out_specs=[full] * 4),
        out_shape=[SDS((L, R, C), f32)] * 4, compiler_params=_params(("parallel", "parallel"), 48),
    )(c, w, g_own[0], g_other[0], g_own[1], g_other[1], m, v)


def _adamw_packed(wmv, g, name):
    _, R, C = wmv.shape

    def body(wmv_ref, g_ref, out_ref):
        out_ref[0], out_ref[1], out_ref[2] = _adamw_math(wmv_ref[0], g_ref[...], wmv_ref[1], wmv_ref[2])

    vm = pl.BlockSpec(memory_space=pltpu.VMEM)
    return pl.pallas_call(body, name=name, in_specs=[vm, vm], out_specs=vm, out_shape=SDS((3, R, C), f32),
                          compiler_params=_params(None, 40))(wmv, g)


def _pack(arrs):
    flat = [a.reshape(-1, LANES) for a in arrs]
    rows = sum(f.shape[0] for f in flat)
    pad = (-rows) % 16
    if pad:
        flat.append(jnp.zeros((pad, LANES), f32))
    return jnp.concatenate(flat, axis=0)


def _unpack(buf, shapes):
    out, r = [], 0
    for s in shapes:
        n = math.prod(s) // LANES
        out.append(buf[r:r + n].reshape(s))
        r += n
    return out


REPLICATED = ("shift_mu", "w_decay0", "a0", "k_k", "k_a", "r_k", "ln_x_w", "ln_x_b", "v_mix0", "lb_logits", "g_norm_w",
              "ln_w", "ln_b")
CHIP_SMALL = (("w_decay_up", 2), ("a_up", 2), ("v_mix_down", 1), ("v_mix_up", 2))
WEIGHTS = ("w_in", "shift_mu", "w_decay0", "w_decay_up", "a0", "a_up", "k_k", "k_a", "r_k", "ln_x_w", "ln_x_b",
           "v_mix0", "v_mix_down", "v_mix_up", "lb_logits", "g_norm_w", "w_out", "ln_w", "ln_b")


def kernel(x, w_in, shift_mu, w_decay0, w_decay_up, a0, a_up, k_k, k_a, r_k, ln_x_w, ln_x_b, v_mix0, v_mix_down, v_mix_up, lb_logits, g_norm_w, w_out, ln_w, ln_b, loss_target, m_w_in, m_shift_mu, m_w_decay0, m_w_decay_up, m_a0, m_a_up, m_k_k, m_k_a, m_r_k, m_ln_x_w, m_ln_x_b, m_v_mix0, m_v_mix_down, m_v_mix_up, m_lb_logits, m_g_norm_w, m_w_out, m_ln_w, m_ln_b, v_w_in, v_shift_mu, v_w_decay0, v_w_decay_up, v_a0, v_a_up, v_k_k, v_k_a, v_r_k, v_ln_x_w, v_ln_x_b, v_v_mix0, v_v_mix_down, v_v_mix_up, v_lb_logits, v_g_norm_w, v_w_out, v_ln_w, v_ln_b):
    W = dict(w_in=w_in, shift_mu=shift_mu, w_decay0=w_decay0, w_decay_up=w_decay_up, a0=a0, a_up=a_up, k_k=k_k, k_a=k_a,
             r_k=r_k, ln_x_w=ln_x_w, ln_x_b=ln_x_b, v_mix0=v_mix0, v_mix_down=v_mix_down, v_mix_up=v_mix_up,
             lb_logits=lb_logits, g_norm_w=g_norm_w, w_out=w_out, ln_w=ln_w, ln_b=ln_b)
    M = dict(w_in=m_w_in, shift_mu=m_shift_mu, w_decay0=m_w_decay0, w_decay_up=m_w_decay_up, a0=m_a0, a_up=m_a_up,
             k_k=m_k_k, k_a=m_k_a, r_k=m_r_k, ln_x_w=m_ln_x_w, ln_x_b=m_ln_x_b, v_mix0=m_v_mix0,
             v_mix_down=m_v_mix_down, v_mix_up=m_v_mix_up, lb_logits=m_lb_logits, g_norm_w=m_g_norm_w, w_out=m_w_out,
             ln_w=m_ln_w, ln_b=m_ln_b)
    V = dict(w_in=v_w_in, shift_mu=v_shift_mu, w_decay0=v_w_decay0, w_decay_up=v_w_decay_up, a0=v_a0, a_up=v_a_up,
             k_k=v_k_k, k_a=v_k_a, r_k=v_r_k, ln_x_w=v_ln_x_w, ln_x_b=v_ln_x_b, v_mix0=v_v_mix0,
             v_mix_down=v_v_mix_down, v_mix_up=v_v_mix_up, lb_logits=v_lb_logits, g_norm_w=v_g_norm_w, w_out=v_w_out,
             ln_w=v_ln_w, ln_b=v_ln_b)
    mx, my, mc = _mesh_pos()
    kc = (2 * mx + my).astype(jnp.int32)
    c_arr = jnp.reshape(mc.astype(jnp.int32), (1,))
    kc_arr = jnp.reshape(kc, (1,))

    small_shard = _pack([W[n] for n, _ in CHIP_SMALL])
    tr_in = lambda a: jnp.swapaxes(a, 1, 2)
    in_halves = tr_in(w_in).astype(bf16).reshape(DEPTH, 2, SHARD_COLS // 2, D_MODEL)
    out_halves = jnp.swapaxes(w_out.astype(bf16).reshape(DEPTH, 2, D_MODEL // 8, D_MODEL), 0, 1)
    g_in0, g_small = _gather_shards([in_halves[0], small_shard.reshape(2, -1, LANES)])
    in_full = lambda g: g.reshape(IN_COLS, D_MODEL)
    shard_shapes = [W[n].shape for n, _ in CHIP_SMALL]
    g_small = g_small.reshape(4, -1, LANES)
    per_chip = [_unpack(g_small[k], shard_shapes) for k in range(4)]
    full_small = {n: jnp.concatenate([per_chip[k][i] for k in range(4)], axis=ax)
                  for i, (n, ax) in enumerate(CHIP_SMALL)}

    def assemble_later(gathered):
        g_in1, g_out = gathered
        return in_full(g_in1), jnp.transpose(g_out, (2, 0, 1, 3, 4)).reshape(DEPTH, D_MODEL, D_MODEL)

    in_slabs = lambda g: g.reshape(4, SHARD_COLS, D_MODEL)
    out_slabs = lambda g: g.reshape(4, D_MODEL // 4, D_MODEL)
    slabs, pairs, arrivals = {}, {}, {}

    def on_grads(l, dw_in, dw_out):
        slabs[l] = [in_slabs(dw_in), out_slabs(dw_out)]
        return slabs[l]

    def on_swapped(l, got):
        pairs[l], to_send = _rs_pair(slabs[l], got, c_arr, f"l{l}")
        return to_send

    def on_recv(l, arrived):
        arrivals[l] = arrived

    small_names = list(REPLICATED) + [n for n, _ in CHIP_SMALL]

    def on_small_grads(g):
        return _allreduce_part(_pack([g[n] for n in small_names]))

    loss_sum, gx, G, (small_sum,) = _device_step(
        x[0], loss_target[0], in_full(g_in0), [in_halves[1], out_halves], assemble_later, on_grads, on_swapped, on_recv,
        on_small_grads, shift_mu, w_decay0, full_small["w_decay_up"], a0, full_small["a_up"], k_k, k_a, r_k, ln_x_w,
        ln_x_b, v_mix0, full_small["v_mix_down"], full_small["v_mix_up"], lb_logits, g_norm_w, ln_w, ln_b)
    loss = lax.psum(loss_sum[0, 0], ("x", "y", "c"))

    own1, other1 = _rs_finish(pairs[1], arrivals[1], c_arr, kc_arr, "l1")
    own0, other0 = _rs_finish(pairs[0], arrivals[0], c_arr, kc_arr, "l0")
    gin_own, gout_own = [own0[0], own1[0]], [own0[1], own1[1]]
    gin_other, gout_other = [other0[0], other1[0]], [other0[1], other1[1]]

    small_full = dict(zip(small_names, _unpack(small_sum, [G[n].shape for n in small_names])))
    grads = {n: small_full[n] for n in REPLICATED}
    for n, ax in CHIP_SMALL:
        width = W[n].shape[ax]
        grads[n] = lax.dynamic_slice_in_dim(small_full[n], kc * width, width, axis=ax)

    delta, new_m, new_v = {}, {}, {}
    res = _adamw_shard(tr_in(w_in), gin_own, gin_other, tr_in(m_w_in), tr_in(v_w_in), c_arr, "adamw_w_in")
    grads["w_in"], delta["w_in"], new_m["w_in"], new_v["w_in"] = [tr_in(a) for a in res]
    grads["w_out"], delta["w_out"], new_m["w_out"], new_v["w_out"] = _adamw_shard(
        w_out, gout_own, gout_other, m_w_out, v_w_out, c_arr, "adamw_w_out")
    shapes = [W[n].shape for n in small_names]
    wmv = jnp.stack([_pack([src[n] for n in small_names]) for src in (W, M, V)])
    dmv = _adamw_packed(wmv, _pack([grads[n] for n in small_names]), "adamw_small")
    for n, d1, m1, v1 in zip(small_names, _unpack(dmv[0], shapes), _unpack(dmv[1], shapes), _unpack(dmv[2], shapes)):
        delta[n], new_m[n], new_v[n] = d1, m1, v1

    return (loss, gx[None], *[grads[n] for n in WEIGHTS], *[delta[n] for n in WEIGHTS],
            *[new_m[n] for n in WEIGHTS], *[new_v[n] for n in WEIGHTS])
```

```python
import functools
import math

import jax
import jax.numpy as jnp
from jax import lax
from jax.experimental import pallas as pl
from jax.experimental.pallas import tpu as pltpu

f32 = jnp.float32
bf16 = jnp.bfloat16
SDS = jax.ShapeDtypeStruct
MESH = pl.DeviceIdType.MESH

D_MODEL = 2048
DEPTH = 2
D_RWKV = 1024
D_HGRN = 1024
RWKV_HEAD = 64
RWKV_HEADS = 16
LORA = 64
VRES = 32
HG_HEAD = 128
HG_HEADS = 8
CHUNK = 64
RWKV_COLS = 4 * D_RWKV + 2 * LORA
IN_COLS = RWKV_COLS + 4 * D_HGRN
SHARD_COLS = IN_COLS // 4
LANES = 128
RW_BLKS = RWKV_COLS // LANES
ALPHA = (2 * DEPTH) ** 0.25
LN_EPS = 1e-5
GN_EPS = 64e-5
RMS_EPS = 1e-5
LB_FLOOR = 1e-30
EXP_M05 = math.exp(-0.5)
ADAM_LR, ADAM_B1, ADAM_B2, ADAM_EPS, ADAM_WD, ADAM_STEP = 0.001, 0.9, 0.999, 1e-08, 0.01, 10

TB = 128
TB_WIDE = 256
SCAN_TB = 8
SCAN_PARTS = 2
HSUB = 32
HIGHEST = lax.Precision.HIGHEST


def _params(sem=None, vmem_mb=None):
    kw = {}
    if sem is not None:
        kw["dimension_semantics"] = sem
    if vmem_mb is not None:
        kw["vmem_limit_bytes"] = vmem_mb << 20
    return pltpu.CompilerParams(**kw)


def _split(x, n):
    parts, r = [], x
    for i in range(n):
        p = r.astype(bf16)
        parts.append(p)
        if i + 1 < n:
            r = r - p.astype(f32)
    return parts


def _seg_raw(x, ee, n=3, per_tile=False):
    R = x.shape[0]
    if per_tile:
        outs = []
        for c in range(4):
            tile_parts = _split(x[:, 256 * c:256 * c + 256], n)
            res = jnp.dot(jnp.concatenate(tile_parts, axis=0), ee, preferred_element_type=f32)
            acc = res[0:R]
            for i in range(1, n):
                acc = acc + res[i * R:(i + 1) * R]
            outs.append(acc)
        return jnp.concatenate(outs, axis=1)
    parts = _split(x, n)
    lhs = jnp.concatenate([p[:, 256 * c:256 * c + 256] for p in parts for c in range(4)], axis=0)
    res = jnp.dot(lhs, ee, preferred_element_type=f32)

    def tile(c):
        acc = res[c * R:(c + 1) * R]
        for i in range(1, n):
            acc = acc + res[(4 * i + c) * R:(4 * i + c + 1) * R]
        return acc

    return jnp.concatenate([tile(c) for c in range(4)], axis=1)


@jax.custom_vjp
def _seg(x, ee):
    return _seg_raw(x, ee)


def _seg_fwd(x, ee):
    return _seg_raw(x, ee), ee


def _seg_bwd(ee, g):
    return _seg_raw(g, ee), jnp.zeros_like(ee)


_seg.defvjp(_seg_fwd, _seg_bwd)


def _const_mats():
    i256 = jnp.arange(256) // RWKV_HEAD
    ee = (i256[:, None] == i256[None, :]).astype(bf16)
    head = jnp.arange(D_RWKV) // RWKV_HEAD
    lane = jnp.arange(LANES)
    red = (head[:, None] == lane[None, :]).astype(bf16)
    exp = ((lane[:, None] % RWKV_HEADS == head[None, :]) & (lane[:, None] < 2 * RWKV_HEADS)).astype(bf16)
    return ee, red, exp


def _matmul(a, b, *, mode, tm, tn, tk, name, add=None, b_layer=None, rider=None):
    bs = b.shape if b_layer is None else b.shape[1:]
    lead = () if b_layer is None else (None,)
    bidx = (lambda *t: t) if b_layer is None else (lambda *t: (b_layer,) + t)
    if mode == "nn":
        (M, K), N = a.shape, bs[1]
        tm, tn, tk = min(tm, M), min(tn, N), min(tk, K)
        a_spec = pl.BlockSpec((tm, tk), lambda i, j, k: (i, k))
        b_spec = pl.BlockSpec(lead + (tk, tn), lambda i, j, k: bidx(k, j))
        dims = (((1,), (0,)), ((), ()))
    elif mode == "nt":
        (M, K), N = a.shape, bs[0]
        tm, tn, tk = min(tm, M), min(tn, N), min(tk, K)
        a_spec = pl.BlockSpec((tm, tk), lambda i, j, k: (i, k))
        b_spec = pl.BlockSpec(lead + (tn, tk), lambda i, j, k: bidx(j, k))
        dims = (((1,), (1,)), ((), ()))
    else:
        (K, M), N = a.shape, bs[1]
        tm, tn, tk = min(tm, M), min(tn, N), min(tk, K)
        a_spec = pl.BlockSpec((tk, tm), lambda i, j, k: (k, i))
        b_spec = pl.BlockSpec(lead + (tk, tn), lambda i, j, k: bidx(k, j))
        dims = (((0,), (0,)), ((), ()))
    assert M % tm == 0 and N % tn == 0 and K % tk == 0, (M, N, K, tm, tn, tk)
    has_add = add is not None
    n_in = 3 if has_add else 2
    grid = (M // tm, N // tn, K // tk)
    rd = rider if rider is not None else _Part([], [], [], [], [], lambda *a: None)
    rd.total = grid[0] * grid[1] * grid[2]

    def body(*refs):
        a_ref, b_ref = refs[:2]
        c_ref = refs[2] if has_add else None
        rd_ins, o_ref = refs[n_in:n_in + rd.n_in], refs[n_in + rd.n_in]
        rest = refs[n_in + rd.n_in + 1:]
        i, j, k = pl.program_id(0), pl.program_id(1), pl.program_id(2)
        rd.step((i * grid[1] + j) * grid[2] + k, rd_ins, rest[:rd.n_out], rest[rd.n_out:])
        p = lax.dot_general(a_ref[...].astype(bf16), b_ref[...].astype(bf16), dims, preferred_element_type=f32)

        @pl.when(k == 0)
        def _():
            o_ref[...] = p + c_ref[...] if has_add else p

        @pl.when(k > 0)
        def _():
            o_ref[...] += p

    o_spec = pl.BlockSpec((tm, tn), lambda i, j, k: (i, j))
    in_specs = [a_spec, b_spec] + ([o_spec] if has_add else []) + rd.in_specs
    args = (a, b) + ((add,) if has_add else ()) + tuple(rd.args)
    outs = pl.pallas_call(
        body, name=name, grid=grid, in_specs=in_specs, out_specs=[o_spec] + rd.out_specs,
        out_shape=[SDS((M, N), f32)] + rd.out_shape, scratch_shapes=rd.scratch,
        compiler_params=_params(("parallel", "parallel", "arbitrary") if rider is None else ("arbitrary",) * 3, 60),
    )(*args)
    return outs[0] if rider is None else (outs[0], list(outs[1:]))


def _rwkv_core(k_in, v_in, wdad, w0, wup, a0, aup, kkw, kaw, ee, vmix):
    w_raw = w0 + jnp.dot(jnp.tanh(wdad), wup, preferred_element_type=f32)
    decay = jnp.exp(-EXP_M05 * jax.nn.sigmoid(w_raw))
    a = jax.nn.sigmoid(a0 + jnp.dot(wdad, aup, preferred_element_type=f32))
    if vmix is None:
        v = v_in
    else:
        v0, vdn, vup, vfirst = vmix
        gate = jax.nn.sigmoid(v0 + jnp.dot(jnp.dot(v_in, vdn, preferred_element_type=f32), vup,
                                           preferred_element_type=f32))
        v = v_in + (vfirst - v_in) * gate
    kk = k_in * kkw
    n2 = _seg(kk * kk, ee)
    kk = kk / jnp.maximum(jnp.sqrt(n2), 1e-12)
    k2 = k_in * (1.0 + (a - 1.0) * kaw)
    return decay, k2, v, -kk, kk * a


def _shifted(y, prev_row):
    yp = pltpu.roll(y, 1, 0)
    row = lax.broadcasted_iota(jnp.int32, y.shape, 0)
    return jnp.where(row == 0, prev_row, yp)


def _rwkv_pre_fwd(proj, mu, w0, wup, a0, aup, kkw, kaw, ee, vmix, layer):
    T = proj.shape[0]
    nb = T // TB
    has_mix = vmix is not None

    def body(*refs):
        y_ref, p8_ref, mu_ref, w0_ref, wup_ref, a0_ref, aup_ref, kkw_ref, kaw_ref, ee_ref = refs[:10]
        rest = refs[10:]
        if has_mix:
            v0_ref, vdn_ref, vup_ref, vf_ref = rest[:4]
            rest = rest[4:]
        r_o, w_o, k_o, v_o, a_o, b_o, z_o = rest
        i = pl.program_id(0)
        y = y_ref[...]
        prev = jnp.where(i == 0, 0.0, p8_ref[7:8, :])
        rw = y + mu_ref[...] * (_shifted(y, prev) - y)
        mix = (v0_ref[...], vdn_ref[...], vup_ref[...], vf_ref[...]) if has_mix else None
        dec, k2, v, av, bv = _rwkv_core(rw[:, 1024:2048], rw[:, 2048:3072], rw[:, 4096:4224], w0_ref[...],
                                        wup_ref[...], a0_ref[...], aup_ref[...], kkw_ref[...], kaw_ref[...],
                                        ee_ref[...], mix)
        r_o[...] = rw[:, 0:1024]
        w_o[...] = dec
        k_o[...] = k2
        v_o[...] = v
        a_o[...] = av
        b_o[...] = bv
        z_o[...] = rw[:, 3072:4096]

    row = lambda n: pl.BlockSpec((1, n), lambda i: (0, 0))
    full = lambda s: pl.BlockSpec(s, lambda i: (0,) * len(s))
    blk = pl.BlockSpec((TB, D_RWKV), lambda i: (i, 0))
    in_specs = [pl.BlockSpec((TB, RWKV_COLS), lambda i: (i, 0)),
                pl.BlockSpec((8, RWKV_COLS), lambda i: (jnp.maximum(i * (TB // 8) - 1, 0), 0)),
                row(RWKV_COLS), row(D_RWKV), full((LANES, D_RWKV)), row(D_RWKV), full((LANES, D_RWKV)),
                row(D_RWKV), row(D_RWKV), full((256, 256))]
    args = [proj, proj, mu, w0, wup, a0, aup, kkw, kaw, ee]
    if has_mix:
        v0, vdn, vup, vfirst = vmix
        in_specs += [row(D_RWKV), full((D_RWKV, LANES)), full((LANES, D_RWKV)), blk]
        args += [v0, vdn, vup, vfirst]
    return pl.pallas_call(
        body, name=f"rwkv_pre_fwd_l{layer}", grid=(nb,), in_specs=in_specs, out_specs=[blk] * 7,
        out_shape=[SDS((T, D_RWKV), f32)] * 7, compiler_params=_params(("parallel",), 48),
    )(*args)


def _rwkv_pre_bwd(proj, mu, w0, wup, a0, aup, kkw, kaw, ee, vmix, cots, layer):
    T = proj.shape[0]
    nb = T // TB
    has_mix = vmix is not None
    cot_list = cots["r"] + [cots["w"]] + cots["k"] + cots["v"] + [cots["a"], cots["b"], cots["z"]]
    n_r, n_k, n_v = len(cots["r"]), len(cots["k"]), len(cots["v"])
    n_cot = len(cot_list)

    def body(*refs):
        y_ref, p8_ref, mu_ref, w0_ref, wup_ref, a0_ref, aup_ref, kkw_ref, kaw_ref, ee_ref = refs[:10]
        rest = refs[10:]
        if has_mix:
            v0_ref, vdn_ref, vup_ref, vf_ref = rest[:4]
            rest = rest[4:]
        cot_refs, rest = rest[:n_cot], rest[n_cot:]
        if has_mix:
            (dproj_o, dmu_o, dw0_o, dwup_o, da0_o, daup_o, dkkw_o, dkaw_o,
             dv0_o, dvdn_o, dvup_o, dvf_o, carry) = rest
        else:
            dproj_o, dmu_o, dw0_o, dwup_o, da0_o, daup_o, dkkw_o, dkaw_o, carry = rest
        i = pl.program_id(0)
        blk_i = nb - 1 - i
        y = y_ref[...]
        prev = jnp.where(blk_i == 0, 0.0, p8_ref[7:8, :])
        yp = _shifted(y, prev)
        mu_v = mu_ref[...]
        rw = y + mu_v * (yp - y)

        def sum_refs(rs):
            acc = rs[0][...]
            for r_ in rs[1:]:
                acc = acc + r_[...]
            return acc

        c = list(cot_refs)
        g_r = sum_refs(c[:n_r]); c = c[n_r:]
        g_w = c[0][...]; c = c[1:]
        g_k = sum_refs(c[:n_k]); c = c[n_k:]
        g_v = sum_refs(c[:n_v]); c = c[n_v:]
        g_a, g_b, g_z = c[0][...], c[1][...], c[2][...]

        ee_v = ee_ref[...]
        if has_mix:
            def fn(k_in, v_in, wdad, w0_, wup_, a0_, aup_, kkw_, kaw_, v0_, vdn_, vup_, vf_):
                return _rwkv_core(k_in, v_in, wdad, w0_, wup_, a0_, aup_, kkw_, kaw_, ee_v, (v0_, vdn_, vup_, vf_))
            prim = (rw[:, 1024:2048], rw[:, 2048:3072], rw[:, 4096:4224], w0_ref[...], wup_ref[...], a0_ref[...],
                    aup_ref[...], kkw_ref[...], kaw_ref[...], v0_ref[...], vdn_ref[...], vup_ref[...], vf_ref[...])
        else:
            def fn(k_in, v_in, wdad, w0_, wup_, a0_, aup_, kkw_, kaw_):
                return _rwkv_core(k_in, v_in, wdad, w0_, wup_, a0_, aup_, kkw_, kaw_, ee_v, None)
            prim = (rw[:, 1024:2048], rw[:, 2048:3072], rw[:, 4096:4224], w0_ref[...], wup_ref[...], a0_ref[...],
                    aup_ref[...], kkw_ref[...], kaw_ref[...])
        _, vjp = jax.vjp(fn, *prim)
        gs = vjp((g_w, g_k, g_v, g_a, g_b))
        d_k, d_v, d_wdad = gs[0], gs[1], gs[2]
        g_rw = jnp.concatenate([g_r, d_k, d_v, g_z, d_wdad], axis=1)

        @pl.when(i == 0)
        def _():
            carry[...] = jnp.zeros_like(carry)
            dmu_o[...] = jnp.zeros_like(dmu_o)
            dw0_o[...] = jnp.zeros_like(dw0_o)
            dwup_o[...] = jnp.zeros_like(dwup_o)
            da0_o[...] = jnp.zeros_like(da0_o)
            daup_o[...] = jnp.zeros_like(daup_o)
            dkkw_o[...] = jnp.zeros_like(dkkw_o)
            dkaw_o[...] = jnp.zeros_like(dkaw_o)
            if has_mix:
                dv0_o[...] = jnp.zeros_like(dv0_o)
                dvdn_o[...] = jnp.zeros_like(dvdn_o)
                dvup_o[...] = jnp.zeros_like(dvup_o)

        dmu_o[...] += jnp.sum(g_rw * (yp - y), axis=0, keepdims=True)
        dw0_o[...] += gs[3]
        dwup_o[...] += gs[4]
        da0_o[...] += gs[5]
        daup_o[...] += gs[6]
        dkkw_o[...] += gs[7]
        dkaw_o[...] += gs[8]
        if has_mix:
            dv0_o[...] += gs[9]
            dvdn_o[...] += gs[10]
            dvup_o[...] += gs[11]
            dvf_o[...] = gs[12]
        gm = g_rw * mu_v
        nxt = pltpu.roll(gm, TB - 1, 0)
        rowi = lax.broadcasted_iota(jnp.int32, gm.shape, 0)
        nxt = jnp.where(rowi == TB - 1, carry[...], nxt)
        dproj_o[...] = g_rw - gm + nxt
        carry[...] = gm[0:1, :]

    rev = lambda i: (nb - 1 - i, 0)
    row = lambda n: pl.BlockSpec((1, n), lambda i: (0, 0))
    full = lambda s: pl.BlockSpec(s, lambda i: (0,) * len(s))
    blk = pl.BlockSpec((TB, D_RWKV), rev)
    in_specs = [pl.BlockSpec((TB, RWKV_COLS), rev),
                pl.BlockSpec((8, RWKV_COLS), lambda i: (jnp.maximum((nb - 1 - i) * (TB // 8) - 1, 0), 0)),
                row(RWKV_COLS), row(D_RWKV), full((LANES, D_RWKV)), row(D_RWKV), full((LANES, D_RWKV)),
                row(D_RWKV), row(D_RWKV), full((256, 256))]
    args = [proj, proj, mu, w0, wup, a0, aup, kkw, kaw, ee]
    out_specs = [pl.BlockSpec((TB, RWKV_COLS), rev), row(RWKV_COLS), row(D_RWKV), full((LANES, D_RWKV)),
                 row(D_RWKV), full((LANES, D_RWKV)), row(D_RWKV), row(D_RWKV)]
    out_shape = [SDS((T, RWKV_COLS), f32), SDS((1, RWKV_COLS), f32), SDS((1, D_RWKV), f32), SDS((LANES, D_RWKV), f32),
                 SDS((1, D_RWKV), f32), SDS((LANES, D_RWKV), f32), SDS((1, D_RWKV), f32), SDS((1, D_RWKV), f32)]
    if has_mix:
        v0, vdn, vup, vfirst = vmix
        in_specs += [row(D_RWKV), full((D_RWKV, LANES)), full((LANES, D_RWKV)), blk]
        args += [v0, vdn, vup, vfirst]
        out_specs += [row(D_RWKV), full((D_RWKV, LANES)), full((LANES, D_RWKV)), blk]
        out_shape += [SDS((1, D_RWKV), f32), SDS((D_RWKV, LANES), f32), SDS((LANES, D_RWKV), f32),
                      SDS((T, D_RWKV), f32)]
    in_specs += [blk] * n_cot
    args += cot_list
    return pl.pallas_call(
        body, name=f"rwkv_pre_bwd_l{layer}", grid=(nb,), in_specs=in_specs, out_specs=out_specs, out_shape=out_shape,
        scratch_shapes=[pltpu.VMEM((1, RWKV_COLS), f32)],
        compiler_params=_params(("arbitrary",), 60),
    )(*args)


def _pack_cols(prod_ref, red):
    res = jnp.dot(prod_ref[...].astype(bf16), red, preferred_element_type=f32)
    out = res[0:RWKV_HEAD]
    for s in range(1, SCAN_TB):
        out = out + pltpu.roll(res[s * RWKV_HEAD:(s + 1) * RWKV_HEAD], RWKV_HEADS * s, 1)
    return out


def _unpack_cols(tile_ref, exp):
    rows = []
    for s in range(SCAN_TB):
        half = tile_ref[0, :, pl.ds((s // 4) * LANES, LANES)]
        rows.append(half if s % 4 == 0 else pltpu.roll(half, LANES - 2 * RWKV_HEADS * (s % 4), 1))
    return jnp.dot(jnp.concatenate(rows, axis=0), exp, preferred_element_type=f32)


def _to_col(v):
    T = v.shape[0]
    hi = lax.reduce_precision(v, 8, 7)
    parts = jnp.stack([hi, v - hi], axis=1).astype(bf16)
    parts = parts.reshape(T // SCAN_TB, SCAN_TB, 2, RWKV_HEADS, RWKV_HEAD)
    return jnp.transpose(parts, (0, 4, 1, 2, 3)).reshape(T // SCAN_TB, RWKV_HEAD, 2 * LANES)


def _from_col(vc):
    nb = vc.shape[0]
    tiles = vc.reshape(nb, RWKV_HEAD, SCAN_TB, RWKV_HEADS)
    return jnp.transpose(tiles, (0, 2, 3, 1)).reshape(nb * SCAN_TB, D_RWKV)


def _rwkv_scan_fwd(r, w, k, vcol, a, b, ee, red, exp, layer, rider, gather=()):
    T = r.shape[0]
    nb = T // SCAN_TB
    ng = len(gather)

    def body(*refs):
        r_ref, w_ref, k_ref, v_ref, a_ref, an_ref, b_ref, ee_ref, red_ref, exp_ref = refs[:10]
        rd_ins, refs = refs[10:10 + rider.n_in], refs[10 + rider.n_in:]
        g_ins, refs = refs[:ng], refs[ng:]
        o_ref, sp_ref = refs[:2]
        rd_outs, refs = refs[2:2 + rider.n_out], refs[2 + rider.n_out:]
        g_outs, refs = refs[:ng], refs[ng:]
        s_ref, sa_ref, vb_ref, po_ref = refs[:4]
        rd_scr, g_sems = refs[4:4 + rider.n_scr], refs[4 + rider.n_scr:]
        step = pl.program_id(0)
        rider.step(step, rd_ins, rd_outs, rd_scr)

        @pl.when(step == 0)
        def _():
            s_ref[...] = jnp.zeros_like(s_ref)
            sa_ref[...] = jnp.zeros_like(sa_ref)
            if ng:
                _gather_start(g_ins, g_outs, *g_sems)

        if ng:
            @pl.when(step == (3 * nb) // 4)
            def _():
                _gather_forward(g_ins, g_outs, *g_sems)

        ee_v = ee_ref[...]
        r_b, w_b, k_b, a_b, b_b = r_ref[...], w_ref[...], k_ref[...], a_ref[...], b_ref[...]
        rowi = lax.broadcasted_iota(jnp.int32, a_b.shape, 0)
        a_nx = jnp.where(rowi == SCAN_TB - 1, an_ref[0:1, :], pltpu.roll(a_b, SCAN_TB - 1, 0))
        wa = w_b * a_nx
        dots = _seg_raw(jnp.concatenate([b_b * a_nx, k_b * a_nx], axis=0), ee_v, 3)
        beta, kappa = dots[0:SCAN_TB], dots[SCAN_TB:2 * SCAN_TB]
        vb_ref[...] = _unpack_cols(v_ref, exp_ref[...])

        for t in range(SCAN_TB):
            row = lambda x: x[t:t + 1, :]
            blk = pl.ds(t * RWKV_HEAD, RWKV_HEAD)
            sp = s_ref[...]
            sa = sa_ref[...]
            sp_ref[t, 0] = sp
            sp_ref[t, 1] = sa
            vb = vb_ref[blk, :]
            ahead = _seg_raw(sp * row(wa), ee_v, SCAN_PARTS, per_tile=True)
            sn = sp * row(w_b) + sa * row(b_b) + vb * row(k_b)
            s_ref[...] = sn
            sa_ref[...] = ahead + sa * row(beta) + vb * row(kappa)
            po_ref[blk, :] = sn * row(r_b)
        o_ref[0] = _pack_cols(po_ref, red_ref[...])

        if ng:
            @pl.when(step == nb - 1)
            def _():
                _gather_finish(g_ins, g_outs, *g_sems)

    rows = pl.BlockSpec((SCAN_TB, D_RWKV), lambda i: (i, 0))
    rows_next = pl.BlockSpec((SCAN_TB, D_RWKV), lambda i: (jnp.minimum(i + 1, nb - 1), 0))
    col = pl.BlockSpec((1, RWKV_HEAD, LANES), lambda i: (i, 0, 0))
    col2 = pl.BlockSpec((1, RWKV_HEAD, 2 * LANES), lambda i: (i, 0, 0))
    full = lambda s: pl.BlockSpec(s, lambda i: (0,) * len(s))
    outs = pl.pallas_call(
        body, name=f"rwkv_scan_fwd_l{layer}", grid=(nb,),
        in_specs=[rows, rows, rows, col2, rows, rows_next, rows, full((256, 256)), full((D_RWKV, LANES)),
                  full((LANES, D_RWKV))] + rider.in_specs + [HBM_SPEC] * ng,
        out_specs=[col, pl.BlockSpec((SCAN_TB, 2, RWKV_HEAD, D_RWKV), lambda i: (i, 0, 0, 0))] + rider.out_specs
        + [HBM_SPEC] * ng,
        out_shape=[SDS((nb, RWKV_HEAD, LANES), f32), SDS((T, 2, RWKV_HEAD, D_RWKV), f32)] + rider.out_shape
        + _gather_out_shapes(gather),
        scratch_shapes=[pltpu.VMEM((RWKV_HEAD, D_RWKV), f32), pltpu.VMEM((RWKV_HEAD, D_RWKV), f32),
                        pltpu.VMEM((SCAN_TB * RWKV_HEAD, D_RWKV), f32),
                        pltpu.VMEM((SCAN_TB * RWKV_HEAD, D_RWKV), f32)] + rider.scratch
        + (_gather_sems(ng) if ng else []),
        compiler_params=_params(("arbitrary",), 52),
    )(r, w, k, vcol, a, a, b, ee, red, exp, *rider.args, *gather)
    return outs[0], outs[1], list(outs[2:2 + rider.n_out]), list(outs[2 + rider.n_out:])


def _rwkv_scan_bwd(r, w, k, vcol, a, b, sprev, docol, ee, red, exp, layer, send=()):
    T = r.shape[0]
    nb = T // SCAN_TB
    ns = len(send)

    def body(*refs):
        r_ref, w_ref, k_ref, v_ref, a_ref, b_ref, sp_ref, do_ref, ee_ref, red_ref, exp_ref = refs[:11]
        c_ins, refs = refs[11:11 + ns], refs[11 + ns:]
        dr_o, dw_o, dk_o, da_o, db_o, dv_o = refs[:6]
        c_outs, refs = refs[6:6 + ns], refs[6 + ns:]
        ds_ref, snext_ref, vb_ref, dob_ref, pdv_ref = refs[:5]
        c_sems = refs[5:]
        if ns:
            @pl.when(pl.program_id(0) == 0)
            def _():
                for cp in _chips_copies(c_ins, c_outs, *c_sems):
                    cp.start()

        ee_v = ee_ref[...]
        csum = lambda x: jnp.sum(x, axis=0, keepdims=True)
        nrow = SCAN_TB * RWKV_HEAD
        last = pl.ds(nrow - RWKV_HEAD, RWKV_HEAD)
        r_b, w_b, k_b, a_b, b_b = r_ref[...], w_ref[...], k_ref[...], a_ref[...], b_ref[...]
        vb_ref[...] = _unpack_cols(v_ref, exp_ref[...])
        dob_ref[...] = _unpack_cols(do_ref, exp_ref[...])

        @pl.when(pl.program_id(0) == 0)
        def _():
            ds_ref[...] = jnp.zeros_like(ds_ref)
            snext_ref[...] = (sp_ref[SCAN_TB - 1, 0] * w_b[SCAN_TB - 1:SCAN_TB, :]
                              + sp_ref[SCAN_TB - 1, 1] * b_b[SCAN_TB - 1:SCAN_TB, :]
                              + vb_ref[last, :] * k_b[SCAN_TB - 1:SCAN_TB, :])

        for s in range(SCAN_TB):
            t = SCAN_TB - 1 - s
            row = lambda x: x[t:t + 1, :]
            blk = pl.ds(t * RWKV_HEAD, RWKV_HEAD)
            sp, sa = sp_ref[t, 0], sp_ref[t, 1]
            sn = snext_ref[...] if s == 0 else sp_ref[t + 1, 0]
            vb, dob = vb_ref[blk, :], dob_ref[blk, :]
            ds = ds_ref[...] + dob * row(r_b)
            dsa = _seg_raw(ds * row(b_b), ee_v, SCAN_PARTS, per_tile=True)
            ds_ref[...] = ds * row(w_b) + dsa * row(a_b)
            dr_o[t:t + 1, :] = csum(sn * dob)
            pdv_ref[blk, :] = ds * row(k_b)
            dk_o[t:t + 1, :] = csum(ds * vb)
            db_o[t:t + 1, :] = csum(ds * sa)
            dw_o[t:t + 1, :] = csum(ds * sp)
            da_o[t:t + 1, :] = csum(sp * dsa)
        snext_ref[...] = sp_ref[0, 0]
        dv_o[0] = _pack_cols(pdv_ref, red_ref[...])

        if ns:
            @pl.when(pl.program_id(0) == nb - 1)
            def _():
                for cp in _chips_copies(c_ins, c_outs, *c_sems):
                    cp.wait()

    rows = pl.BlockSpec((SCAN_TB, D_RWKV), lambda i: (nb - 1 - i, 0))
    col = pl.BlockSpec((1, RWKV_HEAD, LANES), lambda i: (nb - 1 - i, 0, 0))
    col2 = pl.BlockSpec((1, RWKV_HEAD, 2 * LANES), lambda i: (nb - 1 - i, 0, 0))
    st = pl.BlockSpec((SCAN_TB, 2, RWKV_HEAD, D_RWKV), lambda i: (nb - 1 - i, 0, 0, 0))
    full = lambda s: pl.BlockSpec(s, lambda i: (0,) * len(s))
    big = pltpu.VMEM((SCAN_TB * RWKV_HEAD, D_RWKV), f32)
    outs = pl.pallas_call(
        body, name=f"rwkv_scan_bwd_l{layer}", grid=(nb,),
        in_specs=[rows, rows, rows, col2, rows, rows, st, col2, full((256, 256)), full((D_RWKV, LANES)),
                  full((LANES, D_RWKV))] + [HBM_SPEC] * ns,
        out_specs=[rows] * 5 + [col] + [HBM_SPEC] * ns,
        out_shape=[SDS((T, D_RWKV), f32)] * 5 + [SDS((nb, RWKV_HEAD, LANES), f32)] + _chips_out_shapes(send),
        scratch_shapes=[pltpu.VMEM((RWKV_HEAD, D_RWKV), f32)] * 2 + [big] * 3 + (_chips_sems(ns) if ns else []),
        compiler_params=_params(("arbitrary",), 56),
    )(r, w, k, vcol, a, b, sprev, docol, ee, red, exp, *send)
    return (*outs[:6], list(outs[6:]))


def _post_core(o, r, k2, v, z, gw, gb, rk, ee):
    inv = 1.0 / RWKV_HEAD
    mu = _seg(o, ee) * inv
    d = o - mu
    var = _seg(d * d, ee) * inv
    on = d * lax.rsqrt(var + GN_EPS) * gw + gb
    bonus = _seg(r * k2 * rk, ee) * v
    return (on + bonus) * jax.nn.silu(z)


def _rwkv_post_fwd(o, r, k2, v, z, gw, gb, rk, ee, cat, layer):
    T = o.shape[0]

    def body(o_ref, r_ref, k_ref, v_ref, z_ref, gw_ref, gb_ref, rk_ref, ee_ref, cat_in, out_ref):
        del cat_in
        out_ref[...] = _post_core(o_ref[...], r_ref[...], k_ref[...], v_ref[...], z_ref[...], gw_ref[...],
                                  gb_ref[...], rk_ref[...], ee_ref[...])

    blk = pl.BlockSpec((TB_WIDE, D_RWKV), lambda i: (i, 0))
    row = pl.BlockSpec((1, D_RWKV), lambda i: (0, 0))
    return pl.pallas_call(
        body, name=f"rwkv_post_fwd_l{layer}", grid=(T // TB_WIDE,),
        in_specs=[blk] * 5 + [row] * 3 + [pl.BlockSpec((256, 256), lambda i: (0, 0)), HBM_SPEC],
        out_specs=blk, out_shape=SDS((T, D_MODEL), f32), input_output_aliases={9: 0},
        compiler_params=_params(("parallel",), 40),
    )(o, r, k2, v, z, gw, gb, rk, ee, cat)


def _rwkv_post_bwd(o, r, k2, v, z, gw, gb, rk, ee, dcat, layer):
    T = o.shape[0]

    def body(o_ref, r_ref, k_ref, v_ref, z_ref, gw_ref, gb_ref, rk_ref, ee_ref, g_ref,
             do_o, dr_o, dk_o, dv_o, dz_o, dgw_o, dgb_o, drk_o):
        ee_v = ee_ref[...]
        fn = lambda o_, r_, k_, v_, z_, gw_, gb_, rk_: _post_core(o_, r_, k_, v_, z_, gw_, gb_, rk_, ee_v)
        _, vjp = jax.vjp(fn, o_ref[...], r_ref[...], k_ref[...], v_ref[...], z_ref[...], gw_ref[...], gb_ref[...],
                         rk_ref[...])
        gs = vjp(g_ref[...])
        do_o[...], dr_o[...], dk_o[...], dv_o[...], dz_o[...] = gs[:5]

        @pl.when(pl.program_id(0) == 0)
        def _():
            dgw_o[...] = jnp.zeros_like(dgw_o)
            dgb_o[...] = jnp.zeros_like(dgb_o)
            drk_o[...] = jnp.zeros_like(drk_o)

        dgw_o[...] += gs[5]
        dgb_o[...] += gs[6]
        drk_o[...] += gs[7]

    blk = pl.BlockSpec((TB_WIDE, D_RWKV), lambda i: (i, 0))
    row = pl.BlockSpec((1, D_RWKV), lambda i: (0, 0))
    return pl.pallas_call(
        body, name=f"rwkv_post_bwd_l{layer}", grid=(T // TB_WIDE,),
        in_specs=[blk] * 5 + [row] * 3 + [pl.BlockSpec((256, 256), lambda i: (0, 0)), blk],
        out_specs=[blk] * 5 + [row] * 3,
        out_shape=[SDS((T, D_RWKV), f32)] * 5 + [SDS((1, D_RWKV), f32)] * 3,
        compiler_params=_params(("arbitrary",), 48),
    )(o, r, k2, v, z, gw, gb, rk, ee, dcat)


def _hgrn_chunk(qr, fr, ii, z, S, lbl, gw, layer):
    L = CHUNK
    G = qr.shape[1] // HG_HEAD
    hd = lambda x, g: x[..., g * HG_HEAD:(g + 1) * HG_HEAD]
    heads = lambda f: jnp.concatenate([f(g) for g in range(G)], axis=-1)
    m = jnp.max(lbl, axis=0, keepdims=True)
    e = jnp.exp(lbl - m)
    sm = e / jnp.sum(e, axis=0, keepdims=True)
    if layer == 0:
        lb = sm[0:1] - sm[0:1]
    else:
        lb = (sm[0:1] + sm[1:2]) - sm[0:1]
    q = jax.nn.silu(qr)
    log_lb = jnp.log(jnp.maximum(lb, LB_FLOOR))
    lf = jnp.logaddexp(log_lb, jnp.log1p(-lb) + jax.nn.log_sigmoid(fr))
    k = (1.0 - lb) * jax.nn.sigmoid(-fr)
    ti = lax.broadcasted_iota(jnp.int32, (L, L), 0)
    si = lax.broadcasted_iota(jnp.int32, (L, L), 1)
    ltri = (si <= ti).astype(f32)
    b = jnp.dot(ltri, lf, precision=HIGHEST, preferred_element_type=f32)
    btot = jnp.dot(jnp.ones((L, L), f32), lf, precision=HIGHEST, preferred_element_type=f32)
    qd = q * jnp.exp(b)
    o = heads(lambda g: jnp.dot(hd(qd, g), S[g], preferred_element_type=f32))
    nsub = L // HSUB
    t3 = lax.broadcasted_iota(jnp.int32, (HSUB, HSUB, G * HG_HEAD), 0)
    s3 = lax.broadcasted_iota(jnp.int32, (HSUB, HSUB, G * HG_HEAD), 1)
    causal = s3 <= t3
    nt = (((1,), (1,)), ((), ()))
    tn = (((0,), (0,)), ((), ()))
    rows = []
    for I in range(nsub):
        sl = slice(I * HSUB, (I + 1) * HSUB)
        bI, qI, kI, iI = b[sl], q[sl], k[sl], ii[sl]
        d3 = bI[:, None, :] - bI[None, :, :]
        dec = jnp.where(causal, jnp.exp(jnp.where(causal, d3, 0.0)), 0.0)
        prod = qI[:, None, :] * dec * kI[None, :, :]
        oI = heads(lambda g: jnp.dot(jnp.sum(hd(prod, g), axis=-1), hd(iI, g), preferred_element_type=f32))
        for J in range(I):
            sj = slice(J * HSUB, (J + 1) * HSUB)
            bm = b[(J + 1) * HSUB - 1:(J + 1) * HSUB]
            qs = qI * jnp.exp(bI - bm)
            ks = k[sj] * jnp.exp(bm - b[sj])
            iJ = ii[sj]
            oI = oI + heads(lambda g: jnp.dot(
                lax.dot_general(hd(qs, g), hd(ks, g), nt, preferred_element_type=f32), hd(iJ, g),
                preferred_element_type=f32))
        rows.append(oI)
    o = o + jnp.concatenate(rows, axis=0)
    k_dec = k * jnp.exp(btot - b)
    ones = jnp.ones((L, HG_HEAD), f32)
    s_new = jnp.concatenate([
        (S[g] * jnp.exp(lax.dot_general(hd(lf, g), ones, tn, precision=HIGHEST, preferred_element_type=f32))
         + lax.dot_general(hd(k_dec, g), hd(ii, g), tn, preferred_element_type=f32))[None] for g in range(G)], axis=0)
    ms = heads(lambda g: jnp.broadcast_to(jnp.mean(hd(o * o, g), axis=-1, keepdims=True), (L, HG_HEAD)))
    out = o * lax.rsqrt(ms + RMS_EPS) * gw * jax.nn.silu(z)
    return out, s_new


class _Part:
    def __init__(self, args, in_specs, out_specs, out_shape, scratch, step):
        self.args, self.in_specs, self.out_specs, self.out_shape = args, in_specs, out_specs, out_shape
        self.scratch, self.step, self.total = scratch, step, None
        self.n_in, self.n_out, self.n_scr = len(args), len(out_shape), len(scratch)


def _run_part(part, steps, name):
    part.total = steps

    def body(*refs):
        ins, outs = refs[:part.n_in], refs[part.n_in:part.n_in + part.n_out]
        part.step(pl.program_id(0), ins, outs, refs[part.n_in + part.n_out:])

    return pl.pallas_call(body, name=name, grid=(steps,), in_specs=part.in_specs, out_specs=part.out_specs,
                          out_shape=part.out_shape, scratch_shapes=part.scratch,
                          compiler_params=_params(("arbitrary",), 48))(*part.args)


def _hg_index(T, rev, group=1):
    nc = T // CHUNK
    width = group * HG_HEAD
    head = lambda i: i // nc
    chunk = (lambda i: nc - 1 - i % nc) if rev else (lambda i: i % nc)
    first = lambda i: i % nc == 0
    pj = lambda off, g: pl.BlockSpec((CHUNK, HG_HEAD),
                                     lambda i: (chunk(i), RW_BLKS + off + group * head(i) + g))
    pspecs = [pj(kind * HG_HEADS, g) for kind in range(4) for g in range(group)]
    per_head = lambda rows: pl.BlockSpec((rows, width), lambda i: (0, head(i)))
    return nc, head, chunk, first, pspecs, per_head


def _hgrn_fwd_part(proj, lbl, gw, layer):
    T = proj.shape[0]
    nc, head, chunk, first, pspecs, per_head = _hg_index(T, False)

    def step(i, ins, outs, scr):
        q_ref, f_ref, i_ref, z_ref, lbl_ref, gw_ref = ins
        out_ref, sh_ref = outs
        s_ref, = scr

        @pl.when(first(i))
        def _():
            s_ref[...] = jnp.zeros_like(s_ref)

        s0 = s_ref[...]
        sh_ref[:, 0] = s0
        out, s_new = _hgrn_chunk(q_ref[...], f_ref[...], i_ref[...], z_ref[...], s0, lbl_ref[...], gw_ref[...], layer)
        out_ref[...] = out
        s_ref[...] = s_new

    return _Part(
        [proj, proj, proj, proj, lbl, gw], pspecs + [per_head(2), per_head(1)],
        [pl.BlockSpec((CHUNK, HG_HEAD), lambda i: (chunk(i), HG_HEADS + head(i))),
         pl.BlockSpec((1, 1, HG_HEAD, HG_HEAD), lambda i: (head(i), chunk(i), 0, 0))],
        [SDS((T, D_MODEL), f32), SDS((HG_HEADS, nc, HG_HEAD, HG_HEAD), f32)],
        [pltpu.VMEM((1, HG_HEAD, HG_HEAD), f32)], step)


HG_GROUP = 2


def _hgrn_bwd_part(proj, shist, dcat, lbl, gw, layer):
    T = proj.shape[0]
    G = HG_GROUP
    nc, head, chunk, first, pspecs, per_head = _hg_index(T, True, G)

    def step(i, ins, outs, scr):
        p_refs, (lbl_ref, gw_ref, sh_ref, g_ref) = ins[:4 * G], ins[4 * G:]
        dq_o, df_o, di_o, dz_o, dlbl_o, dgw_o = outs
        ds_ref, = scr

        @pl.when(first(i))
        def _():
            ds_ref[...] = jnp.zeros_like(ds_ref)
            dlbl_o[...] = jnp.zeros_like(dlbl_o)
            dgw_o[...] = jnp.zeros_like(dgw_o)

        fn = functools.partial(_hgrn_chunk, layer=layer)
        q, f, ii, z = (jnp.concatenate([p_refs[kind * G + g][...] for g in range(G)], axis=1) for kind in range(4))
        _, vjp = jax.vjp(fn, q, f, ii, z, sh_ref[:, 0], lbl_ref[...], gw_ref[...])
        dq, df, di, dz, ds, dl, dg = vjp((g_ref[...], ds_ref[...]))
        dq_o[...], df_o[...], di_o[...], dz_o[...] = dq, df, di, dz
        ds_ref[...] = ds
        dlbl_o[...] += dl
        dgw_o[...] += dg

    ospec = pl.BlockSpec((CHUNK, G * HG_HEAD), lambda i: (chunk(i), head(i)))
    return _Part(
        [proj] * (4 * G) + [lbl, gw, shist, dcat],
        pspecs + [per_head(2), per_head(1),
                  pl.BlockSpec((G, 1, HG_HEAD, HG_HEAD), lambda i: (head(i), chunk(i), 0, 0)),
                  pl.BlockSpec((CHUNK, G * HG_HEAD), lambda i: (chunk(i), HG_HEADS // G + head(i)))],
        [ospec] * 4 + [per_head(2), per_head(1)],
        [SDS((T, D_HGRN), f32)] * 4 + [SDS((2, D_HGRN), f32), SDS((1, D_HGRN), f32)],
        [pltpu.VMEM((G, HG_HEAD, HG_HEAD), f32)], step)


def _ln_core(h, y, w, b):
    u = ALPHA * h + y
    mu = jnp.mean(u, axis=-1, keepdims=True)
    d = u - mu
    var = jnp.mean(d * d, axis=-1, keepdims=True)
    return d * lax.rsqrt(var + LN_EPS) * w + b


def _ln_fwd(h, y, w, b, layer):
    T = h.shape[0]

    def body(h_ref, y_ref, w_ref, b_ref, o_ref):
        o_ref[...] = _ln_core(h_ref[...], y_ref[...], w_ref[...], b_ref[...])

    blk = pl.BlockSpec((TB_WIDE, D_MODEL), lambda i: (i, 0))
    row = pl.BlockSpec((1, D_MODEL), lambda i: (0, 0))
    return pl.pallas_call(body, name=f"ln_fwd_l{layer}", grid=(T // TB_WIDE,), in_specs=[blk, blk, row, row],
                          out_specs=blk, out_shape=SDS((T, D_MODEL), f32),
                          compiler_params=_params(("parallel",), 40))(h, y, w, b)


def _ln_bwd(h, y, w, b, g, layer):
    T = h.shape[0]

    def body(h_ref, y_ref, w_ref, b_ref, g_ref, dh_o, dy_o, dw_o, db_o):
        _, vjp = jax.vjp(_ln_core, h_ref[...], y_ref[...], w_ref[...], b_ref[...])
        dh, dy, dw, db = vjp(g_ref[...])
        dh_o[...] = dh
        dy_o[...] = dy.astype(bf16)

        @pl.when(pl.program_id(0) == 0)
        def _():
            dw_o[...] = jnp.zeros_like(dw_o)
            db_o[...] = jnp.zeros_like(db_o)

        dw_o[...] += dw
        db_o[...] += db

    blk = pl.BlockSpec((TB_WIDE, D_MODEL), lambda i: (i, 0))
    row = pl.BlockSpec((1, D_MODEL), lambda i: (0, 0))
    return pl.pallas_call(body, name=f"ln_bwd_l{layer}", grid=(T // TB_WIDE,), in_specs=[blk, blk, row, row, blk],
                          out_specs=[blk, blk, row, row],
                          out_shape=[SDS((T, D_MODEL), f32), SDS((T, D_MODEL), bf16)] + [SDS((1, D_MODEL), f32)] * 2,
                          compiler_params=_params(("arbitrary",), 48))(h, y, w, b, g)


def _loss_head(hout, target):
    T = hout.shape[0]

    def body(h_ref, t_ref, loss_o, g_o):
        err = h_ref[...] - t_ref[...]
        g_o[...] = err * (1.0 / D_MODEL)

        @pl.when(pl.program_id(0) == 0)
        def _():
            loss_o[...] = jnp.zeros_like(loss_o)

        part = jnp.sum(jnp.sum(err * err, axis=-1, keepdims=True) * (1.0 / D_MODEL), axis=0, keepdims=True)
        loss_o[...] += 0.5 * part

    blk = pl.BlockSpec((TB_WIDE, D_MODEL), lambda i: (i, 0))
    return pl.pallas_call(body, name="loss_head", grid=(T // TB_WIDE,), in_specs=[blk, blk],
                          out_specs=[pl.BlockSpec((1, 1), lambda i: (0, 0)), blk],
                          out_shape=[SDS((1, 1), f32), SDS((T, D_MODEL), f32)],
                          compiler_params=_params(("arbitrary",), 40))(hout, target)


def _pad_rows(m, lo, total):
    return jnp.pad(m, ((0, 0), (lo, total - lo - m.shape[1]), (0, 0)))


def _device_step(x, target, w_in0, later_shards, assemble_later, on_grads, on_swapped, on_recv, on_small_grads,
                 shift_mu, w_decay0, w_decay_up, a0, a_up, k_k, k_a, r_k, ln_x_w, ln_x_b,
                 v_mix0, v_mix_down, v_mix_up, lb_logits, g_norm_w, ln_w, ln_b):
    T = x.shape[0]
    ee, red, exp = _const_mats()
    wup_pad = _pad_rows(w_decay_up, 0, LANES)
    aup_pad = _pad_rows(a_up, LORA, LANES)
    vdn_pad = jnp.pad(v_mix_down, ((0, 0), (0, 0), (0, LANES - VRES)))
    vup_pad = _pad_rows(v_mix_up, 0, LANES)
    row = lambda p, l: p[l][None, :]

    def vmix_of(l, vfirst):
        if l == 0:
            return None
        return (row(v_mix0, l - 1), vdn_pad[l - 1], vup_pad[l - 1], vfirst)

    h = x
    saved = []
    vfirst = None
    w_in = [w_in0, None]
    w_out = None
    for l in range(DEPTH):
        hb = h.astype(bf16)
        proj = _matmul(hb, w_in[l], mode="nt", tm=1024, tn=1664, tk=D_MODEL, name=f"proj_fwd_l{l}")
        pre_args = (proj, row(shift_mu, l), row(w_decay0, l), wup_pad[l], row(a0, l), aup_pad[l], row(k_k, l),
                    row(k_a, l), ee, vmix_of(l, vfirst))
        r, w, k2, v, av, bv, z = _rwkv_pre_fwd(*pre_args, layer=l)
        if l == 0:
            vfirst = v
        vcol = _to_col(v)
        ocol, sprev, (cat, shist), gathered = _rwkv_scan_fwd(
            r, w, k2, vcol, av, bv, ee, red, exp, l, _hgrn_fwd_part(proj, lb_logits, row(g_norm_w, l), l),
            gather=later_shards if l == 0 else [])
        if l == 0:
            w_in[1], w_out = assemble_later(gathered)
        o = _from_col(ocol)
        post_args = (o, r, k2, v, z, row(ln_x_w, l), row(ln_x_b, l), row(r_k, l), ee)
        cat = _rwkv_post_fwd(*post_args, cat, layer=l)
        catb = cat.astype(bf16)
        y = _matmul(catb, w_out, b_layer=l, mode="nn", tm=1024, tn=1024, tk=D_MODEL, name=f"out_fwd_l{l}")
        h_new = _ln_fwd(h, y, row(ln_w, l), row(ln_b, l), l)
        saved.append(dict(h=h, hb=hb, proj=proj, pre_args=pre_args, post_args=post_args,
                          scan=(r, w, k2, vcol, av, bv, sprev), cat=catb, shist=shist, y=y))
        h = h_new

    loss_sum, g = _loss_head(h, target)

    grads = {n: [None] * DEPTH for n in ("w_in", "w_out", "shift_mu", "w_decay0", "w_decay_up", "a0", "a_up", "k_k",
                                         "k_a", "r_k", "ln_x_w", "ln_x_b", "g_norm_w", "ln_w", "ln_b")}
    dlbl = []
    dvfirst = None
    to_send = []
    for l in reversed(range(DEPTH)):
        s = saved[l]
        dh_res, dy, dlnw, dlnb = _ln_bwd(s["h"], s["y"], row(ln_w, l), row(ln_b, l), g, l)
        grads["ln_w"][l], grads["ln_b"][l] = dlnw[0], dlnb[0]
        dcat = _matmul(dy, w_out, b_layer=l, mode="nt", tm=1024, tn=1024, tk=D_MODEL, name=f"out_bwd_dx_l{l}")
        grads["w_out"][l] = _matmul(s["cat"], dy, mode="tn", tm=1024, tn=1024, tk=512, name=f"out_bwd_dw_l{l}")
        do, dr_p, dk_p, dv_p, dz, dgw, dgb, drk = _rwkv_post_bwd(*s["post_args"], dcat, layer=l)
        grads["ln_x_w"][l], grads["ln_x_b"][l], grads["r_k"][l] = dgw[0], dgb[0], drk[0]
        r, w, k2, vcol, av, bv, sprev = s["scan"]
        dr_s, dw_s, dk_s, da_s, db_s, dvcol, arrived = _rwkv_scan_bwd(
            r, w, k2, vcol, av, bv, sprev, _to_col(do), ee, red, exp, l, send=to_send if l == 0 else [])
        dq, df, di, dzh, dl, dgn = _run_part(
            _hgrn_bwd_part(s["proj"], s["shist"], dcat, lb_logits, row(g_norm_w, l), l),
            (T // CHUNK) * HG_HEADS // HG_GROUP, f"hgrn_bwd_l{l}")
        if l == 0:
            on_recv(1, arrived)
        v_cots = [dv_p, _from_col(dvcol)] + ([dvfirst] if (l == 0 and dvfirst is not None) else [])
        cots = dict(r=[dr_p, dr_s], w=dw_s, k=[dk_p, dk_s], v=v_cots, a=da_s, b=db_s, z=dz)
        outs = _rwkv_pre_bwd(*s["pre_args"], cots, layer=l)
        dproj, dmu, dw0, dwup, da0, daup, dkkw, dkaw = outs[:8]
        grads["shift_mu"][l], grads["w_decay0"][l], grads["a0"][l] = dmu[0], dw0[0], da0[0]
        grads["k_k"][l], grads["k_a"][l] = dkkw[0], dkaw[0]
        grads["w_decay_up"][l], grads["a_up"][l] = dwup[:LORA], daup[LORA:]
        if l > 0:
            dv0, dvdn, dvup, dvfirst = outs[8:]
            g_vmix = (dv0, dvdn[:, :VRES][None], dvup[:VRES][None])
        dlbl.append(dl)
        grads["g_norm_w"][l] = dgn[0]
        dproj = jnp.concatenate([dproj, dq, df, di, dzh], axis=1).astype(bf16)
        dw_args = dict(mode="tn", tm=1664, tn=1024, tk=512, name=f"proj_bwd_dw_l{l}")
        dx_args = dict(mode="nn", tm=1024, tn=1024, tk=1664, name=f"proj_bwd_dx_l{l}", add=dh_res)
        if l == 1:
            grads["w_in"][l] = _matmul(dproj, s["hb"], **dw_args)
            g, swapped = _matmul(dproj, w_in[l], rider=_swap_part(on_grads(l, grads["w_in"][l], grads["w_out"][l])),
                                 **dx_args)
            to_send = on_swapped(l, swapped)
        else:
            small = {n: jnp.stack(v) for n, v in grads.items() if n not in ("w_in", "w_out")}
            small["lb_logits"] = dlbl[0] + dlbl[1]
            small["v_mix0"], small["v_mix_down"], small["v_mix_up"] = g_vmix
            grads["w_in"][l], small_out = _matmul(dproj, s["hb"], rider=on_small_grads(small), **dw_args)
            swapped = _run_part(_swap_part(on_grads(l, grads["w_in"][l], grads["w_out"][l])), 1, "rs_swap_halves_l0")
            g, arrived = _matmul(dproj, w_in[l], rider=_chips_part(on_swapped(l, swapped)), **dx_args)
            on_recv(0, arrived)
    small["w_in"], small["w_out"] = grads["w_in"], grads["w_out"]
    return loss_sum, g, small, small_out


CHIP_COMBOS = ((1, 0), (0, 1), (1, 1))
HBM_SPEC = pl.BlockSpec(memory_space=pl.ANY)


def _mesh_pos():
    return lax.axis_index("x"), lax.axis_index("y"), lax.axis_index("c")


def _flip(v, d):
    return 1 - v if d else v


def _half(ref, dim, cc, hs):
    idx = [slice(None)] * len(ref.shape)
    idx[dim] = pl.ds(cc * hs, hs)
    return ref.at[tuple(idx)]


def _gather_shards(arrs):
    n = len(arrs)

    def body(*refs):
        ins, outs = refs[:n], refs[n:2 * n]
        sems = refs[2 * n:]
        _gather_start(ins, outs, *sems)
        _gather_forward(ins, outs, *sems)
        _gather_finish(ins, outs, *sems)

    return pl.pallas_call(
        body, name="gather_shards", in_specs=[HBM_SPEC] * n, out_specs=[HBM_SPEC] * n,
        out_shape=_gather_out_shapes(arrs), scratch_shapes=_gather_sems(n),
    )(*arrs)


def _gather_out_shapes(arrs):
    return [SDS((4,) + a.shape, a.dtype) for a in arrs]


def _gather_sems(n):
    return [pltpu.SemaphoreType.DMA((n, 6)), pltpu.SemaphoreType.DMA((n, 6)), pltpu.SemaphoreType.DMA((n,))]


def _gather_copies(kind, ins, outs, send_sems, recv_sems, local_sems):
    x, y, c = _mesh_pos()
    kc = 2 * x + y
    cps = []
    for a in range(len(ins)):
        if kind == "local":
            cps.append(pltpu.make_async_copy(ins[a], outs[a].at[kc], local_sems.at[a]))
            continue
        for j, (dx, dy) in enumerate(CHIP_COMBOS):
            px, py = _flip(x, dx), _flip(y, dy)
            kj = 2 * px + py
            src, dst, s, to = {
                "sends": (ins[a].at[c], outs[a].at[kc, c], j, (px, py, c)),
                "landed": (outs[a].at[kj, c], outs[a].at[kj, c], j, (px, py, c)),
                "forwards": (outs[a].at[kj, c], outs[a].at[kj, c], 3 + j, (x, y, 1 - c)),
                "passed": (outs[a].at[kj, 1 - c], outs[a].at[kj, 1 - c], 3 + j, (x, y, 1 - c)),
            }[kind]
            cps.append(pltpu.make_async_remote_copy(
                src_ref=src, dst_ref=dst, send_sem=send_sems.at[a, s], recv_sem=recv_sems.at[a, s], device_id=to,
                device_id_type=MESH))
    return cps


def _gather_start(*refs):
    for cp in _gather_copies("local", *refs) + _gather_copies("sends", *refs):
        cp.start()


def _gather_forward(*refs):
    for arrival, fw in zip(_gather_copies("landed", *refs), _gather_copies("forwards", *refs)):
        arrival.wait_recv()
        fw.start()


def _gather_finish(*refs):
    for cp in _gather_copies("passed", *refs):
        cp.wait_recv()
    for cp in _gather_copies("sends", *refs) + _gather_copies("forwards", *refs):
        cp.wait_send()
    for cp in _gather_copies("local", *refs):
        cp.wait()


DEV_COMBOS = tuple((dx, dy, dc) for dx in (0, 1) for dy in (0, 1) for dc in (0, 1))[1:]


def _allreduce_part(buf):
    R = buf.shape[0]
    whole = pl.BlockSpec((R, LANES), lambda *_: (0, 0))

    def copies(x_ref, slots, send_sems, recv_sems):
        x, y, c = _mesh_pos()
        return [pltpu.make_async_remote_copy(
            src_ref=x_ref, dst_ref=slots.at[k + 1], send_sem=send_sems.at[k], recv_sem=recv_sems.at[k],
            device_id=(_flip(x, dx), _flip(y, dy), _flip(c, dc)), device_id_type=MESH)
            for k, (dx, dy, dc) in enumerate(DEV_COMBOS)]

    def step(i, ins, outs, scr):
        x_ref, = ins
        o_ref, = outs
        slots = scr[0]

        @pl.when(i == 0)
        def _():
            for cp in copies(x_ref, *scr):
                cp.start()
            slots[0] = x_ref[...]

        @pl.when(i == part.total - 1)
        def _():
            x, y, c = _mesh_pos()
            me = 4 * x + 2 * y + c
            cps = copies(x_ref, *scr)
            for cp in cps:
                cp.wait_recv()
            acc = slots[jnp.bitwise_xor(me, 0)]
            for d in range(1, 8):
                acc = acc + slots[jnp.bitwise_xor(me, d)]
            o_ref[...] = acc
            for cp in cps:
                cp.wait_send()

    part = _Part([buf], [whole], [whole], [SDS((R, LANES), f32)],
                 [pltpu.VMEM((8, R, LANES), f32), pltpu.SemaphoreType.DMA((7,)), pltpu.SemaphoreType.DMA((7,))], step)
    return part


def _chips_part(slabs):
    n = len(slabs)

    def step(i, ins, outs, scr):
        if not n:
            return

        @pl.when(i == 0)
        def _():
            for cp in _chips_copies(ins, outs, *scr):
                cp.start()

        @pl.when(i == part.total - 1)
        def _():
            for cp in _chips_copies(ins, outs, *scr):
                cp.wait()

    part = _Part(list(slabs), [HBM_SPEC] * n, [HBM_SPEC] * n, _chips_out_shapes(slabs),
                 _chips_sems(n) if n else [], step)
    return part


def _swap_part(gs):
    n = len(gs)

    def copies(ins, outs, send_sems, recv_sems):
        x, y, c = _mesh_pos()
        return [pltpu.make_async_remote_copy(
            src_ref=_half(ins[a], 1, 1 - c, ins[a].shape[1] // 2), dst_ref=outs[a], send_sem=send_sems.at[a],
            recv_sem=recv_sems.at[a], device_id=(x, y, 1 - c), device_id_type=MESH) for a in range(n)]

    def step(i, ins, outs, scr):
        if not n:
            return

        @pl.when(i == 0)
        def _():
            for cp in copies(ins, outs, *scr):
                cp.start()

        @pl.when(i == part.total - 1)
        def _():
            for cp in copies(ins, outs, *scr):
                cp.wait()

    part = _Part(list(gs), [HBM_SPEC] * n, [HBM_SPEC] * n,
                 [SDS((g.shape[0], g.shape[1] // 2, g.shape[2]), g.dtype) for g in gs],
                 [pltpu.SemaphoreType.DMA((n,)), pltpu.SemaphoreType.DMA((n,))] if n else [], step)
    return part


def _chips_copies(ins, outs, send_sems, recv_sems):
    x, y, c = _mesh_pos()
    cps = []
    for a in range(len(ins)):
        for j, (dx, dy) in enumerate(CHIP_COMBOS):
            px, py = _flip(x, dx), _flip(y, dy)
            cps.append(pltpu.make_async_remote_copy(
                src_ref=ins[a].at[2 * px + py], dst_ref=outs[a].at[j], send_sem=send_sems.at[a, j],
                recv_sem=recv_sems.at[a, j], device_id=(px, py, c), device_id_type=MESH))
    return cps


def _chips_out_shapes(ps):
    return [SDS((3,) + p.shape[1:], p.dtype) for p in ps]


def _chips_sems(n):
    return [pltpu.SemaphoreType.DMA((n, 3)), pltpu.SemaphoreType.DMA((n, 3))]


def _share_halves(qs, tag):
    n = len(qs)

    def body(*refs):
        ins, outs = refs[:n], refs[n:2 * n]
        send_sems, recv_sems = refs[2 * n:]
        x, y, c = _mesh_pos()
        cps = []
        for a in range(n):
            cp = pltpu.make_async_remote_copy(
                src_ref=ins[a], dst_ref=outs[a], send_sem=send_sems.at[a], recv_sem=recv_sems.at[a],
                device_id=(x, y, 1 - c), device_id_type=MESH)
            cp.start()
            cps.append(cp)
        for cp in cps:
            cp.wait()

    return pl.pallas_call(
        body, name=f"rs_share_halves_{tag}", in_specs=[HBM_SPEC] * n, out_specs=[HBM_SPEC] * n,
        out_shape=[SDS(q.shape, q.dtype) for q in qs],
        scratch_shapes=[pltpu.SemaphoreType.DMA((n,)), pltpu.SemaphoreType.DMA((n,))],
    )(*qs)


def _row_tile(rows, cap):
    return max(t for t in range(16, cap + 1, 16) if rows % t == 0)


def _add_own_half(g, recv, c, name):
    S, R, C = g.shape
    h = R // 2
    tr = _row_tile(h, 256)
    nt = h // tr

    def body(c_ref, g_ref, r_ref, o_ref, o16_ref):
        del c_ref
        s = g_ref[...] + r_ref[...]
        o_ref[...] = s
        o16_ref[...] = s.astype(bf16)

    ospec = pl.BlockSpec((1, tr, C), lambda s, i, cr: (s, i, 0))
    return pl.pallas_call(
        body, name=name,
        grid_spec=pltpu.PrefetchScalarGridSpec(
            num_scalar_prefetch=1, grid=(S, nt),
            in_specs=[pl.BlockSpec((1, tr, C), lambda s, i, cr: (s, cr[0] * nt + i, 0)), ospec],
            out_specs=[ospec, ospec]),
        out_shape=[SDS((S, h, C), f32), SDS((S, h, C), bf16)],
        compiler_params=_params(("parallel", "parallel"), 40),
    )(c, g, recv)


def _add_chip_parts(p, recv, kc, name):
    _, R, C = p.shape
    tr = _row_tile(R, 256)

    def body(k_ref, p_ref, r0, r1, r2, o_ref):
        del k_ref
        o_ref[...] = ((p_ref[0] + r0[0].astype(f32)) + r1[0].astype(f32)) + r2[0].astype(f32)

    rspec = lambda j: pl.BlockSpec((1, tr, C), lambda i, kr: (j, i, 0))
    return pl.pallas_call(
        body, name=name,
        grid_spec=pltpu.PrefetchScalarGridSpec(
            num_scalar_prefetch=1, grid=(R // tr,),
            in_specs=[pl.BlockSpec((1, tr, C), lambda i, kr: (kr[0], i, 0)), rspec(0), rspec(1), rspec(2)],
            out_specs=pl.BlockSpec((tr, C), lambda i, kr: (i, 0))),
        out_shape=SDS((R, C), f32), compiler_params=_params(("parallel",), 40),
    )(kc, p, recv, recv, recv)


def _rs_pair(gs, got, c, tag):
    pairs = [_add_own_half(g, r, c, f"rs_add_pair_{tag}_{a}") for a, (g, r) in enumerate(zip(gs, got))]
    return [p[0] for p in pairs], [p[1] for p in pairs]


def _rs_finish(ps, arrived, c, kc, tag):
    qs = [_add_chip_parts(p, r, kc, f"rs_add_chips_{tag}_{a}") for a, (p, r) in enumerate(zip(ps, arrived))]
    return qs, _share_halves(qs, tag)


def _adamw_math(w, g, m, v):
    m2 = ADAM_B1 * m + (1.0 - ADAM_B1) * g
    v2 = ADAM_B2 * v + (1.0 - ADAM_B2) * (g * g)
    m_hat = m2 / (1.0 - ADAM_B1 ** ADAM_STEP)
    v_hat = v2 / (1.0 - ADAM_B2 ** ADAM_STEP)
    return -ADAM_LR * (m_hat / (jnp.sqrt(v_hat) + ADAM_EPS) + ADAM_WD * w), m2, v2


def _adamw_shard(w, g_own, g_other, m, v, c, name):
    L, R, C = w.shape
    h = R // 2
    tr = _row_tile(h, 208)
    nh = h // tr

    def body(c_ref, w_ref, go0, gx0, go1, gx1, m_ref, v_ref, g_o, d_o, m_o, v_o):
        mine = (pl.program_id(1) // nh) == c_ref[0]
        first = pl.program_id(0) == 0
        g = jnp.where(mine, jnp.where(first, go0[...], go1[...]), jnp.where(first, gx0[...], gx1[...]))[None]
        d, m2, v2 = _adamw_math(w_ref[...], g, m_ref[...], v_ref[...])
        g_o[...], d_o[...], m_o[...], v_o[...] = g, d, m2, v2

    full = pl.BlockSpec((1, tr, C), lambda l, i, cr: (l, i, 0))
    half0 = pl.BlockSpec((tr, C), lambda l, i, cr: (jnp.where(l == 0, i % nh, nh - 1), 0))
    half1 = pl.BlockSpec((tr, C), lambda l, i, cr: (jnp.where(l == 1, i % nh, 0), 0))
    return pl.pallas_call(
        body, name=name,
        grid_spec=pltpu.PrefetchScalarGridSpec(num_scalar_prefetch=1, grid=(L, R // tr),
                                               in_specs=[full, half0, half0, half1, half1, full, full],
                                               out_specs=[full] * 4),
        out_shape=[SDS((L, R, C), f32)] * 4, compiler_params=_params(("parallel", "parallel"), 48),
    )(c, w, g_own[0], g_other[0], g_own[1], g_other[1], m, v)


def _adamw_packed(wmv, g, name):
    _, R, C = wmv.shape

    def body(wmv_ref, g_ref, out_ref):
        out_ref[0], out_ref[1], out_ref[2] = _adamw_math(wmv_ref[0], g_ref[...], wmv_ref[1], wmv_ref[2])

    vm = pl.BlockSpec(memory_space=pltpu.VMEM)
    return pl.pallas_call(body, name=name, in_specs=[vm, vm], out_specs=vm, out_shape=SDS((3, R, C), f32),
                          compiler_params=_params(None, 40))(wmv, g)


def _pack(arrs):
    flat = [a.reshape(-1, LANES) for a in arrs]
    rows = sum(f.shape[0] for f in flat)
    pad = (-rows) % 16
    if pad:
        flat.append(jnp.zeros((pad, LANES), f32))
    return jnp.concatenate(flat, axis=0)


def _unpack(buf, shapes):
    out, r = [], 0
    for s in shapes:
        n = math.prod(s) // LANES
        out.append(buf[r:r + n].reshape(s))
        r += n
    return out


REPLICATED = ("shift_mu", "w_decay0", "a0", "k_k", "k_a", "r_k", "ln_x_w", "ln_x_b", "v_mix0", "lb_logits", "g_norm_w",
              "ln_w", "ln_b")
CHIP_SMALL = (("w_decay_up", 2), ("a_up", 2), ("v_mix_down", 1), ("v_mix_up", 2))
WEIGHTS = ("w_in", "shift_mu", "w_decay0", "w_decay_up", "a0", "a_up", "k_k", "k_a", "r_k", "ln_x_w", "ln_x_b",
           "v_mix0", "v_mix_down", "v_mix_up", "lb_logits", "g_norm_w", "w_out", "ln_w", "ln_b")


def kernel(x, w_in, shift_mu, w_decay0, w_decay_up, a0, a_up, k_k, k_a, r_k, ln_x_w, ln_x_b, v_mix0, v_mix_down, v_mix_up, lb_logits, g_norm_w, w_out, ln_w, ln_b, loss_target, m_w_in, m_shift_mu, m_w_decay0, m_w_decay_up, m_a0, m_a_up, m_k_k, m_k_a, m_r_k, m_ln_x_w, m_ln_x_b, m_v_mix0, m_v_mix_down, m_v_mix_up, m_lb_logits, m_g_norm_w, m_w_out, m_ln_w, m_ln_b, v_w_in, v_shift_mu, v_w_decay0, v_w_decay_up, v_a0, v_a_up, v_k_k, v_k_a, v_r_k, v_ln_x_w, v_ln_x_b, v_v_mix0, v_v_mix_down, v_v_mix_up, v_lb_logits, v_g_norm_w, v_w_out, v_ln_w, v_ln_b):
    W = dict(w_in=w_in, shift_mu=shift_mu, w_decay0=w_decay0, w_decay_up=w_decay_up, a0=a0, a_up=a_up, k_k=k_k, k_a=k_a,
             r_k=r_k, ln_x_w=ln_x_w, ln_x_b=ln_x_b, v_mix0=v_mix0, v_mix_down=v_mix_down, v_mix_up=v_mix_up,
             lb_logits=lb_logits, g_norm_w=g_norm_w, w_out=w_out, ln_w=ln_w, ln_b=ln_b)
    M = dict(w_in=m_w_in, shift_mu=m_shift_mu, w_decay0=m_w_decay0, w_decay_up=m_w_decay_up, a0=m_a0, a_up=m_a_up,
             k_k=m_k_k, k_a=m_k_a, r_k=m_r_k, ln_x_w=m_ln_x_w, ln_x_b=m_ln_x_b, v_mix0=m_v_mix0,
             v_mix_down=m_v_mix_down, v_mix_up=m_v_mix_up, lb_logits=m_lb_logits, g_norm_w=m_g_norm_w, w_out=m_w_out,
             ln_w=m_ln_w, ln_b=m_ln_b)
    V = dict(w_in=v_w_in, shift_mu=v_shift_mu, w_decay0=v_w_decay0, w_decay_up=v_w_decay_up, a0=v_a0, a_up=v_a_up,
             k_k=v_k_k, k_a=v_k_a, r_k=v_r_k, ln_x_w=v_ln_x_w, ln_x_b=v_ln_x_b, v_mix0=v_v_mix0,
             v_mix_down=v_v_mix_down, v_mix_up=v_v_mix_up, lb_logits=v_lb_logits, g_norm_w=v_g_norm_w, w_out=v_w_out,
             ln_w=v_ln_w, ln_b=v_ln_b)
    mx, my, mc = _mesh_pos()
    kc = (2 * mx + my).astype(jnp.int32)
    c_arr = jnp.reshape(mc.astype(jnp.int32), (1,))
    kc_arr = jnp.reshape(kc, (1,))

    small_shard = _pack([W[n] for n, _ in CHIP_SMALL])
    tr_in = lambda a: jnp.swapaxes(a, 1, 2)
    in_halves = tr_in(w_in).astype(bf16).reshape(DEPTH, 2, SHARD_COLS // 2, D_MODEL)
    out_halves = jnp.swapaxes(w_out.astype(bf16).reshape(DEPTH, 2, D_MODEL // 8, D_MODEL), 0, 1)
    g_in0, g_small = _gather_shards([in_halves[0], small_shard.reshape(2, -1, LANES)])
    in_full = lambda g: g.reshape(IN_COLS, D_MODEL)
    shard_shapes = [W[n].shape for n, _ in CHIP_SMALL]
    g_small = g_small.reshape(4, -1, LANES)
    per_chip = [_unpack(g_small[k], shard_shapes) for k in range(4)]
    full_small = {n: jnp.concatenate([per_chip[k][i] for k in range(4)], axis=ax)
                  for i, (n, ax) in enumerate(CHIP_SMALL)}

    def assemble_later(gathered):
        g_in1, g_out = gathered
        return in_full(g_in1), jnp.transpose(g_out, (2, 0, 1, 3, 4)).reshape(DEPTH, D_MODEL, D_MODEL)

    in_slabs = lambda g: g.reshape(4, SHARD_COLS, D_MODEL)
    out_slabs = lambda g: g.reshape(4, D_MODEL // 4, D_MODEL)
    slabs, pairs, arrivals = {}, {}, {}

    def on_grads(l, dw_in, dw_out):
        slabs[l] = [in_slabs(dw_in), out_slabs(dw_out)]
        return slabs[l]

    def on_swapped(l, got):
        pairs[l], to_send = _rs_pair(slabs[l], got, c_arr, f"l{l}")
        return to_send

    def on_recv(l, arrived):
        arrivals[l] = arrived

    small_names = list(REPLICATED) + [n for n, _ in CHIP_SMALL]

    def on_small_grads(g):
        return _allreduce_part(_pack([g[n] for n in small_names]))

    loss_sum, gx, G, (small_sum,) = _device_step(
        x[0], loss_target[0], in_full(g_in0), [in_halves[1], out_halves], assemble_later, on_grads, on_swapped, on_recv,
        on_small_grads, shift_mu, w_decay0, full_small["w_decay_up"], a0, full_small["a_up"], k_k, k_a, r_k, ln_x_w,
        ln_x_b, v_mix0, full_small["v_mix_down"], full_small["v_mix_up"], lb_logits, g_norm_w, ln_w, ln_b)
    loss = lax.psum(loss_sum[0, 0], ("x", "y", "c"))

    own1, other1 = _rs_finish(pairs[1], arrivals[1], c_arr, kc_arr, "l1")
    own0, other0 = _rs_finish(pairs[0], arrivals[0], c_arr, kc_arr, "l0")
    gin_own, gout_own = [own0[0], own1[0]], [own0[1], own1[1]]
    gin_other, gout_other = [other0[0], other1[0]], [other0[1], other1[1]]

    small_full = dict(zip(small_names, _unpack(small_sum, [G[n].shape for n in small_names])))
    grads = {n: small_full[n] for n in REPLICATED}
    for n, ax in CHIP_SMALL:
        width = W[n].shape[ax]
        grads[n] = lax.dynamic_slice_in_dim(small_full[n], kc * width, width, axis=ax)

    delta, new_m, new_v = {}, {}, {}
    res = _adamw_shard(tr_in(w_in), gin_own, gin_other, tr_in(m_w_in), tr_in(v_w_in), c_arr, "adamw_w_in")
    grads["w_in"], delta["w_in"], new_m["w_in"], new_v["w_in"] = [tr_in(a) for a in res]
    grads["w_out"], delta["w_out"], new_m["w_out"], new_v["w_out"] = _adamw_shard(
        w_out, gout_own, gout_other, m_w_out, v_w_out, c_arr, "adamw_w_out")
    shapes = [W[n].shape for n in small_names]
    wmv = jnp.stack([_pack([src[n] for n in small_names]) for src in (W, M, V)])
    dmv = _adamw_packed(wmv, _pack([grads[n] for n in small_names]), "adamw_small")
    for n, d1, m1, v1 in zip(small_names, _unpack(dmv[0], shapes), _unpack(dmv[1], shapes), _unpack(dmv[2], shapes)):
        delta[n], new_m[n], new_v[n] = d1, m1, v1

    return (loss, gx[None], *[grads[n] for n in WEIGHTS], *[delta[n] for n in WEIGHTS],
            *[new_m[n] for n in WEIGHTS], *[new_v[n] for n in WEIGHTS])
```

```python
import functools
import math

import jax
import jax.numpy as jnp
from jax import lax
from jax.experimental import pallas as pl
from jax.experimental.pallas import tpu as pltpu

f32 = jnp.float32
bf16 = jnp.bfloat16
SDS = jax.ShapeDtypeStruct
MESH = pl.DeviceIdType.MESH

D_MODEL = 2048
DEPTH = 2
D_RWKV = 1024
D_HGRN = 1024
RWKV_HEAD = 64
RWKV_HEADS = 16
LORA = 64
VRES = 32
HG_HEAD = 128
HG_HEADS = 8
CHUNK = 64
RWKV_COLS = 4 * D_RWKV + 2 * LORA
IN_COLS = RWKV_COLS + 4 * D_HGRN
SHARD_COLS = IN_COLS // 4
LANES = 128
RW_BLKS = RWKV_COLS // LANES
ALPHA = (2 * DEPTH) ** 0.25
LN_EPS = 1e-5
GN_EPS = 64e-5
RMS_EPS = 1e-5
LB_FLOOR = 1e-30
EXP_M05 = math.exp(-0.5)
ADAM_LR, ADAM_B1, ADAM_B2, ADAM_EPS, ADAM_WD, ADAM_STEP = 0.001, 0.9, 0.999, 1e-08, 0.01, 10

TB = 128
TB_WIDE = 256
SCAN_TB = 8
SCAN_PARTS = 2
HSUB = 32
HIGHEST = lax.Precision.HIGHEST


def _params(sem=None, vmem_mb=None):
    kw = {}
    if sem is not None:
        kw["dimension_semantics"] = sem
    if vmem_mb is not None:
        kw["vmem_limit_bytes"] = vmem_mb << 20
    return pltpu.CompilerParams(**kw)


def _split(x, n):
    parts, r = [], x
    for i in range(n):
        p = r.astype(bf16)
        parts.append(p)
        if i + 1 < n:
            r = r - p.astype(f32)
    return parts


def _seg_raw(x, ee, n=3, per_tile=False):
    R = x.shape[0]
    if per_tile:
        outs = []
        for c in range(4):
            tile_parts = _split(x[:, 256 * c:256 * c + 256], n)
            res = jnp.dot(jnp.concatenate(tile_parts, axis=0), ee, preferred_element_type=f32)
            acc = res[0:R]
            for i in range(1, n):
                acc = acc + res[i * R:(i + 1) * R]
            outs.append(acc)
        return jnp.concatenate(outs, axis=1)
    parts = _split(x, n)
    lhs = jnp.concatenate([p[:, 256 * c:256 * c + 256] for p in parts for c in range(4)], axis=0)
    res = jnp.dot(lhs, ee, preferred_element_type=f32)

    def tile(c):
        acc = res[c * R:(c + 1) * R]
        for i in range(1, n):
            acc = acc + res[(4 * i + c) * R:(4 * i + c + 1) * R]
        return acc

    return jnp.concatenate([tile(c) for c in range(4)], axis=1)


@jax.custom_vjp
def _seg(x, ee):
    return _seg_raw(x, ee)


def _seg_fwd(x, ee):
    return _seg_raw(x, ee), ee


def _seg_bwd(ee, g):
    return _seg_raw(g, ee), jnp.zeros_like(ee)


_seg.defvjp(_seg_fwd, _seg_bwd)


def _const_mats():
    i256 = jnp.arange(256) // RWKV_HEAD
    ee = (i256[:, None] == i256[None, :]).astype(bf16)
    head = jnp.arange(D_RWKV) // RWKV_HEAD
    lane = jnp.arange(LANES)
    red = (head[:, None] == lane[None, :]).astype(bf16)
    exp = ((lane[:, None] % RWKV_HEADS == head[None, :]) & (lane[:, None] < 2 * RWKV_HEADS)).astype(bf16)
    return ee, red, exp


def _matmul(a, b, *, mode, tm, tn, tk, name, add=None, b_layer=None, rider=None):
    bs = b.shape if b_layer is None else b.shape[1:]
    lead = () if b_layer is None else (None,)
    bidx = (lambda *t: t) if b_layer is None else (lambda *t: (b_layer,) + t)
    if mode == "nn":
        (M, K), N = a.shape, bs[1]
        tm, tn, tk = min(tm, M), min(tn, N), min(tk, K)
        a_spec = pl.BlockSpec((tm, tk), lambda i, j, k: (i, k))
        b_spec = pl.BlockSpec(lead + (tk, tn), lambda i, j, k: bidx(k, j))
        dims = (((1,), (0,)), ((), ()))
    elif mode == "nt":
        (M, K), N = a.shape, bs[0]
        tm, tn, tk = min(tm, M), min(tn, N), min(tk, K)
        a_spec = pl.BlockSpec((tm, tk), lambda i, j, k: (i, k))
        b_spec = pl.BlockSpec(lead + (tn, tk), lambda i, j, k: bidx(j, k))
        dims = (((1,), (1,)), ((), ()))
    else:
        (K, M), N = a.shape, bs[1]
        tm, tn, tk = min(tm, M), min(tn, N), min(tk, K)
        a_spec = pl.BlockSpec((tk, tm), lambda i, j, k: (k, i))
        b_spec = pl.BlockSpec(lead + (tk, tn), lambda i, j, k: bidx(k, j))
        dims = (((0,), (0,)), ((), ()))
    assert M % tm == 0 and N % tn == 0 and K % tk == 0, (M, N, K, tm, tn, tk)
    has_add = add is not None
    n_in = 3 if has_add else 2
    grid = (M // tm, N // tn, K // tk)
    rd = rider if rider is not None else _Part([], [], [], [], [], lambda *a: None)
    rd.total = grid[0] * grid[1] * grid[2]

    def body(*refs):
        a_ref, b_ref = refs[:2]
        c_ref = refs[2] if has_add else None
        rd_ins, o_ref = refs[n_in:n_in + rd.n_in], refs[n_in + rd.n_in]
        rest = refs[n_in + rd.n_in + 1:]
        i, j, k = pl.program_id(0), pl.program_id(1), pl.program_id(2)
        rd.step((i * grid[1] + j) * grid[2] + k, rd_ins, rest[:rd.n_out], rest[rd.n_out:])
        p = lax.dot_general(a_ref[...].astype(bf16), b_ref[...].astype(bf16), dims, preferred_element_type=f32)

        @pl.when(k == 0)
        def _():
            o_ref[...] = p + c_ref[...] if has_add else p

        @pl.when(k > 0)
        def _():
            o_ref[...] += p

    o_spec = pl.BlockSpec((tm, tn), lambda i, j, k: (i, j))
    in_specs = [a_spec, b_spec] + ([o_spec] if has_add else []) + rd.in_specs
    args = (a, b) + ((add,) if has_add else ()) + tuple(rd.args)
    outs = pl.pallas_call(
        body, name=name, grid=grid, in_specs=in_specs, out_specs=[o_spec] + rd.out_specs,
        out_shape=[SDS((M, N), f32)] + rd.out_shape, scratch_shapes=rd.scratch,
        compiler_params=_params(("parallel", "parallel", "arbitrary") if rider is None else ("arbitrary",) * 3, 60),
    )(*args)
    return outs[0] if rider is None else (outs[0], list(outs[1:]))


def _rwkv_core(k_in, v_in, wdad, w0, wup, a0, aup, kkw, kaw, ee, vmix):
    w_raw = w0 + jnp.dot(jnp.tanh(wdad), wup, preferred_element_type=f32)
    decay = jnp.exp(-EXP_M05 * jax.nn.sigmoid(w_raw))
    a = jax.nn.sigmoid(a0 + jnp.dot(wdad, aup, preferred_element_type=f32))
    if vmix is None:
        v = v_in
    else:
        v0, vdn, vup, vfirst = vmix
        gate = jax.nn.sigmoid(v0 + jnp.dot(jnp.dot(v_in, vdn, preferred_element_type=f32), vup,
                                           preferred_element_type=f32))
        v = v_in + (vfirst - v_in) * gate
    kk = k_in * kkw
    n2 = _seg(kk * kk, ee)
    kk = kk / jnp.maximum(jnp.sqrt(n2), 1e-12)
    k2 = k_in * (1.0 + (a - 1.0) * kaw)
    return decay, k2, v, -kk, kk * a


def _shifted(y, prev_row):
    yp = pltpu.roll(y, 1, 0)
    row = lax.broadcasted_iota(jnp.int32, y.shape, 0)
    return jnp.where(row == 0, prev_row, yp)


def _rwkv_pre_fwd(proj, mu, w0, wup, a0, aup, kkw, kaw, ee, vmix, layer):
    T = proj.shape[0]
    nb = T // TB
    has_mix = vmix is not None

    def body(*refs):
        y_ref, p8_ref, mu_ref, w0_ref, wup_ref, a0_ref, aup_ref, kkw_ref, kaw_ref, ee_ref = refs[:10]
        rest = refs[10:]
        if has_mix:
            v0_ref, vdn_ref, vup_ref, vf_ref = rest[:4]
            rest = rest[4:]
        r_o, w_o, k_o, v_o, a_o, b_o, z_o = rest
        i = pl.program_id(0)
        y = y_ref[...]
        prev = jnp.where(i == 0, 0.0, p8_ref[7:8, :])
        rw = y + mu_ref[...] * (_shifted(y, prev) - y)
        mix = (v0_ref[...], vdn_ref[...], vup_ref[...], vf_ref[...]) if has_mix else None
        dec, k2, v, av, bv = _rwkv_core(rw[:, 1024:2048], rw[:, 2048:3072], rw[:, 4096:4224], w0_ref[...],
                                        wup_ref[...], a0_ref[...], aup_ref[...], kkw_ref[...], kaw_ref[...],
                                        ee_ref[...], mix)
        r_o[...] = rw[:, 0:1024]
        w_o[...] = dec
        k_o[...] = k2
        v_o[...] = v
        a_o[...] = av
        b_o[...] = bv
        z_o[...] = rw[:, 3072:4096]

    row = lambda n: pl.BlockSpec((1, n), lambda i: (0, 0))
    full = lambda s: pl.BlockSpec(s, lambda i: (0,) * len(s))
    blk = pl.BlockSpec((TB, D_RWKV), lambda i: (i, 0))
    in_specs = [pl.BlockSpec((TB, RWKV_COLS), lambda i: (i, 0)),
                pl.BlockSpec((8, RWKV_COLS), lambda i: (jnp.maximum(i * (TB // 8) - 1, 0), 0)),
                row(RWKV_COLS), row(D_RWKV), full((LANES, D_RWKV)), row(D_RWKV), full((LANES, D_RWKV)),
                row(D_RWKV), row(D_RWKV), full((256, 256))]
    args = [proj, proj, mu, w0, wup, a0, aup, kkw, kaw, ee]
    if has_mix:
        v0, vdn, vup, vfirst = vmix
        in_specs += [row(D_RWKV), full((D_RWKV, LANES)), full((LANES, D_RWKV)), blk]
        args += [v0, vdn, vup, vfirst]
    return pl.pallas_call(
        body, name=f"rwkv_pre_fwd_l{layer}", grid=(nb,), in_specs=in_specs, out_specs=[blk] * 7,
        out_shape=[SDS((T, D_RWKV), f32)] * 7, compiler_params=_params(("parallel",), 48),
    )(*args)


def _rwkv_pre_bwd(proj, mu, w0, wup, a0, aup, kkw, kaw, ee, vmix, cots, layer):
    T = proj.shape[0]
    nb = T // TB
    has_mix = vmix is not None
    cot_list = cots["r"] + [cots["w"]] + cots["k"] + cots["v"] + [cots["a"], cots["b"], cots["z"]]
    n_r, n_k, n_v = len(cots["r"]), len(cots["k"]), len(cots["v"])
    n_cot = len(cot_list)

    def body(*refs):
        y_ref, p8_ref, mu_ref, w0_ref, wup_ref, a0_ref, aup_ref, kkw_ref, kaw_ref, ee_ref = refs[:10]
        rest = refs[10:]
        if has_mix:
            v0_ref, vdn_ref, vup_ref, vf_ref = rest[:4]
            rest = rest[4:]
        cot_refs, rest = rest[:n_cot], rest[n_cot:]
        if has_mix:
            (dproj_o, dmu_o, dw0_o, dwup_o, da0_o, daup_o, dkkw_o, dkaw_o,
             dv0_o, dvdn_o, dvup_o, dvf_o, carry) = rest
        else:
            dproj_o, dmu_o, dw0_o, dwup_o, da0_o, daup_o, dkkw_o, dkaw_o, carry = rest
        i = pl.program_id(0)
        blk_i = nb - 1 - i
        y = y_ref[...]
        prev = jnp.where(blk_i == 0, 0.0, p8_ref[7:8, :])
        yp = _shifted(y, prev)
        mu_v = mu_ref[...]
        rw = y + mu_v * (yp - y)

        def sum_refs(rs):
            acc = rs[0][...]
            for r_ in rs[1:]:
                acc = acc + r_[...]
            return acc

        c = list(cot_refs)
        g_r = sum_refs(c[:n_r]); c = c[n_r:]
        g_w = c[0][...]; c = c[1:]
        g_k = sum_refs(c[:n_k]); c = c[n_k:]
        g_v = sum_refs(c[:n_v]); c = c[n_v:]
        g_a, g_b, g_z = c[0][...], c[1][...], c[2][...]

        ee_v = ee_ref[...]
        if has_mix:
            def fn(k_in, v_in, wdad, w0_, wup_, a0_, aup_, kkw_, kaw_, v0_, vdn_, vup_, vf_):
                return _rwkv_core(k_in, v_in, wdad, w0_, wup_, a0_, aup_, kkw_, kaw_, ee_v, (v0_, vdn_, vup_, vf_))
            prim = (rw[:, 1024:2048], rw[:, 2048:3072], rw[:, 4096:4224], w0_ref[...], wup_ref[...], a0_ref[...],
                    aup_ref[...], kkw_ref[...], kaw_ref[...], v0_ref[...], vdn_ref[...], vup_ref[...], vf_ref[...])
        else:
            def fn(k_in, v_in, wdad, w0_, wup_, a0_, aup_, kkw_, kaw_):
                return _rwkv_core(k_in, v_in, wdad, w0_, wup_, a0_, aup_, kkw_, kaw_, ee_v, None)
            prim = (rw[:, 1024:2048], rw[:, 2048:3072], rw[:, 4096:4224], w0_ref[...], wup_ref[...], a0_ref[...],
                    aup_ref[...], kkw_ref[...], kaw_ref[...])
        _, vjp = jax.vjp(fn, *prim)
        gs = vjp((g_w, g_k, g_v, g_a, g_b))
        d_k, d_v, d_wdad = gs[0], gs[1], gs[2]
        g_rw = jnp.concatenate([g_r, d_k, d_v, g_z, d_wdad], axis=1)

        @pl.when(i == 0)
        def _():
            carry[...] = jnp.zeros_like(carry)
            dmu_o[...] = jnp.zeros_like(dmu_o)
            dw0_o[...] = jnp.zeros_like(dw0_o)
            dwup_o[...] = jnp.zeros_like(dwup_o)
            da0_o[...] = jnp.zeros_like(da0_o)
            daup_o[...] = jnp.zeros_like(daup_o)
            dkkw_o[...] = jnp.zeros_like(dkkw_o)
            dkaw_o[...] = jnp.zeros_like(dkaw_o)
            if has_mix:
                dv0_o[...] = jnp.zeros_like(dv0_o)
                dvdn_o[...] = jnp.zeros_like(dvdn_o)
                dvup_o[...] = jnp.zeros_like(dvup_o)

        dmu_o[...] += jnp.sum(g_rw * (yp - y), axis=0, keepdims=True)
        dw0_o[...] += gs[3]
        dwup_o[...] += gs[4]
        da0_o[...] += gs[5]
        daup_o[...] += gs[6]
        dkkw_o[...] += gs[7]
        dkaw_o[...] += gs[8]
        if has_mix:
            dv0_o[...] += gs[9]
            dvdn_o[...] += gs[10]
            dvup_o[...] += gs[11]
            dvf_o[...] = gs[12]
        gm = g_rw * mu_v
        nxt = pltpu.roll(gm, TB - 1, 0)
        rowi = lax.broadcasted_iota(jnp.int32, gm.shape, 0)
        nxt = jnp.where(rowi == TB - 1, carry[...], nxt)
        dproj_o[...] = g_rw - gm + nxt
        carry[...] = gm[0:1, :]

    rev = lambda i: (nb - 1 - i, 0)
    row = lambda n: pl.BlockSpec((1, n), lambda i: (0, 0))
    full = lambda s: pl.BlockSpec(s, lambda i: (0,) * len(s))
    blk = pl.BlockSpec((TB, D_RWKV), rev)
    in_specs = [pl.BlockSpec((TB, RWKV_COLS), rev),
                pl.BlockSpec((8, RWKV_COLS), lambda i: (jnp.maximum((nb - 1 - i) * (TB // 8) - 1, 0), 0)),
                row(RWKV_COLS), row(D_RWKV), full((LANES, D_RWKV)), row(D_RWKV), full((LANES, D_RWKV)),
                row(D_RWKV), row(D_RWKV), full((256, 256))]
    args = [proj, proj, mu, w0, wup, a0, aup, kkw, kaw, ee]
    out_specs = [pl.BlockSpec((TB, RWKV_COLS), rev), row(RWKV_COLS), row(D_RWKV), full((LANES, D_RWKV)),
                 row(D_RWKV), full((LANES, D_RWKV)), row(D_RWKV), row(D_RWKV)]
    out_shape = [SDS((T, RWKV_COLS), f32), SDS((1, RWKV_COLS), f32), SDS((1, D_RWKV), f32), SDS((LANES, D_RWKV), f32),
                 SDS((1, D_RWKV), f32), SDS((LANES, D_RWKV), f32), SDS((1, D_RWKV), f32), SDS((1, D_RWKV), f32)]
    if has_mix:
        v0, vdn, vup, vfirst = vmix
        in_specs += [row(D_RWKV), full((D_RWKV, LANES)), full((LANES, D_RWKV)), blk]
        args += [v0, vdn, vup, vfirst]
        out_specs += [row(D_RWKV), full((D_RWKV, LANES)), full((LANES, D_RWKV)), blk]
        out_shape += [SDS((1, D_RWKV), f32), SDS((D_RWKV, LANES), f32), SDS((LANES, D_RWKV), f32),
                      SDS((T, D_RWKV), f32)]
    in_specs += [blk] * n_cot
    args += cot_list
    return pl.pallas_call(
        body, name=f"rwkv_pre_bwd_l{layer}", grid=(nb,), in_specs=in_specs, out_specs=out_specs, out_shape=out_shape,
        scratch_shapes=[pltpu.VMEM((1, RWKV_COLS), f32)],
        compiler_params=_params(("arbitrary",), 60),
    )(*args)


def _pack_cols(prod_ref, red):
    res = jnp.dot(prod_ref[...].astype(bf16), red, preferred_element_type=f32)
    out = res[0:RWKV_HEAD]
    for s in range(1, SCAN_TB):
        out = out + pltpu.roll(res[s * RWKV_HEAD:(s + 1) * RWKV_HEAD], RWKV_HEADS * s, 1)
    return out


def _unpack_cols(tile_ref, exp):
    rows = []
    for s in range(SCAN_TB):
        half = tile_ref[0, :, pl.ds((s // 4) * LANES, LANES)]
        rows.append(half if s % 4 == 0 else pltpu.roll(half, LANES - 2 * RWKV_HEADS * (s % 4), 1))
    return jnp.dot(jnp.concatenate(rows, axis=0), exp, preferred_element_type=f32)


def _to_col(v):
    T = v.shape[0]
    hi = lax.reduce_precision(v, 8, 7)
    parts = jnp.stack([hi, v - hi], axis=1).astype(bf16)
    parts = parts.reshape(T // SCAN_TB, SCAN_TB, 2, RWKV_HEADS, RWKV_HEAD)
    return jnp.transpose(parts, (0, 4, 1, 2, 3)).reshape(T // SCAN_TB, RWKV_HEAD, 2 * LANES)


def _from_col(vc):
    nb = vc.shape[0]
    tiles = vc.reshape(nb, RWKV_HEAD, SCAN_TB, RWKV_HEADS)
    return jnp.transpose(tiles, (0, 2, 3, 1)).reshape(nb * SCAN_TB, D_RWKV)


def _rwkv_scan_fwd(r, w, k, vcol, a, b, ee, red, exp, layer, rider, gather=()):
    T = r.shape[0]
    nb = T // SCAN_TB
    ng = len(gather)

    def body(*refs):
        r_ref, w_ref, k_ref, v_ref, a_ref, an_ref, b_ref, ee_ref, red_ref, exp_ref = refs[:10]
        rd_ins, refs = refs[10:10 + rider.n_in], refs[10 + rider.n_in:]
        g_ins, refs = refs[:ng], refs[ng:]
        o_ref, sp_ref = refs[:2]
        rd_outs, refs = refs[2:2 + rider.n_out], refs[2 + rider.n_out:]
        g_outs, refs = refs[:ng], refs[ng:]
        s_ref, sa_ref, vb_ref, po_ref = refs[:4]
        rd_scr, g_sems = refs[4:4 + rider.n_scr], refs[4 + rider.n_scr:]
        step = pl.program_id(0)
        rider.step(step, rd_ins, rd_outs, rd_scr)

        @pl.when(step == 0)
        def _():
            s_ref[...] = jnp.zeros_like(s_ref)
            sa_ref[...] = jnp.zeros_like(sa_ref)
            if ng:
                _gather_start(g_ins, g_outs, *g_sems)

        if ng:
            @pl.when(step == (3 * nb) // 4)
            def _():
                _gather_forward(g_ins, g_outs, *g_sems)

        ee_v = ee_ref[...]
        r_b, w_b, k_b, a_b, b_b = r_ref[...], w_ref[...], k_ref[...], a_ref[...], b_ref[...]
        rowi = lax.broadcasted_iota(jnp.int32, a_b.shape, 0)
        a_nx = jnp.where(rowi == SCAN_TB - 1, an_ref[0:1, :], pltpu.roll(a_b, SCAN_TB - 1, 0))
        wa = w_b * a_nx
        dots = _seg_raw(jnp.concatenate([b_b * a_nx, k_b * a_nx], axis=0), ee_v, 3)
        beta, kappa = dots[0:SCAN_TB], dots[SCAN_TB:2 * SCAN_TB]
        vb_ref[...] = _unpack_cols(v_ref, exp_ref[...])

        for t in range(SCAN_TB):
            row = lambda x: x[t:t + 1, :]
            blk = pl.ds(t * RWKV_HEAD, RWKV_HEAD)
            sp = s_ref[...]
            sa = sa_ref[...]
            sp_ref[t, 0] = sp
            sp_ref[t, 1] = sa
            vb = vb_ref[blk, :]
            ahead = _seg_raw(sp * row(wa), ee_v, SCAN_PARTS, per_tile=True)
            sn = sp * row(w_b) + sa * row(b_b) + vb * row(k_b)
            s_ref[...] = sn
            sa_ref[...] = ahead + sa * row(beta) + vb * row(kappa)
            po_ref[blk, :] = sn * row(r_b)
        o_ref[0] = _pack_cols(po_ref, red_ref[...])

        if ng:
            @pl.when(step == nb - 1)
            def _():
                _gather_finish(g_ins, g_outs, *g_sems)

    rows = pl.BlockSpec((SCAN_TB, D_RWKV), lambda i: (i, 0))
    rows_next = pl.BlockSpec((SCAN_TB, D_RWKV), lambda i: (jnp.minimum(i + 1, nb - 1), 0))
    col = pl.BlockSpec((1, RWKV_HEAD, LANES), lambda i: (i, 0, 0))
    col2 = pl.BlockSpec((1, RWKV_HEAD, 2 * LANES), lambda i: (i, 0, 0))
    full = lambda s: pl.BlockSpec(s, lambda i: (0,) * len(s))
    outs = pl.pallas_call(
        body, name=f"rwkv_scan_fwd_l{layer}", grid=(nb,),
        in_specs=[rows, rows, rows, col2, rows, rows_next, rows, full((256, 256)), full((D_RWKV, LANES)),
                  full((LANES, D_RWKV))] + rider.in_specs + [HBM_SPEC] * ng,
        out_specs=[col, pl.BlockSpec((SCAN_TB, 2, RWKV_HEAD, D_RWKV), lambda i: (i, 0, 0, 0))] + rider.out_specs
        + [HBM_SPEC] * ng,
        out_shape=[SDS((nb, RWKV_HEAD, LANES), f32), SDS((T, 2, RWKV_HEAD, D_RWKV), f32)] + rider.out_shape
        + _gather_out_shapes(gather),
        scratch_shapes=[pltpu.VMEM((RWKV_HEAD, D_RWKV), f32), pltpu.VMEM((RWKV_HEAD, D_RWKV), f32),
                        pltpu.VMEM((SCAN_TB * RWKV_HEAD, D_RWKV), f32),
                        pltpu.VMEM((SCAN_TB * RWKV_HEAD, D_RWKV), f32)] + rider.scratch
        + (_gather_sems(ng) if ng else []),
        compiler_params=_params(("arbitrary",), 52),
    )(r, w, k, vcol, a, a, b, ee, red, exp, *rider.args, *gather)
    return outs[0], outs[1], list(outs[2:2 + rider.n_out]), list(outs[2 + rider.n_out:])


def _rwkv_scan_bwd(r, w, k, vcol, a, b, sprev, docol, ee, red, exp, layer, send=()):
    T = r.shape[0]
    nb = T // SCAN_TB
    ns = len(send)

    def body(*refs):
        r_ref, w_ref, k_ref, v_ref, a_ref, b_ref, sp_ref, do_ref, ee_ref, red_ref, exp_ref = refs[:11]
        c_ins, refs = refs[11:11 + ns], refs[11 + ns:]
        dr_o, dw_o, dk_o, da_o, db_o, dv_o = refs[:6]
        c_outs, refs = refs[6:6 + ns], refs[6 + ns:]
        ds_ref, snext_ref, vb_ref, dob_ref, pdv_ref = refs[:5]
        c_sems = refs[5:]
        if ns:
            @pl.when(pl.program_id(0) == 0)
            def _():
                for cp in _chips_copies(c_ins, c_outs, *c_sems):
                    cp.start()

        ee_v = ee_ref[...]
        csum = lambda x: jnp.sum(x, axis=0, keepdims=True)
        nrow = SCAN_TB * RWKV_HEAD
        last = pl.ds(nrow - RWKV_HEAD, RWKV_HEAD)
        r_b, w_b, k_b, a_b, b_b = r_ref[...], w_ref[...], k_ref[...], a_ref[...], b_ref[...]
        vb_ref[...] = _unpack_cols(v_ref, exp_ref[...])
        dob_ref[...] = _unpack_cols(do_ref, exp_ref[...])

        @pl.when(pl.program_id(0) == 0)
        def _():
            ds_ref[...] = jnp.zeros_like(ds_ref)
            snext_ref[...] = (sp_ref[SCAN_TB - 1, 0] * w_b[SCAN_TB - 1:SCAN_TB, :]
                              + sp_ref[SCAN_TB - 1, 1] * b_b[SCAN_TB - 1:SCAN_TB, :]
                              + vb_ref[last, :] * k_b[SCAN_TB - 1:SCAN_TB, :])

        for s in range(SCAN_TB):
            t = SCAN_TB - 1 - s
            row = lambda x: x[t:t + 1, :]
            blk = pl.ds(t * RWKV_HEAD, RWKV_HEAD)
            sp, sa = sp_ref[t, 0], sp_ref[t, 1]
            sn = snext_ref[...] if s == 0 else sp_ref[t + 1, 0]
            vb, dob = vb_ref[blk, :], dob_ref[blk, :]
            ds = ds_ref[...] + dob * row(r_b)
            dsa = _seg_raw(ds * row(b_b), ee_v, SCAN_PARTS, per_tile=True)
            ds_ref[...] = ds * row(w_b) + dsa * row(a_b)
            dr_o[t:t + 1, :] = csum(sn * dob)
            pdv_ref[blk, :] = ds * row(k_b)
            dk_o[t:t + 1, :] = csum(ds * vb)
            db_o[t:t + 1, :] = csum(ds * sa)
            dw_o[t:t + 1, :] = csum(ds * sp)
            da_o[t:t + 1, :] = csum(sp * dsa)
        snext_ref[...] = sp_ref[0, 0]
        dv_o[0] = _pack_cols(pdv_ref, red_ref[...])

        if ns:
            @pl.when(pl.program_id(0) == nb - 1)
            def _():
                for cp in _chips_copies(c_ins, c_outs, *c_sems):
                    cp.wait()

    rows = pl.BlockSpec((SCAN_TB, D_RWKV), lambda i: (nb - 1 - i, 0))
    col = pl.BlockSpec((1, RWKV_HEAD, LANES), lambda i: (nb - 1 - i, 0, 0))
    col2 = pl.BlockSpec((1, RWKV_HEAD, 2 * LANES), lambda i: (nb - 1 - i, 0, 0))
    st = pl.BlockSpec((SCAN_TB, 2, RWKV_HEAD, D_RWKV), lambda i: (nb - 1 - i, 0, 0, 0))
    full = lambda s: pl.BlockSpec(s, lambda i: (0,) * len(s))
    big = pltpu.VMEM((SCAN_TB * RWKV_HEAD, D_RWKV), f32)
    outs = pl.pallas_call(
        body, name=f"rwkv_scan_bwd_l{layer}", grid=(nb,),
        in_specs=[rows, rows, rows, col2, rows, rows, st, col2, full((256, 256)), full((D_RWKV, LANES)),
                  full((LANES, D_RWKV))] + [HBM_SPEC] * ns,
        out_specs=[rows] * 5 + [col] + [HBM_SPEC] * ns,
        out_shape=[SDS((T, D_RWKV), f32)] * 5 + [SDS((nb, RWKV_HEAD, LANES), f32)] + _chips_out_shapes(send),
        scratch_shapes=[pltpu.VMEM((RWKV_HEAD, D_RWKV), f32)] * 2 + [big] * 3 + (_chips_sems(ns) if ns else []),
        compiler_params=_params(("arbitrary",), 56),
    )(r, w, k, vcol, a, b, sprev, docol, ee, red, exp, *send)
    return (*outs[:6], list(outs[6:]))


def _post_core(o, r, k2, v, z, gw, gb, rk, ee):
    inv = 1.0 / RWKV_HEAD
    mu = _seg(o, ee) * inv
    d = o - mu
    var = _seg(d * d, ee) * inv
    on = d * lax.rsqrt(var + GN_EPS) * gw + gb
    bonus = _seg(r * k2 * rk, ee) * v
    return (on + bonus) * jax.nn.silu(z)


def _rwkv_post_fwd(o, r, k2, v, z, gw, gb, rk, ee, cat, layer):
    T = o.shape[0]

    def body(o_ref, r_ref, k_ref, v_ref, z_ref, gw_ref, gb_ref, rk_ref, ee_ref, cat_in, out_ref):
        del cat_in
        out_ref[...] = _post_core(o_ref[...], r_ref[...], k_ref[...], v_ref[...], z_ref[...], gw_ref[...],
                                  gb_ref[...], rk_ref[...], ee_ref[...])

    blk = pl.BlockSpec((TB_WIDE, D_RWKV), lambda i: (i, 0))
    row = pl.BlockSpec((1, D_RWKV), lambda i: (0, 0))
    return pl.pallas_call(
        body, name=f"rwkv_post_fwd_l{layer}", grid=(T // TB_WIDE,),
        in_specs=[blk] * 5 + [row] * 3 + [pl.BlockSpec((256, 256), lambda i: (0, 0)), HBM_SPEC],
        out_specs=blk, out_shape=SDS((T, D_MODEL), f32), input_output_aliases={9: 0},
        compiler_params=_params(("parallel",), 40),
    )(o, r, k2, v, z, gw, gb, rk, ee, cat)


def _rwkv_post_bwd(o, r, k2, v, z, gw, gb, rk, ee, dcat, layer):
    T = o.shape[0]

    def body(o_ref, r_ref, k_ref, v_ref, z_ref, gw_ref, gb_ref, rk_ref, ee_ref, g_ref,
             do_o, dr_o, dk_o, dv_o, dz_o, dgw_o, dgb_o, drk_o):
        ee_v = ee_ref[...]
        fn = lambda o_, r_, k_, v_, z_, gw_, gb_, rk_: _post_core(o_, r_, k_, v_, z_, gw_, gb_, rk_, ee_v)
        _, vjp = jax.vjp(fn, o_ref[...], r_ref[...], k_ref[...], v_ref[...], z_ref[...], gw_ref[...], gb_ref[...],
                         rk_ref[...])
        gs = vjp(g_ref[...])
        do_o[...], dr_o[...], dk_o[...], dv_o[...], dz_o[...] = gs[:5]

        @pl.when(pl.program_id(0) == 0)
        def _():
            dgw_o[...] = jnp.zeros_like(dgw_o)
            dgb_o[...] = jnp.zeros_like(dgb_o)
            drk_o[...] = jnp.zeros_like(drk_o)

        dgw_o[...] += gs[5]
        dgb_o[...] += gs[6]
        drk_o[...] += gs[7]

    blk = pl.BlockSpec((TB_WIDE, D_RWKV), lambda i: (i, 0))
    row = pl.BlockSpec((1, D_RWKV), lambda i: (0, 0))
    return pl.pallas_call(
        body, name=f"rwkv_post_bwd_l{layer}", grid=(T // TB_WIDE,),
        in_specs=[blk] * 5 + [row] * 3 + [pl.BlockSpec((256, 256), lambda i: (0, 0)), blk],
        out_specs=[blk] * 5 + [row] * 3,
        out_shape=[SDS((T, D_RWKV), f32)] * 5 + [SDS((1, D_RWKV), f32)] * 3,
        compiler_params=_params(("arbitrary",), 48),
    )(o, r, k2, v, z, gw, gb, rk, ee, dcat)


def _hgrn_chunk(qr, fr, ii, z, S, lbl, gw, layer):
    L = CHUNK
    G = qr.shape[1] // HG_HEAD
    hd = lambda x, g: x[..., g * HG_HEAD:(g + 1) * HG_HEAD]
    heads = lambda f: jnp.concatenate([f(g) for g in range(G)], axis=-1)
    m = jnp.max(lbl, axis=0, keepdims=True)
    e = jnp.exp(lbl - m)
    sm = e / jnp.sum(e, axis=0, keepdims=True)
    if layer == 0:
        lb = sm[0:1] - sm[0:1]
    else:
        lb = (sm[0:1] + sm[1:2]) - sm[0:1]
    q = jax.nn.silu(qr)
    log_lb = jnp.log(jnp.maximum(lb, LB_FLOOR))
    lf = jnp.logaddexp(log_lb, jnp.log1p(-lb) + jax.nn.log_sigmoid(fr))
    k = (1.0 - lb) * jax.nn.sigmoid(-fr)
    ti = lax.broadcasted_iota(jnp.int32, (L, L), 0)
    si = lax.broadcasted_iota(jnp.int32, (L, L), 1)
    ltri = (si <= ti).astype(f32)
    b = jnp.dot(ltri, lf, precision=HIGHEST, preferred_element_type=f32)
    btot = jnp.dot(jnp.ones((L, L), f32), lf, precision=HIGHEST, preferred_element_type=f32)
    qd = q * jnp.exp(b)
    o = heads(lambda g: jnp.dot(hd(qd, g), S[g], preferred_element_type=f32))
    nsub = L // HSUB
    t3 = lax.broadcasted_iota(jnp.int32, (HSUB, HSUB, G * HG_HEAD), 0)
    s3 = lax.broadcasted_iota(jnp.int32, (HSUB, HSUB, G * HG_HEAD), 1)
    causal = s3 <= t3
    nt = (((1,), (1,)), ((), ()))
    tn = (((0,), (0,)), ((), ()))
    rows = []
    for I in range(nsub):
        sl = slice(I * HSUB, (I + 1) * HSUB)
        bI, qI, kI, iI = b[sl], q[sl], k[sl], ii[sl]
        d3 = bI[:, None, :] - bI[None, :, :]
        dec = jnp.where(causal, jnp.exp(jnp.where(causal, d3, 0.0)), 0.0)
        prod = qI[:, None, :] * dec * kI[None, :, :]
        oI = heads(lambda g: jnp.dot(jnp.sum(hd(prod, g), axis=-1), hd(iI, g), preferred_element_type=f32))
        for J in range(I):
            sj = slice(J * HSUB, (J + 1) * HSUB)
            bm = b[(J + 1) * HSUB - 1:(J + 1) * HSUB]
            qs = qI * jnp.exp(bI - bm)
            ks = k[sj] * jnp.exp(bm - b[sj])
            iJ = ii[sj]
            oI = oI + heads(lambda g: jnp.dot(
                lax.dot_general(hd(qs, g), hd(ks, g), nt, preferred_element_type=f32), hd(iJ, g),
                preferred_element_type=f32))
        rows.append(oI)
    o = o + jnp.concatenate(rows, axis=0)
    k_dec = k * jnp.exp(btot - b)
    ones = jnp.ones((L, HG_HEAD), f32)
    s_new = jnp.concatenate([
        (S[g] * jnp.exp(lax.dot_general(hd(lf, g), ones, tn, precision=HIGHEST, preferred_element_type=f32))
         + lax.dot_general(hd(k_dec, g), hd(ii, g), tn, preferred_element_type=f32))[None] for g in range(G)], axis=0)
    ms = heads(lambda g: jnp.broadcast_to(jnp.mean(hd(o * o, g), axis=-1, keepdims=True), (L, HG_HEAD)))
    out = o * lax.rsqrt(ms + RMS_EPS) * gw * jax.nn.silu(z)
    return out, s_new


class _Part:
    def __init__(self, args, in_specs, out_specs, out_shape, scratch, step):
        self.args, self.in_specs, self.out_specs, self.out_shape = args, in_specs, out_specs, out_shape
        self.scratch, self.step, self.total = scratch, step, None
        self.n_in, self.n_out, self.n_scr = len(args), len(out_shape), len(scratch)


def _run_part(part, steps, name):
    part.total = steps

    def body(*refs):
        ins, outs = refs[:part.n_in], refs[part.n_in:part.n_in + part.n_out]
        part.step(pl.program_id(0), ins, outs, refs[part.n_in + part.n_out:])

    return pl.pallas_call(body, name=name, grid=(steps,), in_specs=part.in_specs, out_specs=part.out_specs,
                          out_shape=part.out_shape, scratch_shapes=part.scratch,
                          compiler_params=_params(("arbitrary",), 48))(*part.args)


def _hg_index(T, rev, group=1):
    nc = T // CHUNK
    width = group * HG_HEAD
    head = lambda i: i // nc
    chunk = (lambda i: nc - 1 - i % nc) if rev else (lambda i: i % nc)
    first = lambda i: i % nc == 0
    pj = lambda off, g: pl.BlockSpec((CHUNK, HG_HEAD),
                                     lambda i: (chunk(i), RW_BLKS + off + group * head(i) + g))
    pspecs = [pj(kind * HG_HEADS, g) for kind in range(4) for g in range(group)]
    per_head = lambda rows: pl.BlockSpec((rows, width), lambda i: (0, head(i)))
    return nc, head, chunk, first, pspecs, per_head


def _hgrn_fwd_part(proj, lbl, gw, layer):
    T = proj.shape[0]
    nc, head, chunk, first, pspecs, per_head = _hg_index(T, False)

    def step(i, ins, outs, scr):
        q_ref, f_ref, i_ref, z_ref, lbl_ref, gw_ref = ins
        out_ref, sh_ref = outs
        s_ref, = scr

        @pl.when(first(i))
        def _():
            s_ref[...] = jnp.zeros_like(s_ref)

        s0 = s_ref[...]
        sh_ref[:, 0] = s0
        out, s_new = _hgrn_chunk(q_ref[...], f_ref[...], i_ref[...], z_ref[...], s0, lbl_ref[...], gw_ref[...], layer)
        out_ref[...] = out
        s_ref[...] = s_new

    return _Part(
        [proj, proj, proj, proj, lbl, gw], pspecs + [per_head(2), per_head(1)],
        [pl.BlockSpec((CHUNK, HG_HEAD), lambda i: (chunk(i), HG_HEADS + head(i))),
         pl.BlockSpec((1, 1, HG_HEAD, HG_HEAD), lambda i: (head(i), chunk(i), 0, 0))],
        [SDS((T, D_MODEL), f32), SDS((HG_HEADS, nc, HG_HEAD, HG_HEAD), f32)],
        [pltpu.VMEM((1, HG_HEAD, HG_HEAD), f32)], step)


HG_GROUP = 2


def _hgrn_bwd_part(proj, shist, dcat, lbl, gw, layer):
    T = proj.shape[0]
    G = HG_GROUP
    nc, head, chunk, first, pspecs, per_head = _hg_index(T, True, G)

    def step(i, ins, outs, scr):
        p_refs, (lbl_ref, gw_ref, sh_ref, g_ref) = ins[:4 * G], ins[4 * G:]
        dq_o, df_o, di_o, dz_o, dlbl_o, dgw_o = outs
        ds_ref, = scr

        @pl.when(first(i))
        def _():
            ds_ref[...] = jnp.zeros_like(ds_ref)
            dlbl_o[...] = jnp.zeros_like(dlbl_o)
            dgw_o[...] = jnp.zeros_like(dgw_o)

        fn = functools.partial(_hgrn_chunk, layer=layer)
        q, f, ii, z = (jnp.concatenate([p_refs[kind * G + g][...] for g in range(G)], axis=1) for kind in range(4))
        _, vjp = jax.vjp(fn, q, f, ii, z, sh_ref[:, 0], lbl_ref[...], gw_ref[...])
        dq, df, di, dz, ds, dl, dg = vjp((g_ref[...], ds_ref[...]))
        dq_o[...], df_o[...], di_o[...], dz_o[...] = dq, df, di, dz
        ds_ref[...] = ds
        dlbl_o[...] += dl
        dgw_o[...] += dg

    ospec = pl.BlockSpec((CHUNK, G * HG_HEAD), lambda i: (chunk(i), head(i)))
    return _Part(
        [proj] * (4 * G) + [lbl, gw, shist, dcat],
        pspecs + [per_head(2), per_head(1),
                  pl.BlockSpec((G, 1, HG_HEAD, HG_HEAD), lambda i: (head(i), chunk(i), 0, 0)),
                  pl.BlockSpec((CHUNK, G * HG_HEAD), lambda i: (chunk(i), HG_HEADS // G + head(i)))],
        [ospec] * 4 + [per_head(2), per_head(1)],
        [SDS((T, D_HGRN), f32)] * 4 + [SDS((2, D_HGRN), f32), SDS((1, D_HGRN), f32)],
        [pltpu.VMEM((G, HG_HEAD, HG_HEAD), f32)], step)


def _ln_core(h, y, w, b):
    u = ALPHA * h + y
    mu = jnp.mean(u, axis=-1, keepdims=True)
    d = u - mu
    var = jnp.mean(d * d, axis=-1, keepdims=True)
    return d * lax.rsqrt(var + LN_EPS) * w + b


def _ln_fwd(h, y, w, b, layer):
    T = h.shape[0]

    def body(h_ref, y_ref, w_ref, b_ref, o_ref):
        o_ref[...] = _ln_core(h_ref[...], y_ref[...], w_ref[...], b_ref[...])

    blk = pl.BlockSpec((TB_WIDE, D_MODEL), lambda i: (i, 0))
    row = pl.BlockSpec((1, D_MODEL), lambda i: (0, 0))
    return pl.pallas_call(body, name=f"ln_fwd_l{layer}", grid=(T // TB_WIDE,), in_specs=[blk, blk, row, row],
                          out_specs=blk, out_shape=SDS((T, D_MODEL), f32),
                          compiler_params=_params(("parallel",), 40))(h, y, w, b)


def _ln_bwd(h, y, w, b, g, layer):
    T = h.shape[0]

    def body(h_ref, y_ref, w_ref, b_ref, g_ref, dh_o, dy_o, dw_o, db_o):
        _, vjp = jax.vjp(_ln_core, h_ref[...], y_ref[...], w_ref[...], b_ref[...])
        dh, dy, dw, db = vjp(g_ref[...])
        dh_o[...] = dh
        dy_o[...] = dy.astype(bf16)

        @pl.when(pl.program_id(0) == 0)
        def _():
            dw_o[...] = jnp.zeros_like(dw_o)
            db_o[...] = jnp.zeros_like(db_o)

        dw_o[...] += dw
        db_o[...] += db

    blk = pl.BlockSpec((TB_WIDE, D_MODEL), lambda i: (i, 0))
    row = pl.BlockSpec((1, D_MODEL), lambda i: (0, 0))
    return pl.pallas_call(body, name=f"ln_bwd_l{layer}", grid=(T // TB_WIDE,), in_specs=[blk, blk, row, row, blk],
                          out_specs=[blk, blk, row, row],
                          out_shape=[SDS((T, D_MODEL), f32), SDS((T, D_MODEL), bf16)] + [SDS((1, D_MODEL), f32)] * 2,
                          compiler_params=_params(("arbitrary",), 48))(h, y, w, b, g)


def _loss_head(hout, target):
    T = hout.shape[0]

    def body(h_ref, t_ref, loss_o, g_o):
        err = h_ref[...] - t_ref[...]
        g_o[...] = err * (1.0 / D_MODEL)

        @pl.when(pl.program_id(0) == 0)
        def _():
            loss_o[...] = jnp.zeros_like(loss_o)

        part = jnp.sum(jnp.sum(err * err, axis=-1, keepdims=True) * (1.0 / D_MODEL), axis=0, keepdims=True)
        loss_o[...] += 0.5 * part

    blk = pl.BlockSpec((TB_WIDE, D_MODEL), lambda i: (i, 0))
    return pl.pallas_call(body, name="loss_head", grid=(T // TB_WIDE,), in_specs=[blk, blk],
                          out_specs=[pl.BlockSpec((1, 1), lambda i: (0, 0)), blk],
                          out_shape=[SDS((1, 1), f32), SDS((T, D_MODEL), f32)],
                          compiler_params=_params(("arbitrary",), 40))(hout, target)


def _pad_rows(m, lo, total):
    return jnp.pad(m, ((0, 0), (lo, total - lo - m.shape[1]), (0, 0)))


def _device_step(x, target, w_in0, later_shards, assemble_later, on_grads, on_swapped, on_recv, on_small_grads,
                 shift_mu, w_decay0, w_decay_up, a0, a_up, k_k, k_a, r_k, ln_x_w, ln_x_b,
                 v_mix0, v_mix_down, v_mix_up, lb_logits, g_norm_w, ln_w, ln_b):
    T = x.shape[0]
    ee, red, exp = _const_mats()
    wup_pad = _pad_rows(w_decay_up, 0, LANES)
    aup_pad = _pad_rows(a_up, LORA, LANES)
    vdn_pad = jnp.pad(v_mix_down, ((0, 0), (0, 0), (0, LANES - VRES)))
    vup_pad = _pad_rows(v_mix_up, 0, LANES)
    row = lambda p, l: p[l][None, :]

    def vmix_of(l, vfirst):
        if l == 0:
            return None
        return (row(v_mix0, l - 1), vdn_pad[l - 1], vup_pad[l - 1], vfirst)

    h = x
    saved = []
    vfirst = None
    w_in = [w_in0, None]
    w_out = None
    for l in range(DEPTH):
        hb = h.astype(bf16)
        proj = _matmul(hb, w_in[l], mode="nt", tm=1024, tn=1664, tk=D_MODEL, name=f"proj_fwd_l{l}")
        pre_args = (proj, row(shift_mu, l), row(w_decay0, l), wup_pad[l], row(a0, l), aup_pad[l], row(k_k, l),
                    row(k_a, l), ee, vmix_of(l, vfirst))
        r, w, k2, v, av, bv, z = _rwkv_pre_fwd(*pre_args, layer=l)
        if l == 0:
            vfirst = v
        vcol = _to_col(v)
        ocol, sprev, (cat, shist), gathered = _rwkv_scan_fwd(
            r, w, k2, vcol, av, bv, ee, red, exp, l, _hgrn_fwd_part(proj, lb_logits, row(g_norm_w, l), l),
            gather=later_shards if l == 0 else [])
        if l == 0:
            w_in[1], w_out = assemble_later(gathered)
        o = _from_col(ocol)
        post_args = (o, r, k2, v, z, row(ln_x_w, l), row(ln_x_b, l), row(r_k, l), ee)
        cat = _rwkv_post_fwd(*post_args, cat, layer=l)
        catb = cat.astype(bf16)
        y = _matmul(catb, w_out, b_layer=l, mode="nn", tm=1024, tn=1024, tk=D_MODEL, name=f"out_fwd_l{l}")
        h_new = _ln_fwd(h, y, row(ln_w, l), row(ln_b, l), l)
        saved.append(dict(h=h, hb=hb, proj=proj, pre_args=pre_args, post_args=post_args,
                          scan=(r, w, k2, vcol, av, bv, sprev), cat=catb, shist=shist, y=y))
        h = h_new

    loss_sum, g = _loss_head(h, target)

    grads = {n: [None] * DEPTH for n in ("w_in", "w_out", "shift_mu", "w_decay0", "w_decay_up", "a0", "a_up", "k_k",
                                         "k_a", "r_k", "ln_x_w", "ln_x_b", "g_norm_w", "ln_w", "ln_b")}
    dlbl = []
    dvfirst = None
    to_send = []
    for l in reversed(range(DEPTH)):
        s = saved[l]
        dh_res, dy, dlnw, dlnb = _ln_bwd(s["h"], s["y"], row(ln_w, l), row(ln_b, l), g, l)
        grads["ln_w"][l], grads["ln_b"][l] = dlnw[0], dlnb[0]
        dcat = _matmul(dy, w_out, b_layer=l, mode="nt", tm=1024, tn=1024, tk=D_MODEL, name=f"out_bwd_dx_l{l}")
        grads["w_out"][l] = _matmul(s["cat"], dy, mode="tn", tm=1024, tn=1024, tk=T, name=f"out_bwd_dw_l{l}")
        do, dr_p, dk_p, dv_p, dz, dgw, dgb, drk = _rwkv_post_bwd(*s["post_args"], dcat, layer=l)
        grads["ln_x_w"][l], grads["ln_x_b"][l], grads["r_k"][l] = dgw[0], dgb[0], drk[0]
        r, w, k2, vcol, av, bv, sprev = s["scan"]
        dr_s, dw_s, dk_s, da_s, db_s, dvcol, arrived = _rwkv_scan_bwd(
            r, w, k2, vcol, av, bv, sprev, _to_col(do), ee, red, exp, l, send=to_send if l == 0 else [])
        dq, df, di, dzh, dl, dgn = _run_part(
            _hgrn_bwd_part(s["proj"], s["shist"], dcat, lb_logits, row(g_norm_w, l), l),
            (T // CHUNK) * HG_HEADS // HG_GROUP, f"hgrn_bwd_l{l}")
        if l == 0:
            on_recv(1, arrived)
        v_cots = [dv_p, _from_col(dvcol)] + ([dvfirst] if (l == 0 and dvfirst is not None) else [])
        cots = dict(r=[dr_p, dr_s], w=dw_s, k=[dk_p, dk_s], v=v_cots, a=da_s, b=db_s, z=dz)
        outs = _rwkv_pre_bwd(*s["pre_args"], cots, layer=l)
        dproj, dmu, dw0, dwup, da0, daup, dkkw, dkaw = outs[:8]
        grads["shift_mu"][l], grads["w_decay0"][l], grads["a0"][l] = dmu[0], dw0[0], da0[0]
        grads["k_k"][l], grads["k_a"][l] = dkkw[0], dkaw[0]
        grads["w_decay_up"][l], grads["a_up"][l] = dwup[:LORA], daup[LORA:]
        if l > 0:
            dv0, dvdn, dvup, dvfirst = outs[8:]
            g_vmix = (dv0, dvdn[:, :VRES][None], dvup[:VRES][None])
        dlbl.append(dl)
        grads["g_norm_w"][l] = dgn[0]
        dproj = jnp.concatenate([dproj, dq, df, di, dzh], axis=1).astype(bf16)
        dw_args = dict(mode="tn", tm=1664, tn=1024, tk=T, name=f"proj_bwd_dw_l{l}")
        dx_args = dict(mode="nn", tm=512, tn=512, tk=IN_COLS, name=f"proj_bwd_dx_l{l}", add=dh_res)
        if l == 1:
            grads["w_in"][l] = _matmul(dproj, s["hb"], **dw_args)
            g, swapped = _matmul(dproj, w_in[l], rider=_swap_part(on_grads(l, grads["w_in"][l], grads["w_out"][l])),
                                 **dx_args)
            to_send = on_swapped(l, swapped)
        else:
            small = {n: jnp.stack(v) for n, v in grads.items() if n not in ("w_in", "w_out")}
            small["lb_logits"] = dlbl[0] + dlbl[1]
            small["v_mix0"], small["v_mix_down"], small["v_mix_up"] = g_vmix
            grads["w_in"][l], small_out = _matmul(dproj, s["hb"], rider=on_small_grads(small), **dw_args)
            swapped = _run_part(_swap_part(on_grads(l, grads["w_in"][l], grads["w_out"][l])), 1, "rs_swap_halves_l0")
            g, arrived = _matmul(dproj, w_in[l], rider=_chips_part(on_swapped(l, swapped)), **dx_args)
            on_recv(0, arrived)
    small["w_in"], small["w_out"] = grads["w_in"], grads["w_out"]
    return loss_sum, g, small, small_out


CHIP_COMBOS = ((1, 0), (0, 1), (1, 1))
HBM_SPEC = pl.BlockSpec(memory_space=pl.ANY)


def _mesh_pos():
    return lax.axis_index("x"), lax.axis_index("y"), lax.axis_index("c")


def _flip(v, d):
    return 1 - v if d else v


def _half(ref, dim, cc, hs):
    idx = [slice(None)] * len(ref.shape)
    idx[dim] = pl.ds(cc * hs, hs)
    return ref.at[tuple(idx)]


def _gather_shards(arrs):
    n = len(arrs)

    def body(*refs):
        ins, outs = refs[:n], refs[n:2 * n]
        sems = refs[2 * n:]
        _gather_start(ins, outs, *sems)
        _gather_forward(ins, outs, *sems)
        _gather_finish(ins, outs, *sems)

    return pl.pallas_call(
        body, name="gather_shards", in_specs=[HBM_SPEC] * n, out_specs=[HBM_SPEC] * n,
        out_shape=_gather_out_shapes(arrs), scratch_shapes=_gather_sems(n),
    )(*arrs)


def _gather_out_shapes(arrs):
    return [SDS((4,) + a.shape, a.dtype) for a in arrs]


def _gather_sems(n):
    return [pltpu.SemaphoreType.DMA((n, 6)), pltpu.SemaphoreType.DMA((n, 6)), pltpu.SemaphoreType.DMA((n,))]


def _gather_copies(kind, ins, outs, send_sems, recv_sems, local_sems):
    x, y, c = _mesh_pos()
    kc = 2 * x + y
    cps = []
    for a in range(len(ins)):
        if kind == "local":
            cps.append(pltpu.make_async_copy(ins[a], outs[a].at[kc], local_sems.at[a]))
            continue
        for j, (dx, dy) in enumerate(CHIP_COMBOS):
            px, py = _flip(x, dx), _flip(y, dy)
            kj = 2 * px + py
            src, dst, s, to = {
                "sends": (ins[a].at[c], outs[a].at[kc, c], j, (px, py, c)),
                "landed": (outs[a].at[kj, c], outs[a].at[kj, c], j, (px, py, c)),
                "forwards": (outs[a].at[kj, c], outs[a].at[kj, c], 3 + j, (x, y, 1 - c)),
                "passed": (outs[a].at[kj, 1 - c], outs[a].at[kj, 1 - c], 3 + j, (x, y, 1 - c)),
            }[kind]
            cps.append(pltpu.make_async_remote_copy(
                src_ref=src, dst_ref=dst, send_sem=send_sems.at[a, s], recv_sem=recv_sems.at[a, s], device_id=to,
                device_id_type=MESH))
    return cps


def _gather_start(*refs):
    for cp in _gather_copies("local", *refs) + _gather_copies("sends", *refs):
        cp.start()


def _gather_forward(*refs):
    for arrival, fw in zip(_gather_copies("landed", *refs), _gather_copies("forwards", *refs)):
        arrival.wait_recv()
        fw.start()


def _gather_finish(*refs):
    for cp in _gather_copies("passed", *refs):
        cp.wait_recv()
    for cp in _gather_copies("sends", *refs) + _gather_copies("forwards", *refs):
        cp.wait_send()
    for cp in _gather_copies("local", *refs):
        cp.wait()


DEV_COMBOS = tuple((dx, dy, dc) for dx in (0, 1) for dy in (0, 1) for dc in (0, 1))[1:]


def _allreduce_part(buf):
    R = buf.shape[0]
    whole = pl.BlockSpec((R, LANES), lambda *_: (0, 0))

    def copies(x_ref, slots, send_sems, recv_sems):
        x, y, c = _mesh_pos()
        return [pltpu.make_async_remote_copy(
            src_ref=x_ref, dst_ref=slots.at[k + 1], send_sem=send_sems.at[k], recv_sem=recv_sems.at[k],
            device_id=(_flip(x, dx), _flip(y, dy), _flip(c, dc)), device_id_type=MESH)
            for k, (dx, dy, dc) in enumerate(DEV_COMBOS)]

    def step(i, ins, outs, scr):
        x_ref, = ins
        o_ref, = outs
        slots = scr[0]

        @pl.when(i == 0)
        def _():
            for cp in copies(x_ref, *scr):
                cp.start()
            slots[0] = x_ref[...]

        @pl.when(i == part.total - 1)
        def _():
            x, y, c = _mesh_pos()
            me = 4 * x + 2 * y + c
            cps = copies(x_ref, *scr)
            for cp in cps:
                cp.wait_recv()
            acc = slots[jnp.bitwise_xor(me, 0)]
            for d in range(1, 8):
                acc = acc + slots[jnp.bitwise_xor(me, d)]
            o_ref[...] = acc
            for cp in cps:
                cp.wait_send()

    part = _Part([buf], [whole], [whole], [SDS((R, LANES), f32)],
                 [pltpu.VMEM((8, R, LANES), f32), pltpu.SemaphoreType.DMA((7,)), pltpu.SemaphoreType.DMA((7,))], step)
    return part


def _chips_part(slabs):
    n = len(slabs)

    def step(i, ins, outs, scr):
        if not n:
            return

        @pl.when(i == 0)
        def _():
            for cp in _chips_copies(ins, outs, *scr):
                cp.start()

        @pl.when(i == part.total - 1)
        def _():
            for cp in _chips_copies(ins, outs, *scr):
                cp.wait()

    part = _Part(list(slabs), [HBM_SPEC] * n, [HBM_SPEC] * n, _chips_out_shapes(slabs),
                 _chips_sems(n) if n else [], step)
    return part


def _swap_part(gs):
    n = len(gs)

    def copies(ins, outs, send_sems, recv_sems):
        x, y, c = _mesh_pos()
        return [pltpu.make_async_remote_copy(
            src_ref=_half(ins[a], 1, 1 - c, ins[a].shape[1] // 2), dst_ref=outs[a], send_sem=send_sems.at[a],
            recv_sem=recv_sems.at[a], device_id=(x, y, 1 - c), device_id_type=MESH) for a in range(n)]

    def step(i, ins, outs, scr):
        if not n:
            return

        @pl.when(i == 0)
        def _():
            for cp in copies(ins, outs, *scr):
                cp.start()

        @pl.when(i == part.total - 1)
        def _():
            for cp in copies(ins, outs, *scr):
                cp.wait()

    part = _Part(list(gs), [HBM_SPEC] * n, [HBM_SPEC] * n,
                 [SDS((g.shape[0], g.shape[1] // 2, g.shape[2]), g.dtype) for g in gs],
                 [pltpu.SemaphoreType.DMA((n,)), pltpu.SemaphoreType.DMA((n,))] if n else [], step)
    return part


def _chips_copies(ins, outs, send_sems, recv_sems):
    x, y, c = _mesh_pos()
    cps = []
    for a in range(len(ins)):
        for j, (dx, dy) in enumerate(CHIP_COMBOS):
            px, py = _flip(x, dx), _flip(y, dy)
            cps.append(pltpu.make_async_remote_copy(
                src_ref=ins[a].at[2 * px + py], dst_ref=outs[a].at[j], send_sem=send_sems.at[a, j],
                recv_sem=recv_sems.at[a, j], device_id=(px, py, c), device_id_type=MESH))
    return cps


def _chips_out_shapes(ps):
    return [SDS((3,) + p.shape[1:], p.dtype) for p in ps]


def _chips_sems(n):
    return [pltpu.SemaphoreType.DMA((n, 3)), pltpu.SemaphoreType.DMA((n, 3))]


def _share_halves(qs, tag):
    n = len(qs)

    def body(*refs):
        ins, outs = refs[:n], refs[n:2 * n]
        send_sems, recv_sems = refs[2 * n:]
        x, y, c = _mesh_pos()
        cps = []
        for a in range(n):
            cp = pltpu.make_async_remote_copy(
                src_ref=ins[a], dst_ref=outs[a], send_sem=send_sems.at[a], recv_sem=recv_sems.at[a],
                device_id=(x, y, 1 - c), device_id_type=MESH)
            cp.start()
            cps.append(cp)
        for cp in cps:
            cp.wait()

    return pl.pallas_call(
        body, name=f"rs_share_halves_{tag}", in_specs=[HBM_SPEC] * n, out_specs=[HBM_SPEC] * n,
        out_shape=[SDS(q.shape, q.dtype) for q in qs],
        scratch_shapes=[pltpu.SemaphoreType.DMA((n,)), pltpu.SemaphoreType.DMA((n,))],
    )(*qs)


def _row_tile(rows, cap):
    return max(t for t in range(16, cap + 1, 16) if rows % t == 0)


def _add_own_half(g, recv, c, name):
    S, R, C = g.shape
    h = R // 2
    tr = _row_tile(h, 256)
    nt = h // tr

    def body(c_ref, g_ref, r_ref, o_ref, o16_ref):
        del c_ref
        s = g_ref[...] + r_ref[...]
        o_ref[...] = s
        o16_ref[...] = s.astype(bf16)

    ospec = pl.BlockSpec((1, tr, C), lambda s, i, cr: (s, i, 0))
    return pl.pallas_call(
        body, name=name,
        grid_spec=pltpu.PrefetchScalarGridSpec(
            num_scalar_prefetch=1, grid=(S, nt),
            in_specs=[pl.BlockSpec((1, tr, C), lambda s, i, cr: (s, cr[0] * nt + i, 0)), ospec],
            out_specs=[ospec, ospec]),
        out_shape=[SDS((S, h, C), f32), SDS((S, h, C), bf16)],
        compiler_params=_params(("parallel", "parallel"), 40),
    )(c, g, recv)


def _add_chip_parts(p, recv, kc, name):
    _, R, C = p.shape
    tr = _row_tile(R, 256)

    def body(k_ref, p_ref, r0, r1, r2, o_ref):
        del k_ref
        o_ref[...] = ((p_ref[0] + r0[0].astype(f32)) + r1[0].astype(f32)) + r2[0].astype(f32)

    rspec = lambda j: pl.BlockSpec((1, tr, C), lambda i, kr: (j, i, 0))
    return pl.pallas_call(
        body, name=name,
        grid_spec=pltpu.PrefetchScalarGridSpec(
            num_scalar_prefetch=1, grid=(R // tr,),
            in_specs=[pl.BlockSpec((1, tr, C), lambda i, kr: (kr[0], i, 0)), rspec(0), rspec(1), rspec(2)],
            out_specs=pl.BlockSpec((tr, C), lambda i, kr: (i, 0))),
        out_shape=SDS((R, C), f32), compiler_params=_params(("parallel",), 40),
    )(kc, p, recv, recv, recv)


def _rs_pair(gs, got, c, tag):
    pairs = [_add_own_half(g, r, c, f"rs_add_pair_{tag}_{a}") for a, (g, r) in enumerate(zip(gs, got))]
    return [p[0] for p in pairs], [p[1] for p in pairs]


def _rs_finish(ps, arrived, c, kc, tag):
    qs = [_add_chip_parts(p, r, kc, f"rs_add_chips_{tag}_{a}") for a, (p, r) in enumerate(zip(ps, arrived))]
    return qs, _share_halves(qs, tag)


def _adamw_math(w, g, m, v):
    m2 = ADAM_B1 * m + (1.0 - ADAM_B1) * g
    v2 = ADAM_B2 * v + (1.0 - ADAM_B2) * (g * g)
    m_hat = m2 / (1.0 - ADAM_B1 ** ADAM_STEP)
    v_hat = v2 / (1.0 - ADAM_B2 ** ADAM_STEP)
    return -ADAM_LR * (m_hat / (jnp.sqrt(v_hat) + ADAM_EPS) + ADAM_WD * w), m2, v2


def _adamw_shard(w, g_own, g_other, m, v, c, name):
    L, R, C = w.shape
    h = R // 2
    tr = _row_tile(h, 208)
    nh = h // tr

    def body(c_ref, w_ref, go0, gx0, go1, gx1, m_ref, v_ref, g_o, d_o, m_o, v_o):
        mine = (pl.program_id(1) // nh) == c_ref[0]
        first = pl.program_id(0) == 0
        g = jnp.where(mine, jnp.where(first, go0[...], go1[...]), jnp.where(first, gx0[...], gx1[...]))[None]
        d, m2, v2 = _adamw_math(w_ref[...], g, m_ref[...], v_ref[...])
        g_o[...], d_o[...], m_o[...], v_o[...] = g, d, m2, v2

    full = pl.BlockSpec((1, tr, C), lambda l, i, cr: (l, i, 0))
    half0 = pl.BlockSpec((tr, C), lambda l, i, cr: (jnp.where(l == 0, i % nh, nh - 1), 0))
    half1 = pl.BlockSpec((tr, C), lambda l, i, cr: (jnp.where(l == 1, i % nh, 0), 0))
    return pl.pallas_call(
        body, name=name,
        grid_spec=pltpu.PrefetchScalarGridSpec(num_scalar_prefetch=1, grid=(L, R // tr),
                                               in_specs=[full, half0, half0, half1, half1, full, full],
                                               out_specs=[full] * 4),
        out_shape=[SDS((L, R, C), f32)] * 4, compiler_params=_params(("parallel", "parallel"), 48),
    )(c, w, g_own[0], g_other[0], g_own[1], g_other[1], m, v)


def _adamw_packed(wmv, g, name):
    _, R, C = wmv.shape

    def body(wmv_ref, g_ref, out_ref):
        out_ref[0], out_ref[1], out_ref[2] = _adamw_math(wmv_ref[0], g_ref[...], wmv_ref[1], wmv_ref[2])

    vm = pl.BlockSpec(memory_space=pltpu.VMEM)
    return pl.pallas_call(body, name=name, in_specs=[vm, vm], out_specs=vm, out_shape=SDS((3, R, C), f32),
                          compiler_params=_params(None, 40))(wmv, g)


def _pack(arrs):
    flat = [a.reshape(-1, LANES) for a in arrs]
    rows = sum(f.shape[0] for f in flat)
    pad = (-rows) % 16
    if pad:
        flat.append(jnp.zeros((pad, LANES), f32))
    return jnp.concatenate(flat, axis=0)


def _unpack(buf, shapes):
    out, r = [], 0
    for s in shapes:
        n = math.prod(s) // LANES
        out.append(buf[r:r + n].reshape(s))
        r += n
    return out


REPLICATED = ("shift_mu", "w_decay0", "a0", "k_k", "k_a", "r_k", "ln_x_w", "ln_x_b", "v_mix0", "lb_logits", "g_norm_w",
              "ln_w", "ln_b")
CHIP_SMALL = (("w_decay_up", 2), ("a_up", 2), ("v_mix_down", 1), ("v_mix_up", 2))
WEIGHTS = ("w_in", "shift_mu", "w_decay0", "w_decay_up", "a0", "a_up", "k_k", "k_a", "r_k", "ln_x_w", "ln_x_b",
           "v_mix0", "v_mix_down", "v_mix_up", "lb_logits", "g_norm_w", "w_out", "ln_w", "ln_b")


def kernel(x, w_in, shift_mu, w_decay0, w_decay_up, a0, a_up, k_k, k_a, r_k, ln_x_w, ln_x_b, v_mix0, v_mix_down, v_mix_up, lb_logits, g_norm_w, w_out, ln_w, ln_b, loss_target, m_w_in, m_shift_mu, m_w_decay0, m_w_decay_up, m_a0, m_a_up, m_k_k, m_k_a, m_r_k, m_ln_x_w, m_ln_x_b, m_v_mix0, m_v_mix_down, m_v_mix_up, m_lb_logits, m_g_norm_w, m_w_out, m_ln_w, m_ln_b, v_w_in, v_shift_mu, v_w_decay0, v_w_decay_up, v_a0, v_a_up, v_k_k, v_k_a, v_r_k, v_ln_x_w, v_ln_x_b, v_v_mix0, v_v_mix_down, v_v_mix_up, v_lb_logits, v_g_norm_w, v_w_out, v_ln_w, v_ln_b):
    W = dict(w_in=w_in, shift_mu=shift_mu, w_decay0=w_decay0, w_decay_up=w_decay_up, a0=a0, a_up=a_up, k_k=k_k, k_a=k_a,
             r_k=r_k, ln_x_w=ln_x_w, ln_x_b=ln_x_b, v_mix0=v_mix0, v_mix_down=v_mix_down, v_mix_up=v_mix_up,
             lb_logits=lb_logits, g_norm_w=g_norm_w, w_out=w_out, ln_w=ln_w, ln_b=ln_b)
    M = dict(w_in=m_w_in, shift_mu=m_shift_mu, w_decay0=m_w_decay0, w_decay_up=m_w_decay_up, a0=m_a0, a_up=m_a_up,
             k_k=m_k_k, k_a=m_k_a, r_k=m_r_k, ln_x_w=m_ln_x_w, ln_x_b=m_ln_x_b, v_mix0=m_v_mix0,
             v_mix_down=m_v_mix_down, v_mix_up=m_v_mix_up, lb_logits=m_lb_logits, g_norm_w=m_g_norm_w, w_out=m_w_out,
             ln_w=m_ln_w, ln_b=m_ln_b)
    V = dict(w_in=v_w_in, shift_mu=v_shift_mu, w_decay0=v_w_decay0, w_decay_up=v_w_decay_up, a0=v_a0, a_up=v_a_up,
             k_k=v_k_k, k_a=v_k_a, r_k=v_r_k, ln_x_w=v_ln_x_w, ln_x_b=v_ln_x_b, v_mix0=v_v_mix0,
             v_mix_down=v_v_mix_down, v_mix_up=v_v_mix_up, lb_logits=v_lb_logits, g_norm_w=v_g_norm_w, w_out=v_w_out,
             ln_w=v_ln_w, ln_b=v_ln_b)
    mx, my, mc = _mesh_pos()
    kc = (2 * mx + my).astype(jnp.int32)
    c_arr = jnp.reshape(mc.astype(jnp.int32), (1,))
    kc_arr = jnp.reshape(kc, (1,))

    small_shard = _pack([W[n] for n, _ in CHIP_SMALL])
    tr_in = lambda a: jnp.swapaxes(a, 1, 2)
    in_halves = tr_in(w_in).astype(bf16).reshape(DEPTH, 2, SHARD_COLS // 2, D_MODEL)
    out_halves = jnp.swapaxes(w_out.astype(bf16).reshape(DEPTH, 2, D_MODEL // 8, D_MODEL), 0, 1)
    g_in0, g_small = _gather_shards([in_halves[0], small_shard.reshape(2, -1, LANES)])
    in_full = lambda g: g.reshape(IN_COLS, D_MODEL)
    shard_shapes = [W[n].shape for n, _ in CHIP_SMALL]
    g_small = g_small.reshape(4, -1, LANES)
    per_chip = [_unpack(g_small[k], shard_shapes) for k in range(4)]
    full_small = {n: jnp.concatenate([per_chip[k][i] for k in range(4)], axis=ax)
                  for i, (n, ax) in enumerate(CHIP_SMALL)}

    def assemble_later(gathered):
        g_in1, g_out = gathered
        return in_full(g_in1), jnp.transpose(g_out, (2, 0, 1, 3, 4)).reshape(DEPTH, D_MODEL, D_MODEL)

    in_slabs = lambda g: g.reshape(4, SHARD_COLS, D_MODEL)
    out_slabs = lambda g: g.reshape(4, D_MODEL // 4, D_MODEL)
    slabs, pairs, arrivals = {}, {}, {}

    def on_grads(l, dw_in, dw_out):
        slabs[l] = [in_slabs(dw_in), out_slabs(dw_out)]
        return slabs[l]

    def on_swapped(l, got):
        pairs[l], to_send = _rs_pair(slabs[l], got, c_arr, f"l{l}")
        return to_send

    def on_recv(l, arrived):
        arrivals[l] = arrived

    small_names = list(REPLICATED) + [n for n, _ in CHIP_SMALL]

    def on_small_grads(g):
        return _allreduce_part(_pack([g[n] for n in small_names]))

    loss_sum, gx, G, (small_sum,) = _device_step(
        x[0], loss_target[0], in_full(g_in0), [in_halves[1], out_halves], assemble_later, on_grads, on_swapped, on_recv,
        on_small_grads, shift_mu, w_decay0, full_small["w_decay_up"], a0, full_small["a_up"], k_k, k_a, r_k, ln_x_w,
        ln_x_b, v_mix0, full_small["v_mix_down"], full_small["v_mix_up"], lb_logits, g_norm_w, ln_w, ln_b)
    loss = lax.psum(loss_sum[0, 0], ("x", "y", "c"))

    own1, other1 = _rs_finish(pairs[1], arrivals[1], c_arr, kc_arr, "l1")
    own0, other0 = _rs_finish(pairs[0], arrivals[0], c_arr, kc_arr, "l0")
    gin_own, gout_own = [own0[0], own1[0]], [own0[1], own1[1]]
    gin_other, gout_other = [other0[0], other1[0]], [other0[1], other1[1]]

    small_full = dict(zip(small_names, _unpack(small_sum, [G[n].shape for n in small_names])))
    grads = {n: small_full[n] for n in REPLICATED}
    for n, ax in CHIP_SMALL:
        width = W[n].shape[ax]
        grads[n] = lax.dynamic_slice_in_dim(small_full[n], kc * width, width, axis=ax)

    delta, new_m, new_v = {}, {}, {}
    res = _adamw_shard(tr_in(w_in), gin_own, gin_other, tr_in(m_w_in), tr_in(v_w_in), c_arr, "adamw_w_in")
    grads["w_in"], delta["w_in"], new_m["w_in"], new_v["w_in"] = [tr_in(a) for a in res]
    grads["w_out"], delta["w_out"], new_m["w_out"], new_v["w_out"] = _adamw_shard(
        w_out, gout_own, gout_other, m_w_out, v_w_out, c_arr, "adamw_w_out")
    shapes = [W[n].shape for n in small_names]
    wmv = jnp.stack([_pack([src[n] for n in small_names]) for src in (W, M, V)])
    dmv = _adamw_packed(wmv, _pack([grads[n] for n in small_names]), "adamw_small")
    for n, d1, m1, v1 in zip(small_names, _unpack(dmv[0], shapes), _unpack(dmv[1], shapes), _unpack(dmv[2], shapes)):
        delta[n], new_m[n], new_v[n] = d1, m1, v1

    return (loss, gx[None], *[grads[n] for n in WEIGHTS], *[delta[n] for n in WEIGHTS],
            *[new_m[n] for n in WEIGHTS], *[new_v[n] for n in WEIGHTS])
```

```python
import functools
import math

import jax
import jax.numpy as jnp
from jax import lax
from jax.experimental import pallas as pl
from jax.experimental.pallas import tpu as pltpu

f32 = jnp.float32
bf16 = jnp.bfloat16
SDS = jax.ShapeDtypeStruct
MESH = pl.DeviceIdType.MESH

D_MODEL = 2048
DEPTH = 2
D_RWKV = 1024
D_HGRN = 1024
RWKV_HEAD = 64
RWKV_HEADS = 16
LORA = 64
VRES = 32
HG_HEAD = 128
HG_HEADS = 8
CHUNK = 64
RWKV_COLS = 4 * D_RWKV + 2 * LORA
IN_COLS = RWKV_COLS + 4 * D_HGRN
SHARD_COLS = IN_COLS // 4
LANES = 128
RW_BLKS = RWKV_COLS // LANES
ALPHA = (2 * DEPTH) ** 0.25
LN_EPS = 1e-5
GN_EPS = 64e-5
RMS_EPS = 1e-5
LB_FLOOR = 1e-30
EXP_M05 = math.exp(-0.5)
ADAM_LR, ADAM_B1, ADAM_B2, ADAM_EPS, ADAM_WD, ADAM_STEP = 0.001, 0.9, 0.999, 1e-08, 0.01, 10

TB = 128
TB_WIDE = 256
SCAN_TB = 8
SCAN_PARTS = 2
HSUB = 32
HIGHEST = lax.Precision.HIGHEST


def _params(sem=None, vmem_mb=None):
    kw = {}
    if sem is not None:
        kw["dimension_semantics"] = sem
    if vmem_mb is not None:
        kw["vmem_limit_bytes"] = vmem_mb << 20
    return pltpu.CompilerParams(**kw)


def _split(x, n):
    parts, r = [], x
    for i in range(n):
        p = r.astype(bf16)
        parts.append(p)
        if i + 1 < n:
            r = r - p.astype(f32)
    return parts


def _seg_raw(x, ee, n=3, per_tile=False):
    R = x.shape[0]
    if per_tile:
        outs = []
        for c in range(4):
            tile_parts = _split(x[:, 256 * c:256 * c + 256], n)
            res = jnp.dot(jnp.concatenate(tile_parts, axis=0), ee, preferred_element_type=f32)
            acc = res[0:R]
            for i in range(1, n):
                acc = acc + res[i * R:(i + 1) * R]
            outs.append(acc)
        return jnp.concatenate(outs, axis=1)
    parts = _split(x, n)
    lhs = jnp.concatenate([p[:, 256 * c:256 * c + 256] for p in parts for c in range(4)], axis=0)
    res = jnp.dot(lhs, ee, preferred_element_type=f32)

    def tile(c):
        acc = res[c * R:(c + 1) * R]
        for i in range(1, n):
            acc = acc + res[(4 * i + c) * R:(4 * i + c + 1) * R]
        return acc

    return jnp.concatenate([tile(c) for c in range(4)], axis=1)


@jax.custom_vjp
def _seg(x, ee):
    return _seg_raw(x, ee)


def _seg_fwd(x, ee):
    return _seg_raw(x, ee), ee


def _seg_bwd(ee, g):
    return _seg_raw(g, ee), jnp.zeros_like(ee)


_seg.defvjp(_seg_fwd, _seg_bwd)


def _const_mats():
    i256 = jnp.arange(256) // RWKV_HEAD
    ee = (i256[:, None] == i256[None, :]).astype(bf16)
    head = jnp.arange(D_RWKV) // RWKV_HEAD
    lane = jnp.arange(LANES)
    red = (head[:, None] == lane[None, :]).astype(bf16)
    exp = ((lane[:, None] % RWKV_HEADS == head[None, :]) & (lane[:, None] < 2 * RWKV_HEADS)).astype(bf16)
    return ee, red, exp


def _matmul(a, b, *, mode, tm, tn, tk, name, add=None, b_layer=None, rider=None):
    bs = b.shape if b_layer is None else b.shape[1:]
    lead = () if b_layer is None else (None,)
    bidx = (lambda *t: t) if b_layer is None else (lambda *t: (b_layer,) + t)
    if mode == "nn":
        (M, K), N = a.shape, bs[1]
        tm, tn, tk = min(tm, M), min(tn, N), min(tk, K)
        a_spec = pl.BlockSpec((tm, tk), lambda i, j, k: (i, k))
        b_spec = pl.BlockSpec(lead + (tk, tn), lambda i, j, k: bidx(k, j))
        dims = (((1,), (0,)), ((), ()))
    elif mode == "nt":
        (M, K), N = a.shape, bs[0]
        tm, tn, tk = min(tm, M), min(tn, N), min(tk, K)
        a_spec = pl.BlockSpec((tm, tk), lambda i, j, k: (i, k))
        b_spec = pl.BlockSpec(lead + (tn, tk), lambda i, j, k: bidx(j, k))
        dims = (((1,), (1,)), ((), ()))
    else:
        (K, M), N = a.shape, bs[1]
        tm, tn, tk = min(tm, M), min(tn, N), min(tk, K)
        a_spec = pl.BlockSpec((tk, tm), lambda i, j, k: (k, i))
        b_spec = pl.BlockSpec(lead + (tk, tn), lambda i, j, k: bidx(k, j))
        dims = (((0,), (0,)), ((), ()))
    assert M % tm == 0 and N % tn == 0 and K % tk == 0, (M, N, K, tm, tn, tk)
    has_add = add is not None
    n_in = 3 if has_add else 2
    grid = (M // tm, N // tn, K // tk)
    rd = rider if rider is not None else _Part([], [], [], [], [], lambda *a: None)
    rd.total = grid[0] * grid[1] * grid[2]

    def body(*refs):
        a_ref, b_ref = refs[:2]
        c_ref = refs[2] if has_add else None
        rd_ins, o_ref = refs[n_in:n_in + rd.n_in], refs[n_in + rd.n_in]
        rest = refs[n_in + rd.n_in + 1:]
        i, j, k = pl.program_id(0), pl.program_id(1), pl.program_id(2)
        rd.step((i * grid[1] + j) * grid[2] + k, rd_ins, rest[:rd.n_out], rest[rd.n_out:])
        p = lax.dot_general(a_ref[...].astype(bf16), b_ref[...].astype(bf16), dims, preferred_element_type=f32)

        @pl.when(k == 0)
        def _():
            o_ref[...] = p + c_ref[...] if has_add else p

        @pl.when(k > 0)
        def _():
            o_ref[...] += p

    o_spec = pl.BlockSpec((tm, tn), lambda i, j, k: (i, j))
    in_specs = [a_spec, b_spec] + ([o_spec] if has_add else []) + rd.in_specs
    args = (a, b) + ((add,) if has_add else ()) + tuple(rd.args)
    outs = pl.pallas_call(
        body, name=name, grid=grid, in_specs=in_specs, out_specs=[o_spec] + rd.out_specs,
        out_shape=[SDS((M, N), f32)] + rd.out_shape, scratch_shapes=rd.scratch,
        compiler_params=_params(("parallel", "parallel", "arbitrary") if rider is None else ("arbitrary",) * 3, 60),
    )(*args)
    return outs[0] if rider is None else (outs[0], list(outs[1:]))


def _rwkv_core(k_in, v_in, wdad, w0, wup, a0, aup, kkw, kaw, ee, vmix):
    w_raw = w0 + jnp.dot(jnp.tanh(wdad), wup, preferred_element_type=f32)
    decay = jnp.exp(-EXP_M05 * jax.nn.sigmoid(w_raw))
    a = jax.nn.sigmoid(a0 + jnp.dot(wdad, aup, preferred_element_type=f32))
    if vmix is None:
        v = v_in
    else:
        v0, vdn, vup, vfirst = vmix
        gate = jax.nn.sigmoid(v0 + jnp.dot(jnp.dot(v_in, vdn, preferred_element_type=f32), vup,
                                           preferred_element_type=f32))
        v = v_in + (vfirst - v_in) * gate
    kk = k_in * kkw
    n2 = _seg(kk * kk, ee)
    kk = kk / jnp.maximum(jnp.sqrt(n2), 1e-12)
    k2 = k_in * (1.0 + (a - 1.0) * kaw)
    return decay, k2, v, -kk, kk * a


def _shifted(y, prev_row):
    yp = pltpu.roll(y, 1, 0)
    row = lax.broadcasted_iota(jnp.int32, y.shape, 0)
    return jnp.where(row == 0, prev_row, yp)


def _rwkv_pre_fwd(proj, mu, w0, wup, a0, aup, kkw, kaw, ee, vmix, layer):
    T = proj.shape[0]
    nb = T // TB
    has_mix = vmix is not None

    def body(*refs):
        y_ref, p8_ref, mu_ref, w0_ref, wup_ref, a0_ref, aup_ref, kkw_ref, kaw_ref, ee_ref = refs[:10]
        rest = refs[10:]
        if has_mix:
            v0_ref, vdn_ref, vup_ref, vf_ref = rest[:4]
            rest = rest[4:]
        r_o, w_o, k_o, v_o, a_o, b_o, z_o = rest
        i = pl.program_id(0)
        y = y_ref[...]
        prev = jnp.where(i == 0, 0.0, p8_ref[7:8, :])
        rw = y + mu_ref[...] * (_shifted(y, prev) - y)
        mix = (v0_ref[...], vdn_ref[...], vup_ref[...], vf_ref[...]) if has_mix else None
        dec, k2, v, av, bv = _rwkv_core(rw[:, 1024:2048], rw[:, 2048:3072], rw[:, 4096:4224], w0_ref[...],
                                        wup_ref[...], a0_ref[...], aup_ref[...], kkw_ref[...], kaw_ref[...],
                                        ee_ref[...], mix)
        r_o[...] = rw[:, 0:1024]
        w_o[...] = dec
        k_o[...] = k2
        v_o[...] = v
        a_o[...] = av
        b_o[...] = bv
        z_o[...] = rw[:, 3072:4096]

    row = lambda n: pl.BlockSpec((1, n), lambda i: (0, 0))
    full = lambda s: pl.BlockSpec(s, lambda i: (0,) * len(s))
    blk = pl.BlockSpec((TB, D_RWKV), lambda i: (i, 0))
    in_specs = [pl.BlockSpec((TB, RWKV_COLS), lambda i: (i, 0)),
                pl.BlockSpec((8, RWKV_COLS), lambda i: (jnp.maximum(i * (TB // 8) - 1, 0), 0)),
                row(RWKV_COLS), row(D_RWKV), full((LANES, D_RWKV)), row(D_RWKV), full((LANES, D_RWKV)),
                row(D_RWKV), row(D_RWKV), full((256, 256))]
    args = [proj, proj, mu, w0, wup, a0, aup, kkw, kaw, ee]
    if has_mix:
        v0, vdn, vup, vfirst = vmix
        in_specs += [row(D_RWKV), full((D_RWKV, LANES)), full((LANES, D_RWKV)), blk]
        args += [v0, vdn, vup, vfirst]
    return pl.pallas_call(
        body, name=f"rwkv_pre_fwd_l{layer}", grid=(nb,), in_specs=in_specs, out_specs=[blk] * 7,
        out_shape=[SDS((T, D_RWKV), f32)] * 7, compiler_params=_params(("parallel",), 48),
    )(*args)


def _rwkv_pre_bwd(proj, mu, w0, wup, a0, aup, kkw, kaw, ee, vmix, cots, layer):
    T = proj.shape[0]
    nb = T // TB
    has_mix = vmix is not None
    cot_list = cots["r"] + [cots["w"]] + cots["k"] + cots["v"] + [cots["a"], cots["b"], cots["z"]]
    n_r, n_k, n_v = len(cots["r"]), len(cots["k"]), len(cots["v"])
    n_cot = len(cot_list)

    def body(*refs):
        y_ref, p8_ref, mu_ref, w0_ref, wup_ref, a0_ref, aup_ref, kkw_ref, kaw_ref, ee_ref = refs[:10]
        rest = refs[10:]
        if has_mix:
            v0_ref, vdn_ref, vup_ref, vf_ref = rest[:4]
            rest = rest[4:]
        cot_refs, rest = rest[:n_cot], rest[n_cot:]
        if has_mix:
            (dproj_o, dmu_o, dw0_o, dwup_o, da0_o, daup_o, dkkw_o, dkaw_o,
             dv0_o, dvdn_o, dvup_o, dvf_o, carry) = rest
        else:
            dproj_o, dmu_o, dw0_o, dwup_o, da0_o, daup_o, dkkw_o, dkaw_o, carry = rest
        i = pl.program_id(0)
        blk_i = nb - 1 - i
        y = y_ref[...]
        prev = jnp.where(blk_i == 0, 0.0, p8_ref[7:8, :])
        yp = _shifted(y, prev)
        mu_v = mu_ref[...]
        rw = y + mu_v * (yp - y)

        def sum_refs(rs):
            acc = rs[0][...]
            for r_ in rs[1:]:
                acc = acc + r_[...]
            return acc

        c = list(cot_refs)
        g_r = sum_refs(c[:n_r]); c = c[n_r:]
        g_w = c[0][...]; c = c[1:]
        g_k = sum_refs(c[:n_k]); c = c[n_k:]
        g_v = sum_refs(c[:n_v]); c = c[n_v:]
        g_a, g_b, g_z = c[0][...], c[1][...], c[2][...]

        ee_v = ee_ref[...]
        if has_mix:
            def fn(k_in, v_in, wdad, w0_, wup_, a0_, aup_, kkw_, kaw_, v0_, vdn_, vup_, vf_):
                return _rwkv_core(k_in, v_in, wdad, w0_, wup_, a0_, aup_, kkw_, kaw_, ee_v, (v0_, vdn_, vup_, vf_))
            prim = (rw[:, 1024:2048], rw[:, 2048:3072], rw[:, 4096:4224], w0_ref[...], wup_ref[...], a0_ref[...],
                    aup_ref[...], kkw_ref[...], kaw_ref[...], v0_ref[...], vdn_ref[...], vup_ref[...], vf_ref[...])
        else:
            def fn(k_in, v_in, wdad, w0_, wup_, a0_, aup_, kkw_, kaw_):
                return _rwkv_core(k_in, v_in, wdad, w0_, wup_, a0_, aup_, kkw_, kaw_, ee_v, None)
            prim = (rw[:, 1024:2048], rw[:, 2048:3072], rw[:, 4096:4224], w0_ref[...], wup_ref[...], a0_ref[...],
                    aup_ref[...], kkw_ref[...], kaw_ref[...])
        _, vjp = jax.vjp(fn, *prim)
        gs = vjp((g_w, g_k, g_v, g_a, g_b))
        d_k, d_v, d_wdad = gs[0], gs[1], gs[2]
        g_rw = jnp.concatenate([g_r, d_k, d_v, g_z, d_wdad], axis=1)

        @pl.when(i == 0)
        def _():
            carry[...] = jnp.zeros_like(carry)
            dmu_o[...] = jnp.zeros_like(dmu_o)
            dw0_o[...] = jnp.zeros_like(dw0_o)
            dwup_o[...] = jnp.zeros_like(dwup_o)
            da0_o[...] = jnp.zeros_like(da0_o)
            daup_o[...] = jnp.zeros_like(daup_o)
            dkkw_o[...] = jnp.zeros_like(dkkw_o)
            dkaw_o[...] = jnp.zeros_like(dkaw_o)
            if has_mix:
                dv0_o[...] = jnp.zeros_like(dv0_o)
                dvdn_o[...] = jnp.zeros_like(dvdn_o)
                dvup_o[...] = jnp.zeros_like(dvup_o)

        dmu_o[...] += jnp.sum(g_rw * (yp - y), axis=0, keepdims=True)
        dw0_o[...] += gs[3]
        dwup_o[...] += gs[4]
        da0_o[...] += gs[5]
        daup_o[...] += gs[6]
        dkkw_o[...] += gs[7]
        dkaw_o[...] += gs[8]
        if has_mix:
            dv0_o[...] += gs[9]
            dvdn_o[...] += gs[10]
            dvup_o[...] += gs[11]
            dvf_o[...] = gs[12]
        gm = g_rw * mu_v
        nxt = pltpu.roll(gm, TB - 1, 0)
        rowi = lax.broadcasted_iota(jnp.int32, gm.shape, 0)
        nxt = jnp.where(rowi == TB - 1, carry[...], nxt)
        dproj_o[...] = g_rw - gm + nxt
        carry[...] = gm[0:1, :]

    rev = lambda i: (nb - 1 - i, 0)
    row = lambda n: pl.BlockSpec((1, n), lambda i: (0, 0))
    full = lambda s: pl.BlockSpec(s, lambda i: (0,) * len(s))
    blk = pl.BlockSpec((TB, D_RWKV), rev)
    in_specs = [pl.BlockSpec((TB, RWKV_COLS), rev),
                pl.BlockSpec((8, RWKV_COLS), lambda i: (jnp.maximum((nb - 1 - i) * (TB // 8) - 1, 0), 0)),
                row(RWKV_COLS), row(D_RWKV), full((LANES, D_RWKV)), row(D_RWKV), full((LANES, D_RWKV)),
                row(D_RWKV), row(D_RWKV), full((256, 256))]
    args = [proj, proj, mu, w0, wup, a0, aup, kkw, kaw, ee]
    out_specs = [pl.BlockSpec((TB, RWKV_COLS), rev), row(RWKV_COLS), row(D_RWKV), full((LANES, D_RWKV)),
                 row(D_RWKV), full((LANES, D_RWKV)), row(D_RWKV), row(D_RWKV)]
    out_shape = [SDS((T, RWKV_COLS), f32), SDS((1, RWKV_COLS), f32), SDS((1, D_RWKV), f32), SDS((LANES, D_RWKV), f32),
                 SDS((1, D_RWKV), f32), SDS((LANES, D_RWKV), f32), SDS((1, D_RWKV), f32), SDS((1, D_RWKV), f32)]
    if has_mix:
        v0, vdn, vup, vfirst = vmix
        in_specs += [row(D_RWKV), full((D_RWKV, LANES)), full((LANES, D_RWKV)), blk]
        args += [v0, vdn, vup, vfirst]
        out_specs += [row(D_RWKV), full((D_RWKV, LANES)), full((LANES, D_RWKV)), blk]
        out_shape += [SDS((1, D_RWKV), f32), SDS((D_RWKV, LANES), f32), SDS((LANES, D_RWKV), f32),
                      SDS((T, D_RWKV), f32)]
    in_specs += [blk] * n_cot
    args += cot_list
    return pl.pallas_call(
        body, name=f"rwkv_pre_bwd_l{layer}", grid=(nb,), in_specs=in_specs, out_specs=out_specs, out_shape=out_shape,
        scratch_shapes=[pltpu.VMEM((1, RWKV_COLS), f32)],
        compiler_params=_params(("arbitrary",), 60),
    )(*args)


def _pack_cols(prod_ref, red):
    res = jnp.dot(prod_ref[...].astype(bf16), red, preferred_element_type=f32)
    out = res[0:RWKV_HEAD]
    for s in range(1, SCAN_TB):
        out = out + pltpu.roll(res[s * RWKV_HEAD:(s + 1) * RWKV_HEAD], RWKV_HEADS * s, 1)
    return out


def _unpack_cols(tile_ref, exp):
    rows = []
    for s in range(SCAN_TB):
        half = tile_ref[0, :, pl.ds((s // 4) * LANES, LANES)]
        rows.append(half if s % 4 == 0 else pltpu.roll(half, LANES - 2 * RWKV_HEADS * (s % 4), 1))
    return jnp.dot(jnp.concatenate(rows, axis=0), exp, preferred_element_type=f32)


def _to_col(v):
    T = v.shape[0]
    hi = lax.reduce_precision(v, 8, 7)
    parts = jnp.stack([hi, v - hi], axis=1).astype(bf16)
    parts = parts.reshape(T // SCAN_TB, SCAN_TB, 2, RWKV_HEADS, RWKV_HEAD)
    return jnp.transpose(parts, (0, 4, 1, 2, 3)).reshape(T // SCAN_TB, RWKV_HEAD, 2 * LANES)


def _from_col(vc):
    nb = vc.shape[0]
    tiles = vc.reshape(nb, RWKV_HEAD, SCAN_TB, RWKV_HEADS)
    return jnp.transpose(tiles, (0, 2, 3, 1)).reshape(nb * SCAN_TB, D_RWKV)


def _rwkv_scan_fwd(r, w, k, vcol, a, b, ee, red, exp, layer, rider, gather=()):
    T = r.shape[0]
    nb = T // SCAN_TB
    ng = len(gather)

    def body(*refs):
        r_ref, w_ref, k_ref, v_ref, a_ref, an_ref, b_ref, ee_ref, red_ref, exp_ref = refs[:10]
        rd_ins, refs = refs[10:10 + rider.n_in], refs[10 + rider.n_in:]
        g_ins, refs = refs[:ng], refs[ng:]
        o_ref, sp_ref = refs[:2]
        rd_outs, refs = refs[2:2 + rider.n_out], refs[2 + rider.n_out:]
        g_outs, refs = refs[:ng], refs[ng:]
        s_ref, sa_ref, vb_ref, po_ref = refs[:4]
        rd_scr, g_sems = refs[4:4 + rider.n_scr], refs[4 + rider.n_scr:]
        step = pl.program_id(0)
        rider.step(step, rd_ins, rd_outs, rd_scr)

        @pl.when(step == 0)
        def _():
            s_ref[...] = jnp.zeros_like(s_ref)
            sa_ref[...] = jnp.zeros_like(sa_ref)
            if ng:
                _gather_start(g_ins, g_outs, *g_sems)

        if ng:
            @pl.when(step == (3 * nb) // 4)
            def _():
                _gather_forward(g_ins, g_outs, *g_sems)

        ee_v = ee_ref[...]
        r_b, w_b, k_b, a_b, b_b = r_ref[...], w_ref[...], k_ref[...], a_ref[...], b_ref[...]
        rowi = lax.broadcasted_iota(jnp.int32, a_b.shape, 0)
        a_nx = jnp.where(rowi == SCAN_TB - 1, an_ref[0:1, :], pltpu.roll(a_b, SCAN_TB - 1, 0))
        wa = w_b * a_nx
        dots = _seg_raw(jnp.concatenate([b_b * a_nx, k_b * a_nx], axis=0), ee_v, 3)
        beta, kappa = dots[0:SCAN_TB], dots[SCAN_TB:2 * SCAN_TB]
        vb_ref[...] = _unpack_cols(v_ref, exp_ref[...])

        for t in range(SCAN_TB):
            row = lambda x: x[t:t + 1, :]
            blk = pl.ds(t * RWKV_HEAD, RWKV_HEAD)
            sp = s_ref[...]
            sa = sa_ref[...]
            sp_ref[t, 0] = sp
            sp_ref[t, 1] = sa
            vb = vb_ref[blk, :]
            ahead = _seg_raw(sp * row(wa), ee_v, SCAN_PARTS, per_tile=True)
            sn = sp * row(w_b) + sa * row(b_b) + vb * row(k_b)
            s_ref[...] = sn
            sa_ref[...] = ahead + sa * row(beta) + vb * row(kappa)
            po_ref[blk, :] = sn * row(r_b)
        o_ref[0] = _pack_cols(po_ref, red_ref[...])

        if ng:
            @pl.when(step == nb - 1)
            def _():
                _gather_finish(g_ins, g_outs, *g_sems)

    rows = pl.BlockSpec((SCAN_TB, D_RWKV), lambda i: (i, 0))
    rows_next = pl.BlockSpec((SCAN_TB, D_RWKV), lambda i: (jnp.minimum(i + 1, nb - 1), 0))
    col = pl.BlockSpec((1, RWKV_HEAD, LANES), lambda i: (i, 0, 0))
    col2 = pl.BlockSpec((1, RWKV_HEAD, 2 * LANES), lambda i: (i, 0, 0))
    full = lambda s: pl.BlockSpec(s, lambda i: (0,) * len(s))
    outs = pl.pallas_call(
        body, name=f"rwkv_scan_fwd_l{layer}", grid=(nb,),
        in_specs=[rows, rows, rows, col2, rows, rows_next, rows, full((256, 256)), full((D_RWKV, LANES)),
                  full((LANES, D_RWKV))] + rider.in_specs + [HBM_SPEC] * ng,
        out_specs=[col, pl.BlockSpec((SCAN_TB, 2, RWKV_HEAD, D_RWKV), lambda i: (i, 0, 0, 0))] + rider.out_specs
        + [HBM_SPEC] * ng,
        out_shape=[SDS((nb, RWKV_HEAD, LANES), f32), SDS((T, 2, RWKV_HEAD, D_RWKV), f32)] + rider.out_shape
        + _gather_out_shapes(gather),
        scratch_shapes=[pltpu.VMEM((RWKV_HEAD, D_RWKV), f32), pltpu.VMEM((RWKV_HEAD, D_RWKV), f32),
                        pltpu.VMEM((SCAN_TB * RWKV_HEAD, D_RWKV), f32),
                        pltpu.VMEM((SCAN_TB * RWKV_HEAD, D_RWKV), f32)] + rider.scratch
        + (_gather_sems(ng) if ng else []),
        compiler_params=_params(("arbitrary",), 52),
    )(r, w, k, vcol, a, a, b, ee, red, exp, *rider.args, *gather)
    return outs[0], outs[1], list(outs[2:2 + rider.n_out]), list(outs[2 + rider.n_out:])


def _rwkv_scan_bwd(r, w, k, vcol, a, b, sprev, docol, ee, red, exp, layer, send=()):
    T = r.shape[0]
    nb = T // SCAN_TB
    ns = len(send)

    def body(*refs):
        r_ref, w_ref, k_ref, v_ref, a_ref, b_ref, sp_ref, do_ref, ee_ref, red_ref, exp_ref = refs[:11]
        c_ins, refs = refs[11:11 + ns], refs[11 + ns:]
        dr_o, dw_o, dk_o, da_o, db_o, dv_o = refs[:6]
        c_outs, refs = refs[6:6 + ns], refs[6 + ns:]
        ds_ref, snext_ref, vb_ref, dob_ref, pdv_ref = refs[:5]
        c_sems = refs[5:]
        if ns:
            @pl.when(pl.program_id(0) == 0)
            def _():
                for cp in _chips_copies(c_ins, c_outs, *c_sems):
                    cp.start()

        ee_v = ee_ref[...]
        csum = lambda x: jnp.sum(x, axis=0, keepdims=True)
        nrow = SCAN_TB * RWKV_HEAD
        last = pl.ds(nrow - RWKV_HEAD, RWKV_HEAD)
        r_b, w_b, k_b, a_b, b_b = r_ref[...], w_ref[...], k_ref[...], a_ref[...], b_ref[...]
        vb_ref[...] = _unpack_cols(v_ref, exp_ref[...])
        dob_ref[...] = _unpack_cols(do_ref, exp_ref[...])

        @pl.when(pl.program_id(0) == 0)
        def _():
            ds_ref[...] = jnp.zeros_like(ds_ref)
            snext_ref[...] = (sp_ref[SCAN_TB - 1, 0] * w_b[SCAN_TB - 1:SCAN_TB, :]
                              + sp_ref[SCAN_TB - 1, 1] * b_b[SCAN_TB - 1:SCAN_TB, :]
                              + vb_ref[last, :] * k_b[SCAN_TB - 1:SCAN_TB, :])

        for s in range(SCAN_TB):
            t = SCAN_TB - 1 - s
            row = lambda x: x[t:t + 1, :]
            blk = pl.ds(t * RWKV_HEAD, RWKV_HEAD)
            sp, sa = sp_ref[t, 0], sp_ref[t, 1]
            sn = snext_ref[...] if s == 0 else sp_ref[t + 1, 0]
            vb, dob = vb_ref[blk, :], dob_ref[blk, :]
            ds = ds_ref[...] + dob * row(r_b)
            dsa = _seg_raw(ds * row(b_b), ee_v, SCAN_PARTS, per_tile=True)
            ds_ref[...] = ds * row(w_b) + dsa * row(a_b)
            dr_o[t:t + 1, :] = csum(sn * dob)
            pdv_ref[blk, :] = ds * row(k_b)
            dk_o[t:t + 1, :] = csum(ds * vb)
            db_o[t:t + 1, :] = csum(ds * sa)
            dw_o[t:t + 1, :] = csum(ds * sp)
            da_o[t:t + 1, :] = csum(sp * dsa)
        snext_ref[...] = sp_ref[0, 0]
        dv_o[0] = _pack_cols(pdv_ref, red_ref[...])

        if ns:
            @pl.when(pl.program_id(0) == nb - 1)
            def _():
                for cp in _chips_copies(c_ins, c_outs, *c_sems):
                    cp.wait()

    rows = pl.BlockSpec((SCAN_TB, D_RWKV), lambda i: (nb - 1 - i, 0))
    col = pl.BlockSpec((1, RWKV_HEAD, LANES), lambda i: (nb - 1 - i, 0, 0))
    col2 = pl.BlockSpec((1, RWKV_HEAD, 2 * LANES), lambda i: (nb - 1 - i, 0, 0))
    st = pl.BlockSpec((SCAN_TB, 2, RWKV_HEAD, D_RWKV), lambda i: (nb - 1 - i, 0, 0, 0))
    full = lambda s: pl.BlockSpec(s, lambda i: (0,) * len(s))
    big = pltpu.VMEM((SCAN_TB * RWKV_HEAD, D_RWKV), f32)
    outs = pl.pallas_call(
        body, name=f"rwkv_scan_bwd_l{layer}", grid=(nb,),
        in_specs=[rows, rows, rows, col2, rows, rows, st, col2, full((256, 256)), full((D_RWKV, LANES)),
                  full((LANES, D_RWKV))] + [HBM_SPEC] * ns,
        out_specs=[rows] * 5 + [col] + [HBM_SPEC] * ns,
        out_shape=[SDS((T, D_RWKV), f32)] * 5 + [SDS((nb, RWKV_HEAD, LANES), f32)] + _chips_out_shapes(send),
        scratch_shapes=[pltpu.VMEM((RWKV_HEAD, D_RWKV), f32)] * 2 + [big] * 3 + (_chips_sems(ns) if ns else []),
        compiler_params=_params(("arbitrary",), 56),
    )(r, w, k, vcol, a, b, sprev, docol, ee, red, exp, *send)
    return (*outs[:6], list(outs[6:]))


def _post_core(o, r, k2, v, z, gw, gb, rk, ee):
    inv = 1.0 / RWKV_HEAD
    mu = _seg(o, ee) * inv
    d = o - mu
    var = _seg(d * d, ee) * inv
    on = d * lax.rsqrt(var + GN_EPS) * gw + gb
    bonus = _seg(r * k2 * rk, ee) * v
    return (on + bonus) * jax.nn.silu(z)


def _rwkv_post_fwd(o, r, k2, v, z, gw, gb, rk, ee, cat, layer):
    T = o.shape[0]

    def body(o_ref, r_ref, k_ref, v_ref, z_ref, gw_ref, gb_ref, rk_ref, ee_ref, cat_in, out_ref):
        del cat_in
        out_ref[...] = _post_core(o_ref[...], r_ref[...], k_ref[...], v_ref[...], z_ref[...], gw_ref[...],
                                  gb_ref[...], rk_ref[...], ee_ref[...])

    blk = pl.BlockSpec((TB_WIDE, D_RWKV), lambda i: (i, 0))
    row = pl.BlockSpec((1, D_RWKV), lambda i: (0, 0))
    return pl.pallas_call(
        body, name=f"rwkv_post_fwd_l{layer}", grid=(T // TB_WIDE,),
        in_specs=[blk] * 5 + [row] * 3 + [pl.BlockSpec((256, 256), lambda i: (0, 0)), HBM_SPEC],
        out_specs=blk, out_shape=SDS((T, D_MODEL), f32), input_output_aliases={9: 0},
        compiler_params=_params(("parallel",), 40),
    )(o, r, k2, v, z, gw, gb, rk, ee, cat)


def _rwkv_post_bwd(o, r, k2, v, z, gw, gb, rk, ee, dcat, layer):
    T = o.shape[0]

    def body(o_ref, r_ref, k_ref, v_ref, z_ref, gw_ref, gb_ref, rk_ref, ee_ref, g_ref,
             do_o, dr_o, dk_o, dv_o, dz_o, dgw_o, dgb_o, drk_o):
        ee_v = ee_ref[...]
        fn = lambda o_, r_, k_, v_, z_, gw_, gb_, rk_: _post_core(o_, r_, k_, v_, z_, gw_, gb_, rk_, ee_v)
        _, vjp = jax.vjp(fn, o_ref[...], r_ref[...], k_ref[...], v_ref[...], z_ref[...], gw_ref[...], gb_ref[...],
                         rk_ref[...])
        gs = vjp(g_ref[...])
        do_o[...], dr_o[...], dk_o[...], dv_o[...], dz_o[...] = gs[:5]

        @pl.when(pl.program_id(0) == 0)
        def _():
            dgw_o[...] = jnp.zeros_like(dgw_o)
            dgb_o[...] = jnp.zeros_like(dgb_o)
            drk_o[...] = jnp.zeros_like(drk_o)

        dgw_o[...] += gs[5]
        dgb_o[...] += gs[6]
        drk_o[...] += gs[7]

    blk = pl.BlockSpec((TB_WIDE, D_RWKV), lambda i: (i, 0))
    row = pl.BlockSpec((1, D_RWKV), lambda i: (0, 0))
    return pl.pallas_call(
        body, name=f"rwkv_post_bwd_l{layer}", grid=(T // TB_WIDE,),
        in_specs=[blk] * 5 + [row] * 3 + [pl.BlockSpec((256, 256), lambda i: (0, 0)), blk],
        out_specs=[blk] * 5 + [row] * 3,
        out_shape=[SDS((T, D_RWKV), f32)] * 5 + [SDS((1, D_RWKV), f32)] * 3,
        compiler_params=_params(("arbitrary",), 48),
    )(o, r, k2, v, z, gw, gb, rk, ee, dcat)


def _hgrn_chunk(qr, fr, ii, z, S, lbl, gw, layer):
    L = CHUNK
    G = qr.shape[1] // HG_HEAD
    hd = lambda x, g: x[..., g * HG_HEAD:(g + 1) * HG_HEAD]
    heads = lambda f: jnp.concatenate([f(g) for g in range(G)], axis=-1)
    m = jnp.max(lbl, axis=0, keepdims=True)
    e = jnp.exp(lbl - m)
    sm = e / jnp.sum(e, axis=0, keepdims=True)
    if layer == 0:
        lb = sm[0:1] - sm[0:1]
    else:
        lb = (sm[0:1] + sm[1:2]) - sm[0:1]
    q = jax.nn.silu(qr)
    log_lb = jnp.log(jnp.maximum(lb, LB_FLOOR))
    lf = jnp.logaddexp(log_lb, jnp.log1p(-lb) + jax.nn.log_sigmoid(fr))
    k = (1.0 - lb) * jax.nn.sigmoid(-fr)
    ti = lax.broadcasted_iota(jnp.int32, (L, L), 0)
    si = lax.broadcasted_iota(jnp.int32, (L, L), 1)
    ltri = (si <= ti).astype(f32)
    b = jnp.dot(ltri, lf, precision=HIGHEST, preferred_element_type=f32)
    btot = jnp.dot(jnp.ones((L, L), f32), lf, precision=HIGHEST, preferred_element_type=f32)
    qd = q * jnp.exp(b)
    o = heads(lambda g: jnp.dot(hd(qd, g), S[g], preferred_element_type=f32))
    nsub = L // HSUB
    t3 = lax.broadcasted_iota(jnp.int32, (HSUB, HSUB, G * HG_HEAD), 0)
    s3 = lax.broadcasted_iota(jnp.int32, (HSUB, HSUB, G * HG_HEAD), 1)
    causal = s3 <= t3
    nt = (((1,), (1,)), ((), ()))
    tn = (((0,), (0,)), ((), ()))
    rows = []
    for I in range(nsub):
        sl = slice(I * HSUB, (I + 1) * HSUB)
        bI, qI, kI, iI = b[sl], q[sl], k[sl], ii[sl]
        d3 = bI[:, None, :] - bI[None, :, :]
        dec = jnp.where(causal, jnp.exp(jnp.where(causal, d3, 0.0)), 0.0)
        prod = qI[:, None, :] * dec * kI[None, :, :]
        oI = heads(lambda g: jnp.dot(jnp.sum(hd(prod, g), axis=-1), hd(iI, g), preferred_element_type=f32))
        for J in range(I):
            sj = slice(J * HSUB, (J + 1) * HSUB)
            bm = b[(J + 1) * HSUB - 1:(J + 1) * HSUB]
            qs = qI * jnp.exp(bI - bm)
            ks = k[sj] * jnp.exp(bm - b[sj])
            iJ = ii[sj]
            oI = oI + heads(lambda g: jnp.dot(
                lax.dot_general(hd(qs, g), hd(ks, g), nt, preferred_element_type=f32), hd(iJ, g),
                preferred_element_type=f32))
        rows.append(oI)
    o = o + jnp.concatenate(rows, axis=0)
    k_dec = k * jnp.exp(btot - b)
    ones = jnp.ones((L, HG_HEAD), f32)
    s_new = jnp.concatenate([
        (S[g] * jnp.exp(lax.dot_general(hd(lf, g), ones, tn, precision=HIGHEST, preferred_element_type=f32))
         + lax.dot_general(hd(k_dec, g), hd(ii, g), tn, preferred_element_type=f32))[None] for g in range(G)], axis=0)
    ms = heads(lambda g: jnp.broadcast_to(jnp.mean(hd(o * o, g), axis=-1, keepdims=True), (L, HG_HEAD)))
    out = o * lax.rsqrt(ms + RMS_EPS) * gw * jax.nn.silu(z)
    return out, s_new


class _Part:
    def __init__(self, args, in_specs, out_specs, out_shape, scratch, step):
        self.args, self.in_specs, self.out_specs, self.out_shape = args, in_specs, out_specs, out_shape
        self.scratch, self.step, self.total = scratch, step, None
        self.n_in, self.n_out, self.n_scr = len(args), len(out_shape), len(scratch)


def _run_part(part, steps, name):
    part.total = steps

    def body(*refs):
        ins, outs = refs[:part.n_in], refs[part.n_in:part.n_in + part.n_out]
        part.step(pl.program_id(0), ins, outs, refs[part.n_in + part.n_out:])

    return pl.pallas_call(body, name=name, grid=(steps,), in_specs=part.in_specs, out_specs=part.out_specs,
                          out_shape=part.out_shape, scratch_shapes=part.scratch,
                          compiler_params=_params(("arbitrary",), 48))(*part.args)


def _hg_index(T, rev, group=1):
    nc = T // CHUNK
    width = group * HG_HEAD
    head = lambda i: i // nc
    chunk = (lambda i: nc - 1 - i % nc) if rev else (lambda i: i % nc)
    first = lambda i: i % nc == 0
    pj = lambda off, g: pl.BlockSpec((CHUNK, HG_HEAD),
                                     lambda i: (chunk(i), RW_BLKS + off + group * head(i) + g))
    pspecs = [pj(kind * HG_HEADS, g) for kind in range(4) for g in range(group)]
    per_head = lambda rows: pl.BlockSpec((rows, width), lambda i: (0, head(i)))
    return nc, head, chunk, first, pspecs, per_head


def _hgrn_fwd_part(proj, lbl, gw, layer):
    T = proj.shape[0]
    nc, head, chunk, first, pspecs, per_head = _hg_index(T, False)

    def step(i, ins, outs, scr):
        q_ref, f_ref, i_ref, z_ref, lbl_ref, gw_ref = ins
        out_ref, sh_ref = outs
        s_ref, = scr

        @pl.when(first(i))
        def _():
            s_ref[...] = jnp.zeros_like(s_ref)

        s0 = s_ref[...]
        sh_ref[:, 0] = s0
        out, s_new = _hgrn_chunk(q_ref[...], f_ref[...], i_ref[...], z_ref[...], s0, lbl_ref[...], gw_ref[...], layer)
        out_ref[...] = out
        s_ref[...] = s_new

    return _Part(
        [proj, proj, proj, proj, lbl, gw], pspecs + [per_head(2), per_head(1)],
        [pl.BlockSpec((CHUNK, HG_HEAD), lambda i: (chunk(i), HG_HEADS + head(i))),
         pl.BlockSpec((1, 1, HG_HEAD, HG_HEAD), lambda i: (head(i), chunk(i), 0, 0))],
        [SDS((T, D_MODEL), f32), SDS((HG_HEADS, nc, HG_HEAD, HG_HEAD), f32)],
        [pltpu.VMEM((1, HG_HEAD, HG_HEAD), f32)], step)


HG_GROUP = 2


def _hgrn_bwd_part(proj, shist, dcat, lbl, gw, layer):
    T = proj.shape[0]
    G = HG_GROUP
    nc, head, chunk, first, pspecs, per_head = _hg_index(T, True, G)

    def step(i, ins, outs, scr):
        p_refs, (lbl_ref, gw_ref, sh_ref, g_ref) = ins[:4 * G], ins[4 * G:]
        dq_o, df_o, di_o, dz_o, dlbl_o, dgw_o = outs
        ds_ref, = scr

        @pl.when(first(i))
        def _():
            ds_ref[...] = jnp.zeros_like(ds_ref)
            dlbl_o[...] = jnp.zeros_like(dlbl_o)
            dgw_o[...] = jnp.zeros_like(dgw_o)

        fn = functools.partial(_hgrn_chunk, layer=layer)
        q, f, ii, z = (jnp.concatenate([p_refs[kind * G + g][...] for g in range(G)], axis=1) for kind in range(4))
        _, vjp = jax.vjp(fn, q, f, ii, z, sh_ref[:, 0], lbl_ref[...], gw_ref[...])
        dq, df, di, dz, ds, dl, dg = vjp((g_ref[...], ds_ref[...]))
        dq_o[...], df_o[...], di_o[...], dz_o[...] = dq, df, di, dz
        ds_ref[...] = ds
        dlbl_o[...] += dl
        dgw_o[...] += dg

    ospec = pl.BlockSpec((CHUNK, G * HG_HEAD), lambda i: (chunk(i), head(i)))
    return _Part(
        [proj] * (4 * G) + [lbl, gw, shist, dcat],
        pspecs + [per_head(2), per_head(1),
                  pl.BlockSpec((G, 1, HG_HEAD, HG_HEAD), lambda i: (head(i), chunk(i), 0, 0)),
                  pl.BlockSpec((CHUNK, G * HG_HEAD), lambda i: (chunk(i), HG_HEADS // G + head(i)))],
        [ospec] * 4 + [per_head(2), per_head(1)],
        [SDS((T, D_HGRN), f32)] * 4 + [SDS((2, D_HGRN), f32), SDS((1, D_HGRN), f32)],
        [pltpu.VMEM((G, HG_HEAD, HG_HEAD), f32)], step)


def _ln_core(h, y, w, b):
    u = ALPHA * h + y
    mu = jnp.mean(u, axis=-1, keepdims=True)
    d = u - mu
    var = jnp.mean(d * d, axis=-1, keepdims=True)
    return d * lax.rsqrt(var + LN_EPS) * w + b


def _ln_fwd(h, y, w, b, layer):
    T = h.shape[0]

    def body(h_ref, y_ref, w_ref, b_ref, o_ref):
        o_ref[...] = _ln_core(h_ref[...], y_ref[...], w_ref[...], b_ref[...])

    blk = pl.BlockSpec((TB_WIDE, D_MODEL), lambda i: (i, 0))
    row = pl.BlockSpec((1, D_MODEL), lambda i: (0, 0))
    return pl.pallas_call(body, name=f"ln_fwd_l{layer}", grid=(T // TB_WIDE,), in_specs=[blk, blk, row, row],
                          out_specs=blk, out_shape=SDS((T, D_MODEL), f32),
                          compiler_params=_params(("parallel",), 40))(h, y, w, b)


def _ln_bwd(h, y, w, b, g, layer):
    T = h.shape[0]

    def body(h_ref, y_ref, w_ref, b_ref, g_ref, dh_o, dy_o, dw_o, db_o):
        _, vjp = jax.vjp(_ln_core, h_ref[...], y_ref[...], w_ref[...], b_ref[...])
        dh, dy, dw, db = vjp(g_ref[...])
        dh_o[...] = dh
        dy_o[...] = dy.astype(bf16)

        @pl.when(pl.program_id(0) == 0)
        def _():
            dw_o[...] = jnp.zeros_like(dw_o)
            db_o[...] = jnp.zeros_like(db_o)

        dw_o[...] += dw
        db_o[...] += db

    blk = pl.BlockSpec((TB_WIDE, D_MODEL), lambda i: (i, 0))
    row = pl.BlockSpec((1, D_MODEL), lambda i: (0, 0))
    return pl.pallas_call(body, name=f"ln_bwd_l{layer}", grid=(T // TB_WIDE,), in_specs=[blk, blk, row, row, blk],
                          out_specs=[blk, blk, row, row],
                          out_shape=[SDS((T, D_MODEL), f32), SDS((T, D_MODEL), bf16)] + [SDS((1, D_MODEL), f32)] * 2,
                          compiler_params=_params(("arbitrary",), 48))(h, y, w, b, g)


def _loss_head(hout, target):
    T = hout.shape[0]

    def body(h_ref, t_ref, loss_o, g_o):
        err = h_ref[...] - t_ref[...]
        g_o[...] = err * (1.0 / D_MODEL)

        @pl.when(pl.program_id(0) == 0)
        def _():
            loss_o[...] = jnp.zeros_like(loss_o)

        part = jnp.sum(jnp.sum(err * err, axis=-1, keepdims=True) * (1.0 / D_MODEL), axis=0, keepdims=True)
        loss_o[...] += 0.5 * part

    blk = pl.BlockSpec((TB_WIDE, D_MODEL), lambda i: (i, 0))
    return pl.pallas_call(body, name="loss_head", grid=(T // TB_WIDE,), in_specs=[blk, blk],
                          out_specs=[pl.BlockSpec((1, 1), lambda i: (0, 0)), blk],
                          out_shape=[SDS((1, 1), f32), SDS((T, D_MODEL), f32)],
                          compiler_params=_params(("arbitrary",), 40))(hout, target)


def _pad_rows(m, lo, total):
    return jnp.pad(m, ((0, 0), (lo, total - lo - m.shape[1]), (0, 0)))


def _device_step(x, target, w_in0, later_shards, assemble_later, on_grads, on_swapped, on_recv, on_small_grads,
                 shift_mu, w_decay0, w_decay_up, a0, a_up, k_k, k_a, r_k, ln_x_w, ln_x_b,
                 v_mix0, v_mix_down, v_mix_up, lb_logits, g_norm_w, ln_w, ln_b):
    T = x.shape[0]
    ee, red, exp = _const_mats()
    wup_pad = _pad_rows(w_decay_up, 0, LANES)
    aup_pad = _pad_rows(a_up, LORA, LANES)
    vdn_pad = jnp.pad(v_mix_down, ((0, 0), (0, 0), (0, LANES - VRES)))
    vup_pad = _pad_rows(v_mix_up, 0, LANES)
    row = lambda p, l: p[l][None, :]

    def vmix_of(l, vfirst):
        if l == 0:
            return None
        return (row(v_mix0, l - 1), vdn_pad[l - 1], vup_pad[l - 1], vfirst)

    h = x
    saved = []
    vfirst = None
    w_in = [w_in0, None]
    w_out = None
    for l in range(DEPTH):
        hb = h.astype(bf16)
        proj = _matmul(hb, w_in[l], mode="nt", tm=1024, tn=1664, tk=D_MODEL, name=f"proj_fwd_l{l}")
        pre_args = (proj, row(shift_mu, l), row(w_decay0, l), wup_pad[l], row(a0, l), aup_pad[l], row(k_k, l),
                    row(k_a, l), ee, vmix_of(l, vfirst))
        r, w, k2, v, av, bv, z = _rwkv_pre_fwd(*pre_args, layer=l)
        if l == 0:
            vfirst = v
        vcol = _to_col(v)
        ocol, sprev, (cat, shist), gathered = _rwkv_scan_fwd(
            r, w, k2, vcol, av, bv, ee, red, exp, l, _hgrn_fwd_part(proj, lb_logits, row(g_norm_w, l), l),
            gather=later_shards if l == 0 else [])
        if l == 0:
            w_in[1], w_out = assemble_later(gathered)
        o = _from_col(ocol)
        post_args = (o, r, k2, v, z, row(ln_x_w, l), row(ln_x_b, l), row(r_k, l), ee)
        cat = _rwkv_post_fwd(*post_args, cat, layer=l)
        catb = cat.astype(bf16)
        y = _matmul(catb, w_out, b_layer=l, mode="nn", tm=1024, tn=1024, tk=D_MODEL, name=f"out_fwd_l{l}")
        h_new = _ln_fwd(h, y, row(ln_w, l), row(ln_b, l), l)
        saved.append(dict(h=h, hb=hb, proj=proj, pre_args=pre_args, post_args=post_args,
                          scan=(r, w, k2, vcol, av, bv, sprev), cat=catb, shist=shist, y=y))
        h = h_new

    loss_sum, g = _loss_head(h, target)

    grads = {n: [None] * DEPTH for n in ("w_in", "w_out", "shift_mu", "w_decay0", "w_decay_up", "a0", "a_up", "k_k",
                                         "k_a", "r_k", "ln_x_w", "ln_x_b", "g_norm_w", "ln_w", "ln_b")}
    dlbl = []
    dvfirst = None
    to_send = []
    for l in reversed(range(DEPTH)):
        s = saved[l]
        dh_res, dy, dlnw, dlnb = _ln_bwd(s["h"], s["y"], row(ln_w, l), row(ln_b, l), g, l)
        grads["ln_w"][l], grads["ln_b"][l] = dlnw[0], dlnb[0]
        dcat = _matmul(dy, w_out, b_layer=l, mode="nt", tm=1024, tn=1024, tk=D_MODEL, name=f"out_bwd_dx_l{l}")
        grads["w_out"][l] = _matmul(s["cat"], dy, mode="tn", tm=1024, tn=1024, tk=T, name=f"out_bwd_dw_l{l}")
        do, dr_p, dk_p, dv_p, dz, dgw, dgb, drk = _rwkv_post_bwd(*s["post_args"], dcat, layer=l)
        grads["ln_x_w"][l], grads["ln_x_b"][l], grads["r_k"][l] = dgw[0], dgb[0], drk[0]
        r, w, k2, vcol, av, bv, sprev = s["scan"]
        dr_s, dw_s, dk_s, da_s, db_s, dvcol, arrived = _rwkv_scan_bwd(
            r, w, k2, vcol, av, bv, sprev, _to_col(do), ee, red, exp, l, send=to_send if l == 0 else [])
        dq, df, di, dzh, dl, dgn = _run_part(
            _hgrn_bwd_part(s["proj"], s["shist"], dcat, lb_logits, row(g_norm_w, l), l),
            (T // CHUNK) * HG_HEADS // HG_GROUP, f"hgrn_bwd_l{l}")
        if l == 0:
            on_recv(1, arrived)
        v_cots = [dv_p, _from_col(dvcol)] + ([dvfirst] if (l == 0 and dvfirst is not None) else [])
        cots = dict(r=[dr_p, dr_s], w=dw_s, k=[dk_p, dk_s], v=v_cots, a=da_s, b=db_s, z=dz)
        outs = _rwkv_pre_bwd(*s["pre_args"], cots, layer=l)
        dproj, dmu, dw0, dwup, da0, daup, dkkw, dkaw = outs[:8]
        grads["shift_mu"][l], grads["w_decay0"][l], grads["a0"][l] = dmu[0], dw0[0], da0[0]
        grads["k_k"][l], grads["k_a"][l] = dkkw[0], dkaw[0]
        grads["w_decay_up"][l], grads["a_up"][l] = dwup[:LORA], daup[LORA:]
        if l > 0:
            dv0, dvdn, dvup, dvfirst = outs[8:]
            g_vmix = (dv0, dvdn[:, :VRES][None], dvup[:VRES][None])
        dlbl.append(dl)
        grads["g_norm_w"][l] = dgn[0]
        dproj = jnp.concatenate([dproj, dq, df, di, dzh], axis=1).astype(bf16)
        dw_args = dict(mode="tn", tm=1664, tn=1024, tk=T, name=f"proj_bwd_dw_l{l}")
        dx_args = dict(mode="nn", tm=1024, tn=1024, tk=1664, name=f"proj_bwd_dx_l{l}", add=dh_res)
        if l == 1:
            grads["w_in"][l] = _matmul(dproj, s["hb"], **dw_args)
            g, swapped = _matmul(dproj, w_in[l], rider=_swap_part(on_grads(l, grads["w_in"][l], grads["w_out"][l])),
                                 **dx_args)
            to_send = on_swapped(l, swapped)
        else:
            small = {n: jnp.stack(v) for n, v in grads.items() if n not in ("w_in", "w_out")}
            small["lb_logits"] = dlbl[0] + dlbl[1]
            small["v_mix0"], small["v_mix_down"], small["v_mix_up"] = g_vmix
            grads["w_in"][l], small_out = _matmul(dproj, s["hb"], rider=on_small_grads(small), **dw_args)
            swapped = _run_part(_swap_part(on_grads(l, grads["w_in"][l], grads["w_out"][l])), 1, "rs_swap_halves_l0")
            g, arrived = _matmul(dproj, w_in[l], rider=_chips_part(on_swapped(l, swapped)), **dx_args)
            on_recv(0, arrived)
    small["w_in"], small["w_out"] = grads["w_in"], grads["w_out"]
    return loss_sum, g, small, small_out


CHIP_COMBOS = ((1, 0), (0, 1), (1, 1))
HBM_SPEC = pl.BlockSpec(memory_space=pl.ANY)


def _mesh_pos():
    return lax.axis_index("x"), lax.axis_index("y"), lax.axis_index("c")


def _flip(v, d):
    return 1 - v if d else v


def _half(ref, dim, cc, hs):
    idx = [slice(None)] * len(ref.shape)
    idx[dim] = pl.ds(cc * hs, hs)
    return ref.at[tuple(idx)]


def _gather_shards(arrs):
    n = len(arrs)

    def body(*refs):
        ins, outs = refs[:n], refs[n:2 * n]
        sems = refs[2 * n:]
        _gather_start(ins, outs, *sems)
        _gather_forward(ins, outs, *sems)
        _gather_finish(ins, outs, *sems)

    return pl.pallas_call(
        body, name="gather_shards", in_specs=[HBM_SPEC] * n, out_specs=[HBM_SPEC] * n,
        out_shape=_gather_out_shapes(arrs), scratch_shapes=_gather_sems(n),
    )(*arrs)


def _gather_out_shapes(arrs):
    return [SDS((4,) + a.shape, a.dtype) for a in arrs]


def _gather_sems(n):
    return [pltpu.SemaphoreType.DMA((n, 6)), pltpu.SemaphoreType.DMA((n, 6)), pltpu.SemaphoreType.DMA((n,))]


def _gather_copies(kind, ins, outs, send_sems, recv_sems, local_sems):
    x, y, c = _mesh_pos()
    kc = 2 * x + y
    cps = []
    for a in range(len(ins)):
        if kind == "local":
            cps.append(pltpu.make_async_copy(ins[a], outs[a].at[kc], local_sems.at[a]))
            continue
        for j, (dx, dy) in enumerate(CHIP_COMBOS):
            px, py = _flip(x, dx), _flip(y, dy)
            kj = 2 * px + py
            src, dst, s, to = {
                "sends": (ins[a].at[c], outs[a].at[kc, c], j, (px, py, c)),
                "landed": (outs[a].at[kj, c], outs[a].at[kj, c], j, (px, py, c)),
                "forwards": (outs[a].at[kj, c], outs[a].at[kj, c], 3 + j, (x, y, 1 - c)),
                "passed": (outs[a].at[kj, 1 - c], outs[a].at[kj, 1 - c], 3 + j, (x, y, 1 - c)),
            }[kind]
            cps.append(pltpu.make_async_remote_copy(
                src_ref=src, dst_ref=dst, send_sem=send_sems.at[a, s], recv_sem=recv_sems.at[a, s], device_id=to,
                device_id_type=MESH))
    return cps


def _gather_start(*refs):
    for cp in _gather_copies("local", *refs) + _gather_copies("sends", *refs):
        cp.start()


def _gather_forward(*refs):
    for arrival, fw in zip(_gather_copies("landed", *refs), _gather_copies("forwards", *refs)):
        arrival.wait_recv()
        fw.start()


def _gather_finish(*refs):
    for cp in _gather_copies("passed", *refs):
        cp.wait_recv()
    for cp in _gather_copies("sends", *refs) + _gather_copies("forwards", *refs):
        cp.wait_send()
    for cp in _gather_copies("local", *refs):
        cp.wait()


DEV_COMBOS = tuple((dx, dy, dc) for dx in (0, 1) for dy in (0, 1) for dc in (0, 1))[1:]


def _allreduce_part(buf, rep_rows, chip_rows):
    R0, Q = rep_rows, chip_rows
    whole = lambda rows: pl.BlockSpec((rows, LANES), lambda *_: (0, 0))

    def copies(x_ref, slots, send_sems, recv_sems):
        x, y, c = _mesh_pos()
        cps = []
        for k, (dx, dy, dc) in enumerate(DEV_COMBOS):
            px, py = _flip(x, dx), _flip(y, dy)
            to = (px, py, _flip(c, dc))
            cps.append(pltpu.make_async_remote_copy(
                src_ref=x_ref.at[pl.ds(0, R0)], dst_ref=slots.at[k + 1, pl.ds(0, R0)], send_sem=send_sems.at[0, k],
                recv_sem=recv_sems.at[0, k], device_id=to, device_id_type=MESH))
            if Q:
                cps.append(pltpu.make_async_remote_copy(
                    src_ref=x_ref.at[pl.ds(pl.multiple_of(R0 + (2 * px + py) * Q, 8), Q)],
                    dst_ref=slots.at[k + 1, pl.ds(R0, Q)], send_sem=send_sems.at[1, k], recv_sem=recv_sems.at[1, k],
                    device_id=to, device_id_type=MESH))
        return cps

    def step(i, ins, outs, scr):
        x_ref, = ins
        o_ref, = outs
        slots = scr[0]

        @pl.when(i == 0)
        def _():
            for cp in copies(x_ref, *scr):
                cp.start()
            x, y, _ = _mesh_pos()
            slots[0, pl.ds(0, R0)] = x_ref[pl.ds(0, R0), :]
            if Q:
                slots[0, pl.ds(R0, Q)] = x_ref[pl.ds(pl.multiple_of(R0 + (2 * x + y) * Q, 8), Q), :]

        @pl.when(i == part.total - 1)
        def _():
            x, y, c = _mesh_pos()
            me = 4 * x + 2 * y + c
            cps = copies(x_ref, *scr)
            for cp in cps:
                cp.wait_recv()
            acc = slots[jnp.bitwise_xor(me, 0)]
            for d in range(1, 8):
                acc = acc + slots[jnp.bitwise_xor(me, d)]
            o_ref[...] = acc
            for cp in cps:
                cp.wait_send()

    part = _Part([buf], [whole(R0 + 4 * Q)], [whole(R0 + Q)], [SDS((R0 + Q, LANES), f32)],
                 [pltpu.VMEM((8, R0 + Q, LANES), f32), pltpu.SemaphoreType.DMA((2, 7)),
                  pltpu.SemaphoreType.DMA((2, 7))], step)
    return part


def _chips_part(slabs):
    n = len(slabs)

    def step(i, ins, outs, scr):
        if not n:
            return

        @pl.when(i == 0)
        def _():
            for cp in _chips_copies(ins, outs, *scr):
                cp.start()

        @pl.when(i == part.total - 1)
        def _():
            for cp in _chips_copies(ins, outs, *scr):
                cp.wait()

    part = _Part(list(slabs), [HBM_SPEC] * n, [HBM_SPEC] * n, _chips_out_shapes(slabs),
                 _chips_sems(n) if n else [], step)
    return part


def _swap_part(gs):
    n = len(gs)

    def copies(ins, outs, send_sems, recv_sems):
        x, y, c = _mesh_pos()
        return [pltpu.make_async_remote_copy(
            src_ref=_half(ins[a], 1, 1 - c, ins[a].shape[1] // 2), dst_ref=outs[a], send_sem=send_sems.at[a],
            recv_sem=recv_sems.at[a], device_id=(x, y, 1 - c), device_id_type=MESH) for a in range(n)]

    def step(i, ins, outs, scr):
        if not n:
            return

        @pl.when(i == 0)
        def _():
            for cp in copies(ins, outs, *scr):
                cp.start()

        @pl.when(i == part.total - 1)
        def _():
            for cp in copies(ins, outs, *scr):
                cp.wait()

    part = _Part(list(gs), [HBM_SPEC] * n, [HBM_SPEC] * n,
                 [SDS((g.shape[0], g.shape[1] // 2, g.shape[2]), g.dtype) for g in gs],
                 [pltpu.SemaphoreType.DMA((n,)), pltpu.SemaphoreType.DMA((n,))] if n else [], step)
    return part


def _chips_copies(ins, outs, send_sems, recv_sems):
    x, y, c = _mesh_pos()
    cps = []
    for a in range(len(ins)):
        for j, (dx, dy) in enumerate(CHIP_COMBOS):
            px, py = _flip(x, dx), _flip(y, dy)
            cps.append(pltpu.make_async_remote_copy(
                src_ref=ins[a].at[2 * px + py], dst_ref=outs[a].at[j], send_sem=send_sems.at[a, j],
                recv_sem=recv_sems.at[a, j], device_id=(px, py, c), device_id_type=MESH))
    return cps


def _chips_out_shapes(ps):
    return [SDS((3,) + p.shape[1:], p.dtype) for p in ps]


def _chips_sems(n):
    return [pltpu.SemaphoreType.DMA((n, 3)), pltpu.SemaphoreType.DMA((n, 3))]


def _share_halves(qs, tag):
    n = len(qs)

    def body(*refs):
        ins, outs = refs[:n], refs[n:2 * n]
        send_sems, recv_sems = refs[2 * n:]
        x, y, c = _mesh_pos()
        cps = []
        for a in range(n):
            cp = pltpu.make_async_remote_copy(
                src_ref=ins[a], dst_ref=outs[a], send_sem=send_sems.at[a], recv_sem=recv_sems.at[a],
                device_id=(x, y, 1 - c), device_id_type=MESH)
            cp.start()
            cps.append(cp)
        for cp in cps:
            cp.wait()

    return pl.pallas_call(
        body, name=f"rs_share_halves_{tag}", in_specs=[HBM_SPEC] * n, out_specs=[HBM_SPEC] * n,
        out_shape=[SDS(q.shape, q.dtype) for q in qs],
        scratch_shapes=[pltpu.SemaphoreType.DMA((n,)), pltpu.SemaphoreType.DMA((n,))],
    )(*qs)


def _row_tile(rows, cap):
    return max(t for t in range(16, cap + 1, 16) if rows % t == 0)


def _add_own_half(g, recv, c, name):
    S, R, C = g.shape
    h = R // 2
    tr = _row_tile(h, 256)
    nt = h // tr

    def body(c_ref, g_ref, r_ref, o_ref, o16_ref):
        del c_ref
        s = g_ref[...] + r_ref[...]
        o_ref[...] = s
        o16_ref[...] = s.astype(bf16)

    ospec = pl.BlockSpec((1, tr, C), lambda s, i, cr: (s, i, 0))
    return pl.pallas_call(
        body, name=name,
        grid_spec=pltpu.PrefetchScalarGridSpec(
            num_scalar_prefetch=1, grid=(S, nt),
            in_specs=[pl.BlockSpec((1, tr, C), lambda s, i, cr: (s, cr[0] * nt + i, 0)), ospec],
            out_specs=[ospec, ospec]),
        out_shape=[SDS((S, h, C), f32), SDS((S, h, C), bf16)],
        compiler_params=_params(("parallel", "parallel"), 40),
    )(c, g, recv)


def _add_chip_parts(p, recv, kc, name):
    _, R, C = p.shape
    tr = _row_tile(R, 256)

    def body(k_ref, p_ref, r0, r1, r2, o_ref):
        del k_ref
        o_ref[...] = ((p_ref[0] + r0[0].astype(f32)) + r1[0].astype(f32)) + r2[0].astype(f32)

    rspec = lambda j: pl.BlockSpec((1, tr, C), lambda i, kr: (j, i, 0))
    return pl.pallas_call(
        body, name=name,
        grid_spec=pltpu.PrefetchScalarGridSpec(
            num_scalar_prefetch=1, grid=(R // tr,),
            in_specs=[pl.BlockSpec((1, tr, C), lambda i, kr: (kr[0], i, 0)), rspec(0), rspec(1), rspec(2)],
            out_specs=pl.BlockSpec((tr, C), lambda i, kr: (i, 0))),
        out_shape=SDS((R, C), f32), compiler_params=_params(("parallel",), 40),
    )(kc, p, recv, recv, recv)


def _rs_pair(gs, got, c, tag):
    pairs = [_add_own_half(g, r, c, f"rs_add_pair_{tag}_{a}") for a, (g, r) in enumerate(zip(gs, got))]
    return [p[0] for p in pairs], [p[1] for p in pairs]


def _rs_finish(ps, arrived, c, kc, tag):
    qs = [_add_chip_parts(p, r, kc, f"rs_add_chips_{tag}_{a}") for a, (p, r) in enumerate(zip(ps, arrived))]
    return qs, _share_halves(qs, tag)


def _adamw_math(w, g, m, v):
    m2 = ADAM_B1 * m + (1.0 - ADAM_B1) * g
    v2 = ADAM_B2 * v + (1.0 - ADAM_B2) * (g * g)
    m_hat = m2 / (1.0 - ADAM_B1 ** ADAM_STEP)
    v_hat = v2 / (1.0 - ADAM_B2 ** ADAM_STEP)
    return -ADAM_LR * (m_hat / (jnp.sqrt(v_hat) + ADAM_EPS) + ADAM_WD * w), m2, v2


def _adamw_shard(w, g_own, g_other, m, v, c, name):
    L, R, C = w.shape
    h = R // 2
    tr = _row_tile(h, 208)
    nh = h // tr

    def body(c_ref, w_ref, go0, gx0, go1, gx1, m_ref, v_ref, g_o, d_o, m_o, v_o):
        mine = (pl.program_id(1) // nh) == c_ref[0]
        first = pl.program_id(0) == 0
        g = jnp.where(mine, jnp.where(first, go0[...], go1[...]), jnp.where(first, gx0[...], gx1[...]))[None]
        d, m2, v2 = _adamw_math(w_ref[...], g, m_ref[...], v_ref[...])
        g_o[...], d_o[...], m_o[...], v_o[...] = g, d, m2, v2

    full = pl.BlockSpec((1, tr, C), lambda l, i, cr: (l, i, 0))
    half0 = pl.BlockSpec((tr, C), lambda l, i, cr: (jnp.where(l == 0, i % nh, nh - 1), 0))
    half1 = pl.BlockSpec((tr, C), lambda l, i, cr: (jnp.where(l == 1, i % nh, 0), 0))
    return pl.pallas_call(
        body, name=name,
        grid_spec=pltpu.PrefetchScalarGridSpec(num_scalar_prefetch=1, grid=(L, R // tr),
                                               in_specs=[full, half0, half0, half1, half1, full, full],
                                               out_specs=[full] * 4),
        out_shape=[SDS((L, R, C), f32)] * 4, compiler_params=_params(("parallel", "parallel"), 48),
    )(c, w, g_own[0], g_other[0], g_own[1], g_other[1], m, v)


def _adamw_packed(wmv, g, name):
    _, R, C = wmv.shape

    def body(wmv_ref, g_ref, out_ref):
        out_ref[0], out_ref[1], out_ref[2] = _adamw_math(wmv_ref[0], g_ref[...], wmv_ref[1], wmv_ref[2])

    vm = pl.BlockSpec(memory_space=pltpu.VMEM)
    return pl.pallas_call(body, name=name, in_specs=[vm, vm], out_specs=vm, out_shape=SDS((3, R, C), f32),
                          compiler_params=_params(None, 40))(wmv, g)


def _pack(arrs):
    flat = [a.reshape(-1, LANES) for a in arrs]
    rows = sum(f.shape[0] for f in flat)
    pad = (-rows) % 16
    if pad:
        flat.append(jnp.zeros((pad, LANES), f32))
    return jnp.concatenate(flat, axis=0)


def _unpack(buf, shapes):
    out, r = [], 0
    for s in shapes:
        n = math.prod(s) // LANES
        out.append(buf[r:r + n].reshape(s))
        r += n
    return out


REPLICATED = ("shift_mu", "w_decay0", "a0", "k_k", "k_a", "r_k", "ln_x_w", "ln_x_b", "v_mix0", "lb_logits", "g_norm_w",
              "ln_w", "ln_b")
CHIP_SMALL = (("w_decay_up", 2), ("a_up", 2), ("v_mix_down", 1), ("v_mix_up", 2))
WEIGHTS = ("w_in", "shift_mu", "w_decay0", "w_decay_up", "a0", "a_up", "k_k", "k_a", "r_k", "ln_x_w", "ln_x_b",
           "v_mix0", "v_mix_down", "v_mix_up", "lb_logits", "g_norm_w", "w_out", "ln_w", "ln_b")


def kernel(x, w_in, shift_mu, w_decay0, w_decay_up, a0, a_up, k_k, k_a, r_k, ln_x_w, ln_x_b, v_mix0, v_mix_down, v_mix_up, lb_logits, g_norm_w, w_out, ln_w, ln_b, loss_target, m_w_in, m_shift_mu, m_w_decay0, m_w_decay_up, m_a0, m_a_up, m_k_k, m_k_a, m_r_k, m_ln_x_w, m_ln_x_b, m_v_mix0, m_v_mix_down, m_v_mix_up, m_lb_logits, m_g_norm_w, m_w_out, m_ln_w, m_ln_b, v_w_in, v_shift_mu, v_w_decay0, v_w_decay_up, v_a0, v_a_up, v_k_k, v_k_a, v_r_k, v_ln_x_w, v_ln_x_b, v_v_mix0, v_v_mix_down, v_v_mix_up, v_lb_logits, v_g_norm_w, v_w_out, v_ln_w, v_ln_b):
    W = dict(w_in=w_in, shift_mu=shift_mu, w_decay0=w_decay0, w_decay_up=w_decay_up, a0=a0, a_up=a_up, k_k=k_k, k_a=k_a,
             r_k=r_k, ln_x_w=ln_x_w, ln_x_b=ln_x_b, v_mix0=v_mix0, v_mix_down=v_mix_down, v_mix_up=v_mix_up,
             lb_logits=lb_logits, g_norm_w=g_norm_w, w_out=w_out, ln_w=ln_w, ln_b=ln_b)
    M = dict(w_in=m_w_in, shift_mu=m_shift_mu, w_decay0=m_w_decay0, w_decay_up=m_w_decay_up, a0=m_a0, a_up=m_a_up,
             k_k=m_k_k, k_a=m_k_a, r_k=m_r_k, ln_x_w=m_ln_x_w, ln_x_b=m_ln_x_b, v_mix0=m_v_mix0,
             v_mix_down=m_v_mix_down, v_mix_up=m_v_mix_up, lb_logits=m_lb_logits, g_norm_w=m_g_norm_w, w_out=m_w_out,
             ln_w=m_ln_w, ln_b=m_ln_b)
    V = dict(w_in=v_w_in, shift_mu=v_shift_mu, w_decay0=v_w_decay0, w_decay_up=v_w_decay_up, a0=v_a0, a_up=v_a_up,
             k_k=v_k_k, k_a=v_k_a, r_k=v_r_k, ln_x_w=v_ln_x_w, ln_x_b=v_ln_x_b, v_mix0=v_v_mix0,
             v_mix_down=v_v_mix_down, v_mix_up=v_v_mix_up, lb_logits=v_lb_logits, g_norm_w=v_g_norm_w, w_out=v_w_out,
             ln_w=v_ln_w, ln_b=v_ln_b)
    mx, my, mc = _mesh_pos()
    kc = (2 * mx + my).astype(jnp.int32)
    c_arr = jnp.reshape(mc.astype(jnp.int32), (1,))
    kc_arr = jnp.reshape(kc, (1,))

    small_shard = _pack([W[n] for n, _ in CHIP_SMALL])
    tr_in = lambda a: jnp.swapaxes(a, 1, 2)
    in_halves = tr_in(w_in).astype(bf16).reshape(DEPTH, 2, SHARD_COLS // 2, D_MODEL)
    out_halves = jnp.swapaxes(w_out.astype(bf16).reshape(DEPTH, 2, D_MODEL // 8, D_MODEL), 0, 1)
    g_in0, g_small = _gather_shards([in_halves[0], small_shard.reshape(2, -1, LANES)])
    in_full = lambda g: g.reshape(IN_COLS, D_MODEL)
    shard_shapes = [W[n].shape for n, _ in CHIP_SMALL]
    g_small = g_small.reshape(4, -1, LANES)
    per_chip = [_unpack(g_small[k], shard_shapes) for k in range(4)]
    full_small = {n: jnp.concatenate([per_chip[k][i] for k in range(4)], axis=ax)
                  for i, (n, ax) in enumerate(CHIP_SMALL)}

    def assemble_later(gathered):
        g_in1, g_out = gathered
        return in_full(g_in1), jnp.transpose(g_out, (2, 0, 1, 3, 4)).reshape(DEPTH, D_MODEL, D_MODEL)

    in_slabs = lambda g: g.reshape(4, SHARD_COLS, D_MODEL)
    out_slabs = lambda g: g.reshape(4, D_MODEL // 4, D_MODEL)
    slabs, pairs, arrivals = {}, {}, {}

    def on_grads(l, dw_in, dw_out):
        slabs[l] = [in_slabs(dw_in), out_slabs(dw_out)]
        return slabs[l]

    def on_swapped(l, got):
        pairs[l], to_send = _rs_pair(slabs[l], got, c_arr, f"l{l}")
        return to_send

    def on_recv(l, arrived):
        arrivals[l] = arrived

    small_names = list(REPLICATED) + [n for n, _ in CHIP_SMALL]

    rep_shapes = [W[n].shape for n in REPLICATED]
    rep_rows = -(-sum(math.prod(s) for s in rep_shapes) // (16 * LANES)) * 16
    chip_rows = small_shard.shape[0]

    def on_small_grads(g):
        blocks = [_pack([g[n] for n in REPLICATED])]
        for k in range(4):
            blocks.append(_pack([lax.slice_in_dim(g[n], k * W[n].shape[ax], (k + 1) * W[n].shape[ax], axis=ax)
                                 for n, ax in CHIP_SMALL]))
        return _allreduce_part(jnp.concatenate(blocks, axis=0), rep_rows, chip_rows)

    loss_sum, gx, G, (small_sum,) = _device_step(
        x[0], loss_target[0], in_full(g_in0), [in_halves[1], out_halves], assemble_later, on_grads, on_swapped, on_recv,
        on_small_grads, shift_mu, w_decay0, full_small["w_decay_up"], a0, full_small["a_up"], k_k, k_a, r_k, ln_x_w,
        ln_x_b, v_mix0, full_small["v_mix_down"], full_small["v_mix_up"], lb_logits, g_norm_w, ln_w, ln_b)
    loss = lax.psum(loss_sum[0, 0], ("x", "y", "c"))

    own1, other1 = _rs_finish(pairs[1], arrivals[1], c_arr, kc_arr, "l1")
    own0, other0 = _rs_finish(pairs[0], arrivals[0], c_arr, kc_arr, "l0")
    gin_own, gout_own = [own0[0], own1[0]], [own0[1], own1[1]]
    gin_other, gout_other = [other0[0], other1[0]], [other0[1], other1[1]]

    grads = dict(zip(REPLICATED, _unpack(small_sum[:rep_rows], rep_shapes)))
    grads.update(zip([n for n, _ in CHIP_SMALL], _unpack(small_sum[rep_rows:], shard_shapes)))

    delta, new_m, new_v = {}, {}, {}
    res = _adamw_shard(tr_in(w_in), gin_own, gin_other, tr_in(m_w_in), tr_in(v_w_in), c_arr, "adamw_w_in")
    grads["w_in"], delta["w_in"], new_m["w_in"], new_v["w_in"] = [tr_in(a) for a in res]
    grads["w_out"], delta["w_out"], new_m["w_out"], new_v["w_out"] = _adamw_shard(
        w_out, gout_own, gout_other, m_w_out, v_w_out, c_arr, "adamw_w_out")
    shapes = [W[n].shape for n in small_names]
    wmv = jnp.stack([_pack([src[n] for n in small_names]) for src in (W, M, V)])
    dmv = _adamw_packed(wmv, _pack([grads[n] for n in small_names]), "adamw_small")
    for n, d1, m1, v1 in zip(small_names, _unpack(dmv[0], shapes), _unpack(dmv[1], shapes), _unpack(dmv[2], shapes)):
        delta[n], new_m[n], new_v[n] = d1, m1, v1

    return (loss, gx[None], *[grads[n] for n in WEIGHTS], *[delta[n] for n in WEIGHTS],
            *[new_m[n] for n in WEIGHTS], *[new_v[n] for n in WEIGHTS])
```

```python
import functools
import math

import jax
import jax.numpy as jnp
from jax import lax
from jax.experimental import pallas as pl
from jax.experimental.pallas import tpu as pltpu

f32 = jnp.float32
bf16 = jnp.bfloat16
SDS = jax.ShapeDtypeStruct
MESH = pl.DeviceIdType.MESH

D_MODEL = 2048
DEPTH = 2
D_RWKV = 1024
D_HGRN = 1024
RWKV_HEAD = 64
RWKV_HEADS = 16
LORA = 64
VRES = 32
HG_HEAD = 128
HG_HEADS = 8
CHUNK = 64
RWKV_COLS = 4 * D_RWKV + 2 * LORA
IN_COLS = RWKV_COLS + 4 * D_HGRN
SHARD_COLS = IN_COLS // 4
LANES = 128
RW_BLKS = RWKV_COLS // LANES
ALPHA = (2 * DEPTH) ** 0.25
LN_EPS = 1e-5
GN_EPS = 64e-5
RMS_EPS = 1e-5
LB_FLOOR = 1e-30
EXP_M05 = math.exp(-0.5)
ADAM_LR, ADAM_B1, ADAM_B2, ADAM_EPS, ADAM_WD, ADAM_STEP = 0.001, 0.9, 0.999, 1e-08, 0.01, 10

TB = 128
TB_WIDE = 256
SCAN_TB = 8
SCAN_PARTS = 2
HSUB = 32
HIGHEST = lax.Precision.HIGHEST


def _params(sem=None, vmem_mb=None):
    kw = {}
    if sem is not None:
        kw["dimension_semantics"] = sem
    if vmem_mb is not None:
        kw["vmem_limit_bytes"] = vmem_mb << 20
    return pltpu.CompilerParams(**kw)


def _split(x, n):
    parts, r = [], x
    for i in range(n):
        p = r.astype(bf16)
        parts.append(p)
        if i + 1 < n:
            r = r - p.astype(f32)
    return parts


def _seg_raw(x, ee, n=3, per_tile=False):
    R = x.shape[0]
    if per_tile:
        outs = []
        for c in range(4):
            tile_parts = _split(x[:, 256 * c:256 * c + 256], n)
            res = jnp.dot(jnp.concatenate(tile_parts, axis=0), ee, preferred_element_type=f32)
            acc = res[0:R]
            for i in range(1, n):
                acc = acc + res[i * R:(i + 1) * R]
            outs.append(acc)
        return jnp.concatenate(outs, axis=1)
    parts = _split(x, n)
    lhs = jnp.concatenate([p[:, 256 * c:256 * c + 256] for p in parts for c in range(4)], axis=0)
    res = jnp.dot(lhs, ee, preferred_element_type=f32)

    def tile(c):
        acc = res[c * R:(c + 1) * R]
        for i in range(1, n):
            acc = acc + res[(4 * i + c) * R:(4 * i + c + 1) * R]
        return acc

    return jnp.concatenate([tile(c) for c in range(4)], axis=1)


@jax.custom_vjp
def _seg(x, ee):
    return _seg_raw(x, ee)


def _seg_fwd(x, ee):
    return _seg_raw(x, ee), ee


def _seg_bwd(ee, g):
    return _seg_raw(g, ee), jnp.zeros_like(ee)


_seg.defvjp(_seg_fwd, _seg_bwd)


def _const_mats():
    i256 = jnp.arange(256) // RWKV_HEAD
    ee = (i256[:, None] == i256[None, :]).astype(bf16)
    head = jnp.arange(D_RWKV) // RWKV_HEAD
    lane = jnp.arange(LANES)
    red = (head[:, None] == lane[None, :]).astype(bf16)
    exp = ((lane[:, None] % RWKV_HEADS == head[None, :]) & (lane[:, None] < 2 * RWKV_HEADS)).astype(bf16)
    return ee, red, exp


def _matmul(a, b, *, mode, tm, tn, tk, name, add=None, b_layer=None, rider=None):
    bs = b.shape if b_layer is None else b.shape[1:]
    lead = () if b_layer is None else (None,)
    bidx = (lambda *t: t) if b_layer is None else (lambda *t: (b_layer,) + t)
    if mode == "nn":
        (M, K), N = a.shape, bs[1]
        tm, tn, tk = min(tm, M), min(tn, N), min(tk, K)
        a_spec = pl.BlockSpec((tm, tk), lambda i, j, k: (i, k))
        b_spec = pl.BlockSpec(lead + (tk, tn), lambda i, j, k: bidx(k, j))
        dims = (((1,), (0,)), ((), ()))
    elif mode == "nt":
        (M, K), N = a.shape, bs[0]
        tm, tn, tk = min(tm, M), min(tn, N), min(tk, K)
        a_spec = pl.BlockSpec((tm, tk), lambda i, j, k: (i, k))
        b_spec = pl.BlockSpec(lead + (tn, tk), lambda i, j, k: bidx(j, k))
        dims = (((1,), (1,)), ((), ()))
    else:
        (K, M), N = a.shape, bs[1]
        tm, tn, tk = min(tm, M), min(tn, N), min(tk, K)
        a_spec = pl.BlockSpec((tk, tm), lambda i, j, k: (k, i))
        b_spec = pl.BlockSpec(lead + (tk, tn), lambda i, j, k: bidx(k, j))
        dims = (((0,), (0,)), ((), ()))
    assert M % tm == 0 and N % tn == 0 and K % tk == 0, (M, N, K, tm, tn, tk)
    has_add = add is not None
    n_in = 3 if has_add else 2
    grid = (M // tm, N // tn, K // tk)
    rd = rider if rider is not None else _Part([], [], [], [], [], lambda *a: None)
    rd.total = grid[0] * grid[1] * grid[2]

    def body(*refs):
        a_ref, b_ref = refs[:2]
        c_ref = refs[2] if has_add else None
        rd_ins, o_ref = refs[n_in:n_in + rd.n_in], refs[n_in + rd.n_in]
        rest = refs[n_in + rd.n_in + 1:]
        i, j, k = pl.program_id(0), pl.program_id(1), pl.program_id(2)
        rd.step((i * grid[1] + j) * grid[2] + k, rd_ins, rest[:rd.n_out], rest[rd.n_out:])
        p = lax.dot_general(a_ref[...].astype(bf16), b_ref[...].astype(bf16), dims, preferred_element_type=f32)

        @pl.when(k == 0)
        def _():
            o_ref[...] = p + c_ref[...] if has_add else p

        @pl.when(k > 0)
        def _():
            o_ref[...] += p

    o_spec = pl.BlockSpec((tm, tn), lambda i, j, k: (i, j))
    in_specs = [a_spec, b_spec] + ([o_spec] if has_add else []) + rd.in_specs
    args = (a, b) + ((add,) if has_add else ()) + tuple(rd.args)
    outs = pl.pallas_call(
        body, name=name, grid=grid, in_specs=in_specs, out_specs=[o_spec] + rd.out_specs,
        out_shape=[SDS((M, N), f32)] + rd.out_shape, scratch_shapes=rd.scratch,
        compiler_params=_params(("parallel", "parallel", "arbitrary") if rider is None else ("arbitrary",) * 3, 60),
    )(*args)
    return outs[0] if rider is None else (outs[0], list(outs[1:]))


def _rwkv_core(k_in, v_in, wdad, w0, wup, a0, aup, kkw, kaw, ee, vmix):
    w_raw = w0 + jnp.dot(jnp.tanh(wdad), wup, preferred_element_type=f32)
    decay = jnp.exp(-EXP_M05 * jax.nn.sigmoid(w_raw))
    a = jax.nn.sigmoid(a0 + jnp.dot(wdad, aup, preferred_element_type=f32))
    if vmix is None:
        v = v_in
    else:
        v0, vdn, vup, vfirst = vmix
        gate = jax.nn.sigmoid(v0 + jnp.dot(jnp.dot(v_in, vdn, preferred_element_type=f32), vup,
                                           preferred_element_type=f32))
        v = v_in + (vfirst - v_in) * gate
    kk = k_in * kkw
    n2 = _seg(kk * kk, ee)
    kk = kk / jnp.maximum(jnp.sqrt(n2), 1e-12)
    k2 = k_in * (1.0 + (a - 1.0) * kaw)
    return decay, k2, v, -kk, kk * a


def _shifted(y, prev_row):
    yp = pltpu.roll(y, 1, 0)
    row = lax.broadcasted_iota(jnp.int32, y.shape, 0)
    return jnp.where(row == 0, prev_row, yp)


def _rwkv_pre_fwd(proj, mu, w0, wup, a0, aup, kkw, kaw, ee, vmix, layer):
    T = proj.shape[0]
    nb = T // TB
    has_mix = vmix is not None

    def body(*refs):
        y_ref, p8_ref, mu_ref, w0_ref, wup_ref, a0_ref, aup_ref, kkw_ref, kaw_ref, ee_ref = refs[:10]
        rest = refs[10:]
        if has_mix:
            v0_ref, vdn_ref, vup_ref, vf_ref = rest[:4]
            rest = rest[4:]
        r_o, w_o, k_o, v_o, a_o, b_o, z_o = rest
        i = pl.program_id(0)
        y = y_ref[...]
        prev = jnp.where(i == 0, 0.0, p8_ref[7:8, :])
        rw = y + mu_ref[...] * (_shifted(y, prev) - y)
        mix = (v0_ref[...], vdn_ref[...], vup_ref[...], vf_ref[...]) if has_mix else None
        dec, k2, v, av, bv = _rwkv_core(rw[:, 1024:2048], rw[:, 2048:3072], rw[:, 4096:4224], w0_ref[...],
                                        wup_ref[...], a0_ref[...], aup_ref[...], kkw_ref[...], kaw_ref[...],
                                        ee_ref[...], mix)
        r_o[...] = rw[:, 0:1024]
        w_o[...] = dec
        k_o[...] = k2
        v_o[...] = v
        a_o[...] = av
        b_o[...] = bv
        z_o[...] = rw[:, 3072:4096]

    row = lambda n: pl.BlockSpec((1, n), lambda i: (0, 0))
    full = lambda s: pl.BlockSpec(s, lambda i: (0,) * len(s))
    blk = pl.BlockSpec((TB, D_RWKV), lambda i: (i, 0))
    in_specs = [pl.BlockSpec((TB, RWKV_COLS), lambda i: (i, 0)),
                pl.BlockSpec((8, RWKV_COLS), lambda i: (jnp.maximum(i * (TB // 8) - 1, 0), 0)),
                row(RWKV_COLS), row(D_RWKV), full((LANES, D_RWKV)), row(D_RWKV), full((LANES, D_RWKV)),
                row(D_RWKV), row(D_RWKV), full((256, 256))]
    args = [proj, proj, mu, w0, wup, a0, aup, kkw, kaw, ee]
    if has_mix:
        v0, vdn, vup, vfirst = vmix
        in_specs += [row(D_RWKV), full((D_RWKV, LANES)), full((LANES, D_RWKV)), blk]
        args += [v0, vdn, vup, vfirst]
    return pl.pallas_call(
        body, name=f"rwkv_pre_fwd_l{layer}", grid=(nb,), in_specs=in_specs, out_specs=[blk] * 7,
        out_shape=[SDS((T, D_RWKV), f32)] * 7, compiler_params=_params(("parallel",), 48),
    )(*args)


def _rwkv_pre_bwd(proj, mu, w0, wup, a0, aup, kkw, kaw, ee, vmix, cots, layer):
    T = proj.shape[0]
    nb = T // TB
    has_mix = vmix is not None
    cot_list = cots["r"] + [cots["w"]] + cots["k"] + cots["v"] + [cots["a"], cots["b"], cots["z"]]
    n_r, n_k, n_v = len(cots["r"]), len(cots["k"]), len(cots["v"])
    n_cot = len(cot_list)

    def body(*refs):
        y_ref, p8_ref, mu_ref, w0_ref, wup_ref, a0_ref, aup_ref, kkw_ref, kaw_ref, ee_ref = refs[:10]
        rest = refs[10:]
        if has_mix:
            v0_ref, vdn_ref, vup_ref, vf_ref = rest[:4]
            rest = rest[4:]
        cot_refs, rest = rest[:n_cot], rest[n_cot:]
        if has_mix:
            (dproj_o, dmu_o, dw0_o, dwup_o, da0_o, daup_o, dkkw_o, dkaw_o,
             dv0_o, dvdn_o, dvup_o, dvf_o, carry) = rest
        else:
            dproj_o, dmu_o, dw0_o, dwup_o, da0_o, daup_o, dkkw_o, dkaw_o, carry = rest
        i = pl.program_id(0)
        blk_i = nb - 1 - i
        y = y_ref[...]
        prev = jnp.where(blk_i == 0, 0.0, p8_ref[7:8, :])
        yp = _shifted(y, prev)
        mu_v = mu_ref[...]
        rw = y + mu_v * (yp - y)

        def sum_refs(rs):
            acc = rs[0][...]
            for r_ in rs[1:]:
                acc = acc + r_[...]
            return acc

        c = list(cot_refs)
        g_r = sum_refs(c[:n_r]); c = c[n_r:]
        g_w = c[0][...]; c = c[1:]
        g_k = sum_refs(c[:n_k]); c = c[n_k:]
        g_v = sum_refs(c[:n_v]); c = c[n_v:]
        g_a, g_b, g_z = c[0][...], c[1][...], c[2][...]

        ee_v = ee_ref[...]
        if has_mix:
            def fn(k_in, v_in, wdad, w0_, wup_, a0_, aup_, kkw_, kaw_, v0_, vdn_, vup_, vf_):
                return _rwkv_core(k_in, v_in, wdad, w0_, wup_, a0_, aup_, kkw_, kaw_, ee_v, (v0_, vdn_, vup_, vf_))
            prim = (rw[:, 1024:2048], rw[:, 2048:3072], rw[:, 4096:4224], w0_ref[...], wup_ref[...], a0_ref[...],
                    aup_ref[...], kkw_ref[...], kaw_ref[...], v0_ref[...], vdn_ref[...], vup_ref[...], vf_ref[...])
        else:
            def fn(k_in, v_in, wdad, w0_, wup_, a0_, aup_, kkw_, kaw_):
                return _rwkv_core(k_in, v_in, wdad, w0_, wup_, a0_, aup_, kkw_, kaw_, ee_v, None)
            prim = (rw[:, 1024:2048], rw[:, 2048:3072], rw[:, 4096:4224], w0_ref[...], wup_ref[...], a0_ref[...],
                    aup_ref[...], kkw_ref[...], kaw_ref[...])
        _, vjp = jax.vjp(fn, *prim)
        gs = vjp((g_w, g_k, g_v, g_a, g_b))
        d_k, d_v, d_wdad = gs[0], gs[1], gs[2]
        g_rw = jnp.concatenate([g_r, d_k, d_v, g_z, d_wdad], axis=1)

        @pl.when(i == 0)
        def _():
            carry[...] = jnp.zeros_like(carry)
            dmu_o[...] = jnp.zeros_like(dmu_o)
            dw0_o[...] = jnp.zeros_like(dw0_o)
            dwup_o[...] = jnp.zeros_like(dwup_o)
            da0_o[...] = jnp.zeros_like(da0_o)
            daup_o[...] = jnp.zeros_like(daup_o)
            dkkw_o[...] = jnp.zeros_like(dkkw_o)
            dkaw_o[...] = jnp.zeros_like(dkaw_o)
            if has_mix:
                dv0_o[...] = jnp.zeros_like(dv0_o)
                dvdn_o[...] = jnp.zeros_like(dvdn_o)
                dvup_o[...] = jnp.zeros_like(dvup_o)

        dmu_o[...] += jnp.sum(g_rw * (yp - y), axis=0, keepdims=True)
        dw0_o[...] += gs[3]
        dwup_o[...] += gs[4]
        da0_o[...] += gs[5]
        daup_o[...] += gs[6]
        dkkw_o[...] += gs[7]
        dkaw_o[...] += gs[8]
        if has_mix:
            dv0_o[...] += gs[9]
            dvdn_o[...] += gs[10]
            dvup_o[...] += gs[11]
            dvf_o[...] = gs[12]
        gm = g_rw * mu_v
        nxt = pltpu.roll(gm, TB - 1, 0)
        rowi = lax.broadcasted_iota(jnp.int32, gm.shape, 0)
        nxt = jnp.where(rowi == TB - 1, carry[...], nxt)
        dproj_o[...] = g_rw - gm + nxt
        carry[...] = gm[0:1, :]

    rev = lambda i: (nb - 1 - i, 0)
    row = lambda n: pl.BlockSpec((1, n), lambda i: (0, 0))
    full = lambda s: pl.BlockSpec(s, lambda i: (0,) * len(s))
    blk = pl.BlockSpec((TB, D_RWKV), rev)
    in_specs = [pl.BlockSpec((TB, RWKV_COLS), rev),
                pl.BlockSpec((8, RWKV_COLS), lambda i: (jnp.maximum((nb - 1 - i) * (TB // 8) - 1, 0), 0)),
                row(RWKV_COLS), row(D_RWKV), full((LANES, D_RWKV)), row(D_RWKV), full((LANES, D_RWKV)),
                row(D_RWKV), row(D_RWKV), full((256, 256))]
    args = [proj, proj, mu, w0, wup, a0, aup, kkw, kaw, ee]
    out_specs = [pl.BlockSpec((TB, RWKV_COLS), rev), row(RWKV_COLS), row(D_RWKV), full((LANES, D_RWKV)),
                 row(D_RWKV), full((LANES, D_RWKV)), row(D_RWKV), row(D_RWKV)]
    out_shape = [SDS((T, RWKV_COLS), f32), SDS((1, RWKV_COLS), f32), SDS((1, D_RWKV), f32), SDS((LANES, D_RWKV), f32),
                 SDS((1, D_RWKV), f32), SDS((LANES, D_RWKV), f32), SDS((1, D_RWKV), f32), SDS((1, D_RWKV), f32)]
    if has_mix:
        v0, vdn, vup, vfirst = vmix
        in_specs += [row(D_RWKV), full((D_RWKV, LANES)), full((LANES, D_RWKV)), blk]
        args += [v0, vdn, vup, vfirst]
        out_specs += [row(D_RWKV), full((D_RWKV, LANES)), full((LANES, D_RWKV)), blk]
        out_shape += [SDS((1, D_RWKV), f32), SDS((D_RWKV, LANES), f32), SDS((LANES, D_RWKV), f32),
                      SDS((T, D_RWKV), f32)]
    in_specs += [blk] * n_cot
    args += cot_list
    return pl.pallas_call(
        body, name=f"rwkv_pre_bwd_l{layer}", grid=(nb,), in_specs=in_specs, out_specs=out_specs, out_shape=out_shape,
        scratch_shapes=[pltpu.VMEM((1, RWKV_COLS), f32)],
        compiler_params=_params(("arbitrary",), 60),
    )(*args)


def _pack_cols(prod_ref, red):
    res = jnp.dot(prod_ref[...].astype(bf16), red, preferred_element_type=f32)
    out = res[0:RWKV_HEAD]
    for s in range(1, SCAN_TB):
        out = out + pltpu.roll(res[s * RWKV_HEAD:(s + 1) * RWKV_HEAD], RWKV_HEADS * s, 1)
    return out


def _unpack_cols(tile_ref, exp):
    rows = []
    for s in range(SCAN_TB):
        half = tile_ref[0, :, pl.ds((s // 4) * LANES, LANES)]
        rows.append(half if s % 4 == 0 else pltpu.roll(half, LANES - 2 * RWKV_HEADS * (s % 4), 1))
    return jnp.dot(jnp.concatenate(rows, axis=0), exp, preferred_element_type=f32)


def _to_col(v):
    T = v.shape[0]
    hi = lax.reduce_precision(v, 8, 7)
    parts = jnp.stack([hi, v - hi], axis=1).astype(bf16)
    parts = parts.reshape(T // SCAN_TB, SCAN_TB, 2, RWKV_HEADS, RWKV_HEAD)
    return jnp.transpose(parts, (0, 4, 1, 2, 3)).reshape(T // SCAN_TB, RWKV_HEAD, 2 * LANES)


def _from_col(vc):
    nb = vc.shape[0]
    tiles = vc.reshape(nb, RWKV_HEAD, SCAN_TB, RWKV_HEADS)
    return jnp.transpose(tiles, (0, 2, 3, 1)).reshape(nb * SCAN_TB, D_RWKV)


def _rwkv_scan_fwd(r, w, k, vcol, a, b, ee, red, exp, layer, rider, gather=()):
    T = r.shape[0]
    nb = T // SCAN_TB
    ng = len(gather)

    def body(*refs):
        r_ref, w_ref, k_ref, v_ref, a_ref, an_ref, b_ref, ee_ref, red_ref, exp_ref = refs[:10]
        rd_ins, refs = refs[10:10 + rider.n_in], refs[10 + rider.n_in:]
        g_ins, refs = refs[:ng], refs[ng:]
        o_ref, sp_ref = refs[:2]
        rd_outs, refs = refs[2:2 + rider.n_out], refs[2 + rider.n_out:]
        g_outs, refs = refs[:ng], refs[ng:]
        s_ref, sa_ref, vb_ref, po_ref = refs[:4]
        rd_scr, g_sems = refs[4:4 + rider.n_scr], refs[4 + rider.n_scr:]
        step = pl.program_id(0)
        rider.step(step, rd_ins, rd_outs, rd_scr)

        @pl.when(step == 0)
        def _():
            s_ref[...] = jnp.zeros_like(s_ref)
            sa_ref[...] = jnp.zeros_like(sa_ref)
            if ng:
                _gather_start(g_ins, g_outs, *g_sems)

        if ng:
            @pl.when(step == (3 * nb) // 4)
            def _():
                _gather_forward(g_ins, g_outs, *g_sems)

        ee_v = ee_ref[...]
        r_b, w_b, k_b, a_b, b_b = r_ref[...], w_ref[...], k_ref[...], a_ref[...], b_ref[...]
        rowi = lax.broadcasted_iota(jnp.int32, a_b.shape, 0)
        a_nx = jnp.where(rowi == SCAN_TB - 1, an_ref[0:1, :], pltpu.roll(a_b, SCAN_TB - 1, 0))
        wa = w_b * a_nx
        dots = _seg_raw(jnp.concatenate([b_b * a_nx, k_b * a_nx], axis=0), ee_v, 3)
        beta, kappa = dots[0:SCAN_TB], dots[SCAN_TB:2 * SCAN_TB]
        vb_ref[...] = _unpack_cols(v_ref, exp_ref[...])

        for t in range(SCAN_TB):
            row = lambda x: x[t:t + 1, :]
            blk = pl.ds(t * RWKV_HEAD, RWKV_HEAD)
            sp = s_ref[...]
            sa = sa_ref[...]
            sp_ref[t, 0] = sp
            sp_ref[t, 1] = sa
            vb = vb_ref[blk, :]
            ahead = _seg_raw(sp * row(wa), ee_v, SCAN_PARTS, per_tile=True)
            sn = sp * row(w_b) + sa * row(b_b) + vb * row(k_b)
            s_ref[...] = sn
            sa_ref[...] = ahead + sa * row(beta) + vb * row(kappa)
            po_ref[blk, :] = sn * row(r_b)
        o_ref[0] = _pack_cols(po_ref, red_ref[...])

        if ng:
            @pl.when(step == nb - 1)
            def _():
                _gather_finish(g_ins, g_outs, *g_sems)

    rows = pl.BlockSpec((SCAN_TB, D_RWKV), lambda i: (i, 0))
    rows_next = pl.BlockSpec((SCAN_TB, D_RWKV), lambda i: (jnp.minimum(i + 1, nb - 1), 0))
    col = pl.BlockSpec((1, RWKV_HEAD, LANES), lambda i: (i, 0, 0))
    col2 = pl.BlockSpec((1, RWKV_HEAD, 2 * LANES), lambda i: (i, 0, 0))
    full = lambda s: pl.BlockSpec(s, lambda i: (0,) * len(s))
    outs = pl.pallas_call(
        body, name=f"rwkv_scan_fwd_l{layer}", grid=(nb,),
        in_specs=[rows, rows, rows, col2, rows, rows_next, rows, full((256, 256)), full((D_RWKV, LANES)),
                  full((LANES, D_RWKV))] + rider.in_specs + [HBM_SPEC] * ng,
        out_specs=[col, pl.BlockSpec((SCAN_TB, 2, RWKV_HEAD, D_RWKV), lambda i: (i, 0, 0, 0))] + rider.out_specs
        + [HBM_SPEC] * ng,
        out_shape=[SDS((nb, RWKV_HEAD, LANES), f32), SDS((T, 2, RWKV_HEAD, D_RWKV), f32)] + rider.out_shape
        + _gather_out_shapes(gather),
        scratch_shapes=[pltpu.VMEM((RWKV_HEAD, D_RWKV), f32), pltpu.VMEM((RWKV_HEAD, D_RWKV), f32),
                        pltpu.VMEM((SCAN_TB * RWKV_HEAD, D_RWKV), f32),
                        pltpu.VMEM((SCAN_TB * RWKV_HEAD, D_RWKV), f32)] + rider.scratch
        + (_gather_sems(ng) if ng else []),
        compiler_params=_params(("arbitrary",), 52),
    )(r, w, k, vcol, a, a, b, ee, red, exp, *rider.args, *gather)
    return outs[0], outs[1], list(outs[2:2 + rider.n_out]), list(outs[2 + rider.n_out:])


def _rwkv_scan_bwd(r, w, k, vcol, a, b, sprev, docol, ee, red, exp, layer, send=()):
    T = r.shape[0]
    nb = T // SCAN_TB
    ns = len(send)

    def body(*refs):
        r_ref, w_ref, k_ref, v_ref, a_ref, b_ref, sp_ref, do_ref, ee_ref, red_ref, exp_ref = refs[:11]
        c_ins, refs = refs[11:11 + ns], refs[11 + ns:]
        dr_o, dw_o, dk_o, da_o, db_o, dv_o = refs[:6]
        c_outs, refs = refs[6:6 + ns], refs[6 + ns:]
        ds_ref, snext_ref, vb_ref, dob_ref, pdv_ref = refs[:5]
        c_sems = refs[5:]
        if ns:
            @pl.when(pl.program_id(0) == 0)
            def _():
                for cp in _chips_copies(c_ins, c_outs, *c_sems):
                    cp.start()

        ee_v = ee_ref[...]
        csum = lambda x: jnp.sum(x, axis=0, keepdims=True)
        nrow = SCAN_TB * RWKV_HEAD
        last = pl.ds(nrow - RWKV_HEAD, RWKV_HEAD)
        r_b, w_b, k_b, a_b, b_b = r_ref[...], w_ref[...], k_ref[...], a_ref[...], b_ref[...]
        vb_ref[...] = _unpack_cols(v_ref, exp_ref[...])
        dob_ref[...] = _unpack_cols(do_ref, exp_ref[...])

        @pl.when(pl.program_id(0) == 0)
        def _():
            ds_ref[...] = jnp.zeros_like(ds_ref)
            snext_ref[...] = (sp_ref[SCAN_TB - 1, 0] * w_b[SCAN_TB - 1:SCAN_TB, :]
                              + sp_ref[SCAN_TB - 1, 1] * b_b[SCAN_TB - 1:SCAN_TB, :]
                              + vb_ref[last, :] * k_b[SCAN_TB - 1:SCAN_TB, :])

        for s in range(SCAN_TB):
            t = SCAN_TB - 1 - s
            row = lambda x: x[t:t + 1, :]
            blk = pl.ds(t * RWKV_HEAD, RWKV_HEAD)
            sp, sa = sp_ref[t, 0], sp_ref[t, 1]
            sn = snext_ref[...] if s == 0 else sp_ref[t + 1, 0]
            vb, dob = vb_ref[blk, :], dob_ref[blk, :]
            ds = ds_ref[...] + dob * row(r_b)
            dsa = _seg_raw(ds * row(b_b), ee_v, SCAN_PARTS, per_tile=True)
            ds_ref[...] = ds * row(w_b) + dsa * row(a_b)
            dr_o[t:t + 1, :] = csum(sn * dob)
            pdv_ref[blk, :] = ds * row(k_b)
            dk_o[t:t + 1, :] = csum(ds * vb)
            db_o[t:t + 1, :] = csum(ds * sa)
            dw_o[t:t + 1, :] = csum(ds * sp)
            da_o[t:t + 1, :] = csum(sp * dsa)
        snext_ref[...] = sp_ref[0, 0]
        dv_o[0] = _pack_cols(pdv_ref, red_ref[...])

        if ns:
            @pl.when(pl.program_id(0) == nb - 1)
            def _():
                for cp in _chips_copies(c_ins, c_outs, *c_sems):
                    cp.wait()

    rows = pl.BlockSpec((SCAN_TB, D_RWKV), lambda i: (nb - 1 - i, 0))
    col = pl.BlockSpec((1, RWKV_HEAD, LANES), lambda i: (nb - 1 - i, 0, 0))
    col2 = pl.BlockSpec((1, RWKV_HEAD, 2 * LANES), lambda i: (nb - 1 - i, 0, 0))
    st = pl.BlockSpec((SCAN_TB, 2, RWKV_HEAD, D_RWKV), lambda i: (nb - 1 - i, 0, 0, 0))
    full = lambda s: pl.BlockSpec(s, lambda i: (0,) * len(s))
    big = pltpu.VMEM((SCAN_TB * RWKV_HEAD, D_RWKV), f32)
    outs = pl.pallas_call(
        body, name=f"rwkv_scan_bwd_l{layer}", grid=(nb,),
        in_specs=[rows, rows, rows, col2, rows, rows, st, col2, full((256, 256)), full((D_RWKV, LANES)),
                  full((LANES, D_RWKV))] + [HBM_SPEC] * ns,
        out_specs=[rows] * 5 + [col] + [HBM_SPEC] * ns,
        out_shape=[SDS((T, D_RWKV), f32)] * 5 + [SDS((nb, RWKV_HEAD, LANES), f32)] + _chips_out_shapes(send),
        scratch_shapes=[pltpu.VMEM((RWKV_HEAD, D_RWKV), f32)] * 2 + [big] * 3 + (_chips_sems(ns) if ns else []),
        compiler_params=_params(("arbitrary",), 56),
    )(r, w, k, vcol, a, b, sprev, docol, ee, red, exp, *send)
    return (*outs[:6], list(outs[6:]))


def _post_core(o, r, k2, v, z, gw, gb, rk, ee):
    inv = 1.0 / RWKV_HEAD
    mu = _seg(o, ee) * inv
    d = o - mu
    var = _seg(d * d, ee) * inv
    on = d * lax.rsqrt(var + GN_EPS) * gw + gb
    bonus = _seg(r * k2 * rk, ee) * v
    return (on + bonus) * jax.nn.silu(z)


def _rwkv_post_fwd(o, r, k2, v, z, gw, gb, rk, ee, cat, layer):
    T = o.shape[0]

    def body(o_ref, r_ref, k_ref, v_ref, z_ref, gw_ref, gb_ref, rk_ref, ee_ref, cat_in, out_ref):
        del cat_in
        out_ref[...] = _post_core(o_ref[...], r_ref[...], k_ref[...], v_ref[...], z_ref[...], gw_ref[...],
                                  gb_ref[...], rk_ref[...], ee_ref[...])

    blk = pl.BlockSpec((TB_WIDE, D_RWKV), lambda i: (i, 0))
    row = pl.BlockSpec((1, D_RWKV), lambda i: (0, 0))
    return pl.pallas_call(
        body, name=f"rwkv_post_fwd_l{layer}", grid=(T // TB_WIDE,),
        in_specs=[blk] * 5 + [row] * 3 + [pl.BlockSpec((256, 256), lambda i: (0, 0)), HBM_SPEC],
        out_specs=blk, out_shape=SDS((T, D_MODEL), f32), input_output_aliases={9: 0},
        compiler_params=_params(("parallel",), 40),
    )(o, r, k2, v, z, gw, gb, rk, ee, cat)


def _rwkv_post_bwd(o, r, k2, v, z, gw, gb, rk, ee, dcat, layer):
    T = o.shape[0]

    def body(o_ref, r_ref, k_ref, v_ref, z_ref, gw_ref, gb_ref, rk_ref, ee_ref, g_ref,
             do_o, dr_o, dk_o, dv_o, dz_o, dgw_o, dgb_o, drk_o):
        ee_v = ee_ref[...]
        fn = lambda o_, r_, k_, v_, z_, gw_, gb_, rk_: _post_core(o_, r_, k_, v_, z_, gw_, gb_, rk_, ee_v)
        _, vjp = jax.vjp(fn, o_ref[...], r_ref[...], k_ref[...], v_ref[...], z_ref[...], gw_ref[...], gb_ref[...],
                         rk_ref[...])
        gs = vjp(g_ref[...])
        do_o[...], dr_o[...], dk_o[...], dv_o[...], dz_o[...] = gs[:5]

        @pl.when(pl.program_id(0) == 0)
        def _():
            dgw_o[...] = jnp.zeros_like(dgw_o)
            dgb_o[...] = jnp.zeros_like(dgb_o)
            drk_o[...] = jnp.zeros_like(drk_o)

        dgw_o[...] += gs[5]
        dgb_o[...] += gs[6]
        drk_o[...] += gs[7]

    blk = pl.BlockSpec((TB_WIDE, D_RWKV), lambda i: (i, 0))
    row = pl.BlockSpec((1, D_RWKV), lambda i: (0, 0))
    return pl.pallas_call(
        body, name=f"rwkv_post_bwd_l{layer}", grid=(T // TB_WIDE,),
        in_specs=[blk] * 5 + [row] * 3 + [pl.BlockSpec((256, 256), lambda i: (0, 0)), blk],
        out_specs=[blk] * 5 + [row] * 3,
        out_shape=[SDS((T, D_RWKV), f32)] * 5 + [SDS((1, D_RWKV), f32)] * 3,
        compiler_params=_params(("arbitrary",), 48),
    )(o, r, k2, v, z, gw, gb, rk, ee, dcat)


def _hgrn_chunk(qr, fr, ii, z, S, lbl, gw, layer):
    L = CHUNK
    G = qr.shape[1] // HG_HEAD
    hd = lambda x, g: x[..., g * HG_HEAD:(g + 1) * HG_HEAD]
    heads = lambda f: jnp.concatenate([f(g) for g in range(G)], axis=-1)
    m = jnp.max(lbl, axis=0, keepdims=True)
    e = jnp.exp(lbl - m)
    sm = e / jnp.sum(e, axis=0, keepdims=True)
    if layer == 0:
        lb = sm[0:1] - sm[0:1]
    else:
        lb = (sm[0:1] + sm[1:2]) - sm[0:1]
    q = jax.nn.silu(qr)
    log_lb = jnp.log(jnp.maximum(lb, LB_FLOOR))
    lf = jnp.logaddexp(log_lb, jnp.log1p(-lb) + jax.nn.log_sigmoid(fr))
    k = (1.0 - lb) * jax.nn.sigmoid(-fr)
    ti = lax.broadcasted_iota(jnp.int32, (L, L), 0)
    si = lax.broadcasted_iota(jnp.int32, (L, L), 1)
    ltri = (si <= ti).astype(f32)
    b = jnp.dot(ltri, lf, precision=HIGHEST, preferred_element_type=f32)
    btot = jnp.dot(jnp.ones((L, L), f32), lf, precision=HIGHEST, preferred_element_type=f32)
    qd = q * jnp.exp(b)
    o = heads(lambda g: jnp.dot(hd(qd, g), S[g], preferred_element_type=f32))
    nsub = L // HSUB
    t3 = lax.broadcasted_iota(jnp.int32, (HSUB, HSUB, G * HG_HEAD), 0)
    s3 = lax.broadcasted_iota(jnp.int32, (HSUB, HSUB, G * HG_HEAD), 1)
    causal = s3 <= t3
    nt = (((1,), (1,)), ((), ()))
    tn = (((0,), (0,)), ((), ()))
    rows = []
    for I in range(nsub):
        sl = slice(I * HSUB, (I + 1) * HSUB)
        bI, qI, kI, iI = b[sl], q[sl], k[sl], ii[sl]
        d3 = bI[:, None, :] - bI[None, :, :]
        dec = jnp.where(causal, jnp.exp(jnp.where(causal, d3, 0.0)), 0.0)
        prod = qI[:, None, :] * dec * kI[None, :, :]
        oI = heads(lambda g: jnp.dot(jnp.sum(hd(prod, g), axis=-1), hd(iI, g), preferred_element_type=f32))
        for J in range(I):
            sj = slice(J * HSUB, (J + 1) * HSUB)
            bm = b[(J + 1) * HSUB - 1:(J + 1) * HSUB]
            qs = qI * jnp.exp(bI - bm)
            ks = k[sj] * jnp.exp(bm - b[sj])
            iJ = ii[sj]
            oI = oI + heads(lambda g: jnp.dot(
                lax.dot_general(hd(qs, g), hd(ks, g), nt, preferred_element_type=f32), hd(iJ, g),
                preferred_element_type=f32))
        rows.append(oI)
    o = o + jnp.concatenate(rows, axis=0)
    k_dec = k * jnp.exp(btot - b)
    ones = jnp.ones((L, HG_HEAD), f32)
    s_new = jnp.concatenate([
        (S[g] * jnp.exp(lax.dot_general(hd(lf, g), ones, tn, precision=HIGHEST, preferred_element_type=f32))
         + lax.dot_general(hd(k_dec, g), hd(ii, g), tn, preferred_element_type=f32))[None] for g in range(G)], axis=0)
    ms = heads(lambda g: jnp.broadcast_to(jnp.mean(hd(o * o, g), axis=-1, keepdims=True), (L, HG_HEAD)))
    out = o * lax.rsqrt(ms + RMS_EPS) * gw * jax.nn.silu(z)
    return out, s_new


class _Part:
    def __init__(self, args, in_specs, out_specs, out_shape, scratch, step):
        self.args, self.in_specs, self.out_specs, self.out_shape = args, in_specs, out_specs, out_shape
        self.scratch, self.step, self.total = scratch, step, None
        self.n_in, self.n_out, self.n_scr = len(args), len(out_shape), len(scratch)


def _run_part(part, steps, name):
    part.total = steps

    def body(*refs):
        ins, outs = refs[:part.n_in], refs[part.n_in:part.n_in + part.n_out]
        part.step(pl.program_id(0), ins, outs, refs[part.n_in + part.n_out:])

    return pl.pallas_call(body, name=name, grid=(steps,), in_specs=part.in_specs, out_specs=part.out_specs,
                          out_shape=part.out_shape, scratch_shapes=part.scratch,
                          compiler_params=_params(("arbitrary",), 48))(*part.args)


def _hg_index(T, rev, group=1):
    nc = T // CHUNK
    width = group * HG_HEAD
    head = lambda i: i // nc
    chunk = (lambda i: nc - 1 - i % nc) if rev else (lambda i: i % nc)
    first = lambda i: i % nc == 0
    pj = lambda off, g: pl.BlockSpec((CHUNK, HG_HEAD),
                                     lambda i: (chunk(i), RW_BLKS + off + group * head(i) + g))
    pspecs = [pj(kind * HG_HEADS, g) for kind in range(4) for g in range(group)]
    per_head = lambda rows: pl.BlockSpec((rows, width), lambda i: (0, head(i)))
    return nc, head, chunk, first, pspecs, per_head


def _hgrn_fwd_part(proj, lbl, gw, layer):
    T = proj.shape[0]
    nc, head, chunk, first, pspecs, per_head = _hg_index(T, False)

    def step(i, ins, outs, scr):
        q_ref, f_ref, i_ref, z_ref, lbl_ref, gw_ref = ins
        out_ref, sh_ref = outs
        s_ref, = scr

        @pl.when(first(i))
        def _():
            s_ref[...] = jnp.zeros_like(s_ref)

        s0 = s_ref[...]
        sh_ref[:, 0] = s0
        out, s_new = _hgrn_chunk(q_ref[...], f_ref[...], i_ref[...], z_ref[...], s0, lbl_ref[...], gw_ref[...], layer)
        out_ref[...] = out
        s_ref[...] = s_new

    return _Part(
        [proj, proj, proj, proj, lbl, gw], pspecs + [per_head(2), per_head(1)],
        [pl.BlockSpec((CHUNK, HG_HEAD), lambda i: (chunk(i), HG_HEADS + head(i))),
         pl.BlockSpec((1, 1, HG_HEAD, HG_HEAD), lambda i: (head(i), chunk(i), 0, 0))],
        [SDS((T, D_MODEL), f32), SDS((HG_HEADS, nc, HG_HEAD, HG_HEAD), f32)],
        [pltpu.VMEM((1, HG_HEAD, HG_HEAD), f32)], step)


HG_GROUP = 2


def _hgrn_bwd_part(proj, shist, dcat, lbl, gw, layer):
    T = proj.shape[0]
    G = HG_GROUP
    nc, head, chunk, first, pspecs, per_head = _hg_index(T, True, G)

    def step(i, ins, outs, scr):
        p_refs, (lbl_ref, gw_ref, sh_ref, g_ref) = ins[:4 * G], ins[4 * G:]
        dq_o, df_o, di_o, dz_o, dlbl_o, dgw_o = outs
        ds_ref, = scr

        @pl.when(first(i))
        def _():
            ds_ref[...] = jnp.zeros_like(ds_ref)
            dlbl_o[...] = jnp.zeros_like(dlbl_o)
            dgw_o[...] = jnp.zeros_like(dgw_o)

        fn = functools.partial(_hgrn_chunk, layer=layer)
        q, f, ii, z = (jnp.concatenate([p_refs[kind * G + g][...] for g in range(G)], axis=1) for kind in range(4))
        _, vjp = jax.vjp(fn, q, f, ii, z, sh_ref[:, 0], lbl_ref[...], gw_ref[...])
        dq, df, di, dz, ds, dl, dg = vjp((g_ref[...], ds_ref[...]))
        dq_o[...], df_o[...], di_o[...], dz_o[...] = dq, df, di, dz
        ds_ref[...] = ds
        dlbl_o[...] += dl
        dgw_o[...] += dg

    ospec = pl.BlockSpec((CHUNK, G * HG_HEAD), lambda i: (chunk(i), head(i)))
    return _Part(
        [proj] * (4 * G) + [lbl, gw, shist, dcat],
        pspecs + [per_head(2), per_head(1),
                  pl.BlockSpec((G, 1, HG_HEAD, HG_HEAD), lambda i: (head(i), chunk(i), 0, 0)),
                  pl.BlockSpec((CHUNK, G * HG_HEAD), lambda i: (chunk(i), HG_HEADS // G + head(i)))],
        [ospec] * 4 + [per_head(2), per_head(1)],
        [SDS((T, D_HGRN), f32)] * 4 + [SDS((2, D_HGRN), f32), SDS((1, D_HGRN), f32)],
        [pltpu.VMEM((G, HG_HEAD, HG_HEAD), f32)], step)


def _ln_core(h, y, w, b):
    u = ALPHA * h + y
    mu = jnp.mean(u, axis=-1, keepdims=True)
    d = u - mu
    var = jnp.mean(d * d, axis=-1, keepdims=True)
    return d * lax.rsqrt(var + LN_EPS) * w + b


def _ln_fwd(h, y, w, b, layer):
    T = h.shape[0]

    def body(h_ref, y_ref, w_ref, b_ref, o_ref):
        o_ref[...] = _ln_core(h_ref[...], y_ref[...], w_ref[...], b_ref[...])

    blk = pl.BlockSpec((TB_WIDE, D_MODEL), lambda i: (i, 0))
    row = pl.BlockSpec((1, D_MODEL), lambda i: (0, 0))
    return pl.pallas_call(body, name=f"ln_fwd_l{layer}", grid=(T // TB_WIDE,), in_specs=[blk, blk, row, row],
                          out_specs=blk, out_shape=SDS((T, D_MODEL), f32),
                          compiler_params=_params(("parallel",), 40))(h, y, w, b)


def _ln_bwd(h, y, w, b, g, layer):
    T = h.shape[0]

    def body(h_ref, y_ref, w_ref, b_ref, g_ref, dh_o, dy_o, dw_o, db_o):
        _, vjp = jax.vjp(_ln_core, h_ref[...], y_ref[...], w_ref[...], b_ref[...])
        dh, dy, dw, db = vjp(g_ref[...])
        dh_o[...] = dh
        dy_o[...] = dy.astype(bf16)

        @pl.when(pl.program_id(0) == 0)
        def _():
            dw_o[...] = jnp.zeros_like(dw_o)
            db_o[...] = jnp.zeros_like(db_o)

        dw_o[...] += dw
        db_o[...] += db

    blk = pl.BlockSpec((TB_WIDE, D_MODEL), lambda i: (i, 0))
    row = pl.BlockSpec((1, D_MODEL), lambda i: (0, 0))
    return pl.pallas_call(body, name=f"ln_bwd_l{layer}", grid=(T // TB_WIDE,), in_specs=[blk, blk, row, row, blk],
                          out_specs=[blk, blk, row, row],
                          out_shape=[SDS((T, D_MODEL), f32), SDS((T, D_MODEL), bf16)] + [SDS((1, D_MODEL), f32)] * 2,
                          compiler_params=_params(("arbitrary",), 48))(h, y, w, b, g)


def _loss_head(hout, target):
    T = hout.shape[0]

    def body(h_ref, t_ref, loss_o, g_o):
        err = h_ref[...] - t_ref[...]
        g_o[...] = err * (1.0 / D_MODEL)

        @pl.when(pl.program_id(0) == 0)
        def _():
            loss_o[...] = jnp.zeros_like(loss_o)

        part = jnp.sum(jnp.sum(err * err, axis=-1, keepdims=True) * (1.0 / D_MODEL), axis=0, keepdims=True)
        loss_o[...] += 0.5 * part

    blk = pl.BlockSpec((TB_WIDE, D_MODEL), lambda i: (i, 0))
    return pl.pallas_call(body, name="loss_head", grid=(T // TB_WIDE,), in_specs=[blk, blk],
                          out_specs=[pl.BlockSpec((1, 1), lambda i: (0, 0)), blk],
                          out_shape=[SDS((1, 1), f32), SDS((T, D_MODEL), f32)],
                          compiler_params=_params(("arbitrary",), 40))(hout, target)


def _pad_rows(m, lo, total):
    return jnp.pad(m, ((0, 0), (lo, total - lo - m.shape[1]), (0, 0)))


def _device_step(x, target, w_in0, later_shards, assemble_later, on_grads, on_swapped, on_recv, on_small_grads,
                 shift_mu, w_decay0, w_decay_up, a0, a_up, k_k, k_a, r_k, ln_x_w, ln_x_b,
                 v_mix0, v_mix_down, v_mix_up, lb_logits, g_norm_w, ln_w, ln_b):
    T = x.shape[0]
    ee, red, exp = _const_mats()
    wup_pad = _pad_rows(w_decay_up, 0, LANES)
    aup_pad = _pad_rows(a_up, LORA, LANES)
    vdn_pad = jnp.pad(v_mix_down, ((0, 0), (0, 0), (0, LANES - VRES)))
    vup_pad = _pad_rows(v_mix_up, 0, LANES)
    row = lambda p, l: p[l][None, :]

    def vmix_of(l, vfirst):
        if l == 0:
            return None
        return (row(v_mix0, l - 1), vdn_pad[l - 1], vup_pad[l - 1], vfirst)

    h = x
    saved = []
    vfirst = None
    w_in = [w_in0, None]
    w_out = None
    for l in range(DEPTH):
        hb = h.astype(bf16)
        proj = _matmul(hb, w_in[l], mode="nt", tm=1024, tn=1664, tk=D_MODEL, name=f"proj_fwd_l{l}")
        pre_args = (proj, row(shift_mu, l), row(w_decay0, l), wup_pad[l], row(a0, l), aup_pad[l], row(k_k, l),
                    row(k_a, l), ee, vmix_of(l, vfirst))
        r, w, k2, v, av, bv, z = _rwkv_pre_fwd(*pre_args, layer=l)
        if l == 0:
            vfirst = v
        vcol = _to_col(v)
        ocol, sprev, (cat, shist), gathered = _rwkv_scan_fwd(
            r, w, k2, vcol, av, bv, ee, red, exp, l, _hgrn_fwd_part(proj, lb_logits, row(g_norm_w, l), l),
            gather=later_shards if l == 0 else [])
        if l == 0:
            w_in[1], w_out = assemble_later(gathered)
        o = _from_col(ocol)
        post_args = (o, r, k2, v, z, row(ln_x_w, l), row(ln_x_b, l), row(r_k, l), ee)
        cat = _rwkv_post_fwd(*post_args, cat, layer=l)
        catb = cat.astype(bf16)
        y = _matmul(catb, w_out, b_layer=l, mode="nn", tm=1024, tn=1024, tk=D_MODEL, name=f"out_fwd_l{l}")
        h_new = _ln_fwd(h, y, row(ln_w, l), row(ln_b, l), l)
        saved.append(dict(h=h, hb=hb, proj=proj, pre_args=pre_args, post_args=post_args,
                          scan=(r, w, k2, vcol, av, bv, sprev), cat=catb, shist=shist, y=y))
        h = h_new

    loss_sum, g = _loss_head(h, target)

    grads = {n: [None] * DEPTH for n in ("w_in", "w_out", "shift_mu", "w_decay0", "w_decay_up", "a0", "a_up", "k_k",
                                         "k_a", "r_k", "ln_x_w", "ln_x_b", "g_norm_w", "ln_w", "ln_b")}
    dlbl = []
    dvfirst = None
    to_send = []
    for l in reversed(range(DEPTH)):
        s = saved[l]
        dh_res, dy, dlnw, dlnb = _ln_bwd(s["h"], s["y"], row(ln_w, l), row(ln_b, l), g, l)
        grads["ln_w"][l], grads["ln_b"][l] = dlnw[0], dlnb[0]
        dcat = _matmul(dy, w_out, b_layer=l, mode="nt", tm=1024, tn=1024, tk=D_MODEL, name=f"out_bwd_dx_l{l}")
        grads["w_out"][l] = _matmul(s["cat"], dy, mode="tn", tm=1024, tn=1024, tk=T, name=f"out_bwd_dw_l{l}")
        do, dr_p, dk_p, dv_p, dz, dgw, dgb, drk = _rwkv_post_bwd(*s["post_args"], dcat, layer=l)
        grads["ln_x_w"][l], grads["ln_x_b"][l], grads["r_k"][l] = dgw[0], dgb[0], drk[0]
        r, w, k2, vcol, av, bv, sprev = s["scan"]
        dr_s, dw_s, dk_s, da_s, db_s, dvcol, arrived = _rwkv_scan_bwd(
            r, w, k2, vcol, av, bv, sprev, _to_col(do), ee, red, exp, l, send=to_send if l == 0 else [])
        dq, df, di, dzh, dl, dgn = _run_part(
            _hgrn_bwd_part(s["proj"], s["shist"], dcat, lb_logits, row(g_norm_w, l), l),
            (T // CHUNK) * HG_HEADS // HG_GROUP, f"hgrn_bwd_l{l}")
        if l == 0:
            on_recv(1, arrived)
        v_cots = [dv_p, _from_col(dvcol)] + ([dvfirst] if (l == 0 and dvfirst is not None) else [])
        cots = dict(r=[dr_p, dr_s], w=dw_s, k=[dk_p, dk_s], v=v_cots, a=da_s, b=db_s, z=dz)
        outs = _rwkv_pre_bwd(*s["pre_args"], cots, layer=l)
        dproj, dmu, dw0, dwup, da0, daup, dkkw, dkaw = outs[:8]
        grads["shift_mu"][l], grads["w_decay0"][l], grads["a0"][l] = dmu[0], dw0[0], da0[0]
        grads["k_k"][l], grads["k_a"][l] = dkkw[0], dkaw[0]
        grads["w_decay_up"][l], grads["a_up"][l] = dwup[:LORA], daup[LORA:]
        if l > 0:
            dv0, dvdn, dvup, dvfirst = outs[8:]
            g_vmix = (dv0, dvdn[:, :VRES][None], dvup[:VRES][None])
        dlbl.append(dl)
        grads["g_norm_w"][l] = dgn[0]
        dproj = jnp.concatenate([dproj, dq, df, di, dzh], axis=1).astype(bf16)
        dw_args = dict(mode="tn", tm=1664, tn=1024, tk=T, name=f"proj_bwd_dw_l{l}")
        dx_args = dict(mode="nn", tm=1024, tn=1024, tk=1664, name=f"proj_bwd_dx_l{l}", add=dh_res)
        if l == 1:
            grads["w_in"][l] = _matmul(dproj, s["hb"], **dw_args)
            g, swapped = _matmul(dproj, w_in[l], rider=_swap_part(on_grads(l, grads["w_in"][l], grads["w_out"][l])),
                                 **dx_args)
            to_send = on_swapped(l, swapped)
        else:
            small = {n: [a[None] for a in v] for n, v in grads.items() if n not in ("w_in", "w_out")}
            small["lb_logits"] = [dlbl[0] + dlbl[1]]
            small["v_mix0"], small["v_mix_down"], small["v_mix_up"] = ([a] for a in g_vmix)
            grads["w_in"][l], small_out = _matmul(dproj, s["hb"], rider=on_small_grads(small), **dw_args)
            swapped = _run_part(_swap_part(on_grads(l, grads["w_in"][l], grads["w_out"][l])), 1, "rs_swap_halves_l0")
            g, arrived = _matmul(dproj, w_in[l], rider=_chips_part(on_swapped(l, swapped)), **dx_args)
            on_recv(0, arrived)
    small["w_in"], small["w_out"] = grads["w_in"], grads["w_out"]
    return loss_sum, g, small, small_out


CHIP_COMBOS = ((1, 0), (0, 1), (1, 1))
HBM_SPEC = pl.BlockSpec(memory_space=pl.ANY)


def _mesh_pos():
    return lax.axis_index("x"), lax.axis_index("y"), lax.axis_index("c")


def _flip(v, d):
    return 1 - v if d else v


def _half(ref, dim, cc, hs):
    idx = [slice(None)] * len(ref.shape)
    idx[dim] = pl.ds(cc * hs, hs)
    return ref.at[tuple(idx)]


def _gather_shards(arrs):
    n = len(arrs)

    def body(*refs):
        ins, outs = refs[:n], refs[n:2 * n]
        sems = refs[2 * n:]
        _gather_start(ins, outs, *sems)
        _gather_forward(ins, outs, *sems)
        _gather_finish(ins, outs, *sems)

    return pl.pallas_call(
        body, name="gather_shards", in_specs=[HBM_SPEC] * n, out_specs=[HBM_SPEC] * n,
        out_shape=_gather_out_shapes(arrs), scratch_shapes=_gather_sems(n),
    )(*arrs)


def _gather_out_shapes(arrs):
    return [SDS((4,) + a.shape, a.dtype) for a in arrs]


def _gather_sems(n):
    return [pltpu.SemaphoreType.DMA((n, 6)), pltpu.SemaphoreType.DMA((n, 6)), pltpu.SemaphoreType.DMA((n,))]


def _gather_copies(kind, ins, outs, send_sems, recv_sems, local_sems):
    x, y, c = _mesh_pos()
    kc = 2 * x + y
    cps = []
    for a in range(len(ins)):
        if kind == "local":
            cps.append(pltpu.make_async_copy(ins[a], outs[a].at[kc], local_sems.at[a]))
            continue
        for j, (dx, dy) in enumerate(CHIP_COMBOS):
            px, py = _flip(x, dx), _flip(y, dy)
            kj = 2 * px + py
            src, dst, s, to = {
                "sends": (ins[a].at[c], outs[a].at[kc, c], j, (px, py, c)),
                "landed": (outs[a].at[kj, c], outs[a].at[kj, c], j, (px, py, c)),
                "forwards": (outs[a].at[kj, c], outs[a].at[kj, c], 3 + j, (x, y, 1 - c)),
                "passed": (outs[a].at[kj, 1 - c], outs[a].at[kj, 1 - c], 3 + j, (x, y, 1 - c)),
            }[kind]
            cps.append(pltpu.make_async_remote_copy(
                src_ref=src, dst_ref=dst, send_sem=send_sems.at[a, s], recv_sem=recv_sems.at[a, s], device_id=to,
                device_id_type=MESH))
    return cps


def _gather_start(*refs):
    for cp in _gather_copies("local", *refs) + _gather_copies("sends", *refs):
        cp.start()


def _gather_forward(*refs):
    for arrival, fw in zip(_gather_copies("landed", *refs), _gather_copies("forwards", *refs)):
        arrival.wait_recv()
        fw.start()


def _gather_finish(*refs):
    for cp in _gather_copies("passed", *refs):
        cp.wait_recv()
    for cp in _gather_copies("sends", *refs) + _gather_copies("forwards", *refs):
        cp.wait_send()
    for cp in _gather_copies("local", *refs):
        cp.wait()


DEV_COMBOS = tuple((dx, dy, dc) for dx in (0, 1) for dy in (0, 1) for dc in (0, 1))[1:]


def _allreduce_part(buf, rep_rows, chip_rows):
    R0, Q = rep_rows, chip_rows
    whole = lambda rows: pl.BlockSpec((rows, LANES), lambda *_: (0, 0))

    def copies(x_ref, slots, send_sems, recv_sems):
        x, y, c = _mesh_pos()
        cps = []
        for k, (dx, dy, dc) in enumerate(DEV_COMBOS):
            px, py = _flip(x, dx), _flip(y, dy)
            to = (px, py, _flip(c, dc))
            cps.append(pltpu.make_async_remote_copy(
                src_ref=x_ref.at[pl.ds(0, R0)], dst_ref=slots.at[k + 1, pl.ds(0, R0)], send_sem=send_sems.at[0, k],
                recv_sem=recv_sems.at[0, k], device_id=to, device_id_type=MESH))
            if Q:
                cps.append(pltpu.make_async_remote_copy(
                    src_ref=x_ref.at[pl.ds(pl.multiple_of(R0 + (2 * px + py) * Q, 8), Q)],
                    dst_ref=slots.at[k + 1, pl.ds(R0, Q)], send_sem=send_sems.at[1, k], recv_sem=recv_sems.at[1, k],
                    device_id=to, device_id_type=MESH))
        return cps

    def step(i, ins, outs, scr):
        x_ref, = ins
        o_ref, = outs
        slots = scr[0]

        @pl.when(i == 0)
        def _():
            for cp in copies(x_ref, *scr):
                cp.start()
            x, y, _ = _mesh_pos()
            slots[0, pl.ds(0, R0)] = x_ref[pl.ds(0, R0), :]
            if Q:
                slots[0, pl.ds(R0, Q)] = x_ref[pl.ds(pl.multiple_of(R0 + (2 * x + y) * Q, 8), Q), :]

        @pl.when(i == part.total - 1)
        def _():
            x, y, c = _mesh_pos()
            me = 4 * x + 2 * y + c
            cps = copies(x_ref, *scr)
            for cp in cps:
                cp.wait_recv()
            acc = slots[jnp.bitwise_xor(me, 0)]
            for d in range(1, 8):
                acc = acc + slots[jnp.bitwise_xor(me, d)]
            o_ref[...] = acc
            for cp in cps:
                cp.wait_send()

    part = _Part([buf], [whole(R0 + 4 * Q)], [whole(R0 + Q)], [SDS((R0 + Q, LANES), f32)],
                 [pltpu.VMEM((8, R0 + Q, LANES), f32), pltpu.SemaphoreType.DMA((2, 7)),
                  pltpu.SemaphoreType.DMA((2, 7))], step)
    return part


def _chips_part(slabs):
    n = len(slabs)

    def step(i, ins, outs, scr):
        if not n:
            return

        @pl.when(i == 0)
        def _():
            for cp in _chips_copies(ins, outs, *scr):
                cp.start()

        @pl.when(i == part.total - 1)
        def _():
            for cp in _chips_copies(ins, outs, *scr):
                cp.wait()

    part = _Part(list(slabs), [HBM_SPEC] * n, [HBM_SPEC] * n, _chips_out_shapes(slabs),
                 _chips_sems(n) if n else [], step)
    return part


def _swap_part(gs):
    n = len(gs)

    def copies(ins, outs, send_sems, recv_sems):
        x, y, c = _mesh_pos()
        return [pltpu.make_async_remote_copy(
            src_ref=_half(ins[a], 1, 1 - c, ins[a].shape[1] // 2), dst_ref=outs[a], send_sem=send_sems.at[a],
            recv_sem=recv_sems.at[a], device_id=(x, y, 1 - c), device_id_type=MESH) for a in range(n)]

    def step(i, ins, outs, scr):
        if not n:
            return

        @pl.when(i == 0)
        def _():
            for cp in copies(ins, outs, *scr):
                cp.start()

        @pl.when(i == part.total - 1)
        def _():
            for cp in copies(ins, outs, *scr):
                cp.wait()

    part = _Part(list(gs), [HBM_SPEC] * n, [HBM_SPEC] * n,
                 [SDS((g.shape[0], g.shape[1] // 2, g.shape[2]), g.dtype) for g in gs],
                 [pltpu.SemaphoreType.DMA((n,)), pltpu.SemaphoreType.DMA((n,))] if n else [], step)
    return part


def _chips_copies(ins, outs, send_sems, recv_sems):
    x, y, c = _mesh_pos()
    cps = []
    for a in range(len(ins)):
        for j, (dx, dy) in enumerate(CHIP_COMBOS):
            px, py = _flip(x, dx), _flip(y, dy)
            cps.append(pltpu.make_async_remote_copy(
                src_ref=ins[a].at[2 * px + py], dst_ref=outs[a].at[j], send_sem=send_sems.at[a, j],
                recv_sem=recv_sems.at[a, j], device_id=(px, py, c), device_id_type=MESH))
    return cps


def _chips_out_shapes(ps):
    return [SDS((3,) + p.shape[1:], p.dtype) for p in ps]


def _chips_sems(n):
    return [pltpu.SemaphoreType.DMA((n, 3)), pltpu.SemaphoreType.DMA((n, 3))]


def _share_halves(qs, tag):
    n = len(qs)

    def body(*refs):
        ins, outs = refs[:n], refs[n:2 * n]
        send_sems, recv_sems = refs[2 * n:]
        x, y, c = _mesh_pos()
        cps = []
        for a in range(n):
            cp = pltpu.make_async_remote_copy(
                src_ref=ins[a], dst_ref=outs[a], send_sem=send_sems.at[a], recv_sem=recv_sems.at[a],
                device_id=(x, y, 1 - c), device_id_type=MESH)
            cp.start()
            cps.append(cp)
        for cp in cps:
            cp.wait()

    return pl.pallas_call(
        body, name=f"rs_share_halves_{tag}", in_specs=[HBM_SPEC] * n, out_specs=[HBM_SPEC] * n,
        out_shape=[SDS(q.shape, q.dtype) for q in qs],
        scratch_shapes=[pltpu.SemaphoreType.DMA((n,)), pltpu.SemaphoreType.DMA((n,))],
    )(*qs)


def _row_tile(rows, cap):
    return max(t for t in range(16, cap + 1, 16) if rows % t == 0)


def _add_own_half(g, recv, c, name):
    S, R, C = g.shape
    h = R // 2
    tr = _row_tile(h, 256)
    nt = h // tr

    def body(c_ref, g_ref, r_ref, o_ref, o16_ref):
        del c_ref
        s = g_ref[...] + r_ref[...]
        o_ref[...] = s
        o16_ref[...] = s.astype(bf16)

    ospec = pl.BlockSpec((1, tr, C), lambda s, i, cr: (s, i, 0))
    return pl.pallas_call(
        body, name=name,
        grid_spec=pltpu.PrefetchScalarGridSpec(
            num_scalar_prefetch=1, grid=(S, nt),
            in_specs=[pl.BlockSpec((1, tr, C), lambda s, i, cr: (s, cr[0] * nt + i, 0)), ospec],
            out_specs=[ospec, ospec]),
        out_shape=[SDS((S, h, C), f32), SDS((S, h, C), bf16)],
        compiler_params=_params(("parallel", "parallel"), 40),
    )(c, g, recv)


def _add_chip_parts(p, recv, kc, name):
    _, R, C = p.shape
    tr = _row_tile(R, 256)

    def body(k_ref, p_ref, r0, r1, r2, o_ref):
        del k_ref
        o_ref[...] = ((p_ref[0] + r0[0].astype(f32)) + r1[0].astype(f32)) + r2[0].astype(f32)

    rspec = lambda j: pl.BlockSpec((1, tr, C), lambda i, kr: (j, i, 0))
    return pl.pallas_call(
        body, name=name,
        grid_spec=pltpu.PrefetchScalarGridSpec(
            num_scalar_prefetch=1, grid=(R // tr,),
            in_specs=[pl.BlockSpec((1, tr, C), lambda i, kr: (kr[0], i, 0)), rspec(0), rspec(1), rspec(2)],
            out_specs=pl.BlockSpec((tr, C), lambda i, kr: (i, 0))),
        out_shape=SDS((R, C), f32), compiler_params=_params(("parallel",), 40),
    )(kc, p, recv, recv, recv)


def _rs_pair(gs, got, c, tag):
    pairs = [_add_own_half(g, r, c, f"rs_add_pair_{tag}_{a}") for a, (g, r) in enumerate(zip(gs, got))]
    return [p[0] for p in pairs], [p[1] for p in pairs]


def _rs_finish(ps, arrived, c, kc, tag):
    qs = [_add_chip_parts(p, r, kc, f"rs_add_chips_{tag}_{a}") for a, (p, r) in enumerate(zip(ps, arrived))]
    return qs, _share_halves(qs, tag)


def _adamw_math(w, g, m, v):
    m2 = ADAM_B1 * m + (1.0 - ADAM_B1) * g
    v2 = ADAM_B2 * v + (1.0 - ADAM_B2) * (g * g)
    m_hat = m2 / (1.0 - ADAM_B1 ** ADAM_STEP)
    v_hat = v2 / (1.0 - ADAM_B2 ** ADAM_STEP)
    return -ADAM_LR * (m_hat / (jnp.sqrt(v_hat) + ADAM_EPS) + ADAM_WD * w), m2, v2


def _adamw_shard(w, g_own, g_other, m, v, c, name):
    L, R, C = w.shape
    h = R // 2
    tr = _row_tile(h, 208)
    nh = h // tr

    def body(c_ref, w_ref, go0, gx0, go1, gx1, m_ref, v_ref, g_o, d_o, m_o, v_o):
        mine = (pl.program_id(1) // nh) == c_ref[0]
        first = pl.program_id(0) == 0
        g = jnp.where(mine, jnp.where(first, go0[...], go1[...]), jnp.where(first, gx0[...], gx1[...]))[None]
        d, m2, v2 = _adamw_math(w_ref[...], g, m_ref[...], v_ref[...])
        g_o[...], d_o[...], m_o[...], v_o[...] = g, d, m2, v2

    full = pl.BlockSpec((1, tr, C), lambda l, i, cr: (l, i, 0))
    half0 = pl.BlockSpec((tr, C), lambda l, i, cr: (jnp.where(l == 0, i % nh, nh - 1), 0))
    half1 = pl.BlockSpec((tr, C), lambda l, i, cr: (jnp.where(l == 1, i % nh, 0), 0))
    return pl.pallas_call(
        body, name=name,
        grid_spec=pltpu.PrefetchScalarGridSpec(num_scalar_prefetch=1, grid=(L, R // tr),
                                               in_specs=[full, half0, half0, half1, half1, full, full],
                                               out_specs=[full] * 4),
        out_shape=[SDS((L, R, C), f32)] * 4, compiler_params=_params(("parallel", "parallel"), 48),
    )(c, w, g_own[0], g_other[0], g_own[1], g_other[1], m, v)


def _adamw_packed(wmv, g, name):
    _, R, C = wmv.shape

    def body(wmv_ref, g_ref, out_ref):
        out_ref[0], out_ref[1], out_ref[2] = _adamw_math(wmv_ref[0], g_ref[...], wmv_ref[1], wmv_ref[2])

    vm = pl.BlockSpec(memory_space=pltpu.VMEM)
    return pl.pallas_call(body, name=name, in_specs=[vm, vm], out_specs=vm, out_shape=SDS((3, R, C), f32),
                          compiler_params=_params(None, 40))(wmv, g)


def _pack(arrs):
    flat = [a.reshape(-1, LANES) for a in arrs]
    rows = sum(f.shape[0] for f in flat)
    pad = (-rows) % 16
    if pad:
        flat.append(jnp.zeros((pad, LANES), f32))
    return jnp.concatenate(flat, axis=0)


def _unpack(buf, shapes):
    out, r = [], 0
    for s in shapes:
        n = math.prod(s) // LANES
        out.append(buf[r:r + n].reshape(s))
        r += n
    return out


REPLICATED = ("shift_mu", "w_decay0", "a0", "k_k", "k_a", "r_k", "ln_x_w", "ln_x_b", "v_mix0", "lb_logits", "g_norm_w",
              "ln_w", "ln_b")
CHIP_SMALL = (("w_decay_up", 2), ("a_up", 2), ("v_mix_down", 1), ("v_mix_up", 2))
WEIGHTS = ("w_in", "shift_mu", "w_decay0", "w_decay_up", "a0", "a_up", "k_k", "k_a", "r_k", "ln_x_w", "ln_x_b",
           "v_mix0", "v_mix_down", "v_mix_up", "lb_logits", "g_norm_w", "w_out", "ln_w", "ln_b")


def kernel(x, w_in, shift_mu, w_decay0, w_decay_up, a0, a_up, k_k, k_a, r_k, ln_x_w, ln_x_b, v_mix0, v_mix_down, v_mix_up, lb_logits, g_norm_w, w_out, ln_w, ln_b, loss_target, m_w_in, m_shift_mu, m_w_decay0, m_w_decay_up, m_a0, m_a_up, m_k_k, m_k_a, m_r_k, m_ln_x_w, m_ln_x_b, m_v_mix0, m_v_mix_down, m_v_mix_up, m_lb_logits, m_g_norm_w, m_w_out, m_ln_w, m_ln_b, v_w_in, v_shift_mu, v_w_decay0, v_w_decay_up, v_a0, v_a_up, v_k_k, v_k_a, v_r_k, v_ln_x_w, v_ln_x_b, v_v_mix0, v_v_mix_down, v_v_mix_up, v_lb_logits, v_g_norm_w, v_w_out, v_ln_w, v_ln_b):
    W = dict(w_in=w_in, shift_mu=shift_mu, w_decay0=w_decay0, w_decay_up=w_decay_up, a0=a0, a_up=a_up, k_k=k_k, k_a=k_a,
             r_k=r_k, ln_x_w=ln_x_w, ln_x_b=ln_x_b, v_mix0=v_mix0, v_mix_down=v_mix_down, v_mix_up=v_mix_up,
             lb_logits=lb_logits, g_norm_w=g_norm_w, w_out=w_out, ln_w=ln_w, ln_b=ln_b)
    M = dict(w_in=m_w_in, shift_mu=m_shift_mu, w_decay0=m_w_decay0, w_decay_up=m_w_decay_up, a0=m_a0, a_up=m_a_up,
             k_k=m_k_k, k_a=m_k_a, r_k=m_r_k, ln_x_w=m_ln_x_w, ln_x_b=m_ln_x_b, v_mix0=m_v_mix0,
             v_mix_down=m_v_mix_down, v_mix_up=m_v_mix_up, lb_logits=m_lb_logits, g_norm_w=m_g_norm_w, w_out=m_w_out,
             ln_w=m_ln_w, ln_b=m_ln_b)
    V = dict(w_in=v_w_in, shift_mu=v_shift_mu, w_decay0=v_w_decay0, w_decay_up=v_w_decay_up, a0=v_a0, a_up=v_a_up,
             k_k=v_k_k, k_a=v_k_a, r_k=v_r_k, ln_x_w=v_ln_x_w, ln_x_b=v_ln_x_b, v_mix0=v_v_mix0,
             v_mix_down=v_v_mix_down, v_mix_up=v_v_mix_up, lb_logits=v_lb_logits, g_norm_w=v_g_norm_w, w_out=v_w_out,
             ln_w=v_ln_w, ln_b=v_ln_b)
    mx, my, mc = _mesh_pos()
    kc = (2 * mx + my).astype(jnp.int32)
    c_arr = jnp.reshape(mc.astype(jnp.int32), (1,))
    kc_arr = jnp.reshape(kc, (1,))

    small_shard = _pack([W[n] for n, _ in CHIP_SMALL])
    tr_in = lambda a: jnp.swapaxes(a, 1, 2)
    in_halves = tr_in(w_in).astype(bf16).reshape(DEPTH, 2, SHARD_COLS // 2, D_MODEL)
    out_halves = jnp.swapaxes(w_out.astype(bf16).reshape(DEPTH, 2, D_MODEL // 8, D_MODEL), 0, 1)
    g_in0, g_small = _gather_shards([in_halves[0], small_shard.reshape(2, -1, LANES)])
    in_full = lambda g: g.reshape(IN_COLS, D_MODEL)
    shard_shapes = [W[n].shape for n, _ in CHIP_SMALL]
    g_small = g_small.reshape(4, -1, LANES)
    per_chip = [_unpack(g_small[k], shard_shapes) for k in range(4)]
    full_small = {n: jnp.concatenate([per_chip[k][i] for k in range(4)], axis=ax)
                  for i, (n, ax) in enumerate(CHIP_SMALL)}

    def assemble_later(gathered):
        g_in1, g_out = gathered
        return in_full(g_in1), jnp.transpose(g_out, (2, 0, 1, 3, 4)).reshape(DEPTH, D_MODEL, D_MODEL)

    in_slabs = lambda g: g.reshape(4, SHARD_COLS, D_MODEL)
    out_slabs = lambda g: g.reshape(4, D_MODEL // 4, D_MODEL)
    slabs, pairs, arrivals = {}, {}, {}

    def on_grads(l, dw_in, dw_out):
        slabs[l] = [in_slabs(dw_in), out_slabs(dw_out)]
        return slabs[l]

    def on_swapped(l, got):
        pairs[l], to_send = _rs_pair(slabs[l], got, c_arr, f"l{l}")
        return to_send

    def on_recv(l, arrived):
        arrivals[l] = arrived

    small_names = list(REPLICATED) + [n for n, _ in CHIP_SMALL]

    rep_shapes = [W[n].shape for n in REPLICATED]
    rep_rows = -(-sum(math.prod(s) for s in rep_shapes) // (16 * LANES)) * 16
    chip_rows = small_shard.shape[0]

    def on_small_grads(g):
        blocks = [_pack([a for n in REPLICATED for a in g[n]])]
        for k in range(4):
            blocks.append(_pack([lax.slice_in_dim(a, k * W[n].shape[ax], (k + 1) * W[n].shape[ax], axis=ax)
                                 for n, ax in CHIP_SMALL for a in g[n]]))
        return _allreduce_part(jnp.concatenate(blocks, axis=0), rep_rows, chip_rows)

    loss_sum, gx, G, (small_sum,) = _device_step(
        x[0], loss_target[0], in_full(g_in0), [in_halves[1], out_halves], assemble_later, on_grads, on_swapped, on_recv,
        on_small_grads, shift_mu, w_decay0, full_small["w_decay_up"], a0, full_small["a_up"], k_k, k_a, r_k, ln_x_w,
        ln_x_b, v_mix0, full_small["v_mix_down"], full_small["v_mix_up"], lb_logits, g_norm_w, ln_w, ln_b)
    loss = lax.psum(loss_sum[0, 0], ("x", "y", "c"))

    own1, other1 = _rs_finish(pairs[1], arrivals[1], c_arr, kc_arr, "l1")
    own0, other0 = _rs_finish(pairs[0], arrivals[0], c_arr, kc_arr, "l0")
    gin_own, gout_own = [own0[0], own1[0]], [own0[1], own1[1]]
    gin_other, gout_other = [other0[0], other1[0]], [other0[1], other1[1]]

    grads = dict(zip(REPLICATED, _unpack(small_sum[:rep_rows], rep_shapes)))
    grads.update(zip([n for n, _ in CHIP_SMALL], _unpack(small_sum[rep_rows:], shard_shapes)))

    delta, new_m, new_v = {}, {}, {}
    res = _adamw_shard(tr_in(w_in), gin_own, gin_other, tr_in(m_w_in), tr_in(v_w_in), c_arr, "adamw_w_in")
    grads["w_in"], delta["w_in"], new_m["w_in"], new_v["w_in"] = [tr_in(a) for a in res]
    grads["w_out"], delta["w_out"], new_m["w_out"], new_v["w_out"] = _adamw_shard(
        w_out, gout_own, gout_other, m_w_out, v_w_out, c_arr, "adamw_w_out")
    shapes = [W[n].shape for n in small_names]
    wmv = jnp.stack([_pack([src[n] for n in small_names]) for src in (W, M, V)])
    dmv = _adamw_packed(wmv, _pack([grads[n] for n in small_names]), "adamw_small")
    for n, d1, m1, v1 in zip(small_names, _unpack(dmv[0], shapes), _unpack(dmv[1], shapes), _unpack(dmv[2], shapes)):
        delta[n], new_m[n], new_v[n] = d1, m1, v1

    return (loss, gx[None], *[grads[n] for n in WEIGHTS], *[delta[n] for n in WEIGHTS],
            *[new_m[n] for n in WEIGHTS], *[new_v[n] for n in WEIGHTS])
```

```python
import functools
import math

import jax
import jax.numpy as jnp
from jax import lax
from jax.experimental import pallas as pl
from jax.experimental.pallas import tpu as pltpu

f32 = jnp.float32
bf16 = jnp.bfloat16
SDS = jax.ShapeDtypeStruct
MESH = pl.DeviceIdType.MESH

D_MODEL = 2048
DEPTH = 2
D_RWKV = 1024
D_HGRN = 1024
RWKV_HEAD = 64
RWKV_HEADS = 16
LORA = 64
VRES = 32
HG_HEAD = 128
HG_HEADS = 8
CHUNK = 64
RWKV_COLS = 4 * D_RWKV + 2 * LORA
IN_COLS = RWKV_COLS + 4 * D_HGRN
SHARD_COLS = IN_COLS // 4
LANES = 128
RW_BLKS = RWKV_COLS // LANES
ALPHA = (2 * DEPTH) ** 0.25
LN_EPS = 1e-5
GN_EPS = 64e-5
RMS_EPS = 1e-5
LB_FLOOR = 1e-30
EXP_M05 = math.exp(-0.5)
ADAM_LR, ADAM_B1, ADAM_B2, ADAM_EPS, ADAM_WD, ADAM_STEP = 0.001, 0.9, 0.999, 1e-08, 0.01, 10

TB = 128
TB_WIDE = 256
SCAN_TB = 8
SCAN_SPLIT = 2
SCAN_PARTS = 2
HSUB = 32
HIGHEST = lax.Precision.HIGHEST


def _params(sem=None, vmem_mb=None):
    kw = {}
    if sem is not None:
        kw["dimension_semantics"] = sem
    if vmem_mb is not None:
        kw["vmem_limit_bytes"] = vmem_mb << 20
    return pltpu.CompilerParams(**kw)


def _split(x, n):
    parts, r = [], x
    for i in range(n):
        p = r.astype(bf16)
        parts.append(p)
        if i + 1 < n:
            r = r - p.astype(f32)
    return parts


def _seg_raw(x, ee, n=3, per_tile=False):
    R = x.shape[0]
    if per_tile:
        outs = []
        for c in range(x.shape[1] // 256):
            tile_parts = _split(x[:, 256 * c:256 * c + 256], n)
            res = jnp.dot(jnp.concatenate(tile_parts, axis=0), ee, preferred_element_type=f32)
            acc = res[0:R]
            for i in range(1, n):
                acc = acc + res[i * R:(i + 1) * R]
            outs.append(acc)
        return jnp.concatenate(outs, axis=1)
    parts = _split(x, n)
    lhs = jnp.concatenate([p[:, 256 * c:256 * c + 256] for p in parts for c in range(4)], axis=0)
    res = jnp.dot(lhs, ee, preferred_element_type=f32)

    def tile(c):
        acc = res[c * R:(c + 1) * R]
        for i in range(1, n):
            acc = acc + res[(4 * i + c) * R:(4 * i + c + 1) * R]
        return acc

    return jnp.concatenate([tile(c) for c in range(4)], axis=1)


@jax.custom_vjp
def _seg(x, ee):
    return _seg_raw(x, ee)


def _seg_fwd(x, ee):
    return _seg_raw(x, ee), ee


def _seg_bwd(ee, g):
    return _seg_raw(g, ee), jnp.zeros_like(ee)


_seg.defvjp(_seg_fwd, _seg_bwd)


def _const_mats():
    i256 = jnp.arange(256) // RWKV_HEAD
    ee = (i256[:, None] == i256[None, :]).astype(bf16)
    head = jnp.arange(D_RWKV) // RWKV_HEAD
    lane = jnp.arange(LANES)
    red = (head[:, None] == lane[None, :]).astype(bf16)
    exp = ((lane[:, None] % RWKV_HEADS == head[None, :]) & (lane[:, None] < 2 * RWKV_HEADS)).astype(bf16)
    return ee, red, exp


def _matmul(a, b, *, mode, tm, tn, tk, name, add=None, b_layer=None, rider=None):
    bs = b.shape if b_layer is None else b.shape[1:]
    lead = () if b_layer is None else (None,)
    bidx = (lambda *t: t) if b_layer is None else (lambda *t: (b_layer,) + t)
    if mode == "nn":
        (M, K), N = a.shape, bs[1]
        tm, tn, tk = min(tm, M), min(tn, N), min(tk, K)
        a_spec = pl.BlockSpec((tm, tk), lambda i, j, k: (i, k))
        b_spec = pl.BlockSpec(lead + (tk, tn), lambda i, j, k: bidx(k, j))
        dims = (((1,), (0,)), ((), ()))
    elif mode == "nt":
        (M, K), N = a.shape, bs[0]
        tm, tn, tk = min(tm, M), min(tn, N), min(tk, K)
        a_spec = pl.BlockSpec((tm, tk), lambda i, j, k: (i, k))
        b_spec = pl.BlockSpec(lead + (tn, tk), lambda i, j, k: bidx(j, k))
        dims = (((1,), (1,)), ((), ()))
    else:
        (K, M), N = a.shape, bs[1]
        tm, tn, tk = min(tm, M), min(tn, N), min(tk, K)
        a_spec = pl.BlockSpec((tk, tm), lambda i, j, k: (k, i))
        b_spec = pl.BlockSpec(lead + (tk, tn), lambda i, j, k: bidx(k, j))
        dims = (((0,), (0,)), ((), ()))
    assert M % tm == 0 and N % tn == 0 and K % tk == 0, (M, N, K, tm, tn, tk)
    has_add = add is not None
    n_in = 3 if has_add else 2
    grid = (M // tm, N // tn, K // tk)
    rd = rider if rider is not None else _Part([], [], [], [], [], lambda *a: None)
    rd.total = grid[0] * grid[1] * grid[2]

    def body(*refs):
        a_ref, b_ref = refs[:2]
        c_ref = refs[2] if has_add else None
        rd_ins, o_ref = refs[n_in:n_in + rd.n_in], refs[n_in + rd.n_in]
        rest = refs[n_in + rd.n_in + 1:]
        i, j, k = pl.program_id(0), pl.program_id(1), pl.program_id(2)
        rd.step((i * grid[1] + j) * grid[2] + k, rd_ins, rest[:rd.n_out], rest[rd.n_out:])
        p = lax.dot_general(a_ref[...].astype(bf16), b_ref[...].astype(bf16), dims, preferred_element_type=f32)

        @pl.when(k == 0)
        def _():
            o_ref[...] = p + c_ref[...] if has_add else p

        @pl.when(k > 0)
        def _():
            o_ref[...] += p

    o_spec = pl.BlockSpec((tm, tn), lambda i, j, k: (i, j))
    in_specs = [a_spec, b_spec] + ([o_spec] if has_add else []) + rd.in_specs
    args = (a, b) + ((add,) if has_add else ()) + tuple(rd.args)
    outs = pl.pallas_call(
        body, name=name, grid=grid, in_specs=in_specs, out_specs=[o_spec] + rd.out_specs,
        out_shape=[SDS((M, N), f32)] + rd.out_shape, scratch_shapes=rd.scratch,
        compiler_params=_params(("parallel", "parallel", "arbitrary") if rider is None else ("arbitrary",) * 3, 60),
    )(*args)
    return outs[0] if rider is None else (outs[0], list(outs[1:]))


def _rwkv_core(k_in, v_in, wdad, w0, wup, a0, aup, kkw, kaw, ee, vmix):
    w_raw = w0 + jnp.dot(jnp.tanh(wdad), wup, preferred_element_type=f32)
    decay = jnp.exp(-EXP_M05 * jax.nn.sigmoid(w_raw))
    a = jax.nn.sigmoid(a0 + jnp.dot(wdad, aup, preferred_element_type=f32))
    if vmix is None:
        v = v_in
    else:
        v0, vdn, vup, vfirst = vmix
        gate = jax.nn.sigmoid(v0 + jnp.dot(jnp.dot(v_in, vdn, preferred_element_type=f32), vup,
                                           preferred_element_type=f32))
        v = v_in + (vfirst - v_in) * gate
    kk = k_in * kkw
    n2 = _seg(kk * kk, ee)
    kk = kk / jnp.maximum(jnp.sqrt(n2), 1e-12)
    k2 = k_in * (1.0 + (a - 1.0) * kaw)
    return decay, k2, v, -kk, kk * a


def _shifted(y, prev_row):
    yp = pltpu.roll(y, 1, 0)
    row = lax.broadcasted_iota(jnp.int32, y.shape, 0)
    return jnp.where(row == 0, prev_row, yp)


def _rwkv_pre_fwd(proj, mu, w0, wup, a0, aup, kkw, kaw, ee, vmix, layer):
    T = proj.shape[0]
    nb = T // TB
    has_mix = vmix is not None

    def body(*refs):
        y_ref, p8_ref, mu_ref, w0_ref, wup_ref, a0_ref, aup_ref, kkw_ref, kaw_ref, ee_ref = refs[:10]
        rest = refs[10:]
        if has_mix:
            v0_ref, vdn_ref, vup_ref, vf_ref = rest[:4]
            rest = rest[4:]
        r_o, w_o, k_o, v_o, a_o, b_o, z_o = rest
        i = pl.program_id(0)
        y = y_ref[...]
        prev = jnp.where(i == 0, 0.0, p8_ref[7:8, :])
        rw = y + mu_ref[...] * (_shifted(y, prev) - y)
        mix = (v0_ref[...], vdn_ref[...], vup_ref[...], vf_ref[...]) if has_mix else None
        dec, k2, v, av, bv = _rwkv_core(rw[:, 1024:2048], rw[:, 2048:3072], rw[:, 4096:4224], w0_ref[...],
                                        wup_ref[...], a0_ref[...], aup_ref[...], kkw_ref[...], kaw_ref[...],
                                        ee_ref[...], mix)
        r_o[...] = rw[:, 0:1024]
        w_o[...] = dec
        k_o[...] = k2
        v_o[...] = v
        a_o[...] = av
        b_o[...] = bv
        z_o[...] = rw[:, 3072:4096]

    row = lambda n: pl.BlockSpec((1, n), lambda i: (0, 0))
    full = lambda s: pl.BlockSpec(s, lambda i: (0,) * len(s))
    blk = pl.BlockSpec((TB, D_RWKV), lambda i: (i, 0))
    in_specs = [pl.BlockSpec((TB, RWKV_COLS), lambda i: (i, 0)),
                pl.BlockSpec((8, RWKV_COLS), lambda i: (jnp.maximum(i * (TB // 8) - 1, 0), 0)),
                row(RWKV_COLS), row(D_RWKV), full((LANES, D_RWKV)), row(D_RWKV), full((LANES, D_RWKV)),
                row(D_RWKV), row(D_RWKV), full((256, 256))]
    args = [proj, proj, mu, w0, wup, a0, aup, kkw, kaw, ee]
    if has_mix:
        v0, vdn, vup, vfirst = vmix
        in_specs += [row(D_RWKV), full((D_RWKV, LANES)), full((LANES, D_RWKV)), blk]
        args += [v0, vdn, vup, vfirst]
    return pl.pallas_call(
        body, name=f"rwkv_pre_fwd_l{layer}", grid=(nb,), in_specs=in_specs, out_specs=[blk] * 7,
        out_shape=[SDS((T, D_RWKV), f32)] * 7, compiler_params=_params(("parallel",), 48),
    )(*args)


def _rwkv_pre_bwd(proj, mu, w0, wup, a0, aup, kkw, kaw, ee, vmix, cots, layer):
    T = proj.shape[0]
    nb = T // TB
    has_mix = vmix is not None
    cot_list = cots["r"] + [cots["w"]] + cots["k"] + cots["v"] + [cots["a"], cots["b"], cots["z"]]
    n_r, n_k, n_v = len(cots["r"]), len(cots["k"]), len(cots["v"])
    n_cot = len(cot_list)

    def body(*refs):
        y_ref, p8_ref, mu_ref, w0_ref, wup_ref, a0_ref, aup_ref, kkw_ref, kaw_ref, ee_ref = refs[:10]
        rest = refs[10:]
        if has_mix:
            v0_ref, vdn_ref, vup_ref, vf_ref = rest[:4]
            rest = rest[4:]
        cot_refs, rest = rest[:n_cot], rest[n_cot:]
        if has_mix:
            (dproj_o, dmu_o, dw0_o, dwup_o, da0_o, daup_o, dkkw_o, dkaw_o,
             dv0_o, dvdn_o, dvup_o, dvf_o, carry) = rest
        else:
            dproj_o, dmu_o, dw0_o, dwup_o, da0_o, daup_o, dkkw_o, dkaw_o, carry = rest
        i = pl.program_id(0)
        blk_i = nb - 1 - i
        y = y_ref[...]
        prev = jnp.where(blk_i == 0, 0.0, p8_ref[7:8, :])
        yp = _shifted(y, prev)
        mu_v = mu_ref[...]
        rw = y + mu_v * (yp - y)

        def sum_refs(rs):
            acc = rs[0][...]
            for r_ in rs[1:]:
                acc = acc + r_[...]
            return acc

        c = list(cot_refs)
        g_r = sum_refs(c[:n_r]); c = c[n_r:]
        g_w = c[0][...]; c = c[1:]
        g_k = sum_refs(c[:n_k]); c = c[n_k:]
        g_v = sum_refs(c[:n_v]); c = c[n_v:]
        g_a, g_b, g_z = c[0][...], c[1][...], c[2][...]

        ee_v = ee_ref[...]
        if has_mix:
            def fn(k_in, v_in, wdad, w0_, wup_, a0_, aup_, kkw_, kaw_, v0_, vdn_, vup_, vf_):
                return _rwkv_core(k_in, v_in, wdad, w0_, wup_, a0_, aup_, kkw_, kaw_, ee_v, (v0_, vdn_, vup_, vf_))
            prim = (rw[:, 1024:2048], rw[:, 2048:3072], rw[:, 4096:4224], w0_ref[...], wup_ref[...], a0_ref[...],
                    aup_ref[...], kkw_ref[...], kaw_ref[...], v0_ref[...], vdn_ref[...], vup_ref[...], vf_ref[...])
        else:
            def fn(k_in, v_in, wdad, w0_, wup_, a0_, aup_, kkw_, kaw_):
                return _rwkv_core(k_in, v_in, wdad, w0_, wup_, a0_, aup_, kkw_, kaw_, ee_v, None)
            prim = (rw[:, 1024:2048], rw[:, 2048:3072], rw[:, 4096:4224], w0_ref[...], wup_ref[...], a0_ref[...],
                    aup_ref[...], kkw_ref[...], kaw_ref[...])
        _, vjp = jax.vjp(fn, *prim)
        gs = vjp((g_w, g_k, g_v, g_a, g_b))
        d_k, d_v, d_wdad = gs[0], gs[1], gs[2]
        g_rw = jnp.concatenate([g_r, d_k, d_v, g_z, d_wdad], axis=1)

        @pl.when(i == 0)
        def _():
            carry[...] = jnp.zeros_like(carry)
            dmu_o[...] = jnp.zeros_like(dmu_o)
            dw0_o[...] = jnp.zeros_like(dw0_o)
            dwup_o[...] = jnp.zeros_like(dwup_o)
            da0_o[...] = jnp.zeros_like(da0_o)
            daup_o[...] = jnp.zeros_like(daup_o)
            dkkw_o[...] = jnp.zeros_like(dkkw_o)
            dkaw_o[...] = jnp.zeros_like(dkaw_o)
            if has_mix:
                dv0_o[...] = jnp.zeros_like(dv0_o)
                dvdn_o[...] = jnp.zeros_like(dvdn_o)
                dvup_o[...] = jnp.zeros_like(dvup_o)

        dmu_o[...] += jnp.sum(g_rw * (yp - y), axis=0, keepdims=True)
        dw0_o[...] += gs[3]
        dwup_o[...] += gs[4]
        da0_o[...] += gs[5]
        daup_o[...] += gs[6]
        dkkw_o[...] += gs[7]
        dkaw_o[...] += gs[8]
        if has_mix:
            dv0_o[...] += gs[9]
            dvdn_o[...] += gs[10]
            dvup_o[...] += gs[11]
            dvf_o[...] = gs[12]
        gm = g_rw * mu_v
        nxt = pltpu.roll(gm, TB - 1, 0)
        rowi = lax.broadcasted_iota(jnp.int32, gm.shape, 0)
        nxt = jnp.where(rowi == TB - 1, carry[...], nxt)
        dproj_o[...] = g_rw - gm + nxt
        carry[...] = gm[0:1, :]

    rev = lambda i: (nb - 1 - i, 0)
    row = lambda n: pl.BlockSpec((1, n), lambda i: (0, 0))
    full = lambda s: pl.BlockSpec(s, lambda i: (0,) * len(s))
    blk = pl.BlockSpec((TB, D_RWKV), rev)
    in_specs = [pl.BlockSpec((TB, RWKV_COLS), rev),
                pl.BlockSpec((8, RWKV_COLS), lambda i: (jnp.maximum((nb - 1 - i) * (TB // 8) - 1, 0), 0)),
                row(RWKV_COLS), row(D_RWKV), full((LANES, D_RWKV)), row(D_RWKV), full((LANES, D_RWKV)),
                row(D_RWKV), row(D_RWKV), full((256, 256))]
    args = [proj, proj, mu, w0, wup, a0, aup, kkw, kaw, ee]
    out_specs = [pl.BlockSpec((TB, RWKV_COLS), rev), row(RWKV_COLS), row(D_RWKV), full((LANES, D_RWKV)),
                 row(D_RWKV), full((LANES, D_RWKV)), row(D_RWKV), row(D_RWKV)]
    out_shape = [SDS((T, RWKV_COLS), f32), SDS((1, RWKV_COLS), f32), SDS((1, D_RWKV), f32), SDS((LANES, D_RWKV), f32),
                 SDS((1, D_RWKV), f32), SDS((LANES, D_RWKV), f32), SDS((1, D_RWKV), f32), SDS((1, D_RWKV), f32)]
    if has_mix:
        v0, vdn, vup, vfirst = vmix
        in_specs += [row(D_RWKV), full((D_RWKV, LANES)), full((LANES, D_RWKV)), blk]
        args += [v0, vdn, vup, vfirst]
        out_specs += [row(D_RWKV), full((D_RWKV, LANES)), full((LANES, D_RWKV)), blk]
        out_shape += [SDS((1, D_RWKV), f32), SDS((D_RWKV, LANES), f32), SDS((LANES, D_RWKV), f32),
                      SDS((T, D_RWKV), f32)]
    in_specs += [blk] * n_cot
    args += cot_list
    return pl.pallas_call(
        body, name=f"rwkv_pre_bwd_l{layer}", grid=(nb,), in_specs=in_specs, out_specs=out_specs, out_shape=out_shape,
        scratch_shapes=[pltpu.VMEM((1, RWKV_COLS), f32)],
        compiler_params=_params(("arbitrary",), 60),
    )(*args)


def _pack_cols(prod_ref, red):
    res = jnp.dot(prod_ref[...].astype(bf16), red, preferred_element_type=f32)
    out = res[0:RWKV_HEAD]
    for s in range(1, SCAN_TB):
        out = out + pltpu.roll(res[s * RWKV_HEAD:(s + 1) * RWKV_HEAD], RWKV_HEADS * s, 1)
    return out


def _unpack_cols(tile_ref, exp):
    rows = []
    for s in range(SCAN_TB):
        half = tile_ref[0, :, pl.ds((s // 4) * LANES, LANES)]
        rows.append(half if s % 4 == 0 else pltpu.roll(half, LANES - 2 * RWKV_HEADS * (s % 4), 1))
    return jnp.dot(jnp.concatenate(rows, axis=0), exp, preferred_element_type=f32)


def _to_col(v):
    T = v.shape[0]
    hi = lax.reduce_precision(v, 8, 7)
    parts = jnp.stack([hi, v - hi], axis=1).astype(bf16)
    parts = parts.reshape(T // SCAN_TB, SCAN_TB, 2, RWKV_HEADS, RWKV_HEAD)
    return jnp.transpose(parts, (0, 4, 1, 2, 3)).reshape(T // SCAN_TB, RWKV_HEAD, 2 * LANES)


def _from_col(vc):
    nb = vc.shape[0]
    tiles = vc.reshape(nb, RWKV_HEAD, SCAN_TB, RWKV_HEADS)
    return jnp.transpose(tiles, (0, 2, 3, 1)).reshape(nb * SCAN_TB, D_RWKV)


def _rwkv_scan_fwd(r, w, k, vcol, a, b, ee, red, exp, layer, rider, gather=()):
    T = r.shape[0]
    nb = T // SCAN_TB
    ng = len(gather)

    def body(*refs):
        r_ref, w_ref, k_ref, v_ref, a_ref, an_ref, b_ref, ee_ref, red_ref, exp_ref = refs[:10]
        rd_ins, refs = refs[10:10 + rider.n_in], refs[10 + rider.n_in:]
        g_ins, refs = refs[:ng], refs[ng:]
        o_ref, sp_ref = refs[:2]
        rd_outs, refs = refs[2:2 + rider.n_out], refs[2 + rider.n_out:]
        g_outs, refs = refs[:ng], refs[ng:]
        s_ref, sa_ref, vb_ref, po_ref = refs[:4]
        rd_scr, g_sems = refs[4:4 + rider.n_scr], refs[4 + rider.n_scr:]
        step = pl.program_id(0)
        rider.step(step, rd_ins, rd_outs, rd_scr)

        @pl.when(step == 0)
        def _():
            s_ref[...] = jnp.zeros_like(s_ref)
            sa_ref[...] = jnp.zeros_like(sa_ref)
            if ng:
                _gather_start(g_ins, g_outs, *g_sems)

        if ng:
            @pl.when(step == (3 * nb) // 4)
            def _():
                _gather_forward(g_ins, g_outs, *g_sems)

        ee_v = ee_ref[...]
        r_b, w_b, k_b, a_b, b_b = r_ref[...], w_ref[...], k_ref[...], a_ref[...], b_ref[...]
        rowi = lax.broadcasted_iota(jnp.int32, a_b.shape, 0)
        a_nx = jnp.where(rowi == SCAN_TB - 1, an_ref[0:1, :], pltpu.roll(a_b, SCAN_TB - 1, 0))
        wa = w_b * a_nx
        dots = _seg_raw(jnp.concatenate([b_b * a_nx, k_b * a_nx], axis=0), ee_v, 3)
        beta, kappa = dots[0:SCAN_TB], dots[SCAN_TB:2 * SCAN_TB]
        vb_ref[...] = _unpack_cols(v_ref, exp_ref[...])

        for t in range(SCAN_TB):
            blk = pl.ds(t * RWKV_HEAD, RWKV_HEAD)
            for hf in range(SCAN_SPLIT):
                ln = pl.ds(hf * (D_RWKV // SCAN_SPLIT), D_RWKV // SCAN_SPLIT)
                row = lambda x: x[t:t + 1, hf * (D_RWKV // SCAN_SPLIT):(hf + 1) * (D_RWKV // SCAN_SPLIT)]
                sp = s_ref[:, ln]
                sa = sa_ref[:, ln]
                sp_ref[t, 0, :, ln] = sp
                sp_ref[t, 1, :, ln] = sa
                vb = vb_ref[blk, ln]
                ahead = _seg_raw(sp * row(wa), ee_v, SCAN_PARTS, per_tile=True)
                sn = sp * row(w_b) + sa * row(b_b) + vb * row(k_b)
                s_ref[:, ln] = sn
                sa_ref[:, ln] = ahead + sa * row(beta) + vb * row(kappa)
                po_ref[blk, ln] = sn * row(r_b)
        o_ref[0] = _pack_cols(po_ref, red_ref[...])

        if ng:
            @pl.when(step == nb - 1)
            def _():
                _gather_finish(g_ins, g_outs, *g_sems)

    rows = pl.BlockSpec((SCAN_TB, D_RWKV), lambda i: (i, 0))
    rows_next = pl.BlockSpec((SCAN_TB, D_RWKV), lambda i: (jnp.minimum(i + 1, nb - 1), 0))
    col = pl.BlockSpec((1, RWKV_HEAD, LANES), lambda i: (i, 0, 0))
    col2 = pl.BlockSpec((1, RWKV_HEAD, 2 * LANES), lambda i: (i, 0, 0))
    full = lambda s: pl.BlockSpec(s, lambda i: (0,) * len(s))
    outs = pl.pallas_call(
        body, name=f"rwkv_scan_fwd_l{layer}", grid=(nb,),
        in_specs=[rows, rows, rows, col2, rows, rows_next, rows, full((256, 256)), full((D_RWKV, LANES)),
                  full((LANES, D_RWKV))] + rider.in_specs + [HBM_SPEC] * ng,
        out_specs=[col, pl.BlockSpec((SCAN_TB, 2, RWKV_HEAD, D_RWKV), lambda i: (i, 0, 0, 0))] + rider.out_specs
        + [HBM_SPEC] * ng,
        out_shape=[SDS((nb, RWKV_HEAD, LANES), f32), SDS((T, 2, RWKV_HEAD, D_RWKV), f32)] + rider.out_shape
        + _gather_out_shapes(gather),
        scratch_shapes=[pltpu.VMEM((RWKV_HEAD, D_RWKV), f32), pltpu.VMEM((RWKV_HEAD, D_RWKV), f32),
                        pltpu.VMEM((SCAN_TB * RWKV_HEAD, D_RWKV), f32),
                        pltpu.VMEM((SCAN_TB * RWKV_HEAD, D_RWKV), f32)] + rider.scratch
        + (_gather_sems(ng) if ng else []),
        compiler_params=_params(("arbitrary",), 52),
    )(r, w, k, vcol, a, a, b, ee, red, exp, *rider.args, *gather)
    return outs[0], outs[1], list(outs[2:2 + rider.n_out]), list(outs[2 + rider.n_out:])


def _rwkv_scan_bwd(r, w, k, vcol, a, b, sprev, docol, ee, red, exp, layer, send=()):
    T = r.shape[0]
    nb = T // SCAN_TB
    ns = len(send)

    def body(*refs):
        r_ref, w_ref, k_ref, v_ref, a_ref, b_ref, sp_ref, do_ref, ee_ref, red_ref, exp_ref = refs[:11]
        c_ins, refs = refs[11:11 + ns], refs[11 + ns:]
        dr_o, dw_o, dk_o, da_o, db_o, dv_o = refs[:6]
        c_outs, refs = refs[6:6 + ns], refs[6 + ns:]
        ds_ref, snext_ref, vb_ref, dob_ref, pdv_ref = refs[:5]
        c_sems = refs[5:]
        if ns:
            @pl.when(pl.program_id(0) == 0)
            def _():
                for cp in _chips_copies(c_ins, c_outs, *c_sems):
                    cp.start()

        ee_v = ee_ref[...]
        csum = lambda x: jnp.sum(x, axis=0, keepdims=True)
        nrow = SCAN_TB * RWKV_HEAD
        last = pl.ds(nrow - RWKV_HEAD, RWKV_HEAD)
        r_b, w_b, k_b, a_b, b_b = r_ref[...], w_ref[...], k_ref[...], a_ref[...], b_ref[...]
        vb_ref[...] = _unpack_cols(v_ref, exp_ref[...])
        dob_ref[...] = _unpack_cols(do_ref, exp_ref[...])

        @pl.when(pl.program_id(0) == 0)
        def _():
            ds_ref[...] = jnp.zeros_like(ds_ref)
            snext_ref[...] = (sp_ref[SCAN_TB - 1, 0] * w_b[SCAN_TB - 1:SCAN_TB, :]
                              + sp_ref[SCAN_TB - 1, 1] * b_b[SCAN_TB - 1:SCAN_TB, :]
                              + vb_ref[last, :] * k_b[SCAN_TB - 1:SCAN_TB, :])

        for s in range(SCAN_TB):
            t = SCAN_TB - 1 - s
            blk = pl.ds(t * RWKV_HEAD, RWKV_HEAD)
            for hf in range(SCAN_SPLIT):
                ln = pl.ds(hf * (D_RWKV // SCAN_SPLIT), D_RWKV // SCAN_SPLIT)
                row = lambda x: x[t:t + 1, hf * (D_RWKV // SCAN_SPLIT):(hf + 1) * (D_RWKV // SCAN_SPLIT)]
                sp, sa = sp_ref[t, 0, :, ln], sp_ref[t, 1, :, ln]
                sn = snext_ref[:, ln] if s == 0 else sp_ref[t + 1, 0, :, ln]
                vb, dob = vb_ref[blk, ln], dob_ref[blk, ln]
                ds = ds_ref[:, ln] + dob * row(r_b)
                dsa = _seg_raw(ds * row(b_b), ee_v, SCAN_PARTS, per_tile=True)
                ds_ref[:, ln] = ds * row(w_b) + dsa * row(a_b)
                dr_o[t:t + 1, ln] = csum(sn * dob)
                pdv_ref[blk, ln] = ds * row(k_b)
                dk_o[t:t + 1, ln] = csum(ds * vb)
                db_o[t:t + 1, ln] = csum(ds * sa)
                dw_o[t:t + 1, ln] = csum(ds * sp)
                da_o[t:t + 1, ln] = csum(sp * dsa)
        snext_ref[...] = sp_ref[0, 0]
        dv_o[0] = _pack_cols(pdv_ref, red_ref[...])

        if ns:
            @pl.when(pl.program_id(0) == nb - 1)
            def _():
                for cp in _chips_copies(c_ins, c_outs, *c_sems):
                    cp.wait()

    rows = pl.BlockSpec((SCAN_TB, D_RWKV), lambda i: (nb - 1 - i, 0))
    col = pl.BlockSpec((1, RWKV_HEAD, LANES), lambda i: (nb - 1 - i, 0, 0))
    col2 = pl.BlockSpec((1, RWKV_HEAD, 2 * LANES), lambda i: (nb - 1 - i, 0, 0))
    st = pl.BlockSpec((SCAN_TB, 2, RWKV_HEAD, D_RWKV), lambda i: (nb - 1 - i, 0, 0, 0))
    full = lambda s: pl.BlockSpec(s, lambda i: (0,) * len(s))
    big = pltpu.VMEM((SCAN_TB * RWKV_HEAD, D_RWKV), f32)
    outs = pl.pallas_call(
        body, name=f"rwkv_scan_bwd_l{layer}", grid=(nb,),
        in_specs=[rows, rows, rows, col2, rows, rows, st, col2, full((256, 256)), full((D_RWKV, LANES)),
                  full((LANES, D_RWKV))] + [HBM_SPEC] * ns,
        out_specs=[rows] * 5 + [col] + [HBM_SPEC] * ns,
        out_shape=[SDS((T, D_RWKV), f32)] * 5 + [SDS((nb, RWKV_HEAD, LANES), f32)] + _chips_out_shapes(send),
        scratch_shapes=[pltpu.VMEM((RWKV_HEAD, D_RWKV), f32)] * 2 + [big] * 3 + (_chips_sems(ns) if ns else []),
        compiler_params=_params(("arbitrary",), 56),
    )(r, w, k, vcol, a, b, sprev, docol, ee, red, exp, *send)
    return (*outs[:6], list(outs[6:]))


def _post_core(o, r, k2, v, z, gw, gb, rk, ee):
    inv = 1.0 / RWKV_HEAD
    mu = _seg(o, ee) * inv
    d = o - mu
    var = _seg(d * d, ee) * inv
    on = d * lax.rsqrt(var + GN_EPS) * gw + gb
    bonus = _seg(r * k2 * rk, ee) * v
    return (on + bonus) * jax.nn.silu(z)


def _rwkv_post_fwd(o, r, k2, v, z, gw, gb, rk, ee, cat, layer):
    T = o.shape[0]

    def body(o_ref, r_ref, k_ref, v_ref, z_ref, gw_ref, gb_ref, rk_ref, ee_ref, cat_in, out_ref):
        del cat_in
        out_ref[...] = _post_core(o_ref[...], r_ref[...], k_ref[...], v_ref[...], z_ref[...], gw_ref[...],
                                  gb_ref[...], rk_ref[...], ee_ref[...])

    blk = pl.BlockSpec((TB_WIDE, D_RWKV), lambda i: (i, 0))
    row = pl.BlockSpec((1, D_RWKV), lambda i: (0, 0))
    return pl.pallas_call(
        body, name=f"rwkv_post_fwd_l{layer}", grid=(T // TB_WIDE,),
        in_specs=[blk] * 5 + [row] * 3 + [pl.BlockSpec((256, 256), lambda i: (0, 0)), HBM_SPEC],
        out_specs=blk, out_shape=SDS((T, D_MODEL), f32), input_output_aliases={9: 0},
        compiler_params=_params(("parallel",), 40),
    )(o, r, k2, v, z, gw, gb, rk, ee, cat)


def _rwkv_post_bwd(o, r, k2, v, z, gw, gb, rk, ee, dcat, layer):
    T = o.shape[0]

    def body(o_ref, r_ref, k_ref, v_ref, z_ref, gw_ref, gb_ref, rk_ref, ee_ref, g_ref,
             do_o, dr_o, dk_o, dv_o, dz_o, dgw_o, dgb_o, drk_o):
        ee_v = ee_ref[...]
        fn = lambda o_, r_, k_, v_, z_, gw_, gb_, rk_: _post_core(o_, r_, k_, v_, z_, gw_, gb_, rk_, ee_v)
        _, vjp = jax.vjp(fn, o_ref[...], r_ref[...], k_ref[...], v_ref[...], z_ref[...], gw_ref[...], gb_ref[...],
                         rk_ref[...])
        gs = vjp(g_ref[...])
        do_o[...], dr_o[...], dk_o[...], dv_o[...], dz_o[...] = gs[:5]

        @pl.when(pl.program_id(0) == 0)
        def _():
            dgw_o[...] = jnp.zeros_like(dgw_o)
            dgb_o[...] = jnp.zeros_like(dgb_o)
            drk_o[...] = jnp.zeros_like(drk_o)

        dgw_o[...] += gs[5]
        dgb_o[...] += gs[6]
        drk_o[...] += gs[7]

    blk = pl.BlockSpec((TB_WIDE, D_RWKV), lambda i: (i, 0))
    row = pl.BlockSpec((1, D_RWKV), lambda i: (0, 0))
    return pl.pallas_call(
        body, name=f"rwkv_post_bwd_l{layer}", grid=(T // TB_WIDE,),
        in_specs=[blk] * 5 + [row] * 3 + [pl.BlockSpec((256, 256), lambda i: (0, 0)), blk],
        out_specs=[blk] * 5 + [row] * 3,
        out_shape=[SDS((T, D_RWKV), f32)] * 5 + [SDS((1, D_RWKV), f32)] * 3,
        compiler_params=_params(("arbitrary",), 48),
    )(o, r, k2, v, z, gw, gb, rk, ee, dcat)


def _hgrn_chunk(qr, fr, ii, z, S, lbl, gw, layer):
    L = CHUNK
    G = qr.shape[1] // HG_HEAD
    hd = lambda x, g: x[..., g * HG_HEAD:(g + 1) * HG_HEAD]
    heads = lambda f: jnp.concatenate([f(g) for g in range(G)], axis=-1)
    m = jnp.max(lbl, axis=0, keepdims=True)
    e = jnp.exp(lbl - m)
    sm = e / jnp.sum(e, axis=0, keepdims=True)
    if layer == 0:
        lb = sm[0:1] - sm[0:1]
    else:
        lb = (sm[0:1] + sm[1:2]) - sm[0:1]
    q = jax.nn.silu(qr)
    log_lb = jnp.log(jnp.maximum(lb, LB_FLOOR))
    lf = jnp.logaddexp(log_lb, jnp.log1p(-lb) + jax.nn.log_sigmoid(fr))
    k = (1.0 - lb) * jax.nn.sigmoid(-fr)
    ti = lax.broadcasted_iota(jnp.int32, (L, L), 0)
    si = lax.broadcasted_iota(jnp.int32, (L, L), 1)
    ltri = (si <= ti).astype(f32)
    b = jnp.dot(ltri, lf, precision=HIGHEST, preferred_element_type=f32)
    btot = jnp.dot(jnp.ones((L, L), f32), lf, precision=HIGHEST, preferred_element_type=f32)
    qd = q * jnp.exp(b)
    o = heads(lambda g: jnp.dot(hd(qd, g), S[g], preferred_element_type=f32))
    nsub = L // HSUB
    t3 = lax.broadcasted_iota(jnp.int32, (HSUB, HSUB, G * HG_HEAD), 0)
    s3 = lax.broadcasted_iota(jnp.int32, (HSUB, HSUB, G * HG_HEAD), 1)
    causal = s3 <= t3
    nt = (((1,), (1,)), ((), ()))
    tn = (((0,), (0,)), ((), ()))
    rows = []
    for I in range(nsub):
        sl = slice(I * HSUB, (I + 1) * HSUB)
        bI, qI, kI, iI = b[sl], q[sl], k[sl], ii[sl]
        d3 = bI[:, None, :] - bI[None, :, :]
        dec = jnp.where(causal, jnp.exp(jnp.where(causal, d3, 0.0)), 0.0)
        prod = qI[:, None, :] * dec * kI[None, :, :]
        oI = heads(lambda g: jnp.dot(jnp.sum(hd(prod, g), axis=-1), hd(iI, g), preferred_element_type=f32))
        for J in range(I):
            sj = slice(J * HSUB, (J + 1) * HSUB)
            bm = b[(J + 1) * HSUB - 1:(J + 1) * HSUB]
            qs = qI * jnp.exp(bI - bm)
            ks = k[sj] * jnp.exp(bm - b[sj])
            iJ = ii[sj]
            oI = oI + heads(lambda g: jnp.dot(
                lax.dot_general(hd(qs, g), hd(ks, g), nt, preferred_element_type=f32), hd(iJ, g),
                preferred_element_type=f32))
        rows.append(oI)
    o = o + jnp.concatenate(rows, axis=0)
    k_dec = k * jnp.exp(btot - b)
    ones = jnp.ones((L, HG_HEAD), f32)
    s_new = jnp.concatenate([
        (S[g] * jnp.exp(lax.dot_general(hd(lf, g), ones, tn, precision=HIGHEST, preferred_element_type=f32))
         + lax.dot_general(hd(k_dec, g), hd(ii, g), tn, preferred_element_type=f32))[None] for g in range(G)], axis=0)
    ms = heads(lambda g: jnp.broadcast_to(jnp.mean(hd(o * o, g), axis=-1, keepdims=True), (L, HG_HEAD)))
    out = o * lax.rsqrt(ms + RMS_EPS) * gw * jax.nn.silu(z)
    return out, s_new


class _Part:
    def __init__(self, args, in_specs, out_specs, out_shape, scratch, step):
        self.args, self.in_specs, self.out_specs, self.out_shape = args, in_specs, out_specs, out_shape
        self.scratch, self.step, self.total = scratch, step, None
        self.n_in, self.n_out, self.n_scr = len(args), len(out_shape), len(scratch)


def _run_part(part, steps, name):
    part.total = steps

    def body(*refs):
        ins, outs = refs[:part.n_in], refs[part.n_in:part.n_in + part.n_out]
        part.step(pl.program_id(0), ins, outs, refs[part.n_in + part.n_out:])

    return pl.pallas_call(body, name=name, grid=(steps,), in_specs=part.in_specs, out_specs=part.out_specs,
                          out_shape=part.out_shape, scratch_shapes=part.scratch,
                          compiler_params=_params(("arbitrary",), 48))(*part.args)


def _hg_index(T, rev, group=1):
    nc = T // CHUNK
    width = group * HG_HEAD
    head = lambda i: i // nc
    chunk = (lambda i: nc - 1 - i % nc) if rev else (lambda i: i % nc)
    first = lambda i: i % nc == 0
    pj = lambda off, g: pl.BlockSpec((CHUNK, HG_HEAD),
                                     lambda i: (chunk(i), RW_BLKS + off + group * head(i) + g))
    pspecs = [pj(kind * HG_HEADS, g) for kind in range(4) for g in range(group)]
    per_head = lambda rows: pl.BlockSpec((rows, width), lambda i: (0, head(i)))
    return nc, head, chunk, first, pspecs, per_head


def _hgrn_fwd_part(proj, lbl, gw, layer):
    T = proj.shape[0]
    nc, head, chunk, first, pspecs, per_head = _hg_index(T, False)

    def step(i, ins, outs, scr):
        q_ref, f_ref, i_ref, z_ref, lbl_ref, gw_ref = ins
        out_ref, sh_ref = outs
        s_ref, = scr

        @pl.when(first(i))
        def _():
            s_ref[...] = jnp.zeros_like(s_ref)

        s0 = s_ref[...]
        sh_ref[:, 0] = s0
        out, s_new = _hgrn_chunk(q_ref[...], f_ref[...], i_ref[...], z_ref[...], s0, lbl_ref[...], gw_ref[...], layer)
        out_ref[...] = out
        s_ref[...] = s_new

    return _Part(
        [proj, proj, proj, proj, lbl, gw], pspecs + [per_head(2), per_head(1)],
        [pl.BlockSpec((CHUNK, HG_HEAD), lambda i: (chunk(i), HG_HEADS + head(i))),
         pl.BlockSpec((1, 1, HG_HEAD, HG_HEAD), lambda i: (head(i), chunk(i), 0, 0))],
        [SDS((T, D_MODEL), f32), SDS((HG_HEADS, nc, HG_HEAD, HG_HEAD), f32)],
        [pltpu.VMEM((1, HG_HEAD, HG_HEAD), f32)], step)


HG_GROUP = 2


def _hgrn_bwd_part(proj, shist, dcat, lbl, gw, layer):
    T = proj.shape[0]
    G = HG_GROUP
    nc, head, chunk, first, pspecs, per_head = _hg_index(T, True, G)

    def step(i, ins, outs, scr):
        p_refs, (lbl_ref, gw_ref, sh_ref, g_ref) = ins[:4 * G], ins[4 * G:]
        dq_o, df_o, di_o, dz_o, dlbl_o, dgw_o = outs
        ds_ref, = scr

        @pl.when(first(i))
        def _():
            ds_ref[...] = jnp.zeros_like(ds_ref)
            dlbl_o[...] = jnp.zeros_like(dlbl_o)
            dgw_o[...] = jnp.zeros_like(dgw_o)

        fn = functools.partial(_hgrn_chunk, layer=layer)
        q, f, ii, z = (jnp.concatenate([p_refs[kind * G + g][...] for g in range(G)], axis=1) for kind in range(4))
        _, vjp = jax.vjp(fn, q, f, ii, z, sh_ref[:, 0], lbl_ref[...], gw_ref[...])
        dq, df, di, dz, ds, dl, dg = vjp((g_ref[...], ds_ref[...]))
        dq_o[...], df_o[...], di_o[...], dz_o[...] = dq, df, di, dz
        ds_ref[...] = ds
        dlbl_o[...] += dl
        dgw_o[...] += dg

    ospec = pl.BlockSpec((CHUNK, G * HG_HEAD), lambda i: (chunk(i), head(i)))
    return _Part(
        [proj] * (4 * G) + [lbl, gw, shist, dcat],
        pspecs + [per_head(2), per_head(1),
                  pl.BlockSpec((G, 1, HG_HEAD, HG_HEAD), lambda i: (head(i), chunk(i), 0, 0)),
                  pl.BlockSpec((CHUNK, G * HG_HEAD), lambda i: (chunk(i), HG_HEADS // G + head(i)))],
        [ospec] * 4 + [per_head(2), per_head(1)],
        [SDS((T, D_HGRN), f32)] * 4 + [SDS((2, D_HGRN), f32), SDS((1, D_HGRN), f32)],
        [pltpu.VMEM((G, HG_HEAD, HG_HEAD), f32)], step)


def _ln_core(h, y, w, b):
    u = ALPHA * h + y
    mu = jnp.mean(u, axis=-1, keepdims=True)
    d = u - mu
    var = jnp.mean(d * d, axis=-1, keepdims=True)
    return d * lax.rsqrt(var + LN_EPS) * w + b


def _ln_fwd(h, y, w, b, layer):
    T = h.shape[0]

    def body(h_ref, y_ref, w_ref, b_ref, o_ref):
        o_ref[...] = _ln_core(h_ref[...], y_ref[...], w_ref[...], b_ref[...])

    blk = pl.BlockSpec((TB_WIDE, D_MODEL), lambda i: (i, 0))
    row = pl.BlockSpec((1, D_MODEL), lambda i: (0, 0))
    return pl.pallas_call(body, name=f"ln_fwd_l{layer}", grid=(T // TB_WIDE,), in_specs=[blk, blk, row, row],
                          out_specs=blk, out_shape=SDS((T, D_MODEL), f32),
                          compiler_params=_params(("parallel",), 40))(h, y, w, b)


def _ln_bwd(h, y, w, b, g, layer):
    T = h.shape[0]

    def body(h_ref, y_ref, w_ref, b_ref, g_ref, dh_o, dy_o, dw_o, db_o):
        _, vjp = jax.vjp(_ln_core, h_ref[...], y_ref[...], w_ref[...], b_ref[...])
        dh, dy, dw, db = vjp(g_ref[...])
        dh_o[...] = dh
        dy_o[...] = dy.astype(bf16)

        @pl.when(pl.program_id(0) == 0)
        def _():
            dw_o[...] = jnp.zeros_like(dw_o)
            db_o[...] = jnp.zeros_like(db_o)

        dw_o[...] += dw
        db_o[...] += db

    blk = pl.BlockSpec((TB_WIDE, D_MODEL), lambda i: (i, 0))
    row = pl.BlockSpec((1, D_MODEL), lambda i: (0, 0))
    return pl.pallas_call(body, name=f"ln_bwd_l{layer}", grid=(T // TB_WIDE,), in_specs=[blk, blk, row, row, blk],
                          out_specs=[blk, blk, row, row],
                          out_shape=[SDS((T, D_MODEL), f32), SDS((T, D_MODEL), bf16)] + [SDS((1, D_MODEL), f32)] * 2,
                          compiler_params=_params(("arbitrary",), 48))(h, y, w, b, g)


def _loss_head(hout, target):
    T = hout.shape[0]

    def body(h_ref, t_ref, loss_o, g_o):
        err = h_ref[...] - t_ref[...]
        g_o[...] = err * (1.0 / D_MODEL)

        @pl.when(pl.program_id(0) == 0)
        def _():
            loss_o[...] = jnp.zeros_like(loss_o)

        part = jnp.sum(jnp.sum(err * err, axis=-1, keepdims=True) * (1.0 / D_MODEL), axis=0, keepdims=True)
        loss_o[...] += 0.5 * part

    blk = pl.BlockSpec((TB_WIDE, D_MODEL), lambda i: (i, 0))
    return pl.pallas_call(body, name="loss_head", grid=(T // TB_WIDE,), in_specs=[blk, blk],
                          out_specs=[pl.BlockSpec((1, 1), lambda i: (0, 0)), blk],
                          out_shape=[SDS((1, 1), f32), SDS((T, D_MODEL), f32)],
                          compiler_params=_params(("arbitrary",), 40))(hout, target)


def _pad_rows(m, lo, total):
    return jnp.pad(m, ((0, 0), (lo, total - lo - m.shape[1]), (0, 0)))


def _device_step(x, target, w_in0, later_shards, assemble_later, on_grads, on_swapped, on_recv, on_small_grads,
                 shift_mu, w_decay0, w_decay_up, a0, a_up, k_k, k_a, r_k, ln_x_w, ln_x_b,
                 v_mix0, v_mix_down, v_mix_up, lb_logits, g_norm_w, ln_w, ln_b):
    T = x.shape[0]
    ee, red, exp = _const_mats()
    wup_pad = _pad_rows(w_decay_up, 0, LANES)
    aup_pad = _pad_rows(a_up, LORA, LANES)
    vdn_pad = jnp.pad(v_mix_down, ((0, 0), (0, 0), (0, LANES - VRES)))
    vup_pad = _pad_rows(v_mix_up, 0, LANES)
    row = lambda p, l: p[l][None, :]

    def vmix_of(l, vfirst):
        if l == 0:
            return None
        return (row(v_mix0, l - 1), vdn_pad[l - 1], vup_pad[l - 1], vfirst)

    h = x
    saved = []
    vfirst = None
    w_in = [w_in0, None]
    w_out = None
    for l in range(DEPTH):
        hb = h.astype(bf16)
        proj = _matmul(hb, w_in[l], mode="nt", tm=1024, tn=1664, tk=D_MODEL, name=f"proj_fwd_l{l}")
        pre_args = (proj, row(shift_mu, l), row(w_decay0, l), wup_pad[l], row(a0, l), aup_pad[l], row(k_k, l),
                    row(k_a, l), ee, vmix_of(l, vfirst))
        r, w, k2, v, av, bv, z = _rwkv_pre_fwd(*pre_args, layer=l)
        if l == 0:
            vfirst = v
        vcol = _to_col(v)
        ocol, sprev, (cat, shist), gathered = _rwkv_scan_fwd(
            r, w, k2, vcol, av, bv, ee, red, exp, l, _hgrn_fwd_part(proj, lb_logits, row(g_norm_w, l), l),
            gather=later_shards if l == 0 else [])
        if l == 0:
            w_in[1], w_out = assemble_later(gathered)
        o = _from_col(ocol)
        post_args = (o, r, k2, v, z, row(ln_x_w, l), row(ln_x_b, l), row(r_k, l), ee)
        cat = _rwkv_post_fwd(*post_args, cat, layer=l)
        catb = cat.astype(bf16)
        y = _matmul(catb, w_out, b_layer=l, mode="nn", tm=1024, tn=1024, tk=D_MODEL, name=f"out_fwd_l{l}")
        h_new = _ln_fwd(h, y, row(ln_w, l), row(ln_b, l), l)
        saved.append(dict(h=h, hb=hb, proj=proj, pre_args=pre_args, post_args=post_args,
                          scan=(r, w, k2, vcol, av, bv, sprev), cat=catb, shist=shist, y=y))
        h = h_new

    loss_sum, g = _loss_head(h, target)

    grads = {n: [None] * DEPTH for n in ("w_in", "w_out", "shift_mu", "w_decay0", "w_decay_up", "a0", "a_up", "k_k",
                                         "k_a", "r_k", "ln_x_w", "ln_x_b", "g_norm_w", "ln_w", "ln_b")}
    dlbl = []
    dvfirst = None
    to_send = []
    for l in reversed(range(DEPTH)):
        s = saved[l]
        dh_res, dy, dlnw, dlnb = _ln_bwd(s["h"], s["y"], row(ln_w, l), row(ln_b, l), g, l)
        grads["ln_w"][l], grads["ln_b"][l] = dlnw[0], dlnb[0]
        dcat = _matmul(dy, w_out, b_layer=l, mode="nt", tm=1024, tn=1024, tk=D_MODEL, name=f"out_bwd_dx_l{l}")
        grads["w_out"][l] = _matmul(s["cat"], dy, mode="tn", tm=1024, tn=1024, tk=T, name=f"out_bwd_dw_l{l}")
        do, dr_p, dk_p, dv_p, dz, dgw, dgb, drk = _rwkv_post_bwd(*s["post_args"], dcat, layer=l)
        grads["ln_x_w"][l], grads["ln_x_b"][l], grads["r_k"][l] = dgw[0], dgb[0], drk[0]
        r, w, k2, vcol, av, bv, sprev = s["scan"]
        dr_s, dw_s, dk_s, da_s, db_s, dvcol, arrived = _rwkv_scan_bwd(
            r, w, k2, vcol, av, bv, sprev, _to_col(do), ee, red, exp, l, send=to_send if l == 0 else [])
        dq, df, di, dzh, dl, dgn = _run_part(
            _hgrn_bwd_part(s["proj"], s["shist"], dcat, lb_logits, row(g_norm_w, l), l),
            (T // CHUNK) * HG_HEADS // HG_GROUP, f"hgrn_bwd_l{l}")
        if l == 0:
            on_recv(1, arrived)
        v_cots = [dv_p, _from_col(dvcol)] + ([dvfirst] if (l == 0 and dvfirst is not None) else [])
        cots = dict(r=[dr_p, dr_s], w=dw_s, k=[dk_p, dk_s], v=v_cots, a=da_s, b=db_s, z=dz)
        outs = _rwkv_pre_bwd(*s["pre_args"], cots, layer=l)
        dproj, dmu, dw0, dwup, da0, daup, dkkw, dkaw = outs[:8]
        grads["shift_mu"][l], grads["w_decay0"][l], grads["a0"][l] = dmu[0], dw0[0], da0[0]
        grads["k_k"][l], grads["k_a"][l] = dkkw[0], dkaw[0]
        grads["w_decay_up"][l], grads["a_up"][l] = dwup[:LORA], daup[LORA:]
        if l > 0:
            dv0, dvdn, dvup, dvfirst = outs[8:]
            g_vmix = (dv0, dvdn[:, :VRES][None], dvup[:VRES][None])
        dlbl.append(dl)
        grads["g_norm_w"][l] = dgn[0]
        dproj = jnp.concatenate([dproj, dq, df, di, dzh], axis=1).astype(bf16)
        dw_args = dict(mode="tn", tm=1664, tn=1024, tk=T, name=f"proj_bwd_dw_l{l}")
        dx_args = dict(mode="nn", tm=1024, tn=1024, tk=1664, name=f"proj_bwd_dx_l{l}", add=dh_res)
        if l == 1:
            grads["w_in"][l] = _matmul(dproj, s["hb"], **dw_args)
            g, swapped = _matmul(dproj, w_in[l], rider=_swap_part(on_grads(l, grads["w_in"][l], grads["w_out"][l])),
                                 **dx_args)
            to_send = on_swapped(l, swapped)
        else:
            small = {n: jnp.stack(v) for n, v in grads.items() if n not in ("w_in", "w_out")}
            small["lb_logits"] = dlbl[0] + dlbl[1]
            small["v_mix0"], small["v_mix_down"], small["v_mix_up"] = g_vmix
            grads["w_in"][l], small_out = _matmul(dproj, s["hb"], rider=on_small_grads(small), **dw_args)
            swapped = _run_part(_swap_part(on_grads(l, grads["w_in"][l], grads["w_out"][l])), 1, "rs_swap_halves_l0")
            g, arrived = _matmul(dproj, w_in[l], rider=_chips_part(on_swapped(l, swapped)), **dx_args)
            on_recv(0, arrived)
    small["w_in"], small["w_out"] = grads["w_in"], grads["w_out"]
    return loss_sum, g, small, small_out


CHIP_COMBOS = ((1, 0), (0, 1), (1, 1))
HBM_SPEC = pl.BlockSpec(memory_space=pl.ANY)


def _mesh_pos():
    return lax.axis_index("x"), lax.axis_index("y"), lax.axis_index("c")


def _flip(v, d):
    return 1 - v if d else v


def _half(ref, dim, cc, hs):
    idx = [slice(None)] * len(ref.shape)
    idx[dim] = pl.ds(cc * hs, hs)
    return ref.at[tuple(idx)]


def _gather_shards(arrs):
    n = len(arrs)

    def body(*refs):
        ins, outs = refs[:n], refs[n:2 * n]
        sems = refs[2 * n:]
        _gather_start(ins, outs, *sems)
        _gather_forward(ins, outs, *sems)
        _gather_finish(ins, outs, *sems)

    return pl.pallas_call(
        body, name="gather_shards", in_specs=[HBM_SPEC] * n, out_specs=[HBM_SPEC] * n,
        out_shape=_gather_out_shapes(arrs), scratch_shapes=_gather_sems(n),
    )(*arrs)


def _gather_out_shapes(arrs):
    return [SDS((4,) + a.shape, a.dtype) for a in arrs]


def _gather_sems(n):
    return [pltpu.SemaphoreType.DMA((n, 6)), pltpu.SemaphoreType.DMA((n, 6)), pltpu.SemaphoreType.DMA((n,))]


def _gather_copies(kind, ins, outs, send_sems, recv_sems, local_sems):
    x, y, c = _mesh_pos()
    kc = 2 * x + y
    cps = []
    for a in range(len(ins)):
        if kind == "local":
            cps.append(pltpu.make_async_copy(ins[a], outs[a].at[kc], local_sems.at[a]))
            continue
        for j, (dx, dy) in enumerate(CHIP_COMBOS):
            px, py = _flip(x, dx), _flip(y, dy)
            kj = 2 * px + py
            src, dst, s, to = {
                "sends": (ins[a].at[c], outs[a].at[kc, c], j, (px, py, c)),
                "landed": (outs[a].at[kj, c], outs[a].at[kj, c], j, (px, py, c)),
                "forwards": (outs[a].at[kj, c], outs[a].at[kj, c], 3 + j, (x, y, 1 - c)),
                "passed": (outs[a].at[kj, 1 - c], outs[a].at[kj, 1 - c], 3 + j, (x, y, 1 - c)),
            }[kind]
            cps.append(pltpu.make_async_remote_copy(
                src_ref=src, dst_ref=dst, send_sem=send_sems.at[a, s], recv_sem=recv_sems.at[a, s], device_id=to,
                device_id_type=MESH))
    return cps


def _gather_start(*refs):
    for cp in _gather_copies("local", *refs) + _gather_copies("sends", *refs):
        cp.start()


def _gather_forward(*refs):
    for arrival, fw in zip(_gather_copies("landed", *refs), _gather_copies("forwards", *refs)):
        arrival.wait_recv()
        fw.start()


def _gather_finish(*refs):
    for cp in _gather_copies("passed", *refs):
        cp.wait_recv()
    for cp in _gather_copies("sends", *refs) + _gather_copies("forwards", *refs):
        cp.wait_send()
    for cp in _gather_copies("local", *refs):
        cp.wait()


DEV_COMBOS = tuple((dx, dy, dc) for dx in (0, 1) for dy in (0, 1) for dc in (0, 1))[1:]


def _allreduce_part(buf, rep_rows, chip_rows):
    R0, Q = rep_rows, chip_rows
    whole = lambda rows: pl.BlockSpec((rows, LANES), lambda *_: (0, 0))

    def copies(x_ref, slots, send_sems, recv_sems):
        x, y, c = _mesh_pos()
        cps = []
        for k, (dx, dy, dc) in enumerate(DEV_COMBOS):
            px, py = _flip(x, dx), _flip(y, dy)
            to = (px, py, _flip(c, dc))
            cps.append(pltpu.make_async_remote_copy(
                src_ref=x_ref.at[pl.ds(0, R0)], dst_ref=slots.at[k + 1, pl.ds(0, R0)], send_sem=send_sems.at[0, k],
                recv_sem=recv_sems.at[0, k], device_id=to, device_id_type=MESH))
            if Q:
                cps.append(pltpu.make_async_remote_copy(
                    src_ref=x_ref.at[pl.ds(pl.multiple_of(R0 + (2 * px + py) * Q, 8), Q)],
                    dst_ref=slots.at[k + 1, pl.ds(R0, Q)], send_sem=send_sems.at[1, k], recv_sem=recv_sems.at[1, k],
                    device_id=to, device_id_type=MESH))
        return cps

    def step(i, ins, outs, scr):
        x_ref, = ins
        o_ref, = outs
        slots = scr[0]

        @pl.when(i == 0)
        def _():
            for cp in copies(x_ref, *scr):
                cp.start()
            x, y, _ = _mesh_pos()
            slots[0, pl.ds(0, R0)] = x_ref[pl.ds(0, R0), :]
            if Q:
                slots[0, pl.ds(R0, Q)] = x_ref[pl.ds(pl.multiple_of(R0 + (2 * x + y) * Q, 8), Q), :]

        @pl.when(i == part.total - 1)
        def _():
            x, y, c = _mesh_pos()
            me = 4 * x + 2 * y + c
            cps = copies(x_ref, *scr)
            for cp in cps:
                cp.wait_recv()
            acc = slots[jnp.bitwise_xor(me, 0)]
            for d in range(1, 8):
                acc = acc + slots[jnp.bitwise_xor(me, d)]
            o_ref[...] = acc
            for cp in cps:
                cp.wait_send()

    part = _Part([buf], [whole(R0 + 4 * Q)], [whole(R0 + Q)], [SDS((R0 + Q, LANES), f32)],
                 [pltpu.VMEM((8, R0 + Q, LANES), f32), pltpu.SemaphoreType.DMA((2, 7)),
                  pltpu.SemaphoreType.DMA((2, 7))], step)
    return part


def _chips_part(slabs):
    n = len(slabs)

    def step(i, ins, outs, scr):
        if not n:
            return

        @pl.when(i == 0)
        def _():
            for cp in _chips_copies(ins, outs, *scr):
                cp.start()

        @pl.when(i == part.total - 1)
        def _():
            for cp in _chips_copies(ins, outs, *scr):
                cp.wait()

    part = _Part(list(slabs), [HBM_SPEC] * n, [HBM_SPEC] * n, _chips_out_shapes(slabs),
                 _chips_sems(n) if n else [], step)
    return part


def _swap_part(gs):
    n = len(gs)

    def copies(ins, outs, send_sems, recv_sems):
        x, y, c = _mesh_pos()
        return [pltpu.make_async_remote_copy(
            src_ref=_half(ins[a], 1, 1 - c, ins[a].shape[1] // 2), dst_ref=outs[a], send_sem=send_sems.at[a],
            recv_sem=recv_sems.at[a], device_id=(x, y, 1 - c), device_id_type=MESH) for a in range(n)]

    def step(i, ins, outs, scr):
        if not n:
            return

        @pl.when(i == 0)
        def _():
            for cp in copies(ins, outs, *scr):
                cp.start()

        @pl.when(i == part.total - 1)
        def _():
            for cp in copies(ins, outs, *scr):
                cp.wait()

    part = _Part(list(gs), [HBM_SPEC] * n, [HBM_SPEC] * n,
                 [SDS((g.shape[0], g.shape[1] // 2, g.shape[2]), g.dtype) for g in gs],
                 [pltpu.SemaphoreType.DMA((n,)), pltpu.SemaphoreType.DMA((n,))] if n else [], step)
    return part


def _chips_copies(ins, outs, send_sems, recv_sems):
    x, y, c = _mesh_pos()
    cps = []
    for a in range(len(ins)):
        for j, (dx, dy) in enumerate(CHIP_COMBOS):
            px, py = _flip(x, dx), _flip(y, dy)
            cps.append(pltpu.make_async_remote_copy(
                src_ref=ins[a].at[2 * px + py], dst_ref=outs[a].at[j], send_sem=send_sems.at[a, j],
                recv_sem=recv_sems.at[a, j], device_id=(px, py, c), device_id_type=MESH))
    return cps


def _chips_out_shapes(ps):
    return [SDS((3,) + p.shape[1:], p.dtype) for p in ps]


def _chips_sems(n):
    return [pltpu.SemaphoreType.DMA((n, 3)), pltpu.SemaphoreType.DMA((n, 3))]


def _share_halves(qs, tag):
    n = len(qs)

    def body(*refs):
        ins, outs = refs[:n], refs[n:2 * n]
        send_sems, recv_sems = refs[2 * n:]
        x, y, c = _mesh_pos()
        cps = []
        for a in range(n):
            cp = pltpu.make_async_remote_copy(
                src_ref=ins[a], dst_ref=outs[a], send_sem=send_sems.at[a], recv_sem=recv_sems.at[a],
                device_id=(x, y, 1 - c), device_id_type=MESH)
            cp.start()
            cps.append(cp)
        for cp in cps:
            cp.wait()

    return pl.pallas_call(
        body, name=f"rs_share_halves_{tag}", in_specs=[HBM_SPEC] * n, out_specs=[HBM_SPEC] * n,
        out_shape=[SDS(q.shape, q.dtype) for q in qs],
        scratch_shapes=[pltpu.SemaphoreType.DMA((n,)), pltpu.SemaphoreType.DMA((n,))],
    )(*qs)


def _row_tile(rows, cap):
    return max(t for t in range(16, cap + 1, 16) if rows % t == 0)


def _add_own_half(g, recv, c, name):
    S, R, C = g.shape
    h = R // 2
    tr = _row_tile(h, 256)
    nt = h // tr

    def body(c_ref, g_ref, r_ref, o_ref, o16_ref):
        del c_ref
        s = g_ref[...] + r_ref[...]
        o_ref[...] = s
        o16_ref[...] = s.astype(bf16)

    ospec = pl.BlockSpec((1, tr, C), lambda s, i, cr: (s, i, 0))
    return pl.pallas_call(
        body, name=name,
        grid_spec=pltpu.PrefetchScalarGridSpec(
            num_scalar_prefetch=1, grid=(S, nt),
            in_specs=[pl.BlockSpec((1, tr, C), lambda s, i, cr: (s, cr[0] * nt + i, 0)), ospec],
            out_specs=[ospec, ospec]),
        out_shape=[SDS((S, h, C), f32), SDS((S, h, C), bf16)],
        compiler_params=_params(("parallel", "parallel"), 40),
    )(c, g, recv)


def _add_chip_parts(p, recv, kc, name):
    _, R, C = p.shape
    tr = _row_tile(R, 256)

    def body(k_ref, p_ref, r0, r1, r2, o_ref):
        del k_ref
        o_ref[...] = ((p_ref[0] + r0[0].astype(f32)) + r1[0].astype(f32)) + r2[0].astype(f32)

    rspec = lambda j: pl.BlockSpec((1, tr, C), lambda i, kr: (j, i, 0))
    return pl.pallas_call(
        body, name=name,
        grid_spec=pltpu.PrefetchScalarGridSpec(
            num_scalar_prefetch=1, grid=(R // tr,),
            in_specs=[pl.BlockSpec((1, tr, C), lambda i, kr: (kr[0], i, 0)), rspec(0), rspec(1), rspec(2)],
            out_specs=pl.BlockSpec((tr, C), lambda i, kr: (i, 0))),
        out_shape=SDS((R, C), f32), compiler_params=_params(("parallel",), 40),
    )(kc, p, recv, recv, recv)


def _rs_pair(gs, got, c, tag):
    pairs = [_add_own_half(g, r, c, f"rs_add_pair_{tag}_{a}") for a, (g, r) in enumerate(zip(gs, got))]
    return [p[0] for p in pairs], [p[1] for p in pairs]


def _rs_finish(ps, arrived, c, kc, tag):
    qs = [_add_chip_parts(p, r, kc, f"rs_add_chips_{tag}_{a}") for a, (p, r) in enumerate(zip(ps, arrived))]
    return qs, _share_halves(qs, tag)


def _adamw_math(w, g, m, v):
    m2 = ADAM_B1 * m + (1.0 - ADAM_B1) * g
    v2 = ADAM_B2 * v + (1.0 - ADAM_B2) * (g * g)
    m_hat = m2 / (1.0 - ADAM_B1 ** ADAM_STEP)
    v_hat = v2 / (1.0 - ADAM_B2 ** ADAM_STEP)
    return -ADAM_LR * (m_hat / (jnp.sqrt(v_hat) + ADAM_EPS) + ADAM_WD * w), m2, v2


def _adamw_shard(w, g_own, g_other, m, v, c, name):
    L, R, C = w.shape
    h = R // 2
    tr = _row_tile(h, 208)
    nh = h // tr

    def body(c_ref, w_ref, go0, gx0, go1, gx1, m_ref, v_ref, g_o, d_o, m_o, v_o):
        mine = (pl.program_id(1) // nh) == c_ref[0]
        first = pl.program_id(0) == 0
        g = jnp.where(mine, jnp.where(first, go0[...], go1[...]), jnp.where(first, gx0[...], gx1[...]))[None]
        d, m2, v2 = _adamw_math(w_ref[...], g, m_ref[...], v_ref[...])
        g_o[...], d_o[...], m_o[...], v_o[...] = g, d, m2, v2

    full = pl.BlockSpec((1, tr, C), lambda l, i, cr: (l, i, 0))
    half0 = pl.BlockSpec((tr, C), lambda l, i, cr: (jnp.where(l == 0, i % nh, nh - 1), 0))
    half1 = pl.BlockSpec((tr, C), lambda l, i, cr: (jnp.where(l == 1, i % nh, 0), 0))
    return pl.pallas_call(
        body, name=name,
        grid_spec=pltpu.PrefetchScalarGridSpec(num_scalar_prefetch=1, grid=(L, R // tr),
                                               in_specs=[full, half0, half0, half1, half1, full, full],
                                               out_specs=[full] * 4),
        out_shape=[SDS((L, R, C), f32)] * 4, compiler_params=_params(("parallel", "parallel"), 48),
    )(c, w, g_own[0], g_other[0], g_own[1], g_other[1], m, v)


def _adamw_packed(wmv, g, name):
    _, R, C = wmv.shape

    def body(wmv_ref, g_ref, out_ref):
        out_ref[0], out_ref[1], out_ref[2] = _adamw_math(wmv_ref[0], g_ref[...], wmv_ref[1], wmv_ref[2])

    vm = pl.BlockSpec(memory_space=pltpu.VMEM)
    return pl.pallas_call(body, name=name, in_specs=[vm, vm], out_specs=vm, out_shape=SDS((3, R, C), f32),
                          compiler_params=_params(None, 40))(wmv, g)


def _pack(arrs):
    flat = [a.reshape(-1, LANES) for a in arrs]
    rows = sum(f.shape[0] for f in flat)
    pad = (-rows) % 16
    if pad:
        flat.append(jnp.zeros((pad, LANES), f32))
    return jnp.concatenate(flat, axis=0)


def _unpack(buf, shapes):
    out, r = [], 0
    for s in shapes:
        n = math.prod(s) // LANES
        out.append(buf[r:r + n].reshape(s))
        r += n
    return out


REPLICATED = ("shift_mu", "w_decay0", "a0", "k_k", "k_a", "r_k", "ln_x_w", "ln_x_b", "v_mix0", "lb_logits", "g_norm_w",
              "ln_w", "ln_b")
CHIP_SMALL = (("w_decay_up", 2), ("a_up", 2), ("v_mix_down", 1), ("v_mix_up", 2))
WEIGHTS = ("w_in", "shift_mu", "w_decay0", "w_decay_up", "a0", "a_up", "k_k", "k_a", "r_k", "ln_x_w", "ln_x_b",
           "v_mix0", "v_mix_down", "v_mix_up", "lb_logits", "g_norm_w", "w_out", "ln_w", "ln_b")


def kernel(x, w_in, shift_mu, w_decay0, w_decay_up, a0, a_up, k_k, k_a, r_k, ln_x_w, ln_x_b, v_mix0, v_mix_down, v_mix_up, lb_logits, g_norm_w, w_out, ln_w, ln_b, loss_target, m_w_in, m_shift_mu, m_w_decay0, m_w_decay_up, m_a0, m_a_up, m_k_k, m_k_a, m_r_k, m_ln_x_w, m_ln_x_b, m_v_mix0, m_v_mix_down, m_v_mix_up, m_lb_logits, m_g_norm_w, m_w_out, m_ln_w, m_ln_b, v_w_in, v_shift_mu, v_w_decay0, v_w_decay_up, v_a0, v_a_up, v_k_k, v_k_a, v_r_k, v_ln_x_w, v_ln_x_b, v_v_mix0, v_v_mix_down, v_v_mix_up, v_lb_logits, v_g_norm_w, v_w_out, v_ln_w, v_ln_b):
    W = dict(w_in=w_in, shift_mu=shift_mu, w_decay0=w_decay0, w_decay_up=w_decay_up, a0=a0, a_up=a_up, k_k=k_k, k_a=k_a,
             r_k=r_k, ln_x_w=ln_x_w, ln_x_b=ln_x_b, v_mix0=v_mix0, v_mix_down=v_mix_down, v_mix_up=v_mix_up,
             lb_logits=lb_logits, g_norm_w=g_norm_w, w_out=w_out, ln_w=ln_w, ln_b=ln_b)
    M = dict(w_in=m_w_in, shift_mu=m_shift_mu, w_decay0=m_w_decay0, w_decay_up=m_w_decay_up, a0=m_a0, a_up=m_a_up,
             k_k=m_k_k, k_a=m_k_a, r_k=m_r_k, ln_x_w=m_ln_x_w, ln_x_b=m_ln_x_b, v_mix0=m_v_mix0,
             v_mix_down=m_v_mix_down, v_mix_up=m_v_mix_up, lb_logits=m_lb_logits, g_norm_w=m_g_norm_w, w_out=m_w_out,
             ln_w=m_ln_w, ln_b=m_ln_b)
    V = dict(w_in=v_w_in, shift_mu=v_shift_mu, w_decay0=v_w_decay0, w_decay_up=v_w_decay_up, a0=v_a0, a_up=v_a_up,
             k_k=v_k_k, k_a=v_k_a, r_k=v_r_k, ln_x_w=v_ln_x_w, ln_x_b=v_ln_x_b, v_mix0=v_v_mix0,
             v_mix_down=v_v_mix_down, v_mix_up=v_v_mix_up, lb_logits=v_lb_logits, g_norm_w=v_g_norm_w, w_out=v_w_out,
             ln_w=v_ln_w, ln_b=v_ln_b)
    mx, my, mc = _mesh_pos()
    kc = (2 * mx + my).astype(jnp.int32)
    c_arr = jnp.reshape(mc.astype(jnp.int32), (1,))
    kc_arr = jnp.reshape(kc, (1,))

    small_shard = _pack([W[n] for n, _ in CHIP_SMALL])
    tr_in = lambda a: jnp.swapaxes(a, 1, 2)
    in_halves = tr_in(w_in).astype(bf16).reshape(DEPTH, 2, SHARD_COLS // 2, D_MODEL)
    out_halves = jnp.swapaxes(w_out.astype(bf16).reshape(DEPTH, 2, D_MODEL // 8, D_MODEL), 0, 1)
    g_in0, g_small = _gather_shards([in_halves[0], small_shard.reshape(2, -1, LANES)])
    in_full = lambda g: g.reshape(IN_COLS, D_MODEL)
    shard_shapes = [W[n].shape for n, _ in CHIP_SMALL]
    g_small = g_small.reshape(4, -1, LANES)
    per_chip = [_unpack(g_small[k], shard_shapes) for k in range(4)]
    full_small = {n: jnp.concatenate([per_chip[k][i] for k in range(4)], axis=ax)
                  for i, (n, ax) in enumerate(CHIP_SMALL)}

    def assemble_later(gathered):
        g_in1, g_out = gathered
        return in_full(g_in1), jnp.transpose(g_out, (2, 0, 1, 3, 4)).reshape(DEPTH, D_MODEL, D_MODEL)

    in_slabs = lambda g: g.reshape(4, SHARD_COLS, D_MODEL)
    out_slabs = lambda g: g.reshape(4, D_MODEL // 4, D_MODEL)
    slabs, pairs, arrivals = {}, {}, {}

    def on_grads(l, dw_in, dw_out):
        slabs[l] = [in_slabs(dw_in), out_slabs(dw_out)]
        return slabs[l]

    def on_swapped(l, got):
        pairs[l], to_send = _rs_pair(slabs[l], got, c_arr, f"l{l}")
        return to_send

    def on_recv(l, arrived):
        arrivals[l] = arrived

    small_names = list(REPLICATED) + [n for n, _ in CHIP_SMALL]

    rep_shapes = [W[n].shape for n in REPLICATED]
    rep_rows = -(-sum(math.prod(s) for s in rep_shapes) // (16 * LANES)) * 16
    chip_rows = small_shard.shape[0]

    def on_small_grads(g):
        blocks = [_pack([g[n] for n in REPLICATED])]
        for k in range(4):
            blocks.append(_pack([lax.slice_in_dim(g[n], k * W[n].shape[ax], (k + 1) * W[n].shape[ax], axis=ax)
                                 for n, ax in CHIP_SMALL]))
        return _allreduce_part(jnp.concatenate(blocks, axis=0), rep_rows, chip_rows)

    loss_sum, gx, G, (small_sum,) = _device_step(
        x[0], loss_target[0], in_full(g_in0), [in_halves[1], out_halves], assemble_later, on_grads, on_swapped, on_recv,
        on_small_grads, shift_mu, w_decay0, full_small["w_decay_up"], a0, full_small["a_up"], k_k, k_a, r_k, ln_x_w,
        ln_x_b, v_mix0, full_small["v_mix_down"], full_small["v_mix_up"], lb_logits, g_norm_w, ln_w, ln_b)
    loss = lax.psum(loss_sum[0, 0], ("x", "y", "c"))

    own1, other1 = _rs_finish(pairs[1], arrivals[1], c_arr, kc_arr, "l1")
    own0, other0 = _rs_finish(pairs[0], arrivals[0], c_arr, kc_arr, "l0")
    gin_own, gout_own = [own0[0], own1[0]], [own0[1], own1[1]]
    gin_other, gout_other = [other0[0], other1[0]], [other0[1], other1[1]]

    grads = dict(zip(REPLICATED, _unpack(small_sum[:rep_rows], rep_shapes)))
    grads.update(zip([n for n, _ in CHIP_SMALL], _unpack(small_sum[rep_rows:], shard_shapes)))

    delta, new_m, new_v = {}, {}, {}
    res = _adamw_shard(tr_in(w_in), gin_own, gin_other, tr_in(m_w_in), tr_in(v_w_in), c_arr, "adamw_w_in")
    grads["w_in"], delta["w_in"], new_m["w_in"], new_v["w_in"] = [tr_in(a) for a in res]
    grads["w_out"], delta["w_out"], new_m["w_out"], new_v["w_out"] = _adamw_shard(
        w_out, gout_own, gout_other, m_w_out, v_w_out, c_arr, "adamw_w_out")
    shapes = [W[n].shape for n in small_names]
    wmv = jnp.stack([_pack([src[n] for n in small_names]) for src in (W, M, V)])
    dmv = _adamw_packed(wmv, _pack([grads[n] for n in small_names]), "adamw_small")
    for n, d1, m1, v1 in zip(small_names, _unpack(dmv[0], shapes), _unpack(dmv[1], shapes), _unpack(dmv[2], shapes)):
        delta[n], new_m[n], new_v[n] = d1, m1, v1

    return (loss, gx[None], *[grads[n] for n in WEIGHTS], *[delta[n] for n in WEIGHTS],
            *[new_m[n] for n in WEIGHTS], *[new_v[n] for n in WEIGHTS])
```
